```python
import math
import jax, jax.numpy as jnp
from jax import lax
import numpy as np

D_MODEL = 1024
BATCH = 8
SEQ = 4096
DEPTH = 1

SSD_HEADS = 8
SSD_HEAD_DIM = 64
SSD_INNER = SSD_HEADS * SSD_HEAD_DIM
SSD_GROUPS = 2
SSD_STATE = 128
SSD_CONV = 4
SSD_CHUNK = 128
SSD_XBC = SSD_INNER + 2 * SSD_GROUPS * SSD_STATE
MLA_HEADS = 8
MLA_NOPE = 64
MLA_ROPE = 32
MLA_QK = MLA_NOPE + MLA_ROPE
MLA_V = 64
MLA_Q_RANK = 384
MLA_KV_RANK = 256
ROPE_THETA = 10000.0
ATTN_BLOCK = 128
MIX_WIDTH = SSD_INNER + MLA_HEADS * MLA_V
IN_WIDTH = SSD_INNER + SSD_XBC + SSD_HEADS + MLA_Q_RANK + MLA_KV_RANK + MLA_ROPE
IN_SPLITS = (SSD_INNER,
             SSD_INNER + SSD_XBC,
             SSD_INNER + SSD_XBC + SSD_HEADS,
             SSD_INNER + SSD_XBC + SSD_HEADS + MLA_Q_RANK,
             SSD_INNER + SSD_XBC + SSD_HEADS + MLA_Q_RANK + MLA_KV_RANK)
MEM_TOKENS = 256
MEM_HEADS = 4
MEM_HEAD_DIM = D_MODEL // MEM_HEADS
D_FF = 4 * D_MODEL
LN_EPS = 1e-5
RMS_EPS = 1e-6
DEEPNORM_ALPHA = (2.0 * DEPTH) ** 0.25
DEEPNORM_BETA = (8.0 * DEPTH) ** -0.25

kernel_name = "hybrid_ssd_mla_memxattn_deepnorm_layer"


def layer_norm(x, g, b):
    xf = x.astype(jnp.float32)
    mu = jnp.mean(xf, axis=-1, keepdims=True)
    var = jnp.mean(jnp.square(xf - mu), axis=-1, keepdims=True)
    return ((xf - mu) * lax.rsqrt(var + LN_EPS) * g.astype(jnp.float32) + b.astype(jnp.float32)).astype(x.dtype)


def rms_norm(x, g):
    xf = x.astype(jnp.float32)
    ms = jnp.mean(jnp.square(xf), axis=-1, keepdims=True)
    return (xf * lax.rsqrt(ms + RMS_EPS) * g.astype(jnp.float32)).astype(x.dtype)


def grouped_rms_norm(y, g, groups):
    b, s, c = y.shape
    yg = y.reshape(b, s, groups, c // groups)
    yg = yg * lax.rsqrt(jnp.mean(jnp.square(yg), axis=-1, keepdims=True) + RMS_EPS)
    return yg.reshape(b, s, c) * g.astype(jnp.float32)


def apply_rope(x, cos, sin):
    half = x.shape[-1] // 2
    xf = x.astype(jnp.float32)
    x1, x2 = xf[..., :half], xf[..., half:]
    return jnp.concatenate([x1 * cos - x2 * sin, x2 * cos + x1 * sin], axis=-1).astype(x.dtype)


def causal_depthwise_conv(u, w, b):
    c = u.shape[-1]
    y = lax.conv_general_dilated(u, w[:, None, :].astype(u.dtype), window_strides=(1,),
                                 padding=[(SSD_CONV - 1, 0)],
                                 dimension_numbers=("NWC", "WIO", "NWC"),
                                 feature_group_count=c)
    return y + b


def segsum(a):
    t = a.shape[-1]
    aa = jnp.broadcast_to(a[..., :, None], a.shape + (t,))
    aa = jnp.where(jnp.tril(jnp.ones((t, t), dtype=bool), -1), aa, 0.0)
    ss = jnp.cumsum(aa, axis=-2)
    return jnp.where(jnp.tril(jnp.ones((t, t), dtype=bool)), ss, -jnp.inf)


def ssd_chunked_scan(x, dt, a_head, bm, cm):
    b, s, h, p = x.shape
    g, n = bm.shape[-2:]
    e = h // g
    L = SSD_CHUNK
    c = s // L
    xf = (x.astype(jnp.float32) * dt[..., None]).reshape(b, c, L, g, e, p)
    a = jnp.moveaxis((dt * a_head).reshape(b, c, L, g, e), 2, -1)
    bc = bm.astype(jnp.float32).reshape(b, c, L, g, n)
    cc = cm.astype(jnp.float32).reshape(b, c, L, g, n)
    a_cs = jnp.cumsum(a, axis=-1)
    decay_ls = jnp.exp(segsum(a))
    cb = jnp.einsum("bclgn,bcsgn->bcgls", cc, bc)
    y_diag = jnp.einsum("bcgls,bcgels,bcsgep->bclgep", cb, decay_ls, xf)
    decay_to_end = jnp.exp(a_cs[..., -1:] - a_cs)
    states = jnp.einsum("bclgn,bcgel,bclgep->bcgepn", bc, decay_to_end, xf)
    chunk_decay = jnp.exp(a_cs[..., -1])

    def step(carry, inp):
        st, dec = inp
        return carry * dec[..., None, None] + st, carry

    init = jnp.zeros((b, g, e, p, n), jnp.float32)
    _, prev = lax.scan(step, init, (jnp.moveaxis(states, 1, 0), jnp.moveaxis(chunk_decay, 1, 0)))
    prev = jnp.moveaxis(prev, 0, 1)
    y_off = jnp.einsum("bclgn,bcgepn,bcgel->bclgep", cc, prev, jnp.exp(a_cs))
    return (y_diag + y_off).reshape(b, s, h, p)


def causal_block_attention(q, k, v, scale):
    b, s, h, d = q.shape
    nb = s // ATTN_BLOCK
    qb = jnp.moveaxis(q.reshape(b, nb, ATTN_BLOCK, h, d), 1, 0)
    k_pos = jnp.arange(s)

    def one_block(args):
        q_blk, i = args
        q_pos = i * ATTN_BLOCK + jnp.arange(ATTN_BLOCK)
        sc = jnp.einsum("bqhd,bkhd->bhqk", q_blk, k).astype(jnp.float32) * scale
        sc = jnp.where(k_pos[None, :] <= q_pos[:, None], sc, -jnp.inf)
        pr = jax.nn.softmax(sc, axis=-1).astype(v.dtype)
        return jnp.einsum("bhqk,bkhd->bqhd", pr, v)

    out = lax.map(one_block, (qb, jnp.arange(nb)))
    return jnp.moveaxis(out, 0, 1).reshape(b, s, h, v.shape[-1])


def hybrid_mixer(h, cos, sin, w_in, conv_w, conv_b, dt_bias, a_log, d_skip, ssd_norm_g,
                 q_norm_g, w_q_up, kv_norm_g, w_kv_up, w_out):
    b, s, _ = h.shape
    proj = h @ w_in
    z, xbc, dt_raw, q_lat, kv_lat, k_r = jnp.split(proj, IN_SPLITS, axis=-1)
    xbc = jax.nn.silu(causal_depthwise_conv(xbc, conv_w, conv_b))
    xs, bm, cm = jnp.split(xbc, [SSD_INNER, SSD_INNER + SSD_GROUPS * SSD_STATE], axis=-1)
    xs = xs.reshape(b, s, SSD_HEADS, SSD_HEAD_DIM)
    bm = bm.reshape(b, s, SSD_GROUPS, SSD_STATE)
    cm = cm.reshape(b, s, SSD_GROUPS, SSD_STATE)
    dt = jax.nn.softplus(dt_raw.astype(jnp.float32) + dt_bias.astype(jnp.float32))
    a_head = -jnp.exp(a_log.astype(jnp.float32))
    y = ssd_chunked_scan(xs, dt, a_head, bm, cm) + xs.astype(jnp.float32) * d_skip.astype(jnp.float32)[:, None]
    y = y.reshape(b, s, SSD_INNER) * jax.nn.silu(z.astype(jnp.float32))
    y = grouped_rms_norm(y, ssd_norm_g, SSD_GROUPS).astype(h.dtype)
    q = (rms_norm(q_lat, q_norm_g) @ w_q_up).reshape(b, s, MLA_HEADS, MLA_QK)
    q = jnp.concatenate([q[..., :MLA_NOPE], apply_rope(q[..., MLA_NOPE:], cos, sin)], axis=-1)
    kv = (rms_norm(kv_lat, kv_norm_g) @ w_kv_up).reshape(b, s, MLA_HEADS, MLA_NOPE + MLA_V)
    k_pe = apply_rope(k_r[:, :, None, :], cos, sin)
    k = jnp.concatenate([kv[..., :MLA_NOPE],
                         jnp.broadcast_to(k_pe, (b, s, MLA_HEADS, MLA_ROPE))], axis=-1)
    v = kv[..., MLA_NOPE:]
    o = causal_block_attention(q, k, v, MLA_QK ** -0.5).reshape(b, s, MLA_HEADS * MLA_V)
    return jnp.concatenate([y, o], axis=-1) @ w_out


def memory_cross_attention(h, mem, w_q, w_k, w_v, w_o):
    b, s, _ = h.shape
    m = mem.shape[1]
    q = (h @ w_q).reshape(b, s, MEM_HEADS, MEM_HEAD_DIM)
    k = (mem @ w_k).reshape(b, m, MEM_HEADS, MEM_HEAD_DIM)
    v = (mem @ w_v).reshape(b, m, MEM_HEADS, MEM_HEAD_DIM)
    sc = jnp.einsum("bshd,bmhd->bhsm", q, k).astype(jnp.float32) * (MEM_HEAD_DIM ** -0.5)
    pr = jax.nn.softmax(sc, axis=-1).astype(v.dtype)
    o = jnp.einsum("bhsm,bmhd->bshd", pr, v).reshape(b, s, D_MODEL)
    return o @ w_o


def sq_relu_mlp(h, w_up, w_down):
    return jnp.square(jax.nn.relu(h @ w_up)) @ w_down


def _fwd_setup_inputs(seed: int = 0) -> dict:
    key = jax.random.key(seed)
    ks = jax.random.split(key, 32)
    f32 = jnp.float32

    def w(k, shape, fan_in, scale=1.0):
        return jax.random.normal(k, shape, f32) * (fan_in ** -0.5) * scale

    def gain(k, shape):
        return 1.0 + 0.02 * jax.random.normal(k, shape, f32)

    def bias(k, shape):
        return 0.02 * jax.random.normal(k, shape, f32)

    x = jax.random.normal(ks[0], (BATCH, SEQ, D_MODEL), f32)
    mem = jax.random.normal(ks[1], (BATCH, MEM_TOKENS, D_MODEL), f32)
    start = jax.random.randint(ks[2], (BATCH, 1), 0, 4096, dtype=jnp.int32)
    positions = (start + jnp.arange(SEQ, dtype=jnp.int32)[None, :]).astype(jnp.int32)

    dt0 = jnp.exp(jax.random.uniform(ks[3], (DEPTH, SSD_HEADS), f32,
                                     minval=math.log(1e-3), maxval=math.log(1e-1)))
    dt_bias = dt0 + jnp.log(-jnp.expm1(-dt0))
    a_log = jnp.log(jax.random.uniform(ks[4], (DEPTH, SSD_HEADS), f32, minval=1.0, maxval=16.0))
    v_col = (jnp.arange(MLA_NOPE + MLA_V) >= MLA_NOPE)
    kv_scale = jnp.tile(jnp.where(v_col, DEEPNORM_BETA, 1.0), MLA_HEADS).astype(f32)

    return {
        "x": x,
        "mem": mem,
        "positions": positions,
        "ln_in_g": gain(ks[5], (D_MODEL,)),
        "ln_in_b": bias(ks[6], (D_MODEL,)),
        "w_in": w(ks[7], (DEPTH, D_MODEL, IN_WIDTH), D_MODEL),
        "conv_w": w(ks[8], (DEPTH, SSD_CONV, SSD_XBC), SSD_CONV),
        "conv_b": bias(ks[9], (DEPTH, SSD_XBC)),
        "dt_bias": dt_bias,
        "a_log": a_log,
        "d_skip": gain(ks[10], (DEPTH, SSD_HEADS)),
        "ssd_norm_g": gain(ks[11], (DEPTH, SSD_INNER)),
        "q_norm_g": gain(ks[12], (DEPTH, MLA_Q_RANK)),
        "w_q_up": w(ks[13], (DEPTH, MLA_Q_RANK, MLA_HEADS * MLA_QK), MLA_Q_RANK),
        "kv_norm_g": gain(ks[14], (DEPTH, MLA_KV_RANK)),
        "w_kv_up": w(ks[15], (DEPTH, MLA_KV_RANK, MLA_HEADS * (MLA_NOPE + MLA_V)), MLA_KV_RANK) * kv_scale,
        "w_mix_out": w(ks[16], (DEPTH, MIX_WIDTH, D_MODEL), MIX_WIDTH, DEEPNORM_BETA),
        "ln1_g": gain(ks[17], (DEPTH, D_MODEL)),
        "ln1_b": bias(ks[18], (DEPTH, D_MODEL)),
        "w_mem_q": w(ks[19], (DEPTH, D_MODEL, D_MODEL), D_MODEL),
        "w_mem_k": w(ks[20], (DEPTH, D_MODEL, D_MODEL), D_MODEL),
        "w_mem_v": w(ks[21], (DEPTH, D_MODEL, D_MODEL), D_MODEL, DEEPNORM_BETA),
        "w_mem_o": w(ks[22], (DEPTH, D_MODEL, D_MODEL), D_MODEL, DEEPNORM_BETA),
        "ln2_g": gain(ks[23], (DEPTH, D_MODEL)),
        "ln2_b": bias(ks[24], (DEPTH, D_MODEL)),
        "w_up": w(ks[25], (DEPTH, D_MODEL, D_FF), D_MODEL, DEEPNORM_BETA),
        "w_down": w(ks[26], (DEPTH, D_FF, D_MODEL), D_FF, DEEPNORM_BETA),
        "ln3_g": gain(ks[27], (DEPTH, D_MODEL)),
        "ln3_b": bias(ks[28], (DEPTH, D_MODEL)),
    }


def _fwd_reference(x, mem, positions, ln_in_g, ln_in_b, w_in, conv_w, conv_b, dt_bias, a_log, d_skip,
              ssd_norm_g, q_norm_g, w_q_up, kv_norm_g, w_kv_up, w_mix_out, ln1_g, ln1_b,
              w_mem_q, w_mem_k, w_mem_v, w_mem_o, ln2_g, ln2_b, w_up, w_down, ln3_g, ln3_b):
    half = MLA_ROPE // 2
    inv_freq = jnp.power(ROPE_THETA, -jnp.arange(half, dtype=jnp.float32) / half)
    ang = positions.astype(jnp.float32)[..., None] * inv_freq
    cos = jnp.cos(ang)[:, :, None, :]
    sin = jnp.sin(ang)[:, :, None, :]

    h = layer_norm(x, ln_in_g, ln_in_b)
    for l in range(DEPTH):
        mix = hybrid_mixer(h, cos, sin, w_in[l], conv_w[l], conv_b[l], dt_bias[l], a_log[l],
                           d_skip[l], ssd_norm_g[l], q_norm_g[l], w_q_up[l], kv_norm_g[l],
                           w_kv_up[l], w_mix_out[l])
        h = layer_norm(DEEPNORM_ALPHA * h + mix, ln1_g[l], ln1_b[l])
        xa = memory_cross_attention(h, mem, w_mem_q[l], w_mem_k[l], w_mem_v[l], w_mem_o[l])
        h = layer_norm(DEEPNORM_ALPHA * h + xa, ln2_g[l], ln2_b[l])
        ff = sq_relu_mlp(h, w_up[l], w_down[l])
        h = layer_norm(DEEPNORM_ALPHA * h + ff, ln3_g[l], ln3_b[l])
    return h


import jax as _jax
import jax.numpy as _jnp

TWIN_FORMAT = 'train_step'
FWD_PARAMS = ['x', 'mem', 'positions', 'ln_in_g', 'ln_in_b', 'w_in', 'conv_w', 'conv_b', 'dt_bias', 'a_log', 'd_skip', 'ssd_norm_g', 'q_norm_g', 'w_q_up', 'kv_norm_g', 'w_kv_up', 'w_mix_out', 'ln1_g', 'ln1_b', 'w_mem_q', 'w_mem_k', 'w_mem_v', 'w_mem_o', 'ln2_g', 'ln2_b', 'w_up', 'w_down', 'ln3_g', 'ln3_b']
TWIN_WEIGHTS = ['ln_in_g', 'ln_in_b', 'w_in', 'conv_w', 'conv_b', 'dt_bias', 'a_log', 'd_skip', 'ssd_norm_g', 'q_norm_g', 'w_q_up', 'kv_norm_g', 'w_kv_up', 'w_mix_out', 'ln1_g', 'ln1_b', 'w_mem_q', 'w_mem_k', 'w_mem_v', 'w_mem_o', 'ln2_g', 'ln2_b', 'w_up', 'w_down', 'ln3_g', 'ln3_b']
TWIN_DIFF_INPUT = 'x'
TWIN_INPUTS = ['x', 'mem', 'positions', 'ln_in_g', 'ln_in_b', 'w_in', 'conv_w', 'conv_b', 'dt_bias', 'a_log', 'd_skip', 'ssd_norm_g', 'q_norm_g', 'w_q_up', 'kv_norm_g', 'w_kv_up', 'w_mix_out', 'ln1_g', 'ln1_b', 'w_mem_q', 'w_mem_k', 'w_mem_v', 'w_mem_o', 'ln2_g', 'ln2_b', 'w_up', 'w_down', 'ln3_g', 'ln3_b', 'loss_target', 'm_ln_in_g', 'm_ln_in_b', 'm_w_in', 'm_conv_w', 'm_conv_b', 'm_dt_bias', 'm_a_log', 'm_d_skip', 'm_ssd_norm_g', 'm_q_norm_g', 'm_w_q_up', 'm_kv_norm_g', 'm_w_kv_up', 'm_w_mix_out', 'm_ln1_g', 'm_ln1_b', 'm_w_mem_q', 'm_w_mem_k', 'm_w_mem_v', 'm_w_mem_o', 'm_ln2_g', 'm_ln2_b', 'm_w_up', 'm_w_down', 'm_ln3_g', 'm_ln3_b', 'v_ln_in_g', 'v_ln_in_b', 'v_w_in', 'v_conv_w', 'v_conv_b', 'v_dt_bias', 'v_a_log', 'v_d_skip', 'v_ssd_norm_g', 'v_q_norm_g', 'v_w_q_up', 'v_kv_norm_g', 'v_w_kv_up', 'v_w_mix_out', 'v_ln1_g', 'v_ln1_b', 'v_w_mem_q', 'v_w_mem_k', 'v_w_mem_v', 'v_w_mem_o', 'v_ln2_g', 'v_ln2_b', 'v_w_up', 'v_w_down', 'v_ln3_g', 'v_ln3_b']
TWIN_OUTPUTS = ['loss', 'grad_x', 'grad_ln_in_g', 'grad_ln_in_b', 'grad_w_in', 'grad_conv_w', 'grad_conv_b', 'grad_dt_bias', 'grad_a_log', 'grad_d_skip', 'grad_ssd_norm_g', 'grad_q_norm_g', 'grad_w_q_up', 'grad_kv_norm_g', 'grad_w_kv_up', 'grad_w_mix_out', 'grad_ln1_g', 'grad_ln1_b', 'grad_w_mem_q', 'grad_w_mem_k', 'grad_w_mem_v', 'grad_w_mem_o', 'grad_ln2_g', 'grad_ln2_b', 'grad_w_up', 'grad_w_down', 'grad_ln3_g', 'grad_ln3_b', 'delta_ln_in_g', 'delta_ln_in_b', 'delta_w_in', 'delta_conv_w', 'delta_conv_b', 'delta_dt_bias', 'delta_a_log', 'delta_d_skip', 'delta_ssd_norm_g', 'delta_q_norm_g', 'delta_w_q_up', 'delta_kv_norm_g', 'delta_w_kv_up', 'delta_w_mix_out', 'delta_ln1_g', 'delta_ln1_b', 'delta_w_mem_q', 'delta_w_mem_k', 'delta_w_mem_v', 'delta_w_mem_o', 'delta_ln2_g', 'delta_ln2_b', 'delta_w_up', 'delta_w_down', 'delta_ln3_g', 'delta_ln3_b', 'new_m_ln_in_g', 'new_m_ln_in_b', 'new_m_w_in', 'new_m_conv_w', 'new_m_conv_b', 'new_m_dt_bias', 'new_m_a_log', 'new_m_d_skip', 'new_m_ssd_norm_g', 'new_m_q_norm_g', 'new_m_w_q_up', 'new_m_kv_norm_g', 'new_m_w_kv_up', 'new_m_w_mix_out', 'new_m_ln1_g', 'new_m_ln1_b', 'new_m_w_mem_q', 'new_m_w_mem_k', 'new_m_w_mem_v', 'new_m_w_mem_o', 'new_m_ln2_g', 'new_m_ln2_b', 'new_m_w_up', 'new_m_w_down', 'new_m_ln3_g', 'new_m_ln3_b', 'new_v_ln_in_g', 'new_v_ln_in_b', 'new_v_w_in', 'new_v_conv_w', 'new_v_conv_b', 'new_v_dt_bias', 'new_v_a_log', 'new_v_d_skip', 'new_v_ssd_norm_g', 'new_v_q_norm_g', 'new_v_w_q_up', 'new_v_kv_norm_g', 'new_v_w_kv_up', 'new_v_w_mix_out', 'new_v_ln1_g', 'new_v_ln1_b', 'new_v_w_mem_q', 'new_v_w_mem_k', 'new_v_w_mem_v', 'new_v_w_mem_o', 'new_v_ln2_g', 'new_v_ln2_b', 'new_v_w_up', 'new_v_w_down', 'new_v_ln3_g', 'new_v_ln3_b']
TWIN_LEAF_KINDS = {'loss': 'loss', 'grad_x': 'grad_x', 'grad_ln_in_g': 'grad_w', 'grad_ln_in_b': 'grad_w', 'grad_w_in': 'grad_w', 'grad_conv_w': 'grad_w', 'grad_conv_b': 'grad_w', 'grad_dt_bias': 'grad_w', 'grad_a_log': 'grad_w', 'grad_d_skip': 'grad_w', 'grad_ssd_norm_g': 'grad_w', 'grad_q_norm_g': 'grad_w', 'grad_w_q_up': 'grad_w', 'grad_kv_norm_g': 'grad_w', 'grad_w_kv_up': 'grad_w', 'grad_w_mix_out': 'grad_w', 'grad_ln1_g': 'grad_w', 'grad_ln1_b': 'grad_w', 'grad_w_mem_q': 'grad_w', 'grad_w_mem_k': 'grad_w', 'grad_w_mem_v': 'grad_w', 'grad_w_mem_o': 'grad_w', 'grad_ln2_g': 'grad_w', 'grad_ln2_b': 'grad_w', 'grad_w_up': 'grad_w', 'grad_w_down': 'grad_w', 'grad_ln3_g': 'grad_w', 'grad_ln3_b': 'grad_w', 'delta_ln_in_g': 'delta_w', 'delta_ln_in_b': 'delta_w', 'delta_w_in': 'delta_w', 'delta_conv_w': 'delta_w', 'delta_conv_b': 'delta_w', 'delta_dt_bias': 'delta_w', 'delta_a_log': 'delta_w', 'delta_d_skip': 'delta_w', 'delta_ssd_norm_g': 'delta_w', 'delta_q_norm_g': 'delta_w', 'delta_w_q_up': 'delta_w', 'delta_kv_norm_g': 'delta_w', 'delta_w_kv_up': 'delta_w', 'delta_w_mix_out': 'delta_w', 'delta_ln1_g': 'delta_w', 'delta_ln1_b': 'delta_w', 'delta_w_mem_q': 'delta_w', 'delta_w_mem_k': 'delta_w', 'delta_w_mem_v': 'delta_w', 'delta_w_mem_o': 'delta_w', 'delta_ln2_g': 'delta_w', 'delta_ln2_b': 'delta_w', 'delta_w_up': 'delta_w', 'delta_w_down': 'delta_w', 'delta_ln3_g': 'delta_w', 'delta_ln3_b': 'delta_w', 'new_m_ln_in_g': 'new_m', 'new_m_ln_in_b': 'new_m', 'new_m_w_in': 'new_m', 'new_m_conv_w': 'new_m', 'new_m_conv_b': 'new_m', 'new_m_dt_bias': 'new_m', 'new_m_a_log': 'new_m', 'new_m_d_skip': 'new_m', 'new_m_ssd_norm_g': 'new_m', 'new_m_q_norm_g': 'new_m', 'new_m_w_q_up': 'new_m', 'new_m_kv_norm_g': 'new_m', 'new_m_w_kv_up': 'new_m', 'new_m_w_mix_out': 'new_m', 'new_m_ln1_g': 'new_m', 'new_m_ln1_b': 'new_m', 'new_m_w_mem_q': 'new_m', 'new_m_w_mem_k': 'new_m', 'new_m_w_mem_v': 'new_m', 'new_m_w_mem_o': 'new_m', 'new_m_ln2_g': 'new_m', 'new_m_ln2_b': 'new_m', 'new_m_w_up': 'new_m', 'new_m_w_down': 'new_m', 'new_m_ln3_g': 'new_m', 'new_m_ln3_b': 'new_m', 'new_v_ln_in_g': 'new_v', 'new_v_ln_in_b': 'new_v', 'new_v_w_in': 'new_v', 'new_v_conv_w': 'new_v', 'new_v_conv_b': 'new_v', 'new_v_dt_bias': 'new_v', 'new_v_a_log': 'new_v', 'new_v_d_skip': 'new_v', 'new_v_ssd_norm_g': 'new_v', 'new_v_q_norm_g': 'new_v', 'new_v_w_q_up': 'new_v', 'new_v_kv_norm_g': 'new_v', 'new_v_w_kv_up': 'new_v', 'new_v_w_mix_out': 'new_v', 'new_v_ln1_g': 'new_v', 'new_v_ln1_b': 'new_v', 'new_v_w_mem_q': 'new_v', 'new_v_w_mem_k': 'new_v', 'new_v_w_mem_v': 'new_v', 'new_v_w_mem_o': 'new_v', 'new_v_ln2_g': 'new_v', 'new_v_ln2_b': 'new_v', 'new_v_w_up': 'new_v', 'new_v_w_down': 'new_v', 'new_v_ln3_g': 'new_v', 'new_v_ln3_b': 'new_v'}


def _forward(args):
    return _fwd_reference(*[args[k] for k in FWD_PARAMS])


def _output_shape():
    out = _jax.eval_shape(lambda: _forward(_fwd_setup_inputs(0)))
    return out.shape, out.dtype

N_MICROBATCH = 1
ADAM_LR = 0.001
ADAM_B1 = 0.9
ADAM_B2 = 0.999
ADAM_EPS = 1e-08
ADAM_WD = 0.01
ADAM_STEP = 10
PER_EXAMPLE_BATCH_AXIS = {'x': 0, 'mem': 0, 'positions': 0, 'loss_target': 0}
SHARED_INPUTS = []
_WEIGHT_DTYPES = {'ln_in_g': _jnp.float32, 'ln_in_b': _jnp.float32, 'w_in': _jnp.float32, 'conv_w': _jnp.float32, 'conv_b': _jnp.float32, 'dt_bias': _jnp.float32, 'a_log': _jnp.float32, 'd_skip': _jnp.float32, 'ssd_norm_g': _jnp.float32, 'q_norm_g': _jnp.float32, 'w_q_up': _jnp.float32, 'kv_norm_g': _jnp.float32, 'w_kv_up': _jnp.float32, 'w_mix_out': _jnp.float32, 'ln1_g': _jnp.float32, 'ln1_b': _jnp.float32, 'w_mem_q': _jnp.float32, 'w_mem_k': _jnp.float32, 'w_mem_v': _jnp.float32, 'w_mem_o': _jnp.float32, 'ln2_g': _jnp.float32, 'ln2_b': _jnp.float32, 'w_up': _jnp.float32, 'w_down': _jnp.float32, 'ln3_g': _jnp.float32, 'ln3_b': _jnp.float32}
MOMENT_SCALE = {'ln_in_g': 7.941087e-01, 'ln_in_b': 4.484283e-01, 'w_in': 6.278407e-02, 'conv_w': 6.437875e-02, 'conv_b': 9.229164e-02, 'dt_bias': 4.325637e-01, 'a_log': 2.985339e-01, 'd_skip': 2.677477e-01, 'ssd_norm_g': 8.512856e-02, 'q_norm_g': 1.053711e-02, 'w_q_up': 7.470210e-03, 'kv_norm_g': 1.816091e-02, 'w_kv_up': 1.441242e-02, 'w_mix_out': 1.059505e-01, 'ln1_g': 9.624966e-01, 'ln1_b': 4.656568e-01, 'w_mem_q': 5.585395e-03, 'w_mem_k': 5.611291e-03, 'w_mem_v': 1.086960e-02, 'w_mem_o': 1.061582e-02, 'ln2_g': 9.621782e-01, 'ln2_b': 4.669835e-01, 'w_up': 3.652402e-02, 'w_down': 8.249844e-02, 'ln3_g': 3.206526e+01, 'ln3_b': 3.564374e+00}


def _to_microbatches(a, axis):
    t = _jnp.moveaxis(a, axis, 0)
    t = t.reshape((N_MICROBATCH, t.shape[0] // N_MICROBATCH) + t.shape[1:])
    return _jnp.moveaxis(t, 1, axis + 1)


def setup_inputs(seed: int = 0) -> dict:
    inp = _fwd_setup_inputs(seed)
    key = _jax.random.fold_in(_jax.random.key(seed), 7919)
    shape, _ = _output_shape()
    out = dict(inp)
    out["loss_target"] = _jax.random.normal(_jax.random.fold_in(key, 0), shape, _jnp.float32)
    for i, name in enumerate(TWIN_WEIGHTS):
        w = inp[name].astype(_jnp.float32)
        if MOMENT_SCALE is None:
            s = _jnp.sqrt(_jnp.mean(_jnp.square(w)) + 1e-30)
        else:
            s = MOMENT_SCALE[name]
        km, kv = _jax.random.split(_jax.random.fold_in(key, i + 1))
        out[name] = w
        out["m_" + name] = s * _jax.random.normal(km, w.shape, _jnp.float32)
        out["v_" + name] = (s * s) * _jax.random.uniform(kv, w.shape, _jnp.float32, 0.5, 1.5)
    if N_MICROBATCH > 1:
        for name, axis in PER_EXAMPLE_BATCH_AXIS.items():
            out[name] = _to_microbatches(out[name], axis)
    return {'x': out['x'], 'mem': out['mem'], 'positions': out['positions'], 'ln_in_g': out['ln_in_g'], 'ln_in_b': out['ln_in_b'], 'w_in': out['w_in'], 'conv_w': out['conv_w'], 'conv_b': out['conv_b'], 'dt_bias': out['dt_bias'], 'a_log': out['a_log'], 'd_skip': out['d_skip'], 'ssd_norm_g': out['ssd_norm_g'], 'q_norm_g': out['q_norm_g'], 'w_q_up': out['w_q_up'], 'kv_norm_g': out['kv_norm_g'], 'w_kv_up': out['w_kv_up'], 'w_mix_out': out['w_mix_out'], 'ln1_g': out['ln1_g'], 'ln1_b': out['ln1_b'], 'w_mem_q': out['w_mem_q'], 'w_mem_k': out['w_mem_k'], 'w_mem_v': out['w_mem_v'], 'w_mem_o': out['w_mem_o'], 'ln2_g': out['ln2_g'], 'ln2_b': out['ln2_b'], 'w_up': out['w_up'], 'w_down': out['w_down'], 'ln3_g': out['ln3_g'], 'ln3_b': out['ln3_b'], 'loss_target': out['loss_target'], 'm_ln_in_g': out['m_ln_in_g'], 'm_ln_in_b': out['m_ln_in_b'], 'm_w_in': out['m_w_in'], 'm_conv_w': out['m_conv_w'], 'm_conv_b': out['m_conv_b'], 'm_dt_bias': out['m_dt_bias'], 'm_a_log': out['m_a_log'], 'm_d_skip': out['m_d_skip'], 'm_ssd_norm_g': out['m_ssd_norm_g'], 'm_q_norm_g': out['m_q_norm_g'], 'm_w_q_up': out['m_w_q_up'], 'm_kv_norm_g': out['m_kv_norm_g'], 'm_w_kv_up': out['m_w_kv_up'], 'm_w_mix_out': out['m_w_mix_out'], 'm_ln1_g': out['m_ln1_g'], 'm_ln1_b': out['m_ln1_b'], 'm_w_mem_q': out['m_w_mem_q'], 'm_w_mem_k': out['m_w_mem_k'], 'm_w_mem_v': out['m_w_mem_v'], 'm_w_mem_o': out['m_w_mem_o'], 'm_ln2_g': out['m_ln2_g'], 'm_ln2_b': out['m_ln2_b'], 'm_w_up': out['m_w_up'], 'm_w_down': out['m_w_down'], 'm_ln3_g': out['m_ln3_g'], 'm_ln3_b': out['m_ln3_b'], 'v_ln_in_g': out['v_ln_in_g'], 'v_ln_in_b': out['v_ln_in_b'], 'v_w_in': out['v_w_in'], 'v_conv_w': out['v_conv_w'], 'v_conv_b': out['v_conv_b'], 'v_dt_bias': out['v_dt_bias'], 'v_a_log': out['v_a_log'], 'v_d_skip': out['v_d_skip'], 'v_ssd_norm_g': out['v_ssd_norm_g'], 'v_q_norm_g': out['v_q_norm_g'], 'v_w_q_up': out['v_w_q_up'], 'v_kv_norm_g': out['v_kv_norm_g'], 'v_w_kv_up': out['v_w_kv_up'], 'v_w_mix_out': out['v_w_mix_out'], 'v_ln1_g': out['v_ln1_g'], 'v_ln1_b': out['v_ln1_b'], 'v_w_mem_q': out['v_w_mem_q'], 'v_w_mem_k': out['v_w_mem_k'], 'v_w_mem_v': out['v_w_mem_v'], 'v_w_mem_o': out['v_w_mem_o'], 'v_ln2_g': out['v_ln2_g'], 'v_ln2_b': out['v_ln2_b'], 'v_w_up': out['v_w_up'], 'v_w_down': out['v_w_down'], 'v_ln3_g': out['v_ln3_g'], 'v_ln3_b': out['v_ln3_b']}


def _loss(weights, diff, rest, loss_target):
    with _jax.named_scope("forward"):
        args = {**rest, TWIN_DIFF_INPUT: diff, **{k: w.astype(_WEIGHT_DTYPES[k]) for k, w in weights.items()}}
        y = _forward(args)
    with _jax.named_scope("loss_head"):
        err = _jnp.square(y.astype(_jnp.float32) - loss_target)
        return 0.5 * _jnp.sum(_jnp.mean(err, axis=-1)) if err.ndim else 0.5 * err


def _adamw(w, g, m, v):
    m = ADAM_B1 * m + (1.0 - ADAM_B1) * g
    v = ADAM_B2 * v + (1.0 - ADAM_B2) * _jnp.square(g)
    m_hat = m / (1.0 - ADAM_B1 ** ADAM_STEP)
    v_hat = v / (1.0 - ADAM_B2 ** ADAM_STEP)
    delta = -ADAM_LR * (m_hat / (_jnp.sqrt(v_hat) + ADAM_EPS) + ADAM_WD * w)
    return delta, m, v


def reference(x, mem, positions, ln_in_g, ln_in_b, w_in, conv_w, conv_b, dt_bias, a_log, d_skip, ssd_norm_g, q_norm_g, w_q_up, kv_norm_g, w_kv_up, w_mix_out, ln1_g, ln1_b, w_mem_q, w_mem_k, w_mem_v, w_mem_o, ln2_g, ln2_b, w_up, w_down, ln3_g, ln3_b, loss_target, m_ln_in_g, m_ln_in_b, m_w_in, m_conv_w, m_conv_b, m_dt_bias, m_a_log, m_d_skip, m_ssd_norm_g, m_q_norm_g, m_w_q_up, m_kv_norm_g, m_w_kv_up, m_w_mix_out, m_ln1_g, m_ln1_b, m_w_mem_q, m_w_mem_k, m_w_mem_v, m_w_mem_o, m_ln2_g, m_ln2_b, m_w_up, m_w_down, m_ln3_g, m_ln3_b, v_ln_in_g, v_ln_in_b, v_w_in, v_conv_w, v_conv_b, v_dt_bias, v_a_log, v_d_skip, v_ssd_norm_g, v_q_norm_g, v_w_q_up, v_kv_norm_g, v_w_kv_up, v_w_mix_out, v_ln1_g, v_ln1_b, v_w_mem_q, v_w_mem_k, v_w_mem_v, v_w_mem_o, v_ln2_g, v_ln2_b, v_w_up, v_w_down, v_ln3_g, v_ln3_b):
    given = dict(x=x, mem=mem, positions=positions, ln_in_g=ln_in_g, ln_in_b=ln_in_b, w_in=w_in, conv_w=conv_w, conv_b=conv_b, dt_bias=dt_bias, a_log=a_log, d_skip=d_skip, ssd_norm_g=ssd_norm_g, q_norm_g=q_norm_g, w_q_up=w_q_up, kv_norm_g=kv_norm_g, w_kv_up=w_kv_up, w_mix_out=w_mix_out, ln1_g=ln1_g, ln1_b=ln1_b, w_mem_q=w_mem_q, w_mem_k=w_mem_k, w_mem_v=w_mem_v, w_mem_o=w_mem_o, ln2_g=ln2_g, ln2_b=ln2_b, w_up=w_up, w_down=w_down, ln3_g=ln3_g, ln3_b=ln3_b, loss_target=loss_target, m_ln_in_g=m_ln_in_g, m_ln_in_b=m_ln_in_b, m_w_in=m_w_in, m_conv_w=m_conv_w, m_conv_b=m_conv_b, m_dt_bias=m_dt_bias, m_a_log=m_a_log, m_d_skip=m_d_skip, m_ssd_norm_g=m_ssd_norm_g, m_q_norm_g=m_q_norm_g, m_w_q_up=m_w_q_up, m_kv_norm_g=m_kv_norm_g, m_w_kv_up=m_w_kv_up, m_w_mix_out=m_w_mix_out, m_ln1_g=m_ln1_g, m_ln1_b=m_ln1_b, m_w_mem_q=m_w_mem_q, m_w_mem_k=m_w_mem_k, m_w_mem_v=m_w_mem_v, m_w_mem_o=m_w_mem_o, m_ln2_g=m_ln2_g, m_ln2_b=m_ln2_b, m_w_up=m_w_up, m_w_down=m_w_down, m_ln3_g=m_ln3_g, m_ln3_b=m_ln3_b, v_ln_in_g=v_ln_in_g, v_ln_in_b=v_ln_in_b, v_w_in=v_w_in, v_conv_w=v_conv_w, v_conv_b=v_conv_b, v_dt_bias=v_dt_bias, v_a_log=v_a_log, v_d_skip=v_d_skip, v_ssd_norm_g=v_ssd_norm_g, v_q_norm_g=v_q_norm_g, v_w_q_up=v_w_q_up, v_kv_norm_g=v_kv_norm_g, v_w_kv_up=v_w_kv_up, v_w_mix_out=v_w_mix_out, v_ln1_g=v_ln1_g, v_ln1_b=v_ln1_b, v_w_mem_q=v_w_mem_q, v_w_mem_k=v_w_mem_k, v_w_mem_v=v_w_mem_v, v_w_mem_o=v_w_mem_o, v_ln2_g=v_ln2_g, v_ln2_b=v_ln2_b, v_w_up=v_w_up, v_w_down=v_w_down, v_ln3_g=v_ln3_g, v_ln3_b=v_ln3_b)
    weights = {n: given[n] for n in TWIN_WEIGHTS}
    shared = {n: given[n] for n in SHARED_INPUTS}
    per_example = {n: given[n] for n in ['x', 'mem', 'positions']}
    grad_fn = _jax.value_and_grad(_loss, argnums=(0, 1))

    def one_microbatch(ex, loss_target):
        ex = dict(ex)
        diff = ex.pop(TWIN_DIFF_INPUT)
        return grad_fn(weights, diff, {**shared, **ex}, loss_target)

    if N_MICROBATCH == 1:
        loss, (grad_w, grad_x) = one_microbatch(per_example, given["loss_target"])
    else:
        def body(carry, xs):
            loss_sum, grad_sum = carry
            l_k, (gw_k, gx_k) = one_microbatch(xs[0], xs[1])
            with _jax.named_scope("update"):
                return (loss_sum + l_k, _jax.tree.map(_jnp.add, grad_sum, gw_k)), gx_k

        init = (_jnp.zeros((), _jnp.float32), _jax.tree.map(_jnp.zeros_like, weights))
        (loss, grad_w), grad_x = _jax.lax.scan(body, init, (per_example, given["loss_target"]))
    with _jax.named_scope("update"):
        delta_w, new_m, new_v = {}, {}, {}
        for n in TWIN_WEIGHTS:
            delta_w[n], new_m[n], new_v[n] = _adamw(weights[n], grad_w[n], given["m_" + n], given["v_" + n])
    return (loss, grad_x, *[grad_w[n] for n in TWIN_WEIGHTS], *[delta_w[n] for n in TWIN_WEIGHTS],
            *[new_m[n] for n in TWIN_WEIGHTS], *[new_v[n] for n in TWIN_WEIGHTS])
```

```python
import functools
import math

import jax
import jax.numpy as jnp
from jax import lax
from jax.experimental import pallas as pl
from jax.experimental.pallas import tpu as pltpu

F32, BF16 = jnp.float32, jnp.bfloat16

N_DEV = 8
D_MODEL = 1024
SSD_HEADS, SSD_HEAD_DIM, SSD_INNER, SSD_STATE, SSD_CONV, SSD_CHUNK = 8, 64, 512, 128, 4, 128
SSD_XBC = 1024
MLA_HEADS, MLA_NOPE, MLA_ROPE, MLA_QK, MLA_V = 8, 64, 32, 96, 64
MLA_Q_RANK, MLA_KV_RANK = 384, 256
ROPE_THETA = 10000.0
MEM_HEADS, MEM_HEAD_DIM = 4, 256
D_FF = 4096
IN_WIDTH = 2216
LN_EPS, RMS_EPS = 1e-5, 1e-6
ALPHA = 2.0 ** 0.25
ADAM_LR, ADAM_B1, ADAM_B2, ADAM_EPS, ADAM_WD, ADAM_STEP = 0.001, 0.9, 0.999, 1e-08, 0.01, 10

LANES = 128
NEG = -1e30
VMEM_LIMIT = 56 * 1024 * 1024

PQ, PZ, PX, PKV, PDT, PKR, PW = 0, 512, 1024, 2048, 2304, 2432, 2560
KR_LANE = 64


def _cparams(sem):
    return pltpu.CompilerParams(dimension_semantics=sem, vmem_limit_bytes=VMEM_LIMIT)


def _sigmoid(x):
    return 1.0 / (1.0 + jnp.exp(-x))


def _mm(a, b, mode, name, *, tm=512, tn=None, tk=None, out_dtypes=(F32,), epi=None, extras=()):
    if mode == "nn":
        (M, K), (K2, N) = a.shape, b.shape
    elif mode == "nt":
        (M, K), (N, K2) = a.shape, b.shape
    else:
        (K, M), (K2, N) = a.shape, b.shape
    assert K == K2, (name, a.shape, b.shape)
    tm, tn, tk = min(tm, M), min(tn or N, N), min(tk or K, K)
    assert M % tm == 0 and N % tn == 0 and K % tk == 0, (name, M, N, K, tm, tn, tk)
    gk = K // tk
    a_spec = pl.BlockSpec((tk, tm), lambda i, j, k: (k, i)) if mode == "tn" else pl.BlockSpec((tm, tk), lambda i, j, k: (i, k))
    b_spec = pl.BlockSpec((tn, tk), lambda i, j, k: (j, k)) if mode == "nt" else pl.BlockSpec((tk, tn), lambda i, j, k: (k, j))
    dims = {"nn": ((1,), (0,)), "nt": ((1,), (1,)), "tn": ((0,), (0,))}[mode]
    ex_specs = []
    for arr, kind in extras:
        if kind == "mn":
            ex_specs.append(pl.BlockSpec((tm, tn), lambda i, j, k: (i, j)))
        elif kind == "n":
            ex_specs.append(pl.BlockSpec((1, tn), lambda i, j, k: (0, j)))
        else:
            ex_specs.append(pl.BlockSpec((tm, arr.shape[1]), lambda i, j, k: (i, 0)))
    ne, no = len(extras), len(out_dtypes)

    def body(*refs):
        a_ref, b_ref = refs[0], refs[1]
        ex, outs = refs[2:2 + ne], refs[2 + ne:2 + ne + no]
        part = lax.dot_general(a_ref[...].astype(BF16), b_ref[...].astype(BF16), (dims, ((), ())),
                               preferred_element_type=F32)

        def finish(acc):
            res = epi(acc, *[e[...] for e in ex]) if epi is not None else (acc,)
            for o, r in zip(outs, res):
                o[...] = r.astype(o.dtype)

        if gk == 1:
            finish(part)
        else:
            acc_ref = refs[-1]
            k = pl.program_id(2)

            @pl.when(k == 0)
            def _():
                acc_ref[...] = part

            @pl.when(k > 0)
            def _():
                acc_ref[...] += part

            @pl.when(k == gk - 1)
            def _():
                finish(acc_ref[...])

    res = pl.pallas_call(
        body, name=name, grid=(M // tm, N // tn, gk),
        in_specs=[a_spec, b_spec] + ex_specs,
        out_specs=[pl.BlockSpec((tm, tn), lambda i, j, k: (i, j)) for _ in out_dtypes],
        out_shape=[jax.ShapeDtypeStruct((M, N), dt) for dt in out_dtypes],
        scratch_shapes=[pltpu.VMEM((tm, tn), F32)] if gk > 1 else [],
        compiler_params=_cparams(("parallel", "parallel", "arbitrary")),
    )(a, b, *[e[0] for e in extras])
    return res[0] if no == 1 else res


def _ln_stats(r):
    mu = jnp.mean(r, axis=-1, keepdims=True)
    xc = r - mu
    var = jnp.mean(xc * xc, axis=-1, keepdims=True)
    rstd = lax.rsqrt(var + LN_EPS)
    return xc * rstd, rstd


def _ln_fwd(terms, g, b, name, tm=512):
    S, D = terms[0][0].shape
    coefs = [c for _, c in terms]
    nt = len(terms)

    def body(*refs):
        r = sum(c * t[...] for t, c in zip(refs[:nt], coefs))
        xh, _ = _ln_stats(r)
        refs[nt + 2][...] = xh * refs[nt][...] + refs[nt + 1][...]

    row = pl.BlockSpec((tm, D), lambda i: (i, 0))
    vec = pl.BlockSpec((1, D), lambda i: (0, 0))
    return pl.pallas_call(
        body, name=name, grid=(S // tm,), in_specs=[row] * nt + [vec, vec], out_specs=row,
        out_shape=jax.ShapeDtypeStruct((S, D), F32), compiler_params=_cparams(("parallel",)),
    )(*[t for t, _ in terms], g, b)


def _ln_bwd(terms, dterms, g, name, tm=512):
    S, D = terms[0][0].shape
    coefs, dcoefs = [c for _, c in terms], [c for _, c in dterms]
    nt, nd = len(terms), len(dterms)

    def body(*refs):
        i = pl.program_id(0)
        r = sum(c * t[...] for t, c in zip(refs[:nt], coefs))
        dh = sum(c * t[...].astype(F32) for t, c in zip(refs[nt:nt + nd], dcoefs))
        g_ref = refs[nt + nd]
        dr_ref, dg_ref, db_ref = refs[nt + nd + 1:]
        xh, rstd = _ln_stats(r)
        dxh = dh * g_ref[...]
        m1 = jnp.mean(dxh, axis=-1, keepdims=True)
        m2 = jnp.mean(dxh * xh, axis=-1, keepdims=True)
        dr_ref[...] = rstd * (dxh - m1 - xh * m2)
        pg = jnp.sum(dh * xh, axis=0, keepdims=True)
        pb = jnp.sum(dh, axis=0, keepdims=True)

        @pl.when(i == 0)
        def _():
            dg_ref[...] = pg
            db_ref[...] = pb

        @pl.when(i > 0)
        def _():
            dg_ref[...] += pg
            db_ref[...] += pb

    row = pl.BlockSpec((tm, D), lambda i: (i, 0))
    vec = pl.BlockSpec((1, D), lambda i: (0, 0))
    return pl.pallas_call(
        body, name=name, grid=(S // tm,), in_specs=[row] * (nt + nd) + [vec], out_specs=[row, vec, vec],
        out_shape=[jax.ShapeDtypeStruct((S, D), F32), jax.ShapeDtypeStruct((1, D), F32), jax.ShapeDtypeStruct((1, D), F32)],
        compiler_params=_cparams(("arbitrary",)),
    )(*[t for t, _ in terms], *[t for t, _ in dterms], g)


def _ln_loss_bwd(terms, g, b, tgt, name, tm=512):
    S, D = terms[0][0].shape
    coefs = [c for _, c in terms]
    nt = len(terms)

    def body(*refs):
        i = pl.program_id(0)
        r = sum(c * t[...] for t, c in zip(refs[:nt], coefs))
        g_ref, b_ref, t_ref = refs[nt:nt + 3]
        loss_ref, dr_ref, dg_ref, db_ref = refs[nt + 3:]
        xh, rstd = _ln_stats(r)
        h = xh * g_ref[...] + b_ref[...]
        diff = h - t_ref[...]
        pl_ = 0.5 * jnp.sum(jnp.mean(diff * diff, axis=-1, keepdims=True), axis=0, keepdims=True)
        dh = diff * (1.0 / D)
        dxh = dh * g_ref[...]
        m1 = jnp.mean(dxh, axis=-1, keepdims=True)
        m2 = jnp.mean(dxh * xh, axis=-1, keepdims=True)
        dr_ref[...] = rstd * (dxh - m1 - xh * m2)
        pg = jnp.sum(dh * xh, axis=0, keepdims=True)
        pb = jnp.sum(dh, axis=0, keepdims=True)
        plb = jnp.broadcast_to(pl_, (1, LANES))

        @pl.when(i == 0)
        def _():
            dg_ref[...] = pg
            db_ref[...] = pb
            loss_ref[...] = plb

        @pl.when(i > 0)
        def _():
            dg_ref[...] += pg
            db_ref[...] += pb
            loss_ref[...] += plb

    row = pl.BlockSpec((tm, D), lambda i: (i, 0))
    vec = pl.BlockSpec((1, D), lambda i: (0, 0))
    lvec = pl.BlockSpec((1, LANES), lambda i: (0, 0))
    return pl.pallas_call(
        body, name=name, grid=(S // tm,), in_specs=[row] * nt + [vec, vec, row], out_specs=[lvec, row, vec, vec],
        out_shape=[jax.ShapeDtypeStruct((1, LANES), F32), jax.ShapeDtypeStruct((S, D), F32),
                   jax.ShapeDtypeStruct((1, D), F32), jax.ShapeDtypeStruct((1, D), F32)],
        compiler_params=_cparams(("arbitrary",)),
    )(*[t for t, _ in terms], g, b, tgt)


def _rope_tables(positions):
    half = MLA_ROPE // 2
    inv_freq = jnp.power(ROPE_THETA, -jnp.arange(half, dtype=F32) / half)
    ang = positions.astype(F32)[:, None] * inv_freq
    cos, sin = jnp.cos(ang), jnp.sin(ang)
    S = positions.shape[0]
    one, zero = jnp.ones((S, MLA_NOPE), F32), jnp.zeros((S, half), F32)
    pad = jnp.zeros((S, LANES - MLA_QK), F32)
    c = jnp.concatenate([one, cos, cos, pad], axis=1)
    s1 = jnp.concatenate([0 * one, -sin, zero, pad], axis=1)
    s2 = jnp.concatenate([0 * one, zero, sin, pad], axis=1)
    return c, s1, s2


def _rope_block(x, c, s1, s2):
    half = MLA_ROPE // 2
    return x * c + pltpu.roll(x, LANES - half, axis=1) * s1 + pltpu.roll(x, half, axis=1) * s2


def _rms_fwd(x, g):
    r = lax.rsqrt(jnp.mean(x * x, axis=-1, keepdims=True) + RMS_EPS)
    return x * r * g


def _rms_bwd(x, g, dy):
    r = lax.rsqrt(jnp.mean(x * x, axis=-1, keepdims=True) + RMS_EPS)
    xh = x * r
    dyh = dy * g
    dx = r * (dyh - xh * jnp.mean(dyh * xh, axis=-1, keepdims=True))
    return dx, jnp.sum(dy * xh, axis=0, keepdims=True)


def _mla_prep(proj, qg, kvg, tabs, tm=512):
    S = proj.shape[0]

    def body(ql_ref, kvl_ref, kr_ref, qg_ref, kvg_ref, c_ref, s1_ref, s2_ref, qn_ref, kvn_ref, kpe_ref):
        qn_ref[...] = _rms_fwd(ql_ref[...], qg_ref[...]).astype(BF16)
        kvn_ref[...] = _rms_fwd(kvl_ref[...], kvg_ref[...]).astype(BF16)
        kpe_ref[...] = _rope_block(kr_ref[...], c_ref[...], s1_ref[...], s2_ref[...])

    tab = pl.BlockSpec((tm, LANES), lambda i: (i, 0))
    return pl.pallas_call(
        body, name="mla_prep", grid=(S // tm,),
        in_specs=[pl.BlockSpec((tm, MLA_Q_RANK), lambda i: (i, PQ // MLA_Q_RANK)),
                  pl.BlockSpec((tm, MLA_KV_RANK), lambda i: (i, PKV // MLA_KV_RANK)),
                  pl.BlockSpec((tm, LANES), lambda i: (i, PKR // LANES)),
                  pl.BlockSpec((1, MLA_Q_RANK), lambda i: (0, 0)), pl.BlockSpec((1, MLA_KV_RANK), lambda i: (0, 0)),
                  tab, tab, tab],
        out_specs=[pl.BlockSpec((tm, MLA_Q_RANK), lambda i: (i, 0)), pl.BlockSpec((tm, MLA_KV_RANK), lambda i: (i, 0)), tab],
        out_shape=[jax.ShapeDtypeStruct((S, MLA_Q_RANK), BF16), jax.ShapeDtypeStruct((S, MLA_KV_RANK), BF16),
                   jax.ShapeDtypeStruct((S, LANES), F32)],
        compiler_params=_cparams(("parallel",)),
    )(proj, proj, proj, qg, kvg, *tabs)


def _mla_prep_bwd(proj, qg, kvg, tabs, dqn, dkvn, dk_all, tm=512):
    S = proj.shape[0]

    def body(ql_ref, kvl_ref, qg_ref, kvg_ref, c_ref, s1_ref, s2_ref, dqn_ref, dkvn_ref, dk_ref,
             dql_ref, dkvl_ref, dkr_ref, dqg_ref, dkvg_ref):
        i = pl.program_id(0)
        dql, pq = _rms_bwd(ql_ref[...], qg_ref[...], dqn_ref[...])
        dkvl, pkv = _rms_bwd(kvl_ref[...], kvg_ref[...], dkvn_ref[...])
        dql_ref[...] = dql
        dkvl_ref[...] = dkvl
        dk = dk_ref[...]
        dkpe = dk[:, 0:LANES]
        for h in range(1, MLA_HEADS):
            dkpe = dkpe + dk[:, h * LANES:(h + 1) * LANES]
        lane = lax.broadcasted_iota(jnp.int32, dkpe.shape, 1)
        dkpe = jnp.where((lane >= KR_LANE) & (lane < KR_LANE + MLA_ROPE), dkpe, 0.0)
        dkr_ref[...] = _rope_block(dkpe, c_ref[...], -s1_ref[...], -s2_ref[...])

        @pl.when(i == 0)
        def _():
            dqg_ref[...] = pq
            dkvg_ref[...] = pkv

        @pl.when(i > 0)
        def _():
            dqg_ref[...] += pq
            dkvg_ref[...] += pkv

    tab = pl.BlockSpec((tm, LANES), lambda i: (i, 0))
    qspec = pl.BlockSpec((tm, MLA_Q_RANK), lambda i: (i, 0))
    kvspec = pl.BlockSpec((tm, MLA_KV_RANK), lambda i: (i, 0))
    qv, kvv = pl.BlockSpec((1, MLA_Q_RANK), lambda i: (0, 0)), pl.BlockSpec((1, MLA_KV_RANK), lambda i: (0, 0))
    return pl.pallas_call(
        body, name="mla_prep_bwd", grid=(S // tm,),
        in_specs=[pl.BlockSpec((tm, MLA_Q_RANK), lambda i: (i, PQ // MLA_Q_RANK)),
                  pl.BlockSpec((tm, MLA_KV_RANK), lambda i: (i, PKV // MLA_KV_RANK)),
                  qv, kvv, tab, tab, tab, qspec, kvspec, pl.BlockSpec((tm, MLA_HEADS * LANES), lambda i: (i, 0))],
        out_specs=[qspec, kvspec, tab, qv, kvv],
        out_shape=[jax.ShapeDtypeStruct((S, MLA_Q_RANK), F32), jax.ShapeDtypeStruct((S, MLA_KV_RANK), F32),
                   jax.ShapeDtypeStruct((S, LANES), F32), jax.ShapeDtypeStruct((1, MLA_Q_RANK), F32),
                   jax.ShapeDtypeStruct((1, MLA_KV_RANK), F32)],
        compiler_params=_cparams(("arbitrary",)),
    )(proj, proj, qg, kvg, *tabs, dqn, dkvn, dk_all)


def _rope_bwd_all(dq_all, tabs, tm=512):
    S, W = dq_all.shape

    def body(dq_ref, c_ref, s1_ref, s2_ref, o_ref):
        c, s1, s2 = c_ref[...], -s1_ref[...], -s2_ref[...]
        for h in range(W // LANES):
            o_ref[:, h * LANES:(h + 1) * LANES] = _rope_block(dq_ref[:, h * LANES:(h + 1) * LANES], c, s1, s2).astype(BF16)

    tab = pl.BlockSpec((tm, LANES), lambda i: (i, 0))
    row = pl.BlockSpec((tm, W), lambda i: (i, 0))
    return pl.pallas_call(body, name="rope_bwd", grid=(S // tm,), in_specs=[row, tab, tab, tab], out_specs=row,
                          out_shape=jax.ShapeDtypeStruct((S, W), BF16), compiler_params=_cparams(("parallel",)))(dq_all, *tabs)


ATT_SCALE = MLA_QK ** -0.5
N_PAIR = MLA_HEADS // 2


def _causal_mask(qi, ki, tq, tk):
    row = qi * tq + lax.broadcasted_iota(jnp.int32, (tq, tk), 0)
    col = ki * tk + lax.broadcasted_iota(jnp.int32, (tq, tk), 1)
    return col <= row


def _attn_fwd(q_all, kv_all, tq=512, tk=512):
    S = q_all.shape[0]
    nq, nk = S // tq, S // tk
    assert tq == tk

    def body(q_ref, k_ref, v_ref, o_ref, lse_ref, m_s, l_s, acc_s):
        qi, ki = pl.program_id(1), pl.program_id(2)

        @pl.when(ki == 0)
        def _():
            m_s[...] = jnp.full(m_s.shape, NEG, F32)
            l_s[...] = jnp.zeros(l_s.shape, F32)
            acc_s[...] = jnp.zeros(acc_s.shape, F32)

        @pl.when(ki <= qi)
        def _():
            v = v_ref[...]
            mask = _causal_mask(qi, ki, tq, tk)
            for hh in range(2):
                q = q_ref[:, hh * LANES:(hh + 1) * LANES]
                k = k_ref[:, hh * LANES:(hh + 1) * LANES]
                s = lax.dot_general(q, k, (((1,), (1,)), ((), ())), preferred_element_type=F32) * ATT_SCALE
                s = jnp.where(mask, s, NEG)
                m_prev = m_s[hh]
                m_new = jnp.maximum(m_prev, jnp.max(s, axis=-1, keepdims=True))
                p = jnp.exp(s - m_new)
                alpha = jnp.exp(m_prev - m_new)
                l_s[hh] = alpha * l_s[hh] + jnp.sum(p, axis=-1, keepdims=True)
                acc_s[hh] = alpha * acc_s[hh] + jnp.dot(p.astype(BF16), v, preferred_element_type=F32)
                m_s[hh] = m_new

        @pl.when(ki == qi)
        def _():
            lane = lax.broadcasted_iota(jnp.int32, (tq, LANES), 1)
            first = lane < MLA_V
            o_ref[...] = jnp.where(first, acc_s[0] / l_s[0], acc_s[1] / l_s[1])
            lse_ref[...] = jnp.where(first, m_s[0] + jnp.log(l_s[0]), m_s[1] + jnp.log(l_s[1]))

    out = pl.BlockSpec((tq, LANES), lambda p, qi, ki: (qi, p))
    return pl.pallas_call(
        body, name="mla_attn_fwd", grid=(N_PAIR, nq, nk),
        in_specs=[pl.BlockSpec((tq, 2 * LANES), lambda p, qi, ki: (qi, p)),
                  pl.BlockSpec((tk, 2 * LANES), lambda p, qi, ki: (jnp.minimum(ki, qi), p)),
                  pl.BlockSpec((tk, LANES), lambda p, qi, ki: (jnp.minimum(ki, qi), MLA_HEADS + p))],
        out_specs=[out, out],
        out_shape=[jax.ShapeDtypeStruct((S, MLA_HEADS * MLA_V), F32)] * 2,
        scratch_shapes=[pltpu.VMEM((2, tq, 1), F32), pltpu.VMEM((2, tq, 1), F32), pltpu.VMEM((2, tq, LANES), F32)],
        compiler_params=_cparams(("parallel", "parallel", "arbitrary")),
    )(q_all, kv_all, kv_all)


def _attn_p_ds(q, k, v, do, lse_h, delta_h, half_mask, mask):
    s = lax.dot_general(q, k, (((1,), (1,)), ((), ())), preferred_element_type=F32) * ATT_SCALE
    p = jnp.exp(jnp.where(mask, s, NEG) - lse_h)
    do_h = jnp.where(half_mask, do, 0.0).astype(BF16)
    dp = lax.dot_general(do_h, v, (((1,), (1,)), ((), ())), preferred_element_type=F32)
    ds = p * (dp - delta_h) * ATT_SCALE
    return p, ds


def _attn_deltas(do, o, first):
    prod = do * o
    return (jnp.sum(jnp.where(first, prod, 0.0), axis=-1, keepdims=True),
            jnp.sum(jnp.where(first, 0.0, prod), axis=-1, keepdims=True))


def _attn_bwd_dq(q_all, kv_all, dcat, o, lse, tq=512, tk=512):
    S = q_all.shape[0]
    nq, nk = S // tq, S // tk

    def body(q_ref, k_ref, v_ref, do_ref, o_ref, lse_ref, dq_ref, acc_s):
        qi, ki = pl.program_id(1), pl.program_id(2)

        @pl.when(ki == 0)
        def _():
            acc_s[...] = jnp.zeros(acc_s.shape, F32)

        @pl.when(ki <= qi)
        def _():
            v, do, lse_t = v_ref[...], do_ref[...], lse_ref[...]
            first = lax.broadcasted_iota(jnp.int32, (tq, LANES), 1) < MLA_V
            deltas = _attn_deltas(do, o_ref[...], first)
            mask = _causal_mask(qi, ki, tq, tk)
            for hh in range(2):
                k = k_ref[:, hh * LANES:(hh + 1) * LANES]
                _, ds = _attn_p_ds(q_ref[:, hh * LANES:(hh + 1) * LANES], k, v, do,
                                   lse_t[:, hh * MLA_V:hh * MLA_V + 1], deltas[hh], first if hh == 0 else ~first, mask)
                acc_s[:, hh * LANES:(hh + 1) * LANES] += jnp.dot(ds.astype(BF16), k, preferred_element_type=F32)

        @pl.when(ki == qi)
        def _():
            dq_ref[...] = acc_s[...]

    half = pl.BlockSpec((tq, LANES), lambda p, qi, ki: (qi, p))
    return pl.pallas_call(
        body, name="mla_attn_bwd_dq", grid=(N_PAIR, nq, nk),
        in_specs=[pl.BlockSpec((tq, 2 * LANES), lambda p, qi, ki: (qi, p)),
                  pl.BlockSpec((tk, 2 * LANES), lambda p, qi, ki: (jnp.minimum(ki, qi), p)),
                  pl.BlockSpec((tk, LANES), lambda p, qi, ki: (jnp.minimum(ki, qi), MLA_HEADS + p)),
                  pl.BlockSpec((tq, LANES), lambda p, qi, ki: (qi, N_PAIR + p)), half, half],
        out_specs=pl.BlockSpec((tq, 2 * LANES), lambda p, qi, ki: (qi, p)),
        out_shape=jax.ShapeDtypeStruct((S, MLA_HEADS * LANES), F32),
        scratch_shapes=[pltpu.VMEM((tq, 2 * LANES), F32)],
        compiler_params=_cparams(("parallel", "parallel", "arbitrary")),
    )(q_all, kv_all, kv_all, dcat, o, lse)


def _attn_bwd_dkv(q_all, kv_all, dcat, o, lse, tq=512, tk=512):
    S = q_all.shape[0]
    nq, nk = S // tq, S // tk

    def body(q_ref, k_ref, v_ref, do_ref, o_ref, lse_ref, dk_ref, dv_ref, dk_s, dv_s):
        ki, qi = pl.program_id(1), pl.program_id(2)

        @pl.when(qi == 0)
        def _():
            dk_s[...] = jnp.zeros(dk_s.shape, F32)
            dv_s[...] = jnp.zeros(dv_s.shape, F32)

        @pl.when(qi >= ki)
        def _():
            v, do, lse_t = v_ref[...], do_ref[...], lse_ref[...]
            first = lax.broadcasted_iota(jnp.int32, (tq, LANES), 1) < MLA_V
            firstk = lax.broadcasted_iota(jnp.int32, (tk, LANES), 1) < MLA_V
            deltas = _attn_deltas(do, o_ref[...], first)
            mask = _causal_mask(qi, ki, tq, tk)
            do_b = do.astype(BF16)
            for hh in range(2):
                q = q_ref[:, hh * LANES:(hh + 1) * LANES]
                p, ds = _attn_p_ds(q, k_ref[:, hh * LANES:(hh + 1) * LANES], v, do,
                                   lse_t[:, hh * MLA_V:hh * MLA_V + 1], deltas[hh], first if hh == 0 else ~first, mask)
                pv = lax.dot_general(p.astype(BF16), do_b, (((0,), (0,)), ((), ())), preferred_element_type=F32)
                dv_s[...] += jnp.where(firstk if hh == 0 else ~firstk, pv, 0.0)
                dk_s[:, hh * LANES:(hh + 1) * LANES] += lax.dot_general(ds.astype(BF16), q, (((0,), (0,)), ((), ())),
                                                                       preferred_element_type=F32)

        @pl.when(qi == nq - 1)
        def _():
            dk_ref[...] = dk_s[...]
            dv_ref[...] = dv_s[...]

    half = pl.BlockSpec((tq, LANES), lambda p, ki, qi: (jnp.maximum(qi, ki), p))
    return pl.pallas_call(
        body, name="mla_attn_bwd_dkv", grid=(N_PAIR, nk, nq),
        in_specs=[pl.BlockSpec((tq, 2 * LANES), lambda p, ki, qi: (jnp.maximum(qi, ki), p)),
                  pl.BlockSpec((tk, 2 * LANES), lambda p, ki, qi: (ki, p)),
                  pl.BlockSpec((tk, LANES), lambda p, ki, qi: (ki, MLA_HEADS + p)),
                  pl.BlockSpec((tq, LANES), lambda p, ki, qi: (jnp.maximum(qi, ki), N_PAIR + p)), half, half],
        out_specs=[pl.BlockSpec((tk, 2 * LANES), lambda p, ki, qi: (ki, p)), pl.BlockSpec((tk, LANES), lambda p, ki, qi: (ki, p))],
        out_shape=[jax.ShapeDtypeStruct((S, MLA_HEADS * LANES), F32), jax.ShapeDtypeStruct((S, MLA_HEADS * MLA_V), F32)],
        scratch_shapes=[pltpu.VMEM((tk, 2 * LANES), F32), pltpu.VMEM((tk, LANES), F32)],
        compiler_params=_cparams(("parallel", "parallel", "arbitrary")),
    )(q_all, kv_all, kv_all, dcat, o, lse)


MEM_SCALE = MEM_HEAD_DIM ** -0.5


def _mem_probs(q, k):
    s = lax.dot_general(q, k, (((1,), (1,)), ((), ())), preferred_element_type=F32) * MEM_SCALE
    e = jnp.exp(s - jnp.max(s, axis=-1, keepdims=True))
    return e / jnp.sum(e, axis=-1, keepdims=True)


def _mem_attn_fwd(qm, km, vm, tq=512):
    S, W = qm.shape
    M = km.shape[0]

    def body(q_ref, k_ref, v_ref, o_ref):
        for h in range(MEM_HEADS):
            sl = slice(h * MEM_HEAD_DIM, (h + 1) * MEM_HEAD_DIM)
            p = _mem_probs(q_ref[:, sl], k_ref[:, sl])
            o_ref[:, sl] = jnp.dot(p.astype(BF16), v_ref[:, sl], preferred_element_type=F32).astype(BF16)

    row = pl.BlockSpec((tq, W), lambda i: (i, 0))
    full = pl.BlockSpec((M, W), lambda i: (0, 0))
    return pl.pallas_call(body, name="mem_attn_fwd", grid=(S // tq,), in_specs=[row, full, full], out_specs=row,
                          out_shape=jax.ShapeDtypeStruct((S, W), BF16), compiler_params=_cparams(("parallel",)))(qm, km, vm)


def _mem_attn_bwd(qm, km, vm, dom, tq=512):
    S, W = qm.shape
    M = km.shape[0]

    def body(q_ref, k_ref, v_ref, do_ref, dq_ref, dk_ref, dv_ref):
        i = pl.program_id(0)

        @pl.when(i == 0)
        def _():
            dk_ref[...] = jnp.zeros(dk_ref.shape, F32)
            dv_ref[...] = jnp.zeros(dv_ref.shape, F32)

        for h in range(MEM_HEADS):
            sl = slice(h * MEM_HEAD_DIM, (h + 1) * MEM_HEAD_DIM)
            q, k, v, do = q_ref[:, sl], k_ref[:, sl], v_ref[:, sl], do_ref[:, sl]
            p = _mem_probs(q, k)
            dv_ref[:, sl] += lax.dot_general(p.astype(BF16), do, (((0,), (0,)), ((), ())), preferred_element_type=F32)
            dp = lax.dot_general(do, v, (((1,), (1,)), ((), ())), preferred_element_type=F32)
            ds = (p * (dp - jnp.sum(dp * p, axis=-1, keepdims=True)) * MEM_SCALE).astype(BF16)
            dq_ref[:, sl] = jnp.dot(ds, k, preferred_element_type=F32).astype(BF16)
            dk_ref[:, sl] += lax.dot_general(ds, q, (((0,), (0,)), ((), ())), preferred_element_type=F32)

    row = pl.BlockSpec((tq, W), lambda i: (i, 0))
    full = pl.BlockSpec((M, W), lambda i: (0, 0))
    return pl.pallas_call(
        body, name="mem_attn_bwd", grid=(S // tq,), in_specs=[row, full, full, row], out_specs=[row, full, full],
        out_shape=[jax.ShapeDtypeStruct((S, W), BF16), jax.ShapeDtypeStruct((M, W), F32), jax.ShapeDtypeStruct((M, W), F32)],
        compiler_params=_cparams(("arbitrary",)),
    )(qm, km, vm, dom)


L = SSD_CHUNK
N_SPAIR = SSD_HEADS // 2
GRP_W = SSD_INNER // 2
XB0, XC0 = SSD_INNER, SSD_INNER + 2 * SSD_STATE


def _cumsum_rows(a, reverse=False):
    row = lax.broadcasted_iota(jnp.int32, a.shape, 0)
    x, sft = a, 1
    while sft < L:
        if reverse:
            x = x + jnp.where(row < L - sft, pltpu.roll(x, L - sft, axis=0), 0.0)
        else:
            x = x + jnp.where(row >= sft, pltpu.roll(x, sft, axis=0), 0.0)
        sft *= 2
    return x


def _shift_down(cur, prev, s):
    if s == 0:
        return cur
    row = lax.broadcasted_iota(jnp.int32, cur.shape, 0)
    return jnp.where(row < s, pltpu.roll(prev, s, axis=0), pltpu.roll(cur, s, axis=0))


def _shift_up(cur, nxt, s):
    if s == 0:
        return cur
    row = lax.broadcasted_iota(jnp.int32, cur.shape, 0)
    return jnp.where(row >= L - s, pltpu.roll(nxt, L - s, axis=0), pltpu.roll(cur, L - s, axis=0))


def _ssd_conv(u, prev, cw, cb):
    conv = cb + cw[SSD_CONV - 1:SSD_CONV, :] * u
    for s in range(1, SSD_CONV):
        conv = conv + cw[SSD_CONV - 1 - s:SSD_CONV - s, :] * _shift_down(u, prev, s)
    return conv


def _pair_lanes(v, h0, first):
    return jnp.where(first, v[:, h0:h0 + 1], v[:, h0 + 1:h0 + 2])


def _ssd_common(u, prev, dt_raw, cw, cb, dtb, alog):
    conv = _ssd_conv(u, prev, cw, cb)
    sg = _sigmoid(conv)
    xa = conv * sg
    dpre = dt_raw + dtb
    dtv = jnp.maximum(dpre, 0.0) + jnp.log1p(jnp.exp(-jnp.abs(dpre)))
    a_row = -jnp.exp(alog)
    cs = _cumsum_rows(dtv * a_row)
    return conv, sg, xa, dpre, dtv, a_row, cs


def _ssd_pair_fwd(xa, dtv, cs, csT, G, Cg, Bg, Sp, dsk, pp, first, tri, rowfirst):
    h0 = 2 * pp
    x = xa[:, pp * LANES:(pp + 1) * LANES]
    xdt = x * _pair_lanes(dtv, h0, first)
    xdt_b = xdt.astype(BF16)
    Ms, yd = [], []
    for h in (h0, h0 + 1):
        lam = jnp.exp(jnp.where(tri, cs[:, h:h + 1] - csT[h:h + 1, :], NEG))
        M = G * lam
        Ms.append((M, lam))
        yd.append(jnp.dot(M.astype(BF16), xdt_b, preferred_element_type=F32))
    T = lax.dot_general(Cg, Sp.astype(BF16), (((1,), (1,)), ((), ())), preferred_element_type=F32)
    E = jnp.exp(_pair_lanes(cs, h0, first))
    yoff = E * T
    csl = cs[L - 1:L, :]
    Fd = jnp.exp(_pair_lanes(csl, h0, first) - _pair_lanes(cs, h0, first))
    el = jnp.exp(csl)
    el_rows = jnp.where(rowfirst, el[:, h0:h0 + 1], el[:, h0 + 1:h0 + 2])
    Sloc = lax.dot_general((xdt * Fd).astype(BF16), Bg, (((0,), (0,)), ((), ())), preferred_element_type=F32)
    S_new = el_rows * Sp + Sloc
    y = jnp.where(first, yd[0], yd[1]) + yoff + x * _pair_lanes(dsk, h0, first[:1])
    return y, S_new, (x, xdt, xdt_b, Ms, E, yoff, Fd, el, el_rows)


def _ssd_masks():
    lane = lax.broadcasted_iota(jnp.int32, (L, LANES), 1)
    row = lax.broadcasted_iota(jnp.int32, (L, LANES), 0)
    return lane, row, lane < SSD_HEAD_DIM, row >= lane, row[:, :1] < SSD_HEAD_DIM


def _ssd_specs(nc, rev):
    def cidx(i):
        return nc - 1 - i if rev else i
    z = pl.BlockSpec((L, SSD_INNER), lambda i: (cidx(i), PZ // SSD_INNER))
    u = pl.BlockSpec((L, SSD_XBC), lambda i: (cidx(i), PX // SSD_XBC))
    dt = pl.BlockSpec((L, LANES), lambda i: (cidx(i), PDT // LANES))
    return cidx, z, u, dt


def _vec(w):
    return pl.BlockSpec((1, w), lambda i: (0, 0))


def _ssd_fwd(proj, cw, cb, dtb, alog, dsk, ng):
    S = proj.shape[0]
    nc = S // L

    def body(z_ref, u_ref, dt_ref, cw_ref, cb_ref, dtb_ref, alog_ref, dsk_ref, ng_ref, y_ref, st_ref, prev_s, state_s):
        c = pl.program_id(0)

        @pl.when(c == 0)
        def _():
            prev_s[...] = jnp.zeros(prev_s.shape, F32)
            state_s[...] = jnp.zeros(state_s.shape, F32)

        u = u_ref[...]
        _, _, xa, _, dtv, _, cs = _ssd_common(u, prev_s[...], dt_ref[...], cw_ref[...], cb_ref[...], dtb_ref[...], alog_ref[...])
        prev_s[...] = u
        csT = cs.T
        _, _, first, tri, rowfirst = _ssd_masks()
        dsk_v = dsk_ref[...]
        ys = []
        for g in range(2):
            Bg = xa[:, XB0 + g * SSD_STATE:XB0 + (g + 1) * SSD_STATE].astype(BF16)
            Cg = xa[:, XC0 + g * SSD_STATE:XC0 + (g + 1) * SSD_STATE].astype(BF16)
            G = lax.dot_general(Cg, Bg, (((1,), (1,)), ((), ())), preferred_element_type=F32)
            for pp in (2 * g, 2 * g + 1):
                Sp = state_s[pp]
                st_ref[pp * LANES:(pp + 1) * LANES, :] = Sp
                y, S_new, _ = _ssd_pair_fwd(xa, dtv, cs, csT, G, Cg, Bg, Sp, dsk_v, pp, first, tri, rowfirst)
                state_s[pp] = S_new
                ys.append(y)
        z = z_ref[...]
        for g in range(2):
            yg = jnp.concatenate([ys[2 * g], ys[2 * g + 1]], axis=1)
            zg = z[:, g * GRP_W:(g + 1) * GRP_W]
            gated = yg * (zg * _sigmoid(zg))
            r = lax.rsqrt(jnp.mean(gated * gated, axis=-1, keepdims=True) + RMS_EPS)
            y_ref[:, g * GRP_W:(g + 1) * GRP_W] = gated * r * ng_ref[:, g * GRP_W:(g + 1) * GRP_W]

    _, zs, us, dts = _ssd_specs(nc, False)
    return pl.pallas_call(
        body, name="ssd_fwd", grid=(nc,),
        in_specs=[zs, us, dts, pl.BlockSpec((8, SSD_XBC), lambda i: (0, 0)), _vec(SSD_XBC), _vec(LANES), _vec(LANES), _vec(LANES),
                  _vec(SSD_INNER)],
        out_specs=[pl.BlockSpec((L, SSD_INNER), lambda i: (i, 0)), pl.BlockSpec((N_SPAIR * LANES, SSD_STATE), lambda i: (i, 0))],
        out_shape=[jax.ShapeDtypeStruct((S, SSD_INNER), F32), jax.ShapeDtypeStruct((nc * N_SPAIR * LANES, SSD_STATE), F32)],
        scratch_shapes=[pltpu.VMEM((L, SSD_XBC), F32), pltpu.VMEM((N_SPAIR, LANES, SSD_STATE), F32)],
        compiler_params=_cparams(("arbitrary",)),
    )(proj, proj, proj, cw, cb, dtb, alog, dsk, ng)


def _ssd_bwd(proj, states, dy, cw, cb, dtb, alog, dsk, ng):
    S = proj.shape[0]
    nc = S // L

    def body(z_ref, u_ref, up_ref, dt_ref, st_ref, dy_ref, cw_ref, cb_ref, dtb_ref, alog_ref, dsk_ref, ng_ref,
             dz_ref, du_ref, ddt_ref, dcw_ref, dcb_ref, ddtb_ref, dalog_ref, ddsk_ref, dng_ref,
             dS_s, dconv_s, dD_s):
        i = pl.program_id(0)
        c = nc - 1 - i

        @pl.when(i == 0)
        def _():
            dS_s[...] = jnp.zeros(dS_s.shape, F32)
            dconv_s[...] = jnp.zeros(dconv_s.shape, F32)
            dD_s[...] = jnp.zeros(dD_s.shape, F32)
            for r in (dcw_ref, dcb_ref, ddtb_ref, dalog_ref, ddsk_ref, dng_ref):
                r[...] = jnp.zeros(r.shape, F32)

        u = u_ref[...]
        prev = jnp.where(c > 0, up_ref[...], 0.0)
        cw_v = cw_ref[...]
        conv, sg, xa, dpre, dtv, a_row, cs = _ssd_common(u, prev, dt_ref[...], cw_v, cb_ref[...], dtb_ref[...], alog_ref[...])
        csT = cs.T
        lane, row, first, tri, rowfirst = _ssd_masks()
        dsk_v = dsk_ref[...]

        fw = []
        Gs, Bs, Cs = [], [], []
        for g in range(2):
            Bg = xa[:, XB0 + g * SSD_STATE:XB0 + (g + 1) * SSD_STATE].astype(BF16)
            Cg = xa[:, XC0 + g * SSD_STATE:XC0 + (g + 1) * SSD_STATE].astype(BF16)
            G = lax.dot_general(Cg, Bg, (((1,), (1,)), ((), ())), preferred_element_type=F32)
            Gs.append(G), Bs.append(Bg), Cs.append(Cg)
            for pp in (2 * g, 2 * g + 1):
                Sp = st_ref[pp * LANES:(pp + 1) * LANES, :]
                y, _, keep = _ssd_pair_fwd(xa, dtv, cs, csT, G, Cg, Bg, Sp, dsk_v, pp, first, tri, rowfirst)
                fw.append((y, Sp, keep))

        z = z_ref[...]
        dys = []
        for g in range(2):
            sl = slice(g * GRP_W, (g + 1) * GRP_W)
            yg = jnp.concatenate([fw[2 * g][0], fw[2 * g + 1][0]], axis=1)
            zg = z[:, sl]
            sz = _sigmoid(zg)
            silu_z = zg * sz
            gated = yg * silu_z
            r = lax.rsqrt(jnp.mean(gated * gated, axis=-1, keepdims=True) + RMS_EPS)
            nh = gated * r
            dout = dy_ref[:, sl]
            dng_ref[:, sl] += jnp.sum(dout * nh, axis=0, keepdims=True)
            dnh = dout * ng_ref[:, sl]
            dgated = r * (dnh - nh * jnp.mean(dnh * nh, axis=-1, keepdims=True))
            dz_ref[:, sl] = dgated * yg * (sz * (1.0 + zg * (1.0 - sz)))
            dyg = dgated * silu_z
            dys.append(dyg[:, :LANES]), dys.append(dyg[:, LANES:])

        dcs_c = jnp.zeros((L, LANES), F32)
        dcs_r = jnp.zeros((L, LANES), F32)
        ddt_c = jnp.zeros((L, LANES), F32)
        dxs = []
        dB, dC = [None, None], [None, None]
        last = row == L - 1
        for g in range(2):
            Bg, Cg, G = Bs[g], Cs[g], Gs[g]
            dG = jnp.zeros((L, L), F32)
            dBg = jnp.zeros((L, SSD_STATE), F32)
            dCg = jnp.zeros((L, SSD_STATE), F32)
            for pp in (2 * g, 2 * g + 1):
                h0 = 2 * pp
                y, Sp, (x, xdt, xdt_b, Ms, E, yoff, Fd, el, el_rows) = fw[pp]
                dY = dys[pp]
                dS = dS_s[pp]
                dS_b, Sp_b = dS.astype(BF16), Sp.astype(BF16)
                dD_s[:, pp * LANES:(pp + 1) * LANES] += jnp.sum(dY * x, axis=0, keepdims=True)
                dx = dY * _pair_lanes(dsk_v, h0, first[:1])
                dxdt = jnp.zeros((L, LANES), F32)
                dY_b = dY.astype(BF16)
                for hh, h in enumerate((h0, h0 + 1)):
                    hm = first if hh == 0 else ~first
                    M, lam = Ms[hh]
                    dYh = jnp.where(hm, dY, 0.0).astype(BF16)
                    dM = lax.dot_general(dYh, xdt_b, (((1,), (1,)), ((), ())), preferred_element_type=F32)
                    W = dM * M
                    dcs_c = dcs_c + jnp.where(lane == h, jnp.sum(W, axis=-1, keepdims=True), 0.0)
                    dcs_r = dcs_r + jnp.where(row == h, jnp.sum(W, axis=0, keepdims=True), 0.0)
                    dG = dG + dM * lam
                    mt = lax.dot_general(M.astype(BF16), dY_b, (((0,), (0,)), ((), ())), preferred_element_type=F32)
                    dxdt = dxdt + jnp.where(hm, mt, 0.0)
                dT = (E * dY).astype(BF16)
                dCg = dCg + jnp.dot(dT, Sp_b, preferred_element_type=F32)
                dS_in = lax.dot_general(dT, Cg, (((0,), (0,)), ((), ())), preferred_element_type=F32) + el_rows * dS
                q1 = dY * yoff
                dZ = lax.dot_general(Bg, dS_b, (((1,), (1,)), ((), ())), preferred_element_type=F32)
                dBg = dBg + jnp.dot((xdt * Fd).astype(BF16), dS_b, preferred_element_type=F32)
                dxdt = dxdt + dZ * Fd
                q2 = dZ * xdt * Fd
                dSS = dS * Sp
                for hh, h in enumerate((h0, h0 + 1)):
                    hm = first if hh == 0 else ~first
                    rs1 = jnp.sum(jnp.where(hm, q1, 0.0), axis=-1, keepdims=True)
                    rs2 = jnp.sum(jnp.where(hm, q2, 0.0), axis=-1, keepdims=True)
                    rmask = rowfirst if hh == 0 else ~rowfirst
                    d_el = jnp.sum(jnp.sum(jnp.where(rmask, dSS, 0.0), axis=-1, keepdims=True), axis=0, keepdims=True)
                    tail = jnp.sum(rs2, axis=0, keepdims=True) + d_el * el[:, h:h + 1]
                    dcs_c = dcs_c + jnp.where(lane == h, rs1 - rs2 + jnp.where(last[:, :1], tail, 0.0), 0.0)
                    ddt_c = ddt_c + jnp.where(lane == h, jnp.sum(jnp.where(hm, dxdt * x, 0.0), axis=-1, keepdims=True), 0.0)
                dS_s[pp] = dS_in
                dxs.append(dx + dxdt * _pair_lanes(dtv, h0, first))
            dG_b = dG.astype(BF16)
            dC[g] = dCg + jnp.dot(dG_b, Bg, preferred_element_type=F32)
            dB[g] = dBg + lax.dot_general(dG_b, Cg, (((0,), (0,)), ((), ())), preferred_element_type=F32)

        dcs = dcs_c - dcs_r.T
        da = _cumsum_rows(dcs, reverse=True)
        ddt_c = ddt_c + da * a_row
        dalog_ref[...] += jnp.sum(da * dtv, axis=0, keepdims=True) * a_row
        ddt_raw = ddt_c * _sigmoid(dpre)
        ddt_ref[...] = ddt_raw
        ddtb_ref[...] += jnp.sum(ddt_raw, axis=0, keepdims=True)

        dxa = jnp.concatenate(dxs + dB + dC, axis=1)
        dconv = dxa * (sg * (1.0 + conv * (1.0 - sg)))
        dcb_ref[...] += jnp.sum(dconv, axis=0, keepdims=True)
        nxt = dconv_s[...]
        du = cw_v[SSD_CONV - 1:SSD_CONV, :] * dconv
        dcw_ref[SSD_CONV - 1:SSD_CONV, :] += jnp.sum(dconv * u, axis=0, keepdims=True)
        for s in range(1, SSD_CONV):
            k = SSD_CONV - 1 - s
            du = du + cw_v[k:k + 1, :] * _shift_up(dconv, nxt, s)
            dcw_ref[k:k + 1, :] += jnp.sum(dconv * _shift_down(u, prev, s), axis=0, keepdims=True)
        du_ref[...] = du
        dconv_s[...] = dconv

        @pl.when(i == nc - 1)
        def _():
            acc = dD_s[...]
            lane1 = lax.broadcasted_iota(jnp.int32, (1, LANES), 1)
            lanew = lax.broadcasted_iota(jnp.int32, acc.shape, 1)
            out = jnp.zeros((1, LANES), F32)
            for h in range(SSD_HEADS):
                tot = jnp.sum(jnp.where((lanew >= h * SSD_HEAD_DIM) & (lanew < (h + 1) * SSD_HEAD_DIM), acc, 0.0),
                              axis=-1, keepdims=True)
                out = out + jnp.where(lane1 == h, tot, 0.0)
            ddsk_ref[...] = out

    cidx, zs, us, dts = _ssd_specs(nc, True)
    ups = pl.BlockSpec((L, SSD_XBC), lambda i: (jnp.maximum(cidx(i) - 1, 0), PX // SSD_XBC))
    rowc = lambda w: pl.BlockSpec((L, w), lambda i: (cidx(i), 0))
    return pl.pallas_call(
        body, name="ssd_bwd", grid=(nc,),
        in_specs=[zs, us, ups, dts, pl.BlockSpec((N_SPAIR * LANES, SSD_STATE), lambda i: (cidx(i), 0)), rowc(SSD_INNER),
                  pl.BlockSpec((8, SSD_XBC), lambda i: (0, 0)), _vec(SSD_XBC), _vec(LANES), _vec(LANES), _vec(LANES), _vec(SSD_INNER)],
        out_specs=[rowc(SSD_INNER), rowc(SSD_XBC), rowc(LANES), pl.BlockSpec((8, SSD_XBC), lambda i: (0, 0)), _vec(SSD_XBC),
                   _vec(LANES), _vec(LANES), _vec(LANES), _vec(SSD_INNER)],
        out_shape=[jax.ShapeDtypeStruct((S, SSD_INNER), F32), jax.ShapeDtypeStruct((S, SSD_XBC), F32),
                   jax.ShapeDtypeStruct((S, LANES), F32), jax.ShapeDtypeStruct((8, SSD_XBC), F32),
                   jax.ShapeDtypeStruct((1, SSD_XBC), F32), jax.ShapeDtypeStruct((1, LANES), F32),
                   jax.ShapeDtypeStruct((1, LANES), F32), jax.ShapeDtypeStruct((1, LANES), F32),
                   jax.ShapeDtypeStruct((1, SSD_INNER), F32)],
        scratch_shapes=[pltpu.VMEM((N_SPAIR, LANES, SSD_STATE), F32), pltpu.VMEM((L, SSD_XBC), F32),
                        pltpu.VMEM((1, SSD_INNER), F32)],
        compiler_params=_cparams(("arbitrary",)),
    )(proj, proj, proj, proj, states, dy, cw, cb, dtb, alog, dsk, ng)


_IN_SEGS = ((PZ, 0, 512), (PX, 512, 1024), (PDT, 1536, 8), (PQ, 1544, 384), (PKV, 1928, 256), (PKR + KR_LANE, 2184, 32))


def _pad_w_in(w):
    out = jnp.zeros((w.shape[0], PW), w.dtype)
    for dst, src, n in _IN_SEGS:
        out = lax.dynamic_update_slice(out, w[:, src:src + n], (0, dst))
    return out


def _unpad_w_in(wp):
    segs = sorted(_IN_SEGS, key=lambda t: t[1])
    return jnp.concatenate([wp[:, dst:dst + n] for dst, src, n in segs], axis=1)


def _pad_w_q(w):
    return jnp.pad(w.reshape(MLA_Q_RANK, MLA_HEADS, MLA_QK), ((0, 0), (0, 0), (0, LANES - MLA_QK))).reshape(MLA_Q_RANK, MLA_HEADS * LANES)


def _unpad_w_q(wp):
    return wp.reshape(MLA_Q_RANK, MLA_HEADS, LANES)[:, :, :MLA_QK].reshape(MLA_Q_RANK, MLA_HEADS * MLA_QK)


def _pad_w_kv(w):
    w3 = w.reshape(MLA_KV_RANK, MLA_HEADS, MLA_NOPE + MLA_V)
    k = jnp.pad(w3[:, :, :MLA_NOPE], ((0, 0), (0, 0), (0, LANES - MLA_NOPE))).reshape(MLA_KV_RANK, MLA_HEADS * LANES)
    return jnp.concatenate([k, w3[:, :, MLA_NOPE:].reshape(MLA_KV_RANK, MLA_HEADS * MLA_V)], axis=1)


def _unpad_w_kv(wp):
    k = wp[:, :MLA_HEADS * LANES].reshape(MLA_KV_RANK, MLA_HEADS, LANES)[:, :, :MLA_NOPE]
    v = wp[:, MLA_HEADS * LANES:].reshape(MLA_KV_RANK, MLA_HEADS, MLA_V)
    return jnp.concatenate([k, v], axis=2).reshape(MLA_KV_RANK, MLA_HEADS * (MLA_NOPE + MLA_V))


def _head_lanes(v):
    return jnp.pad(v, ((0, 0), (0, LANES - v.shape[1])))


def _local_step(x, mem, positions, tgt, P):
    tabs = _rope_tables(positions)
    G = {}

    h0 = _ln_fwd([(x, 1.0)], P["ln_in_g"], P["ln_in_b"], "ln_in")
    proj = _mm(h0, P["w_in"], "nn", "proj_in", tm=1024, tn=640)
    y_ssd, states = _ssd_fwd(proj, P["conv_w"], P["conv_b"], P["dt_bias"], P["a_log"], P["d_skip"], P["ssd_norm_g"])
    qn, kvn, kpe = _mla_prep(proj, P["q_norm_g"], P["kv_norm_g"], tabs)

    def q_epi(acc, c, s1, s2):
        return (jnp.concatenate([_rope_block(acc[:, h * LANES:(h + 1) * LANES], c, s1, s2) for h in range(MLA_HEADS)], axis=1),)

    q_all = _mm(qn, P["w_q_up"], "nn", "q_up", out_dtypes=(BF16,), epi=q_epi, extras=[(t, "m") for t in tabs])

    def kv_epi(acc, kp):
        kb = [acc[:, h * LANES:(h + 1) * LANES] + kp for h in range(MLA_HEADS)]
        return (jnp.concatenate(kb + [acc[:, MLA_HEADS * LANES:]], axis=1),)

    kv_all = _mm(kvn, P["w_kv_up"], "nn", "kv_up", out_dtypes=(BF16,), epi=kv_epi, extras=[(kpe, "m")])
    o_att, lse = _attn_fwd(q_all, kv_all)
    cat = jnp.concatenate([y_ssd, o_att], axis=1)
    mix = _mm(cat, P["w_mix_out"], "nn", "mix_out", tm=1024)
    h1 = _ln_fwd([(h0, ALPHA), (mix, 1.0)], P["ln1_g"], P["ln1_b"], "ln1")
    qm = _mm(h1, P["w_mem_q"], "nn", "mem_q", tm=1024, out_dtypes=(BF16,))
    km = _mm(mem, P["w_mem_k"], "nn", "mem_k", out_dtypes=(BF16,))
    vm = _mm(mem, P["w_mem_v"], "nn", "mem_v", out_dtypes=(BF16,))
    om = _mem_attn_fwd(qm, km, vm)
    xa = _mm(om, P["w_mem_o"], "nn", "mem_o", tm=1024)
    h2 = _ln_fwd([(h1, ALPHA), (xa, 1.0)], P["ln2_g"], P["ln2_b"], "ln2")

    def up_epi(acc):
        r = jnp.maximum(acc, 0.0)
        return acc, r * r

    u, act = _mm(h2, P["w_up"], "nn", "mlp_up", tm=1024, tn=1024, out_dtypes=(F32, BF16), epi=up_epi)
    ff = _mm(act, P["w_down"], "nn", "mlp_down", tm=1024, tk=1024)

    loss, dr3, G["ln3_g"], G["ln3_b"] = _ln_loss_bwd([(h2, ALPHA), (ff, 1.0)], P["ln3_g"], P["ln3_b"], tgt, "ln3_loss")

    def dact_epi(acc, uu):
        return (acc * (2.0 * jnp.maximum(uu, 0.0)),)

    du = _mm(dr3, P["w_down"], "nt", "mlp_down_dx", tm=1024, tn=1024, out_dtypes=(BF16,), epi=dact_epi, extras=[(u, "mn")])
    G["w_down"] = _mm(act, dr3, "tn", "mlp_down_dw", tm=1024, tk=512)
    G["w_up"] = _mm(h2, du, "tn", "mlp_up_dw", tm=1024, tn=1024, tk=512)
    dh2_ff = _mm(du, P["w_up"], "nt", "mlp_up_dx", tm=1024, tk=1024)
    dr2, G["ln2_g"], G["ln2_b"] = _ln_bwd([(h1, ALPHA), (xa, 1.0)], [(dr3, ALPHA), (dh2_ff, 1.0)], P["ln2_g"], "ln2_bwd")

    dom = _mm(dr2, P["w_mem_o"], "nt", "mem_o_dx", tm=1024, out_dtypes=(BF16,))
    G["w_mem_o"] = _mm(om, dr2, "tn", "mem_o_dw", tm=1024, tk=512)
    dqm, dkm, dvm = _mem_attn_bwd(qm, km, vm, dom)
    G["w_mem_q"] = _mm(h1, dqm, "tn", "mem_q_dw", tm=1024, tk=512)
    G["w_mem_k"] = _mm(mem, dkm, "tn", "mem_k_dw", tm=1024)
    G["w_mem_v"] = _mm(mem, dvm, "tn", "mem_v_dw", tm=1024)
    dh1_q = _mm(dqm, P["w_mem_q"], "nt", "mem_q_dx", tm=1024)
    dr1, G["ln1_g"], G["ln1_b"] = _ln_bwd([(h0, ALPHA), (mix, 1.0)], [(dr2, ALPHA), (dh1_q, 1.0)], P["ln1_g"], "ln1_bwd")

    dcat = _mm(dr1, P["w_mix_out"], "nt", "mix_out_dx", tm=1024)
    G["w_mix_out"] = _mm(cat, dr1, "tn", "mix_out_dw", tm=1024, tk=512)
    dq_all = _attn_bwd_dq(q_all, kv_all, dcat, o_att, lse)
    dk_all, dv_all = _attn_bwd_dkv(q_all, kv_all, dcat, o_att, lse)
    dq_pre = _rope_bwd_all(dq_all, tabs)
    G["w_q_up"] = _mm(qn, dq_pre, "tn", "q_up_dw", tk=512)
    dqn = _mm(dq_pre, P["w_q_up"], "nt", "q_up_dx", tm=1024)
    dkv_all = jnp.concatenate([dk_all, dv_all], axis=1)
    G["w_kv_up"] = _mm(kvn, dkv_all, "tn", "kv_up_dw", tk=512)
    dkvn = _mm(dkv_all, P["w_kv_up"], "nt", "kv_up_dx", tm=1024)
    dql, dkvl, dkr, G["q_norm_g"], G["kv_norm_g"] = _mla_prep_bwd(proj, P["q_norm_g"], P["kv_norm_g"], tabs, dqn, dkvn, dk_all)
    (dz, dxbc, ddt, G["conv_w"], G["conv_b"], G["dt_bias"], G["a_log"], G["d_skip"], G["ssd_norm_g"]) = _ssd_bwd(
        proj, states, dcat, P["conv_w"], P["conv_b"], P["dt_bias"], P["a_log"], P["d_skip"], P["ssd_norm_g"])
    S = x.shape[0]
    dproj = jnp.concatenate([dql, jnp.zeros((S, PZ - MLA_Q_RANK), F32), dz, dxbc, dkvl, ddt, dkr], axis=1)
    G["w_in"] = _mm(h0, dproj, "tn", "proj_in_dw", tm=1024, tn=640, tk=512)
    dh0_p = _mm(dproj, P["w_in"], "nt", "proj_in_dx", tm=1024, tk=640)
    gx, G["ln_in_g"], G["ln_in_b"] = _ln_bwd([(x, 1.0)], [(dr1, ALPHA), (dh0_p, 1.0)], P["ln_in_g"], "ln_in_bwd")
    return loss, gx, G


PACK_W = 1024
BIG = (("w_in", (1024, 277), 1), ("conv_w", (4, 128), 1), ("w_q_up", (384, 96), 1), ("w_kv_up", (256, 128), 1),
       ("w_mix_out", (128, 1024), 0), ("w_mem_q", (128, 1024), 0), ("w_mem_k", (128, 1024), 0), ("w_mem_v", (128, 1024), 0),
       ("w_mem_o", (128, 1024), 0), ("w_up", (1024, 512), 1), ("w_down", (512, 1024), 0))
SMALL = ("ln_in_g", "ln_in_b", "conv_b", "ln1_g", "ln1_b", "ln2_g", "ln2_b", "ln3_g", "ln3_b",
         "ssd_norm_g", "q_norm_g", "kv_norm_g", "dt_bias", "a_log", "d_skip")
ALL_W = ("ln_in_g", "ln_in_b", "w_in", "conv_w", "conv_b", "dt_bias", "a_log", "d_skip", "ssd_norm_g", "q_norm_g", "w_q_up",
         "kv_norm_g", "w_kv_up", "w_mix_out", "ln1_g", "ln1_b", "w_mem_q", "w_mem_k", "w_mem_v", "w_mem_o", "ln2_g", "ln2_b",
         "w_up", "w_down", "ln3_g", "ln3_b")


def _rows_of(shape):
    return -(-(shape[0] * shape[1]) // PACK_W)


BIG_ROWS = sum(_rows_of(s) for _, s, _ in BIG)
BIG_R = -(-BIG_ROWS // 32) * 32
SMALL_R = 16
LOSS_ROW = 15
ADAM_TILE = 96
assert BIG_R % ADAM_TILE == 0


def _flat_rows(a, rows):
    lead = a.shape[:-2]
    f = a.reshape(lead + (-1,))
    f = jnp.pad(f, [(0, 0)] * len(lead) + [(0, rows * PACK_W - f.shape[-1])])
    return f.reshape(lead + (rows, PACK_W))


def _pack_big(arrs, dtype):
    parts = []
    for n, s, _ in BIG:
        if n == "conv_w" and dtype == BF16:
            parts.append(lax.bitcast_convert_type(arrs[n].astype(F32), BF16).reshape(1, PACK_W))
        else:
            parts.append(_flat_rows(arrs[n].astype(dtype), _rows_of(s)))
    parts.append(jnp.zeros((BIG_R - BIG_ROWS, PACK_W), dtype))
    return jnp.concatenate(parts, axis=0)


def _unpack_big(buf):
    out, r = {}, 0
    for n, s, _ in BIG:
        k = _rows_of(s)
        out[n] = buf[r:r + k].reshape(-1)[:s[0] * s[1]].reshape(s)
        r += k
    return out


def _unpack_gathered(buf):
    out, r = {}, 0
    for n, s, ax in BIG:
        k = _rows_of(s)
        if n == "conv_w":
            sh = lax.bitcast_convert_type(buf[:, r:r + k].reshape((N_DEV,) + s + (2,)), F32)
        else:
            sh = buf[:, r:r + k].reshape(N_DEV, -1)[:, :s[0] * s[1]].reshape((N_DEV,) + s)
        out[n] = sh.reshape(N_DEV * s[0], s[1]) if ax == 0 else sh.transpose(1, 0, 2).reshape(s[0], N_DEV * s[1])
        r += k
    return out


def _pack_grads(G):
    parts = []
    for n, s, ax in BIG:
        g = G[n]
        sh = g.reshape((N_DEV,) + s) if ax == 0 else g.reshape(s[0], N_DEV, s[1]).transpose(1, 0, 2)
        parts.append(_flat_rows(sh, _rows_of(s)))
    parts.append(jnp.zeros((N_DEV, BIG_R - BIG_ROWS, PACK_W), F32))
    return jnp.concatenate(parts, axis=1)


_SMALL_ROWS = (("ln_in_g",), ("ln_in_b",), ("conv_b",), ("ln1_g",), ("ln1_b",), ("ln2_g",), ("ln2_b",), ("ln3_g",), ("ln3_b",),
               ("ssd_norm_g", "q_norm_g"), ("kv_norm_g", "dt_bias", "a_log", "d_skip"))
_SMALL_W = {"ssd_norm_g": 512, "q_norm_g": 384, "kv_norm_g": 256, "dt_bias": LANES, "a_log": LANES, "d_skip": LANES}


def _pack_small(V, extra_row=None):
    rows = []
    for names in _SMALL_ROWS:
        r = jnp.concatenate([V[n] for n in names], axis=1)
        rows.append(jnp.pad(r, ((0, 0), (0, PACK_W - r.shape[1]))))
    rows.append(jnp.zeros((SMALL_R - len(rows) - 1, PACK_W), F32))
    rows.append(jnp.zeros((1, PACK_W), F32) if extra_row is None else extra_row)
    return jnp.concatenate(rows, axis=0)


def _unpack_small(buf):
    out = {}
    for i, names in enumerate(_SMALL_ROWS):
        c = 0
        for n in names:
            w = _SMALL_W.get(n, PACK_W)
            out[n] = buf[i:i + 1, c:c + w]
            c += w
    return out


MESH = pl.DeviceIdType.MESH
ANY = pl.BlockSpec(memory_space=pl.ANY)
VM = pl.BlockSpec(memory_space=pltpu.VMEM)


def _coords():
    return lax.axis_index("x"), lax.axis_index("y"), lax.axis_index("c")


def _slot(px, py, pc):
    return 4 * px + 2 * py + pc


def _gather_big(shard):
    R, W = shard.shape

    def body(x_ref, out_ref, send_sems, recv_sems, local_sem):
        x, y, c = _coords()
        me, sibling = (x, y, c), (x, y, 1 - c)
        chips = [(1 - x, y), (x, 1 - y), (1 - x, 1 - y)]

        def rows(px, py, pc):
            return out_ref.at[_slot(px, py, pc)]

        def copy(k, block, to, src=None):
            return pltpu.make_async_remote_copy(
                src_ref=rows(*block) if src is None else src, dst_ref=rows(*block),
                send_sem=send_sems.at[k], recv_sem=recv_sems.at[k], device_id=to, device_id_type=MESH)

        mine = pltpu.make_async_copy(x_ref, rows(*me), local_sem)
        mine.start()
        first = [copy(0, me, sibling, src=x_ref)]
        first += [copy(1 + j, me, (*chip, c), src=x_ref) for j, chip in enumerate(chips)]
        for cp in first:
            cp.start()
        passed = [copy(4 + j, (*chip, c), sibling) for j, chip in enumerate(chips)]
        for j, chip in enumerate(chips):
            copy(1 + j, (*chip, c), me).wait_recv()
            passed[j].start()
        copy(0, sibling, me).wait_recv()
        for j, chip in enumerate(chips):
            copy(4 + j, (*chip, 1 - c), me).wait_recv()
        for cp in first + passed:
            cp.wait_send()
        mine.wait()

    return pl.pallas_call(
        body, name="gather_weights", out_shape=jax.ShapeDtypeStruct((N_DEV, R, W), shard.dtype),
        in_specs=[ANY], out_specs=ANY,
        scratch_shapes=[pltpu.SemaphoreType.DMA((7,)), pltpu.SemaphoreType.DMA((7,)), pltpu.SemaphoreType.DMA],
    )(shard)


def _peer(k, x, y, c):
    dx, dy, dc = (k >> 2) & 1, (k >> 1) & 1, k & 1
    return (1 - x if dx else x, 1 - y if dy else y, 1 - c if dc else c)


def _scatter_grads(gpack):
    _, R, W = gpack.shape

    def body(g_ref, out_ref, send_sems, recv_sems, local_sem):
        x, y, c = _coords()
        my = _slot(x, y, c)
        mine = pltpu.make_async_copy(g_ref.at[my], out_ref.at[my], local_sem)
        mine.start()
        cps = []
        for k in range(1, N_DEV):
            peer = _peer(k, x, y, c)
            cps.append(pltpu.make_async_remote_copy(
                src_ref=g_ref.at[_slot(*peer)], dst_ref=out_ref.at[my],
                send_sem=send_sems.at[k - 1], recv_sem=recv_sems.at[k - 1], device_id=peer, device_id_type=MESH))
        for cp in cps:
            cp.start()
        for cp in cps:
            cp.wait_recv()
        for cp in cps:
            cp.wait_send()
        mine.wait()

    return pl.pallas_call(
        body, name="scatter_grads", out_shape=jax.ShapeDtypeStruct(gpack.shape, gpack.dtype),
        in_specs=[ANY], out_specs=ANY,
        scratch_shapes=[pltpu.SemaphoreType.DMA((7,)), pltpu.SemaphoreType.DMA((7,)), pltpu.SemaphoreType.DMA],
    )(gpack)


def _adam(w, g, m, v):
    m = ADAM_B1 * m + (1.0 - ADAM_B1) * g
    v = ADAM_B2 * v + (1.0 - ADAM_B2) * (g * g)
    m_hat = m / (1.0 - ADAM_B1 ** ADAM_STEP)
    v_hat = v / (1.0 - ADAM_B2 ** ADAM_STEP)
    delta = -ADAM_LR * (m_hat / (jnp.sqrt(v_hat) + ADAM_EPS) + ADAM_WD * w)
    return delta, m, v


def _sum_slots(ref):
    tot = ref[0]
    for q in range(1, N_DEV):
        tot = tot + ref[q]
    return tot


def _reduce_adam_big(recv, w, m, v):
    _, R, W = recv.shape

    def body(r_ref, w_ref, m_ref, v_ref, g_ref, d_ref, nm_ref, nv_ref):
        g = _sum_slots(r_ref)
        g_ref[...] = g
        d_ref[...], nm_ref[...], nv_ref[...] = _adam(w_ref[...], g, m_ref[...], v_ref[...])

    row = pl.BlockSpec((ADAM_TILE, W), lambda i: (i, 0))
    return pl.pallas_call(
        body, name="reduce_adam", grid=(R // ADAM_TILE,),
        in_specs=[pl.BlockSpec((N_DEV, ADAM_TILE, W), lambda i: (0, i, 0)), row, row, row], out_specs=[row] * 4,
        out_shape=[jax.ShapeDtypeStruct((R, W), F32)] * 4, compiler_params=_cparams(("parallel",)),
    )(recv, w, m, v)


def _allreduce_adam_small(gs, w, m, v):
    R, W = gs.shape

    def body(g_ref, w_ref, m_ref, v_ref, go_ref, d_ref, nm_ref, nv_ref, land, send_sems, recv_sems):
        x, y, c = _coords()
        my = _slot(x, y, c)
        cps = []
        for k in range(1, N_DEV):
            peer = _peer(k, x, y, c)
            cps.append(pltpu.make_async_remote_copy(
                src_ref=g_ref, dst_ref=land.at[my], send_sem=send_sems.at[k - 1], recv_sem=recv_sems.at[k - 1],
                device_id=peer, device_id_type=MESH))
        for cp in cps:
            cp.start()
        land[my] = g_ref[...]
        for cp in cps:
            cp.wait_recv()
        for cp in cps:
            cp.wait_send()
        g = _sum_slots(land)
        go_ref[...] = g
        d_ref[...], nm_ref[...], nv_ref[...] = _adam(w_ref[...], g, m_ref[...], v_ref[...])

    return pl.pallas_call(
        body, name="allreduce_small", in_specs=[VM] * 4, out_specs=[VM] * 4,
        out_shape=[jax.ShapeDtypeStruct((R, W), F32)] * 4,
        scratch_shapes=[pltpu.VMEM((N_DEV, R, W), F32), pltpu.SemaphoreType.DMA((7,)), pltpu.SemaphoreType.DMA((7,))],
    )(gs, w, m, v)


def _row(v):
    return v.reshape(1, -1).astype(F32)


def _step(x, mem, positions, tgt, W, M, V):
    shard = {n: W[n][0] for n, _, _ in BIG}
    full = _unpack_gathered(_gather_big(_pack_big(shard, BF16)))
    P = {n: _row(W[n]) for n in SMALL}
    for n in ("dt_bias", "a_log", "d_skip"):
        P[n] = _head_lanes(P[n])
    P.update(w_in=_pad_w_in(full["w_in"]), w_q_up=_pad_w_q(full["w_q_up"]), w_kv_up=_pad_w_kv(full["w_kv_up"]),
             conv_w=jnp.pad(full["conv_w"].astype(F32), ((0, 8 - SSD_CONV), (0, 0))))
    for n in ("w_mix_out", "w_mem_q", "w_mem_k", "w_mem_v", "w_mem_o", "w_up", "w_down"):
        P[n] = full[n]

    loss, gx, G = _local_step(x[0], mem[0], positions[0], tgt[0], P)

    G["w_in"], G["w_q_up"], G["w_kv_up"] = _unpad_w_in(G["w_in"]), _unpad_w_q(G["w_q_up"]), _unpad_w_kv(G["w_kv_up"])
    G["conv_w"] = G["conv_w"][:SSD_CONV]
    recv = _scatter_grads(_pack_grads(G))
    big = [_unpack_big(b) for b in _reduce_adam_big(
        recv, _pack_big(shard, F32), _pack_big({n: M[n][0] for n, _, _ in BIG}, F32), _pack_big({n: V[n][0] for n, _, _ in BIG}, F32))]

    def small_rows(D):
        rows = {n: _row(D[n]) for n in SMALL}
        for n in ("dt_bias", "a_log", "d_skip"):
            rows[n] = _head_lanes(rows[n])
        return rows

    loss_row = jnp.broadcast_to(loss[:, :1], (1, PACK_W))
    small_bufs = _allreduce_adam_small(
        _pack_small({n: G[n] for n in SMALL}, loss_row), _pack_small(small_rows(W)), _pack_small(small_rows(M)),
        _pack_small(small_rows(V)))
    loss_tot = small_bufs[0][LOSS_ROW, 0]
    small = [_unpack_small(b) for b in small_bufs]

    outs = []
    for res_b, res_s in zip(big, small):
        for n in ALL_W:
            if n in res_b:
                outs.append(res_b[n].reshape(W[n].shape))
            else:
                outs.append(res_s[n][:, :W[n].size].reshape(W[n].shape))
    return (loss_tot, gx[None], *outs)


def kernel(x, mem, positions, ln_in_g, ln_in_b, w_in, conv_w, conv_b, dt_bias, a_log, d_skip, ssd_norm_g, q_norm_g, w_q_up, kv_norm_g, w_kv_up, w_mix_out, ln1_g, ln1_b, w_mem_q, w_mem_k, w_mem_v, w_mem_o, ln2_g, ln2_b, w_up, w_down, ln3_g, ln3_b, loss_target, m_ln_in_g, m_ln_in_b, m_w_in, m_conv_w, m_conv_b, m_dt_bias, m_a_log, m_d_skip, m_ssd_norm_g, m_q_norm_g, m_w_q_up, m_kv_norm_g, m_w_kv_up, m_w_mix_out, m_ln1_g, m_ln1_b, m_w_mem_q, m_w_mem_k, m_w_mem_v, m_w_mem_o, m_ln2_g, m_ln2_b, m_w_up, m_w_down, m_ln3_g, m_ln3_b, v_ln_in_g, v_ln_in_b, v_w_in, v_conv_w, v_conv_b, v_dt_bias, v_a_log, v_d_skip, v_ssd_norm_g, v_q_norm_g, v_w_q_up, v_kv_norm_g, v_w_kv_up, v_w_mix_out, v_ln1_g, v_ln1_b, v_w_mem_q, v_w_mem_k, v_w_mem_v, v_w_mem_o, v_ln2_g, v_ln2_b, v_w_up, v_w_down, v_ln3_g, v_ln3_b):
    a = dict(locals())
    W = {n: a[n] for n in ALL_W}
    M = {n: a["m_" + n] for n in ALL_W}
    V = {n: a["v_" + n] for n in ALL_W}
    return _step(x, mem, positions, loss_target, W, M, V)
```

```python
import functools
import math

import jax
import jax.numpy as jnp
from jax import lax
from jax.experimental import pallas as pl
from jax.experimental.pallas import tpu as pltpu

F32, BF16 = jnp.float32, jnp.bfloat16

N_DEV = 8
D_MODEL = 1024
SSD_HEADS, SSD_HEAD_DIM, SSD_INNER, SSD_STATE, SSD_CONV, SSD_CHUNK = 8, 64, 512, 128, 4, 128
SSD_XBC = 1024
MLA_HEADS, MLA_NOPE, MLA_ROPE, MLA_QK, MLA_V = 8, 64, 32, 96, 64
MLA_Q_RANK, MLA_KV_RANK = 384, 256
ROPE_THETA = 10000.0
MEM_HEADS, MEM_HEAD_DIM = 4, 256
D_FF = 4096
IN_WIDTH = 2216
LN_EPS, RMS_EPS = 1e-5, 1e-6
ALPHA = 2.0 ** 0.25
ADAM_LR, ADAM_B1, ADAM_B2, ADAM_EPS, ADAM_WD, ADAM_STEP = 0.001, 0.9, 0.999, 1e-08, 0.01, 10

LANES = 128
NEG = -1e30
VMEM_LIMIT = 56 * 1024 * 1024

PQ, PZ, PX, PKV, PDT, PKR, PW = 0, 512, 1024, 2048, 2304, 2432, 2560
KR_LANE = 64


def _cparams(sem):
    return pltpu.CompilerParams(dimension_semantics=sem, vmem_limit_bytes=VMEM_LIMIT)


def _sigmoid(x):
    return 1.0 / (1.0 + jnp.exp(-x))


def _mm(a, b, mode, name, *, tm=512, tn=None, tk=None, out_dtypes=(F32,), epi=None, extras=(), col_slots=False):
    if mode == "nn":
        (M, K), (K2, N) = a.shape, b.shape
    elif mode == "nt":
        (M, K), (N, K2) = a.shape, b.shape
    else:
        (K, M), (K2, N) = a.shape, b.shape
    assert K == K2, (name, a.shape, b.shape)
    tm, tn, tk = min(tm, M), min(tn or N, N), min(tk or K, K)
    assert M % tm == 0 and N % tn == 0 and K % tk == 0, (name, M, N, K, tm, tn, tk)
    gk = K // tk
    a_spec = pl.BlockSpec((tk, tm), lambda i, j, k: (k, i)) if mode == "tn" else pl.BlockSpec((tm, tk), lambda i, j, k: (i, k))
    b_spec = pl.BlockSpec((tn, tk), lambda i, j, k: (j, k)) if mode == "nt" else pl.BlockSpec((tk, tn), lambda i, j, k: (k, j))
    dims = {"nn": ((1,), (0,)), "nt": ((1,), (1,)), "tn": ((0,), (0,))}[mode]
    ex_specs = []
    for arr, kind in extras:
        if kind == "mn":
            ex_specs.append(pl.BlockSpec((tm, tn), lambda i, j, k: (i, j)))
        elif kind == "n":
            ex_specs.append(pl.BlockSpec((1, tn), lambda i, j, k: (0, j)))
        else:
            ex_specs.append(pl.BlockSpec((tm, arr.shape[1]), lambda i, j, k: (i, 0)))
    ne, no = len(extras), len(out_dtypes)

    def body(*refs):
        a_ref, b_ref = refs[0], refs[1]
        ex, outs = refs[2:2 + ne], refs[2 + ne:2 + ne + no]
        part = lax.dot_general(a_ref[...].astype(BF16), b_ref[...].astype(BF16), (dims, ((), ())),
                               preferred_element_type=F32)

        def finish(acc):
            res = epi(acc, *[e[...] for e in ex]) if epi is not None else (acc,)
            for o, r in zip(outs, res):
                o[...] = r.astype(o.dtype)

        if gk == 1:
            finish(part)
        else:
            acc_ref = refs[-1]
            k = pl.program_id(2)

            @pl.when(k == 0)
            def _():
                acc_ref[...] = part

            @pl.when(k > 0)
            def _():
                acc_ref[...] += part

            @pl.when(k == gk - 1)
            def _():
                finish(acc_ref[...])

    res = pl.pallas_call(
        body, name=name, grid=(M // tm, N // tn, gk),
        in_specs=[a_spec, b_spec] + ex_specs,
        out_specs=[pl.BlockSpec((None, tm, tn), lambda i, j, k: (j, i, 0)) if col_slots else pl.BlockSpec((tm, tn), lambda i, j, k: (i, j))
                   for _ in out_dtypes],
        out_shape=[jax.ShapeDtypeStruct((N // tn, M, tn) if col_slots else (M, N), dt) for dt in out_dtypes],
        scratch_shapes=[pltpu.VMEM((tm, tn), F32)] if gk > 1 else [],
        compiler_params=_cparams(("parallel", "parallel", "arbitrary")),
    )(a, b, *[e[0] for e in extras])
    return res[0] if no == 1 else res


def _ln_stats(r):
    mu = jnp.mean(r, axis=-1, keepdims=True)
    xc = r - mu
    var = jnp.mean(xc * xc, axis=-1, keepdims=True)
    rstd = lax.rsqrt(var + LN_EPS)
    return xc * rstd, rstd


def _ln_fwd(terms, g, b, name, tm=512):
    S, D = terms[0][0].shape
    coefs = [c for _, c in terms]
    nt = len(terms)

    def body(*refs):
        r = sum(c * t[...] for t, c in zip(refs[:nt], coefs))
        xh, _ = _ln_stats(r)
        refs[nt + 2][...] = xh * refs[nt][...] + refs[nt + 1][...]

    row = pl.BlockSpec((tm, D), lambda i: (i, 0))
    vec = pl.BlockSpec((1, D), lambda i: (0, 0))
    return pl.pallas_call(
        body, name=name, grid=(S // tm,), in_specs=[row] * nt + [vec, vec], out_specs=row,
        out_shape=jax.ShapeDtypeStruct((S, D), F32), compiler_params=_cparams(("parallel",)),
    )(*[t for t, _ in terms], g, b)


def _ln_bwd(terms, dterms, g, name, tm=512):
    S, D = terms[0][0].shape
    coefs, dcoefs = [c for _, c in terms], [c for _, c in dterms]
    nt, nd = len(terms), len(dterms)

    def body(*refs):
        i = pl.program_id(0)
        r = sum(c * t[...] for t, c in zip(refs[:nt], coefs))
        dh = sum(c * t[...].astype(F32) for t, c in zip(refs[nt:nt + nd], dcoefs))
        g_ref = refs[nt + nd]
        dr_ref, dg_ref, db_ref = refs[nt + nd + 1:]
        xh, rstd = _ln_stats(r)
        dxh = dh * g_ref[...]
        m1 = jnp.mean(dxh, axis=-1, keepdims=True)
        m2 = jnp.mean(dxh * xh, axis=-1, keepdims=True)
        dr_ref[...] = rstd * (dxh - m1 - xh * m2)
        pg = jnp.sum(dh * xh, axis=0, keepdims=True)
        pb = jnp.sum(dh, axis=0, keepdims=True)

        @pl.when(i == 0)
        def _():
            dg_ref[...] = pg
            db_ref[...] = pb

        @pl.when(i > 0)
        def _():
            dg_ref[...] += pg
            db_ref[...] += pb

    row = pl.BlockSpec((tm, D), lambda i: (i, 0))
    vec = pl.BlockSpec((1, D), lambda i: (0, 0))
    return pl.pallas_call(
        body, name=name, grid=(S // tm,), in_specs=[row] * (nt + nd) + [vec], out_specs=[row, vec, vec],
        out_shape=[jax.ShapeDtypeStruct((S, D), F32), jax.ShapeDtypeStruct((1, D), F32), jax.ShapeDtypeStruct((1, D), F32)],
        compiler_params=_cparams(("arbitrary",)),
    )(*[t for t, _ in terms], *[t for t, _ in dterms], g)


def _ln_loss_bwd(terms, g, b, tgt, name, tm=512):
    S, D = terms[0][0].shape
    coefs = [c for _, c in terms]
    nt = len(terms)

    def body(*refs):
        i = pl.program_id(0)
        r = sum(c * t[...] for t, c in zip(refs[:nt], coefs))
        g_ref, b_ref, t_ref = refs[nt:nt + 3]
        loss_ref, dr_ref, dg_ref, db_ref = refs[nt + 3:]
        xh, rstd = _ln_stats(r)
        h = xh * g_ref[...] + b_ref[...]
        diff = h - t_ref[...]
        pl_ = 0.5 * jnp.sum(jnp.mean(diff * diff, axis=-1, keepdims=True), axis=0, keepdims=True)
        dh = diff * (1.0 / D)
        dxh = dh * g_ref[...]
        m1 = jnp.mean(dxh, axis=-1, keepdims=True)
        m2 = jnp.mean(dxh * xh, axis=-1, keepdims=True)
        dr_ref[...] = rstd * (dxh - m1 - xh * m2)
        pg = jnp.sum(dh * xh, axis=0, keepdims=True)
        pb = jnp.sum(dh, axis=0, keepdims=True)
        plb = jnp.broadcast_to(pl_, (1, LANES))

        @pl.when(i == 0)
        def _():
            dg_ref[...] = pg
            db_ref[...] = pb
            loss_ref[...] = plb

        @pl.when(i > 0)
        def _():
            dg_ref[...] += pg
            db_ref[...] += pb
            loss_ref[...] += plb

    row = pl.BlockSpec((tm, D), lambda i: (i, 0))
    vec = pl.BlockSpec((1, D), lambda i: (0, 0))
    lvec = pl.BlockSpec((1, LANES), lambda i: (0, 0))
    return pl.pallas_call(
        body, name=name, grid=(S // tm,), in_specs=[row] * nt + [vec, vec, row], out_specs=[lvec, row, vec, vec],
        out_shape=[jax.ShapeDtypeStruct((1, LANES), F32), jax.ShapeDtypeStruct((S, D), F32),
                   jax.ShapeDtypeStruct((1, D), F32), jax.ShapeDtypeStruct((1, D), F32)],
        compiler_params=_cparams(("arbitrary",)),
    )(*[t for t, _ in terms], g, b, tgt)


def _rope_tables(positions):
    half = MLA_ROPE // 2
    inv_freq = jnp.power(ROPE_THETA, -jnp.arange(half, dtype=F32) / half)
    ang = positions.astype(F32)[:, None] * inv_freq
    cos, sin = jnp.cos(ang), jnp.sin(ang)
    S = positions.shape[0]
    one, zero = jnp.ones((S, MLA_NOPE), F32), jnp.zeros((S, half), F32)
    pad = jnp.zeros((S, LANES - MLA_QK), F32)
    c = jnp.concatenate([one, cos, cos, pad], axis=1)
    s1 = jnp.concatenate([0 * one, -sin, zero, pad], axis=1)
    s2 = jnp.concatenate([0 * one, zero, sin, pad], axis=1)
    return c, s1, s2


def _rope_block(x, c, s1, s2):
    half = MLA_ROPE // 2
    return x * c + pltpu.roll(x, LANES - half, axis=1) * s1 + pltpu.roll(x, half, axis=1) * s2


def _rms_fwd(x, g):
    r = lax.rsqrt(jnp.mean(x * x, axis=-1, keepdims=True) + RMS_EPS)
    return x * r * g


def _rms_bwd(x, g, dy):
    r = lax.rsqrt(jnp.mean(x * x, axis=-1, keepdims=True) + RMS_EPS)
    xh = x * r
    dyh = dy * g
    dx = r * (dyh - xh * jnp.mean(dyh * xh, axis=-1, keepdims=True))
    return dx, jnp.sum(dy * xh, axis=0, keepdims=True)


def _mla_prep(proj, qg, kvg, tabs, tm=512):
    S = proj.shape[0]

    def body(ql_ref, kvl_ref, kr_ref, qg_ref, kvg_ref, c_ref, s1_ref, s2_ref, qn_ref, kvn_ref, kpe_ref):
        qn_ref[...] = _rms_fwd(ql_ref[...], qg_ref[...]).astype(BF16)
        kvn_ref[...] = _rms_fwd(kvl_ref[...], kvg_ref[...]).astype(BF16)
        kpe_ref[...] = _rope_block(kr_ref[...], c_ref[...], s1_ref[...], s2_ref[...])

    tab = pl.BlockSpec((tm, LANES), lambda i: (i, 0))
    return pl.pallas_call(
        body, name="mla_prep", grid=(S // tm,),
        in_specs=[pl.BlockSpec((tm, MLA_Q_RANK), lambda i: (i, PQ // MLA_Q_RANK)),
                  pl.BlockSpec((tm, MLA_KV_RANK), lambda i: (i, PKV // MLA_KV_RANK)),
                  pl.BlockSpec((tm, LANES), lambda i: (i, PKR // LANES)),
                  pl.BlockSpec((1, MLA_Q_RANK), lambda i: (0, 0)), pl.BlockSpec((1, MLA_KV_RANK), lambda i: (0, 0)),
                  tab, tab, tab],
        out_specs=[pl.BlockSpec((tm, MLA_Q_RANK), lambda i: (i, 0)), pl.BlockSpec((tm, MLA_KV_RANK), lambda i: (i, 0)), tab],
        out_shape=[jax.ShapeDtypeStruct((S, MLA_Q_RANK), BF16), jax.ShapeDtypeStruct((S, MLA_KV_RANK), BF16),
                   jax.ShapeDtypeStruct((S, LANES), F32)],
        compiler_params=_cparams(("parallel",)),
    )(proj, proj, proj, qg, kvg, *tabs)


def _mla_prep_bwd(proj, qg, kvg, tabs, dqn, dkvn, dk_all, tm=512):
    S = proj.shape[0]

    def body(ql_ref, kvl_ref, qg_ref, kvg_ref, c_ref, s1_ref, s2_ref, dqn_ref, dkvn_ref, dk_ref,
             dql_ref, dkvl_ref, dkr_ref, dqg_ref, dkvg_ref):
        i = pl.program_id(0)
        dql, pq = _rms_bwd(ql_ref[...], qg_ref[...], dqn_ref[...])
        dkvl, pkv = _rms_bwd(kvl_ref[...], kvg_ref[...], dkvn_ref[...])
        dql_ref[...] = dql
        dkvl_ref[...] = dkvl
        dk = dk_ref[...]
        dkpe = dk[:, 0:LANES]
        for h in range(1, MLA_HEADS):
            dkpe = dkpe + dk[:, h * LANES:(h + 1) * LANES]
        lane = lax.broadcasted_iota(jnp.int32, dkpe.shape, 1)
        dkpe = jnp.where((lane >= KR_LANE) & (lane < KR_LANE + MLA_ROPE), dkpe, 0.0)
        dkr_ref[...] = _rope_block(dkpe, c_ref[...], -s1_ref[...], -s2_ref[...])

        @pl.when(i == 0)
        def _():
            dqg_ref[...] = pq
            dkvg_ref[...] = pkv

        @pl.when(i > 0)
        def _():
            dqg_ref[...] += pq
            dkvg_ref[...] += pkv

    tab = pl.BlockSpec((tm, LANES), lambda i: (i, 0))
    qspec = pl.BlockSpec((tm, MLA_Q_RANK), lambda i: (i, 0))
    kvspec = pl.BlockSpec((tm, MLA_KV_RANK), lambda i: (i, 0))
    qv, kvv = pl.BlockSpec((1, MLA_Q_RANK), lambda i: (0, 0)), pl.BlockSpec((1, MLA_KV_RANK), lambda i: (0, 0))
    return pl.pallas_call(
        body, name="mla_prep_bwd", grid=(S // tm,),
        in_specs=[pl.BlockSpec((tm, MLA_Q_RANK), lambda i: (i, PQ // MLA_Q_RANK)),
                  pl.BlockSpec((tm, MLA_KV_RANK), lambda i: (i, PKV // MLA_KV_RANK)),
                  qv, kvv, tab, tab, tab, qspec, kvspec, pl.BlockSpec((tm, MLA_HEADS * LANES), lambda i: (i, 0))],
        out_specs=[qspec, kvspec, tab, qv, kvv],
        out_shape=[jax.ShapeDtypeStruct((S, MLA_Q_RANK), F32), jax.ShapeDtypeStruct((S, MLA_KV_RANK), F32),
                   jax.ShapeDtypeStruct((S, LANES), F32), jax.ShapeDtypeStruct((1, MLA_Q_RANK), F32),
                   jax.ShapeDtypeStruct((1, MLA_KV_RANK), F32)],
        compiler_params=_cparams(("arbitrary",)),
    )(proj, proj, qg, kvg, *tabs, dqn, dkvn, dk_all)


def _rope_bwd_all(dq_all, tabs, tm=512):
    S, W = dq_all.shape

    def body(dq_ref, c_ref, s1_ref, s2_ref, o_ref):
        c, s1, s2 = c_ref[...], -s1_ref[...], -s2_ref[...]
        for h in range(W // LANES):
            o_ref[:, h * LANES:(h + 1) * LANES] = _rope_block(dq_ref[:, h * LANES:(h + 1) * LANES], c, s1, s2).astype(BF16)

    tab = pl.BlockSpec((tm, LANES), lambda i: (i, 0))
    row = pl.BlockSpec((tm, W), lambda i: (i, 0))
    return pl.pallas_call(body, name="rope_bwd", grid=(S // tm,), in_specs=[row, tab, tab, tab], out_specs=row,
                          out_shape=jax.ShapeDtypeStruct((S, W), BF16), compiler_params=_cparams(("parallel",)))(dq_all, *tabs)


ATT_SCALE = MLA_QK ** -0.5
N_PAIR = MLA_HEADS // 2


def _causal_mask(qi, ki, tq, tk):
    row = qi * tq + lax.broadcasted_iota(jnp.int32, (tq, tk), 0)
    col = ki * tk + lax.broadcasted_iota(jnp.int32, (tq, tk), 1)
    return col <= row


def _lane_tile(x, n):
    return jnp.concatenate([x] * n, axis=1) if n > 1 else x


def _attn_fwd(q_all, kv_all, tq=512, tk=1024):
    S = q_all.shape[0]
    tq, tk = min(tq, S), min(tk, S)
    nq, nk, nb, r = S // tq, S // tk, tk // LANES, tk // tq

    def body(q_ref, k_ref, v_ref, o_ref, lse_ref, m_s, l_s, acc_s):
        qi, ki = pl.program_id(1), pl.program_id(2)
        last = lax.div(qi, r)

        @pl.when(ki == 0)
        def _():
            m_s[...] = jnp.full(m_s.shape, NEG, F32)
            l_s[...] = jnp.zeros(l_s.shape, F32)
            acc_s[...] = jnp.zeros(acc_s.shape, F32)

        @pl.when(ki <= last)
        def _():
            v = v_ref[...]
            mask = _causal_mask(qi, ki, tq, tk)
            for hh in range(2):
                q = q_ref[:, hh * LANES:(hh + 1) * LANES]
                k = k_ref[:, hh * LANES:(hh + 1) * LANES]
                s = lax.dot_general(q, k, (((1,), (1,)), ((), ())), preferred_element_type=F32) * ATT_SCALE
                s = jnp.where(mask, s, NEG)
                m_prev = m_s[hh]
                m_new = jnp.maximum(m_prev, jnp.max(s, axis=-1, keepdims=True))
                p = jnp.exp(s - _lane_tile(m_new, nb))
                alpha = jnp.exp(m_prev - m_new)
                ps = p[:, :LANES]
                for j in range(1, nb):
                    ps = ps + p[:, j * LANES:(j + 1) * LANES]
                l_s[hh] = alpha * l_s[hh] + ps
                acc_s[hh] = alpha * acc_s[hh] + jnp.dot(p.astype(BF16), v, preferred_element_type=F32)
                m_s[hh] = m_new

        @pl.when(ki == last)
        def _():
            first = lax.broadcasted_iota(jnp.int32, (tq, LANES), 1) < MLA_V
            l0 = jnp.sum(l_s[0], axis=-1, keepdims=True)
            l1 = jnp.sum(l_s[1], axis=-1, keepdims=True)
            o_ref[...] = jnp.where(first, acc_s[0] / l0, acc_s[1] / l1)
            lse_ref[:, :LANES] = m_s[0] + jnp.log(l0)
            lse_ref[:, LANES:] = m_s[1] + jnp.log(l1)

    return pl.pallas_call(
        body, name="mla_attn_fwd", grid=(N_PAIR, nq, nk),
        in_specs=[pl.BlockSpec((tq, 2 * LANES), lambda p, qi, ki: (qi, p)),
                  pl.BlockSpec((tk, 2 * LANES), lambda p, qi, ki: (jnp.minimum(ki, lax.div(qi, r)), p)),
                  pl.BlockSpec((tk, LANES), lambda p, qi, ki: (jnp.minimum(ki, lax.div(qi, r)), MLA_HEADS + p))],
        out_specs=[pl.BlockSpec((tq, LANES), lambda p, qi, ki: (qi, p)), pl.BlockSpec((tq, 2 * LANES), lambda p, qi, ki: (qi, p))],
        out_shape=[jax.ShapeDtypeStruct((S, MLA_HEADS * MLA_V), F32), jax.ShapeDtypeStruct((S, MLA_HEADS * LANES), F32)],
        scratch_shapes=[pltpu.VMEM((2, tq, LANES), F32), pltpu.VMEM((2, tq, LANES), F32), pltpu.VMEM((2, tq, LANES), F32)],
        compiler_params=_cparams(("parallel", "parallel", "arbitrary")),
    )(q_all, kv_all, kv_all)


def _attn_delta(dcat, o, tm=512):
    S = o.shape[0]

    def body(do_ref, o_ref, d_ref):
        prod = do_ref[...] * o_ref[...]
        first = lax.broadcasted_iota(jnp.int32, (tm, LANES), 1) < MLA_V
        for p in range(N_PAIR):
            pp = prod[:, p * LANES:(p + 1) * LANES]
            d0 = jnp.sum(jnp.where(first, pp, 0.0), axis=-1, keepdims=True)
            d1 = jnp.sum(jnp.where(first, 0.0, pp), axis=-1, keepdims=True)
            d_ref[:, 2 * p * LANES:(2 * p + 1) * LANES] = jnp.broadcast_to(d0, (tm, LANES))
            d_ref[:, (2 * p + 1) * LANES:(2 * p + 2) * LANES] = jnp.broadcast_to(d1, (tm, LANES))

    W = MLA_HEADS * MLA_V
    return pl.pallas_call(
        body, name="mla_attn_delta", grid=(S // tm,),
        in_specs=[pl.BlockSpec((tm, W), lambda i: (i, 1)), pl.BlockSpec((tm, W), lambda i: (i, 0))],
        out_specs=pl.BlockSpec((tm, MLA_HEADS * LANES), lambda i: (i, 0)),
        out_shape=jax.ShapeDtypeStruct((S, MLA_HEADS * LANES), F32), compiler_params=_cparams(("parallel",)),
    )(dcat, o)


def _attn_bwd(q_all, kv_all, dcat, lse, delta, tq=512, tk=512):
    S = q_all.shape[0]
    tq, tk = min(tq, S), min(tk, S)
    nq, nk, nb = S // tq, S // tk, tk // LANES
    assert tq == tk

    def body(q_ref, k_ref, v_ref, do_ref, lse_ref, dl_ref, dq_ref, dk_ref, dv_ref, dk_s, dv_s):
        ki, qi = pl.program_id(1), pl.program_id(2)

        @pl.when((ki == 0) & (qi == 0))
        def _():
            dq_ref[...] = jnp.zeros(dq_ref.shape, F32)

        @pl.when(qi == 0)
        def _():
            dk_s[...] = jnp.zeros(dk_s.shape, F32)
            dv_s[...] = jnp.zeros(dv_s.shape, F32)

        @pl.when(qi >= ki)
        def _():
            v, do = v_ref[...], do_ref[...]
            first = lax.broadcasted_iota(jnp.int32, (tq, LANES), 1) < MLA_V
            firstk = lax.broadcasted_iota(jnp.int32, (tk, LANES), 1) < MLA_V
            mask = _causal_mask(qi, ki, tq, tk)
            do_b = do.astype(BF16)
            rows = pl.ds(pl.multiple_of(qi * tq, tq), tq)
            for hh in range(2):
                sl = slice(hh * LANES, (hh + 1) * LANES)
                q, k = q_ref[:, sl], k_ref[:, sl]
                s = lax.dot_general(q, k, (((1,), (1,)), ((), ())), preferred_element_type=F32) * ATT_SCALE
                p = jnp.exp(jnp.where(mask, s, NEG) - _lane_tile(lse_ref[:, sl], nb))
                do_h = jnp.where(first if hh == 0 else ~first, do, 0.0).astype(BF16)
                dp = lax.dot_general(do_h, v, (((1,), (1,)), ((), ())), preferred_element_type=F32)
                ds_b = (p * (dp - _lane_tile(dl_ref[:, sl], nb)) * ATT_SCALE).astype(BF16)
                pv = lax.dot_general(p.astype(BF16), do_b, (((0,), (0,)), ((), ())), preferred_element_type=F32)
                dv_s[...] += jnp.where(firstk if hh == 0 else ~firstk, pv, 0.0)
                dk_s[:, sl] += lax.dot_general(ds_b, q, (((0,), (0,)), ((), ())), preferred_element_type=F32)
                dq_ref[rows, sl] += jnp.dot(ds_b, k, preferred_element_type=F32)

        @pl.when(qi == nq - 1)
        def _():
            dk_ref[...] = dk_s[...]
            dv_ref[...] = dv_s[...]

    wide = pl.BlockSpec((tq, 2 * LANES), lambda p, ki, qi: (jnp.maximum(qi, ki), p))
    return pl.pallas_call(
        body, name="mla_attn_bwd", grid=(N_PAIR, nk, nq),
        in_specs=[wide, pl.BlockSpec((tk, 2 * LANES), lambda p, ki, qi: (ki, p)),
                  pl.BlockSpec((tk, LANES), lambda p, ki, qi: (ki, MLA_HEADS + p)),
                  pl.BlockSpec((tq, LANES), lambda p, ki, qi: (jnp.maximum(qi, ki), N_PAIR + p)), wide, wide],
        out_specs=[pl.BlockSpec((S, 2 * LANES), lambda p, ki, qi: (0, p)),
                   pl.BlockSpec((tk, 2 * LANES), lambda p, ki, qi: (ki, p)), pl.BlockSpec((tk, LANES), lambda p, ki, qi: (ki, p))],
        out_shape=[jax.ShapeDtypeStruct((S, MLA_HEADS * LANES), F32), jax.ShapeDtypeStruct((S, MLA_HEADS * LANES), F32),
                   jax.ShapeDtypeStruct((S, MLA_HEADS * MLA_V), F32)],
        scratch_shapes=[pltpu.VMEM((tk, 2 * LANES), F32), pltpu.VMEM((tk, LANES), F32)],
        compiler_params=_cparams(("parallel", "arbitrary", "arbitrary")),
    )(q_all, kv_all, kv_all, dcat, lse, delta)


def _attn_p_ds(q, k, v, do, lse_h, delta_h, half_mask, mask):
    s = lax.dot_general(q, k, (((1,), (1,)), ((), ())), preferred_element_type=F32) * ATT_SCALE
    p = jnp.exp(jnp.where(mask, s, NEG) - lse_h)
    do_h = jnp.where(half_mask, do, 0.0).astype(BF16)
    dp = lax.dot_general(do_h, v, (((1,), (1,)), ((), ())), preferred_element_type=F32)
    ds = p * (dp - delta_h) * ATT_SCALE
    return p, ds


def _attn_deltas(do, o, first):
    prod = do * o
    return (jnp.sum(jnp.where(first, prod, 0.0), axis=-1, keepdims=True),
            jnp.sum(jnp.where(first, 0.0, prod), axis=-1, keepdims=True))


def _attn_bwd_dq(q_all, kv_all, dcat, o, lse, tq=512, tk=512):
    S = q_all.shape[0]
    nq, nk = S // tq, S // tk

    def body(q_ref, k_ref, v_ref, do_ref, o_ref, lse_ref, dq_ref, acc_s):
        qi, ki = pl.program_id(1), pl.program_id(2)

        @pl.when(ki == 0)
        def _():
            acc_s[...] = jnp.zeros(acc_s.shape, F32)

        @pl.when(ki <= qi)
        def _():
            v, do, lse_t = v_ref[...], do_ref[...], lse_ref[...]
            first = lax.broadcasted_iota(jnp.int32, (tq, LANES), 1) < MLA_V
            deltas = _attn_deltas(do, o_ref[...], first)
            mask = _causal_mask(qi, ki, tq, tk)
            for hh in range(2):
                k = k_ref[:, hh * LANES:(hh + 1) * LANES]
                _, ds = _attn_p_ds(q_ref[:, hh * LANES:(hh + 1) * LANES], k, v, do,
                                   lse_t[:, hh * MLA_V:hh * MLA_V + 1], deltas[hh], first if hh == 0 else ~first, mask)
                acc_s[:, hh * LANES:(hh + 1) * LANES] += jnp.dot(ds.astype(BF16), k, preferred_element_type=F32)

        @pl.when(ki == qi)
        def _():
            dq_ref[...] = acc_s[...]

    half = pl.BlockSpec((tq, LANES), lambda p, qi, ki: (qi, p))
    return pl.pallas_call(
        body, name="mla_attn_bwd_dq", grid=(N_PAIR, nq, nk),
        in_specs=[pl.BlockSpec((tq, 2 * LANES), lambda p, qi, ki: (qi, p)),
                  pl.BlockSpec((tk, 2 * LANES), lambda p, qi, ki: (jnp.minimum(ki, qi), p)),
                  pl.BlockSpec((tk, LANES), lambda p, qi, ki: (jnp.minimum(ki, qi), MLA_HEADS + p)),
                  pl.BlockSpec((tq, LANES), lambda p, qi, ki: (qi, N_PAIR + p)), half, half],
        out_specs=pl.BlockSpec((tq, 2 * LANES), lambda p, qi, ki: (qi, p)),
        out_shape=jax.ShapeDtypeStruct((S, MLA_HEADS * LANES), F32),
        scratch_shapes=[pltpu.VMEM((tq, 2 * LANES), F32)],
        compiler_params=_cparams(("parallel", "parallel", "arbitrary")),
    )(q_all, kv_all, kv_all, dcat, o, lse)


def _attn_bwd_dkv(q_all, kv_all, dcat, o, lse, tq=512, tk=512):
    S = q_all.shape[0]
    nq, nk = S // tq, S // tk

    def body(q_ref, k_ref, v_ref, do_ref, o_ref, lse_ref, dk_ref, dv_ref, dk_s, dv_s):
        ki, qi = pl.program_id(1), pl.program_id(2)

        @pl.when(qi == 0)
        def _():
            dk_s[...] = jnp.zeros(dk_s.shape, F32)
            dv_s[...] = jnp.zeros(dv_s.shape, F32)

        @pl.when(qi >= ki)
        def _():
            v, do, lse_t = v_ref[...], do_ref[...], lse_ref[...]
            first = lax.broadcasted_iota(jnp.int32, (tq, LANES), 1) < MLA_V
            firstk = lax.broadcasted_iota(jnp.int32, (tk, LANES), 1) < MLA_V
            deltas = _attn_deltas(do, o_ref[...], first)
            mask = _causal_mask(qi, ki, tq, tk)
            do_b = do.astype(BF16)
            for hh in range(2):
                q = q_ref[:, hh * LANES:(hh + 1) * LANES]
                p, ds = _attn_p_ds(q, k_ref[:, hh * LANES:(hh + 1) * LANES], v, do,
                                   lse_t[:, hh * MLA_V:hh * MLA_V + 1], deltas[hh], first if hh == 0 else ~first, mask)
                pv = lax.dot_general(p.astype(BF16), do_b, (((0,), (0,)), ((), ())), preferred_element_type=F32)
                dv_s[...] += jnp.where(firstk if hh == 0 else ~firstk, pv, 0.0)
                dk_s[:, hh * LANES:(hh + 1) * LANES] += lax.dot_general(ds.astype(BF16), q, (((0,), (0,)), ((), ())),
                                                                       preferred_element_type=F32)

        @pl.when(qi == nq - 1)
        def _():
            dk_ref[...] = dk_s[...]
            dv_ref[...] = dv_s[...]

    half = pl.BlockSpec((tq, LANES), lambda p, ki, qi: (jnp.maximum(qi, ki), p))
    return pl.pallas_call(
        body, name="mla_attn_bwd_dkv", grid=(N_PAIR, nk, nq),
        in_specs=[pl.BlockSpec((tq, 2 * LANES), lambda p, ki, qi: (jnp.maximum(qi, ki), p)),
                  pl.BlockSpec((tk, 2 * LANES), lambda p, ki, qi: (ki, p)),
                  pl.BlockSpec((tk, LANES), lambda p, ki, qi: (ki, MLA_HEADS + p)),
                  pl.BlockSpec((tq, LANES), lambda p, ki, qi: (jnp.maximum(qi, ki), N_PAIR + p)), half, half],
        out_specs=[pl.BlockSpec((tk, 2 * LANES), lambda p, ki, qi: (ki, p)), pl.BlockSpec((tk, LANES), lambda p, ki, qi: (ki, p))],
        out_shape=[jax.ShapeDtypeStruct((S, MLA_HEADS * LANES), F32), jax.ShapeDtypeStruct((S, MLA_HEADS * MLA_V), F32)],
        scratch_shapes=[pltpu.VMEM((tk, 2 * LANES), F32), pltpu.VMEM((tk, LANES), F32)],
        compiler_params=_cparams(("parallel", "parallel", "arbitrary")),
    )(q_all, kv_all, kv_all, dcat, o, lse)


MEM_SCALE = MEM_HEAD_DIM ** -0.5


def _mem_probs(q, k):
    s = lax.dot_general(q, k, (((1,), (1,)), ((), ())), preferred_element_type=F32) * MEM_SCALE
    e = jnp.exp(s - jnp.max(s, axis=-1, keepdims=True))
    return e / jnp.sum(e, axis=-1, keepdims=True)


def _mem_attn_fwd(qm, km, vm, tq=512):
    S, W = qm.shape
    M = km.shape[0]

    def body(q_ref, k_ref, v_ref, o_ref):
        for h in range(MEM_HEADS):
            sl = slice(h * MEM_HEAD_DIM, (h + 1) * MEM_HEAD_DIM)
            p = _mem_probs(q_ref[:, sl], k_ref[:, sl])
            o_ref[:, sl] = jnp.dot(p.astype(BF16), v_ref[:, sl], preferred_element_type=F32).astype(BF16)

    row = pl.BlockSpec((tq, W), lambda i: (i, 0))
    full = pl.BlockSpec((M, W), lambda i: (0, 0))
    return pl.pallas_call(body, name="mem_attn_fwd", grid=(S // tq,), in_specs=[row, full, full], out_specs=row,
                          out_shape=jax.ShapeDtypeStruct((S, W), BF16), compiler_params=_cparams(("parallel",)))(qm, km, vm)


def _mem_attn_bwd(qm, km, vm, dom, tq=512):
    S, W = qm.shape
    M = km.shape[0]

    def body(q_ref, k_ref, v_ref, do_ref, dq_ref, dk_ref, dv_ref):
        i = pl.program_id(0)

        @pl.when(i == 0)
        def _():
            dk_ref[...] = jnp.zeros(dk_ref.shape, F32)
            dv_ref[...] = jnp.zeros(dv_ref.shape, F32)

        for h in range(MEM_HEADS):
            sl = slice(h * MEM_HEAD_DIM, (h + 1) * MEM_HEAD_DIM)
            q, k, v, do = q_ref[:, sl], k_ref[:, sl], v_ref[:, sl], do_ref[:, sl]
            p = _mem_probs(q, k)
            dv_ref[:, sl] += lax.dot_general(p.astype(BF16), do, (((0,), (0,)), ((), ())), preferred_element_type=F32)
            dp = lax.dot_general(do, v, (((1,), (1,)), ((), ())), preferred_element_type=F32)
            ds = (p * (dp - jnp.sum(dp * p, axis=-1, keepdims=True)) * MEM_SCALE).astype(BF16)
            dq_ref[:, sl] = jnp.dot(ds, k, preferred_element_type=F32).astype(BF16)
            dk_ref[:, sl] += lax.dot_general(ds, q, (((0,), (0,)), ((), ())), preferred_element_type=F32)

    row = pl.BlockSpec((tq, W), lambda i: (i, 0))
    full = pl.BlockSpec((M, W), lambda i: (0, 0))
    return pl.pallas_call(
        body, name="mem_attn_bwd", grid=(S // tq,), in_specs=[row, full, full, row], out_specs=[row, full, full],
        out_shape=[jax.ShapeDtypeStruct((S, W), BF16), jax.ShapeDtypeStruct((M, W), F32), jax.ShapeDtypeStruct((M, W), F32)],
        compiler_params=_cparams(("arbitrary",)),
    )(qm, km, vm, dom)


L = SSD_CHUNK
N_SPAIR = SSD_HEADS // 2
GRP_W = SSD_INNER // 2
XB0, XC0 = SSD_INNER, SSD_INNER + 2 * SSD_STATE


def _cumsum_rows(a, reverse=False):
    row = lax.broadcasted_iota(jnp.int32, a.shape, 0)
    x, sft = a, 1
    while sft < L:
        if reverse:
            x = x + jnp.where(row < L - sft, pltpu.roll(x, L - sft, axis=0), 0.0)
        else:
            x = x + jnp.where(row >= sft, pltpu.roll(x, sft, axis=0), 0.0)
        sft *= 2
    return x


def _shift_down(cur, prev, s):
    if s == 0:
        return cur
    row = lax.broadcasted_iota(jnp.int32, cur.shape, 0)
    return jnp.where(row < s, pltpu.roll(prev, s, axis=0), pltpu.roll(cur, s, axis=0))


def _shift_up(cur, nxt, s):
    if s == 0:
        return cur
    row = lax.broadcasted_iota(jnp.int32, cur.shape, 0)
    return jnp.where(row >= L - s, pltpu.roll(nxt, L - s, axis=0), pltpu.roll(cur, L - s, axis=0))


def _ssd_conv(u, prev, cw, cb):
    conv = cb + cw[SSD_CONV - 1:SSD_CONV, :] * u
    for s in range(1, SSD_CONV):
        conv = conv + cw[SSD_CONV - 1 - s:SSD_CONV - s, :] * _shift_down(u, prev, s)
    return conv


def _pair_lanes(v, h0, first):
    return jnp.where(first, v[:, h0:h0 + 1], v[:, h0 + 1:h0 + 2])


def _ssd_common(u, prev, dt_raw, cw, cb, dtb, alog):
    conv = _ssd_conv(u, prev, cw, cb)
    sg = _sigmoid(conv)
    xa = conv * sg
    dpre = dt_raw + dtb
    dtv = jnp.maximum(dpre, 0.0) + jnp.log1p(jnp.exp(-jnp.abs(dpre)))
    a_row = -jnp.exp(alog)
    cs = _cumsum_rows(dtv * a_row)
    return conv, sg, xa, dpre, dtv, a_row, cs


def _ssd_pair_fwd(xa, dtv, cs, csT, G, Cg, Bg, Sp, dsk, pp, first, tri, rowfirst):
    h0 = 2 * pp
    x = xa[:, pp * LANES:(pp + 1) * LANES]
    xdt = x * _pair_lanes(dtv, h0, first)
    xdt_b = xdt.astype(BF16)
    Ms, yd = [], []
    for h in (h0, h0 + 1):
        lam = jnp.exp(jnp.where(tri, cs[:, h:h + 1] - csT[h:h + 1, :], NEG))
        M = G * lam
        Ms.append((M, lam))
        yd.append(jnp.dot(M.astype(BF16), xdt_b, preferred_element_type=F32))
    T = lax.dot_general(Cg, Sp.astype(BF16), (((1,), (1,)), ((), ())), preferred_element_type=F32)
    E = jnp.exp(_pair_lanes(cs, h0, first))
    yoff = E * T
    csl = cs[L - 1:L, :]
    Fd = jnp.exp(_pair_lanes(csl, h0, first) - _pair_lanes(cs, h0, first))
    el = jnp.exp(csl)
    el_rows = jnp.where(rowfirst, el[:, h0:h0 + 1], el[:, h0 + 1:h0 + 2])
    Sloc = lax.dot_general((xdt * Fd).astype(BF16), Bg, (((0,), (0,)), ((), ())), preferred_element_type=F32)
    S_new = el_rows * Sp + Sloc
    y = jnp.where(first, yd[0], yd[1]) + yoff + x * _pair_lanes(dsk, h0, first[:1])
    return y, S_new, (x, xdt, xdt_b, Ms, E, yoff, Fd, el, el_rows)


def _ssd_masks():
    lane = lax.broadcasted_iota(jnp.int32, (L, LANES), 1)
    row = lax.broadcasted_iota(jnp.int32, (L, LANES), 0)
    return lane, row, lane < SSD_HEAD_DIM, row >= lane, row[:, :1] < SSD_HEAD_DIM


def _ssd_specs(nc, rev):
    def cidx(i):
        return nc - 1 - i if rev else i
    z = pl.BlockSpec((L, SSD_INNER), lambda i: (cidx(i), PZ // SSD_INNER))
    u = pl.BlockSpec((L, SSD_XBC), lambda i: (cidx(i), PX // SSD_XBC))
    dt = pl.BlockSpec((L, LANES), lambda i: (cidx(i), PDT // LANES))
    return cidx, z, u, dt


def _vec(w):
    return pl.BlockSpec((1, w), lambda i: (0, 0))


def _ssd_fwd(proj, cw, cb, dtb, alog, dsk, ng):
    S = proj.shape[0]
    nc = S // L

    def body(z_ref, u_ref, dt_ref, cw_ref, cb_ref, dtb_ref, alog_ref, dsk_ref, ng_ref, y_ref, st_ref, prev_s, state_s):
        c = pl.program_id(0)

        @pl.when(c == 0)
        def _():
            prev_s[...] = jnp.zeros(prev_s.shape, F32)
            state_s[...] = jnp.zeros(state_s.shape, F32)

        u = u_ref[...]
        _, _, xa, _, dtv, _, cs = _ssd_common(u, prev_s[...], dt_ref[...], cw_ref[...], cb_ref[...], dtb_ref[...], alog_ref[...])
        prev_s[...] = u
        csT = cs.T
        _, _, first, tri, rowfirst = _ssd_masks()
        dsk_v = dsk_ref[...]
        ys = []
        for g in range(2):
            Bg = xa[:, XB0 + g * SSD_STATE:XB0 + (g + 1) * SSD_STATE].astype(BF16)
            Cg = xa[:, XC0 + g * SSD_STATE:XC0 + (g + 1) * SSD_STATE].astype(BF16)
            G = lax.dot_general(Cg, Bg, (((1,), (1,)), ((), ())), preferred_element_type=F32)
            for pp in (2 * g, 2 * g + 1):
                Sp = state_s[pp]
                st_ref[pp * LANES:(pp + 1) * LANES, :] = Sp
                y, S_new, _ = _ssd_pair_fwd(xa, dtv, cs, csT, G, Cg, Bg, Sp, dsk_v, pp, first, tri, rowfirst)
                state_s[pp] = S_new
                ys.append(y)
        z = z_ref[...]
        for g in range(2):
            yg = jnp.concatenate([ys[2 * g], ys[2 * g + 1]], axis=1)
            zg = z[:, g * GRP_W:(g + 1) * GRP_W]
            gated = yg * (zg * _sigmoid(zg))
            r = lax.rsqrt(jnp.mean(gated * gated, axis=-1, keepdims=True) + RMS_EPS)
            y_ref[:, g * GRP_W:(g + 1) * GRP_W] = gated * r * ng_ref[:, g * GRP_W:(g + 1) * GRP_W]

    _, zs, us, dts = _ssd_specs(nc, False)
    return pl.pallas_call(
        body, name="ssd_fwd", grid=(nc,),
        in_specs=[zs, us, dts, pl.BlockSpec((8, SSD_XBC), lambda i: (0, 0)), _vec(SSD_XBC), _vec(LANES), _vec(LANES), _vec(LANES),
                  _vec(SSD_INNER)],
        out_specs=[pl.BlockSpec((L, SSD_INNER), lambda i: (i, 0)), pl.BlockSpec((N_SPAIR * LANES, SSD_STATE), lambda i: (i, 0))],
        out_shape=[jax.ShapeDtypeStruct((S, SSD_INNER), F32), jax.ShapeDtypeStruct((nc * N_SPAIR * LANES, SSD_STATE), F32)],
        scratch_shapes=[pltpu.VMEM((L, SSD_XBC), F32), pltpu.VMEM((N_SPAIR, LANES, SSD_STATE), F32)],
        compiler_params=_cparams(("arbitrary",)),
    )(proj, proj, proj, cw, cb, dtb, alog, dsk, ng)


def _ssd_bwd(proj, states, dy, cw, cb, dtb, alog, dsk, ng):
    S = proj.shape[0]
    nc = S // L

    def body(z_ref, u_ref, up_ref, dt_ref, st_ref, dy_ref, cw_ref, cb_ref, dtb_ref, alog_ref, dsk_ref, ng_ref,
             dz_ref, du_ref, ddt_ref, dcw_ref, dcb_ref, ddtb_ref, dalog_ref, ddsk_ref, dng_ref,
             dS_s, dconv_s, dD_s):
        i = pl.program_id(0)
        c = nc - 1 - i

        @pl.when(i == 0)
        def _():
            dS_s[...] = jnp.zeros(dS_s.shape, F32)
            dconv_s[...] = jnp.zeros(dconv_s.shape, F32)
            dD_s[...] = jnp.zeros(dD_s.shape, F32)
            for r in (dcw_ref, dcb_ref, ddtb_ref, dalog_ref, ddsk_ref, dng_ref):
                r[...] = jnp.zeros(r.shape, F32)

        u = u_ref[...]
        prev = jnp.where(c > 0, up_ref[...], 0.0)
        cw_v = cw_ref[...]
        conv, sg, xa, dpre, dtv, a_row, cs = _ssd_common(u, prev, dt_ref[...], cw_v, cb_ref[...], dtb_ref[...], alog_ref[...])
        csT = cs.T
        lane, row, first, tri, rowfirst = _ssd_masks()
        dsk_v = dsk_ref[...]

        fw = []
        Gs, Bs, Cs = [], [], []
        for g in range(2):
            Bg = xa[:, XB0 + g * SSD_STATE:XB0 + (g + 1) * SSD_STATE].astype(BF16)
            Cg = xa[:, XC0 + g * SSD_STATE:XC0 + (g + 1) * SSD_STATE].astype(BF16)
            G = lax.dot_general(Cg, Bg, (((1,), (1,)), ((), ())), preferred_element_type=F32)
            Gs.append(G), Bs.append(Bg), Cs.append(Cg)
            for pp in (2 * g, 2 * g + 1):
                Sp = st_ref[pp * LANES:(pp + 1) * LANES, :]
                y, _, keep = _ssd_pair_fwd(xa, dtv, cs, csT, G, Cg, Bg, Sp, dsk_v, pp, first, tri, rowfirst)
                fw.append((y, Sp, keep))

        z = z_ref[...]
        dys = []
        for g in range(2):
            sl = slice(g * GRP_W, (g + 1) * GRP_W)
            yg = jnp.concatenate([fw[2 * g][0], fw[2 * g + 1][0]], axis=1)
            zg = z[:, sl]
            sz = _sigmoid(zg)
            silu_z = zg * sz
            gated = yg * silu_z
            r = lax.rsqrt(jnp.mean(gated * gated, axis=-1, keepdims=True) + RMS_EPS)
            nh = gated * r
            dout = dy_ref[:, sl]
            dng_ref[:, sl] += jnp.sum(dout * nh, axis=0, keepdims=True)
            dnh = dout * ng_ref[:, sl]
            dgated = r * (dnh - nh * jnp.mean(dnh * nh, axis=-1, keepdims=True))
            dz_ref[:, sl] = dgated * yg * (sz * (1.0 + zg * (1.0 - sz)))
            dyg = dgated * silu_z
            dys.append(dyg[:, :LANES]), dys.append(dyg[:, LANES:])

        dcs_c = jnp.zeros((L, LANES), F32)
        dcs_r = jnp.zeros((L, LANES), F32)
        ddt_c = jnp.zeros((L, LANES), F32)
        dxs = []
        dB, dC = [None, None], [None, None]
        last = row == L - 1
        for g in range(2):
            Bg, Cg, G = Bs[g], Cs[g], Gs[g]
            dG = jnp.zeros((L, L), F32)
            dBg = jnp.zeros((L, SSD_STATE), F32)
            dCg = jnp.zeros((L, SSD_STATE), F32)
            for pp in (2 * g, 2 * g + 1):
                h0 = 2 * pp
                y, Sp, (x, xdt, xdt_b, Ms, E, yoff, Fd, el, el_rows) = fw[pp]
                dY = dys[pp]
                dS = dS_s[pp]
                dS_b, Sp_b = dS.astype(BF16), Sp.astype(BF16)
                dD_s[:, pp * LANES:(pp + 1) * LANES] += jnp.sum(dY * x, axis=0, keepdims=True)
                dx = dY * _pair_lanes(dsk_v, h0, first[:1])
                dxdt = jnp.zeros((L, LANES), F32)
                dY_b = dY.astype(BF16)
                for hh, h in enumerate((h0, h0 + 1)):
                    hm = first if hh == 0 else ~first
                    M, lam = Ms[hh]
                    dYh = jnp.where(hm, dY, 0.0).astype(BF16)
                    dM = lax.dot_general(dYh, xdt_b, (((1,), (1,)), ((), ())), preferred_element_type=F32)
                    W = dM * M
                    dcs_c = dcs_c + jnp.where(lane == h, jnp.sum(W, axis=-1, keepdims=True), 0.0)
                    dcs_r = dcs_r + jnp.where(row == h, jnp.sum(W, axis=0, keepdims=True), 0.0)
                    dG = dG + dM * lam
                    mt = lax.dot_general(M.astype(BF16), dY_b, (((0,), (0,)), ((), ())), preferred_element_type=F32)
                    dxdt = dxdt + jnp.where(hm, mt, 0.0)
                dT = (E * dY).astype(BF16)
                dCg = dCg + jnp.dot(dT, Sp_b, preferred_element_type=F32)
                dS_in = lax.dot_general(dT, Cg, (((0,), (0,)), ((), ())), preferred_element_type=F32) + el_rows * dS
                q1 = dY * yoff
                dZ = lax.dot_general(Bg, dS_b, (((1,), (1,)), ((), ())), preferred_element_type=F32)
                dBg = dBg + jnp.dot((xdt * Fd).astype(BF16), dS_b, preferred_element_type=F32)
                dxdt = dxdt + dZ * Fd
                q2 = dZ * xdt * Fd
                dSS = dS * Sp
                for hh, h in enumerate((h0, h0 + 1)):
                    hm = first if hh == 0 else ~first
                    rs1 = jnp.sum(jnp.where(hm, q1, 0.0), axis=-1, keepdims=True)
                    rs2 = jnp.sum(jnp.where(hm, q2, 0.0), axis=-1, keepdims=True)
                    rmask = rowfirst if hh == 0 else ~rowfirst
                    d_el = jnp.sum(jnp.sum(jnp.where(rmask, dSS, 0.0), axis=-1, keepdims=True), axis=0, keepdims=True)
                    tail = jnp.sum(rs2, axis=0, keepdims=True) + d_el * el[:, h:h + 1]
                    dcs_c = dcs_c + jnp.where(lane == h, rs1 - rs2 + jnp.where(last[:, :1], tail, 0.0), 0.0)
                    ddt_c = ddt_c + jnp.where(lane == h, jnp.sum(jnp.where(hm, dxdt * x, 0.0), axis=-1, keepdims=True), 0.0)
                dS_s[pp] = dS_in
                dxs.append(dx + dxdt * _pair_lanes(dtv, h0, first))
            dG_b = dG.astype(BF16)
            dC[g] = dCg + jnp.dot(dG_b, Bg, preferred_element_type=F32)
            dB[g] = dBg + lax.dot_general(dG_b, Cg, (((0,), (0,)), ((), ())), preferred_element_type=F32)

        dcs = dcs_c - dcs_r.T
        da = _cumsum_rows(dcs, reverse=True)
        ddt_c = ddt_c + da * a_row
        dalog_ref[...] += jnp.sum(da * dtv, axis=0, keepdims=True) * a_row
        ddt_raw = ddt_c * _sigmoid(dpre)
        ddt_ref[...] = ddt_raw
        ddtb_ref[...] += jnp.sum(ddt_raw, axis=0, keepdims=True)

        dxa = jnp.concatenate(dxs + dB + dC, axis=1)
        dconv = dxa * (sg * (1.0 + conv * (1.0 - sg)))
        dcb_ref[...] += jnp.sum(dconv, axis=0, keepdims=True)
        nxt = dconv_s[...]
        du = cw_v[SSD_CONV - 1:SSD_CONV, :] * dconv
        dcw_ref[SSD_CONV - 1:SSD_CONV, :] += jnp.sum(dconv * u, axis=0, keepdims=True)
        for s in range(1, SSD_CONV):
            k = SSD_CONV - 1 - s
            du = du + cw_v[k:k + 1, :] * _shift_up(dconv, nxt, s)
            dcw_ref[k:k + 1, :] += jnp.sum(dconv * _shift_down(u, prev, s), axis=0, keepdims=True)
        du_ref[...] = du
        dconv_s[...] = dconv

        @pl.when(i == nc - 1)
        def _():
            acc = dD_s[...]
            lane1 = lax.broadcasted_iota(jnp.int32, (1, LANES), 1)
            lanew = lax.broadcasted_iota(jnp.int32, acc.shape, 1)
            out = jnp.zeros((1, LANES), F32)
            for h in range(SSD_HEADS):
                tot = jnp.sum(jnp.where((lanew >= h * SSD_HEAD_DIM) & (lanew < (h + 1) * SSD_HEAD_DIM), acc, 0.0),
                              axis=-1, keepdims=True)
                out = out + jnp.where(lane1 == h, tot, 0.0)
            ddsk_ref[...] = out

    cidx, zs, us, dts = _ssd_specs(nc, True)
    ups = pl.BlockSpec((L, SSD_XBC), lambda i: (jnp.maximum(cidx(i) - 1, 0), PX // SSD_XBC))
    rowc = lambda w: pl.BlockSpec((L, w), lambda i: (cidx(i), 0))
    return pl.pallas_call(
        body, name="ssd_bwd", grid=(nc,),
        in_specs=[zs, us, ups, dts, pl.BlockSpec((N_SPAIR * LANES, SSD_STATE), lambda i: (cidx(i), 0)), rowc(SSD_INNER),
                  pl.BlockSpec((8, SSD_XBC), lambda i: (0, 0)), _vec(SSD_XBC), _vec(LANES), _vec(LANES), _vec(LANES), _vec(SSD_INNER)],
        out_specs=[rowc(SSD_INNER), rowc(SSD_XBC), rowc(LANES), pl.BlockSpec((8, SSD_XBC), lambda i: (0, 0)), _vec(SSD_XBC),
                   _vec(LANES), _vec(LANES), _vec(LANES), _vec(SSD_INNER)],
        out_shape=[jax.ShapeDtypeStruct((S, SSD_INNER), F32), jax.ShapeDtypeStruct((S, SSD_XBC), F32),
                   jax.ShapeDtypeStruct((S, LANES), F32), jax.ShapeDtypeStruct((8, SSD_XBC), F32),
                   jax.ShapeDtypeStruct((1, SSD_XBC), F32), jax.ShapeDtypeStruct((1, LANES), F32),
                   jax.ShapeDtypeStruct((1, LANES), F32), jax.ShapeDtypeStruct((1, LANES), F32),
                   jax.ShapeDtypeStruct((1, SSD_INNER), F32)],
        scratch_shapes=[pltpu.VMEM((N_SPAIR, LANES, SSD_STATE), F32), pltpu.VMEM((L, SSD_XBC), F32),
                        pltpu.VMEM((1, SSD_INNER), F32)],
        compiler_params=_cparams(("arbitrary",)),
    )(proj, proj, proj, proj, states, dy, cw, cb, dtb, alog, dsk, ng)


_IN_SEGS = ((PZ, 0, 512), (PX, 512, 1024), (PDT, 1536, 8), (PQ, 1544, 384), (PKV, 1928, 256), (PKR + KR_LANE, 2184, 32))


def _pad_w_in(w):
    out = jnp.zeros((w.shape[0], PW), w.dtype)
    for dst, src, n in _IN_SEGS:
        out = lax.dynamic_update_slice(out, w[:, src:src + n], (0, dst))
    return out


def _unpad_w_in(wp):
    segs = sorted(_IN_SEGS, key=lambda t: t[1])
    return jnp.concatenate([wp[:, dst:dst + n] for dst, src, n in segs], axis=1)


def _pad_w_q(w):
    return jnp.pad(w.reshape(MLA_Q_RANK, MLA_HEADS, MLA_QK), ((0, 0), (0, 0), (0, LANES - MLA_QK))).reshape(MLA_Q_RANK, MLA_HEADS * LANES)


def _unpad_w_q(wp):
    return wp.reshape(MLA_Q_RANK, MLA_HEADS, LANES)[:, :, :MLA_QK].reshape(MLA_Q_RANK, MLA_HEADS * MLA_QK)


def _pad_w_kv(w):
    w3 = w.reshape(MLA_KV_RANK, MLA_HEADS, MLA_NOPE + MLA_V)
    k = jnp.pad(w3[:, :, :MLA_NOPE], ((0, 0), (0, 0), (0, LANES - MLA_NOPE))).reshape(MLA_KV_RANK, MLA_HEADS * LANES)
    return jnp.concatenate([k, w3[:, :, MLA_NOPE:].reshape(MLA_KV_RANK, MLA_HEADS * MLA_V)], axis=1)


def _unpad_w_kv(wp):
    k = wp[:, :MLA_HEADS * LANES].reshape(MLA_KV_RANK, MLA_HEADS, LANES)[:, :, :MLA_NOPE]
    v = wp[:, MLA_HEADS * LANES:].reshape(MLA_KV_RANK, MLA_HEADS, MLA_V)
    return jnp.concatenate([k, v], axis=2).reshape(MLA_KV_RANK, MLA_HEADS * (MLA_NOPE + MLA_V))


def _head_lanes(v):
    return jnp.pad(v, ((0, 0), (0, LANES - v.shape[1])))


def _local_step(x, mem, positions, tgt, P, late_weights=None, emit=None):
    tabs = _rope_tables(positions)
    G = {}
    emit = emit or (lambda names, grads: None)

    h0 = _ln_fwd([(x, 1.0)], P["ln_in_g"], P["ln_in_b"], "ln_in")
    proj = _mm(h0, P["w_in"], "nn", "proj_in", tm=1024, tn=640)
    y_ssd, states = _ssd_fwd(proj, P["conv_w"], P["conv_b"], P["dt_bias"], P["a_log"], P["d_skip"], P["ssd_norm_g"])
    qn, kvn, kpe = _mla_prep(proj, P["q_norm_g"], P["kv_norm_g"], tabs)

    def q_epi(acc, c, s1, s2):
        return (jnp.concatenate([_rope_block(acc[:, h * LANES:(h + 1) * LANES], c, s1, s2) for h in range(MLA_HEADS)], axis=1),)

    q_all = _mm(qn, P["w_q_up"], "nn", "q_up", out_dtypes=(BF16,), epi=q_epi, extras=[(t, "m") for t in tabs])

    def kv_epi(acc, kp):
        kb = [acc[:, h * LANES:(h + 1) * LANES] + kp for h in range(MLA_HEADS)]
        return (jnp.concatenate(kb + [acc[:, MLA_HEADS * LANES:]], axis=1),)

    kv_all = _mm(kvn, P["w_kv_up"], "nn", "kv_up", out_dtypes=(BF16,), epi=kv_epi, extras=[(kpe, "m")])
    o_att, lse = _attn_fwd(q_all, kv_all)
    cat = jnp.concatenate([y_ssd, o_att], axis=1)
    mix = _mm(cat, P["w_mix_out"], "nn", "mix_out", tm=1024)
    h1 = _ln_fwd([(h0, ALPHA), (mix, 1.0)], P["ln1_g"], P["ln1_b"], "ln1")
    if late_weights is not None:
        P = {**P, **late_weights(h1)}
    qm = _mm(h1, P["w_mem_q"], "nn", "mem_q", tm=1024, out_dtypes=(BF16,))
    km = _mm(mem, P["w_mem_k"], "nn", "mem_k", out_dtypes=(BF16,))
    vm = _mm(mem, P["w_mem_v"], "nn", "mem_v", out_dtypes=(BF16,))
    om = _mem_attn_fwd(qm, km, vm)
    xa = _mm(om, P["w_mem_o"], "nn", "mem_o", tm=1024)
    h2 = _ln_fwd([(h1, ALPHA), (xa, 1.0)], P["ln2_g"], P["ln2_b"], "ln2")

    def up_epi(acc):
        r = jnp.maximum(acc, 0.0)
        return acc, r * r

    u, act = _mm(h2, P["w_up"], "nn", "mlp_up", tm=1024, tn=1024, out_dtypes=(F32, BF16), epi=up_epi)
    ff = _mm(act, P["w_down"], "nn", "mlp_down", tm=1024, tk=1024)

    loss, dr3, G["ln3_g"], G["ln3_b"] = _ln_loss_bwd([(h2, ALPHA), (ff, 1.0)], P["ln3_g"], P["ln3_b"], tgt, "ln3_loss")

    def dact_epi(acc, uu):
        return (acc * (2.0 * jnp.maximum(uu, 0.0)),)

    du = _mm(dr3, P["w_down"], "nt", "mlp_down_dx", tm=1024, tn=1024, out_dtypes=(BF16,), epi=dact_epi, extras=[(u, "mn")])
    G["w_down"] = _mm(act, dr3, "tn", "mlp_down_dw", tm=1024, tk=512)
    G["w_up"] = _mm(h2, du, "tn", "mlp_up_dw", tm=1024, tn=D_FF // N_DEV, tk=512, col_slots=True)
    emit(("w_down", "w_up"), G)
    dh2_ff = _mm(du, P["w_up"], "nt", "mlp_up_dx", tm=1024, tk=1024)
    dr2, G["ln2_g"], G["ln2_b"] = _ln_bwd([(h1, ALPHA), (xa, 1.0)], [(dr3, ALPHA), (dh2_ff, 1.0)], P["ln2_g"], "ln2_bwd")

    dom = _mm(dr2, P["w_mem_o"], "nt", "mem_o_dx", tm=1024, out_dtypes=(BF16,))
    G["w_mem_o"] = _mm(om, dr2, "tn", "mem_o_dw", tm=1024, tk=512)
    dqm, dkm, dvm = _mem_attn_bwd(qm, km, vm, dom)
    G["w_mem_q"] = _mm(h1, dqm, "tn", "mem_q_dw", tm=1024, tk=512)
    G["w_mem_k"] = _mm(mem, dkm, "tn", "mem_k_dw", tm=1024)
    G["w_mem_v"] = _mm(mem, dvm, "tn", "mem_v_dw", tm=1024)
    emit(("w_mem_o", "w_mem_q", "w_mem_k", "w_mem_v"), G)
    dh1_q = _mm(dqm, P["w_mem_q"], "nt", "mem_q_dx", tm=1024)
    dr1, G["ln1_g"], G["ln1_b"] = _ln_bwd([(h0, ALPHA), (mix, 1.0)], [(dr2, ALPHA), (dh1_q, 1.0)], P["ln1_g"], "ln1_bwd")

    dcat = _mm(dr1, P["w_mix_out"], "nt", "mix_out_dx", tm=1024)
    G["w_mix_out"] = _mm(cat, dr1, "tn", "mix_out_dw", tm=1024, tk=512)
    dq_all, dk_all, dv_all = _attn_bwd(q_all, kv_all, dcat, lse, _attn_delta(dcat, o_att))
    dq_pre = _rope_bwd_all(dq_all, tabs)
    G["w_q_up"] = _mm(qn, dq_pre, "tn", "q_up_dw", tk=512)
    dqn = _mm(dq_pre, P["w_q_up"], "nt", "q_up_dx", tm=1024)
    dkv_all = jnp.concatenate([dk_all, dv_all], axis=1)
    G["w_kv_up"] = _mm(kvn, dkv_all, "tn", "kv_up_dw", tk=512)
    dkvn = _mm(dkv_all, P["w_kv_up"], "nt", "kv_up_dx", tm=1024)
    dql, dkvl, dkr, G["q_norm_g"], G["kv_norm_g"] = _mla_prep_bwd(proj, P["q_norm_g"], P["kv_norm_g"], tabs, dqn, dkvn, dk_all)
    (dz, dxbc, ddt, G["conv_w"], G["conv_b"], G["dt_bias"], G["a_log"], G["d_skip"], G["ssd_norm_g"]) = _ssd_bwd(
        proj, states, dcat, P["conv_w"], P["conv_b"], P["dt_bias"], P["a_log"], P["d_skip"], P["ssd_norm_g"])
    emit(("w_mix_out", "w_q_up", "w_kv_up", "conv_w"), G)
    S = x.shape[0]
    dproj = jnp.concatenate([dql, jnp.zeros((S, PZ - MLA_Q_RANK), F32), dz, dxbc, dkvl, ddt, dkr], axis=1)
    G["w_in"] = _mm(h0, dproj, "tn", "proj_in_dw", tm=1024, tn=640, tk=512)
    emit(("w_in",), G)
    dh0_p = _mm(dproj, P["w_in"], "nt", "proj_in_dx", tm=1024, tk=640)
    gx, G["ln_in_g"], G["ln_in_b"] = _ln_bwd([(x, 1.0)], [(dr1, ALPHA), (dh0_p, 1.0)], P["ln_in_g"], "ln_in_bwd")
    return loss, gx, G


PACK_W = 1024
BIG = (("w_in", (1024, 277), 1), ("conv_w", (4, 128), 1), ("w_q_up", (384, 96), 1), ("w_kv_up", (256, 128), 1),
       ("w_mix_out", (128, 1024), 0), ("w_mem_q", (128, 1024), 0), ("w_mem_k", (128, 1024), 0), ("w_mem_v", (128, 1024), 0),
       ("w_mem_o", (128, 1024), 0), ("w_up", (1024, 512), 1), ("w_down", (512, 1024), 0))
SMALL = ("ln_in_g", "ln_in_b", "conv_b", "ln1_g", "ln1_b", "ln2_g", "ln2_b", "ln3_g", "ln3_b",
         "ssd_norm_g", "q_norm_g", "kv_norm_g", "dt_bias", "a_log", "d_skip")
ALL_W = ("ln_in_g", "ln_in_b", "w_in", "conv_w", "conv_b", "dt_bias", "a_log", "d_skip", "ssd_norm_g", "q_norm_g", "w_q_up",
         "kv_norm_g", "w_kv_up", "w_mix_out", "ln1_g", "ln1_b", "w_mem_q", "w_mem_k", "w_mem_v", "w_mem_o", "ln2_g", "ln2_b",
         "w_up", "w_down", "ln3_g", "ln3_b")


def _rows_of(shape):
    return -(-(shape[0] * shape[1]) // PACK_W)


BIG_ROWS = sum(_rows_of(s) for _, s, _ in BIG)
BIG_R = -(-BIG_ROWS // 32) * 32
SMALL_R = 16
LOSS_ROW = 15
ADAM_TILE = 96
assert BIG_R % ADAM_TILE == 0


def _flat_rows(a, rows):
    lead = a.shape[:-2]
    f = a.reshape(lead + (-1,))
    f = jnp.pad(f, [(0, 0)] * len(lead) + [(0, rows * PACK_W - f.shape[-1])])
    return f.reshape(lead + (rows, PACK_W))


def _pack_big(arrs, dtype):
    parts = []
    for n, s, _ in BIG:
        if n == "conv_w" and dtype == BF16:
            parts.append(lax.bitcast_convert_type(arrs[n].astype(F32), BF16).reshape(1, PACK_W))
        else:
            parts.append(_flat_rows(arrs[n].astype(dtype), _rows_of(s)))
    parts.append(jnp.zeros((BIG_R - BIG_ROWS, PACK_W), dtype))
    return jnp.concatenate(parts, axis=0)


def _unpack_big(buf):
    out, r = {}, 0
    for n, s, _ in BIG:
        k = _rows_of(s)
        out[n] = buf[r:r + k].reshape(-1)[:s[0] * s[1]].reshape(s)
        r += k
    return out


def _unpack_gathered(buf):
    out, r = {}, 0
    for n, s, ax in BIG:
        k = _rows_of(s)
        if n == "conv_w":
            sh = lax.bitcast_convert_type(buf[:, r:r + k].reshape((N_DEV,) + s + (2,)), F32)
        else:
            sh = buf[:, r:r + k].reshape(N_DEV, -1)[:, :s[0] * s[1]].reshape((N_DEV,) + s)
        out[n] = sh.reshape(N_DEV * s[0], s[1]) if ax == 0 else sh.transpose(1, 0, 2).reshape(s[0], N_DEV * s[1])
        r += k
    return out


def _pack_grads(G):
    parts = []
    for n, s, ax in BIG:
        g = G[n]
        sh = g.reshape((N_DEV,) + s) if ax == 0 else g.reshape(s[0], N_DEV, s[1]).transpose(1, 0, 2)
        parts.append(_flat_rows(sh, _rows_of(s)))
    parts.append(jnp.zeros((N_DEV, BIG_R - BIG_ROWS, PACK_W), F32))
    return jnp.concatenate(parts, axis=1)


_SMALL_ROWS = (("ln_in_g",), ("ln_in_b",), ("conv_b",), ("ln1_g",), ("ln1_b",), ("ln2_g",), ("ln2_b",), ("ln3_g",), ("ln3_b",),
               ("ssd_norm_g", "q_norm_g"), ("kv_norm_g", "dt_bias", "a_log", "d_skip"))
_SMALL_W = {"ssd_norm_g": 512, "q_norm_g": 384, "kv_norm_g": 256, "dt_bias": LANES, "a_log": LANES, "d_skip": LANES}


def _pack_small(V, extra_row=None):
    rows = []
    for names in _SMALL_ROWS:
        r = jnp.concatenate([V[n] for n in names], axis=1)
        rows.append(jnp.pad(r, ((0, 0), (0, PACK_W - r.shape[1]))))
    rows.append(jnp.zeros((SMALL_R - len(rows) - 1, PACK_W), F32))
    rows.append(jnp.zeros((1, PACK_W), F32) if extra_row is None else extra_row)
    return jnp.concatenate(rows, axis=0)


def _unpack_small(buf):
    out = {}
    for i, names in enumerate(_SMALL_ROWS):
        c = 0
        for n in names:
            w = _SMALL_W.get(n, PACK_W)
            out[n] = buf[i:i + 1, c:c + w]
            c += w
    return out


MESH = pl.DeviceIdType.MESH
ANY = pl.BlockSpec(memory_space=pl.ANY)
VM = pl.BlockSpec(memory_space=pltpu.VMEM)


def _coords():
    return lax.axis_index("x"), lax.axis_index("y"), lax.axis_index("c")


def _slot(px, py, pc):
    return 4 * px + 2 * py + pc


def _gather_big(shard):
    R, W = shard.shape

    def body(x_ref, out_ref, send_sems, recv_sems, local_sem):
        x, y, c = _coords()
        me, sibling = (x, y, c), (x, y, 1 - c)
        chips = [(1 - x, y), (x, 1 - y), (1 - x, 1 - y)]

        def rows(px, py, pc):
            return out_ref.at[_slot(px, py, pc)]

        def copy(k, block, to, src=None):
            return pltpu.make_async_remote_copy(
                src_ref=rows(*block) if src is None else src, dst_ref=rows(*block),
                send_sem=send_sems.at[k], recv_sem=recv_sems.at[k], device_id=to, device_id_type=MESH)

        mine = pltpu.make_async_copy(x_ref, rows(*me), local_sem)
        mine.start()
        first = [copy(0, me, sibling, src=x_ref)]
        first += [copy(1 + j, me, (*chip, c), src=x_ref) for j, chip in enumerate(chips)]
        for cp in first:
            cp.start()
        passed = [copy(4 + j, (*chip, c), sibling) for j, chip in enumerate(chips)]
        for j, chip in enumerate(chips):
            copy(1 + j, (*chip, c), me).wait_recv()
            passed[j].start()
        copy(0, sibling, me).wait_recv()
        for j, chip in enumerate(chips):
            copy(4 + j, (*chip, 1 - c), me).wait_recv()
        for cp in first + passed:
            cp.wait_send()
        mine.wait()

    return pl.pallas_call(
        body, name="gather_weights", out_shape=jax.ShapeDtypeStruct((N_DEV, R, W), shard.dtype),
        in_specs=[ANY], out_specs=ANY,
        scratch_shapes=[pltpu.SemaphoreType.DMA((7,)), pltpu.SemaphoreType.DMA((7,)), pltpu.SemaphoreType.DMA],
    )(shard)


def _peer(k, x, y, c):
    dx, dy, dc = (k >> 2) & 1, (k >> 1) & 1, k & 1
    return (1 - x if dx else x, 1 - y if dy else y, 1 - c if dc else c)


def _scatter_grads(gpack):
    _, R, W = gpack.shape

    def body(g_ref, out_ref, send_sems, recv_sems, local_sem):
        x, y, c = _coords()
        my = _slot(x, y, c)
        mine = pltpu.make_async_copy(g_ref.at[my], out_ref.at[my], local_sem)
        mine.start()
        cps = []
        for k in range(1, N_DEV):
            peer = _peer(k, x, y, c)
            cps.append(pltpu.make_async_remote_copy(
                src_ref=g_ref.at[_slot(*peer)], dst_ref=out_ref.at[my],
                send_sem=send_sems.at[k - 1], recv_sem=recv_sems.at[k - 1], device_id=peer, device_id_type=MESH))
        for cp in cps:
            cp.start()
        for cp in cps:
            cp.wait_recv()
        for cp in cps:
            cp.wait_send()
        mine.wait()

    return pl.pallas_call(
        body, name="scatter_grads", out_shape=jax.ShapeDtypeStruct(gpack.shape, gpack.dtype),
        in_specs=[ANY], out_specs=ANY,
        scratch_shapes=[pltpu.SemaphoreType.DMA((7,)), pltpu.SemaphoreType.DMA((7,)), pltpu.SemaphoreType.DMA],
    )(gpack)


def _adam(w, g, m, v):
    m = ADAM_B1 * m + (1.0 - ADAM_B1) * g
    v = ADAM_B2 * v + (1.0 - ADAM_B2) * (g * g)
    m_hat = m / (1.0 - ADAM_B1 ** ADAM_STEP)
    v_hat = v / (1.0 - ADAM_B2 ** ADAM_STEP)
    delta = -ADAM_LR * (m_hat / (jnp.sqrt(v_hat) + ADAM_EPS) + ADAM_WD * w)
    return delta, m, v


def _sum_slots(ref):
    tot = ref[0]
    for q in range(1, N_DEV):
        tot = tot + ref[q]
    return tot


def _reduce_adam_big(recv, w, m, v):
    _, R, W = recv.shape

    def body(r_ref, w_ref, m_ref, v_ref, g_ref, d_ref, nm_ref, nv_ref):
        g = _sum_slots(r_ref)
        g_ref[...] = g
        d_ref[...], nm_ref[...], nv_ref[...] = _adam(w_ref[...], g, m_ref[...], v_ref[...])

    row = pl.BlockSpec((ADAM_TILE, W), lambda i: (i, 0))
    return pl.pallas_call(
        body, name="reduce_adam", grid=(R // ADAM_TILE,),
        in_specs=[pl.BlockSpec((N_DEV, ADAM_TILE, W), lambda i: (0, i, 0)), row, row, row], out_specs=[row] * 4,
        out_shape=[jax.ShapeDtypeStruct((R, W), F32)] * 4, compiler_params=_cparams(("parallel",)),
    )(recv, w, m, v)


def _allreduce_adam_small(gs, w, m, v):
    R, W = gs.shape

    def body(g_ref, w_ref, m_ref, v_ref, go_ref, d_ref, nm_ref, nv_ref, land, send_sems, recv_sems):
        x, y, c = _coords()
        my = _slot(x, y, c)
        cps = []
        for k in range(1, N_DEV):
            peer = _peer(k, x, y, c)
            cps.append(pltpu.make_async_remote_copy(
                src_ref=g_ref, dst_ref=land.at[my], send_sem=send_sems.at[k - 1], recv_sem=recv_sems.at[k - 1],
                device_id=peer, device_id_type=MESH))
        for cp in cps:
            cp.start()
        land[my] = g_ref[...]
        for cp in cps:
            cp.wait_recv()
        for cp in cps:
            cp.wait_send()
        g = _sum_slots(land)
        go_ref[...] = g
        d_ref[...], nm_ref[...], nv_ref[...] = _adam(w_ref[...], g, m_ref[...], v_ref[...])

    return pl.pallas_call(
        body, name="allreduce_small", in_specs=[VM] * 4, out_specs=[VM] * 4,
        out_shape=[jax.ShapeDtypeStruct((R, W), F32)] * 4,
        scratch_shapes=[pltpu.VMEM((N_DEV, R, W), F32), pltpu.SemaphoreType.DMA((7,)), pltpu.SemaphoreType.DMA((7,))],
    )(gs, w, m, v)


def _row(v):
    return v.reshape(1, -1).astype(F32)


def _step(x, mem, positions, tgt, W, M, V):
    shard = {n: W[n][0] for n, _, _ in BIG}
    full = _unpack_gathered(_gather_big(_pack_big(shard, BF16)))
    P = {n: _row(W[n]) for n in SMALL}
    for n in ("dt_bias", "a_log", "d_skip"):
        P[n] = _head_lanes(P[n])
    P.update(w_in=_pad_w_in(full["w_in"]), w_q_up=_pad_w_q(full["w_q_up"]), w_kv_up=_pad_w_kv(full["w_kv_up"]),
             conv_w=jnp.pad(full["conv_w"].astype(F32), ((0, 8 - SSD_CONV), (0, 0))))
    for n in ("w_mix_out", "w_mem_q", "w_mem_k", "w_mem_v", "w_mem_o", "w_up", "w_down"):
        P[n] = full[n]

    loss, gx, G = _local_step(x[0], mem[0], positions[0], tgt[0], P)

    G["w_in"], G["w_q_up"], G["w_kv_up"] = _unpad_w_in(G["w_in"]), _unpad_w_q(G["w_q_up"]), _unpad_w_kv(G["w_kv_up"])
    G["conv_w"] = G["conv_w"][:SSD_CONV]
    recv = _scatter_grads(_pack_grads(G))
    big = [_unpack_big(b) for b in _reduce_adam_big(
        recv, _pack_big(shard, F32), _pack_big({n: M[n][0] for n, _, _ in BIG}, F32), _pack_big({n: V[n][0] for n, _, _ in BIG}, F32))]

    def small_rows(D):
        rows = {n: _row(D[n]) for n in SMALL}
        for n in ("dt_bias", "a_log", "d_skip"):
            rows[n] = _head_lanes(rows[n])
        return rows

    loss_row = jnp.broadcast_to(loss[:, :1], (1, PACK_W))
    small_bufs = _allreduce_adam_small(
        _pack_small({n: G[n] for n in SMALL}, loss_row), _pack_small(small_rows(W)), _pack_small(small_rows(M)),
        _pack_small(small_rows(V)))
    loss_tot = small_bufs[0][LOSS_ROW, 0]
    small = [_unpack_small(b) for b in small_bufs]

    outs = []
    for res_b, res_s in zip(big, small):
        for n in ALL_W:
            if n in res_b:
                outs.append(res_b[n].reshape(W[n].shape))
            else:
                outs.append(res_s[n][:, :W[n].size].reshape(W[n].shape))
    return (loss_tot, gx[None], *outs)


def kernel(x, mem, positions, ln_in_g, ln_in_b, w_in, conv_w, conv_b, dt_bias, a_log, d_skip, ssd_norm_g, q_norm_g, w_q_up, kv_norm_g, w_kv_up, w_mix_out, ln1_g, ln1_b, w_mem_q, w_mem_k, w_mem_v, w_mem_o, ln2_g, ln2_b, w_up, w_down, ln3_g, ln3_b, loss_target, m_ln_in_g, m_ln_in_b, m_w_in, m_conv_w, m_conv_b, m_dt_bias, m_a_log, m_d_skip, m_ssd_norm_g, m_q_norm_g, m_w_q_up, m_kv_norm_g, m_w_kv_up, m_w_mix_out, m_ln1_g, m_ln1_b, m_w_mem_q, m_w_mem_k, m_w_mem_v, m_w_mem_o, m_ln2_g, m_ln2_b, m_w_up, m_w_down, m_ln3_g, m_ln3_b, v_ln_in_g, v_ln_in_b, v_w_in, v_conv_w, v_conv_b, v_dt_bias, v_a_log, v_d_skip, v_ssd_norm_g, v_q_norm_g, v_w_q_up, v_kv_norm_g, v_w_kv_up, v_w_mix_out, v_ln1_g, v_ln1_b, v_w_mem_q, v_w_mem_k, v_w_mem_v, v_w_mem_o, v_ln2_g, v_ln2_b, v_w_up, v_w_down, v_ln3_g, v_ln3_b):
    a = dict(locals())
    W = {n: a[n] for n in ALL_W}
    M = {n: a["m_" + n] for n in ALL_W}
    V = {n: a["v_" + n] for n in ALL_W}
    return _step_overlapped(x, mem, positions, loss_target, W, M, V)


HBM = pl.BlockSpec(memory_space=pltpu.HBM)
SEM = pl.BlockSpec(memory_space=pltpu.SEMAPHORE)
EFFECT = pltpu.SideEffectType.DATAFLOW_SIDE_EFFECTING
SHARD_SHAPE = {n: s for n, s, _ in BIG}
SHARD_AXIS = {n: ax for n, _, ax in BIG}
GATHER_NOW = ("w_in", "conv_w", "w_q_up", "w_kv_up", "w_mix_out")
GATHER_LATE = ("w_mem_q", "w_mem_k", "w_mem_v", "w_mem_o", "w_up", "w_down")


def _my_slot():
    return _slot(*_coords())


def _group_copies(src_refs, land_refs, send_sems, recv_sems, slotted, landing_of_peer):
    x, y, c = _coords()
    my = _slot(x, y, c)
    cps = []
    for a, (s_ref, l_ref) in enumerate(zip(src_refs, land_refs)):
        for k in range(1, N_DEV):
            peer = _peer(k, x, y, c)
            cps.append(pltpu.make_async_remote_copy(
                src_ref=s_ref.at[_slot(*peer)] if slotted else s_ref,
                dst_ref=l_ref.at[_slot(*peer)] if landing_of_peer else l_ref.at[my],
                send_sem=send_sems.at[7 * a + k - 1], recv_sem=recv_sems.at[7 * a + k - 1],
                device_id=peer, device_id_type=MESH))
    return cps


def _send_start(srcs, lands, slotted, name):
    n = len(srcs)

    def body(*refs):
        for cp in _group_copies(refs[:n], refs[n:2 * n], refs[2 * n], refs[2 * n + 1], slotted, False):
            cp.start()
        refs[-1][...] = jnp.zeros(refs[-1].shape, F32)

    res = pl.pallas_call(
        body, name=name,
        out_shape=(pltpu.SemaphoreType.DMA((7 * n,)), pltpu.SemaphoreType.DMA((7 * n,)),
                   *[pltpu.HBM(a.shape, a.dtype) for a in srcs], *[pltpu.HBM(a.shape, a.dtype) for a in lands],
                   jax.ShapeDtypeStruct((8, LANES), F32)),
        in_specs=[HBM] * (2 * n), out_specs=(SEM, SEM, *[HBM] * (2 * n), VM),
        input_output_aliases={i: 2 + i for i in range(2 * n)},
        compiler_params=pltpu.CompilerParams(has_side_effects=EFFECT),
    )(*[pltpu.with_memory_space_constraint(a, pltpu.HBM) for a in list(srcs) + list(lands)])
    return res[0], res[1], res[2:2 + n], res[2 + n:2 + 2 * n]


def _send_wait(started, after, slotted, name):
    send_sems, recv_sems, srcs, lands = started
    n = len(srcs)

    def body(*refs):
        for cp in _group_copies(refs[:n], refs[n:2 * n], refs[2 * n], refs[2 * n + 1], slotted, True):
            cp.wait_send()
            cp.wait_recv()

    res = pl.pallas_call(
        body, name=name, out_shape=tuple(pltpu.HBM(a.shape, a.dtype) for a in list(srcs) + list(lands)),
        in_specs=[HBM] * (2 * n) + [SEM, SEM, ANY], out_specs=tuple([HBM] * (2 * n)),
        input_output_aliases={i: i for i in range(2 * n)},
        compiler_params=pltpu.CompilerParams(has_side_effects=EFFECT),
    )(*srcs, *lands, send_sems, recv_sems, after)
    return res[n:]


def _landing(own, my):
    return lax.dynamic_update_slice(jnp.zeros((N_DEV,) + own.shape, own.dtype), own[None], (my,) + (0,) * own.ndim)


def _gather_now(shards):
    n = len(shards)

    def body(*refs):
        x_refs, out_refs = refs[:n], refs[n:2 * n]
        send_sems, recv_sems, local_sems = refs[2 * n:]
        x, y, c = _coords()
        me, sibling = (x, y, c), (x, y, 1 - c)
        chips = [(1 - x, y), (x, 1 - y), (1 - x, 1 - y)]

        def copy(a, k, block, to, src=None):
            rows = out_refs[a].at[_slot(*block)]
            return pltpu.make_async_remote_copy(
                src_ref=rows if src is None else src, dst_ref=rows,
                send_sem=send_sems.at[7 * a + k], recv_sem=recv_sems.at[7 * a + k], device_id=to, device_id_type=MESH)

        mine = [pltpu.make_async_copy(x_refs[a], out_refs[a].at[_slot(*me)], local_sems.at[a]) for a in range(n)]
        for cp in mine:
            cp.start()
        first = []
        for a in range(n):
            first.append(copy(a, 0, me, sibling, src=x_refs[a]))
            first += [copy(a, 1 + j, me, (*chip, c), src=x_refs[a]) for j, chip in enumerate(chips)]
        for cp in first:
            cp.start()
        passed = []
        for j, chip in enumerate(chips):
            for a in range(n):
                copy(a, 1 + j, (*chip, c), me).wait_recv()
                fwd = copy(a, 4 + j, (*chip, c), sibling)
                fwd.start()
                passed.append(fwd)
        for a in range(n):
            copy(a, 0, sibling, me).wait_recv()
            for j, chip in enumerate(chips):
                copy(a, 4 + j, (*chip, 1 - c), me).wait_recv()
        for cp in first + passed:
            cp.wait_send()
        for cp in mine:
            cp.wait()

    return pl.pallas_call(
        body, name="gather_now", out_shape=[jax.ShapeDtypeStruct((N_DEV,) + s.shape, s.dtype) for s in shards],
        in_specs=[ANY] * n, out_specs=[ANY] * n,
        scratch_shapes=[pltpu.SemaphoreType.DMA((7 * n,)), pltpu.SemaphoreType.DMA((7 * n,)), pltpu.SemaphoreType.DMA((n,))],
    )(*shards)


def _full_from_slots(name, slots):
    a, b = SHARD_SHAPE[name]
    return slots.reshape(N_DEV * a, b) if SHARD_AXIS[name] == 0 else slots.transpose(1, 0, 2).reshape(a, N_DEV * b)


def _slots_from_full(name, g):
    a, b = SHARD_SHAPE[name]
    return g.reshape(N_DEV, a, b) if SHARD_AXIS[name] == 0 else g.reshape(a, N_DEV, b).transpose(1, 0, 2)


def _reduce_adam(recv, w, m, v, name):
    _, a, b = recv.shape
    ta = a
    while ta * b * 4 * N_DEV > 4 * 1024 * 1024 and ta % 16 == 0:
        ta //= 2

    def body(r_ref, w_ref, m_ref, v_ref, g_ref, d_ref, nm_ref, nv_ref):
        g = _sum_slots(r_ref)
        g_ref[...] = g
        d_ref[...], nm_ref[...], nv_ref[...] = _adam(w_ref[...], g, m_ref[...], v_ref[...])

    row = pl.BlockSpec((ta, b), lambda i: (i, 0))
    return pl.pallas_call(
        body, name=name, grid=(a // ta,),
        in_specs=[pl.BlockSpec((N_DEV, ta, b), lambda i: (0, i, 0)), row, row, row], out_specs=[row] * 4,
        out_shape=[jax.ShapeDtypeStruct((a, b), F32)] * 4, compiler_params=_cparams(("parallel",)),
    )(recv, w, m, v)


def _step_overlapped(x, mem, positions, tgt, W, M, V):
    my = _my_slot()
    shard = {n: W[n][0] for n, _, _ in BIG}
    send = {n: (shard[n] if n == "conv_w" else shard[n].astype(BF16)) for n in shard}

    now = dict(zip(GATHER_NOW, _gather_now([send[n] for n in GATHER_NOW])))
    late_src = [send[n] for n in GATHER_LATE]
    late_src, _ = lax.optimization_barrier((late_src, now["w_mix_out"]))
    late = _send_start(late_src, [_landing(s, my) for s in late_src], False, "gather_late_start")

    P = {n: _row(W[n]) for n in SMALL}
    for n in ("dt_bias", "a_log", "d_skip"):
        P[n] = _head_lanes(P[n])
    full = {n: _full_from_slots(n, now[n]) for n in GATHER_NOW}
    P.update(w_in=_pad_w_in(full["w_in"]), w_q_up=_pad_w_q(full["w_q_up"]), w_kv_up=_pad_w_kv(full["w_kv_up"]),
             conv_w=jnp.pad(full["conv_w"], ((0, 8 - SSD_CONV), (0, 0))), w_mix_out=full["w_mix_out"])

    def late_weights(after):
        lands = _send_wait(late, after, False, "gather_late_wait")
        return {n: _full_from_slots(n, l) for n, l in zip(GATHER_LATE, lands)}

    started = []

    def emit(names, G):
        srcs = []
        for n in names:
            g = G[n]
            if n == "w_in":
                g = _unpad_w_in(g)
            elif n == "w_q_up":
                g = _unpad_w_q(g)
            elif n == "w_kv_up":
                g = _unpad_w_kv(g)
            elif n == "conv_w":
                g = g[:SSD_CONV]
            srcs.append(g if g.ndim == 3 else _slots_from_full(n, g))
        lands = [_landing(lax.dynamic_index_in_dim(s, my, 0, keepdims=False), my) for s in srcs]
        started.append((names, _send_start(srcs, lands, True, "scatter_start_%d" % len(started))))

    loss, gx, G = _local_step(x[0], mem[0], positions[0], tgt[0], P, late_weights, emit)

    res = {}
    for i, (names, st) in enumerate(started):
        lands = _send_wait(st, gx, True, "scatter_wait_%d" % i)
        for n, recv in zip(names, lands):
            res[n] = _reduce_adam(recv, shard[n], M[n][0], V[n][0], "reduce_adam_" + n)

    def small_rows(D):
        rows = {n: _row(D[n]) for n in SMALL}
        for n in ("dt_bias", "a_log", "d_skip"):
            rows[n] = _head_lanes(rows[n])
        return rows

    loss_row = jnp.broadcast_to(loss[:, :1], (1, PACK_W))
    small_bufs = _allreduce_adam_small(
        _pack_small({n: G[n] for n in SMALL}, loss_row), _pack_small(small_rows(W)), _pack_small(small_rows(M)),
        _pack_small(small_rows(V)))
    loss_tot = small_bufs[0][LOSS_ROW, 0]
    small = [_unpack_small(b) for b in small_bufs]

    outs = []
    for j in range(4):
        for n in ALL_W:
            if n in res:
                outs.append(res[n][j].reshape(W[n].shape))
            else:
                outs.append(small[j][n][:, :W[n].size].reshape(W[n].shape))
    return (loss_tot, gx[None], *outs)
```

```python
import functools
import math

import jax
import jax.numpy as jnp
from jax import lax
from jax.experimental import pallas as pl
from jax.experimental.pallas import tpu as pltpu

F32, BF16 = jnp.float32, jnp.bfloat16

N_DEV = 8
D_MODEL = 1024
SSD_HEADS, SSD_HEAD_DIM, SSD_INNER, SSD_STATE, SSD_CONV, SSD_CHUNK = 8, 64, 512, 128, 4, 128
SSD_XBC = 1024
MLA_HEADS, MLA_NOPE, MLA_ROPE, MLA_QK, MLA_V = 8, 64, 32, 96, 64
MLA_Q_RANK, MLA_KV_RANK = 384, 256
ROPE_THETA = 10000.0
MEM_HEADS, MEM_HEAD_DIM = 4, 256
D_FF = 4096
IN_WIDTH = 2216
LN_EPS, RMS_EPS = 1e-5, 1e-6
ALPHA = 2.0 ** 0.25
ADAM_LR, ADAM_B1, ADAM_B2, ADAM_EPS, ADAM_WD, ADAM_STEP = 0.001, 0.9, 0.999, 1e-08, 0.01, 10

LANES = 128
NEG = -1e30
VMEM_LIMIT = 56 * 1024 * 1024

PQ, PZ, PX, PKV, PDT, PKR, PW = 0, 512, 1024, 2048, 2304, 2432, 2560
KR_LANE = 64


def _cparams(sem):
    return pltpu.CompilerParams(dimension_semantics=sem, vmem_limit_bytes=VMEM_LIMIT)


def _sigmoid(x):
    return 1.0 / (1.0 + jnp.exp(-x))


def _mm(a, b, mode, name, *, tm=512, tn=None, tk=None, out_dtypes=(F32,), epi=None, extras=(), col_slots=False):
    if mode == "nn":
        (M, K), (K2, N) = a.shape, b.shape
    elif mode == "nt":
        (M, K), (N, K2) = a.shape, b.shape
    else:
        (K, M), (K2, N) = a.shape, b.shape
    assert K == K2, (name, a.shape, b.shape)
    tm, tn, tk = min(tm, M), min(tn or N, N), min(tk or K, K)
    assert M % tm == 0 and N % tn == 0 and K % tk == 0, (name, M, N, K, tm, tn, tk)
    gk = K // tk
    a_spec = pl.BlockSpec((tk, tm), lambda i, j, k: (k, i)) if mode == "tn" else pl.BlockSpec((tm, tk), lambda i, j, k: (i, k))
    b_spec = pl.BlockSpec((tn, tk), lambda i, j, k: (j, k)) if mode == "nt" else pl.BlockSpec((tk, tn), lambda i, j, k: (k, j))
    dims = {"nn": ((1,), (0,)), "nt": ((1,), (1,)), "tn": ((0,), (0,))}[mode]
    ex_specs = []
    for arr, kind in extras:
        if kind == "mn":
            ex_specs.append(pl.BlockSpec((tm, tn), lambda i, j, k: (i, j)))
        elif kind == "n":
            ex_specs.append(pl.BlockSpec((1, tn), lambda i, j, k: (0, j)))
        else:
            ex_specs.append(pl.BlockSpec((tm, arr.shape[1]), lambda i, j, k: (i, 0)))
    ne, no = len(extras), len(out_dtypes)

    def body(*refs):
        a_ref, b_ref = refs[0], refs[1]
        ex, outs = refs[2:2 + ne], refs[2 + ne:2 + ne + no]
        part = lax.dot_general(a_ref[...].astype(BF16), b_ref[...].astype(BF16), (dims, ((), ())),
                               preferred_element_type=F32)

        def finish(acc):
            res = epi(acc, *[e[...] for e in ex]) if epi is not None else (acc,)
            for o, r in zip(outs, res):
                o[...] = r.astype(o.dtype)

        if gk == 1:
            finish(part)
        else:
            acc_ref = refs[-1]
            k = pl.program_id(2)

            @pl.when(k == 0)
            def _():
                acc_ref[...] = part

            @pl.when(k > 0)
            def _():
                acc_ref[...] += part

            @pl.when(k == gk - 1)
            def _():
                finish(acc_ref[...])

    res = pl.pallas_call(
        body, name=name, grid=(M // tm, N // tn, gk),
        in_specs=[a_spec, b_spec] + ex_specs,
        out_specs=[pl.BlockSpec((None, tm, tn), lambda i, j, k: (j, i, 0)) if col_slots else pl.BlockSpec((tm, tn), lambda i, j, k: (i, j))
                   for _ in out_dtypes],
        out_shape=[jax.ShapeDtypeStruct((N // tn, M, tn) if col_slots else (M, N), dt) for dt in out_dtypes],
        scratch_shapes=[pltpu.VMEM((tm, tn), F32)] if gk > 1 else [],
        compiler_params=_cparams(("parallel", "parallel", "arbitrary")),
    )(a, b, *[e[0] for e in extras])
    return res[0] if no == 1 else res


def _ln_stats(r):
    mu = jnp.mean(r, axis=-1, keepdims=True)
    xc = r - mu
    var = jnp.mean(xc * xc, axis=-1, keepdims=True)
    rstd = lax.rsqrt(var + LN_EPS)
    return xc * rstd, rstd


def _ln_fwd(terms, g, b, name, tm=512):
    S, D = terms[0][0].shape
    coefs = [c for _, c in terms]
    nt = len(terms)

    def body(*refs):
        r = sum(c * t[...] for t, c in zip(refs[:nt], coefs))
        xh, _ = _ln_stats(r)
        refs[nt + 2][...] = xh * refs[nt][...] + refs[nt + 1][...]

    row = pl.BlockSpec((tm, D), lambda i: (i, 0))
    vec = pl.BlockSpec((1, D), lambda i: (0, 0))
    return pl.pallas_call(
        body, name=name, grid=(S // tm,), in_specs=[row] * nt + [vec, vec], out_specs=row,
        out_shape=jax.ShapeDtypeStruct((S, D), F32), compiler_params=_cparams(("parallel",)),
    )(*[t for t, _ in terms], g, b)


def _ln_bwd(terms, dterms, g, name, tm=512):
    S, D = terms[0][0].shape
    coefs, dcoefs = [c for _, c in terms], [c for _, c in dterms]
    nt, nd = len(terms), len(dterms)

    def body(*refs):
        i = pl.program_id(0)
        r = sum(c * t[...] for t, c in zip(refs[:nt], coefs))
        dh = sum(c * t[...].astype(F32) for t, c in zip(refs[nt:nt + nd], dcoefs))
        g_ref = refs[nt + nd]
        dr_ref, dg_ref, db_ref = refs[nt + nd + 1:]
        xh, rstd = _ln_stats(r)
        dxh = dh * g_ref[...]
        m1 = jnp.mean(dxh, axis=-1, keepdims=True)
        m2 = jnp.mean(dxh * xh, axis=-1, keepdims=True)
        dr_ref[...] = rstd * (dxh - m1 - xh * m2)
        pg = jnp.sum(dh * xh, axis=0, keepdims=True)
        pb = jnp.sum(dh, axis=0, keepdims=True)

        @pl.when(i == 0)
        def _():
            dg_ref[...] = pg
            db_ref[...] = pb

        @pl.when(i > 0)
        def _():
            dg_ref[...] += pg
            db_ref[...] += pb

    row = pl.BlockSpec((tm, D), lambda i: (i, 0))
    vec = pl.BlockSpec((1, D), lambda i: (0, 0))
    return pl.pallas_call(
        body, name=name, grid=(S // tm,), in_specs=[row] * (nt + nd) + [vec], out_specs=[row, vec, vec],
        out_shape=[jax.ShapeDtypeStruct((S, D), F32), jax.ShapeDtypeStruct((1, D), F32), jax.ShapeDtypeStruct((1, D), F32)],
        compiler_params=_cparams(("arbitrary",)),
    )(*[t for t, _ in terms], *[t for t, _ in dterms], g)


def _ln_loss_bwd(terms, g, b, tgt, name, tm=512):
    S, D = terms[0][0].shape
    coefs = [c for _, c in terms]
    nt = len(terms)

    def body(*refs):
        i = pl.program_id(0)
        r = sum(c * t[...] for t, c in zip(refs[:nt], coefs))
        g_ref, b_ref, t_ref = refs[nt:nt + 3]
        loss_ref, dr_ref, dg_ref, db_ref = refs[nt + 3:]
        xh, rstd = _ln_stats(r)
        h = xh * g_ref[...] + b_ref[...]
        diff = h - t_ref[...]
        pl_ = 0.5 * jnp.sum(jnp.mean(diff * diff, axis=-1, keepdims=True), axis=0, keepdims=True)
        dh = diff * (1.0 / D)
        dxh = dh * g_ref[...]
        m1 = jnp.mean(dxh, axis=-1, keepdims=True)
        m2 = jnp.mean(dxh * xh, axis=-1, keepdims=True)
        dr_ref[...] = rstd * (dxh - m1 - xh * m2)
        pg = jnp.sum(dh * xh, axis=0, keepdims=True)
        pb = jnp.sum(dh, axis=0, keepdims=True)
        plb = jnp.broadcast_to(pl_, (1, LANES))

        @pl.when(i == 0)
        def _():
            dg_ref[...] = pg
            db_ref[...] = pb
            loss_ref[...] = plb

        @pl.when(i > 0)
        def _():
            dg_ref[...] += pg
            db_ref[...] += pb
            loss_ref[...] += plb

    row = pl.BlockSpec((tm, D), lambda i: (i, 0))
    vec = pl.BlockSpec((1, D), lambda i: (0, 0))
    lvec = pl.BlockSpec((1, LANES), lambda i: (0, 0))
    return pl.pallas_call(
        body, name=name, grid=(S // tm,), in_specs=[row] * nt + [vec, vec, row], out_specs=[lvec, row, vec, vec],
        out_shape=[jax.ShapeDtypeStruct((1, LANES), F32), jax.ShapeDtypeStruct((S, D), F32),
                   jax.ShapeDtypeStruct((1, D), F32), jax.ShapeDtypeStruct((1, D), F32)],
        compiler_params=_cparams(("arbitrary",)),
    )(*[t for t, _ in terms], g, b, tgt)


def _rope_tables(positions):
    half = MLA_ROPE // 2
    inv_freq = jnp.power(ROPE_THETA, -jnp.arange(half, dtype=F32) / half)
    ang = positions.astype(F32)[:, None] * inv_freq
    cos, sin = jnp.cos(ang), jnp.sin(ang)
    S = positions.shape[0]
    one, zero = jnp.ones((S, MLA_NOPE), F32), jnp.zeros((S, half), F32)
    pad = jnp.zeros((S, LANES - MLA_QK), F32)
    c = jnp.concatenate([one, cos, cos, pad], axis=1)
    s1 = jnp.concatenate([0 * one, -sin, zero, pad], axis=1)
    s2 = jnp.concatenate([0 * one, zero, sin, pad], axis=1)
    return c, s1, s2


def _rope_block(x, c, s1, s2):
    half = MLA_ROPE // 2
    return x * c + pltpu.roll(x, LANES - half, axis=1) * s1 + pltpu.roll(x, half, axis=1) * s2


def _rms_fwd(x, g):
    r = lax.rsqrt(jnp.mean(x * x, axis=-1, keepdims=True) + RMS_EPS)
    return x * r * g


def _rms_bwd(x, g, dy):
    r = lax.rsqrt(jnp.mean(x * x, axis=-1, keepdims=True) + RMS_EPS)
    xh = x * r
    dyh = dy * g
    dx = r * (dyh - xh * jnp.mean(dyh * xh, axis=-1, keepdims=True))
    return dx, jnp.sum(dy * xh, axis=0, keepdims=True)


def _mla_prep(proj, qg, kvg, tabs, tm=512):
    S = proj.shape[0]

    def body(ql_ref, kvl_ref, kr_ref, qg_ref, kvg_ref, c_ref, s1_ref, s2_ref, qn_ref, kvn_ref, kpe_ref):
        qn_ref[...] = _rms_fwd(ql_ref[...], qg_ref[...]).astype(BF16)
        kvn_ref[...] = _rms_fwd(kvl_ref[...], kvg_ref[...]).astype(BF16)
        kpe_ref[...] = _rope_block(kr_ref[...], c_ref[...], s1_ref[...], s2_ref[...])

    tab = pl.BlockSpec((tm, LANES), lambda i: (i, 0))
    return pl.pallas_call(
        body, name="mla_prep", grid=(S // tm,),
        in_specs=[pl.BlockSpec((tm, MLA_Q_RANK), lambda i: (i, PQ // MLA_Q_RANK)),
                  pl.BlockSpec((tm, MLA_KV_RANK), lambda i: (i, PKV // MLA_KV_RANK)),
                  pl.BlockSpec((tm, LANES), lambda i: (i, PKR // LANES)),
                  pl.BlockSpec((1, MLA_Q_RANK), lambda i: (0, 0)), pl.BlockSpec((1, MLA_KV_RANK), lambda i: (0, 0)),
                  tab, tab, tab],
        out_specs=[pl.BlockSpec((tm, MLA_Q_RANK), lambda i: (i, 0)), pl.BlockSpec((tm, MLA_KV_RANK), lambda i: (i, 0)), tab],
        out_shape=[jax.ShapeDtypeStruct((S, MLA_Q_RANK), BF16), jax.ShapeDtypeStruct((S, MLA_KV_RANK), BF16),
                   jax.ShapeDtypeStruct((S, LANES), F32)],
        compiler_params=_cparams(("parallel",)),
    )(proj, proj, proj, qg, kvg, *tabs)


def _mla_prep_bwd(proj, qg, kvg, tabs, dqn, dkvn, dk_all, tm=512):
    S = proj.shape[0]

    def body(ql_ref, kvl_ref, qg_ref, kvg_ref, c_ref, s1_ref, s2_ref, dqn_ref, dkvn_ref, dk_ref,
             dql_ref, dkvl_ref, dkr_ref, dqg_ref, dkvg_ref):
        i = pl.program_id(0)
        dql, pq = _rms_bwd(ql_ref[...], qg_ref[...], dqn_ref[...])
        dkvl, pkv = _rms_bwd(kvl_ref[...], kvg_ref[...], dkvn_ref[...])
        dql_ref[...] = dql
        dkvl_ref[...] = dkvl
        dk = dk_ref[...]
        dkpe = dk[:, 0:LANES]
        for h in range(1, MLA_HEADS):
            dkpe = dkpe + dk[:, h * LANES:(h + 1) * LANES]
        lane = lax.broadcasted_iota(jnp.int32, dkpe.shape, 1)
        dkpe = jnp.where((lane >= KR_LANE) & (lane < KR_LANE + MLA_ROPE), dkpe, 0.0)
        dkr_ref[...] = _rope_block(dkpe, c_ref[...], -s1_ref[...], -s2_ref[...])

        @pl.when(i == 0)
        def _():
            dqg_ref[...] = pq
            dkvg_ref[...] = pkv

        @pl.when(i > 0)
        def _():
            dqg_ref[...] += pq
            dkvg_ref[...] += pkv

    tab = pl.BlockSpec((tm, LANES), lambda i: (i, 0))
    qspec = pl.BlockSpec((tm, MLA_Q_RANK), lambda i: (i, 0))
    kvspec = pl.BlockSpec((tm, MLA_KV_RANK), lambda i: (i, 0))
    qv, kvv = pl.BlockSpec((1, MLA_Q_RANK), lambda i: (0, 0)), pl.BlockSpec((1, MLA_KV_RANK), lambda i: (0, 0))
    return pl.pallas_call(
        body, name="mla_prep_bwd", grid=(S // tm,),
        in_specs=[pl.BlockSpec((tm, MLA_Q_RANK), lambda i: (i, PQ // MLA_Q_RANK)),
                  pl.BlockSpec((tm, MLA_KV_RANK), lambda i: (i, PKV // MLA_KV_RANK)),
                  qv, kvv, tab, tab, tab, qspec, kvspec, pl.BlockSpec((tm, MLA_HEADS * LANES), lambda i: (i, 0))],
        out_specs=[qspec, kvspec, tab, qv, kvv],
        out_shape=[jax.ShapeDtypeStruct((S, MLA_Q_RANK), F32), jax.ShapeDtypeStruct((S, MLA_KV_RANK), F32),
                   jax.ShapeDtypeStruct((S, LANES), F32), jax.ShapeDtypeStruct((1, MLA_Q_RANK), F32),
                   jax.ShapeDtypeStruct((1, MLA_KV_RANK), F32)],
        compiler_params=_cparams(("arbitrary",)),
    )(proj, proj, qg, kvg, *tabs, dqn, dkvn, dk_all)


def _rope_bwd_all(dq_all, tabs, tm=512):
    S, W = dq_all.shape

    def body(dq_ref, c_ref, s1_ref, s2_ref, o_ref):
        c, s1, s2 = c_ref[...], -s1_ref[...], -s2_ref[...]
        for h in range(W // LANES):
            o_ref[:, h * LANES:(h + 1) * LANES] = _rope_block(dq_ref[:, h * LANES:(h + 1) * LANES], c, s1, s2).astype(BF16)

    tab = pl.BlockSpec((tm, LANES), lambda i: (i, 0))
    row = pl.BlockSpec((tm, W), lambda i: (i, 0))
    return pl.pallas_call(body, name="rope_bwd", grid=(S // tm,), in_specs=[row, tab, tab, tab], out_specs=row,
                          out_shape=jax.ShapeDtypeStruct((S, W), BF16), compiler_params=_cparams(("parallel",)))(dq_all, *tabs)


ATT_SCALE = MLA_QK ** -0.5
N_PAIR = MLA_HEADS // 2


def _causal_mask(qi, ki, tq, tk):
    row = qi * tq + lax.broadcasted_iota(jnp.int32, (tq, tk), 0)
    col = ki * tk + lax.broadcasted_iota(jnp.int32, (tq, tk), 1)
    return col <= row


def _lane_tile(x, n):
    return jnp.concatenate([x] * n, axis=1) if n > 1 else x


def _attn_fwd(q_all, kv_all, tq=512, tk=1024):
    S = q_all.shape[0]
    tq, tk = min(tq, S), min(tk, S)
    nq, nk, nb, r = S // tq, S // tk, tk // LANES, tk // tq

    def body(q_ref, k_ref, v_ref, o_ref, lse_ref, m_s, l_s, acc_s):
        qi, ki = pl.program_id(1), pl.program_id(2)
        last = lax.div(qi, r)

        @pl.when(ki == 0)
        def _():
            m_s[...] = jnp.full(m_s.shape, NEG, F32)
            l_s[...] = jnp.zeros(l_s.shape, F32)
            acc_s[...] = jnp.zeros(acc_s.shape, F32)

        @pl.when(ki <= last)
        def _():
            v = v_ref[...]
            mask = _causal_mask(qi, ki, tq, tk)
            for hh in range(2):
                q = q_ref[:, hh * LANES:(hh + 1) * LANES]
                k = k_ref[:, hh * LANES:(hh + 1) * LANES]
                s = lax.dot_general(q, k, (((1,), (1,)), ((), ())), preferred_element_type=F32) * ATT_SCALE
                s = jnp.where(mask, s, NEG)
                m_prev = m_s[hh]
                m_new = jnp.maximum(m_prev, jnp.max(s, axis=-1, keepdims=True))
                p = jnp.exp(s - _lane_tile(m_new, nb))
                alpha = jnp.exp(m_prev - m_new)
                ps = p[:, :LANES]
                for j in range(1, nb):
                    ps = ps + p[:, j * LANES:(j + 1) * LANES]
                l_s[hh] = alpha * l_s[hh] + ps
                acc_s[hh] = alpha * acc_s[hh] + jnp.dot(p.astype(BF16), v, preferred_element_type=F32)
                m_s[hh] = m_new

        @pl.when(ki == last)
        def _():
            first = lax.broadcasted_iota(jnp.int32, (tq, LANES), 1) < MLA_V
            l0 = jnp.sum(l_s[0], axis=-1, keepdims=True)
            l1 = jnp.sum(l_s[1], axis=-1, keepdims=True)
            o_ref[...] = jnp.where(first, acc_s[0] / l0, acc_s[1] / l1)
            lse_ref[:, :LANES] = m_s[0] + jnp.log(l0)
            lse_ref[:, LANES:] = m_s[1] + jnp.log(l1)

    return pl.pallas_call(
        body, name="mla_attn_fwd", grid=(N_PAIR, nq, nk),
        in_specs=[pl.BlockSpec((tq, 2 * LANES), lambda p, qi, ki: (qi, p)),
                  pl.BlockSpec((tk, 2 * LANES), lambda p, qi, ki: (jnp.minimum(ki, lax.div(qi, r)), p)),
                  pl.BlockSpec((tk, LANES), lambda p, qi, ki: (jnp.minimum(ki, lax.div(qi, r)), MLA_HEADS + p))],
        out_specs=[pl.BlockSpec((tq, LANES), lambda p, qi, ki: (qi, p)), pl.BlockSpec((tq, 2 * LANES), lambda p, qi, ki: (qi, p))],
        out_shape=[jax.ShapeDtypeStruct((S, MLA_HEADS * MLA_V), F32), jax.ShapeDtypeStruct((S, MLA_HEADS * LANES), F32)],
        scratch_shapes=[pltpu.VMEM((2, tq, LANES), F32), pltpu.VMEM((2, tq, LANES), F32), pltpu.VMEM((2, tq, LANES), F32)],
        compiler_params=_cparams(("parallel", "parallel", "arbitrary")),
    )(q_all, kv_all, kv_all)


def _attn_delta(dcat, o, tm=512):
    S = o.shape[0]

    def body(do_ref, o_ref, d_ref):
        prod = do_ref[...] * o_ref[...]
        first = lax.broadcasted_iota(jnp.int32, (tm, LANES), 1) < MLA_V
        for p in range(N_PAIR):
            pp = prod[:, p * LANES:(p + 1) * LANES]
            d0 = jnp.sum(jnp.where(first, pp, 0.0), axis=-1, keepdims=True)
            d1 = jnp.sum(jnp.where(first, 0.0, pp), axis=-1, keepdims=True)
            d_ref[:, 2 * p * LANES:(2 * p + 1) * LANES] = jnp.broadcast_to(d0, (tm, LANES))
            d_ref[:, (2 * p + 1) * LANES:(2 * p + 2) * LANES] = jnp.broadcast_to(d1, (tm, LANES))

    W = MLA_HEADS * MLA_V
    return pl.pallas_call(
        body, name="mla_attn_delta", grid=(S // tm,),
        in_specs=[pl.BlockSpec((tm, W), lambda i: (i, 1)), pl.BlockSpec((tm, W), lambda i: (i, 0))],
        out_specs=pl.BlockSpec((tm, MLA_HEADS * LANES), lambda i: (i, 0)),
        out_shape=jax.ShapeDtypeStruct((S, MLA_HEADS * LANES), F32), compiler_params=_cparams(("parallel",)),
    )(dcat, o)


def _attn_bwd(q_all, kv_all, dcat, lse, delta, tq=512, tk=512):
    S = q_all.shape[0]
    tq, tk = min(tq, S), min(tk, S)
    nq, nk, nb = S // tq, S // tk, tk // LANES
    assert tq == tk

    def body(q_ref, k_ref, v_ref, do_ref, lse_ref, dl_ref, dq_ref, dk_ref, dv_ref, dk_s, dv_s):
        ki, qi = pl.program_id(1), pl.program_id(2)

        @pl.when((ki == 0) & (qi == 0))
        def _():
            dq_ref[...] = jnp.zeros(dq_ref.shape, F32)

        @pl.when(qi == 0)
        def _():
            dk_s[...] = jnp.zeros(dk_s.shape, F32)
            dv_s[...] = jnp.zeros(dv_s.shape, F32)

        @pl.when(qi >= ki)
        def _():
            v, do = v_ref[...], do_ref[...]
            first = lax.broadcasted_iota(jnp.int32, (tq, LANES), 1) < MLA_V
            firstk = lax.broadcasted_iota(jnp.int32, (tk, LANES), 1) < MLA_V
            mask = _causal_mask(qi, ki, tq, tk)
            do_b = do.astype(BF16)
            rows = pl.ds(pl.multiple_of(qi * tq, tq), tq)
            for hh in range(2):
                sl = slice(hh * LANES, (hh + 1) * LANES)
                q, k = q_ref[:, sl], k_ref[:, sl]
                s = lax.dot_general(q, k, (((1,), (1,)), ((), ())), preferred_element_type=F32) * ATT_SCALE
                p = jnp.exp(jnp.where(mask, s, NEG) - _lane_tile(lse_ref[:, sl], nb))
                do_h = jnp.where(first if hh == 0 else ~first, do, 0.0).astype(BF16)
                dp = lax.dot_general(do_h, v, (((1,), (1,)), ((), ())), preferred_element_type=F32)
                ds_b = (p * (dp - _lane_tile(dl_ref[:, sl], nb)) * ATT_SCALE).astype(BF16)
                pv = lax.dot_general(p.astype(BF16), do_b, (((0,), (0,)), ((), ())), preferred_element_type=F32)
                dv_s[...] += jnp.where(firstk if hh == 0 else ~firstk, pv, 0.0)
                dk_s[:, sl] += lax.dot_general(ds_b, q, (((0,), (0,)), ((), ())), preferred_element_type=F32)
                dq_ref[rows, sl] += jnp.dot(ds_b, k, preferred_element_type=F32)

        @pl.when(qi == nq - 1)
        def _():
            dk_ref[...] = dk_s[...]
            dv_ref[...] = dv_s[...]

    wide = pl.BlockSpec((tq, 2 * LANES), lambda p, ki, qi: (jnp.maximum(qi, ki), p))
    return pl.pallas_call(
        body, name="mla_attn_bwd", grid=(N_PAIR, nk, nq),
        in_specs=[wide, pl.BlockSpec((tk, 2 * LANES), lambda p, ki, qi: (ki, p)),
                  pl.BlockSpec((tk, LANES), lambda p, ki, qi: (ki, MLA_HEADS + p)),
                  pl.BlockSpec((tq, LANES), lambda p, ki, qi: (jnp.maximum(qi, ki), N_PAIR + p)), wide, wide],
        out_specs=[pl.BlockSpec((S, 2 * LANES), lambda p, ki, qi: (0, p)),
                   pl.BlockSpec((tk, 2 * LANES), lambda p, ki, qi: (ki, p)), pl.BlockSpec((tk, LANES), lambda p, ki, qi: (ki, p))],
        out_shape=[jax.ShapeDtypeStruct((S, MLA_HEADS * LANES), F32), jax.ShapeDtypeStruct((S, MLA_HEADS * LANES), F32),
                   jax.ShapeDtypeStruct((S, MLA_HEADS * MLA_V), F32)],
        scratch_shapes=[pltpu.VMEM((tk, 2 * LANES), F32), pltpu.VMEM((tk, LANES), F32)],
        compiler_params=_cparams(("parallel", "arbitrary", "arbitrary")),
    )(q_all, kv_all, kv_all, dcat, lse, delta)


def _attn_p_ds(q, k, v, do, lse_h, delta_h, half_mask, mask):
    s = lax.dot_general(q, k, (((1,), (1,)), ((), ())), preferred_element_type=F32) * ATT_SCALE
    p = jnp.exp(jnp.where(mask, s, NEG) - lse_h)
    do_h = jnp.where(half_mask, do, 0.0).astype(BF16)
    dp = lax.dot_general(do_h, v, (((1,), (1,)), ((), ())), preferred_element_type=F32)
    ds = p * (dp - delta_h) * ATT_SCALE
    return p, ds


def _attn_deltas(do, o, first):
    prod = do * o
    return (jnp.sum(jnp.where(first, prod, 0.0), axis=-1, keepdims=True),
            jnp.sum(jnp.where(first, 0.0, prod), axis=-1, keepdims=True))


def _attn_bwd_dq(q_all, kv_all, dcat, o, lse, tq=512, tk=512):
    S = q_all.shape[0]
    nq, nk = S // tq, S // tk

    def body(q_ref, k_ref, v_ref, do_ref, o_ref, lse_ref, dq_ref, acc_s):
        qi, ki = pl.program_id(1), pl.program_id(2)

        @pl.when(ki == 0)
        def _():
            acc_s[...] = jnp.zeros(acc_s.shape, F32)

        @pl.when(ki <= qi)
        def _():
            v, do, lse_t = v_ref[...], do_ref[...], lse_ref[...]
            first = lax.broadcasted_iota(jnp.int32, (tq, LANES), 1) < MLA_V
            deltas = _attn_deltas(do, o_ref[...], first)
            mask = _causal_mask(qi, ki, tq, tk)
            for hh in range(2):
                k = k_ref[:, hh * LANES:(hh + 1) * LANES]
                _, ds = _attn_p_ds(q_ref[:, hh * LANES:(hh + 1) * LANES], k, v, do,
                                   lse_t[:, hh * MLA_V:hh * MLA_V + 1], deltas[hh], first if hh == 0 else ~first, mask)
                acc_s[:, hh * LANES:(hh + 1) * LANES] += jnp.dot(ds.astype(BF16), k, preferred_element_type=F32)

        @pl.when(ki == qi)
        def _():
            dq_ref[...] = acc_s[...]

    half = pl.BlockSpec((tq, LANES), lambda p, qi, ki: (qi, p))
    return pl.pallas_call(
        body, name="mla_attn_bwd_dq", grid=(N_PAIR, nq, nk),
        in_specs=[pl.BlockSpec((tq, 2 * LANES), lambda p, qi, ki: (qi, p)),
                  pl.BlockSpec((tk, 2 * LANES), lambda p, qi, ki: (jnp.minimum(ki, qi), p)),
                  pl.BlockSpec((tk, LANES), lambda p, qi, ki: (jnp.minimum(ki, qi), MLA_HEADS + p)),
                  pl.BlockSpec((tq, LANES), lambda p, qi, ki: (qi, N_PAIR + p)), half, half],
        out_specs=pl.BlockSpec((tq, 2 * LANES), lambda p, qi, ki: (qi, p)),
        out_shape=jax.ShapeDtypeStruct((S, MLA_HEADS * LANES), F32),
        scratch_shapes=[pltpu.VMEM((tq, 2 * LANES), F32)],
        compiler_params=_cparams(("parallel", "parallel", "arbitrary")),
    )(q_all, kv_all, kv_all, dcat, o, lse)


def _attn_bwd_dkv(q_all, kv_all, dcat, o, lse, tq=512, tk=512):
    S = q_all.shape[0]
    nq, nk = S // tq, S // tk

    def body(q_ref, k_ref, v_ref, do_ref, o_ref, lse_ref, dk_ref, dv_ref, dk_s, dv_s):
        ki, qi = pl.program_id(1), pl.program_id(2)

        @pl.when(qi == 0)
        def _():
            dk_s[...] = jnp.zeros(dk_s.shape, F32)
            dv_s[...] = jnp.zeros(dv_s.shape, F32)

        @pl.when(qi >= ki)
        def _():
            v, do, lse_t = v_ref[...], do_ref[...], lse_ref[...]
            first = lax.broadcasted_iota(jnp.int32, (tq, LANES), 1) < MLA_V
            firstk = lax.broadcasted_iota(jnp.int32, (tk, LANES), 1) < MLA_V
            deltas = _attn_deltas(do, o_ref[...], first)
            mask = _causal_mask(qi, ki, tq, tk)
            do_b = do.astype(BF16)
            for hh in range(2):
                q = q_ref[:, hh * LANES:(hh + 1) * LANES]
                p, ds = _attn_p_ds(q, k_ref[:, hh * LANES:(hh + 1) * LANES], v, do,
                                   lse_t[:, hh * MLA_V:hh * MLA_V + 1], deltas[hh], first if hh == 0 else ~first, mask)
                pv = lax.dot_general(p.astype(BF16), do_b, (((0,), (0,)), ((), ())), preferred_element_type=F32)
                dv_s[...] += jnp.where(firstk if hh == 0 else ~firstk, pv, 0.0)
                dk_s[:, hh * LANES:(hh + 1) * LANES] += lax.dot_general(ds.astype(BF16), q, (((0,), (0,)), ((), ())),
                                                                       preferred_element_type=F32)

        @pl.when(qi == nq - 1)
        def _():
            dk_ref[...] = dk_s[...]
            dv_ref[...] = dv_s[...]

    half = pl.BlockSpec((tq, LANES), lambda p, ki, qi: (jnp.maximum(qi, ki), p))
    return pl.pallas_call(
        body, name="mla_attn_bwd_dkv", grid=(N_PAIR, nk, nq),
        in_specs=[pl.BlockSpec((tq, 2 * LANES), lambda p, ki, qi: (jnp.maximum(qi, ki), p)),
                  pl.BlockSpec((tk, 2 * LANES), lambda p, ki, qi: (ki, p)),
                  pl.BlockSpec((tk, LANES), lambda p, ki, qi: (ki, MLA_HEADS + p)),
                  pl.BlockSpec((tq, LANES), lambda p, ki, qi: (jnp.maximum(qi, ki), N_PAIR + p)), half, half],
        out_specs=[pl.BlockSpec((tk, 2 * LANES), lambda p, ki, qi: (ki, p)), pl.BlockSpec((tk, LANES), lambda p, ki, qi: (ki, p))],
        out_shape=[jax.ShapeDtypeStruct((S, MLA_HEADS * LANES), F32), jax.ShapeDtypeStruct((S, MLA_HEADS * MLA_V), F32)],
        scratch_shapes=[pltpu.VMEM((tk, 2 * LANES), F32), pltpu.VMEM((tk, LANES), F32)],
        compiler_params=_cparams(("parallel", "parallel", "arbitrary")),
    )(q_all, kv_all, kv_all, dcat, o, lse)


MEM_SCALE = MEM_HEAD_DIM ** -0.5


def _mem_probs(q, k):
    s = lax.dot_general(q, k, (((1,), (1,)), ((), ())), preferred_element_type=F32) * MEM_SCALE
    e = jnp.exp(s - jnp.max(s, axis=-1, keepdims=True))
    return e / jnp.sum(e, axis=-1, keepdims=True)


def _mem_attn_fwd(qm, km, vm, tq=512):
    S, W = qm.shape
    M = km.shape[0]

    def body(q_ref, k_ref, v_ref, o_ref):
        for h in range(MEM_HEADS):
            sl = slice(h * MEM_HEAD_DIM, (h + 1) * MEM_HEAD_DIM)
            p = _mem_probs(q_ref[:, sl], k_ref[:, sl])
            o_ref[:, sl] = jnp.dot(p.astype(BF16), v_ref[:, sl], preferred_element_type=F32).astype(BF16)

    row = pl.BlockSpec((tq, W), lambda i: (i, 0))
    full = pl.BlockSpec((M, W), lambda i: (0, 0))
    return pl.pallas_call(body, name="mem_attn_fwd", grid=(S // tq,), in_specs=[row, full, full], out_specs=row,
                          out_shape=jax.ShapeDtypeStruct((S, W), BF16), compiler_params=_cparams(("parallel",)))(qm, km, vm)


def _mem_attn_bwd(qm, km, vm, dom, tq=512):
    S, W = qm.shape
    M = km.shape[0]

    def body(q_ref, k_ref, v_ref, do_ref, dq_ref, dk_ref, dv_ref):
        i = pl.program_id(0)

        @pl.when(i == 0)
        def _():
            dk_ref[...] = jnp.zeros(dk_ref.shape, F32)
            dv_ref[...] = jnp.zeros(dv_ref.shape, F32)

        for h in range(MEM_HEADS):
            sl = slice(h * MEM_HEAD_DIM, (h + 1) * MEM_HEAD_DIM)
            q, k, v, do = q_ref[:, sl], k_ref[:, sl], v_ref[:, sl], do_ref[:, sl]
            p = _mem_probs(q, k)
            dv_ref[:, sl] += lax.dot_general(p.astype(BF16), do, (((0,), (0,)), ((), ())), preferred_element_type=F32)
            dp = lax.dot_general(do, v, (((1,), (1,)), ((), ())), preferred_element_type=F32)
            ds = (p * (dp - jnp.sum(dp * p, axis=-1, keepdims=True)) * MEM_SCALE).astype(BF16)
            dq_ref[:, sl] = jnp.dot(ds, k, preferred_element_type=F32).astype(BF16)
            dk_ref[:, sl] += lax.dot_general(ds, q, (((0,), (0,)), ((), ())), preferred_element_type=F32)

    row = pl.BlockSpec((tq, W), lambda i: (i, 0))
    full = pl.BlockSpec((M, W), lambda i: (0, 0))
    return pl.pallas_call(
        body, name="mem_attn_bwd", grid=(S // tq,), in_specs=[row, full, full, row], out_specs=[row, full, full],
        out_shape=[jax.ShapeDtypeStruct((S, W), BF16), jax.ShapeDtypeStruct((M, W), F32), jax.ShapeDtypeStruct((M, W), F32)],
        compiler_params=_cparams(("arbitrary",)),
    )(qm, km, vm, dom)


L = SSD_CHUNK
N_SPAIR = SSD_HEADS // 2
GRP_W = SSD_INNER // 2
XB0, XC0 = SSD_INNER, SSD_INNER + 2 * SSD_STATE


def _cumsum_rows(a, reverse=False):
    row = lax.broadcasted_iota(jnp.int32, a.shape, 0)
    x, sft = a, 1
    while sft < L:
        if reverse:
            x = x + jnp.where(row < L - sft, pltpu.roll(x, L - sft, axis=0), 0.0)
        else:
            x = x + jnp.where(row >= sft, pltpu.roll(x, sft, axis=0), 0.0)
        sft *= 2
    return x


def _shift_down(cur, prev, s):
    if s == 0:
        return cur
    row = lax.broadcasted_iota(jnp.int32, cur.shape, 0)
    return jnp.where(row < s, pltpu.roll(prev, s, axis=0), pltpu.roll(cur, s, axis=0))


def _shift_up(cur, nxt, s):
    if s == 0:
        return cur
    row = lax.broadcasted_iota(jnp.int32, cur.shape, 0)
    return jnp.where(row >= L - s, pltpu.roll(nxt, L - s, axis=0), pltpu.roll(cur, L - s, axis=0))


def _ssd_conv(u, prev, cw, cb):
    conv = cb + cw[SSD_CONV - 1:SSD_CONV, :] * u
    for s in range(1, SSD_CONV):
        conv = conv + cw[SSD_CONV - 1 - s:SSD_CONV - s, :] * _shift_down(u, prev, s)
    return conv


def _pair_lanes(v, h0, first):
    return jnp.where(first, v[:, h0:h0 + 1], v[:, h0 + 1:h0 + 2])


def _ssd_common(u, prev, dt_raw, cw, cb, dtb, alog):
    conv = _ssd_conv(u, prev, cw, cb)
    sg = _sigmoid(conv)
    xa = conv * sg
    dpre = dt_raw + dtb
    dtv = jnp.maximum(dpre, 0.0) + jnp.log1p(jnp.exp(-jnp.abs(dpre)))
    a_row = -jnp.exp(alog)
    cs = _cumsum_rows(dtv * a_row)
    return conv, sg, xa, dpre, dtv, a_row, cs


def _ssd_pair_fwd(xa, dtv, cs, csT, G, Cg, Bg, Sp, dsk, pp, first, tri, rowfirst):
    h0 = 2 * pp
    x = xa[:, pp * LANES:(pp + 1) * LANES]
    xdt = x * _pair_lanes(dtv, h0, first)
    xdt_b = xdt.astype(BF16)
    Ms, yd = [], []
    for h in (h0, h0 + 1):
        lam = jnp.exp(jnp.where(tri, cs[:, h:h + 1] - csT[h:h + 1, :], NEG))
        M = G * lam
        Ms.append((M, lam))
        yd.append(jnp.dot(M.astype(BF16), xdt_b, preferred_element_type=F32))
    T = lax.dot_general(Cg, Sp.astype(BF16), (((1,), (1,)), ((), ())), preferred_element_type=F32)
    E = jnp.exp(_pair_lanes(cs, h0, first))
    yoff = E * T
    csl = cs[L - 1:L, :]
    Fd = jnp.exp(_pair_lanes(csl, h0, first) - _pair_lanes(cs, h0, first))
    el = jnp.exp(csl)
    el_rows = jnp.where(rowfirst, el[:, h0:h0 + 1], el[:, h0 + 1:h0 + 2])
    Sloc = lax.dot_general((xdt * Fd).astype(BF16), Bg, (((0,), (0,)), ((), ())), preferred_element_type=F32)
    S_new = el_rows * Sp + Sloc
    y = jnp.where(first, yd[0], yd[1]) + yoff + x * _pair_lanes(dsk, h0, first[:1])
    return y, S_new, (x, xdt, xdt_b, Ms, E, yoff, Fd, el, el_rows)


def _ssd_masks():
    lane = lax.broadcasted_iota(jnp.int32, (L, LANES), 1)
    row = lax.broadcasted_iota(jnp.int32, (L, LANES), 0)
    return lane, row, lane < SSD_HEAD_DIM, row >= lane, row[:, :1] < SSD_HEAD_DIM


def _ssd_specs(nc, rev):
    def cidx(i):
        return nc - 1 - i if rev else i
    z = pl.BlockSpec((L, SSD_INNER), lambda i: (cidx(i), PZ // SSD_INNER))
    u = pl.BlockSpec((L, SSD_XBC), lambda i: (cidx(i), PX // SSD_XBC))
    dt = pl.BlockSpec((L, LANES), lambda i: (cidx(i), PDT // LANES))
    return cidx, z, u, dt


def _vec(w):
    return pl.BlockSpec((1, w), lambda i: (0, 0))


def _ssd_fwd(proj, cw, cb, dtb, alog, dsk, ng):
    S = proj.shape[0]
    nc = S // L

    def body(z_ref, u_ref, dt_ref, cw_ref, cb_ref, dtb_ref, alog_ref, dsk_ref, ng_ref, y_ref, st_ref, prev_s, state_s):
        c = pl.program_id(0)

        @pl.when(c == 0)
        def _():
            prev_s[...] = jnp.zeros(prev_s.shape, F32)
            state_s[...] = jnp.zeros(state_s.shape, F32)

        u = u_ref[...]
        _, _, xa, _, dtv, _, cs = _ssd_common(u, prev_s[...], dt_ref[...], cw_ref[...], cb_ref[...], dtb_ref[...], alog_ref[...])
        prev_s[...] = u
        csT = cs.T
        _, _, first, tri, rowfirst = _ssd_masks()
        dsk_v = dsk_ref[...]
        ys = []
        for g in range(2):
            Bg = xa[:, XB0 + g * SSD_STATE:XB0 + (g + 1) * SSD_STATE].astype(BF16)
            Cg = xa[:, XC0 + g * SSD_STATE:XC0 + (g + 1) * SSD_STATE].astype(BF16)
            G = lax.dot_general(Cg, Bg, (((1,), (1,)), ((), ())), preferred_element_type=F32)
            for pp in (2 * g, 2 * g + 1):
                Sp = state_s[pp]
                st_ref[pp * LANES:(pp + 1) * LANES, :] = Sp
                y, S_new, _ = _ssd_pair_fwd(xa, dtv, cs, csT, G, Cg, Bg, Sp, dsk_v, pp, first, tri, rowfirst)
                state_s[pp] = S_new
                ys.append(y)
        z = z_ref[...]
        for g in range(2):
            yg = jnp.concatenate([ys[2 * g], ys[2 * g + 1]], axis=1)
            zg = z[:, g * GRP_W:(g + 1) * GRP_W]
            gated = yg * (zg * _sigmoid(zg))
            r = lax.rsqrt(jnp.mean(gated * gated, axis=-1, keepdims=True) + RMS_EPS)
            y_ref[:, g * GRP_W:(g + 1) * GRP_W] = gated * r * ng_ref[:, g * GRP_W:(g + 1) * GRP_W]

    _, zs, us, dts = _ssd_specs(nc, False)
    return pl.pallas_call(
        body, name="ssd_fwd", grid=(nc,),
        in_specs=[zs, us, dts, pl.BlockSpec((8, SSD_XBC), lambda i: (0, 0)), _vec(SSD_XBC), _vec(LANES), _vec(LANES), _vec(LANES),
                  _vec(SSD_INNER)],
        out_specs=[pl.BlockSpec((L, SSD_INNER), lambda i: (i, 0)), pl.BlockSpec((N_SPAIR * LANES, SSD_STATE), lambda i: (i, 0))],
        out_shape=[jax.ShapeDtypeStruct((S, SSD_INNER), F32), jax.ShapeDtypeStruct((nc * N_SPAIR * LANES, SSD_STATE), F32)],
        scratch_shapes=[pltpu.VMEM((L, SSD_XBC), F32), pltpu.VMEM((N_SPAIR, LANES, SSD_STATE), F32)],
        compiler_params=_cparams(("arbitrary",)),
    )(proj, proj, proj, cw, cb, dtb, alog, dsk, ng)


def _ssd_bwd(proj, states, dy, cw, cb, dtb, alog, dsk, ng):
    S = proj.shape[0]
    nc = S // L

    def body(z_ref, u_ref, up_ref, dt_ref, st_ref, dy_ref, cw_ref, cb_ref, dtb_ref, alog_ref, dsk_ref, ng_ref,
             dz_ref, du_ref, ddt_ref, dcw_ref, dcb_ref, ddtb_ref, dalog_ref, ddsk_ref, dng_ref,
             dS_s, dconv_s, dD_s):
        i = pl.program_id(0)
        c = nc - 1 - i

        @pl.when(i == 0)
        def _():
            dS_s[...] = jnp.zeros(dS_s.shape, F32)
            dconv_s[...] = jnp.zeros(dconv_s.shape, F32)
            dD_s[...] = jnp.zeros(dD_s.shape, F32)
            for r in (dcw_ref, dcb_ref, ddtb_ref, dalog_ref, ddsk_ref, dng_ref):
                r[...] = jnp.zeros(r.shape, F32)

        u = u_ref[...]
        prev = jnp.where(c > 0, up_ref[...], 0.0)
        cw_v = cw_ref[...]
        conv, sg, xa, dpre, dtv, a_row, cs = _ssd_common(u, prev, dt_ref[...], cw_v, cb_ref[...], dtb_ref[...], alog_ref[...])
        csT = cs.T
        lane, row, first, tri, rowfirst = _ssd_masks()
        dsk_v = dsk_ref[...]

        fw = []
        Gs, Bs, Cs = [], [], []
        for g in range(2):
            Bg = xa[:, XB0 + g * SSD_STATE:XB0 + (g + 1) * SSD_STATE].astype(BF16)
            Cg = xa[:, XC0 + g * SSD_STATE:XC0 + (g + 1) * SSD_STATE].astype(BF16)
            G = lax.dot_general(Cg, Bg, (((1,), (1,)), ((), ())), preferred_element_type=F32)
            Gs.append(G), Bs.append(Bg), Cs.append(Cg)
            for pp in (2 * g, 2 * g + 1):
                Sp = st_ref[pp * LANES:(pp + 1) * LANES, :]
                y, _, keep = _ssd_pair_fwd(xa, dtv, cs, csT, G, Cg, Bg, Sp, dsk_v, pp, first, tri, rowfirst)
                fw.append((y, Sp, keep))

        z = z_ref[...]
        dys = []
        for g in range(2):
            sl = slice(g * GRP_W, (g + 1) * GRP_W)
            yg = jnp.concatenate([fw[2 * g][0], fw[2 * g + 1][0]], axis=1)
            zg = z[:, sl]
            sz = _sigmoid(zg)
            silu_z = zg * sz
            gated = yg * silu_z
            r = lax.rsqrt(jnp.mean(gated * gated, axis=-1, keepdims=True) + RMS_EPS)
            nh = gated * r
            dout = dy_ref[:, sl]
            dng_ref[:, sl] += jnp.sum(dout * nh, axis=0, keepdims=True)
            dnh = dout * ng_ref[:, sl]
            dgated = r * (dnh - nh * jnp.mean(dnh * nh, axis=-1, keepdims=True))
            dz_ref[:, sl] = dgated * yg * (sz * (1.0 + zg * (1.0 - sz)))
            dyg = dgated * silu_z
            dys.append(dyg[:, :LANES]), dys.append(dyg[:, LANES:])

        dcs_c = jnp.zeros((L, LANES), F32)
        dcs_r = jnp.zeros((L, LANES), F32)
        ddt_c = jnp.zeros((L, LANES), F32)
        dxs = []
        dB, dC = [None, None], [None, None]
        last = row == L - 1
        for g in range(2):
            Bg, Cg, G = Bs[g], Cs[g], Gs[g]
            dG = jnp.zeros((L, L), F32)
            dBg = jnp.zeros((L, SSD_STATE), F32)
            dCg = jnp.zeros((L, SSD_STATE), F32)
            for pp in (2 * g, 2 * g + 1):
                h0 = 2 * pp
                y, Sp, (x, xdt, xdt_b, Ms, E, yoff, Fd, el, el_rows) = fw[pp]
                dY = dys[pp]
                dS = dS_s[pp]
                dS_b, Sp_b = dS.astype(BF16), Sp.astype(BF16)
                dD_s[:, pp * LANES:(pp + 1) * LANES] += jnp.sum(dY * x, axis=0, keepdims=True)
                dx = dY * _pair_lanes(dsk_v, h0, first[:1])
                dxdt = jnp.zeros((L, LANES), F32)
                dY_b = dY.astype(BF16)
                for hh, h in enumerate((h0, h0 + 1)):
                    hm = first if hh == 0 else ~first
                    M, lam = Ms[hh]
                    dYh = jnp.where(hm, dY, 0.0).astype(BF16)
                    dM = lax.dot_general(dYh, xdt_b, (((1,), (1,)), ((), ())), preferred_element_type=F32)
                    W = dM * M
                    dcs_c = dcs_c + jnp.where(lane == h, jnp.sum(W, axis=-1, keepdims=True), 0.0)
                    dcs_r = dcs_r + jnp.where(row == h, jnp.sum(W, axis=0, keepdims=True), 0.0)
                    dG = dG + dM * lam
                    mt = lax.dot_general(M.astype(BF16), dY_b, (((0,), (0,)), ((), ())), preferred_element_type=F32)
                    dxdt = dxdt + jnp.where(hm, mt, 0.0)
                dT = (E * dY).astype(BF16)
                dCg = dCg + jnp.dot(dT, Sp_b, preferred_element_type=F32)
                dS_in = lax.dot_general(dT, Cg, (((0,), (0,)), ((), ())), preferred_element_type=F32) + el_rows * dS
                q1 = dY * yoff
                dZ = lax.dot_general(Bg, dS_b, (((1,), (1,)), ((), ())), preferred_element_type=F32)
                dBg = dBg + jnp.dot((xdt * Fd).astype(BF16), dS_b, preferred_element_type=F32)
                dxdt = dxdt + dZ * Fd
                q2 = dZ * xdt * Fd
                dSS = dS * Sp
                for hh, h in enumerate((h0, h0 + 1)):
                    hm = first if hh == 0 else ~first
                    rs1 = jnp.sum(jnp.where(hm, q1, 0.0), axis=-1, keepdims=True)
                    rs2 = jnp.sum(jnp.where(hm, q2, 0.0), axis=-1, keepdims=True)
                    rmask = rowfirst if hh == 0 else ~rowfirst
                    d_el = jnp.sum(jnp.sum(jnp.where(rmask, dSS, 0.0), axis=-1, keepdims=True), axis=0, keepdims=True)
                    tail = jnp.sum(rs2, axis=0, keepdims=True) + d_el * el[:, h:h + 1]
                    dcs_c = dcs_c + jnp.where(lane == h, rs1 - rs2 + jnp.where(last[:, :1], tail, 0.0), 0.0)
                    ddt_c = ddt_c + jnp.where(lane == h, jnp.sum(jnp.where(hm, dxdt * x, 0.0), axis=-1, keepdims=True), 0.0)
                dS_s[pp] = dS_in
                dxs.append(dx + dxdt * _pair_lanes(dtv, h0, first))
            dG_b = dG.astype(BF16)
            dC[g] = dCg + jnp.dot(dG_b, Bg, preferred_element_type=F32)
            dB[g] = dBg + lax.dot_general(dG_b, Cg, (((0,), (0,)), ((), ())), preferred_element_type=F32)

        dcs = dcs_c - dcs_r.T
        da = _cumsum_rows(dcs, reverse=True)
        ddt_c = ddt_c + da * a_row
        dalog_ref[...] += jnp.sum(da * dtv, axis=0, keepdims=True) * a_row
        ddt_raw = ddt_c * _sigmoid(dpre)
        ddt_ref[...] = ddt_raw
        ddtb_ref[...] += jnp.sum(ddt_raw, axis=0, keepdims=True)

        dxa = jnp.concatenate(dxs + dB + dC, axis=1)
        dconv = dxa * (sg * (1.0 + conv * (1.0 - sg)))
        dcb_ref[...] += jnp.sum(dconv, axis=0, keepdims=True)
        nxt = dconv_s[...]
        du = cw_v[SSD_CONV - 1:SSD_CONV, :] * dconv
        dcw_ref[SSD_CONV - 1:SSD_CONV, :] += jnp.sum(dconv * u, axis=0, keepdims=True)
        for s in range(1, SSD_CONV):
            k = SSD_CONV - 1 - s
            du = du + cw_v[k:k + 1, :] * _shift_up(dconv, nxt, s)
            dcw_ref[k:k + 1, :] += jnp.sum(dconv * _shift_down(u, prev, s), axis=0, keepdims=True)
        du_ref[...] = du
        dconv_s[...] = dconv

        @pl.when(i == nc - 1)
        def _():
            acc = dD_s[...]
            lane1 = lax.broadcasted_iota(jnp.int32, (1, LANES), 1)
            lanew = lax.broadcasted_iota(jnp.int32, acc.shape, 1)
            out = jnp.zeros((1, LANES), F32)
            for h in range(SSD_HEADS):
                tot = jnp.sum(jnp.where((lanew >= h * SSD_HEAD_DIM) & (lanew < (h + 1) * SSD_HEAD_DIM), acc, 0.0),
                              axis=-1, keepdims=True)
                out = out + jnp.where(lane1 == h, tot, 0.0)
            ddsk_ref[...] = out

    cidx, zs, us, dts = _ssd_specs(nc, True)
    ups = pl.BlockSpec((L, SSD_XBC), lambda i: (jnp.maximum(cidx(i) - 1, 0), PX // SSD_XBC))
    rowc = lambda w: pl.BlockSpec((L, w), lambda i: (cidx(i), 0))
    return pl.pallas_call(
        body, name="ssd_bwd", grid=(nc,),
        in_specs=[zs, us, ups, dts, pl.BlockSpec((N_SPAIR * LANES, SSD_STATE), lambda i: (cidx(i), 0)), rowc(SSD_INNER),
                  pl.BlockSpec((8, SSD_XBC), lambda i: (0, 0)), _vec(SSD_XBC), _vec(LANES), _vec(LANES), _vec(LANES), _vec(SSD_INNER)],
        out_specs=[rowc(SSD_INNER), rowc(SSD_XBC), rowc(LANES), pl.BlockSpec((8, SSD_XBC), lambda i: (0, 0)), _vec(SSD_XBC),
                   _vec(LANES), _vec(LANES), _vec(LANES), _vec(SSD_INNER)],
        out_shape=[jax.ShapeDtypeStruct((S, SSD_INNER), F32), jax.ShapeDtypeStruct((S, SSD_XBC), F32),
                   jax.ShapeDtypeStruct((S, LANES), F32), jax.ShapeDtypeStruct((8, SSD_XBC), F32),
                   jax.ShapeDtypeStruct((1, SSD_XBC), F32), jax.ShapeDtypeStruct((1, LANES), F32),
                   jax.ShapeDtypeStruct((1, LANES), F32), jax.ShapeDtypeStruct((1, LANES), F32),
                   jax.ShapeDtypeStruct((1, SSD_INNER), F32)],
        scratch_shapes=[pltpu.VMEM((N_SPAIR, LANES, SSD_STATE), F32), pltpu.VMEM((L, SSD_XBC), F32),
                        pltpu.VMEM((1, SSD_INNER), F32)],
        compiler_params=_cparams(("arbitrary",)),
    )(proj, proj, proj, proj, states, dy, cw, cb, dtb, alog, dsk, ng)


_IN_SEGS = ((PZ, 0, 512), (PX, 512, 1024), (PDT, 1536, 8), (PQ, 1544, 384), (PKV, 1928, 256), (PKR + KR_LANE, 2184, 32))


def _pad_w_in(w):
    out = jnp.zeros((w.shape[0], PW), w.dtype)
    for dst, src, n in _IN_SEGS:
        out = lax.dynamic_update_slice(out, w[:, src:src + n], (0, dst))
    return out


def _unpad_w_in(wp):
    segs = sorted(_IN_SEGS, key=lambda t: t[1])
    return jnp.concatenate([wp[:, dst:dst + n] for dst, src, n in segs], axis=1)


def _pad_w_q(w):
    return jnp.pad(w.reshape(MLA_Q_RANK, MLA_HEADS, MLA_QK), ((0, 0), (0, 0), (0, LANES - MLA_QK))).reshape(MLA_Q_RANK, MLA_HEADS * LANES)


def _unpad_w_q(wp):
    return wp.reshape(MLA_Q_RANK, MLA_HEADS, LANES)[:, :, :MLA_QK].reshape(MLA_Q_RANK, MLA_HEADS * MLA_QK)


def _pad_w_kv(w):
    w3 = w.reshape(MLA_KV_RANK, MLA_HEADS, MLA_NOPE + MLA_V)
    k = jnp.pad(w3[:, :, :MLA_NOPE], ((0, 0), (0, 0), (0, LANES - MLA_NOPE))).reshape(MLA_KV_RANK, MLA_HEADS * LANES)
    return jnp.concatenate([k, w3[:, :, MLA_NOPE:].reshape(MLA_KV_RANK, MLA_HEADS * MLA_V)], axis=1)


def _unpad_w_kv(wp):
    k = wp[:, :MLA_HEADS * LANES].reshape(MLA_KV_RANK, MLA_HEADS, LANES)[:, :, :MLA_NOPE]
    v = wp[:, MLA_HEADS * LANES:].reshape(MLA_KV_RANK, MLA_HEADS, MLA_V)
    return jnp.concatenate([k, v], axis=2).reshape(MLA_KV_RANK, MLA_HEADS * (MLA_NOPE + MLA_V))


def _head_lanes(v):
    return jnp.pad(v, ((0, 0), (0, LANES - v.shape[1])))


def _local_step(x, mem, positions, tgt, P, late_weights=None, emit=None):
    tabs = _rope_tables(positions)
    G = {}
    emit = emit or (lambda names, grads: 0.0)

    h0 = _ln_fwd([(x, 1.0)], P["ln_in_g"], P["ln_in_b"], "ln_in")
    proj = _mm(h0, P["w_in"], "nn", "proj_in", tm=1024, tn=640)
    y_ssd, states = _ssd_fwd(proj, P["conv_w"], P["conv_b"], P["dt_bias"], P["a_log"], P["d_skip"], P["ssd_norm_g"])
    qn, kvn, kpe = _mla_prep(proj, P["q_norm_g"], P["kv_norm_g"], tabs)

    def q_epi(acc, c, s1, s2):
        return (jnp.concatenate([_rope_block(acc[:, h * LANES:(h + 1) * LANES], c, s1, s2) for h in range(MLA_HEADS)], axis=1),)

    q_all = _mm(qn, P["w_q_up"], "nn", "q_up", out_dtypes=(BF16,), epi=q_epi, extras=[(t, "m") for t in tabs])

    def kv_epi(acc, kp):
        kb = [acc[:, h * LANES:(h + 1) * LANES] + kp for h in range(MLA_HEADS)]
        return (jnp.concatenate(kb + [acc[:, MLA_HEADS * LANES:]], axis=1),)

    kv_all = _mm(kvn, P["w_kv_up"], "nn", "kv_up", out_dtypes=(BF16,), epi=kv_epi, extras=[(kpe, "m")])
    o_att, lse = _attn_fwd(q_all, kv_all)
    cat = jnp.concatenate([y_ssd, o_att], axis=1)
    mix = _mm(cat, P["w_mix_out"], "nn", "mix_out", tm=1024)
    h1 = _ln_fwd([(h0, ALPHA), (mix, 1.0)], P["ln1_g"], P["ln1_b"], "ln1")
    if late_weights is not None:
        P = {**P, **late_weights(h1)}
    qm = _mm(h1, P["w_mem_q"], "nn", "mem_q", tm=1024, out_dtypes=(BF16,))
    km = _mm(mem, P["w_mem_k"], "nn", "mem_k", out_dtypes=(BF16,))
    vm = _mm(mem, P["w_mem_v"], "nn", "mem_v", out_dtypes=(BF16,))
    om = _mem_attn_fwd(qm, km, vm)
    xa = _mm(om, P["w_mem_o"], "nn", "mem_o", tm=1024)
    h2 = _ln_fwd([(h1, ALPHA), (xa, 1.0)], P["ln2_g"], P["ln2_b"], "ln2")

    def up_epi(acc):
        r = jnp.maximum(acc, 0.0)
        return acc, r * r

    u, act = _mm(h2, P["w_up"], "nn", "mlp_up", tm=1024, tn=1024, out_dtypes=(F32, BF16), epi=up_epi)
    ff = _mm(act, P["w_down"], "nn", "mlp_down", tm=1024, tk=1024)

    loss, dr3, G["ln3_g"], G["ln3_b"] = _ln_loss_bwd([(h2, ALPHA), (ff, 1.0)], P["ln3_g"], P["ln3_b"], tgt, "ln3_loss")

    def dact_epi(acc, uu):
        return (acc * (2.0 * jnp.maximum(uu, 0.0)),)

    du = _mm(dr3, P["w_down"], "nt", "mlp_down_dx", tm=1024, tn=1024, out_dtypes=(BF16,), epi=dact_epi, extras=[(u, "mn")])
    G["w_down"] = _mm(act, dr3, "tn", "mlp_down_dw", tm=1024, tk=512)
    G["w_up"] = _mm(h2, du, "tn", "mlp_up_dw", tm=1024, tn=D_FF // N_DEV, tk=1024, col_slots=True)
    tie = emit(("w_down", "w_up"), G)
    dh2_ff = _mm(du, P["w_up"], "nt", "mlp_up_dx", tm=1024, tk=1024)
    dr2, G["ln2_g"], G["ln2_b"] = _ln_bwd([(h1, ALPHA), (xa, 1.0)], [(dr3, ALPHA), (dh2_ff, 1.0)], P["ln2_g"] + tie, "ln2_bwd")

    dom = _mm(dr2, P["w_mem_o"], "nt", "mem_o_dx", tm=1024, out_dtypes=(BF16,))
    G["w_mem_o"] = _mm(om, dr2, "tn", "mem_o_dw", tm=1024, tk=512)
    dqm, dkm, dvm = _mem_attn_bwd(qm, km, vm, dom)
    G["w_mem_q"] = _mm(h1, dqm, "tn", "mem_q_dw", tm=1024, tk=512)
    G["w_mem_k"] = _mm(mem, dkm, "tn", "mem_k_dw", tm=1024)
    G["w_mem_v"] = _mm(mem, dvm, "tn", "mem_v_dw", tm=1024)
    tie = emit(("w_mem_o", "w_mem_q", "w_mem_k", "w_mem_v"), G)
    dh1_q = _mm(dqm, P["w_mem_q"], "nt", "mem_q_dx", tm=1024)
    dr1, G["ln1_g"], G["ln1_b"] = _ln_bwd([(h0, ALPHA), (mix, 1.0)], [(dr2, ALPHA), (dh1_q, 1.0)], P["ln1_g"] + tie, "ln1_bwd")

    dcat = _mm(dr1, P["w_mix_out"], "nt", "mix_out_dx", tm=1024)
    G["w_mix_out"] = _mm(cat, dr1, "tn", "mix_out_dw", tm=1024, tk=512)
    dq_all, dk_all, dv_all = _attn_bwd(q_all, kv_all, dcat, lse, _attn_delta(dcat, o_att))
    dq_pre = _rope_bwd_all(dq_all, tabs)
    G["w_q_up"] = _mm(qn, dq_pre, "tn", "q_up_dw", tk=512)
    dqn = _mm(dq_pre, P["w_q_up"], "nt", "q_up_dx", tm=1024)
    dkv_all = jnp.concatenate([dk_all, dv_all], axis=1)
    G["w_kv_up"] = _mm(kvn, dkv_all, "tn", "kv_up_dw", tk=512)
    dkvn = _mm(dkv_all, P["w_kv_up"], "nt", "kv_up_dx", tm=1024)
    dql, dkvl, dkr, G["q_norm_g"], G["kv_norm_g"] = _mla_prep_bwd(proj, P["q_norm_g"], P["kv_norm_g"], tabs, dqn, dkvn, dk_all)
    (dz, dxbc, ddt, G["conv_w"], G["conv_b"], G["dt_bias"], G["a_log"], G["d_skip"], G["ssd_norm_g"]) = _ssd_bwd(
        proj, states, dcat, P["conv_w"], P["conv_b"], P["dt_bias"], P["a_log"], P["d_skip"], P["ssd_norm_g"])
    tie = emit(("w_mix_out", "w_q_up", "w_kv_up", "conv_w"), G)
    S = x.shape[0]
    dproj = jnp.concatenate([dql, jnp.zeros((S, PZ - MLA_Q_RANK), F32) + tie, dz, dxbc, dkvl, ddt, dkr], axis=1)
    G["w_in"] = _mm(h0, dproj, "tn", "proj_in_dw", tm=1024, tn=640, tk=512)
    tie = emit(("w_in",), G)
    dh0_p = _mm(dproj, P["w_in"], "nt", "proj_in_dx", tm=1024, tk=640)
    gx, G["ln_in_g"], G["ln_in_b"] = _ln_bwd([(x, 1.0)], [(dr1, ALPHA), (dh0_p, 1.0)], P["ln_in_g"] + tie, "ln_in_bwd")
    return loss, gx, G


PACK_W = 1024
BIG = (("w_in", (1024, 277), 1), ("conv_w", (4, 128), 1), ("w_q_up", (384, 96), 1), ("w_kv_up", (256, 128), 1),
       ("w_mix_out", (128, 1024), 0), ("w_mem_q", (128, 1024), 0), ("w_mem_k", (128, 1024), 0), ("w_mem_v", (128, 1024), 0),
       ("w_mem_o", (128, 1024), 0), ("w_up", (1024, 512), 1), ("w_down", (512, 1024), 0))
SMALL = ("ln_in_g", "ln_in_b", "conv_b", "ln1_g", "ln1_b", "ln2_g", "ln2_b", "ln3_g", "ln3_b",
         "ssd_norm_g", "q_norm_g", "kv_norm_g", "dt_bias", "a_log", "d_skip")
ALL_W = ("ln_in_g", "ln_in_b", "w_in", "conv_w", "conv_b", "dt_bias", "a_log", "d_skip", "ssd_norm_g", "q_norm_g", "w_q_up",
         "kv_norm_g", "w_kv_up", "w_mix_out", "ln1_g", "ln1_b", "w_mem_q", "w_mem_k", "w_mem_v", "w_mem_o", "ln2_g", "ln2_b",
         "w_up", "w_down", "ln3_g", "ln3_b")


def _rows_of(shape):
    return -(-(shape[0] * shape[1]) // PACK_W)


BIG_ROWS = sum(_rows_of(s) for _, s, _ in BIG)
BIG_R = -(-BIG_ROWS // 32) * 32
SMALL_R = 16
LOSS_ROW = 15
ADAM_TILE = 96
assert BIG_R % ADAM_TILE == 0


def _flat_rows(a, rows):
    lead = a.shape[:-2]
    f = a.reshape(lead + (-1,))
    f = jnp.pad(f, [(0, 0)] * len(lead) + [(0, rows * PACK_W - f.shape[-1])])
    return f.reshape(lead + (rows, PACK_W))


def _pack_big(arrs, dtype):
    parts = []
    for n, s, _ in BIG:
        if n == "conv_w" and dtype == BF16:
            parts.append(lax.bitcast_convert_type(arrs[n].astype(F32), BF16).reshape(1, PACK_W))
        else:
            parts.append(_flat_rows(arrs[n].astype(dtype), _rows_of(s)))
    parts.append(jnp.zeros((BIG_R - BIG_ROWS, PACK_W), dtype))
    return jnp.concatenate(parts, axis=0)


def _unpack_big(buf):
    out, r = {}, 0
    for n, s, _ in BIG:
        k = _rows_of(s)
        out[n] = buf[r:r + k].reshape(-1)[:s[0] * s[1]].reshape(s)
        r += k
    return out


def _unpack_gathered(buf):
    out, r = {}, 0
    for n, s, ax in BIG:
        k = _rows_of(s)
        if n == "conv_w":
            sh = lax.bitcast_convert_type(buf[:, r:r + k].reshape((N_DEV,) + s + (2,)), F32)
        else:
            sh = buf[:, r:r + k].reshape(N_DEV, -1)[:, :s[0] * s[1]].reshape((N_DEV,) + s)
        out[n] = sh.reshape(N_DEV * s[0], s[1]) if ax == 0 else sh.transpose(1, 0, 2).reshape(s[0], N_DEV * s[1])
        r += k
    return out


def _pack_grads(G):
    parts = []
    for n, s, ax in BIG:
        g = G[n]
        sh = g.reshape((N_DEV,) + s) if ax == 0 else g.reshape(s[0], N_DEV, s[1]).transpose(1, 0, 2)
        parts.append(_flat_rows(sh, _rows_of(s)))
    parts.append(jnp.zeros((N_DEV, BIG_R - BIG_ROWS, PACK_W), F32))
    return jnp.concatenate(parts, axis=1)


_SMALL_ROWS = (("ln_in_g",), ("ln_in_b",), ("conv_b",), ("ln1_g",), ("ln1_b",), ("ln2_g",), ("ln2_b",), ("ln3_g",), ("ln3_b",),
               ("ssd_norm_g", "q_norm_g"), ("kv_norm_g", "dt_bias", "a_log", "d_skip"))
_SMALL_W = {"ssd_norm_g": 512, "q_norm_g": 384, "kv_norm_g": 256, "dt_bias": LANES, "a_log": LANES, "d_skip": LANES}


def _pack_small(V, extra_row=None):
    rows = []
    for names in _SMALL_ROWS:
        r = jnp.concatenate([V[n] for n in names], axis=1)
        rows.append(jnp.pad(r, ((0, 0), (0, PACK_W - r.shape[1]))))
    rows.append(jnp.zeros((SMALL_R - len(rows) - 1, PACK_W), F32))
    rows.append(jnp.zeros((1, PACK_W), F32) if extra_row is None else extra_row)
    return jnp.concatenate(rows, axis=0)


def _unpack_small(buf):
    out = {}
    for i, names in enumerate(_SMALL_ROWS):
        c = 0
        for n in names:
            w = _SMALL_W.get(n, PACK_W)
            out[n] = buf[i:i + 1, c:c + w]
            c += w
    return out


MESH = pl.DeviceIdType.MESH
ANY = pl.BlockSpec(memory_space=pl.ANY)
VM = pl.BlockSpec(memory_space=pltpu.VMEM)


def _coords():
    return lax.axis_index("x"), lax.axis_index("y"), lax.axis_index("c")


def _slot(px, py, pc):
    return 4 * px + 2 * py + pc


def _gather_big(shard):
    R, W = shard.shape

    def body(x_ref, out_ref, send_sems, recv_sems, local_sem):
        x, y, c = _coords()
        me, sibling = (x, y, c), (x, y, 1 - c)
        chips = [(1 - x, y), (x, 1 - y), (1 - x, 1 - y)]

        def rows(px, py, pc):
            return out_ref.at[_slot(px, py, pc)]

        def copy(k, block, to, src=None):
            return pltpu.make_async_remote_copy(
                src_ref=rows(*block) if src is None else src, dst_ref=rows(*block),
                send_sem=send_sems.at[k], recv_sem=recv_sems.at[k], device_id=to, device_id_type=MESH)

        mine = pltpu.make_async_copy(x_ref, rows(*me), local_sem)
        mine.start()
        first = [copy(0, me, sibling, src=x_ref)]
        first += [copy(1 + j, me, (*chip, c), src=x_ref) for j, chip in enumerate(chips)]
        for cp in first:
            cp.start()
        passed = [copy(4 + j, (*chip, c), sibling) for j, chip in enumerate(chips)]
        for j, chip in enumerate(chips):
            copy(1 + j, (*chip, c), me).wait_recv()
            passed[j].start()
        copy(0, sibling, me).wait_recv()
        for j, chip in enumerate(chips):
            copy(4 + j, (*chip, 1 - c), me).wait_recv()
        for cp in first + passed:
            cp.wait_send()
        mine.wait()

    return pl.pallas_call(
        body, name="gather_weights", out_shape=jax.ShapeDtypeStruct((N_DEV, R, W), shard.dtype),
        in_specs=[ANY], out_specs=ANY,
        scratch_shapes=[pltpu.SemaphoreType.DMA((7,)), pltpu.SemaphoreType.DMA((7,)), pltpu.SemaphoreType.DMA],
    )(shard)


def _peer(k, x, y, c):
    dx, dy, dc = (k >> 2) & 1, (k >> 1) & 1, k & 1
    return (1 - x if dx else x, 1 - y if dy else y, 1 - c if dc else c)


def _scatter_grads(gpack):
    _, R, W = gpack.shape

    def body(g_ref, out_ref, send_sems, recv_sems, local_sem):
        x, y, c = _coords()
        my = _slot(x, y, c)
        mine = pltpu.make_async_copy(g_ref.at[my], out_ref.at[my], local_sem)
        mine.start()
        cps = []
        for k in range(1, N_DEV):
            peer = _peer(k, x, y, c)
            cps.append(pltpu.make_async_remote_copy(
                src_ref=g_ref.at[_slot(*peer)], dst_ref=out_ref.at[my],
                send_sem=send_sems.at[k - 1], recv_sem=recv_sems.at[k - 1], device_id=peer, device_id_type=MESH))
        for cp in cps:
            cp.start()
        for cp in cps:
            cp.wait_recv()
        for cp in cps:
            cp.wait_send()
        mine.wait()

    return pl.pallas_call(
        body, name="scatter_grads", out_shape=jax.ShapeDtypeStruct(gpack.shape, gpack.dtype),
        in_specs=[ANY], out_specs=ANY,
        scratch_shapes=[pltpu.SemaphoreType.DMA((7,)), pltpu.SemaphoreType.DMA((7,)), pltpu.SemaphoreType.DMA],
    )(gpack)


def _adam(w, g, m, v):
    m = ADAM_B1 * m + (1.0 - ADAM_B1) * g
    v = ADAM_B2 * v + (1.0 - ADAM_B2) * (g * g)
    m_hat = m / (1.0 - ADAM_B1 ** ADAM_STEP)
    v_hat = v / (1.0 - ADAM_B2 ** ADAM_STEP)
    delta = -ADAM_LR * (m_hat / (jnp.sqrt(v_hat) + ADAM_EPS) + ADAM_WD * w)
    return delta, m, v


def _sum_slots(ref):
    tot = ref[0]
    for q in range(1, N_DEV):
        tot = tot + ref[q]
    return tot


def _reduce_adam_big(recv, w, m, v):
    _, R, W = recv.shape

    def body(r_ref, w_ref, m_ref, v_ref, g_ref, d_ref, nm_ref, nv_ref):
        g = _sum_slots(r_ref)
        g_ref[...] = g
        d_ref[...], nm_ref[...], nv_ref[...] = _adam(w_ref[...], g, m_ref[...], v_ref[...])

    row = pl.BlockSpec((ADAM_TILE, W), lambda i: (i, 0))
    return pl.pallas_call(
        body, name="reduce_adam", grid=(R // ADAM_TILE,),
        in_specs=[pl.BlockSpec((N_DEV, ADAM_TILE, W), lambda i: (0, i, 0)), row, row, row], out_specs=[row] * 4,
        out_shape=[jax.ShapeDtypeStruct((R, W), F32)] * 4, compiler_params=_cparams(("parallel",)),
    )(recv, w, m, v)


def _allreduce_adam_small(gs, w, m, v):
    R, W = gs.shape

    def body(g_ref, w_ref, m_ref, v_ref, go_ref, d_ref, nm_ref, nv_ref, land, send_sems, recv_sems):
        x, y, c = _coords()
        my = _slot(x, y, c)
        cps = []
        for k in range(1, N_DEV):
            peer = _peer(k, x, y, c)
            cps.append(pltpu.make_async_remote_copy(
                src_ref=g_ref, dst_ref=land.at[my], send_sem=send_sems.at[k - 1], recv_sem=recv_sems.at[k - 1],
                device_id=peer, device_id_type=MESH))
        for cp in cps:
            cp.start()
        land[my] = g_ref[...]
        for cp in cps:
            cp.wait_recv()
        for cp in cps:
            cp.wait_send()
        g = _sum_slots(land)
        go_ref[...] = g
        d_ref[...], nm_ref[...], nv_ref[...] = _adam(w_ref[...], g, m_ref[...], v_ref[...])

    return pl.pallas_call(
        body, name="allreduce_small", in_specs=[VM] * 4, out_specs=[VM] * 4,
        out_shape=[jax.ShapeDtypeStruct((R, W), F32)] * 4,
        scratch_shapes=[pltpu.VMEM((N_DEV, R, W), F32), pltpu.SemaphoreType.DMA((7,)), pltpu.SemaphoreType.DMA((7,))],
    )(gs, w, m, v)


def _row(v):
    return v.reshape(1, -1).astype(F32)


def _step(x, mem, positions, tgt, W, M, V):
    shard = {n: W[n][0] for n, _, _ in BIG}
    full = _unpack_gathered(_gather_big(_pack_big(shard, BF16)))
    P = {n: _row(W[n]) for n in SMALL}
    for n in ("dt_bias", "a_log", "d_skip"):
        P[n] = _head_lanes(P[n])
    P.update(w_in=_pad_w_in(full["w_in"]), w_q_up=_pad_w_q(full["w_q_up"]), w_kv_up=_pad_w_kv(full["w_kv_up"]),
             conv_w=jnp.pad(full["conv_w"].astype(F32), ((0, 8 - SSD_CONV), (0, 0))))
    for n in ("w_mix_out", "w_mem_q", "w_mem_k", "w_mem_v", "w_mem_o", "w_up", "w_down"):
        P[n] = full[n]

    loss, gx, G = _local_step(x[0], mem[0], positions[0], tgt[0], P)

    G["w_in"], G["w_q_up"], G["w_kv_up"] = _unpad_w_in(G["w_in"]), _unpad_w_q(G["w_q_up"]), _unpad_w_kv(G["w_kv_up"])
    G["conv_w"] = G["conv_w"][:SSD_CONV]
    recv = _scatter_grads(_pack_grads(G))
    big = [_unpack_big(b) for b in _reduce_adam_big(
        recv, _pack_big(shard, F32), _pack_big({n: M[n][0] for n, _, _ in BIG}, F32), _pack_big({n: V[n][0] for n, _, _ in BIG}, F32))]

    def small_rows(D):
        rows = {n: _row(D[n]) for n in SMALL}
        for n in ("dt_bias", "a_log", "d_skip"):
            rows[n] = _head_lanes(rows[n])
        return rows

    loss_row = jnp.broadcast_to(loss[:, :1], (1, PACK_W))
    small_bufs = _allreduce_adam_small(
        _pack_small({n: G[n] for n in SMALL}, loss_row), _pack_small(small_rows(W)), _pack_small(small_rows(M)),
        _pack_small(small_rows(V)))
    loss_tot = small_bufs[0][LOSS_ROW, 0]
    small = [_unpack_small(b) for b in small_bufs]

    outs = []
    for res_b, res_s in zip(big, small):
        for n in ALL_W:
            if n in res_b:
                outs.append(res_b[n].reshape(W[n].shape))
            else:
                outs.append(res_s[n][:, :W[n].size].reshape(W[n].shape))
    return (loss_tot, gx[None], *outs)


def kernel(x, mem, positions, ln_in_g, ln_in_b, w_in, conv_w, conv_b, dt_bias, a_log, d_skip, ssd_norm_g, q_norm_g, w_q_up, kv_norm_g, w_kv_up, w_mix_out, ln1_g, ln1_b, w_mem_q, w_mem_k, w_mem_v, w_mem_o, ln2_g, ln2_b, w_up, w_down, ln3_g, ln3_b, loss_target, m_ln_in_g, m_ln_in_b, m_w_in, m_conv_w, m_conv_b, m_dt_bias, m_a_log, m_d_skip, m_ssd_norm_g, m_q_norm_g, m_w_q_up, m_kv_norm_g, m_w_kv_up, m_w_mix_out, m_ln1_g, m_ln1_b, m_w_mem_q, m_w_mem_k, m_w_mem_v, m_w_mem_o, m_ln2_g, m_ln2_b, m_w_up, m_w_down, m_ln3_g, m_ln3_b, v_ln_in_g, v_ln_in_b, v_w_in, v_conv_w, v_conv_b, v_dt_bias, v_a_log, v_d_skip, v_ssd_norm_g, v_q_norm_g, v_w_q_up, v_kv_norm_g, v_w_kv_up, v_w_mix_out, v_ln1_g, v_ln1_b, v_w_mem_q, v_w_mem_k, v_w_mem_v, v_w_mem_o, v_ln2_g, v_ln2_b, v_w_up, v_w_down, v_ln3_g, v_ln3_b):
    a = dict(locals())
    W = {n: a[n] for n in ALL_W}
    M = {n: a["m_" + n] for n in ALL_W}
    V = {n: a["v_" + n] for n in ALL_W}
    return _step_overlapped(x, mem, positions, loss_target, W, M, V)


HBM = pl.BlockSpec(memory_space=pltpu.HBM)
SEM = pl.BlockSpec(memory_space=pltpu.SEMAPHORE)
EFFECT = pltpu.SideEffectType.DATAFLOW_SIDE_EFFECTING
SHARD_SHAPE = {n: s for n, s, _ in BIG}
SHARD_AXIS = {n: ax for n, _, ax in BIG}
GATHER_NOW = ("w_in", "conv_w", "w_q_up", "w_kv_up", "w_mix_out")
GATHER_LATE = ("w_mem_q", "w_mem_k", "w_mem_v", "w_mem_o", "w_up", "w_down")


def _my_slot():
    return _slot(*_coords())


def _group_copies(src_refs, land_refs, send_sems, recv_sems, slotted, landing_of_peer):
    x, y, c = _coords()
    my = _slot(x, y, c)
    cps = []
    for a, (s_ref, l_ref) in enumerate(zip(src_refs, land_refs)):
        for k in range(1, N_DEV):
            peer = _peer(k, x, y, c)
            cps.append(pltpu.make_async_remote_copy(
                src_ref=s_ref.at[_slot(*peer)] if slotted else s_ref,
                dst_ref=l_ref.at[_slot(*peer)] if landing_of_peer else l_ref.at[my],
                send_sem=send_sems.at[7 * a + k - 1], recv_sem=recv_sems.at[7 * a + k - 1],
                device_id=peer, device_id_type=MESH))
    return cps


def _send_start(srcs, lands, slotted, name):
    n = len(srcs)

    def body(*refs):
        for cp in _group_copies(refs[:n], refs[n:2 * n], refs[2 * n], refs[2 * n + 1], slotted, False):
            cp.start()
        refs[-1][...] = jnp.zeros(refs[-1].shape, F32)

    res = pl.pallas_call(
        body, name=name,
        out_shape=(pltpu.SemaphoreType.DMA((7 * n,)), pltpu.SemaphoreType.DMA((7 * n,)),
                   *[pltpu.HBM(a.shape, a.dtype) for a in srcs], *[pltpu.HBM(a.shape, a.dtype) for a in lands],
                   jax.ShapeDtypeStruct((8, LANES), F32)),
        in_specs=[HBM] * (2 * n), out_specs=(SEM, SEM, *[HBM] * (2 * n), VM),
        input_output_aliases={i: 2 + i for i in range(2 * n)},
        compiler_params=pltpu.CompilerParams(has_side_effects=EFFECT),
    )(*[pltpu.with_memory_space_constraint(a, pltpu.HBM) for a in list(srcs) + list(lands)])
    return (res[0], res[1], res[2:2 + n], res[2 + n:2 + 2 * n]), res[-1][:1, :1]


def _send_wait(started, after, slotted, name):
    send_sems, recv_sems, srcs, lands = started
    n = len(srcs)

    def body(*refs):
        for cp in _group_copies(refs[:n], refs[n:2 * n], refs[2 * n], refs[2 * n + 1], slotted, True):
            cp.wait_send()
            cp.wait_recv()

    res = pl.pallas_call(
        body, name=name, out_shape=tuple(pltpu.HBM(a.shape, a.dtype) for a in list(srcs) + list(lands)),
        in_specs=[HBM] * (2 * n) + [SEM, SEM, ANY], out_specs=tuple([HBM] * (2 * n)),
        input_output_aliases={i: i for i in range(2 * n)},
        compiler_params=pltpu.CompilerParams(has_side_effects=EFFECT),
    )(*srcs, *lands, send_sems, recv_sems, after)
    return res[n:]


def _landing(own, my):
    return lax.dynamic_update_slice(jnp.zeros((N_DEV,) + own.shape, own.dtype), own[None], (my,) + (0,) * own.ndim)


def _gather_now(shards):
    n = len(shards)

    def body(*refs):
        x_refs, out_refs = refs[:n], refs[n:2 * n]
        send_sems, recv_sems, local_sems = refs[2 * n:]
        x, y, c = _coords()
        me, sibling = (x, y, c), (x, y, 1 - c)
        chips = [(1 - x, y), (x, 1 - y), (1 - x, 1 - y)]

        def copy(a, k, block, to, src=None):
            rows = out_refs[a].at[_slot(*block)]
            return pltpu.make_async_remote_copy(
                src_ref=rows if src is None else src, dst_ref=rows,
                send_sem=send_sems.at[7 * a + k], recv_sem=recv_sems.at[7 * a + k], device_id=to, device_id_type=MESH)

        mine = [pltpu.make_async_copy(x_refs[a], out_refs[a].at[_slot(*me)], local_sems.at[a]) for a in range(n)]
        for cp in mine:
            cp.start()
        first = []
        for a in range(n):
            first.append(copy(a, 0, me, sibling, src=x_refs[a]))
            first += [copy(a, 1 + j, me, (*chip, c), src=x_refs[a]) for j, chip in enumerate(chips)]
        for cp in first:
            cp.start()
        passed = []
        for j, chip in enumerate(chips):
            for a in range(n):
                copy(a, 1 + j, (*chip, c), me).wait_recv()
                fwd = copy(a, 4 + j, (*chip, c), sibling)
                fwd.start()
                passed.append(fwd)
        for a in range(n):
            copy(a, 0, sibling, me).wait_recv()
            for j, chip in enumerate(chips):
                copy(a, 4 + j, (*chip, 1 - c), me).wait_recv()
        for cp in first + passed:
            cp.wait_send()
        for cp in mine:
            cp.wait()

    return pl.pallas_call(
        body, name="gather_now", out_shape=[jax.ShapeDtypeStruct((N_DEV,) + s.shape, s.dtype) for s in shards],
        in_specs=[ANY] * n, out_specs=[ANY] * n,
        scratch_shapes=[pltpu.SemaphoreType.DMA((7 * n,)), pltpu.SemaphoreType.DMA((7 * n,)), pltpu.SemaphoreType.DMA((n,))],
    )(*shards)


def _full_from_slots(name, slots):
    a, b = SHARD_SHAPE[name]
    return slots.reshape(N_DEV * a, b) if SHARD_AXIS[name] == 0 else slots.transpose(1, 0, 2).reshape(a, N_DEV * b)


def _slots_from_full(name, g):
    a, b = SHARD_SHAPE[name]
    return g.reshape(N_DEV, a, b) if SHARD_AXIS[name] == 0 else g.reshape(a, N_DEV, b).transpose(1, 0, 2)


def _reduce_adam(recv, w, m, v, name):
    _, a, b = recv.shape
    ta = a
    while ta * b * 4 * N_DEV > 4 * 1024 * 1024 and ta % 16 == 0:
        ta //= 2

    def body(r_ref, w_ref, m_ref, v_ref, g_ref, d_ref, nm_ref, nv_ref):
        g = _sum_slots(r_ref)
        g_ref[...] = g
        d_ref[...], nm_ref[...], nv_ref[...] = _adam(w_ref[...], g, m_ref[...], v_ref[...])

    row = pl.BlockSpec((ta, b), lambda i: (i, 0))
    return pl.pallas_call(
        body, name=name, grid=(a // ta,),
        in_specs=[pl.BlockSpec((N_DEV, ta, b), lambda i: (0, i, 0)), row, row, row], out_specs=[row] * 4,
        out_shape=[jax.ShapeDtypeStruct((a, b), F32)] * 4, compiler_params=_cparams(("parallel",)),
    )(recv, w, m, v)


def _step_overlapped(x, mem, positions, tgt, W, M, V):
    my = _my_slot()
    shard = {n: W[n][0] for n, _, _ in BIG}
    send = {n: (shard[n] if n == "conv_w" else shard[n].astype(BF16)) for n in shard}

    now = dict(zip(GATHER_NOW, _gather_now([send[n] for n in GATHER_NOW])))
    late_src = [send[n] for n in GATHER_LATE]
    late_src, _ = lax.optimization_barrier((late_src, now["w_mix_out"]))
    late, tie = _send_start(late_src, [_landing(s, my) for s in late_src], False, "gather_late_start")

    P = {n: _row(W[n]) for n in SMALL}
    for n in ("dt_bias", "a_log", "d_skip"):
        P[n] = _head_lanes(P[n])
    P["ln_in_g"] = P["ln_in_g"] + tie
    full = {n: _full_from_slots(n, now[n]) for n in GATHER_NOW}
    P.update(w_in=_pad_w_in(full["w_in"]), w_q_up=_pad_w_q(full["w_q_up"]), w_kv_up=_pad_w_kv(full["w_kv_up"]),
             conv_w=jnp.pad(full["conv_w"], ((0, 8 - SSD_CONV), (0, 0))), w_mix_out=full["w_mix_out"])

    def late_weights(after):
        lands = _send_wait(late, after, False, "gather_late_wait")
        return {n: _full_from_slots(n, l) for n, l in zip(GATHER_LATE, lands)}

    started = []

    def emit(names, G):
        srcs = []
        for n in names:
            g = G[n]
            if n == "w_in":
                g = _unpad_w_in(g)
            elif n == "w_q_up":
                g = _unpad_w_q(g)
            elif n == "w_kv_up":
                g = _unpad_w_kv(g)
            elif n == "conv_w":
                g = g[:SSD_CONV]
            srcs.append(g if g.ndim == 3 else _slots_from_full(n, g))
        lands = [_landing(lax.dynamic_index_in_dim(s, my, 0, keepdims=False), my) for s in srcs]
        st, tie = _send_start(srcs, lands, True, "scatter_start_%d" % len(started))
        started.append((names, st))
        return tie

    loss, gx, G = _local_step(x[0], mem[0], positions[0], tgt[0], P, late_weights, emit)

    res = {}
    for i, (names, st) in enumerate(started):
        lands = _send_wait(st, gx, True, "scatter_wait_%d" % i)
        for n, recv in zip(names, lands):
            res[n] = _reduce_adam(recv, shard[n], M[n][0], V[n][0], "reduce_adam_" + n)

    def small_rows(D):
        rows = {n: _row(D[n]) for n in SMALL}
        for n in ("dt_bias", "a_log", "d_skip"):
            rows[n] = _head_lanes(rows[n])
        return rows

    loss_row = jnp.broadcast_to(loss[:, :1], (1, PACK_W))
    small_bufs = _allreduce_adam_small(
        _pack_small({n: G[n] for n in SMALL}, loss_row), _pack_small(small_rows(W)), _pack_small(small_rows(M)),
        _pack_small(small_rows(V)))
    loss_tot = small_bufs[0][LOSS_ROW, 0]
    small = [_unpack_small(b) for b in small_bufs]

    outs = []
    for j in range(4):
        for n in ALL_W:
            if n in res:
                outs.append(res[n][j].reshape(W[n].shape))
            else:
                outs.append(small[j][n][:, :W[n].size].reshape(W[n].shape))
    return (loss_tot, gx[None], *outs)
```

```python
import functools
import math

import jax
import jax.numpy as jnp
from jax import lax
from jax.experimental import pallas as pl
from jax.experimental.pallas import tpu as pltpu

F32, BF16 = jnp.float32, jnp.bfloat16

N_DEV = 8
D_MODEL = 1024
SSD_HEADS, SSD_HEAD_DIM, SSD_INNER, SSD_STATE, SSD_CONV, SSD_CHUNK = 8, 64, 512, 128, 4, 128
SSD_XBC = 1024
MLA_HEADS, MLA_NOPE, MLA_ROPE, MLA_QK, MLA_V = 8, 64, 32, 96, 64
MLA_Q_RANK, MLA_KV_RANK = 384, 256
ROPE_THETA = 10000.0
MEM_HEADS, MEM_HEAD_DIM = 4, 256
D_FF = 4096
IN_WIDTH = 2216
LN_EPS, RMS_EPS = 1e-5, 1e-6
ALPHA = 2.0 ** 0.25
ADAM_LR, ADAM_B1, ADAM_B2, ADAM_EPS, ADAM_WD, ADAM_STEP = 0.001, 0.9, 0.999, 1e-08, 0.01, 10

LANES = 128
NEG = -1e30
VMEM_LIMIT = 56 * 1024 * 1024

PQ, PZ, PX, PKV, PDT, PKR, PW = 0, 512, 1024, 2048, 2304, 2432, 2560
KR_LANE = 64


def _cparams(sem):
    return pltpu.CompilerParams(dimension_semantics=sem, vmem_limit_bytes=VMEM_LIMIT)


def _sigmoid(x):
    return 1.0 / (1.0 + jnp.exp(-x))


def _mm(a, b, mode, name, *, tm=512, tn=None, tk=None, out_dtypes=(F32,), epi=None, extras=(), col_slots=False):
    if mode == "nn":
        (M, K), (K2, N) = a.shape, b.shape
    elif mode == "nt":
        (M, K), (N, K2) = a.shape, b.shape
    else:
        (K, M), (K2, N) = a.shape, b.shape
    assert K == K2, (name, a.shape, b.shape)
    tm, tn, tk = min(tm, M), min(tn or N, N), min(tk or K, K)
    assert M % tm == 0 and N % tn == 0 and K % tk == 0, (name, M, N, K, tm, tn, tk)
    gk = K // tk
    a_spec = pl.BlockSpec((tk, tm), lambda i, j, k: (k, i)) if mode == "tn" else pl.BlockSpec((tm, tk), lambda i, j, k: (i, k))
    b_spec = pl.BlockSpec((tn, tk), lambda i, j, k: (j, k)) if mode == "nt" else pl.BlockSpec((tk, tn), lambda i, j, k: (k, j))
    dims = {"nn": ((1,), (0,)), "nt": ((1,), (1,)), "tn": ((0,), (0,))}[mode]
    ex_specs = []
    for arr, kind in extras:
        if kind == "mn":
            ex_specs.append(pl.BlockSpec((tm, tn), lambda i, j, k: (i, j)))
        elif kind == "n":
            ex_specs.append(pl.BlockSpec((1, tn), lambda i, j, k: (0, j)))
        else:
            ex_specs.append(pl.BlockSpec((tm, arr.shape[1]), lambda i, j, k: (i, 0)))
    ne, no = len(extras), len(out_dtypes)

    def body(*refs):
        a_ref, b_ref = refs[0], refs[1]
        ex, outs = refs[2:2 + ne], refs[2 + ne:2 + ne + no]
        part = lax.dot_general(a_ref[...].astype(BF16), b_ref[...].astype(BF16), (dims, ((), ())),
                               preferred_element_type=F32)

        def finish(acc):
            res = epi(acc, *[e[...] for e in ex]) if epi is not None else (acc,)
            for o, r in zip(outs, res):
                o[...] = r.astype(o.dtype)

        if gk == 1:
            finish(part)
        else:
            acc_ref = refs[-1]
            k = pl.program_id(2)

            @pl.when(k == 0)
            def _():
                acc_ref[...] = part

            @pl.when(k > 0)
            def _():
                acc_ref[...] += part

            @pl.when(k == gk - 1)
            def _():
                finish(acc_ref[...])

    res = pl.pallas_call(
        body, name=name, grid=(M // tm, N // tn, gk),
        in_specs=[a_spec, b_spec] + ex_specs,
        out_specs=[pl.BlockSpec((None, tm, tn), lambda i, j, k: (j, i, 0)) if col_slots else pl.BlockSpec((tm, tn), lambda i, j, k: (i, j))
                   for _ in out_dtypes],
        out_shape=[jax.ShapeDtypeStruct((N // tn, M, tn) if col_slots else (M, N), dt) for dt in out_dtypes],
        scratch_shapes=[pltpu.VMEM((tm, tn), F32)] if gk > 1 else [],
        compiler_params=_cparams(("parallel", "parallel", "arbitrary")),
    )(a, b, *[e[0] for e in extras])
    return res[0] if no == 1 else res


def _ln_stats(r):
    mu = jnp.mean(r, axis=-1, keepdims=True)
    xc = r - mu
    var = jnp.mean(xc * xc, axis=-1, keepdims=True)
    rstd = lax.rsqrt(var + LN_EPS)
    return xc * rstd, rstd


def _ln_fwd(terms, g, b, name, tm=512):
    S, D = terms[0][0].shape
    coefs = [c for _, c in terms]
    nt = len(terms)

    def body(*refs):
        r = sum(c * t[...] for t, c in zip(refs[:nt], coefs))
        xh, _ = _ln_stats(r)
        h = xh * refs[nt][...] + refs[nt + 1][...]
        refs[nt + 2][...] = h
        refs[nt + 3][...] = h.astype(BF16)

    row = pl.BlockSpec((tm, D), lambda i: (i, 0))
    vec = pl.BlockSpec((1, D), lambda i: (0, 0))
    return pl.pallas_call(
        body, name=name, grid=(S // tm,), in_specs=[row] * nt + [vec, vec], out_specs=[row, row],
        out_shape=[jax.ShapeDtypeStruct((S, D), F32), jax.ShapeDtypeStruct((S, D), BF16)], compiler_params=_cparams(("parallel",)),
    )(*[t for t, _ in terms], g, b)


def _ln_bwd(terms, dterms, g, name, tm=512):
    S, D = terms[0][0].shape
    coefs, dcoefs = [c for _, c in terms], [c for _, c in dterms]
    nt, nd = len(terms), len(dterms)

    def body(*refs):
        i = pl.program_id(0)
        r = sum(c * t[...] for t, c in zip(refs[:nt], coefs))
        dh = sum(c * t[...].astype(F32) for t, c in zip(refs[nt:nt + nd], dcoefs))
        g_ref = refs[nt + nd]
        dr_ref, drb_ref, dg_ref, db_ref = refs[nt + nd + 1:]
        xh, rstd = _ln_stats(r)
        dxh = dh * g_ref[...]
        m1 = jnp.mean(dxh, axis=-1, keepdims=True)
        m2 = jnp.mean(dxh * xh, axis=-1, keepdims=True)
        dr = rstd * (dxh - m1 - xh * m2)
        dr_ref[...] = dr
        drb_ref[...] = dr.astype(BF16)
        pg = jnp.sum(dh * xh, axis=0, keepdims=True)
        pb = jnp.sum(dh, axis=0, keepdims=True)

        @pl.when(i == 0)
        def _():
            dg_ref[...] = pg
            db_ref[...] = pb

        @pl.when(i > 0)
        def _():
            dg_ref[...] += pg
            db_ref[...] += pb

    row = pl.BlockSpec((tm, D), lambda i: (i, 0))
    vec = pl.BlockSpec((1, D), lambda i: (0, 0))
    return pl.pallas_call(
        body, name=name, grid=(S // tm,), in_specs=[row] * (nt + nd) + [vec], out_specs=[row, row, vec, vec],
        out_shape=[jax.ShapeDtypeStruct((S, D), F32), jax.ShapeDtypeStruct((S, D), BF16), jax.ShapeDtypeStruct((1, D), F32),
                   jax.ShapeDtypeStruct((1, D), F32)],
        compiler_params=_cparams(("arbitrary",)),
    )(*[t for t, _ in terms], *[t for t, _ in dterms], g)


def _ln_loss_bwd(terms, g, b, tgt, name, tm=512):
    S, D = terms[0][0].shape
    coefs = [c for _, c in terms]
    nt = len(terms)

    def body(*refs):
        i = pl.program_id(0)
        r = sum(c * t[...] for t, c in zip(refs[:nt], coefs))
        g_ref, b_ref, t_ref = refs[nt:nt + 3]
        loss_ref, dr_ref, drb_ref, dg_ref, db_ref = refs[nt + 3:]
        xh, rstd = _ln_stats(r)
        h = xh * g_ref[...] + b_ref[...]
        diff = h - t_ref[...]
        pl_ = 0.5 * jnp.sum(jnp.mean(diff * diff, axis=-1, keepdims=True), axis=0, keepdims=True)
        dh = diff * (1.0 / D)
        dxh = dh * g_ref[...]
        m1 = jnp.mean(dxh, axis=-1, keepdims=True)
        m2 = jnp.mean(dxh * xh, axis=-1, keepdims=True)
        dr = rstd * (dxh - m1 - xh * m2)
        dr_ref[...] = dr
        drb_ref[...] = dr.astype(BF16)
        pg = jnp.sum(dh * xh, axis=0, keepdims=True)
        pb = jnp.sum(dh, axis=0, keepdims=True)
        plb = jnp.broadcast_to(pl_, (1, LANES))

        @pl.when(i == 0)
        def _():
            dg_ref[...] = pg
            db_ref[...] = pb
            loss_ref[...] = plb

        @pl.when(i > 0)
        def _():
            dg_ref[...] += pg
            db_ref[...] += pb
            loss_ref[...] += plb

    row = pl.BlockSpec((tm, D), lambda i: (i, 0))
    vec = pl.BlockSpec((1, D), lambda i: (0, 0))
    lvec = pl.BlockSpec((1, LANES), lambda i: (0, 0))
    return pl.pallas_call(
        body, name=name, grid=(S // tm,), in_specs=[row] * nt + [vec, vec, row], out_specs=[lvec, row, row, vec, vec],
        out_shape=[jax.ShapeDtypeStruct((1, LANES), F32), jax.ShapeDtypeStruct((S, D), F32), jax.ShapeDtypeStruct((S, D), BF16),
                   jax.ShapeDtypeStruct((1, D), F32), jax.ShapeDtypeStruct((1, D), F32)],
        compiler_params=_cparams(("arbitrary",)),
    )(*[t for t, _ in terms], g, b, tgt)


def _rope_tables(positions):
    half = MLA_ROPE // 2
    inv_freq = jnp.power(ROPE_THETA, -jnp.arange(half, dtype=F32) / half)
    ang = positions.astype(F32)[:, None] * inv_freq
    cos, sin = jnp.cos(ang), jnp.sin(ang)
    S = positions.shape[0]
    one, zero = jnp.ones((S, MLA_NOPE), F32), jnp.zeros((S, half), F32)
    pad = jnp.zeros((S, LANES - MLA_QK), F32)
    c = jnp.concatenate([one, cos, cos, pad], axis=1)
    s1 = jnp.concatenate([0 * one, -sin, zero, pad], axis=1)
    s2 = jnp.concatenate([0 * one, zero, sin, pad], axis=1)
    return c, s1, s2


def _rope_block(x, c, s1, s2):
    half = MLA_ROPE // 2
    return x * c + pltpu.roll(x, LANES - half, axis=1) * s1 + pltpu.roll(x, half, axis=1) * s2


def _rms_fwd(x, g):
    r = lax.rsqrt(jnp.mean(x * x, axis=-1, keepdims=True) + RMS_EPS)
    return x * r * g


def _rms_bwd(x, g, dy):
    r = lax.rsqrt(jnp.mean(x * x, axis=-1, keepdims=True) + RMS_EPS)
    xh = x * r
    dyh = dy * g
    dx = r * (dyh - xh * jnp.mean(dyh * xh, axis=-1, keepdims=True))
    return dx, jnp.sum(dy * xh, axis=0, keepdims=True)


def _mla_prep(proj, qg, kvg, tabs, tm=512):
    S = proj.shape[0]

    def body(ql_ref, kvl_ref, kr_ref, qg_ref, kvg_ref, c_ref, s1_ref, s2_ref, qn_ref, kvn_ref, kpe_ref):
        qn_ref[...] = _rms_fwd(ql_ref[...], qg_ref[...]).astype(BF16)
        kvn_ref[...] = _rms_fwd(kvl_ref[...], kvg_ref[...]).astype(BF16)
        kpe_ref[...] = _rope_block(kr_ref[...], c_ref[...], s1_ref[...], s2_ref[...])

    tab = pl.BlockSpec((tm, LANES), lambda i: (i, 0))
    return pl.pallas_call(
        body, name="mla_prep", grid=(S // tm,),
        in_specs=[pl.BlockSpec((tm, MLA_Q_RANK), lambda i: (i, PQ // MLA_Q_RANK)),
                  pl.BlockSpec((tm, MLA_KV_RANK), lambda i: (i, PKV // MLA_KV_RANK)),
                  pl.BlockSpec((tm, LANES), lambda i: (i, PKR // LANES)),
                  pl.BlockSpec((1, MLA_Q_RANK), lambda i: (0, 0)), pl.BlockSpec((1, MLA_KV_RANK), lambda i: (0, 0)),
                  tab, tab, tab],
        out_specs=[pl.BlockSpec((tm, MLA_Q_RANK), lambda i: (i, 0)), pl.BlockSpec((tm, MLA_KV_RANK), lambda i: (i, 0)), tab],
        out_shape=[jax.ShapeDtypeStruct((S, MLA_Q_RANK), BF16), jax.ShapeDtypeStruct((S, MLA_KV_RANK), BF16),
                   jax.ShapeDtypeStruct((S, LANES), F32)],
        compiler_params=_cparams(("parallel",)),
    )(proj, proj, proj, qg, kvg, *tabs)


def _mla_prep_bwd(proj, qg, kvg, tabs, dqn, dkvn, dk_all, tm=512):
    S = proj.shape[0]

    def body(ql_ref, kvl_ref, qg_ref, kvg_ref, c_ref, s1_ref, s2_ref, dqn_ref, dkvn_ref, dk_ref,
             dql_ref, dkvl_ref, dkr_ref, dqg_ref, dkvg_ref):
        i = pl.program_id(0)
        dql, pq = _rms_bwd(ql_ref[...], qg_ref[...], dqn_ref[...])
        dkvl, pkv = _rms_bwd(kvl_ref[...], kvg_ref[...], dkvn_ref[...])
        dql_ref[...] = dql
        dkvl_ref[...] = dkvl
        dk = dk_ref[...]
        dkpe = dk[:, 0:LANES]
        for h in range(1, MLA_HEADS):
            dkpe = dkpe + dk[:, h * LANES:(h + 1) * LANES]
        lane = lax.broadcasted_iota(jnp.int32, dkpe.shape, 1)
        dkpe = jnp.where((lane >= KR_LANE) & (lane < KR_LANE + MLA_ROPE), dkpe, 0.0)
        dkr_ref[...] = _rope_block(dkpe, c_ref[...], -s1_ref[...], -s2_ref[...])

        @pl.when(i == 0)
        def _():
            dqg_ref[...] = pq
            dkvg_ref[...] = pkv

        @pl.when(i > 0)
        def _():
            dqg_ref[...] += pq
            dkvg_ref[...] += pkv

    tab = pl.BlockSpec((tm, LANES), lambda i: (i, 0))
    qspec = pl.BlockSpec((tm, MLA_Q_RANK), lambda i: (i, 0))
    kvspec = pl.BlockSpec((tm, MLA_KV_RANK), lambda i: (i, 0))
    qv, kvv = pl.BlockSpec((1, MLA_Q_RANK), lambda i: (0, 0)), pl.BlockSpec((1, MLA_KV_RANK), lambda i: (0, 0))
    return pl.pallas_call(
        body, name="mla_prep_bwd", grid=(S // tm,),
        in_specs=[pl.BlockSpec((tm, MLA_Q_RANK), lambda i: (i, PQ // MLA_Q_RANK)),
                  pl.BlockSpec((tm, MLA_KV_RANK), lambda i: (i, PKV // MLA_KV_RANK)),
                  qv, kvv, tab, tab, tab, qspec, kvspec, pl.BlockSpec((tm, MLA_HEADS * LANES), lambda i: (i, 0))],
        out_specs=[qspec, kvspec, tab, qv, kvv],
        out_shape=[jax.ShapeDtypeStruct((S, MLA_Q_RANK), F32), jax.ShapeDtypeStruct((S, MLA_KV_RANK), F32),
                   jax.ShapeDtypeStruct((S, LANES), F32), jax.ShapeDtypeStruct((1, MLA_Q_RANK), F32),
                   jax.ShapeDtypeStruct((1, MLA_KV_RANK), F32)],
        compiler_params=_cparams(("arbitrary",)),
    )(proj, proj, qg, kvg, *tabs, dqn, dkvn, dk_all)


def _rope_bwd_all(dq_all, tabs, tm=512):
    S, W = dq_all.shape

    def body(dq_ref, c_ref, s1_ref, s2_ref, o_ref):
        c, s1, s2 = c_ref[...], -s1_ref[...], -s2_ref[...]
        for h in range(W // LANES):
            o_ref[:, h * LANES:(h + 1) * LANES] = _rope_block(dq_ref[:, h * LANES:(h + 1) * LANES], c, s1, s2).astype(BF16)

    tab = pl.BlockSpec((tm, LANES), lambda i: (i, 0))
    row = pl.BlockSpec((tm, W), lambda i: (i, 0))
    return pl.pallas_call(body, name="rope_bwd", grid=(S // tm,), in_specs=[row, tab, tab, tab], out_specs=row,
                          out_shape=jax.ShapeDtypeStruct((S, W), BF16), compiler_params=_cparams(("parallel",)))(dq_all, *tabs)


ATT_SCALE = MLA_QK ** -0.5
N_PAIR = MLA_HEADS // 2


def _causal_mask(qi, ki, tq, tk):
    row = qi * tq + lax.broadcasted_iota(jnp.int32, (tq, tk), 0)
    col = ki * tk + lax.broadcasted_iota(jnp.int32, (tq, tk), 1)
    return col <= row


def _lane_tile(x, n):
    return jnp.concatenate([x] * n, axis=1) if n > 1 else x


def _attn_fwd(q_all, kv_all, tq=512, tk=1024):
    S = q_all.shape[0]
    tq, tk = min(tq, S), min(tk, S)
    nq, nk, nb, r = S // tq, S // tk, tk // LANES, tk // tq

    def body(q_ref, k_ref, v_ref, o_ref, lse_ref, m_s, l_s, acc_s):
        qi, ki = pl.program_id(1), pl.program_id(2)
        last = lax.div(qi, r)

        @pl.when(ki == 0)
        def _():
            m_s[...] = jnp.full(m_s.shape, NEG, F32)
            l_s[...] = jnp.zeros(l_s.shape, F32)
            acc_s[...] = jnp.zeros(acc_s.shape, F32)

        @pl.when(ki <= last)
        def _():
            v = v_ref[...]
            mask = _causal_mask(qi, ki, tq, tk)
            for hh in range(2):
                q = q_ref[:, hh * LANES:(hh + 1) * LANES]
                k = k_ref[:, hh * LANES:(hh + 1) * LANES]
                s = lax.dot_general(q, k, (((1,), (1,)), ((), ())), preferred_element_type=F32) * ATT_SCALE
                s = jnp.where(mask, s, NEG)
                m_prev = m_s[hh]
                m_new = jnp.maximum(m_prev, jnp.max(s, axis=-1, keepdims=True))
                p = jnp.exp(s - _lane_tile(m_new, nb))
                alpha = jnp.exp(m_prev - m_new)
                ps = p[:, :LANES]
                for j in range(1, nb):
                    ps = ps + p[:, j * LANES:(j + 1) * LANES]
                l_s[hh] = alpha * l_s[hh] + ps
                acc_s[hh] = alpha * acc_s[hh] + jnp.dot(p.astype(BF16), v, preferred_element_type=F32)
                m_s[hh] = m_new

        @pl.when(ki == last)
        def _():
            first = lax.broadcasted_iota(jnp.int32, (tq, LANES), 1) < MLA_V
            l0 = jnp.sum(l_s[0], axis=-1, keepdims=True)
            l1 = jnp.sum(l_s[1], axis=-1, keepdims=True)
            o_ref[...] = jnp.where(first, acc_s[0] / l0, acc_s[1] / l1)
            lse_ref[:, :LANES] = m_s[0] + jnp.log(l0)
            lse_ref[:, LANES:] = m_s[1] + jnp.log(l1)

    return pl.pallas_call(
        body, name="mla_attn_fwd", grid=(N_PAIR, nq, nk),
        in_specs=[pl.BlockSpec((tq, 2 * LANES), lambda p, qi, ki: (qi, p)),
                  pl.BlockSpec((tk, 2 * LANES), lambda p, qi, ki: (jnp.minimum(ki, lax.div(qi, r)), p)),
                  pl.BlockSpec((tk, LANES), lambda p, qi, ki: (jnp.minimum(ki, lax.div(qi, r)), MLA_HEADS + p))],
        out_specs=[pl.BlockSpec((tq, LANES), lambda p, qi, ki: (qi, p)), pl.BlockSpec((tq, 2 * LANES), lambda p, qi, ki: (qi, p))],
        out_shape=[jax.ShapeDtypeStruct((S, MLA_HEADS * MLA_V), F32), jax.ShapeDtypeStruct((S, MLA_HEADS * LANES), F32)],
        scratch_shapes=[pltpu.VMEM((2, tq, LANES), F32), pltpu.VMEM((2, tq, LANES), F32), pltpu.VMEM((2, tq, LANES), F32)],
        compiler_params=_cparams(("parallel", "parallel", "arbitrary")),
    )(q_all, kv_all, kv_all)


def _attn_delta(dcat, o, tm=512):
    S = o.shape[0]

    def body(do_ref, o_ref, d_ref):
        prod = do_ref[...] * o_ref[...]
        first = lax.broadcasted_iota(jnp.int32, (tm, LANES), 1) < MLA_V
        for p in range(N_PAIR):
            pp = prod[:, p * LANES:(p + 1) * LANES]
            d0 = jnp.sum(jnp.where(first, pp, 0.0), axis=-1, keepdims=True)
            d1 = jnp.sum(jnp.where(first, 0.0, pp), axis=-1, keepdims=True)
            d_ref[:, 2 * p * LANES:(2 * p + 1) * LANES] = jnp.broadcast_to(d0, (tm, LANES))
            d_ref[:, (2 * p + 1) * LANES:(2 * p + 2) * LANES] = jnp.broadcast_to(d1, (tm, LANES))

    W = MLA_HEADS * MLA_V
    return pl.pallas_call(
        body, name="mla_attn_delta", grid=(S // tm,),
        in_specs=[pl.BlockSpec((tm, W), lambda i: (i, 1)), pl.BlockSpec((tm, W), lambda i: (i, 0))],
        out_specs=pl.BlockSpec((tm, MLA_HEADS * LANES), lambda i: (i, 0)),
        out_shape=jax.ShapeDtypeStruct((S, MLA_HEADS * LANES), F32), compiler_params=_cparams(("parallel",)),
    )(dcat, o)


def _attn_bwd(q_all, kv_all, dcat, lse, delta, tq=512, tk=512):
    S = q_all.shape[0]
    tq, tk = min(tq, S), min(tk, S)
    nq, nk, nb = S // tq, S // tk, tk // LANES
    assert tq == tk

    def body(q_ref, k_ref, v_ref, do_ref, lse_ref, dl_ref, dq_ref, dk_ref, dv_ref, dk_s, dv_s):
        ki, qi = pl.program_id(1), pl.program_id(2)

        @pl.when((ki == 0) & (qi == 0))
        def _():
            dq_ref[...] = jnp.zeros(dq_ref.shape, F32)

        @pl.when(qi == 0)
        def _():
            dk_s[...] = jnp.zeros(dk_s.shape, F32)
            dv_s[...] = jnp.zeros(dv_s.shape, F32)

        @pl.when(qi >= ki)
        def _():
            v, do = v_ref[...], do_ref[...]
            first = lax.broadcasted_iota(jnp.int32, (tq, LANES), 1) < MLA_V
            firstk = lax.broadcasted_iota(jnp.int32, (tk, LANES), 1) < MLA_V
            mask = _causal_mask(qi, ki, tq, tk)
            do_b = do.astype(BF16)
            rows = pl.ds(pl.multiple_of(qi * tq, tq), tq)
            for hh in range(2):
                sl = slice(hh * LANES, (hh + 1) * LANES)
                q, k = q_ref[:, sl], k_ref[:, sl]
                s = lax.dot_general(q, k, (((1,), (1,)), ((), ())), preferred_element_type=F32) * ATT_SCALE
                p = jnp.exp(jnp.where(mask, s, NEG) - _lane_tile(lse_ref[:, sl], nb))
                do_h = jnp.where(first if hh == 0 else ~first, do, 0.0).astype(BF16)
                dp = lax.dot_general(do_h, v, (((1,), (1,)), ((), ())), preferred_element_type=F32)
                ds_b = (p * (dp - _lane_tile(dl_ref[:, sl], nb)) * ATT_SCALE).astype(BF16)
                pv = lax.dot_general(p.astype(BF16), do_b, (((0,), (0,)), ((), ())), preferred_element_type=F32)
                dv_s[...] += jnp.where(firstk if hh == 0 else ~firstk, pv, 0.0)
                dk_s[:, sl] += lax.dot_general(ds_b, q, (((0,), (0,)), ((), ())), preferred_element_type=F32)
                dq_ref[rows, sl] += jnp.dot(ds_b, k, preferred_element_type=F32)

        @pl.when(qi == nq - 1)
        def _():
            dk_ref[...] = dk_s[...]
            dv_ref[...] = dv_s[...]

    wide = pl.BlockSpec((tq, 2 * LANES), lambda p, ki, qi: (jnp.maximum(qi, ki), p))
    return pl.pallas_call(
        body, name="mla_attn_bwd", grid=(N_PAIR, nk, nq),
        in_specs=[wide, pl.BlockSpec((tk, 2 * LANES), lambda p, ki, qi: (ki, p)),
                  pl.BlockSpec((tk, LANES), lambda p, ki, qi: (ki, MLA_HEADS + p)),
                  pl.BlockSpec((tq, LANES), lambda p, ki, qi: (jnp.maximum(qi, ki), N_PAIR + p)), wide, wide],
        out_specs=[pl.BlockSpec((S, 2 * LANES), lambda p, ki, qi: (0, p)),
                   pl.BlockSpec((tk, 2 * LANES), lambda p, ki, qi: (ki, p)), pl.BlockSpec((tk, LANES), lambda p, ki, qi: (ki, p))],
        out_shape=[jax.ShapeDtypeStruct((S, MLA_HEADS * LANES), F32), jax.ShapeDtypeStruct((S, MLA_HEADS * LANES), F32),
                   jax.ShapeDtypeStruct((S, MLA_HEADS * MLA_V), F32)],
        scratch_shapes=[pltpu.VMEM((tk, 2 * LANES), F32), pltpu.VMEM((tk, LANES), F32)],
        compiler_params=_cparams(("parallel", "arbitrary", "arbitrary")),
    )(q_all, kv_all, kv_all, dcat, lse, delta)


def _attn_p_ds(q, k, v, do, lse_h, delta_h, half_mask, mask):
    s = lax.dot_general(q, k, (((1,), (1,)), ((), ())), preferred_element_type=F32) * ATT_SCALE
    p = jnp.exp(jnp.where(mask, s, NEG) - lse_h)
    do_h = jnp.where(half_mask, do, 0.0).astype(BF16)
    dp = lax.dot_general(do_h, v, (((1,), (1,)), ((), ())), preferred_element_type=F32)
    ds = p * (dp - delta_h) * ATT_SCALE
    return p, ds


def _attn_deltas(do, o, first):
    prod = do * o
    return (jnp.sum(jnp.where(first, prod, 0.0), axis=-1, keepdims=True),
            jnp.sum(jnp.where(first, 0.0, prod), axis=-1, keepdims=True))


def _attn_bwd_dq(q_all, kv_all, dcat, o, lse, tq=512, tk=512):
    S = q_all.shape[0]
    nq, nk = S // tq, S // tk

    def body(q_ref, k_ref, v_ref, do_ref, o_ref, lse_ref, dq_ref, acc_s):
        qi, ki = pl.program_id(1), pl.program_id(2)

        @pl.when(ki == 0)
        def _():
            acc_s[...] = jnp.zeros(acc_s.shape, F32)

        @pl.when(ki <= qi)
        def _():
            v, do, lse_t = v_ref[...], do_ref[...], lse_ref[...]
            first = lax.broadcasted_iota(jnp.int32, (tq, LANES), 1) < MLA_V
            deltas = _attn_deltas(do, o_ref[...], first)
            mask = _causal_mask(qi, ki, tq, tk)
            for hh in range(2):
                k = k_ref[:, hh * LANES:(hh + 1) * LANES]
                _, ds = _attn_p_ds(q_ref[:, hh * LANES:(hh + 1) * LANES], k, v, do,
                                   lse_t[:, hh * MLA_V:hh * MLA_V + 1], deltas[hh], first if hh == 0 else ~first, mask)
                acc_s[:, hh * LANES:(hh + 1) * LANES] += jnp.dot(ds.astype(BF16), k, preferred_element_type=F32)

        @pl.when(ki == qi)
        def _():
            dq_ref[...] = acc_s[...]

    half = pl.BlockSpec((tq, LANES), lambda p, qi, ki: (qi, p))
    return pl.pallas_call(
        body, name="mla_attn_bwd_dq", grid=(N_PAIR, nq, nk),
        in_specs=[pl.BlockSpec((tq, 2 * LANES), lambda p, qi, ki: (qi, p)),
                  pl.BlockSpec((tk, 2 * LANES), lambda p, qi, ki: (jnp.minimum(ki, qi), p)),
                  pl.BlockSpec((tk, LANES), lambda p, qi, ki: (jnp.minimum(ki, qi), MLA_HEADS + p)),
                  pl.BlockSpec((tq, LANES), lambda p, qi, ki: (qi, N_PAIR + p)), half, half],
        out_specs=pl.BlockSpec((tq, 2 * LANES), lambda p, qi, ki: (qi, p)),
        out_shape=jax.ShapeDtypeStruct((S, MLA_HEADS * LANES), F32),
        scratch_shapes=[pltpu.VMEM((tq, 2 * LANES), F32)],
        compiler_params=_cparams(("parallel", "parallel", "arbitrary")),
    )(q_all, kv_all, kv_all, dcat, o, lse)


def _attn_bwd_dkv(q_all, kv_all, dcat, o, lse, tq=512, tk=512):
    S = q_all.shape[0]
    nq, nk = S // tq, S // tk

    def body(q_ref, k_ref, v_ref, do_ref, o_ref, lse_ref, dk_ref, dv_ref, dk_s, dv_s):
        ki, qi = pl.program_id(1), pl.program_id(2)

        @pl.when(qi == 0)
        def _():
            dk_s[...] = jnp.zeros(dk_s.shape, F32)
            dv_s[...] = jnp.zeros(dv_s.shape, F32)

        @pl.when(qi >= ki)
        def _():
            v, do, lse_t = v_ref[...], do_ref[...], lse_ref[...]
            first = lax.broadcasted_iota(jnp.int32, (tq, LANES), 1) < MLA_V
            firstk = lax.broadcasted_iota(jnp.int32, (tk, LANES), 1) < MLA_V
            deltas = _attn_deltas(do, o_ref[...], first)
            mask = _causal_mask(qi, ki, tq, tk)
            do_b = do.astype(BF16)
            for hh in range(2):
                q = q_ref[:, hh * LANES:(hh + 1) * LANES]
                p, ds = _attn_p_ds(q, k_ref[:, hh * LANES:(hh + 1) * LANES], v, do,
                                   lse_t[:, hh * MLA_V:hh * MLA_V + 1], deltas[hh], first if hh == 0 else ~first, mask)
                pv = lax.dot_general(p.astype(BF16), do_b, (((0,), (0,)), ((), ())), preferred_element_type=F32)
                dv_s[...] += jnp.where(firstk if hh == 0 else ~firstk, pv, 0.0)
                dk_s[:, hh * LANES:(hh + 1) * LANES] += lax.dot_general(ds.astype(BF16), q, (((0,), (0,)), ((), ())),
                                                                       preferred_element_type=F32)

        @pl.when(qi == nq - 1)
        def _():
            dk_ref[...] = dk_s[...]
            dv_ref[...] = dv_s[...]

    half = pl.BlockSpec((tq, LANES), lambda p, ki, qi: (jnp.maximum(qi, ki), p))
    return pl.pallas_call(
        body, name="mla_attn_bwd_dkv", grid=(N_PAIR, nk, nq),
        in_specs=[pl.BlockSpec((tq, 2 * LANES), lambda p, ki, qi: (jnp.maximum(qi, ki), p)),
                  pl.BlockSpec((tk, 2 * LANES), lambda p, ki, qi: (ki, p)),
                  pl.BlockSpec((tk, LANES), lambda p, ki, qi: (ki, MLA_HEADS + p)),
                  pl.BlockSpec((tq, LANES), lambda p, ki, qi: (jnp.maximum(qi, ki), N_PAIR + p)), half, half],
        out_specs=[pl.BlockSpec((tk, 2 * LANES), lambda p, ki, qi: (ki, p)), pl.BlockSpec((tk, LANES), lambda p, ki, qi: (ki, p))],
        out_shape=[jax.ShapeDtypeStruct((S, MLA_HEADS * LANES), F32), jax.ShapeDtypeStruct((S, MLA_HEADS * MLA_V), F32)],
        scratch_shapes=[pltpu.VMEM((tk, 2 * LANES), F32), pltpu.VMEM((tk, LANES), F32)],
        compiler_params=_cparams(("parallel", "parallel", "arbitrary")),
    )(q_all, kv_all, kv_all, dcat, o, lse)


MEM_SCALE = MEM_HEAD_DIM ** -0.5


def _mem_probs(q, k):
    s = lax.dot_general(q, k, (((1,), (1,)), ((), ())), preferred_element_type=F32) * MEM_SCALE
    e = jnp.exp(s - jnp.max(s, axis=-1, keepdims=True))
    return e / jnp.sum(e, axis=-1, keepdims=True)


def _mem_attn_fwd(qm, km, vm, tq=512):
    S, W = qm.shape
    M = km.shape[0]

    def body(q_ref, k_ref, v_ref, o_ref):
        for h in range(MEM_HEADS):
            sl = slice(h * MEM_HEAD_DIM, (h + 1) * MEM_HEAD_DIM)
            p = _mem_probs(q_ref[:, sl], k_ref[:, sl])
            o_ref[:, sl] = jnp.dot(p.astype(BF16), v_ref[:, sl], preferred_element_type=F32).astype(BF16)

    row = pl.BlockSpec((tq, W), lambda i: (i, 0))
    full = pl.BlockSpec((M, W), lambda i: (0, 0))
    return pl.pallas_call(body, name="mem_attn_fwd", grid=(S // tq,), in_specs=[row, full, full], out_specs=row,
                          out_shape=jax.ShapeDtypeStruct((S, W), BF16), compiler_params=_cparams(("parallel",)))(qm, km, vm)


def _mem_attn_bwd(qm, km, vm, dom, tq=512):
    S, W = qm.shape
    M = km.shape[0]

    def body(q_ref, k_ref, v_ref, do_ref, dq_ref, dk_ref, dv_ref):
        i = pl.program_id(0)

        @pl.when(i == 0)
        def _():
            dk_ref[...] = jnp.zeros(dk_ref.shape, F32)
            dv_ref[...] = jnp.zeros(dv_ref.shape, F32)

        for h in range(MEM_HEADS):
            sl = slice(h * MEM_HEAD_DIM, (h + 1) * MEM_HEAD_DIM)
            q, k, v, do = q_ref[:, sl], k_ref[:, sl], v_ref[:, sl], do_ref[:, sl]
            p = _mem_probs(q, k)
            dv_ref[:, sl] += lax.dot_general(p.astype(BF16), do, (((0,), (0,)), ((), ())), preferred_element_type=F32)
            dp = lax.dot_general(do, v, (((1,), (1,)), ((), ())), preferred_element_type=F32)
            ds = (p * (dp - jnp.sum(dp * p, axis=-1, keepdims=True)) * MEM_SCALE).astype(BF16)
            dq_ref[:, sl] = jnp.dot(ds, k, preferred_element_type=F32).astype(BF16)
            dk_ref[:, sl] += lax.dot_general(ds, q, (((0,), (0,)), ((), ())), preferred_element_type=F32)

    row = pl.BlockSpec((tq, W), lambda i: (i, 0))
    full = pl.BlockSpec((M, W), lambda i: (0, 0))
    return pl.pallas_call(
        body, name="mem_attn_bwd", grid=(S // tq,), in_specs=[row, full, full, row], out_specs=[row, full, full],
        out_shape=[jax.ShapeDtypeStruct((S, W), BF16), jax.ShapeDtypeStruct((M, W), F32), jax.ShapeDtypeStruct((M, W), F32)],
        compiler_params=_cparams(("arbitrary",)),
    )(qm, km, vm, dom)


L = SSD_CHUNK
N_SPAIR = SSD_HEADS // 2
GRP_W = SSD_INNER // 2
XB0, XC0 = SSD_INNER, SSD_INNER + 2 * SSD_STATE


def _cumsum_rows(a, reverse=False):
    row = lax.broadcasted_iota(jnp.int32, a.shape, 0)
    x, sft = a, 1
    while sft < L:
        if reverse:
            x = x + jnp.where(row < L - sft, pltpu.roll(x, L - sft, axis=0), 0.0)
        else:
            x = x + jnp.where(row >= sft, pltpu.roll(x, sft, axis=0), 0.0)
        sft *= 2
    return x


def _shift_down(cur, prev, s):
    if s == 0:
        return cur
    row = lax.broadcasted_iota(jnp.int32, cur.shape, 0)
    return jnp.where(row < s, pltpu.roll(prev, s, axis=0), pltpu.roll(cur, s, axis=0))


def _shift_up(cur, nxt, s):
    if s == 0:
        return cur
    row = lax.broadcasted_iota(jnp.int32, cur.shape, 0)
    return jnp.where(row >= L - s, pltpu.roll(nxt, L - s, axis=0), pltpu.roll(cur, L - s, axis=0))


def _ssd_conv(u, prev, cw, cb):
    conv = cb + cw[SSD_CONV - 1:SSD_CONV, :] * u
    for s in range(1, SSD_CONV):
        conv = conv + cw[SSD_CONV - 1 - s:SSD_CONV - s, :] * _shift_down(u, prev, s)
    return conv


def _pair_lanes(v, h0, first):
    return jnp.where(first, v[:, h0:h0 + 1], v[:, h0 + 1:h0 + 2])


def _ssd_common(u, prev, dt_raw, cw, cb, dtb, alog):
    conv = _ssd_conv(u, prev, cw, cb)
    sg = _sigmoid(conv)
    xa = conv * sg
    dpre = dt_raw + dtb
    dtv = jnp.maximum(dpre, 0.0) + jnp.log1p(jnp.exp(-jnp.abs(dpre)))
    a_row = -jnp.exp(alog)
    cs = _cumsum_rows(dtv * a_row)
    return conv, sg, xa, dpre, dtv, a_row, cs


def _ssd_pair_fwd(xa, dtv, cs, csT, G, Cg, Bg, Sp, dsk, pp, first, tri, rowfirst):
    h0 = 2 * pp
    x = xa[:, pp * LANES:(pp + 1) * LANES]
    xdt = x * _pair_lanes(dtv, h0, first)
    xdt_b = xdt.astype(BF16)
    Ms, yd = [], []
    for h in (h0, h0 + 1):
        lam = jnp.exp(jnp.where(tri, cs[:, h:h + 1] - csT[h:h + 1, :], NEG))
        M = G * lam
        Ms.append((M, lam))
        yd.append(jnp.dot(M.astype(BF16), xdt_b, preferred_element_type=F32))
    T = lax.dot_general(Cg, Sp.astype(BF16), (((1,), (1,)), ((), ())), preferred_element_type=F32)
    E = jnp.exp(_pair_lanes(cs, h0, first))
    yoff = E * T
    csl = cs[L - 1:L, :]
    Fd = jnp.exp(_pair_lanes(csl, h0, first) - _pair_lanes(cs, h0, first))
    el = jnp.exp(csl)
    el_rows = jnp.where(rowfirst, el[:, h0:h0 + 1], el[:, h0 + 1:h0 + 2])
    Sloc = lax.dot_general((xdt * Fd).astype(BF16), Bg, (((0,), (0,)), ((), ())), preferred_element_type=F32)
    S_new = el_rows * Sp + Sloc
    y = jnp.where(first, yd[0], yd[1]) + yoff + x * _pair_lanes(dsk, h0, first[:1])
    return y, S_new, (x, xdt, xdt_b, Ms, E, yoff, Fd, el, el_rows)


def _ssd_masks():
    lane = lax.broadcasted_iota(jnp.int32, (L, LANES), 1)
    row = lax.broadcasted_iota(jnp.int32, (L, LANES), 0)
    return lane, row, lane < SSD_HEAD_DIM, row >= lane, row[:, :1] < SSD_HEAD_DIM


def _ssd_specs(nc, rev):
    def cidx(i):
        return nc - 1 - i if rev else i
    z = pl.BlockSpec((L, SSD_INNER), lambda i: (cidx(i), PZ // SSD_INNER))
    u = pl.BlockSpec((L, SSD_XBC), lambda i: (cidx(i), PX // SSD_XBC))
    dt = pl.BlockSpec((L, LANES), lambda i: (cidx(i), PDT // LANES))
    return cidx, z, u, dt


def _vec(w):
    return pl.BlockSpec((1, w), lambda i: (0, 0))


def _ssd_fwd(proj, cw, cb, dtb, alog, dsk, ng):
    S = proj.shape[0]
    nc = S // L

    def body(z_ref, u_ref, dt_ref, cw_ref, cb_ref, dtb_ref, alog_ref, dsk_ref, ng_ref, y_ref, st_ref, prev_s, state_s):
        c = pl.program_id(0)

        @pl.when(c == 0)
        def _():
            prev_s[...] = jnp.zeros(prev_s.shape, F32)
            state_s[...] = jnp.zeros(state_s.shape, F32)

        u = u_ref[...]
        _, _, xa, _, dtv, _, cs = _ssd_common(u, prev_s[...], dt_ref[...], cw_ref[...], cb_ref[...], dtb_ref[...], alog_ref[...])
        prev_s[...] = u
        csT = cs.T
        _, _, first, tri, rowfirst = _ssd_masks()
        dsk_v = dsk_ref[...]
        ys = []
        for g in range(2):
            Bg = xa[:, XB0 + g * SSD_STATE:XB0 + (g + 1) * SSD_STATE].astype(BF16)
            Cg = xa[:, XC0 + g * SSD_STATE:XC0 + (g + 1) * SSD_STATE].astype(BF16)
            G = lax.dot_general(Cg, Bg, (((1,), (1,)), ((), ())), preferred_element_type=F32)
            for pp in (2 * g, 2 * g + 1):
                Sp = state_s[pp]
                st_ref[pp * LANES:(pp + 1) * LANES, :] = Sp
                y, S_new, _ = _ssd_pair_fwd(xa, dtv, cs, csT, G, Cg, Bg, Sp, dsk_v, pp, first, tri, rowfirst)
                state_s[pp] = S_new
                ys.append(y)
        z = z_ref[...]
        for g in range(2):
            yg = jnp.concatenate([ys[2 * g], ys[2 * g + 1]], axis=1)
            zg = z[:, g * GRP_W:(g + 1) * GRP_W]
            gated = yg * (zg * _sigmoid(zg))
            r = lax.rsqrt(jnp.mean(gated * gated, axis=-1, keepdims=True) + RMS_EPS)
            y_ref[:, g * GRP_W:(g + 1) * GRP_W] = gated * r * ng_ref[:, g * GRP_W:(g + 1) * GRP_W]

    _, zs, us, dts = _ssd_specs(nc, False)
    return pl.pallas_call(
        body, name="ssd_fwd", grid=(nc,),
        in_specs=[zs, us, dts, pl.BlockSpec((8, SSD_XBC), lambda i: (0, 0)), _vec(SSD_XBC), _vec(LANES), _vec(LANES), _vec(LANES),
                  _vec(SSD_INNER)],
        out_specs=[pl.BlockSpec((L, SSD_INNER), lambda i: (i, 0)), pl.BlockSpec((N_SPAIR * LANES, SSD_STATE), lambda i: (i, 0))],
        out_shape=[jax.ShapeDtypeStruct((S, SSD_INNER), F32), jax.ShapeDtypeStruct((nc * N_SPAIR * LANES, SSD_STATE), F32)],
        scratch_shapes=[pltpu.VMEM((L, SSD_XBC), F32), pltpu.VMEM((N_SPAIR, LANES, SSD_STATE), F32)],
        compiler_params=_cparams(("arbitrary",)),
    )(proj, proj, proj, cw, cb, dtb, alog, dsk, ng)


def _ssd_bwd(proj, states, dy, cw, cb, dtb, alog, dsk, ng):
    S = proj.shape[0]
    nc = S // L

    def body(z_ref, u_ref, up_ref, dt_ref, st_ref, dy_ref, cw_ref, cb_ref, dtb_ref, alog_ref, dsk_ref, ng_ref,
             dz_ref, du_ref, ddt_ref, dcw_ref, dcb_ref, ddtb_ref, dalog_ref, ddsk_ref, dng_ref,
             dS_s, dconv_s, dD_s):
        i = pl.program_id(0)
        c = nc - 1 - i

        @pl.when(i == 0)
        def _():
            dS_s[...] = jnp.zeros(dS_s.shape, F32)
            dconv_s[...] = jnp.zeros(dconv_s.shape, F32)
            dD_s[...] = jnp.zeros(dD_s.shape, F32)
            for r in (dcw_ref, dcb_ref, ddtb_ref, dalog_ref, ddsk_ref, dng_ref):
                r[...] = jnp.zeros(r.shape, F32)

        u = u_ref[...]
        prev = jnp.where(c > 0, up_ref[...], 0.0)
        cw_v = cw_ref[...]
        conv, sg, xa, dpre, dtv, a_row, cs = _ssd_common(u, prev, dt_ref[...], cw_v, cb_ref[...], dtb_ref[...], alog_ref[...])
        csT = cs.T
        lane, row, first, tri, rowfirst = _ssd_masks()
        dsk_v = dsk_ref[...]

        fw = []
        Gs, Bs, Cs = [], [], []
        for g in range(2):
            Bg = xa[:, XB0 + g * SSD_STATE:XB0 + (g + 1) * SSD_STATE].astype(BF16)
            Cg = xa[:, XC0 + g * SSD_STATE:XC0 + (g + 1) * SSD_STATE].astype(BF16)
            G = lax.dot_general(Cg, Bg, (((1,), (1,)), ((), ())), preferred_element_type=F32)
            Gs.append(G), Bs.append(Bg), Cs.append(Cg)
            for pp in (2 * g, 2 * g + 1):
                Sp = st_ref[pp * LANES:(pp + 1) * LANES, :]
                y, _, keep = _ssd_pair_fwd(xa, dtv, cs, csT, G, Cg, Bg, Sp, dsk_v, pp, first, tri, rowfirst)
                fw.append((y, Sp, keep))

        z = z_ref[...]
        dys = []
        for g in range(2):
            sl = slice(g * GRP_W, (g + 1) * GRP_W)
            yg = jnp.concatenate([fw[2 * g][0], fw[2 * g + 1][0]], axis=1)
            zg = z[:, sl]
            sz = _sigmoid(zg)
            silu_z = zg * sz
            gated = yg * silu_z
            r = lax.rsqrt(jnp.mean(gated * gated, axis=-1, keepdims=True) + RMS_EPS)
            nh = gated * r
            dout = dy_ref[:, sl]
            dng_ref[:, sl] += jnp.sum(dout * nh, axis=0, keepdims=True)
            dnh = dout * ng_ref[:, sl]
            dgated = r * (dnh - nh * jnp.mean(dnh * nh, axis=-1, keepdims=True))
            dz_ref[:, sl] = dgated * yg * (sz * (1.0 + zg * (1.0 - sz)))
            dyg = dgated * silu_z
            dys.append(dyg[:, :LANES]), dys.append(dyg[:, LANES:])

        dcs_c = jnp.zeros((L, LANES), F32)
        dcs_r = jnp.zeros((L, LANES), F32)
        ddt_c = jnp.zeros((L, LANES), F32)
        dxs = []
        dB, dC = [None, None], [None, None]
        last = row == L - 1
        for g in range(2):
            Bg, Cg, G = Bs[g], Cs[g], Gs[g]
            dG = jnp.zeros((L, L), F32)
            dBg = jnp.zeros((L, SSD_STATE), F32)
            dCg = jnp.zeros((L, SSD_STATE), F32)
            for pp in (2 * g, 2 * g + 1):
                h0 = 2 * pp
                y, Sp, (x, xdt, xdt_b, Ms, E, yoff, Fd, el, el_rows) = fw[pp]
                dY = dys[pp]
                dS = dS_s[pp]
                dS_b, Sp_b = dS.astype(BF16), Sp.astype(BF16)
                dD_s[:, pp * LANES:(pp + 1) * LANES] += jnp.sum(dY * x, axis=0, keepdims=True)
                dx = dY * _pair_lanes(dsk_v, h0, first[:1])
                dxdt = jnp.zeros((L, LANES), F32)
                dY_b = dY.astype(BF16)
                for hh, h in enumerate((h0, h0 + 1)):
                    hm = first if hh == 0 else ~first
                    M, lam = Ms[hh]
                    dYh = jnp.where(hm, dY, 0.0).astype(BF16)
                    dM = lax.dot_general(dYh, xdt_b, (((1,), (1,)), ((), ())), preferred_element_type=F32)
                    W = dM * M
                    dcs_c = dcs_c + jnp.where(lane == h, jnp.sum(W, axis=-1, keepdims=True), 0.0)
                    dcs_r = dcs_r + jnp.where(row == h, jnp.sum(W, axis=0, keepdims=True), 0.0)
                    dG = dG + dM * lam
                    mt = lax.dot_general(M.astype(BF16), dY_b, (((0,), (0,)), ((), ())), preferred_element_type=F32)
                    dxdt = dxdt + jnp.where(hm, mt, 0.0)
                dT = (E * dY).astype(BF16)
                dCg = dCg + jnp.dot(dT, Sp_b, preferred_element_type=F32)
                dS_in = lax.dot_general(dT, Cg, (((0,), (0,)), ((), ())), preferred_element_type=F32) + el_rows * dS
                q1 = dY * yoff
                dZ = lax.dot_general(Bg, dS_b, (((1,), (1,)), ((), ())), preferred_element_type=F32)
                dBg = dBg + jnp.dot((xdt * Fd).astype(BF16), dS_b, preferred_element_type=F32)
                dxdt = dxdt + dZ * Fd
                q2 = dZ * xdt * Fd
                dSS = dS * Sp
                for hh, h in enumerate((h0, h0 + 1)):
                    hm = first if hh == 0 else ~first
                    rs1 = jnp.sum(jnp.where(hm, q1, 0.0), axis=-1, keepdims=True)
                    rs2 = jnp.sum(jnp.where(hm, q2, 0.0), axis=-1, keepdims=True)
                    rmask = rowfirst if hh == 0 else ~rowfirst
                    d_el = jnp.sum(jnp.sum(jnp.where(rmask, dSS, 0.0), axis=-1, keepdims=True), axis=0, keepdims=True)
                    tail = jnp.sum(rs2, axis=0, keepdims=True) + d_el * el[:, h:h + 1]
                    dcs_c = dcs_c + jnp.where(lane == h, rs1 - rs2 + jnp.where(last[:, :1], tail, 0.0), 0.0)
                    ddt_c = ddt_c + jnp.where(lane == h, jnp.sum(jnp.where(hm, dxdt * x, 0.0), axis=-1, keepdims=True), 0.0)
                dS_s[pp] = dS_in
                dxs.append(dx + dxdt * _pair_lanes(dtv, h0, first))
            dG_b = dG.astype(BF16)
            dC[g] = dCg + jnp.dot(dG_b, Bg, preferred_element_type=F32)
            dB[g] = dBg + lax.dot_general(dG_b, Cg, (((0,), (0,)), ((), ())), preferred_element_type=F32)

        dcs = dcs_c - dcs_r.T
        da = _cumsum_rows(dcs, reverse=True)
        ddt_c = ddt_c + da * a_row
        dalog_ref[...] += jnp.sum(da * dtv, axis=0, keepdims=True) * a_row
        ddt_raw = ddt_c * _sigmoid(dpre)
        ddt_ref[...] = ddt_raw
        ddtb_ref[...] += jnp.sum(ddt_raw, axis=0, keepdims=True)

        dxa = jnp.concatenate(dxs + dB + dC, axis=1)
        dconv = dxa * (sg * (1.0 + conv * (1.0 - sg)))
        dcb_ref[...] += jnp.sum(dconv, axis=0, keepdims=True)
        nxt = dconv_s[...]
        du = cw_v[SSD_CONV - 1:SSD_CONV, :] * dconv
        dcw_ref[SSD_CONV - 1:SSD_CONV, :] += jnp.sum(dconv * u, axis=0, keepdims=True)
        for s in range(1, SSD_CONV):
            k = SSD_CONV - 1 - s
            du = du + cw_v[k:k + 1, :] * _shift_up(dconv, nxt, s)
            dcw_ref[k:k + 1, :] += jnp.sum(dconv * _shift_down(u, prev, s), axis=0, keepdims=True)
        du_ref[...] = du
        dconv_s[...] = dconv

        @pl.when(i == nc - 1)
        def _():
            acc = dD_s[...]
            lane1 = lax.broadcasted_iota(jnp.int32, (1, LANES), 1)
            lanew = lax.broadcasted_iota(jnp.int32, acc.shape, 1)
            out = jnp.zeros((1, LANES), F32)
            for h in range(SSD_HEADS):
                tot = jnp.sum(jnp.where((lanew >= h * SSD_HEAD_DIM) & (lanew < (h + 1) * SSD_HEAD_DIM), acc, 0.0),
                              axis=-1, keepdims=True)
                out = out + jnp.where(lane1 == h, tot, 0.0)
            ddsk_ref[...] = out

    cidx, zs, us, dts = _ssd_specs(nc, True)
    ups = pl.BlockSpec((L, SSD_XBC), lambda i: (jnp.maximum(cidx(i) - 1, 0), PX // SSD_XBC))
    rowc = lambda w: pl.BlockSpec((L, w), lambda i: (cidx(i), 0))
    return pl.pallas_call(
        body, name="ssd_bwd", grid=(nc,),
        in_specs=[zs, us, ups, dts, pl.BlockSpec((N_SPAIR * LANES, SSD_STATE), lambda i: (cidx(i), 0)), rowc(SSD_INNER),
                  pl.BlockSpec((8, SSD_XBC), lambda i: (0, 0)), _vec(SSD_XBC), _vec(LANES), _vec(LANES), _vec(LANES), _vec(SSD_INNER)],
        out_specs=[rowc(SSD_INNER), rowc(SSD_XBC), rowc(LANES), pl.BlockSpec((8, SSD_XBC), lambda i: (0, 0)), _vec(SSD_XBC),
                   _vec(LANES), _vec(LANES), _vec(LANES), _vec(SSD_INNER)],
        out_shape=[jax.ShapeDtypeStruct((S, SSD_INNER), F32), jax.ShapeDtypeStruct((S, SSD_XBC), F32),
                   jax.ShapeDtypeStruct((S, LANES), F32), jax.ShapeDtypeStruct((8, SSD_XBC), F32),
                   jax.ShapeDtypeStruct((1, SSD_XBC), F32), jax.ShapeDtypeStruct((1, LANES), F32),
                   jax.ShapeDtypeStruct((1, LANES), F32), jax.ShapeDtypeStruct((1, LANES), F32),
                   jax.ShapeDtypeStruct((1, SSD_INNER), F32)],
        scratch_shapes=[pltpu.VMEM((N_SPAIR, LANES, SSD_STATE), F32), pltpu.VMEM((L, SSD_XBC), F32),
                        pltpu.VMEM((1, SSD_INNER), F32)],
        compiler_params=_cparams(("arbitrary",)),
    )(proj, proj, proj, proj, states, dy, cw, cb, dtb, alog, dsk, ng)


_IN_SEGS = ((PZ, 0, 512), (PX, 512, 1024), (PDT, 1536, 8), (PQ, 1544, 384), (PKV, 1928, 256), (PKR + KR_LANE, 2184, 32))


def _pad_w_in(w):
    parts, at = [], 0
    for dst, src, n in sorted(_IN_SEGS):
        parts += [jnp.zeros((w.shape[0], dst - at), w.dtype), w[:, src:src + n]]
        at = dst + n
    return jnp.concatenate(parts + [jnp.zeros((w.shape[0], PW - at), w.dtype)], axis=1)


def _unpad_w_in(wp):
    segs = sorted(_IN_SEGS, key=lambda t: t[1])
    return jnp.concatenate([wp[:, dst:dst + n] for dst, src, n in segs], axis=1)


def _pad_w_q(w):
    return jnp.pad(w.reshape(MLA_Q_RANK, MLA_HEADS, MLA_QK), ((0, 0), (0, 0), (0, LANES - MLA_QK))).reshape(MLA_Q_RANK, MLA_HEADS * LANES)


def _unpad_w_q(wp):
    return wp.reshape(MLA_Q_RANK, MLA_HEADS, LANES)[:, :, :MLA_QK].reshape(MLA_Q_RANK, MLA_HEADS * MLA_QK)


def _pad_w_kv(w):
    w3 = w.reshape(MLA_KV_RANK, MLA_HEADS, MLA_NOPE + MLA_V)
    k = jnp.pad(w3[:, :, :MLA_NOPE], ((0, 0), (0, 0), (0, LANES - MLA_NOPE))).reshape(MLA_KV_RANK, MLA_HEADS * LANES)
    return jnp.concatenate([k, w3[:, :, MLA_NOPE:].reshape(MLA_KV_RANK, MLA_HEADS * MLA_V)], axis=1)


def _unpad_w_kv(wp):
    k = wp[:, :MLA_HEADS * LANES].reshape(MLA_KV_RANK, MLA_HEADS, LANES)[:, :, :MLA_NOPE]
    v = wp[:, MLA_HEADS * LANES:].reshape(MLA_KV_RANK, MLA_HEADS, MLA_V)
    return jnp.concatenate([k, v], axis=2).reshape(MLA_KV_RANK, MLA_HEADS * (MLA_NOPE + MLA_V))


def _head_lanes(v):
    return jnp.pad(v, ((0, 0), (0, LANES - v.shape[1])))


def _local_step(x, mem, positions, tgt, P, late_weights=None, emit=None):
    tabs = _rope_tables(positions)
    G = {}
    emit = emit or (lambda names, grads: 0.0)

    h0, h0b = _ln_fwd([(x, 1.0)], P["ln_in_g"], P["ln_in_b"], "ln_in")
    proj = _mm(h0b, P["w_in"], "nn", "proj_in", tm=1024, tn=640)
    y_ssd, states = _ssd_fwd(proj, P["conv_w"], P["conv_b"], P["dt_bias"], P["a_log"], P["d_skip"], P["ssd_norm_g"])
    qn, kvn, kpe = _mla_prep(proj, P["q_norm_g"], P["kv_norm_g"], tabs)

    def q_epi(acc, c, s1, s2):
        return (jnp.concatenate([_rope_block(acc[:, h * LANES:(h + 1) * LANES], c, s1, s2) for h in range(MLA_HEADS)], axis=1),)

    q_all = _mm(qn, P["w_q_up"], "nn", "q_up", out_dtypes=(BF16,), epi=q_epi, extras=[(t, "m") for t in tabs])

    def kv_epi(acc, kp):
        kb = [acc[:, h * LANES:(h + 1) * LANES] + kp for h in range(MLA_HEADS)]
        return (jnp.concatenate(kb + [acc[:, MLA_HEADS * LANES:]], axis=1),)

    kv_all = _mm(kvn, P["w_kv_up"], "nn", "kv_up", out_dtypes=(BF16,), epi=kv_epi, extras=[(kpe, "m")])
    o_att, lse = _attn_fwd(q_all, kv_all)
    cat = jnp.concatenate([y_ssd, o_att], axis=1).astype(BF16)
    mix = _mm(cat, P["w_mix_out"], "nn", "mix_out", tm=1024)
    h1, h1b = _ln_fwd([(h0, ALPHA), (mix, 1.0)], P["ln1_g"], P["ln1_b"], "ln1")
    if late_weights is not None:
        P = {**P, **late_weights(h1b)}
    qm = _mm(h1b, P["w_mem_q"], "nn", "mem_q", tm=1024, out_dtypes=(BF16,))
    km = _mm(mem, P["w_mem_k"], "nn", "mem_k", out_dtypes=(BF16,))
    vm = _mm(mem, P["w_mem_v"], "nn", "mem_v", out_dtypes=(BF16,))
    om = _mem_attn_fwd(qm, km, vm)
    xa = _mm(om, P["w_mem_o"], "nn", "mem_o", tm=1024)
    h2, h2b = _ln_fwd([(h1, ALPHA), (xa, 1.0)], P["ln2_g"], P["ln2_b"], "ln2")

    def up_epi(acc):
        r = jnp.maximum(acc, 0.0)
        return acc, r * r

    u, act = _mm(h2b, P["w_up"], "nn", "mlp_up", tm=1024, tn=1024, out_dtypes=(F32, BF16), epi=up_epi)
    ff = _mm(act, P["w_down"], "nn", "mlp_down", tm=1024, tk=1024)

    loss, dr3, dr3b, G["ln3_g"], G["ln3_b"] = _ln_loss_bwd([(h2, ALPHA), (ff, 1.0)], P["ln3_g"], P["ln3_b"], tgt, "ln3_loss")

    def dact_epi(acc, uu):
        return (acc * (2.0 * jnp.maximum(uu, 0.0)),)

    du = _mm(dr3b, P["w_down"], "nt", "mlp_down_dx", tm=1024, tn=1024, out_dtypes=(BF16,), epi=dact_epi, extras=[(u, "mn")])
    G["w_down"] = _mm(act, dr3b, "tn", "mlp_down_dw", tm=1024, tk=1024, out_dtypes=(BF16,))
    G["w_up"] = _mm(h2b, du, "tn", "mlp_up_dw", tm=1024, tn=D_FF // N_DEV, tk=1024, out_dtypes=(BF16,), col_slots=True)
    tie = emit(("w_down", "w_up"), G)
    dh2_ff = _mm(du, P["w_up"], "nt", "mlp_up_dx", tm=1024, tk=1024)
    dr2, dr2b, G["ln2_g"], G["ln2_b"] = _ln_bwd([(h1, ALPHA), (xa, 1.0)], [(dr3, ALPHA), (dh2_ff, 1.0)], P["ln2_g"] + tie, "ln2_bwd")

    dom = _mm(dr2b, P["w_mem_o"], "nt", "mem_o_dx", tm=1024, out_dtypes=(BF16,))
    G["w_mem_o"] = _mm(om, dr2b, "tn", "mem_o_dw", tm=1024, tk=1024, out_dtypes=(BF16,))
    dqm, dkm, dvm = _mem_attn_bwd(qm, km, vm, dom)
    G["w_mem_q"] = _mm(h1b, dqm, "tn", "mem_q_dw", tm=1024, tk=1024, out_dtypes=(BF16,))
    G["w_mem_k"] = _mm(mem, dkm, "tn", "mem_k_dw", tm=1024, out_dtypes=(BF16,))
    G["w_mem_v"] = _mm(mem, dvm, "tn", "mem_v_dw", tm=1024, out_dtypes=(BF16,))
    tie = emit(("w_mem_o", "w_mem_q", "w_mem_k", "w_mem_v"), G)
    dh1_q = _mm(dqm, P["w_mem_q"], "nt", "mem_q_dx", tm=1024)
    dr1, dr1b, G["ln1_g"], G["ln1_b"] = _ln_bwd([(h0, ALPHA), (mix, 1.0)], [(dr2, ALPHA), (dh1_q, 1.0)], P["ln1_g"] + tie, "ln1_bwd")

    dcat = _mm(dr1b, P["w_mix_out"], "nt", "mix_out_dx", tm=1024)
    G["w_mix_out"] = _mm(cat, dr1b, "tn", "mix_out_dw", tm=1024, tk=1024, out_dtypes=(BF16,))
    dq_all, dk_all, dv_all = _attn_bwd(q_all, kv_all, dcat, lse, _attn_delta(dcat, o_att))
    dq_pre = _rope_bwd_all(dq_all, tabs)
    G["w_q_up"] = _mm(qn, dq_pre, "tn", "q_up_dw", tk=1024, out_dtypes=(BF16,))
    dqn = _mm(dq_pre, P["w_q_up"], "nt", "q_up_dx", tm=1024)
    dkv_all = jnp.concatenate([dk_all, dv_all], axis=1).astype(BF16)
    G["w_kv_up"] = _mm(kvn, dkv_all, "tn", "kv_up_dw", tk=1024, out_dtypes=(BF16,))
    dkvn = _mm(dkv_all, P["w_kv_up"], "nt", "kv_up_dx", tm=1024)
    dql, dkvl, dkr, G["q_norm_g"], G["kv_norm_g"] = _mla_prep_bwd(proj, P["q_norm_g"], P["kv_norm_g"], tabs, dqn, dkvn, dk_all)
    (dz, dxbc, ddt, G["conv_w"], G["conv_b"], G["dt_bias"], G["a_log"], G["d_skip"], G["ssd_norm_g"]) = _ssd_bwd(
        proj, states, dcat, P["conv_w"], P["conv_b"], P["dt_bias"], P["a_log"], P["d_skip"], P["ssd_norm_g"])
    tie = emit(("w_mix_out", "w_q_up", "w_kv_up", "conv_w"), G)
    S = x.shape[0]
    dproj = jnp.concatenate([dql, jnp.zeros((S, PZ - MLA_Q_RANK), F32) + tie, dz, dxbc, dkvl, ddt, dkr], axis=1).astype(BF16)
    G["w_in"] = _mm(h0b, dproj, "tn", "proj_in_dw", tm=1024, tn=640, tk=1024, out_dtypes=(BF16,))
    tie = emit(("w_in",), G)
    dh0_p = _mm(dproj, P["w_in"], "nt", "proj_in_dx", tm=1024, tk=640)
    gx, _, G["ln_in_g"], G["ln_in_b"] = _ln_bwd([(x, 1.0)], [(dr1, ALPHA), (dh0_p, 1.0)], P["ln_in_g"] + tie, "ln_in_bwd")
    return loss, gx, G


PACK_W = 1024
BIG = (("w_in", (1024, 277), 1), ("conv_w", (4, 128), 1), ("w_q_up", (384, 96), 1), ("w_kv_up", (256, 128), 1),
       ("w_mix_out", (128, 1024), 0), ("w_mem_q", (128, 1024), 0), ("w_mem_k", (128, 1024), 0), ("w_mem_v", (128, 1024), 0),
       ("w_mem_o", (128, 1024), 0), ("w_up", (1024, 512), 1), ("w_down", (512, 1024), 0))
SMALL = ("ln_in_g", "ln_in_b", "conv_b", "ln1_g", "ln1_b", "ln2_g", "ln2_b", "ln3_g", "ln3_b",
         "ssd_norm_g", "q_norm_g", "kv_norm_g", "dt_bias", "a_log", "d_skip")
ALL_W = ("ln_in_g", "ln_in_b", "w_in", "conv_w", "conv_b", "dt_bias", "a_log", "d_skip", "ssd_norm_g", "q_norm_g", "w_q_up",
         "kv_norm_g", "w_kv_up", "w_mix_out", "ln1_g", "ln1_b", "w_mem_q", "w_mem_k", "w_mem_v", "w_mem_o", "ln2_g", "ln2_b",
         "w_up", "w_down", "ln3_g", "ln3_b")


def _rows_of(shape):
    return -(-(shape[0] * shape[1]) // PACK_W)


BIG_ROWS = sum(_rows_of(s) for _, s, _ in BIG)
BIG_R = -(-BIG_ROWS // 32) * 32
SMALL_R = 16
LOSS_ROW = 15
ADAM_TILE = 96
assert BIG_R % ADAM_TILE == 0


def _flat_rows(a, rows):
    lead = a.shape[:-2]
    f = a.reshape(lead + (-1,))
    f = jnp.pad(f, [(0, 0)] * len(lead) + [(0, rows * PACK_W - f.shape[-1])])
    return f.reshape(lead + (rows, PACK_W))


def _pack_big(arrs, dtype):
    parts = []
    for n, s, _ in BIG:
        if n == "conv_w" and dtype == BF16:
            parts.append(lax.bitcast_convert_type(arrs[n].astype(F32), BF16).reshape(1, PACK_W))
        else:
            parts.append(_flat_rows(arrs[n].astype(dtype), _rows_of(s)))
    parts.append(jnp.zeros((BIG_R - BIG_ROWS, PACK_W), dtype))
    return jnp.concatenate(parts, axis=0)


def _unpack_big(buf):
    out, r = {}, 0
    for n, s, _ in BIG:
        k = _rows_of(s)
        out[n] = buf[r:r + k].reshape(-1)[:s[0] * s[1]].reshape(s)
        r += k
    return out


def _unpack_gathered(buf):
    out, r = {}, 0
    for n, s, ax in BIG:
        k = _rows_of(s)
        if n == "conv_w":
            sh = lax.bitcast_convert_type(buf[:, r:r + k].reshape((N_DEV,) + s + (2,)), F32)
        else:
            sh = buf[:, r:r + k].reshape(N_DEV, -1)[:, :s[0] * s[1]].reshape((N_DEV,) + s)
        out[n] = sh.reshape(N_DEV * s[0], s[1]) if ax == 0 else sh.transpose(1, 0, 2).reshape(s[0], N_DEV * s[1])
        r += k
    return out


def _pack_grads(G):
    parts = []
    for n, s, ax in BIG:
        g = G[n]
        sh = g.reshape((N_DEV,) + s) if ax == 0 else g.reshape(s[0], N_DEV, s[1]).transpose(1, 0, 2)
        parts.append(_flat_rows(sh, _rows_of(s)))
    parts.append(jnp.zeros((N_DEV, BIG_R - BIG_ROWS, PACK_W), F32))
    return jnp.concatenate(parts, axis=1)


_SMALL_ROWS = (("ln_in_g",), ("ln_in_b",), ("conv_b",), ("ln1_g",), ("ln1_b",), ("ln2_g",), ("ln2_b",), ("ln3_g",), ("ln3_b",),
               ("ssd_norm_g", "q_norm_g"), ("kv_norm_g", "dt_bias", "a_log", "d_skip"))
_SMALL_W = {"ssd_norm_g": 512, "q_norm_g": 384, "kv_norm_g": 256, "dt_bias": LANES, "a_log": LANES, "d_skip": LANES}


def _pack_small(V, extra_row=None):
    rows = []
    for names in _SMALL_ROWS:
        r = jnp.concatenate([V[n] for n in names], axis=1)
        rows.append(jnp.pad(r, ((0, 0), (0, PACK_W - r.shape[1]))))
    rows.append(jnp.zeros((SMALL_R - len(rows) - 1, PACK_W), F32))
    rows.append(jnp.zeros((1, PACK_W), F32) if extra_row is None else extra_row)
    return jnp.concatenate(rows, axis=0)


def _unpack_small(buf):
    out = {}
    for i, names in enumerate(_SMALL_ROWS):
        c = 0
        for n in names:
            w = _SMALL_W.get(n, PACK_W)
            out[n] = buf[i:i + 1, c:c + w]
            c += w
    return out


MESH = pl.DeviceIdType.MESH
ANY = pl.BlockSpec(memory_space=pl.ANY)
VM = pl.BlockSpec(memory_space=pltpu.VMEM)


def _coords():
    return lax.axis_index("x"), lax.axis_index("y"), lax.axis_index("c")


def _slot(px, py, pc):
    return 4 * px + 2 * py + pc


def _gather_big(shard):
    R, W = shard.shape

    def body(x_ref, out_ref, send_sems, recv_sems, local_sem):
        x, y, c = _coords()
        me, sibling = (x, y, c), (x, y, 1 - c)
        chips = [(1 - x, y), (x, 1 - y), (1 - x, 1 - y)]

        def rows(px, py, pc):
            return out_ref.at[_slot(px, py, pc)]

        def copy(k, block, to, src=None):
            return pltpu.make_async_remote_copy(
                src_ref=rows(*block) if src is None else src, dst_ref=rows(*block),
                send_sem=send_sems.at[k], recv_sem=recv_sems.at[k], device_id=to, device_id_type=MESH)

        mine = pltpu.make_async_copy(x_ref, rows(*me), local_sem)
        mine.start()
        first = [copy(0, me, sibling, src=x_ref)]
        first += [copy(1 + j, me, (*chip, c), src=x_ref) for j, chip in enumerate(chips)]
        for cp in first:
            cp.start()
        passed = [copy(4 + j, (*chip, c), sibling) for j, chip in enumerate(chips)]
        for j, chip in enumerate(chips):
            copy(1 + j, (*chip, c), me).wait_recv()
            passed[j].start()
        copy(0, sibling, me).wait_recv()
        for j, chip in enumerate(chips):
            copy(4 + j, (*chip, 1 - c), me).wait_recv()
        for cp in first + passed:
            cp.wait_send()
        mine.wait()

    return pl.pallas_call(
        body, name="gather_weights", out_shape=jax.ShapeDtypeStruct((N_DEV, R, W), shard.dtype),
        in_specs=[ANY], out_specs=ANY,
        scratch_shapes=[pltpu.SemaphoreType.DMA((7,)), pltpu.SemaphoreType.DMA((7,)), pltpu.SemaphoreType.DMA],
    )(shard)


def _peer(k, x, y, c):
    dx, dy, dc = (k >> 2) & 1, (k >> 1) & 1, k & 1
    return (1 - x if dx else x, 1 - y if dy else y, 1 - c if dc else c)


def _scatter_grads(gpack):
    _, R, W = gpack.shape

    def body(g_ref, out_ref, send_sems, recv_sems, local_sem):
        x, y, c = _coords()
        my = _slot(x, y, c)
        mine = pltpu.make_async_copy(g_ref.at[my], out_ref.at[my], local_sem)
        mine.start()
        cps = []
        for k in range(1, N_DEV):
            peer = _peer(k, x, y, c)
            cps.append(pltpu.make_async_remote_copy(
                src_ref=g_ref.at[_slot(*peer)], dst_ref=out_ref.at[my],
                send_sem=send_sems.at[k - 1], recv_sem=recv_sems.at[k - 1], device_id=peer, device_id_type=MESH))
        for cp in cps:
            cp.start()
        for cp in cps:
            cp.wait_recv()
        for cp in cps:
            cp.wait_send()
        mine.wait()

    return pl.pallas_call(
        body, name="scatter_grads", out_shape=jax.ShapeDtypeStruct(gpack.shape, gpack.dtype),
        in_specs=[ANY], out_specs=ANY,
        scratch_shapes=[pltpu.SemaphoreType.DMA((7,)), pltpu.SemaphoreType.DMA((7,)), pltpu.SemaphoreType.DMA],
    )(gpack)


def _adam(w, g, m, v):
    m = ADAM_B1 * m + (1.0 - ADAM_B1) * g
    v = ADAM_B2 * v + (1.0 - ADAM_B2) * (g * g)
    m_hat = m / (1.0 - ADAM_B1 ** ADAM_STEP)
    v_hat = v / (1.0 - ADAM_B2 ** ADAM_STEP)
    delta = -ADAM_LR * (m_hat / (jnp.sqrt(v_hat) + ADAM_EPS) + ADAM_WD * w)
    return delta, m, v


def _sum_slots(ref):
    tot = ref[0].astype(F32)
    for q in range(1, N_DEV):
        tot = tot + ref[q].astype(F32)
    return tot


def _reduce_adam_big(recv, w, m, v):
    _, R, W = recv.shape

    def body(r_ref, w_ref, m_ref, v_ref, g_ref, d_ref, nm_ref, nv_ref):
        g = _sum_slots(r_ref)
        g_ref[...] = g
        d_ref[...], nm_ref[...], nv_ref[...] = _adam(w_ref[...], g, m_ref[...], v_ref[...])

    row = pl.BlockSpec((ADAM_TILE, W), lambda i: (i, 0))
    return pl.pallas_call(
        body, name="reduce_adam", grid=(R // ADAM_TILE,),
        in_specs=[pl.BlockSpec((N_DEV, ADAM_TILE, W), lambda i: (0, i, 0)), row, row, row], out_specs=[row] * 4,
        out_shape=[jax.ShapeDtypeStruct((R, W), F32)] * 4, compiler_params=_cparams(("parallel",)),
    )(recv, w, m, v)


def _allreduce_adam_small(gs, w, m, v):
    R, W = gs.shape

    def body(g_ref, w_ref, m_ref, v_ref, go_ref, d_ref, nm_ref, nv_ref, land, send_sems, recv_sems):
        x, y, c = _coords()
        my = _slot(x, y, c)
        cps = []
        for k in range(1, N_DEV):
            peer = _peer(k, x, y, c)
            cps.append(pltpu.make_async_remote_copy(
                src_ref=g_ref, dst_ref=land.at[my], send_sem=send_sems.at[k - 1], recv_sem=recv_sems.at[k - 1],
                device_id=peer, device_id_type=MESH))
        for cp in cps:
            cp.start()
        land[my] = g_ref[...]
        for cp in cps:
            cp.wait_recv()
        for cp in cps:
            cp.wait_send()
        g = _sum_slots(land)
        go_ref[...] = g
        d_ref[...], nm_ref[...], nv_ref[...] = _adam(w_ref[...], g, m_ref[...], v_ref[...])

    return pl.pallas_call(
        body, name="allreduce_small", in_specs=[VM] * 4, out_specs=[VM] * 4,
        out_shape=[jax.ShapeDtypeStruct((R, W), F32)] * 4,
        scratch_shapes=[pltpu.VMEM((N_DEV, R, W), F32), pltpu.SemaphoreType.DMA((7,)), pltpu.SemaphoreType.DMA((7,))],
    )(gs, w, m, v)


def _row(v):
    return v.reshape(1, -1).astype(F32)


def _step(x, mem, positions, tgt, W, M, V):
    shard = {n: W[n][0] for n, _, _ in BIG}
    full = _unpack_gathered(_gather_big(_pack_big(shard, BF16)))
    P = {n: _row(W[n]) for n in SMALL}
    for n in ("dt_bias", "a_log", "d_skip"):
        P[n] = _head_lanes(P[n])
    P.update(w_in=_pad_w_in(full["w_in"]), w_q_up=_pad_w_q(full["w_q_up"]), w_kv_up=_pad_w_kv(full["w_kv_up"]),
             conv_w=jnp.pad(full["conv_w"].astype(F32), ((0, 8 - SSD_CONV), (0, 0))))
    for n in ("w_mix_out", "w_mem_q", "w_mem_k", "w_mem_v", "w_mem_o", "w_up", "w_down"):
        P[n] = full[n]

    loss, gx, G = _local_step(x[0], mem[0], positions[0], tgt[0], P)

    G["w_in"], G["w_q_up"], G["w_kv_up"] = _unpad_w_in(G["w_in"]), _unpad_w_q(G["w_q_up"]), _unpad_w_kv(G["w_kv_up"])
    G["conv_w"] = G["conv_w"][:SSD_CONV]
    recv = _scatter_grads(_pack_grads(G))
    big = [_unpack_big(b) for b in _reduce_adam_big(
        recv, _pack_big(shard, F32), _pack_big({n: M[n][0] for n, _, _ in BIG}, F32), _pack_big({n: V[n][0] for n, _, _ in BIG}, F32))]

    def small_rows(D):
        rows = {n: _row(D[n]) for n in SMALL}
        for n in ("dt_bias", "a_log", "d_skip"):
            rows[n] = _head_lanes(rows[n])
        return rows

    loss_row = jnp.broadcast_to(loss[:, :1], (1, PACK_W))
    small_bufs = _allreduce_adam_small(
        _pack_small({n: G[n] for n in SMALL}, loss_row), _pack_small(small_rows(W)), _pack_small(small_rows(M)),
        _pack_small(small_rows(V)))
    loss_tot = small_bufs[0][LOSS_ROW, 0]
    small = [_unpack_small(b) for b in small_bufs]

    outs = []
    for res_b, res_s in zip(big, small):
        for n in ALL_W:
            if n in res_b:
                outs.append(res_b[n].reshape(W[n].shape))
            else:
                outs.append(res_s[n][:, :W[n].size].reshape(W[n].shape))
    return (loss_tot, gx[None], *outs)


def kernel(x, mem, positions, ln_in_g, ln_in_b, w_in, conv_w, conv_b, dt_bias, a_log, d_skip, ssd_norm_g, q_norm_g, w_q_up, kv_norm_g, w_kv_up, w_mix_out, ln1_g, ln1_b, w_mem_q, w_mem_k, w_mem_v, w_mem_o, ln2_g, ln2_b, w_up, w_down, ln3_g, ln3_b, loss_target, m_ln_in_g, m_ln_in_b, m_w_in, m_conv_w, m_conv_b, m_dt_bias, m_a_log, m_d_skip, m_ssd_norm_g, m_q_norm_g, m_w_q_up, m_kv_norm_g, m_w_kv_up, m_w_mix_out, m_ln1_g, m_ln1_b, m_w_mem_q, m_w_mem_k, m_w_mem_v, m_w_mem_o, m_ln2_g, m_ln2_b, m_w_up, m_w_down, m_ln3_g, m_ln3_b, v_ln_in_g, v_ln_in_b, v_w_in, v_conv_w, v_conv_b, v_dt_bias, v_a_log, v_d_skip, v_ssd_norm_g, v_q_norm_g, v_w_q_up, v_kv_norm_g, v_w_kv_up, v_w_mix_out, v_ln1_g, v_ln1_b, v_w_mem_q, v_w_mem_k, v_w_mem_v, v_w_mem_o, v_ln2_g, v_ln2_b, v_w_up, v_w_down, v_ln3_g, v_ln3_b):
    a = dict(locals())
    W = {n: a[n] for n in ALL_W}
    M = {n: a["m_" + n] for n in ALL_W}
    V = {n: a["v_" + n] for n in ALL_W}
    return _step_overlapped(x, mem, positions, loss_target, W, M, V)


HBM = pl.BlockSpec(memory_space=pltpu.HBM)
SEM = pl.BlockSpec(memory_space=pltpu.SEMAPHORE)
EFFECT = pltpu.SideEffectType.DATAFLOW_SIDE_EFFECTING
SHARD_SHAPE = {n: s for n, s, _ in BIG}
SHARD_AXIS = {n: ax for n, _, ax in BIG}
GATHER_NOW = ("w_in", "conv_w", "w_q_up", "w_kv_up", "w_mix_out")
GATHER_LATE = ("w_mem_q", "w_mem_k", "w_mem_v", "w_mem_o", "w_up", "w_down")


def _my_slot():
    return _slot(*_coords())


def _group_copies(src_refs, land_refs, send_sems, recv_sems, slotted, landing_of_peer):
    x, y, c = _coords()
    my = _slot(x, y, c)
    cps = []
    for a, (s_ref, l_ref) in enumerate(zip(src_refs, land_refs)):
        for k in range(1, N_DEV):
            peer = _peer(k, x, y, c)
            cps.append(pltpu.make_async_remote_copy(
                src_ref=s_ref.at[_slot(*peer)] if slotted else s_ref,
                dst_ref=l_ref.at[_slot(*peer)] if landing_of_peer else l_ref.at[my],
                send_sem=send_sems.at[7 * a + k - 1], recv_sem=recv_sems.at[7 * a + k - 1],
                device_id=peer, device_id_type=MESH))
    return cps


def _send_start(srcs, lands, slotted, name):
    n = len(srcs)

    def body(*refs):
        for cp in _group_copies(refs[:n], refs[n:2 * n], refs[2 * n], refs[2 * n + 1], slotted, False):
            cp.start()
        refs[-1][...] = jnp.zeros(refs[-1].shape, F32)

    res = pl.pallas_call(
        body, name=name,
        out_shape=(pltpu.SemaphoreType.DMA((7 * n,)), pltpu.SemaphoreType.DMA((7 * n,)),
                   *[pltpu.HBM(a.shape, a.dtype) for a in srcs], *[pltpu.HBM(a.shape, a.dtype) for a in lands],
                   jax.ShapeDtypeStruct((8, LANES), F32)),
        in_specs=[HBM] * (2 * n), out_specs=(SEM, SEM, *[HBM] * (2 * n), VM),
        input_output_aliases={i: 2 + i for i in range(2 * n)},
        compiler_params=pltpu.CompilerParams(has_side_effects=EFFECT),
    )(*[pltpu.with_memory_space_constraint(a, pltpu.HBM) for a in list(srcs) + list(lands)])
    return (res[0], res[1], res[2:2 + n], res[2 + n:2 + 2 * n]), res[-1][:1, :1]


def _send_wait(started, after, slotted, name):
    send_sems, recv_sems, srcs, lands = started
    n = len(srcs)

    def body(*refs):
        for cp in _group_copies(refs[:n], refs[n:2 * n], refs[2 * n], refs[2 * n + 1], slotted, True):
            cp.wait_send()
            cp.wait_recv()

    res = pl.pallas_call(
        body, name=name, out_shape=tuple(pltpu.HBM(a.shape, a.dtype) for a in list(srcs) + list(lands)),
        in_specs=[HBM] * (2 * n) + [SEM, SEM, ANY], out_specs=tuple([HBM] * (2 * n)),
        input_output_aliases={i: i for i in range(2 * n)},
        compiler_params=pltpu.CompilerParams(has_side_effects=EFFECT),
    )(*srcs, *lands, send_sems, recv_sems, after)
    return res[n:]


def _landing(own, my):
    return lax.dynamic_update_slice(lax.empty((N_DEV,) + own.shape, own.dtype), own[None], (my,) + (0,) * own.ndim)


def _gather_now(shards):
    n = len(shards)

    def body(*refs):
        x_refs, out_refs = refs[:n], refs[n:2 * n]
        send_sems, recv_sems, local_sems = refs[2 * n:]
        x, y, c = _coords()
        me, sibling = (x, y, c), (x, y, 1 - c)
        chips = [(1 - x, y), (x, 1 - y), (1 - x, 1 - y)]

        def copy(a, k, block, to, src=None):
            rows = out_refs[a].at[_slot(*block)]
            return pltpu.make_async_remote_copy(
                src_ref=rows if src is None else src, dst_ref=rows,
                send_sem=send_sems.at[7 * a + k], recv_sem=recv_sems.at[7 * a + k], device_id=to, device_id_type=MESH)

        mine = [pltpu.make_async_copy(x_refs[a], out_refs[a].at[_slot(*me)], local_sems.at[a]) for a in range(n)]
        for cp in mine:
            cp.start()
        first = []
        for a in range(n):
            first.append(copy(a, 0, me, sibling, src=x_refs[a]))
            first += [copy(a, 1 + j, me, (*chip, c), src=x_refs[a]) for j, chip in enumerate(chips)]
        for cp in first:
            cp.start()
        passed = []
        for j, chip in enumerate(chips):
            for a in range(n):
                copy(a, 1 + j, (*chip, c), me).wait_recv()
                fwd = copy(a, 4 + j, (*chip, c), sibling)
                fwd.start()
                passed.append(fwd)
        for a in range(n):
            copy(a, 0, sibling, me).wait_recv()
            for j, chip in enumerate(chips):
                copy(a, 4 + j, (*chip, 1 - c), me).wait_recv()
        for cp in first + passed:
            cp.wait_send()
        for cp in mine:
            cp.wait()

    return pl.pallas_call(
        body, name="gather_now", out_shape=[jax.ShapeDtypeStruct((N_DEV,) + s.shape, s.dtype) for s in shards],
        in_specs=[ANY] * n, out_specs=[ANY] * n,
        scratch_shapes=[pltpu.SemaphoreType.DMA((7 * n,)), pltpu.SemaphoreType.DMA((7 * n,)), pltpu.SemaphoreType.DMA((n,))],
    )(*shards)


def _full_from_slots(name, slots):
    a, b = SHARD_SHAPE[name]
    return slots.reshape(N_DEV * a, b) if SHARD_AXIS[name] == 0 else slots.transpose(1, 0, 2).reshape(a, N_DEV * b)


def _slots_from_full(name, g):
    a, b = SHARD_SHAPE[name]
    return g.reshape(N_DEV, a, b) if SHARD_AXIS[name] == 0 else g.reshape(a, N_DEV, b).transpose(1, 0, 2)


def _reduce_adam(recv, w, m, v, name):
    _, a, b = recv.shape
    ta = a
    while ta * b * 4 * N_DEV > 4 * 1024 * 1024 and ta % 16 == 0:
        ta //= 2

    def body(r_ref, w_ref, m_ref, v_ref, g_ref, d_ref, nm_ref, nv_ref):
        g = _sum_slots(r_ref)
        g_ref[...] = g
        d_ref[...], nm_ref[...], nv_ref[...] = _adam(w_ref[...], g, m_ref[...], v_ref[...])

    row = pl.BlockSpec((ta, b), lambda i: (i, 0))
    return pl.pallas_call(
        body, name=name, grid=(a // ta,),
        in_specs=[pl.BlockSpec((N_DEV, ta, b), lambda i: (0, i, 0)), row, row, row], out_specs=[row] * 4,
        out_shape=[jax.ShapeDtypeStruct((a, b), F32)] * 4, compiler_params=_cparams(("parallel",)),
    )(recv, w, m, v)


def _step_overlapped(x, mem, positions, tgt, W, M, V):
    my = _my_slot()
    shard = {n: W[n][0] for n, _, _ in BIG}
    send = {n: (shard[n] if n == "conv_w" else shard[n].astype(BF16)) for n in shard}

    now = dict(zip(GATHER_NOW, _gather_now([send[n] for n in GATHER_NOW])))
    late_src = [send[n] for n in GATHER_LATE]
    late_src, _ = lax.optimization_barrier((late_src, now["w_mix_out"]))
    late, tie = _send_start(late_src, [_landing(s, my) for s in late_src], False, "gather_late_start")

    P = {n: _row(W[n]) for n in SMALL}
    for n in ("dt_bias", "a_log", "d_skip"):
        P[n] = _head_lanes(P[n])
    P["ln_in_g"] = P["ln_in_g"] + tie
    full = {n: _full_from_slots(n, now[n]) for n in GATHER_NOW}
    P.update(w_in=_pad_w_in(full["w_in"]), w_q_up=_pad_w_q(full["w_q_up"]), w_kv_up=_pad_w_kv(full["w_kv_up"]),
             conv_w=jnp.pad(full["conv_w"], ((0, 8 - SSD_CONV), (0, 0))), w_mix_out=full["w_mix_out"])

    def late_weights(after):
        lands = _send_wait(late, after, False, "gather_late_wait")
        return {n: _full_from_slots(n, l) for n, l in zip(GATHER_LATE, lands)}

    started = []

    def emit(names, G):
        srcs = []
        for n in names:
            g = G[n]
            if n == "w_in":
                g = _unpad_w_in(g)
            elif n == "w_q_up":
                g = _unpad_w_q(g)
            elif n == "w_kv_up":
                g = _unpad_w_kv(g)
            elif n == "conv_w":
                g = g[:SSD_CONV]
            srcs.append(g if g.ndim == 3 else _slots_from_full(n, g))
        lands = [_landing(lax.dynamic_index_in_dim(s, my, 0, keepdims=False), my) for s in srcs]
        st, tie = _send_start(srcs, lands, True, "scatter_start_%d" % len(started))
        started.append((names, st))
        return tie

    loss, gx, G = _local_step(x[0], mem[0], positions[0], tgt[0], P, late_weights, emit)

    res = {}
    for i, (names, st) in enumerate(started):
        lands = _send_wait(st, gx, True, "scatter_wait_%d" % i)
        for n, recv in zip(names, lands):
            res[n] = _reduce_adam(recv, shard[n], M[n][0], V[n][0], "reduce_adam_" + n)

    small, loss_tot = _allreduce_adam_vectors(
        {n: G[n] for n in SMALL}, loss, {n: _row(W[n]) for n in SMALL}, {n: _row(M[n]) for n in SMALL}, {n: _row(V[n]) for n in SMALL})

    outs = []
    for j in range(4):
        for n in ALL_W:
            outs.append((res[n][j] if n in res else small[j][n]).reshape(W[n].shape))
    return (loss_tot[0, 0], gx[None], *outs)


def _vector_places():
    places = {}
    for r, names in enumerate(_SMALL_ROWS):
        c = 0
        for n in names:
            w = _SMALL_W.get(n, PACK_W)
            places[n] = (r, c, w, SSD_HEADS if n in ("dt_bias", "a_log", "d_skip") else w)
            c += w
    return places


def _allreduce_adam_vectors(grads, loss, Ws, Ms, Vs):
    places = _vector_places()
    ns = len(SMALL)

    def body(*refs):
        g_in, loss_in = refs[:ns], refs[ns]
        w_in, m_in, v_in = refs[ns + 1:2 * ns + 1], refs[2 * ns + 1:3 * ns + 1], refs[3 * ns + 1:4 * ns + 1]
        o = 4 * ns + 1
        outs = [refs[o + j * ns:o + (j + 1) * ns] for j in range(4)]
        loss_out, stage, land, send_sems, recv_sems = refs[o + 4 * ns:]
        stage[...] = jnp.zeros(stage.shape, F32)
        for i, n in enumerate(SMALL):
            r, c, w, _ = places[n]
            stage[r:r + 1, c:c + w] = g_in[i][...]
        stage[LOSS_ROW:LOSS_ROW + 1, 0:LANES] = loss_in[...]
        x, y, c_ = _coords()
        my = _slot(x, y, c_)
        cps = []
        for k in range(1, N_DEV):
            peer = _peer(k, x, y, c_)
            cps.append(pltpu.make_async_remote_copy(
                src_ref=stage, dst_ref=land.at[my], send_sem=send_sems.at[k - 1], recv_sem=recv_sems.at[k - 1],
                device_id=peer, device_id_type=MESH))
        for cp in cps:
            cp.start()
        land[my] = stage[...]
        for cp in cps:
            cp.wait_recv()
        for cp in cps:
            cp.wait_send()
        tot = _sum_slots(land)
        loss_out[...] = tot[LOSS_ROW:LOSS_ROW + 1, 0:LANES]
        for i, n in enumerate(SMALL):
            r, c, _, wt = places[n]
            g = tot[r:r + 1, c:c + wt]
            outs[0][i][...] = g
            outs[1][i][...], outs[2][i][...], outs[3][i][...] = _adam(w_in[i][...], g, m_in[i][...], v_in[i][...])

    shapes = [jax.ShapeDtypeStruct((1, places[n][3]), F32) for n in SMALL]
    res = pl.pallas_call(
        body, name="allreduce_vectors", in_specs=[VM] * (4 * ns + 1), out_specs=[VM] * (4 * ns + 1),
        out_shape=shapes * 4 + [jax.ShapeDtypeStruct((1, LANES), F32)],
        scratch_shapes=[pltpu.VMEM((SMALL_R, PACK_W), F32), pltpu.VMEM((N_DEV, SMALL_R, PACK_W), F32),
                        pltpu.SemaphoreType.DMA((7,)), pltpu.SemaphoreType.DMA((7,))],
    )(*[grads[n] for n in SMALL], loss, *[Ws[n] for n in SMALL], *[Ms[n] for n in SMALL], *[Vs[n] for n in SMALL])
    return [dict(zip(SMALL, res[j * ns:(j + 1) * ns])) for j in range(4)], res[4 * ns]
```

```python
import functools
import math

import jax
import jax.numpy as jnp
from jax import lax
from jax.experimental import pallas as pl
from jax.experimental.pallas import tpu as pltpu

F32, BF16 = jnp.float32, jnp.bfloat16

N_DEV = 8
D_MODEL = 1024
SSD_HEADS, SSD_HEAD_DIM, SSD_INNER, SSD_STATE, SSD_CONV, SSD_CHUNK = 8, 64, 512, 128, 4, 128
SSD_XBC = 1024
MLA_HEADS, MLA_NOPE, MLA_ROPE, MLA_QK, MLA_V = 8, 64, 32, 96, 64
MLA_Q_RANK, MLA_KV_RANK = 384, 256
ROPE_THETA = 10000.0
MEM_HEADS, MEM_HEAD_DIM = 4, 256
D_FF = 4096
IN_WIDTH = 2216
LN_EPS, RMS_EPS = 1e-5, 1e-6
ALPHA = 2.0 ** 0.25
ADAM_LR, ADAM_B1, ADAM_B2, ADAM_EPS, ADAM_WD, ADAM_STEP = 0.001, 0.9, 0.999, 1e-08, 0.01, 10

LANES = 128
NEG = -1e30
VMEM_LIMIT = 56 * 1024 * 1024

PQ, PZ, PX, PKV, PDT, PKR, PW = 0, 512, 1024, 2048, 2304, 2432, 2560
KR_LANE = 64


def _cparams(sem):
    return pltpu.CompilerParams(dimension_semantics=sem, vmem_limit_bytes=VMEM_LIMIT)


def _sigmoid(x):
    return 1.0 / (1.0 + jnp.exp(-x))


def _mm(a, b, mode, name, *, tm=512, tn=None, tk=None, out_dtypes=(F32,), epi=None, extras=(), col_slots=False):
    if mode == "nn":
        (M, K), (K2, N) = a.shape, b.shape
    elif mode == "nt":
        (M, K), (N, K2) = a.shape, b.shape
    else:
        (K, M), (K2, N) = a.shape, b.shape
    assert K == K2, (name, a.shape, b.shape)
    tm, tn, tk = min(tm, M), min(tn or N, N), min(tk or K, K)
    assert M % tm == 0 and N % tn == 0 and K % tk == 0, (name, M, N, K, tm, tn, tk)
    gk = K // tk
    a_spec = pl.BlockSpec((tk, tm), lambda i, j, k: (k, i)) if mode == "tn" else pl.BlockSpec((tm, tk), lambda i, j, k: (i, k))
    b_spec = pl.BlockSpec((tn, tk), lambda i, j, k: (j, k)) if mode == "nt" else pl.BlockSpec((tk, tn), lambda i, j, k: (k, j))
    dims = {"nn": ((1,), (0,)), "nt": ((1,), (1,)), "tn": ((0,), (0,))}[mode]
    ex_specs = []
    for arr, kind in extras:
        if kind == "mn":
            ex_specs.append(pl.BlockSpec((tm, tn), lambda i, j, k: (i, j)))
        elif kind == "n":
            ex_specs.append(pl.BlockSpec((1, tn), lambda i, j, k: (0, j)))
        else:
            ex_specs.append(pl.BlockSpec((tm, arr.shape[1]), lambda i, j, k: (i, 0)))
    ne, no = len(extras), len(out_dtypes)

    def body(*refs):
        a_ref, b_ref = refs[0], refs[1]
        ex, outs = refs[2:2 + ne], refs[2 + ne:2 + ne + no]
        part = lax.dot_general(a_ref[...].astype(BF16), b_ref[...].astype(BF16), (dims, ((), ())),
                               preferred_element_type=F32)

        def finish(acc):
            res = epi(acc, *[e[...] for e in ex]) if epi is not None else (acc,)
            for o, r in zip(outs, res):
                o[...] = r.astype(o.dtype)

        if gk == 1:
            finish(part)
        else:
            acc_ref = refs[-1]
            k = pl.program_id(2)

            @pl.when(k == 0)
            def _():
                acc_ref[...] = part

            @pl.when(k > 0)
            def _():
                acc_ref[...] += part

            @pl.when(k == gk - 1)
            def _():
                finish(acc_ref[...])

    res = pl.pallas_call(
        body, name=name, grid=(M // tm, N // tn, gk),
        in_specs=[a_spec, b_spec] + ex_specs,
        out_specs=[pl.BlockSpec((None, tm, tn), lambda i, j, k: (j, i, 0)) if col_slots else pl.BlockSpec((tm, tn), lambda i, j, k: (i, j))
                   for _ in out_dtypes],
        out_shape=[jax.ShapeDtypeStruct((N // tn, M, tn) if col_slots else (M, N), dt) for dt in out_dtypes],
        scratch_shapes=[pltpu.VMEM((tm, tn), F32)] if gk > 1 else [],
        compiler_params=_cparams(("parallel", "parallel", "arbitrary")),
    )(a, b, *[e[0] for e in extras])
    return res[0] if no == 1 else res


def _ln_stats(r):
    mu = jnp.mean(r, axis=-1, keepdims=True)
    xc = r - mu
    var = jnp.mean(xc * xc, axis=-1, keepdims=True)
    rstd = lax.rsqrt(var + LN_EPS)
    return xc * rstd, rstd


def _ln_fwd(terms, g, b, name, tm=512):
    S, D = terms[0][0].shape
    coefs = [c for _, c in terms]
    nt = len(terms)

    def body(*refs):
        r = sum(c * t[...] for t, c in zip(refs[:nt], coefs))
        xh, _ = _ln_stats(r)
        h = xh * refs[nt][...] + refs[nt + 1][...]
        refs[nt + 2][...] = h
        refs[nt + 3][...] = h.astype(BF16)

    row = pl.BlockSpec((tm, D), lambda i: (i, 0))
    vec = pl.BlockSpec((1, D), lambda i: (0, 0))
    return pl.pallas_call(
        body, name=name, grid=(S // tm,), in_specs=[row] * nt + [vec, vec], out_specs=[row, row],
        out_shape=[jax.ShapeDtypeStruct((S, D), F32), jax.ShapeDtypeStruct((S, D), BF16)], compiler_params=_cparams(("parallel",)),
    )(*[t for t, _ in terms], g, b)


def _ln_bwd(terms, dterms, g, name, tm=512):
    S, D = terms[0][0].shape
    coefs, dcoefs = [c for _, c in terms], [c for _, c in dterms]
    nt, nd = len(terms), len(dterms)

    def body(*refs):
        i = pl.program_id(0)
        r = sum(c * t[...] for t, c in zip(refs[:nt], coefs))
        dh = sum(c * t[...].astype(F32) for t, c in zip(refs[nt:nt + nd], dcoefs))
        g_ref = refs[nt + nd]
        dr_ref, drb_ref, dg_ref, db_ref = refs[nt + nd + 1:]
        xh, rstd = _ln_stats(r)
        dxh = dh * g_ref[...]
        m1 = jnp.mean(dxh, axis=-1, keepdims=True)
        m2 = jnp.mean(dxh * xh, axis=-1, keepdims=True)
        dr = rstd * (dxh - m1 - xh * m2)
        dr_ref[...] = dr
        drb_ref[...] = dr.astype(BF16)
        pg = jnp.sum(dh * xh, axis=0, keepdims=True)
        pb = jnp.sum(dh, axis=0, keepdims=True)

        @pl.when(i == 0)
        def _():
            dg_ref[...] = pg
            db_ref[...] = pb

        @pl.when(i > 0)
        def _():
            dg_ref[...] += pg
            db_ref[...] += pb

    row = pl.BlockSpec((tm, D), lambda i: (i, 0))
    vec = pl.BlockSpec((1, D), lambda i: (0, 0))
    return pl.pallas_call(
        body, name=name, grid=(S // tm,), in_specs=[row] * (nt + nd) + [vec], out_specs=[row, row, vec, vec],
        out_shape=[jax.ShapeDtypeStruct((S, D), F32), jax.ShapeDtypeStruct((S, D), BF16), jax.ShapeDtypeStruct((1, D), F32),
                   jax.ShapeDtypeStruct((1, D), F32)],
        compiler_params=_cparams(("arbitrary",)),
    )(*[t for t, _ in terms], *[t for t, _ in dterms], g)


def _ln_loss_bwd(terms, g, b, tgt, name, tm=512):
    S, D = terms[0][0].shape
    coefs = [c for _, c in terms]
    nt = len(terms)

    def body(*refs):
        i = pl.program_id(0)
        r = sum(c * t[...] for t, c in zip(refs[:nt], coefs))
        g_ref, b_ref, t_ref = refs[nt:nt + 3]
        loss_ref, dr_ref, drb_ref, dg_ref, db_ref = refs[nt + 3:]
        xh, rstd = _ln_stats(r)
        h = xh * g_ref[...] + b_ref[...]
        diff = h - t_ref[...]
        pl_ = 0.5 * jnp.sum(jnp.mean(diff * diff, axis=-1, keepdims=True), axis=0, keepdims=True)
        dh = diff * (1.0 / D)
        dxh = dh * g_ref[...]
        m1 = jnp.mean(dxh, axis=-1, keepdims=True)
        m2 = jnp.mean(dxh * xh, axis=-1, keepdims=True)
        dr = rstd * (dxh - m1 - xh * m2)
        dr_ref[...] = dr
        drb_ref[...] = dr.astype(BF16)
        pg = jnp.sum(dh * xh, axis=0, keepdims=True)
        pb = jnp.sum(dh, axis=0, keepdims=True)
        plb = jnp.broadcast_to(pl_, (1, LANES))

        @pl.when(i == 0)
        def _():
            dg_ref[...] = pg
            db_ref[...] = pb
            loss_ref[...] = plb

        @pl.when(i > 0)
        def _():
            dg_ref[...] += pg
            db_ref[...] += pb
            loss_ref[...] += plb

    row = pl.BlockSpec((tm, D), lambda i: (i, 0))
    vec = pl.BlockSpec((1, D), lambda i: (0, 0))
    lvec = pl.BlockSpec((1, LANES), lambda i: (0, 0))
    return pl.pallas_call(
        body, name=name, grid=(S // tm,), in_specs=[row] * nt + [vec, vec, row], out_specs=[lvec, row, row, vec, vec],
        out_shape=[jax.ShapeDtypeStruct((1, LANES), F32), jax.ShapeDtypeStruct((S, D), F32), jax.ShapeDtypeStruct((S, D), BF16),
                   jax.ShapeDtypeStruct((1, D), F32), jax.ShapeDtypeStruct((1, D), F32)],
        compiler_params=_cparams(("arbitrary",)),
    )(*[t for t, _ in terms], g, b, tgt)


def _rope_tables(positions):
    half = MLA_ROPE // 2
    inv_freq = jnp.power(ROPE_THETA, -jnp.arange(half, dtype=F32) / half)
    ang = positions.astype(F32)[:, None] * inv_freq
    cos, sin = jnp.cos(ang), jnp.sin(ang)
    S = positions.shape[0]
    one, zero = jnp.ones((S, MLA_NOPE), F32), jnp.zeros((S, half), F32)
    pad = jnp.zeros((S, LANES - MLA_QK), F32)
    c = jnp.concatenate([one, cos, cos, pad], axis=1)
    s1 = jnp.concatenate([0 * one, -sin, zero, pad], axis=1)
    s2 = jnp.concatenate([0 * one, zero, sin, pad], axis=1)
    return c, s1, s2


def _rope_block(x, c, s1, s2):
    half = MLA_ROPE // 2
    return x * c + pltpu.roll(x, LANES - half, axis=1) * s1 + pltpu.roll(x, half, axis=1) * s2


def _rms_fwd(x, g):
    r = lax.rsqrt(jnp.mean(x * x, axis=-1, keepdims=True) + RMS_EPS)
    return x * r * g


def _rms_bwd(x, g, dy):
    r = lax.rsqrt(jnp.mean(x * x, axis=-1, keepdims=True) + RMS_EPS)
    xh = x * r
    dyh = dy * g
    dx = r * (dyh - xh * jnp.mean(dyh * xh, axis=-1, keepdims=True))
    return dx, jnp.sum(dy * xh, axis=0, keepdims=True)


def _mla_prep(proj, qg, kvg, tabs, tm=512):
    S = proj.shape[0]

    def body(ql_ref, kvl_ref, kr_ref, qg_ref, kvg_ref, c_ref, s1_ref, s2_ref, qn_ref, kvn_ref, kpe_ref):
        qn_ref[...] = _rms_fwd(ql_ref[...], qg_ref[...]).astype(BF16)
        kvn_ref[...] = _rms_fwd(kvl_ref[...], kvg_ref[...]).astype(BF16)
        kpe_ref[...] = _rope_block(kr_ref[...], c_ref[...], s1_ref[...], s2_ref[...])

    tab = pl.BlockSpec((tm, LANES), lambda i: (i, 0))
    return pl.pallas_call(
        body, name="mla_prep", grid=(S // tm,),
        in_specs=[pl.BlockSpec((tm, MLA_Q_RANK), lambda i: (i, PQ // MLA_Q_RANK)),
                  pl.BlockSpec((tm, MLA_KV_RANK), lambda i: (i, PKV // MLA_KV_RANK)),
                  pl.BlockSpec((tm, LANES), lambda i: (i, PKR // LANES)),
                  pl.BlockSpec((1, MLA_Q_RANK), lambda i: (0, 0)), pl.BlockSpec((1, MLA_KV_RANK), lambda i: (0, 0)),
                  tab, tab, tab],
        out_specs=[pl.BlockSpec((tm, MLA_Q_RANK), lambda i: (i, 0)), pl.BlockSpec((tm, MLA_KV_RANK), lambda i: (i, 0)), tab],
        out_shape=[jax.ShapeDtypeStruct((S, MLA_Q_RANK), BF16), jax.ShapeDtypeStruct((S, MLA_KV_RANK), BF16),
                   jax.ShapeDtypeStruct((S, LANES), F32)],
        compiler_params=_cparams(("parallel",)),
    )(proj, proj, proj, qg, kvg, *tabs)


def _mla_prep_bwd(proj, qg, kvg, tabs, dqn, dkvn, dk_all, tm=512):
    S = proj.shape[0]

    def body(ql_ref, kvl_ref, qg_ref, kvg_ref, c_ref, s1_ref, s2_ref, dqn_ref, dkvn_ref, dk_ref,
             dql_ref, dkvl_ref, dkr_ref, dqg_ref, dkvg_ref):
        i = pl.program_id(0)
        dql, pq = _rms_bwd(ql_ref[...], qg_ref[...], dqn_ref[...])
        dkvl, pkv = _rms_bwd(kvl_ref[...], kvg_ref[...], dkvn_ref[...])
        dql_ref[...] = dql
        dkvl_ref[...] = dkvl
        dk = dk_ref[...]
        dkpe = dk[:, 0:LANES]
        for h in range(1, MLA_HEADS):
            dkpe = dkpe + dk[:, h * LANES:(h + 1) * LANES]
        lane = lax.broadcasted_iota(jnp.int32, dkpe.shape, 1)
        dkpe = jnp.where((lane >= KR_LANE) & (lane < KR_LANE + MLA_ROPE), dkpe, 0.0)
        dkr_ref[...] = _rope_block(dkpe, c_ref[...], -s1_ref[...], -s2_ref[...])

        @pl.when(i == 0)
        def _():
            dqg_ref[...] = pq
            dkvg_ref[...] = pkv

        @pl.when(i > 0)
        def _():
            dqg_ref[...] += pq
            dkvg_ref[...] += pkv

    tab = pl.BlockSpec((tm, LANES), lambda i: (i, 0))
    qspec = pl.BlockSpec((tm, MLA_Q_RANK), lambda i: (i, 0))
    kvspec = pl.BlockSpec((tm, MLA_KV_RANK), lambda i: (i, 0))
    qv, kvv = pl.BlockSpec((1, MLA_Q_RANK), lambda i: (0, 0)), pl.BlockSpec((1, MLA_KV_RANK), lambda i: (0, 0))
    return pl.pallas_call(
        body, name="mla_prep_bwd", grid=(S // tm,),
        in_specs=[pl.BlockSpec((tm, MLA_Q_RANK), lambda i: (i, PQ // MLA_Q_RANK)),
                  pl.BlockSpec((tm, MLA_KV_RANK), lambda i: (i, PKV // MLA_KV_RANK)),
                  qv, kvv, tab, tab, tab, qspec, kvspec, pl.BlockSpec((tm, MLA_HEADS * LANES), lambda i: (i, 0))],
        out_specs=[qspec, kvspec, tab, qv, kvv],
        out_shape=[jax.ShapeDtypeStruct((S, MLA_Q_RANK), F32), jax.ShapeDtypeStruct((S, MLA_KV_RANK), F32),
                   jax.ShapeDtypeStruct((S, LANES), F32), jax.ShapeDtypeStruct((1, MLA_Q_RANK), F32),
                   jax.ShapeDtypeStruct((1, MLA_KV_RANK), F32)],
        compiler_params=_cparams(("arbitrary",)),
    )(proj, proj, qg, kvg, *tabs, dqn, dkvn, dk_all)


def _rope_bwd_all(dq_all, tabs, tm=512):
    S, W = dq_all.shape

    def body(dq_ref, c_ref, s1_ref, s2_ref, o_ref):
        c, s1, s2 = c_ref[...], -s1_ref[...], -s2_ref[...]
        for h in range(W // LANES):
            o_ref[:, h * LANES:(h + 1) * LANES] = _rope_block(dq_ref[:, h * LANES:(h + 1) * LANES], c, s1, s2).astype(BF16)

    tab = pl.BlockSpec((tm, LANES), lambda i: (i, 0))
    row = pl.BlockSpec((tm, W), lambda i: (i, 0))
    return pl.pallas_call(body, name="rope_bwd", grid=(S // tm,), in_specs=[row, tab, tab, tab], out_specs=row,
                          out_shape=jax.ShapeDtypeStruct((S, W), BF16), compiler_params=_cparams(("parallel",)))(dq_all, *tabs)


ATT_SCALE = MLA_QK ** -0.5
N_PAIR = MLA_HEADS // 2


def _causal_mask(qi, ki, tq, tk):
    row = qi * tq + lax.broadcasted_iota(jnp.int32, (tq, tk), 0)
    col = ki * tk + lax.broadcasted_iota(jnp.int32, (tq, tk), 1)
    return col <= row


def _lane_tile(x, n):
    return jnp.concatenate([x] * n, axis=1) if n > 1 else x


def _attn_fwd(q_all, kv_all, tq=512, tk=1024):
    S = q_all.shape[0]
    tq, tk = min(tq, S), min(tk, S)
    nq, nk, nb, r = S // tq, S // tk, tk // LANES, tk // tq

    def body(q_ref, k_ref, v_ref, o_ref, lse_ref, m_s, l_s, acc_s):
        qi, ki = pl.program_id(1), pl.program_id(2)
        last = lax.div(qi, r)

        @pl.when(ki == 0)
        def _():
            m_s[...] = jnp.full(m_s.shape, NEG, F32)
            l_s[...] = jnp.zeros(l_s.shape, F32)
            acc_s[...] = jnp.zeros(acc_s.shape, F32)

        @pl.when(ki <= last)
        def _():
            v = v_ref[...]
            mask = _causal_mask(qi, ki, tq, tk)
            for hh in range(2):
                q = q_ref[:, hh * LANES:(hh + 1) * LANES]
                k = k_ref[:, hh * LANES:(hh + 1) * LANES]
                s = lax.dot_general(q, k, (((1,), (1,)), ((), ())), preferred_element_type=F32) * ATT_SCALE
                s = jnp.where(mask, s, NEG)
                m_prev = m_s[hh]
                m_new = jnp.maximum(m_prev, jnp.max(s, axis=-1, keepdims=True))
                p = jnp.exp(s - _lane_tile(m_new, nb))
                alpha = jnp.exp(m_prev - m_new)
                ps = p[:, :LANES]
                for j in range(1, nb):
                    ps = ps + p[:, j * LANES:(j + 1) * LANES]
                l_s[hh] = alpha * l_s[hh] + ps
                acc_s[hh] = alpha * acc_s[hh] + jnp.dot(p.astype(BF16), v, preferred_element_type=F32)
                m_s[hh] = m_new

        @pl.when(ki == last)
        def _():
            first = lax.broadcasted_iota(jnp.int32, (tq, LANES), 1) < MLA_V
            l0 = jnp.sum(l_s[0], axis=-1, keepdims=True)
            l1 = jnp.sum(l_s[1], axis=-1, keepdims=True)
            o_ref[...] = jnp.where(first, acc_s[0] / l0, acc_s[1] / l1)
            lse_ref[:, :LANES] = m_s[0] + jnp.log(l0)
            lse_ref[:, LANES:] = m_s[1] + jnp.log(l1)

    return pl.pallas_call(
        body, name="mla_attn_fwd", grid=(N_PAIR, nq, nk),
        in_specs=[pl.BlockSpec((tq, 2 * LANES), lambda p, qi, ki: (qi, p)),
                  pl.BlockSpec((tk, 2 * LANES), lambda p, qi, ki: (jnp.minimum(ki, lax.div(qi, r)), p)),
                  pl.BlockSpec((tk, LANES), lambda p, qi, ki: (jnp.minimum(ki, lax.div(qi, r)), MLA_HEADS + p))],
        out_specs=[pl.BlockSpec((tq, LANES), lambda p, qi, ki: (qi, p)), pl.BlockSpec((tq, 2 * LANES), lambda p, qi, ki: (qi, p))],
        out_shape=[jax.ShapeDtypeStruct((S, MLA_HEADS * MLA_V), F32), jax.ShapeDtypeStruct((S, MLA_HEADS * LANES), F32)],
        scratch_shapes=[pltpu.VMEM((2, tq, LANES), F32), pltpu.VMEM((2, tq, LANES), F32), pltpu.VMEM((2, tq, LANES), F32)],
        compiler_params=_cparams(("parallel", "parallel", "arbitrary")),
    )(q_all, kv_all, kv_all)


def _attn_delta(dcat, o, tm=512):
    S = o.shape[0]

    def body(do_ref, o_ref, d_ref):
        prod = do_ref[...] * o_ref[...]
        first = lax.broadcasted_iota(jnp.int32, (tm, LANES), 1) < MLA_V
        for p in range(N_PAIR):
            pp = prod[:, p * LANES:(p + 1) * LANES]
            d0 = jnp.sum(jnp.where(first, pp, 0.0), axis=-1, keepdims=True)
            d1 = jnp.sum(jnp.where(first, 0.0, pp), axis=-1, keepdims=True)
            d_ref[:, 2 * p * LANES:(2 * p + 1) * LANES] = jnp.broadcast_to(d0, (tm, LANES))
            d_ref[:, (2 * p + 1) * LANES:(2 * p + 2) * LANES] = jnp.broadcast_to(d1, (tm, LANES))

    W = MLA_HEADS * MLA_V
    return pl.pallas_call(
        body, name="mla_attn_delta", grid=(S // tm,),
        in_specs=[pl.BlockSpec((tm, W), lambda i: (i, 1)), pl.BlockSpec((tm, W), lambda i: (i, 0))],
        out_specs=pl.BlockSpec((tm, MLA_HEADS * LANES), lambda i: (i, 0)),
        out_shape=jax.ShapeDtypeStruct((S, MLA_HEADS * LANES), F32), compiler_params=_cparams(("parallel",)),
    )(dcat, o)


def _attn_bwd(q_all, kv_all, dcat, lse, delta, tq=512, tk=512):
    S = q_all.shape[0]
    tq, tk = min(tq, S), min(tk, S)
    nq, nk, nb = S // tq, S // tk, tk // LANES
    assert tq == tk

    def body(q_ref, k_ref, v_ref, do_ref, lse_ref, dl_ref, dq_ref, dk_ref, dv_ref, dk_s, dv_s):
        ki, qi = pl.program_id(1), pl.program_id(2)

        @pl.when((ki == 0) & (qi == 0))
        def _():
            dq_ref[...] = jnp.zeros(dq_ref.shape, F32)

        @pl.when(qi == 0)
        def _():
            dk_s[...] = jnp.zeros(dk_s.shape, F32)
            dv_s[...] = jnp.zeros(dv_s.shape, F32)

        @pl.when(qi >= ki)
        def _():
            v, do = v_ref[...], do_ref[...]
            first = lax.broadcasted_iota(jnp.int32, (tq, LANES), 1) < MLA_V
            firstk = lax.broadcasted_iota(jnp.int32, (tk, LANES), 1) < MLA_V
            mask = _causal_mask(qi, ki, tq, tk)
            do_b = do.astype(BF16)
            rows = pl.ds(pl.multiple_of(qi * tq, tq), tq)
            for hh in range(2):
                sl = slice(hh * LANES, (hh + 1) * LANES)
                q, k = q_ref[:, sl], k_ref[:, sl]
                s = lax.dot_general(q, k, (((1,), (1,)), ((), ())), preferred_element_type=F32) * ATT_SCALE
                p = jnp.exp(jnp.where(mask, s, NEG) - _lane_tile(lse_ref[:, sl], nb))
                do_h = jnp.where(first if hh == 0 else ~first, do, 0.0).astype(BF16)
                dp = lax.dot_general(do_h, v, (((1,), (1,)), ((), ())), preferred_element_type=F32)
                ds_b = (p * (dp - _lane_tile(dl_ref[:, sl], nb)) * ATT_SCALE).astype(BF16)
                pv = lax.dot_general(p.astype(BF16), do_b, (((0,), (0,)), ((), ())), preferred_element_type=F32)
                dv_s[...] += jnp.where(firstk if hh == 0 else ~firstk, pv, 0.0)
                dk_s[:, sl] += lax.dot_general(ds_b, q, (((0,), (0,)), ((), ())), preferred_element_type=F32)
                dq_ref[rows, sl] += jnp.dot(ds_b, k, preferred_element_type=F32)

        @pl.when(qi == nq - 1)
        def _():
            dk_ref[...] = dk_s[...]
            dv_ref[...] = dv_s[...]

    wide = pl.BlockSpec((tq, 2 * LANES), lambda p, ki, qi: (jnp.maximum(qi, ki), p))
    return pl.pallas_call(
        body, name="mla_attn_bwd", grid=(N_PAIR, nk, nq),
        in_specs=[wide, pl.BlockSpec((tk, 2 * LANES), lambda p, ki, qi: (ki, p)),
                  pl.BlockSpec((tk, LANES), lambda p, ki, qi: (ki, MLA_HEADS + p)),
                  pl.BlockSpec((tq, LANES), lambda p, ki, qi: (jnp.maximum(qi, ki), N_PAIR + p)), wide, wide],
        out_specs=[pl.BlockSpec((S, 2 * LANES), lambda p, ki, qi: (0, p)),
                   pl.BlockSpec((tk, 2 * LANES), lambda p, ki, qi: (ki, p)), pl.BlockSpec((tk, LANES), lambda p, ki, qi: (ki, p))],
        out_shape=[jax.ShapeDtypeStruct((S, MLA_HEADS * LANES), F32), jax.ShapeDtypeStruct((S, MLA_HEADS * LANES), F32),
                   jax.ShapeDtypeStruct((S, MLA_HEADS * MLA_V), F32)],
        scratch_shapes=[pltpu.VMEM((tk, 2 * LANES), F32), pltpu.VMEM((tk, LANES), F32)],
        compiler_params=_cparams(("parallel", "arbitrary", "arbitrary")),
    )(q_all, kv_all, kv_all, dcat, lse, delta)


def _attn_p_ds(q, k, v, do, lse_h, delta_h, half_mask, mask):
    s = lax.dot_general(q, k, (((1,), (1,)), ((), ())), preferred_element_type=F32) * ATT_SCALE
    p = jnp.exp(jnp.where(mask, s, NEG) - lse_h)
    do_h = jnp.where(half_mask, do, 0.0).astype(BF16)
    dp = lax.dot_general(do_h, v, (((1,), (1,)), ((), ())), preferred_element_type=F32)
    ds = p * (dp - delta_h) * ATT_SCALE
    return p, ds


def _attn_deltas(do, o, first):
    prod = do * o
    return (jnp.sum(jnp.where(first, prod, 0.0), axis=-1, keepdims=True),
            jnp.sum(jnp.where(first, 0.0, prod), axis=-1, keepdims=True))


def _attn_bwd_dq(q_all, kv_all, dcat, o, lse, tq=512, tk=512):
    S = q_all.shape[0]
    nq, nk = S // tq, S // tk

    def body(q_ref, k_ref, v_ref, do_ref, o_ref, lse_ref, dq_ref, acc_s):
        qi, ki = pl.program_id(1), pl.program_id(2)

        @pl.when(ki == 0)
        def _():
            acc_s[...] = jnp.zeros(acc_s.shape, F32)

        @pl.when(ki <= qi)
        def _():
            v, do, lse_t = v_ref[...], do_ref[...], lse_ref[...]
            first = lax.broadcasted_iota(jnp.int32, (tq, LANES), 1) < MLA_V
            deltas = _attn_deltas(do, o_ref[...], first)
            mask = _causal_mask(qi, ki, tq, tk)
            for hh in range(2):
                k = k_ref[:, hh * LANES:(hh + 1) * LANES]
                _, ds = _attn_p_ds(q_ref[:, hh * LANES:(hh + 1) * LANES], k, v, do,
                                   lse_t[:, hh * MLA_V:hh * MLA_V + 1], deltas[hh], first if hh == 0 else ~first, mask)
                acc_s[:, hh * LANES:(hh + 1) * LANES] += jnp.dot(ds.astype(BF16), k, preferred_element_type=F32)

        @pl.when(ki == qi)
        def _():
            dq_ref[...] = acc_s[...]

    half = pl.BlockSpec((tq, LANES), lambda p, qi, ki: (qi, p))
    return pl.pallas_call(
        body, name="mla_attn_bwd_dq", grid=(N_PAIR, nq, nk),
        in_specs=[pl.BlockSpec((tq, 2 * LANES), lambda p, qi, ki: (qi, p)),
                  pl.BlockSpec((tk, 2 * LANES), lambda p, qi, ki: (jnp.minimum(ki, qi), p)),
                  pl.BlockSpec((tk, LANES), lambda p, qi, ki: (jnp.minimum(ki, qi), MLA_HEADS + p)),
                  pl.BlockSpec((tq, LANES), lambda p, qi, ki: (qi, N_PAIR + p)), half, half],
        out_specs=pl.BlockSpec((tq, 2 * LANES), lambda p, qi, ki: (qi, p)),
        out_shape=jax.ShapeDtypeStruct((S, MLA_HEADS * LANES), F32),
        scratch_shapes=[pltpu.VMEM((tq, 2 * LANES), F32)],
        compiler_params=_cparams(("parallel", "parallel", "arbitrary")),
    )(q_all, kv_all, kv_all, dcat, o, lse)


def _attn_bwd_dkv(q_all, kv_all, dcat, o, lse, tq=512, tk=512):
    S = q_all.shape[0]
    nq, nk = S // tq, S // tk

    def body(q_ref, k_ref, v_ref, do_ref, o_ref, lse_ref, dk_ref, dv_ref, dk_s, dv_s):
        ki, qi = pl.program_id(1), pl.program_id(2)

        @pl.when(qi == 0)
        def _():
            dk_s[...] = jnp.zeros(dk_s.shape, F32)
            dv_s[...] = jnp.zeros(dv_s.shape, F32)

        @pl.when(qi >= ki)
        def _():
            v, do, lse_t = v_ref[...], do_ref[...], lse_ref[...]
            first = lax.broadcasted_iota(jnp.int32, (tq, LANES), 1) < MLA_V
            firstk = lax.broadcasted_iota(jnp.int32, (tk, LANES), 1) < MLA_V
            deltas = _attn_deltas(do, o_ref[...], first)
            mask = _causal_mask(qi, ki, tq, tk)
            do_b = do.astype(BF16)
            for hh in range(2):
                q = q_ref[:, hh * LANES:(hh + 1) * LANES]
                p, ds = _attn_p_ds(q, k_ref[:, hh * LANES:(hh + 1) * LANES], v, do,
                                   lse_t[:, hh * MLA_V:hh * MLA_V + 1], deltas[hh], first if hh == 0 else ~first, mask)
                pv = lax.dot_general(p.astype(BF16), do_b, (((0,), (0,)), ((), ())), preferred_element_type=F32)
                dv_s[...] += jnp.where(firstk if hh == 0 else ~firstk, pv, 0.0)
                dk_s[:, hh * LANES:(hh + 1) * LANES] += lax.dot_general(ds.astype(BF16), q, (((0,), (0,)), ((), ())),
                                                                       preferred_element_type=F32)

        @pl.when(qi == nq - 1)
        def _():
            dk_ref[...] = dk_s[...]
            dv_ref[...] = dv_s[...]

    half = pl.BlockSpec((tq, LANES), lambda p, ki, qi: (jnp.maximum(qi, ki), p))
    return pl.pallas_call(
        body, name="mla_attn_bwd_dkv", grid=(N_PAIR, nk, nq),
        in_specs=[pl.BlockSpec((tq, 2 * LANES), lambda p, ki, qi: (jnp.maximum(qi, ki), p)),
                  pl.BlockSpec((tk, 2 * LANES), lambda p, ki, qi: (ki, p)),
                  pl.BlockSpec((tk, LANES), lambda p, ki, qi: (ki, MLA_HEADS + p)),
                  pl.BlockSpec((tq, LANES), lambda p, ki, qi: (jnp.maximum(qi, ki), N_PAIR + p)), half, half],
        out_specs=[pl.BlockSpec((tk, 2 * LANES), lambda p, ki, qi: (ki, p)), pl.BlockSpec((tk, LANES), lambda p, ki, qi: (ki, p))],
        out_shape=[jax.ShapeDtypeStruct((S, MLA_HEADS * LANES), F32), jax.ShapeDtypeStruct((S, MLA_HEADS * MLA_V), F32)],
        scratch_shapes=[pltpu.VMEM((tk, 2 * LANES), F32), pltpu.VMEM((tk, LANES), F32)],
        compiler_params=_cparams(("parallel", "parallel", "arbitrary")),
    )(q_all, kv_all, kv_all, dcat, o, lse)


MEM_SCALE = MEM_HEAD_DIM ** -0.5


def _mem_probs(q, k):
    s = lax.dot_general(q, k, (((1,), (1,)), ((), ())), preferred_element_type=F32) * MEM_SCALE
    e = jnp.exp(s - jnp.max(s, axis=-1, keepdims=True))
    return e / jnp.sum(e, axis=-1, keepdims=True)


def _mem_attn_fwd(qm, km, vm, tq=512):
    S, W = qm.shape
    M = km.shape[0]

    def body(q_ref, k_ref, v_ref, o_ref):
        for h in range(MEM_HEADS):
            sl = slice(h * MEM_HEAD_DIM, (h + 1) * MEM_HEAD_DIM)
            p = _mem_probs(q_ref[:, sl], k_ref[:, sl])
            o_ref[:, sl] = jnp.dot(p.astype(BF16), v_ref[:, sl], preferred_element_type=F32).astype(BF16)

    row = pl.BlockSpec((tq, W), lambda i: (i, 0))
    full = pl.BlockSpec((M, W), lambda i: (0, 0))
    return pl.pallas_call(body, name="mem_attn_fwd", grid=(S // tq,), in_specs=[row, full, full], out_specs=row,
                          out_shape=jax.ShapeDtypeStruct((S, W), BF16), compiler_params=_cparams(("parallel",)))(qm, km, vm)


def _mem_attn_bwd(qm, km, vm, dom, tq=512):
    S, W = qm.shape
    M = km.shape[0]

    def body(q_ref, k_ref, v_ref, do_ref, dq_ref, dk_ref, dv_ref):
        i = pl.program_id(0)

        @pl.when(i == 0)
        def _():
            dk_ref[...] = jnp.zeros(dk_ref.shape, F32)
            dv_ref[...] = jnp.zeros(dv_ref.shape, F32)

        for h in range(MEM_HEADS):
            sl = slice(h * MEM_HEAD_DIM, (h + 1) * MEM_HEAD_DIM)
            q, k, v, do = q_ref[:, sl], k_ref[:, sl], v_ref[:, sl], do_ref[:, sl]
            p = _mem_probs(q, k)
            dv_ref[:, sl] += lax.dot_general(p.astype(BF16), do, (((0,), (0,)), ((), ())), preferred_element_type=F32)
            dp = lax.dot_general(do, v, (((1,), (1,)), ((), ())), preferred_element_type=F32)
            ds = (p * (dp - jnp.sum(dp * p, axis=-1, keepdims=True)) * MEM_SCALE).astype(BF16)
            dq_ref[:, sl] = jnp.dot(ds, k, preferred_element_type=F32).astype(BF16)
            dk_ref[:, sl] += lax.dot_general(ds, q, (((0,), (0,)), ((), ())), preferred_element_type=F32)

    row = pl.BlockSpec((tq, W), lambda i: (i, 0))
    full = pl.BlockSpec((M, W), lambda i: (0, 0))
    return pl.pallas_call(
        body, name="mem_attn_bwd", grid=(S // tq,), in_specs=[row, full, full, row], out_specs=[row, full, full],
        out_shape=[jax.ShapeDtypeStruct((S, W), BF16), jax.ShapeDtypeStruct((M, W), F32), jax.ShapeDtypeStruct((M, W), F32)],
        compiler_params=_cparams(("arbitrary",)),
    )(qm, km, vm, dom)


L = SSD_CHUNK
N_SPAIR = SSD_HEADS // 2
GRP_W = SSD_INNER // 2
XB0, XC0 = SSD_INNER, SSD_INNER + 2 * SSD_STATE


def _cumsum_rows(a, reverse=False):
    row = lax.broadcasted_iota(jnp.int32, a.shape, 0)
    x, sft = a, 1
    while sft < L:
        if reverse:
            x = x + jnp.where(row < L - sft, pltpu.roll(x, L - sft, axis=0), 0.0)
        else:
            x = x + jnp.where(row >= sft, pltpu.roll(x, sft, axis=0), 0.0)
        sft *= 2
    return x


def _shift_down(cur, prev, s):
    if s == 0:
        return cur
    row = lax.broadcasted_iota(jnp.int32, cur.shape, 0)
    return jnp.where(row < s, pltpu.roll(prev, s, axis=0), pltpu.roll(cur, s, axis=0))


def _shift_up(cur, nxt, s):
    if s == 0:
        return cur
    row = lax.broadcasted_iota(jnp.int32, cur.shape, 0)
    return jnp.where(row >= L - s, pltpu.roll(nxt, L - s, axis=0), pltpu.roll(cur, L - s, axis=0))


def _ssd_conv(u, prev, cw, cb):
    conv = cb + cw[SSD_CONV - 1:SSD_CONV, :] * u
    for s in range(1, SSD_CONV):
        conv = conv + cw[SSD_CONV - 1 - s:SSD_CONV - s, :] * _shift_down(u, prev, s)
    return conv


def _pair_lanes(v, h0, first):
    return jnp.where(first, v[:, h0:h0 + 1], v[:, h0 + 1:h0 + 2])


def _ssd_common(u, prev, dt_raw, cw, cb, dtb, alog):
    conv = _ssd_conv(u, prev, cw, cb)
    sg = _sigmoid(conv)
    xa = conv * sg
    dpre = dt_raw + dtb
    dtv = jnp.maximum(dpre, 0.0) + jnp.log1p(jnp.exp(-jnp.abs(dpre)))
    a_row = -jnp.exp(alog)
    cs = _cumsum_rows(dtv * a_row)
    return conv, sg, xa, dpre, dtv, a_row, cs


def _ssd_pair_fwd(xa, dtv, cs, csT, G, Cg, Bg, Sp, dsk, pp, first, tri, rowfirst):
    h0 = 2 * pp
    x = xa[:, pp * LANES:(pp + 1) * LANES]
    xdt = x * _pair_lanes(dtv, h0, first)
    xdt_b = xdt.astype(BF16)
    Ms, yd = [], []
    for h in (h0, h0 + 1):
        lam = jnp.exp(jnp.where(tri, cs[:, h:h + 1] - csT[h:h + 1, :], NEG))
        M = G * lam
        Ms.append((M, lam))
        yd.append(jnp.dot(M.astype(BF16), xdt_b, preferred_element_type=F32))
    T = lax.dot_general(Cg, Sp.astype(BF16), (((1,), (1,)), ((), ())), preferred_element_type=F32)
    E = jnp.exp(_pair_lanes(cs, h0, first))
    yoff = E * T
    csl = cs[L - 1:L, :]
    Fd = jnp.exp(_pair_lanes(csl, h0, first) - _pair_lanes(cs, h0, first))
    el = jnp.exp(csl)
    el_rows = jnp.where(rowfirst, el[:, h0:h0 + 1], el[:, h0 + 1:h0 + 2])
    Sloc = lax.dot_general((xdt * Fd).astype(BF16), Bg, (((0,), (0,)), ((), ())), preferred_element_type=F32)
    S_new = el_rows * Sp + Sloc
    y = jnp.where(first, yd[0], yd[1]) + yoff + x * _pair_lanes(dsk, h0, first[:1])
    return y, S_new, (x, xdt, xdt_b, Ms, E, yoff, Fd, el, el_rows)


def _ssd_masks():
    lane = lax.broadcasted_iota(jnp.int32, (L, LANES), 1)
    row = lax.broadcasted_iota(jnp.int32, (L, LANES), 0)
    return lane, row, lane < SSD_HEAD_DIM, row >= lane, row[:, :1] < SSD_HEAD_DIM


def _ssd_specs(nc, rev):
    def cidx(i):
        return nc - 1 - i if rev else i
    z = pl.BlockSpec((L, SSD_INNER), lambda i: (cidx(i), PZ // SSD_INNER))
    u = pl.BlockSpec((L, SSD_XBC), lambda i: (cidx(i), PX // SSD_XBC))
    dt = pl.BlockSpec((L, LANES), lambda i: (cidx(i), PDT // LANES))
    return cidx, z, u, dt


def _vec(w):
    return pl.BlockSpec((1, w), lambda i: (0, 0))


def _ssd_fwd(proj, cw, cb, dtb, alog, dsk, ng):
    S = proj.shape[0]
    nc = S // L

    def body(z_ref, u_ref, dt_ref, cw_ref, cb_ref, dtb_ref, alog_ref, dsk_ref, ng_ref, y_ref, st_ref, prev_s, state_s):
        c = pl.program_id(0)

        @pl.when(c == 0)
        def _():
            prev_s[...] = jnp.zeros(prev_s.shape, F32)
            state_s[...] = jnp.zeros(state_s.shape, F32)

        u = u_ref[...]
        _, _, xa, _, dtv, _, cs = _ssd_common(u, prev_s[...], dt_ref[...], cw_ref[...], cb_ref[...], dtb_ref[...], alog_ref[...])
        prev_s[...] = u
        csT = cs.T
        _, _, first, tri, rowfirst = _ssd_masks()
        dsk_v = dsk_ref[...]
        ys = []
        for g in range(2):
            Bg = xa[:, XB0 + g * SSD_STATE:XB0 + (g + 1) * SSD_STATE].astype(BF16)
            Cg = xa[:, XC0 + g * SSD_STATE:XC0 + (g + 1) * SSD_STATE].astype(BF16)
            G = lax.dot_general(Cg, Bg, (((1,), (1,)), ((), ())), preferred_element_type=F32)
            for pp in (2 * g, 2 * g + 1):
                Sp = state_s[pp]
                st_ref[pp * LANES:(pp + 1) * LANES, :] = Sp
                y, S_new, _ = _ssd_pair_fwd(xa, dtv, cs, csT, G, Cg, Bg, Sp, dsk_v, pp, first, tri, rowfirst)
                state_s[pp] = S_new
                ys.append(y)
        z = z_ref[...]
        for g in range(2):
            yg = jnp.concatenate([ys[2 * g], ys[2 * g + 1]], axis=1)
            zg = z[:, g * GRP_W:(g + 1) * GRP_W]
            gated = yg * (zg * _sigmoid(zg))
            r = lax.rsqrt(jnp.mean(gated * gated, axis=-1, keepdims=True) + RMS_EPS)
            y_ref[:, g * GRP_W:(g + 1) * GRP_W] = gated * r * ng_ref[:, g * GRP_W:(g + 1) * GRP_W]

    _, zs, us, dts = _ssd_specs(nc, False)
    return pl.pallas_call(
        body, name="ssd_fwd", grid=(nc,),
        in_specs=[zs, us, dts, pl.BlockSpec((8, SSD_XBC), lambda i: (0, 0)), _vec(SSD_XBC), _vec(LANES), _vec(LANES), _vec(LANES),
                  _vec(SSD_INNER)],
        out_specs=[pl.BlockSpec((L, SSD_INNER), lambda i: (i, 0)), pl.BlockSpec((N_SPAIR * LANES, SSD_STATE), lambda i: (i, 0))],
        out_shape=[jax.ShapeDtypeStruct((S, SSD_INNER), F32), jax.ShapeDtypeStruct((nc * N_SPAIR * LANES, SSD_STATE), F32)],
        scratch_shapes=[pltpu.VMEM((L, SSD_XBC), F32), pltpu.VMEM((N_SPAIR, LANES, SSD_STATE), F32)],
        compiler_params=_cparams(("arbitrary",)),
    )(proj, proj, proj, cw, cb, dtb, alog, dsk, ng)


def _ssd_bwd(proj, states, dy, cw, cb, dtb, alog, dsk, ng):
    S = proj.shape[0]
    nc = S // L

    def body(z_ref, u_ref, up_ref, dt_ref, st_ref, dy_ref, cw_ref, cb_ref, dtb_ref, alog_ref, dsk_ref, ng_ref,
             dz_ref, du_ref, ddt_ref, dcw_ref, dcb_ref, ddtb_ref, dalog_ref, ddsk_ref, dng_ref,
             dS_s, dconv_s, dD_s):
        i = pl.program_id(0)
        c = nc - 1 - i

        @pl.when(i == 0)
        def _():
            dS_s[...] = jnp.zeros(dS_s.shape, F32)
            dconv_s[...] = jnp.zeros(dconv_s.shape, F32)
            dD_s[...] = jnp.zeros(dD_s.shape, F32)
            for r in (dcw_ref, dcb_ref, ddtb_ref, dalog_ref, ddsk_ref, dng_ref):
                r[...] = jnp.zeros(r.shape, F32)

        u = u_ref[...]
        prev = jnp.where(c > 0, up_ref[...], 0.0)
        cw_v = cw_ref[...]
        conv, sg, xa, dpre, dtv, a_row, cs = _ssd_common(u, prev, dt_ref[...], cw_v, cb_ref[...], dtb_ref[...], alog_ref[...])
        csT = cs.T
        lane, row, first, tri, rowfirst = _ssd_masks()
        dsk_v = dsk_ref[...]

        fw = []
        Gs, Bs, Cs = [], [], []
        for g in range(2):
            Bg = xa[:, XB0 + g * SSD_STATE:XB0 + (g + 1) * SSD_STATE].astype(BF16)
            Cg = xa[:, XC0 + g * SSD_STATE:XC0 + (g + 1) * SSD_STATE].astype(BF16)
            G = lax.dot_general(Cg, Bg, (((1,), (1,)), ((), ())), preferred_element_type=F32)
            Gs.append(G), Bs.append(Bg), Cs.append(Cg)
            for pp in (2 * g, 2 * g + 1):
                Sp = st_ref[pp * LANES:(pp + 1) * LANES, :]
                y, _, keep = _ssd_pair_fwd(xa, dtv, cs, csT, G, Cg, Bg, Sp, dsk_v, pp, first, tri, rowfirst)
                fw.append((y, Sp, keep))

        z = z_ref[...]
        dys = []
        for g in range(2):
            sl = slice(g * GRP_W, (g + 1) * GRP_W)
            yg = jnp.concatenate([fw[2 * g][0], fw[2 * g + 1][0]], axis=1)
            zg = z[:, sl]
            sz = _sigmoid(zg)
            silu_z = zg * sz
            gated = yg * silu_z
            r = lax.rsqrt(jnp.mean(gated * gated, axis=-1, keepdims=True) + RMS_EPS)
            nh = gated * r
            dout = dy_ref[:, sl]
            dng_ref[:, sl] += jnp.sum(dout * nh, axis=0, keepdims=True)
            dnh = dout * ng_ref[:, sl]
            dgated = r * (dnh - nh * jnp.mean(dnh * nh, axis=-1, keepdims=True))
            dz_ref[:, sl] = dgated * yg * (sz * (1.0 + zg * (1.0 - sz)))
            dyg = dgated * silu_z
            dys.append(dyg[:, :LANES]), dys.append(dyg[:, LANES:])

        dcs_c = jnp.zeros((L, LANES), F32)
        dcs_r = jnp.zeros((L, LANES), F32)
        ddt_c = jnp.zeros((L, LANES), F32)
        dxs = []
        dB, dC = [None, None], [None, None]
        last = row == L - 1
        for g in range(2):
            Bg, Cg, G = Bs[g], Cs[g], Gs[g]
            dG = jnp.zeros((L, L), F32)
            dBg = jnp.zeros((L, SSD_STATE), F32)
            dCg = jnp.zeros((L, SSD_STATE), F32)
            for pp in (2 * g, 2 * g + 1):
                h0 = 2 * pp
                y, Sp, (x, xdt, xdt_b, Ms, E, yoff, Fd, el, el_rows) = fw[pp]
                dY = dys[pp]
                dS = dS_s[pp]
                dS_b, Sp_b = dS.astype(BF16), Sp.astype(BF16)
                dD_s[:, pp * LANES:(pp + 1) * LANES] += jnp.sum(dY * x, axis=0, keepdims=True)
                dx = dY * _pair_lanes(dsk_v, h0, first[:1])
                dxdt = jnp.zeros((L, LANES), F32)
                dY_b = dY.astype(BF16)
                for hh, h in enumerate((h0, h0 + 1)):
                    hm = first if hh == 0 else ~first
                    M, lam = Ms[hh]
                    dYh = jnp.where(hm, dY, 0.0).astype(BF16)
                    dM = lax.dot_general(dYh, xdt_b, (((1,), (1,)), ((), ())), preferred_element_type=F32)
                    W = dM * M
                    dcs_c = dcs_c + jnp.where(lane == h, jnp.sum(W, axis=-1, keepdims=True), 0.0)
                    dcs_r = dcs_r + jnp.where(row == h, jnp.sum(W, axis=0, keepdims=True), 0.0)
                    dG = dG + dM * lam
                    mt = lax.dot_general(M.astype(BF16), dY_b, (((0,), (0,)), ((), ())), preferred_element_type=F32)
                    dxdt = dxdt + jnp.where(hm, mt, 0.0)
                dT = (E * dY).astype(BF16)
                dCg = dCg + jnp.dot(dT, Sp_b, preferred_element_type=F32)
                dS_in = lax.dot_general(dT, Cg, (((0,), (0,)), ((), ())), preferred_element_type=F32) + el_rows * dS
                q1 = dY * yoff
                dZ = lax.dot_general(Bg, dS_b, (((1,), (1,)), ((), ())), preferred_element_type=F32)
                dBg = dBg + jnp.dot((xdt * Fd).astype(BF16), dS_b, preferred_element_type=F32)
                dxdt = dxdt + dZ * Fd
                q2 = dZ * xdt * Fd
                dSS = dS * Sp
                for hh, h in enumerate((h0, h0 + 1)):
                    hm = first if hh == 0 else ~first
                    rs1 = jnp.sum(jnp.where(hm, q1, 0.0), axis=-1, keepdims=True)
                    rs2 = jnp.sum(jnp.where(hm, q2, 0.0), axis=-1, keepdims=True)
                    rmask = rowfirst if hh == 0 else ~rowfirst
                    d_el = jnp.sum(jnp.sum(jnp.where(rmask, dSS, 0.0), axis=-1, keepdims=True), axis=0, keepdims=True)
                    tail = jnp.sum(rs2, axis=0, keepdims=True) + d_el * el[:, h:h + 1]
                    dcs_c = dcs_c + jnp.where(lane == h, rs1 - rs2 + jnp.where(last[:, :1], tail, 0.0), 0.0)
                    ddt_c = ddt_c + jnp.where(lane == h, jnp.sum(jnp.where(hm, dxdt * x, 0.0), axis=-1, keepdims=True), 0.0)
                dS_s[pp] = dS_in
                dxs.append(dx + dxdt * _pair_lanes(dtv, h0, first))
            dG_b = dG.astype(BF16)
            dC[g] = dCg + jnp.dot(dG_b, Bg, preferred_element_type=F32)
            dB[g] = dBg + lax.dot_general(dG_b, Cg, (((0,), (0,)), ((), ())), preferred_element_type=F32)

        dcs = dcs_c - dcs_r.T
        da = _cumsum_rows(dcs, reverse=True)
        ddt_c = ddt_c + da * a_row
        dalog_ref[...] += jnp.sum(da * dtv, axis=0, keepdims=True) * a_row
        ddt_raw = ddt_c * _sigmoid(dpre)
        ddt_ref[...] = ddt_raw
        ddtb_ref[...] += jnp.sum(ddt_raw, axis=0, keepdims=True)

        dxa = jnp.concatenate(dxs + dB + dC, axis=1)
        dconv = dxa * (sg * (1.0 + conv * (1.0 - sg)))
        dcb_ref[...] += jnp.sum(dconv, axis=0, keepdims=True)
        nxt = dconv_s[...]
        du = cw_v[SSD_CONV - 1:SSD_CONV, :] * dconv
        dcw_ref[SSD_CONV - 1:SSD_CONV, :] += jnp.sum(dconv * u, axis=0, keepdims=True)
        for s in range(1, SSD_CONV):
            k = SSD_CONV - 1 - s
            du = du + cw_v[k:k + 1, :] * _shift_up(dconv, nxt, s)
            dcw_ref[k:k + 1, :] += jnp.sum(dconv * _shift_down(u, prev, s), axis=0, keepdims=True)
        du_ref[...] = du
        dconv_s[...] = dconv

        @pl.when(i == nc - 1)
        def _():
            acc = dD_s[...]
            lane1 = lax.broadcasted_iota(jnp.int32, (1, LANES), 1)
            lanew = lax.broadcasted_iota(jnp.int32, acc.shape, 1)
            out = jnp.zeros((1, LANES), F32)
            for h in range(SSD_HEADS):
                tot = jnp.sum(jnp.where((lanew >= h * SSD_HEAD_DIM) & (lanew < (h + 1) * SSD_HEAD_DIM), acc, 0.0),
                              axis=-1, keepdims=True)
                out = out + jnp.where(lane1 == h, tot, 0.0)
            ddsk_ref[...] = out

    cidx, zs, us, dts = _ssd_specs(nc, True)
    ups = pl.BlockSpec((L, SSD_XBC), lambda i: (jnp.maximum(cidx(i) - 1, 0), PX // SSD_XBC))
    rowc = lambda w: pl.BlockSpec((L, w), lambda i: (cidx(i), 0))
    return pl.pallas_call(
        body, name="ssd_bwd", grid=(nc,),
        in_specs=[zs, us, ups, dts, pl.BlockSpec((N_SPAIR * LANES, SSD_STATE), lambda i: (cidx(i), 0)), rowc(SSD_INNER),
                  pl.BlockSpec((8, SSD_XBC), lambda i: (0, 0)), _vec(SSD_XBC), _vec(LANES), _vec(LANES), _vec(LANES), _vec(SSD_INNER)],
        out_specs=[rowc(SSD_INNER), rowc(SSD_XBC), rowc(LANES), pl.BlockSpec((8, SSD_XBC), lambda i: (0, 0)), _vec(SSD_XBC),
                   _vec(LANES), _vec(LANES), _vec(LANES), _vec(SSD_INNER)],
        out_shape=[jax.ShapeDtypeStruct((S, SSD_INNER), F32), jax.ShapeDtypeStruct((S, SSD_XBC), F32),
                   jax.ShapeDtypeStruct((S, LANES), F32), jax.ShapeDtypeStruct((8, SSD_XBC), F32),
                   jax.ShapeDtypeStruct((1, SSD_XBC), F32), jax.ShapeDtypeStruct((1, LANES), F32),
                   jax.ShapeDtypeStruct((1, LANES), F32), jax.ShapeDtypeStruct((1, LANES), F32),
                   jax.ShapeDtypeStruct((1, SSD_INNER), F32)],
        scratch_shapes=[pltpu.VMEM((N_SPAIR, LANES, SSD_STATE), F32), pltpu.VMEM((L, SSD_XBC), F32),
                        pltpu.VMEM((1, SSD_INNER), F32)],
        compiler_params=_cparams(("arbitrary",)),
    )(proj, proj, proj, proj, states, dy, cw, cb, dtb, alog, dsk, ng)


_IN_SEGS = ((PZ, 0, 512), (PX, 512, 1024), (PDT, 1536, 8), (PQ, 1544, 384), (PKV, 1928, 256), (PKR + KR_LANE, 2184, 32))


def _pad_w_in(w):
    parts, at = [], 0
    for dst, src, n in sorted(_IN_SEGS):
        parts += [jnp.zeros((w.shape[0], dst - at), w.dtype), w[:, src:src + n]]
        at = dst + n
    return jnp.concatenate(parts + [jnp.zeros((w.shape[0], PW - at), w.dtype)], axis=1)


def _unpad_w_in(wp):
    segs = sorted(_IN_SEGS, key=lambda t: t[1])
    return jnp.concatenate([wp[:, dst:dst + n] for dst, src, n in segs], axis=1)


def _pad_w_q(w):
    return jnp.pad(w.reshape(MLA_Q_RANK, MLA_HEADS, MLA_QK), ((0, 0), (0, 0), (0, LANES - MLA_QK))).reshape(MLA_Q_RANK, MLA_HEADS * LANES)


def _unpad_w_q(wp):
    return wp.reshape(MLA_Q_RANK, MLA_HEADS, LANES)[:, :, :MLA_QK].reshape(MLA_Q_RANK, MLA_HEADS * MLA_QK)


def _pad_w_kv(w):
    w3 = w.reshape(MLA_KV_RANK, MLA_HEADS, MLA_NOPE + MLA_V)
    k = jnp.pad(w3[:, :, :MLA_NOPE], ((0, 0), (0, 0), (0, LANES - MLA_NOPE))).reshape(MLA_KV_RANK, MLA_HEADS * LANES)
    return jnp.concatenate([k, w3[:, :, MLA_NOPE:].reshape(MLA_KV_RANK, MLA_HEADS * MLA_V)], axis=1)


def _unpad_w_kv(wp):
    k = wp[:, :MLA_HEADS * LANES].reshape(MLA_KV_RANK, MLA_HEADS, LANES)[:, :, :MLA_NOPE]
    v = wp[:, MLA_HEADS * LANES:].reshape(MLA_KV_RANK, MLA_HEADS, MLA_V)
    return jnp.concatenate([k, v], axis=2).reshape(MLA_KV_RANK, MLA_HEADS * (MLA_NOPE + MLA_V))


def _head_lanes(v):
    return jnp.pad(v, ((0, 0), (0, LANES - v.shape[1])))


def _local_step(x, mem, positions, tgt, P, weights_at=None, emit=None):
    tabs = _rope_tables(positions)
    G = {}
    emit = emit or (lambda names, grads: 0.0)

    h0, h0b = _ln_fwd([(x, 1.0)], P["ln_in_g"], P["ln_in_b"], "ln_in")
    proj = _mm(h0b, P["w_in"], "nn", "proj_in", tm=1024, tn=640)
    if weights_at is not None:
        P = {**P, **weights_at("mid", proj)}
    y_ssd, states = _ssd_fwd(proj, P["conv_w"], P["conv_b"], P["dt_bias"], P["a_log"], P["d_skip"], P["ssd_norm_g"])
    qn, kvn, kpe = _mla_prep(proj, P["q_norm_g"], P["kv_norm_g"], tabs)

    def q_epi(acc, c, s1, s2):
        return (jnp.concatenate([_rope_block(acc[:, h * LANES:(h + 1) * LANES], c, s1, s2) for h in range(MLA_HEADS)], axis=1),)

    q_all = _mm(qn, P["w_q_up"], "nn", "q_up", out_dtypes=(BF16,), epi=q_epi, extras=[(t, "m") for t in tabs])

    def kv_epi(acc, kp):
        kb = [acc[:, h * LANES:(h + 1) * LANES] + kp for h in range(MLA_HEADS)]
        return (jnp.concatenate(kb + [acc[:, MLA_HEADS * LANES:]], axis=1),)

    kv_all = _mm(kvn, P["w_kv_up"], "nn", "kv_up", out_dtypes=(BF16,), epi=kv_epi, extras=[(kpe, "m")])
    o_att, lse = _attn_fwd(q_all, kv_all)
    cat = jnp.concatenate([y_ssd, o_att], axis=1).astype(BF16)
    mix = _mm(cat, P["w_mix_out"], "nn", "mix_out", tm=1024)
    h1, h1b = _ln_fwd([(h0, ALPHA), (mix, 1.0)], P["ln1_g"], P["ln1_b"], "ln1")
    if weights_at is not None:
        P = {**P, **weights_at("late", h1b)}
    qm = _mm(h1b, P["w_mem_q"], "nn", "mem_q", tm=1024, out_dtypes=(BF16,))
    km = _mm(mem, P["w_mem_k"], "nn", "mem_k", out_dtypes=(BF16,))
    vm = _mm(mem, P["w_mem_v"], "nn", "mem_v", out_dtypes=(BF16,))
    om = _mem_attn_fwd(qm, km, vm)
    xa = _mm(om, P["w_mem_o"], "nn", "mem_o", tm=1024)
    h2, h2b = _ln_fwd([(h1, ALPHA), (xa, 1.0)], P["ln2_g"], P["ln2_b"], "ln2")

    def up_epi(acc):
        r = jnp.maximum(acc, 0.0)
        return (r * r,)

    act = _mm(h2b, P["w_up"], "nn", "mlp_up", tm=1024, tn=1024, out_dtypes=(BF16,), epi=up_epi)
    ff = _mm(act, P["w_down"], "nn", "mlp_down", tm=1024, tk=1024)

    loss, dr3, dr3b, G["ln3_g"], G["ln3_b"] = _ln_loss_bwd([(h2, ALPHA), (ff, 1.0)], P["ln3_g"], P["ln3_b"], tgt, "ln3_loss")

    def dact_epi(acc, a):
        return (acc * (2.0 * jnp.sqrt(a.astype(F32))),)

    du = _mm(dr3b, P["w_down"], "nt", "mlp_down_dx", tm=1024, tn=1024, out_dtypes=(BF16,), epi=dact_epi, extras=[(act, "mn")])
    G["w_down"] = _mm(act, dr3b, "tn", "mlp_down_dw", tm=1024, tk=1024, out_dtypes=(BF16,))
    G["w_up"] = _mm(h2b, du, "tn", "mlp_up_dw", tm=1024, tn=D_FF // N_DEV, tk=1024, out_dtypes=(BF16,), col_slots=True)
    tie = emit(("w_down", "w_up"), G)
    dh2_ff = _mm(du, P["w_up"], "nt", "mlp_up_dx", tm=1024, tk=1024)
    dr2, dr2b, G["ln2_g"], G["ln2_b"] = _ln_bwd([(h1, ALPHA), (xa, 1.0)], [(dr3, ALPHA), (dh2_ff, 1.0)], P["ln2_g"] + tie, "ln2_bwd")

    dom = _mm(dr2b, P["w_mem_o"], "nt", "mem_o_dx", tm=1024, out_dtypes=(BF16,))
    G["w_mem_o"] = _mm(om, dr2b, "tn", "mem_o_dw", tm=1024, tk=1024, out_dtypes=(BF16,))
    dqm, dkm, dvm = _mem_attn_bwd(qm, km, vm, dom)
    G["w_mem_q"] = _mm(h1b, dqm, "tn", "mem_q_dw", tm=1024, tk=1024, out_dtypes=(BF16,))
    G["w_mem_k"] = _mm(mem, dkm, "tn", "mem_k_dw", tm=1024, out_dtypes=(BF16,))
    G["w_mem_v"] = _mm(mem, dvm, "tn", "mem_v_dw", tm=1024, out_dtypes=(BF16,))
    tie = emit(("w_mem_o", "w_mem_q", "w_mem_k", "w_mem_v"), G)
    dh1_q = _mm(dqm, P["w_mem_q"], "nt", "mem_q_dx", tm=1024)
    dr1, dr1b, G["ln1_g"], G["ln1_b"] = _ln_bwd([(h0, ALPHA), (mix, 1.0)], [(dr2, ALPHA), (dh1_q, 1.0)], P["ln1_g"] + tie, "ln1_bwd")

    dcat = _mm(dr1b, P["w_mix_out"], "nt", "mix_out_dx", tm=1024)
    G["w_mix_out"] = _mm(cat, dr1b, "tn", "mix_out_dw", tm=1024, tk=1024, out_dtypes=(BF16,))
    dq_all, dk_all, dv_all = _attn_bwd(q_all, kv_all, dcat, lse, _attn_delta(dcat, o_att))
    dq_pre = _rope_bwd_all(dq_all, tabs)
    G["w_q_up"] = _mm(qn, dq_pre, "tn", "q_up_dw", tk=1024, out_dtypes=(BF16,))
    dqn = _mm(dq_pre, P["w_q_up"], "nt", "q_up_dx", tm=1024)
    dkv_all = jnp.concatenate([dk_all, dv_all], axis=1).astype(BF16)
    G["w_kv_up"] = _mm(kvn, dkv_all, "tn", "kv_up_dw", tk=1024, out_dtypes=(BF16,))
    dkvn = _mm(dkv_all, P["w_kv_up"], "nt", "kv_up_dx", tm=1024)
    dql, dkvl, dkr, G["q_norm_g"], G["kv_norm_g"] = _mla_prep_bwd(proj, P["q_norm_g"], P["kv_norm_g"], tabs, dqn, dkvn, dk_all)
    (dz, dxbc, ddt, G["conv_w"], G["conv_b"], G["dt_bias"], G["a_log"], G["d_skip"], G["ssd_norm_g"]) = _ssd_bwd(
        proj, states, dcat, P["conv_w"], P["conv_b"], P["dt_bias"], P["a_log"], P["d_skip"], P["ssd_norm_g"])
    tie = emit(("w_mix_out", "w_q_up", "w_kv_up", "conv_w"), G)
    S = x.shape[0]
    dproj = jnp.concatenate([dql, jnp.zeros((S, PZ - MLA_Q_RANK), F32) + tie, dz, dxbc, dkvl, ddt, dkr], axis=1).astype(BF16)
    G["w_in"] = _mm(h0b, dproj, "tn", "proj_in_dw", tm=1024, tn=640, tk=1024, out_dtypes=(BF16,))
    tie = emit(("w_in",), G)
    dh0_p = _mm(dproj, P["w_in"], "nt", "proj_in_dx", tm=1024, tk=640, epi=lambda acc, t: (acc + t,),
                extras=[(jnp.zeros((1, D_MODEL), F32) + tie, "n")])
    gx, _, G["ln_in_g"], G["ln_in_b"] = _ln_bwd([(x, 1.0)], [(dr1, ALPHA), (dh0_p, 1.0)], P["ln_in_g"] + tie, "ln_in_bwd")
    return loss, gx, G


PACK_W = 1024
BIG = (("w_in", (1024, 277), 1), ("conv_w", (4, 128), 1), ("w_q_up", (384, 96), 1), ("w_kv_up", (256, 128), 1),
       ("w_mix_out", (128, 1024), 0), ("w_mem_q", (128, 1024), 0), ("w_mem_k", (128, 1024), 0), ("w_mem_v", (128, 1024), 0),
       ("w_mem_o", (128, 1024), 0), ("w_up", (1024, 512), 1), ("w_down", (512, 1024), 0))
SMALL = ("ln_in_g", "ln_in_b", "conv_b", "ln1_g", "ln1_b", "ln2_g", "ln2_b", "ln3_g", "ln3_b",
         "ssd_norm_g", "q_norm_g", "kv_norm_g", "dt_bias", "a_log", "d_skip")
ALL_W = ("ln_in_g", "ln_in_b", "w_in", "conv_w", "conv_b", "dt_bias", "a_log", "d_skip", "ssd_norm_g", "q_norm_g", "w_q_up",
         "kv_norm_g", "w_kv_up", "w_mix_out", "ln1_g", "ln1_b", "w_mem_q", "w_mem_k", "w_mem_v", "w_mem_o", "ln2_g", "ln2_b",
         "w_up", "w_down", "ln3_g", "ln3_b")


def _rows_of(shape):
    return -(-(shape[0] * shape[1]) // PACK_W)


BIG_ROWS = sum(_rows_of(s) for _, s, _ in BIG)
BIG_R = -(-BIG_ROWS // 32) * 32
SMALL_R = 16
LOSS_ROW = 15
ADAM_TILE = 96
assert BIG_R % ADAM_TILE == 0


def _flat_rows(a, rows):
    lead = a.shape[:-2]
    f = a.reshape(lead + (-1,))
    f = jnp.pad(f, [(0, 0)] * len(lead) + [(0, rows * PACK_W - f.shape[-1])])
    return f.reshape(lead + (rows, PACK_W))


def _pack_big(arrs, dtype):
    parts = []
    for n, s, _ in BIG:
        if n == "conv_w" and dtype == BF16:
            parts.append(lax.bitcast_convert_type(arrs[n].astype(F32), BF16).reshape(1, PACK_W))
        else:
            parts.append(_flat_rows(arrs[n].astype(dtype), _rows_of(s)))
    parts.append(jnp.zeros((BIG_R - BIG_ROWS, PACK_W), dtype))
    return jnp.concatenate(parts, axis=0)


def _unpack_big(buf):
    out, r = {}, 0
    for n, s, _ in BIG:
        k = _rows_of(s)
        out[n] = buf[r:r + k].reshape(-1)[:s[0] * s[1]].reshape(s)
        r += k
    return out


def _unpack_gathered(buf):
    out, r = {}, 0
    for n, s, ax in BIG:
        k = _rows_of(s)
        if n == "conv_w":
            sh = lax.bitcast_convert_type(buf[:, r:r + k].reshape((N_DEV,) + s + (2,)), F32)
        else:
            sh = buf[:, r:r + k].reshape(N_DEV, -1)[:, :s[0] * s[1]].reshape((N_DEV,) + s)
        out[n] = sh.reshape(N_DEV * s[0], s[1]) if ax == 0 else sh.transpose(1, 0, 2).reshape(s[0], N_DEV * s[1])
        r += k
    return out


def _pack_grads(G):
    parts = []
    for n, s, ax in BIG:
        g = G[n]
        sh = g.reshape((N_DEV,) + s) if ax == 0 else g.reshape(s[0], N_DEV, s[1]).transpose(1, 0, 2)
        parts.append(_flat_rows(sh, _rows_of(s)))
    parts.append(jnp.zeros((N_DEV, BIG_R - BIG_ROWS, PACK_W), F32))
    return jnp.concatenate(parts, axis=1)


_SMALL_ROWS = (("ln_in_g",), ("ln_in_b",), ("conv_b",), ("ln1_g",), ("ln1_b",), ("ln2_g",), ("ln2_b",), ("ln3_g",), ("ln3_b",),
               ("ssd_norm_g", "q_norm_g"), ("kv_norm_g", "dt_bias", "a_log", "d_skip"))
_SMALL_W = {"ssd_norm_g": 512, "q_norm_g": 384, "kv_norm_g": 256, "dt_bias": LANES, "a_log": LANES, "d_skip": LANES}


def _pack_small(V, extra_row=None):
    rows = []
    for names in _SMALL_ROWS:
        r = jnp.concatenate([V[n] for n in names], axis=1)
        rows.append(jnp.pad(r, ((0, 0), (0, PACK_W - r.shape[1]))))
    rows.append(jnp.zeros((SMALL_R - len(rows) - 1, PACK_W), F32))
    rows.append(jnp.zeros((1, PACK_W), F32) if extra_row is None else extra_row)
    return jnp.concatenate(rows, axis=0)


def _unpack_small(buf):
    out = {}
    for i, names in enumerate(_SMALL_ROWS):
        c = 0
        for n in names:
            w = _SMALL_W.get(n, PACK_W)
            out[n] = buf[i:i + 1, c:c + w]
            c += w
    return out


MESH = pl.DeviceIdType.MESH
ANY = pl.BlockSpec(memory_space=pl.ANY)
VM = pl.BlockSpec(memory_space=pltpu.VMEM)


def _coords():
    return lax.axis_index("x"), lax.axis_index("y"), lax.axis_index("c")


def _slot(px, py, pc):
    return 4 * px + 2 * py + pc


def _gather_big(shard):
    R, W = shard.shape

    def body(x_ref, out_ref, send_sems, recv_sems, local_sem):
        x, y, c = _coords()
        me, sibling = (x, y, c), (x, y, 1 - c)
        chips = [(1 - x, y), (x, 1 - y), (1 - x, 1 - y)]

        def rows(px, py, pc):
            return out_ref.at[_slot(px, py, pc)]

        def copy(k, block, to, src=None):
            return pltpu.make_async_remote_copy(
                src_ref=rows(*block) if src is None else src, dst_ref=rows(*block),
                send_sem=send_sems.at[k], recv_sem=recv_sems.at[k], device_id=to, device_id_type=MESH)

        mine = pltpu.make_async_copy(x_ref, rows(*me), local_sem)
        mine.start()
        first = [copy(0, me, sibling, src=x_ref)]
        first += [copy(1 + j, me, (*chip, c), src=x_ref) for j, chip in enumerate(chips)]
        for cp in first:
            cp.start()
        passed = [copy(4 + j, (*chip, c), sibling) for j, chip in enumerate(chips)]
        for j, chip in enumerate(chips):
            copy(1 + j, (*chip, c), me).wait_recv()
            passed[j].start()
        copy(0, sibling, me).wait_recv()
        for j, chip in enumerate(chips):
            copy(4 + j, (*chip, 1 - c), me).wait_recv()
        for cp in first + passed:
            cp.wait_send()
        mine.wait()

    return pl.pallas_call(
        body, name="gather_weights", out_shape=jax.ShapeDtypeStruct((N_DEV, R, W), shard.dtype),
        in_specs=[ANY], out_specs=ANY,
        scratch_shapes=[pltpu.SemaphoreType.DMA((7,)), pltpu.SemaphoreType.DMA((7,)), pltpu.SemaphoreType.DMA],
    )(shard)


def _peer(k, x, y, c):
    dx, dy, dc = (k >> 2) & 1, (k >> 1) & 1, k & 1
    return (1 - x if dx else x, 1 - y if dy else y, 1 - c if dc else c)


def _scatter_grads(gpack):
    _, R, W = gpack.shape

    def body(g_ref, out_ref, send_sems, recv_sems, local_sem):
        x, y, c = _coords()
        my = _slot(x, y, c)
        mine = pltpu.make_async_copy(g_ref.at[my], out_ref.at[my], local_sem)
        mine.start()
        cps = []
        for k in range(1, N_DEV):
            peer = _peer(k, x, y, c)
            cps.append(pltpu.make_async_remote_copy(
                src_ref=g_ref.at[_slot(*peer)], dst_ref=out_ref.at[my],
                send_sem=send_sems.at[k - 1], recv_sem=recv_sems.at[k - 1], device_id=peer, device_id_type=MESH))
        for cp in cps:
            cp.start()
        for cp in cps:
            cp.wait_recv()
        for cp in cps:
            cp.wait_send()
        mine.wait()

    return pl.pallas_call(
        body, name="scatter_grads", out_shape=jax.ShapeDtypeStruct(gpack.shape, gpack.dtype),
        in_specs=[ANY], out_specs=ANY,
        scratch_shapes=[pltpu.SemaphoreType.DMA((7,)), pltpu.SemaphoreType.DMA((7,)), pltpu.SemaphoreType.DMA],
    )(gpack)


def _adam(w, g, m, v):
    m = ADAM_B1 * m + (1.0 - ADAM_B1) * g
    v = ADAM_B2 * v + (1.0 - ADAM_B2) * (g * g)
    m_hat = m / (1.0 - ADAM_B1 ** ADAM_STEP)
    v_hat = v / (1.0 - ADAM_B2 ** ADAM_STEP)
    delta = -ADAM_LR * (m_hat / (jnp.sqrt(v_hat) + ADAM_EPS) + ADAM_WD * w)
    return delta, m, v


def _sum_slots(ref):
    tot = ref[0].astype(F32)
    for q in range(1, N_DEV):
        tot = tot + ref[q].astype(F32)
    return tot


def _reduce_adam_big(recv, w, m, v):
    _, R, W = recv.shape

    def body(r_ref, w_ref, m_ref, v_ref, g_ref, d_ref, nm_ref, nv_ref):
        g = _sum_slots(r_ref)
        g_ref[...] = g
        d_ref[...], nm_ref[...], nv_ref[...] = _adam(w_ref[...], g, m_ref[...], v_ref[...])

    row = pl.BlockSpec((ADAM_TILE, W), lambda i: (i, 0))
    return pl.pallas_call(
        body, name="reduce_adam", grid=(R // ADAM_TILE,),
        in_specs=[pl.BlockSpec((N_DEV, ADAM_TILE, W), lambda i: (0, i, 0)), row, row, row], out_specs=[row] * 4,
        out_shape=[jax.ShapeDtypeStruct((R, W), F32)] * 4, compiler_params=_cparams(("parallel",)),
    )(recv, w, m, v)


def _allreduce_adam_small(gs, w, m, v):
    R, W = gs.shape

    def body(g_ref, w_ref, m_ref, v_ref, go_ref, d_ref, nm_ref, nv_ref, land, send_sems, recv_sems):
        x, y, c = _coords()
        my = _slot(x, y, c)
        cps = []
        for k in range(1, N_DEV):
            peer = _peer(k, x, y, c)
            cps.append(pltpu.make_async_remote_copy(
                src_ref=g_ref, dst_ref=land.at[my], send_sem=send_sems.at[k - 1], recv_sem=recv_sems.at[k - 1],
                device_id=peer, device_id_type=MESH))
        for cp in cps:
            cp.start()
        land[my] = g_ref[...]
        for cp in cps:
            cp.wait_recv()
        for cp in cps:
            cp.wait_send()
        g = _sum_slots(land)
        go_ref[...] = g
        d_ref[...], nm_ref[...], nv_ref[...] = _adam(w_ref[...], g, m_ref[...], v_ref[...])

    return pl.pallas_call(
        body, name="allreduce_small", in_specs=[VM] * 4, out_specs=[VM] * 4,
        out_shape=[jax.ShapeDtypeStruct((R, W), F32)] * 4,
        scratch_shapes=[pltpu.VMEM((N_DEV, R, W), F32), pltpu.SemaphoreType.DMA((7,)), pltpu.SemaphoreType.DMA((7,))],
    )(gs, w, m, v)


def _row(v):
    return v.reshape(1, -1).astype(F32)


def _step(x, mem, positions, tgt, W, M, V):
    shard = {n: W[n][0] for n, _, _ in BIG}
    full = _unpack_gathered(_gather_big(_pack_big(shard, BF16)))
    P = {n: _row(W[n]) for n in SMALL}
    for n in ("dt_bias", "a_log", "d_skip"):
        P[n] = _head_lanes(P[n])
    P.update(w_in=_pad_w_in(full["w_in"]), w_q_up=_pad_w_q(full["w_q_up"]), w_kv_up=_pad_w_kv(full["w_kv_up"]),
             conv_w=jnp.pad(full["conv_w"].astype(F32), ((0, 8 - SSD_CONV), (0, 0))))
    for n in ("w_mix_out", "w_mem_q", "w_mem_k", "w_mem_v", "w_mem_o", "w_up", "w_down"):
        P[n] = full[n]

    loss, gx, G = _local_step(x[0], mem[0], positions[0], tgt[0], P)

    G["w_in"], G["w_q_up"], G["w_kv_up"] = _unpad_w_in(G["w_in"]), _unpad_w_q(G["w_q_up"]), _unpad_w_kv(G["w_kv_up"])
    G["conv_w"] = G["conv_w"][:SSD_CONV]
    recv = _scatter_grads(_pack_grads(G))
    big = [_unpack_big(b) for b in _reduce_adam_big(
        recv, _pack_big(shard, F32), _pack_big({n: M[n][0] for n, _, _ in BIG}, F32), _pack_big({n: V[n][0] for n, _, _ in BIG}, F32))]

    def small_rows(D):
        rows = {n: _row(D[n]) for n in SMALL}
        for n in ("dt_bias", "a_log", "d_skip"):
            rows[n] = _head_lanes(rows[n])
        return rows

    loss_row = jnp.broadcast_to(loss[:, :1], (1, PACK_W))
    small_bufs = _allreduce_adam_small(
        _pack_small({n: G[n] for n in SMALL}, loss_row), _pack_small(small_rows(W)), _pack_small(small_rows(M)),
        _pack_small(small_rows(V)))
    loss_tot = small_bufs[0][LOSS_ROW, 0]
    small = [_unpack_small(b) for b in small_bufs]

    outs = []
    for res_b, res_s in zip(big, small):
        for n in ALL_W:
            if n in res_b:
                outs.append(res_b[n].reshape(W[n].shape))
            else:
                outs.append(res_s[n][:, :W[n].size].reshape(W[n].shape))
    return (loss_tot, gx[None], *outs)


def kernel(x, mem, positions, ln_in_g, ln_in_b, w_in, conv_w, conv_b, dt_bias, a_log, d_skip, ssd_norm_g, q_norm_g, w_q_up, kv_norm_g, w_kv_up, w_mix_out, ln1_g, ln1_b, w_mem_q, w_mem_k, w_mem_v, w_mem_o, ln2_g, ln2_b, w_up, w_down, ln3_g, ln3_b, loss_target, m_ln_in_g, m_ln_in_b, m_w_in, m_conv_w, m_conv_b, m_dt_bias, m_a_log, m_d_skip, m_ssd_norm_g, m_q_norm_g, m_w_q_up, m_kv_norm_g, m_w_kv_up, m_w_mix_out, m_ln1_g, m_ln1_b, m_w_mem_q, m_w_mem_k, m_w_mem_v, m_w_mem_o, m_ln2_g, m_ln2_b, m_w_up, m_w_down, m_ln3_g, m_ln3_b, v_ln_in_g, v_ln_in_b, v_w_in, v_conv_w, v_conv_b, v_dt_bias, v_a_log, v_d_skip, v_ssd_norm_g, v_q_norm_g, v_w_q_up, v_kv_norm_g, v_w_kv_up, v_w_mix_out, v_ln1_g, v_ln1_b, v_w_mem_q, v_w_mem_k, v_w_mem_v, v_w_mem_o, v_ln2_g, v_ln2_b, v_w_up, v_w_down, v_ln3_g, v_ln3_b):
    a = dict(locals())
    W = {n: a[n] for n in ALL_W}
    M = {n: a["m_" + n] for n in ALL_W}
    V = {n: a["v_" + n] for n in ALL_W}
    return _step_overlapped(x, mem, positions, loss_target, W, M, V)


HBM = pl.BlockSpec(memory_space=pltpu.HBM)
SEM = pl.BlockSpec(memory_space=pltpu.SEMAPHORE)
EFFECT = pltpu.SideEffectType.DATAFLOW_SIDE_EFFECTING
SHARD_SHAPE = {n: s for n, s, _ in BIG}
SHARD_AXIS = {n: ax for n, _, ax in BIG}
GATHER_NOW = ("w_in",)
GATHER_MID = ("conv_w", "w_q_up", "w_kv_up", "w_mix_out")
GATHER_LATE = ("w_mem_q", "w_mem_k", "w_mem_v", "w_mem_o", "w_up", "w_down")


def _my_slot():
    return _slot(*_coords())


def _group_copies(src_refs, land_refs, send_sems, recv_sems, slotted, landing_of_peer):
    x, y, c = _coords()
    my = _slot(x, y, c)
    cps = []
    for a, (s_ref, l_ref) in enumerate(zip(src_refs, land_refs)):
        for k in range(1, N_DEV):
            peer = _peer(k, x, y, c)
            cps.append(pltpu.make_async_remote_copy(
                src_ref=s_ref.at[_slot(*peer)] if slotted else s_ref,
                dst_ref=l_ref.at[_slot(*peer)] if landing_of_peer else l_ref.at[my],
                send_sem=send_sems.at[7 * a + k - 1], recv_sem=recv_sems.at[7 * a + k - 1],
                device_id=peer, device_id_type=MESH))
    return cps


def _send_start(srcs, lands, slotted, name):
    n = len(srcs)

    def body(*refs):
        for cp in _group_copies(refs[:n], refs[n:2 * n], refs[2 * n], refs[2 * n + 1], slotted, False):
            cp.start()
        refs[-1][...] = jnp.zeros(refs[-1].shape, F32)

    res = pl.pallas_call(
        body, name=name,
        out_shape=(pltpu.SemaphoreType.DMA((7 * n,)), pltpu.SemaphoreType.DMA((7 * n,)),
                   *[pltpu.HBM(a.shape, a.dtype) for a in srcs], *[pltpu.HBM(a.shape, a.dtype) for a in lands],
                   jax.ShapeDtypeStruct((8, LANES), F32)),
        in_specs=[HBM] * (2 * n), out_specs=(SEM, SEM, *[HBM] * (2 * n), VM),
        input_output_aliases={i: 2 + i for i in range(2 * n)},
        compiler_params=pltpu.CompilerParams(has_side_effects=EFFECT),
    )(*[pltpu.with_memory_space_constraint(a, pltpu.HBM) for a in list(srcs) + list(lands)])
    return (res[0], res[1], res[2:2 + n], res[2 + n:2 + 2 * n]), res[-1][:1, :1]


def _send_wait(started, after, slotted, name):
    send_sems, recv_sems, srcs, lands = started
    n = len(srcs)

    def body(*refs):
        for cp in _group_copies(refs[:n], refs[n:2 * n], refs[2 * n], refs[2 * n + 1], slotted, True):
            cp.wait_send()
            cp.wait_recv()

    res = pl.pallas_call(
        body, name=name, out_shape=tuple(pltpu.HBM(a.shape, a.dtype) for a in list(srcs) + list(lands)),
        in_specs=[HBM] * (2 * n) + [SEM, SEM, ANY], out_specs=tuple([HBM] * (2 * n)),
        input_output_aliases={i: i for i in range(2 * n)},
        compiler_params=pltpu.CompilerParams(has_side_effects=EFFECT),
    )(*srcs, *lands, send_sems, recv_sems, after)
    return res[n:]


def _landing(own, my):
    return lax.dynamic_update_slice(lax.empty((N_DEV,) + own.shape, own.dtype), own[None], (my,) + (0,) * own.ndim)


def _gather_now(shards):
    n = len(shards)

    def body(*refs):
        x_refs, out_refs = refs[:n], refs[n:2 * n]
        send_sems, recv_sems, local_sems = refs[2 * n:]
        x, y, c = _coords()
        me, sibling = (x, y, c), (x, y, 1 - c)
        chips = [(1 - x, y), (x, 1 - y), (1 - x, 1 - y)]

        def copy(a, k, block, to, src=None):
            rows = out_refs[a].at[_slot(*block)]
            return pltpu.make_async_remote_copy(
                src_ref=rows if src is None else src, dst_ref=rows,
                send_sem=send_sems.at[7 * a + k], recv_sem=recv_sems.at[7 * a + k], device_id=to, device_id_type=MESH)

        mine = [pltpu.make_async_copy(x_refs[a], out_refs[a].at[_slot(*me)], local_sems.at[a]) for a in range(n)]
        for cp in mine:
            cp.start()
        first = []
        for a in range(n):
            first.append(copy(a, 0, me, sibling, src=x_refs[a]))
            first += [copy(a, 1 + j, me, (*chip, c), src=x_refs[a]) for j, chip in enumerate(chips)]
        for cp in first:
            cp.start()
        passed = []
        for j, chip in enumerate(chips):
            for a in range(n):
                copy(a, 1 + j, (*chip, c), me).wait_recv()
                fwd = copy(a, 4 + j, (*chip, c), sibling)
                fwd.start()
                passed.append(fwd)
        for a in range(n):
            copy(a, 0, sibling, me).wait_recv()
            for j, chip in enumerate(chips):
                copy(a, 4 + j, (*chip, 1 - c), me).wait_recv()
        for cp in first + passed:
            cp.wait_send()
        for cp in mine:
            cp.wait()

    return pl.pallas_call(
        body, name="gather_now", out_shape=[jax.ShapeDtypeStruct((N_DEV,) + s.shape, s.dtype) for s in shards],
        in_specs=[ANY] * n, out_specs=[ANY] * n,
        scratch_shapes=[pltpu.SemaphoreType.DMA((7 * n,)), pltpu.SemaphoreType.DMA((7 * n,)), pltpu.SemaphoreType.DMA((n,))],
    )(*shards)


def _full_from_slots(name, slots):
    a, b = SHARD_SHAPE[name]
    return slots.reshape(N_DEV * a, b) if SHARD_AXIS[name] == 0 else slots.transpose(1, 0, 2).reshape(a, N_DEV * b)


def _slots_from_full(name, g):
    a, b = SHARD_SHAPE[name]
    return g.reshape(N_DEV, a, b) if SHARD_AXIS[name] == 0 else g.reshape(a, N_DEV, b).transpose(1, 0, 2)


def _reduce_adam(recv, w, m, v, name):
    _, a, b = recv.shape
    ta = a
    while ta * b * 4 * N_DEV > 4 * 1024 * 1024 and ta % 16 == 0:
        ta //= 2

    def body(r_ref, w_ref, m_ref, v_ref, g_ref, d_ref, nm_ref, nv_ref):
        g = _sum_slots(r_ref)
        g_ref[...] = g
        d_ref[...], nm_ref[...], nv_ref[...] = _adam(w_ref[...], g, m_ref[...], v_ref[...])

    row = pl.BlockSpec((ta, b), lambda i: (i, 0))
    return pl.pallas_call(
        body, name=name, grid=(a // ta,),
        in_specs=[pl.BlockSpec((N_DEV, ta, b), lambda i: (0, i, 0)), row, row, row], out_specs=[row] * 4,
        out_shape=[jax.ShapeDtypeStruct((a, b), F32)] * 4, compiler_params=_cparams(("parallel",)),
    )(recv, w, m, v)


def _step_overlapped(x, mem, positions, tgt, W, M, V):
    my = _my_slot()
    shard = {n: W[n][0] for n, _, _ in BIG}
    send = {n: (shard[n] if n == "conv_w" else shard[n].astype(BF16)) for n in shard}

    now = dict(zip(GATHER_NOW, _gather_now([send[n] for n in GATHER_NOW])))
    mid_src, late_src = [send[n] for n in GATHER_MID], [send[n] for n in GATHER_LATE]
    mid_src, _ = lax.optimization_barrier((mid_src, now["w_in"]))
    mid, tie = _send_start(mid_src, [_landing(s, my) for s in mid_src], False, "gather_mid_start")
    my_after_mid = my + tie[0, 0].astype(jnp.int32)
    late, tie = _send_start(late_src, [_landing(s, my_after_mid) for s in late_src], False, "gather_late_start")

    P = {n: _row(W[n]) for n in SMALL}
    for n in ("dt_bias", "a_log", "d_skip"):
        P[n] = _head_lanes(P[n])
    P["ln_in_g"] = P["ln_in_g"] + tie
    P["w_in"] = _pad_w_in(_full_from_slots("w_in", now["w_in"]))

    def weights_at(stage, after):
        if stage == "mid":
            lands = _send_wait(mid, after, False, "gather_mid_wait")
            full = {n: _full_from_slots(n, l) for n, l in zip(GATHER_MID, lands)}
            return dict(w_q_up=_pad_w_q(full["w_q_up"]), w_kv_up=_pad_w_kv(full["w_kv_up"]), w_mix_out=full["w_mix_out"],
                        conv_w=jnp.pad(full["conv_w"], ((0, 8 - SSD_CONV), (0, 0))))
        lands = _send_wait(late, after, False, "gather_late_wait")
        return {n: _full_from_slots(n, l) for n, l in zip(GATHER_LATE, lands)}

    started, res = [], {}

    def finish(i, after):
        names, st = started[i]
        lands = _send_wait(st, after, True, "scatter_wait_%d" % i)
        for n, recv in zip(names, lands):
            res[n] = _reduce_adam(recv, shard[n], M[n][0], V[n][0], "reduce_adam_" + n)
        return sum(res[n][0][:1, :1] for n in names) * 0.0

    def emit(names, G):
        srcs = []
        for n in names:
            g = G[n]
            if n == "w_in":
                g = _unpad_w_in(g)
            elif n == "w_q_up":
                g = _unpad_w_q(g)
            elif n == "w_kv_up":
                g = _unpad_w_kv(g)
            elif n == "conv_w":
                g = g[:SSD_CONV]
            srcs.append(g if g.ndim == 3 else _slots_from_full(n, g))
        lands = [_landing(lax.dynamic_index_in_dim(s, my, 0, keepdims=False), my) for s in srcs]
        st, tie = _send_start(srcs, lands, True, "scatter_start_%d" % len(started))
        started.append((names, st))
        if len(started) == 3:
            tie = tie + finish(0, srcs[0]) + finish(1, srcs[0])
        return tie

    loss, gx, G = _local_step(x[0], mem[0], positions[0], tgt[0], P, weights_at, emit)
    finish(2, gx)
    finish(3, gx)

    small, loss_tot = _allreduce_adam_vectors(
        {n: G[n] for n in SMALL}, loss, {n: _row(W[n]) for n in SMALL}, {n: _row(M[n]) for n in SMALL}, {n: _row(V[n]) for n in SMALL})

    outs = []
    for j in range(4):
        for n in ALL_W:
            outs.append((res[n][j] if n in res else small[j][n]).reshape(W[n].shape))
    return (loss_tot[0, 0], gx[None], *outs)


def _vector_places():
    places = {}
    for r, names in enumerate(_SMALL_ROWS):
        c = 0
        for n in names:
            w = _SMALL_W.get(n, PACK_W)
            places[n] = (r, c, w, SSD_HEADS if n in ("dt_bias", "a_log", "d_skip") else w)
            c += w
    return places


def _allreduce_adam_vectors(grads, loss, Ws, Ms, Vs):
    places = _vector_places()
    ns = len(SMALL)

    def body(*refs):
        g_in, loss_in = refs[:ns], refs[ns]
        w_in, m_in, v_in = refs[ns + 1:2 * ns + 1], refs[2 * ns + 1:3 * ns + 1], refs[3 * ns + 1:4 * ns + 1]
        o = 4 * ns + 1
        outs = [refs[o + j * ns:o + (j + 1) * ns] for j in range(4)]
        loss_out, stage, land, send_sems, recv_sems = refs[o + 4 * ns:]
        stage[...] = jnp.zeros(stage.shape, F32)
        for i, n in enumerate(SMALL):
            r, c, w, _ = places[n]
            stage[r:r + 1, c:c + w] = g_in[i][...]
        stage[LOSS_ROW:LOSS_ROW + 1, 0:LANES] = loss_in[...]
        x, y, c_ = _coords()
        my = _slot(x, y, c_)
        cps = []
        for k in range(1, N_DEV):
            peer = _peer(k, x, y, c_)
            cps.append(pltpu.make_async_remote_copy(
                src_ref=stage, dst_ref=land.at[my], send_sem=send_sems.at[k - 1], recv_sem=recv_sems.at[k - 1],
                device_id=peer, device_id_type=MESH))
        for cp in cps:
            cp.start()
        land[my] = stage[...]
        for cp in cps:
            cp.wait_recv()
        for cp in cps:
            cp.wait_send()
        tot = _sum_slots(land)
        loss_out[...] = tot[LOSS_ROW:LOSS_ROW + 1, 0:LANES]
        for i, n in enumerate(SMALL):
            r, c, _, wt = places[n]
            g = tot[r:r + 1, c:c + wt]
            outs[0][i][...] = g
            outs[1][i][...], outs[2][i][...], outs[3][i][...] = _adam(w_in[i][...], g, m_in[i][...], v_in[i][...])

    shapes = [jax.ShapeDtypeStruct((1, places[n][3]), F32) for n in SMALL]
    res = pl.pallas_call(
        body, name="allreduce_vectors", in_specs=[VM] * (4 * ns + 1), out_specs=[VM] * (4 * ns + 1),
        out_shape=shapes * 4 + [jax.ShapeDtypeStruct((1, LANES), F32)],
        scratch_shapes=[pltpu.VMEM((SMALL_R, PACK_W), F32), pltpu.VMEM((N_DEV, SMALL_R, PACK_W), F32),
                        pltpu.SemaphoreType.DMA((7,)), pltpu.SemaphoreType.DMA((7,))],
    )(*[grads[n] for n in SMALL], loss, *[Ws[n] for n in SMALL], *[Ms[n] for n in SMALL], *[Vs[n] for n in SMALL])
    return [dict(zip(SMALL, res[j * ns:(j + 1) * ns])) for j in range(4)], res[4 * ns]
```

```python
import functools
import math

import jax
import jax.numpy as jnp
from jax import lax
from jax.experimental import pallas as pl
from jax.experimental.pallas import tpu as pltpu

F32, BF16 = jnp.float32, jnp.bfloat16

N_DEV = 8
D_MODEL = 1024
SSD_HEADS, SSD_HEAD_DIM, SSD_INNER, SSD_STATE, SSD_CONV, SSD_CHUNK = 8, 64, 512, 128, 4, 128
SSD_XBC = 1024
MLA_HEADS, MLA_NOPE, MLA_ROPE, MLA_QK, MLA_V = 8, 64, 32, 96, 64
MLA_Q_RANK, MLA_KV_RANK = 384, 256
ROPE_THETA = 10000.0
MEM_HEADS, MEM_HEAD_DIM = 4, 256
D_FF = 4096
IN_WIDTH = 2216
LN_EPS, RMS_EPS = 1e-5, 1e-6
ALPHA = 2.0 ** 0.25
ADAM_LR, ADAM_B1, ADAM_B2, ADAM_EPS, ADAM_WD, ADAM_STEP = 0.001, 0.9, 0.999, 1e-08, 0.01, 10

LANES = 128
NEG = -1e30
VMEM_LIMIT = 56 * 1024 * 1024
TOK_K = 4096

PQ, PZ, PX, PKV, PDT, PKR, PW = 0, 512, 1024, 2048, 2304, 2432, 2560
KR_LANE = 64


def _cparams(sem):
    return pltpu.CompilerParams(dimension_semantics=sem, vmem_limit_bytes=VMEM_LIMIT)


def _sigmoid(x):
    return 1.0 / (1.0 + jnp.exp(-x))


def _mm(a, b, mode, name, *, tm=512, tn=None, tk=None, out_dtypes=(F32,), epi=None, extras=(), col_slots=False):
    if mode == "nn":
        (M, K), (K2, N) = a.shape, b.shape
    elif mode == "nt":
        (M, K), (N, K2) = a.shape, b.shape
    else:
        (K, M), (K2, N) = a.shape, b.shape
    assert K == K2, (name, a.shape, b.shape)
    tm, tn, tk = min(tm, M), min(tn or N, N), min(tk or K, K)
    assert M % tm == 0 and N % tn == 0 and K % tk == 0, (name, M, N, K, tm, tn, tk)
    gk = K // tk
    a_spec = pl.BlockSpec((tk, tm), lambda i, j, k: (k, i)) if mode == "tn" else pl.BlockSpec((tm, tk), lambda i, j, k: (i, k))
    b_spec = pl.BlockSpec((tn, tk), lambda i, j, k: (j, k)) if mode == "nt" else pl.BlockSpec((tk, tn), lambda i, j, k: (k, j))
    dims = {"nn": ((1,), (0,)), "nt": ((1,), (1,)), "tn": ((0,), (0,))}[mode]
    ex_specs = []
    for arr, kind in extras:
        if kind == "mn":
            ex_specs.append(pl.BlockSpec((tm, tn), lambda i, j, k: (i, j)))
        elif kind == "n":
            ex_specs.append(pl.BlockSpec((1, tn), lambda i, j, k: (0, j)))
        else:
            ex_specs.append(pl.BlockSpec((tm, arr.shape[1]), lambda i, j, k: (i, 0)))
    ne, no = len(extras), len(out_dtypes)

    def body(*refs):
        a_ref, b_ref = refs[0], refs[1]
        ex, outs = refs[2:2 + ne], refs[2 + ne:2 + ne + no]
        part = lax.dot_general(a_ref[...].astype(BF16), b_ref[...].astype(BF16), (dims, ((), ())),
                               preferred_element_type=F32)

        def finish(acc):
            res = epi(acc, *[e[...] for e in ex]) if epi is not None else (acc,)
            for o, r in zip(outs, res):
                o[...] = r.astype(o.dtype)

        if gk == 1:
            finish(part)
        else:
            acc_ref = refs[-1]
            k = pl.program_id(2)

            @pl.when(k == 0)
            def _():
                acc_ref[...] = part

            @pl.when(k > 0)
            def _():
                acc_ref[...] += part

            @pl.when(k == gk - 1)
            def _():
                finish(acc_ref[...])

    res = pl.pallas_call(
        body, name=name, grid=(M // tm, N // tn, gk),
        in_specs=[a_spec, b_spec] + ex_specs,
        out_specs=[pl.BlockSpec((None, tm, tn), lambda i, j, k: (j, i, 0)) if col_slots else pl.BlockSpec((tm, tn), lambda i, j, k: (i, j))
                   for _ in out_dtypes],
        out_shape=[jax.ShapeDtypeStruct((N // tn, M, tn) if col_slots else (M, N), dt) for dt in out_dtypes],
        scratch_shapes=[pltpu.VMEM((tm, tn), F32)] if gk > 1 else [],
        compiler_params=_cparams(("parallel", "parallel", "arbitrary")),
    )(a, b, *[e[0] for e in extras])
    return res[0] if no == 1 else res


def _ln_stats(r):
    mu = jnp.mean(r, axis=-1, keepdims=True)
    xc = r - mu
    var = jnp.mean(xc * xc, axis=-1, keepdims=True)
    rstd = lax.rsqrt(var + LN_EPS)
    return xc * rstd, rstd


def _ln_fwd(terms, g, b, name, tm=512):
    S, D = terms[0][0].shape
    coefs = [c for _, c in terms]
    nt = len(terms)

    def body(*refs):
        r = sum(c * t[...] for t, c in zip(refs[:nt], coefs))
        xh, _ = _ln_stats(r)
        h = xh * refs[nt][...] + refs[nt + 1][...]
        refs[nt + 2][...] = h
        refs[nt + 3][...] = h.astype(BF16)

    row = pl.BlockSpec((tm, D), lambda i: (i, 0))
    vec = pl.BlockSpec((1, D), lambda i: (0, 0))
    return pl.pallas_call(
        body, name=name, grid=(S // tm,), in_specs=[row] * nt + [vec, vec], out_specs=[row, row],
        out_shape=[jax.ShapeDtypeStruct((S, D), F32), jax.ShapeDtypeStruct((S, D), BF16)], compiler_params=_cparams(("parallel",)),
    )(*[t for t, _ in terms], g, b)


def _ln_bwd(terms, dterms, g, name, tm=512):
    S, D = terms[0][0].shape
    coefs, dcoefs = [c for _, c in terms], [c for _, c in dterms]
    nt, nd = len(terms), len(dterms)

    def body(*refs):
        i = pl.program_id(0)
        r = sum(c * t[...] for t, c in zip(refs[:nt], coefs))
        dh = sum(c * t[...].astype(F32) for t, c in zip(refs[nt:nt + nd], dcoefs))
        g_ref = refs[nt + nd]
        dr_ref, drb_ref, dg_ref, db_ref = refs[nt + nd + 1:]
        xh, rstd = _ln_stats(r)
        dxh = dh * g_ref[...]
        m1 = jnp.mean(dxh, axis=-1, keepdims=True)
        m2 = jnp.mean(dxh * xh, axis=-1, keepdims=True)
        dr = rstd * (dxh - m1 - xh * m2)
        dr_ref[...] = dr
        drb_ref[...] = dr.astype(BF16)
        pg = jnp.sum(dh * xh, axis=0, keepdims=True)
        pb = jnp.sum(dh, axis=0, keepdims=True)

        @pl.when(i == 0)
        def _():
            dg_ref[...] = pg
            db_ref[...] = pb

        @pl.when(i > 0)
        def _():
            dg_ref[...] += pg
            db_ref[...] += pb

    row = pl.BlockSpec((tm, D), lambda i: (i, 0))
    vec = pl.BlockSpec((1, D), lambda i: (0, 0))
    return pl.pallas_call(
        body, name=name, grid=(S // tm,), in_specs=[row] * (nt + nd) + [vec], out_specs=[row, row, vec, vec],
        out_shape=[jax.ShapeDtypeStruct((S, D), F32), jax.ShapeDtypeStruct((S, D), BF16), jax.ShapeDtypeStruct((1, D), F32),
                   jax.ShapeDtypeStruct((1, D), F32)],
        compiler_params=_cparams(("arbitrary",)),
    )(*[t for t, _ in terms], *[t for t, _ in dterms], g)


def _ln_loss_bwd(terms, g, b, tgt, name, tm=512):
    S, D = terms[0][0].shape
    coefs = [c for _, c in terms]
    nt = len(terms)

    def body(*refs):
        i = pl.program_id(0)
        r = sum(c * t[...] for t, c in zip(refs[:nt], coefs))
        g_ref, b_ref, t_ref = refs[nt:nt + 3]
        loss_ref, dr_ref, drb_ref, dg_ref, db_ref = refs[nt + 3:]
        xh, rstd = _ln_stats(r)
        h = xh * g_ref[...] + b_ref[...]
        diff = h - t_ref[...]
        pl_ = 0.5 * jnp.sum(jnp.mean(diff * diff, axis=-1, keepdims=True), axis=0, keepdims=True)
        dh = diff * (1.0 / D)
        dxh = dh * g_ref[...]
        m1 = jnp.mean(dxh, axis=-1, keepdims=True)
        m2 = jnp.mean(dxh * xh, axis=-1, keepdims=True)
        dr = rstd * (dxh - m1 - xh * m2)
        dr_ref[...] = dr
        drb_ref[...] = dr.astype(BF16)
        pg = jnp.sum(dh * xh, axis=0, keepdims=True)
        pb = jnp.sum(dh, axis=0, keepdims=True)
        plb = jnp.broadcast_to(pl_, (1, LANES))

        @pl.when(i == 0)
        def _():
            dg_ref[...] = pg
            db_ref[...] = pb
            loss_ref[...] = plb

        @pl.when(i > 0)
        def _():
            dg_ref[...] += pg
            db_ref[...] += pb
            loss_ref[...] += plb

    row = pl.BlockSpec((tm, D), lambda i: (i, 0))
    vec = pl.BlockSpec((1, D), lambda i: (0, 0))
    lvec = pl.BlockSpec((1, LANES), lambda i: (0, 0))
    return pl.pallas_call(
        body, name=name, grid=(S // tm,), in_specs=[row] * nt + [vec, vec, row], out_specs=[lvec, row, row, vec, vec],
        out_shape=[jax.ShapeDtypeStruct((1, LANES), F32), jax.ShapeDtypeStruct((S, D), F32), jax.ShapeDtypeStruct((S, D), BF16),
                   jax.ShapeDtypeStruct((1, D), F32), jax.ShapeDtypeStruct((1, D), F32)],
        compiler_params=_cparams(("arbitrary",)),
    )(*[t for t, _ in terms], g, b, tgt)


def _rope_tables(positions):
    half = MLA_ROPE // 2
    inv_freq = jnp.power(ROPE_THETA, -jnp.arange(half, dtype=F32) / half)
    ang = positions.astype(F32)[:, None] * inv_freq
    cos, sin = jnp.cos(ang), jnp.sin(ang)
    S = positions.shape[0]
    one, zero = jnp.ones((S, MLA_NOPE), F32), jnp.zeros((S, half), F32)
    pad = jnp.zeros((S, LANES - MLA_QK), F32)
    c = jnp.concatenate([one, cos, cos, pad], axis=1)
    s1 = jnp.concatenate([0 * one, -sin, zero, pad], axis=1)
    s2 = jnp.concatenate([0 * one, zero, sin, pad], axis=1)
    return c, s1, s2


def _rope_block(x, c, s1, s2):
    half = MLA_ROPE // 2
    return x * c + pltpu.roll(x, LANES - half, axis=1) * s1 + pltpu.roll(x, half, axis=1) * s2


def _rms_fwd(x, g):
    r = lax.rsqrt(jnp.mean(x * x, axis=-1, keepdims=True) + RMS_EPS)
    return x * r * g


def _rms_bwd(x, g, dy):
    r = lax.rsqrt(jnp.mean(x * x, axis=-1, keepdims=True) + RMS_EPS)
    xh = x * r
    dyh = dy * g
    dx = r * (dyh - xh * jnp.mean(dyh * xh, axis=-1, keepdims=True))
    return dx, jnp.sum(dy * xh, axis=0, keepdims=True)


def _mla_prep(proj, qg, kvg, tabs, tm=512):
    S = proj.shape[0]

    def body(ql_ref, kvl_ref, kr_ref, qg_ref, kvg_ref, c_ref, s1_ref, s2_ref, qn_ref, kvn_ref, kpe_ref):
        qn_ref[...] = _rms_fwd(ql_ref[...], qg_ref[...]).astype(BF16)
        kvn_ref[...] = _rms_fwd(kvl_ref[...], kvg_ref[...]).astype(BF16)
        kpe_ref[...] = _rope_block(kr_ref[...], c_ref[...], s1_ref[...], s2_ref[...])

    tab = pl.BlockSpec((tm, LANES), lambda i: (i, 0))
    return pl.pallas_call(
        body, name="mla_prep", grid=(S // tm,),
        in_specs=[pl.BlockSpec((tm, MLA_Q_RANK), lambda i: (i, PQ // MLA_Q_RANK)),
                  pl.BlockSpec((tm, MLA_KV_RANK), lambda i: (i, PKV // MLA_KV_RANK)),
                  pl.BlockSpec((tm, LANES), lambda i: (i, PKR // LANES)),
                  pl.BlockSpec((1, MLA_Q_RANK), lambda i: (0, 0)), pl.BlockSpec((1, MLA_KV_RANK), lambda i: (0, 0)),
                  tab, tab, tab],
        out_specs=[pl.BlockSpec((tm, MLA_Q_RANK), lambda i: (i, 0)), pl.BlockSpec((tm, MLA_KV_RANK), lambda i: (i, 0)), tab],
        out_shape=[jax.ShapeDtypeStruct((S, MLA_Q_RANK), BF16), jax.ShapeDtypeStruct((S, MLA_KV_RANK), BF16),
                   jax.ShapeDtypeStruct((S, LANES), F32)],
        compiler_params=_cparams(("parallel",)),
    )(proj, proj, proj, qg, kvg, *tabs)


def _mla_prep_bwd(proj, qg, kvg, tabs, dqn, dkvn, dk_all, tm=512):
    S = proj.shape[0]

    def body(ql_ref, kvl_ref, qg_ref, kvg_ref, c_ref, s1_ref, s2_ref, dqn_ref, dkvn_ref, dk_ref,
             dql_ref, dkvl_ref, dkr_ref, dqg_ref, dkvg_ref):
        i = pl.program_id(0)
        dql, pq = _rms_bwd(ql_ref[...], qg_ref[...], dqn_ref[...])
        dkvl, pkv = _rms_bwd(kvl_ref[...], kvg_ref[...], dkvn_ref[...])
        dql_ref[...] = dql
        dkvl_ref[...] = dkvl
        dk = dk_ref[...]
        dkpe = dk[:, 0:LANES]
        for h in range(1, MLA_HEADS):
            dkpe = dkpe + dk[:, h * LANES:(h + 1) * LANES]
        lane = lax.broadcasted_iota(jnp.int32, dkpe.shape, 1)
        dkpe = jnp.where((lane >= KR_LANE) & (lane < KR_LANE + MLA_ROPE), dkpe, 0.0)
        dkr_ref[...] = _rope_block(dkpe, c_ref[...], -s1_ref[...], -s2_ref[...])

        @pl.when(i == 0)
        def _():
            dqg_ref[...] = pq
            dkvg_ref[...] = pkv

        @pl.when(i > 0)
        def _():
            dqg_ref[...] += pq
            dkvg_ref[...] += pkv

    tab = pl.BlockSpec((tm, LANES), lambda i: (i, 0))
    qspec = pl.BlockSpec((tm, MLA_Q_RANK), lambda i: (i, 0))
    kvspec = pl.BlockSpec((tm, MLA_KV_RANK), lambda i: (i, 0))
    qv, kvv = pl.BlockSpec((1, MLA_Q_RANK), lambda i: (0, 0)), pl.BlockSpec((1, MLA_KV_RANK), lambda i: (0, 0))
    return pl.pallas_call(
        body, name="mla_prep_bwd", grid=(S // tm,),
        in_specs=[pl.BlockSpec((tm, MLA_Q_RANK), lambda i: (i, PQ // MLA_Q_RANK)),
                  pl.BlockSpec((tm, MLA_KV_RANK), lambda i: (i, PKV // MLA_KV_RANK)),
                  qv, kvv, tab, tab, tab, qspec, kvspec, pl.BlockSpec((tm, MLA_HEADS * LANES), lambda i: (i, 0))],
        out_specs=[qspec, kvspec, tab, qv, kvv],
        out_shape=[jax.ShapeDtypeStruct((S, MLA_Q_RANK), F32), jax.ShapeDtypeStruct((S, MLA_KV_RANK), F32),
                   jax.ShapeDtypeStruct((S, LANES), F32), jax.ShapeDtypeStruct((1, MLA_Q_RANK), F32),
                   jax.ShapeDtypeStruct((1, MLA_KV_RANK), F32)],
        compiler_params=_cparams(("arbitrary",)),
    )(proj, proj, qg, kvg, *tabs, dqn, dkvn, dk_all)


def _rope_bwd_all(dq_all, tabs, scale, tm=512):
    S, W = dq_all.shape

    def body(dq_ref, c_ref, s1_ref, s2_ref, o_ref):
        c, s1, s2 = scale * c_ref[...], -scale * s1_ref[...], -scale * s2_ref[...]
        for h in range(W // LANES):
            o_ref[:, h * LANES:(h + 1) * LANES] = _rope_block(dq_ref[:, h * LANES:(h + 1) * LANES], c, s1, s2).astype(BF16)

    tab = pl.BlockSpec((tm, LANES), lambda i: (i, 0))
    row = pl.BlockSpec((tm, W), lambda i: (i, 0))
    return pl.pallas_call(body, name="rope_bwd", grid=(S // tm,), in_specs=[row, tab, tab, tab], out_specs=row,
                          out_shape=jax.ShapeDtypeStruct((S, W), BF16), compiler_params=_cparams(("parallel",)))(dq_all, *tabs)


ATT_SCALE = MLA_QK ** -0.5
LN2 = math.log(2.0)
Q_PRESCALE = ATT_SCALE / LN2
DQ_POSTSCALE = ATT_SCALE / LN2
N_PAIR = MLA_HEADS // 2


def _causal_mask(qi, ki, tq, tk):
    row = qi * tq + lax.broadcasted_iota(jnp.int32, (tq, tk), 0)
    col = ki * tk + lax.broadcasted_iota(jnp.int32, (tq, tk), 1)
    return col <= row


def _lane_tile(x, n):
    return jnp.concatenate([x] * n, axis=1) if n > 1 else x


def _attn_fwd(q_all, kv_all, tq=512, tk=1024):
    S = q_all.shape[0]
    tq, tk = min(tq, S), min(tk, S)
    nq, nk, nb, r = S // tq, S // tk, tk // LANES, tk // tq

    def body(q_ref, k_ref, v_ref, o_ref, lse_ref, m_s, l_s, acc_s):
        qi, ki = pl.program_id(1), pl.program_id(2)
        last = lax.div(qi, r)

        @pl.when(ki == 0)
        def _():
            m_s[...] = jnp.full(m_s.shape, NEG, F32)
            l_s[...] = jnp.zeros(l_s.shape, F32)
            acc_s[...] = jnp.zeros(acc_s.shape, F32)

        def block(kc, mask):
            v = v_ref[0:kc, :]
            for hh in range(2):
                q = q_ref[:, hh * LANES:(hh + 1) * LANES]
                k = k_ref[0:kc, hh * LANES:(hh + 1) * LANES]
                s = lax.dot_general(q, k, (((1,), (1,)), ((), ())), preferred_element_type=F32)
                if mask is not None:
                    s = jnp.where(mask, s, NEG)
                m_prev = m_s[hh]
                m_new = jnp.maximum(m_prev, jnp.max(s, axis=-1, keepdims=True))
                p = jnp.exp2(s - _lane_tile(m_new, kc // LANES))
                alpha = jnp.exp2(m_prev - m_new)
                ps = p[:, :LANES]
                for j in range(1, kc // LANES):
                    ps = ps + p[:, j * LANES:(j + 1) * LANES]
                l_s[hh] = alpha * l_s[hh] + ps
                acc_s[hh] = alpha * acc_s[hh] + jnp.dot(p.astype(BF16), v, preferred_element_type=F32)
                m_s[hh] = m_new

        if r == 2:
            @pl.when(ki < last)
            def _():
                block(tk, None)

            @pl.when((ki == last) & (lax.rem(qi, 2) == 0))
            def _():
                block(tq, _causal_mask(0, 0, tq, tq))

            @pl.when((ki == last) & (lax.rem(qi, 2) == 1))
            def _():
                block(tk, _causal_mask(qi, ki, tq, tk))
        else:
            @pl.when(ki <= last)
            def _():
                block(tk, _causal_mask(qi, ki, tq, tk))

        @pl.when(ki == last)
        def _():
            first = lax.broadcasted_iota(jnp.int32, (tq, LANES), 1) < MLA_V
            l0 = jnp.sum(l_s[0], axis=-1, keepdims=True)
            l1 = jnp.sum(l_s[1], axis=-1, keepdims=True)
            o_ref[...] = jnp.where(first, acc_s[0] / l0, acc_s[1] / l1)
            lse_ref[:, :LANES] = m_s[0] + jnp.log2(l0)
            lse_ref[:, LANES:] = m_s[1] + jnp.log2(l1)

    return pl.pallas_call(
        body, name="mla_attn_fwd", grid=(N_PAIR, nq, nk),
        in_specs=[pl.BlockSpec((tq, 2 * LANES), lambda p, qi, ki: (qi, p)),
                  pl.BlockSpec((tk, 2 * LANES), lambda p, qi, ki: (jnp.minimum(ki, lax.div(qi, r)), p)),
                  pl.BlockSpec((tk, LANES), lambda p, qi, ki: (jnp.minimum(ki, lax.div(qi, r)), MLA_HEADS + p))],
        out_specs=[pl.BlockSpec((tq, LANES), lambda p, qi, ki: (qi, p)), pl.BlockSpec((tq, 2 * LANES), lambda p, qi, ki: (qi, p))],
        out_shape=[jax.ShapeDtypeStruct((S, MLA_HEADS * MLA_V), F32), jax.ShapeDtypeStruct((S, MLA_HEADS * LANES), F32)],
        scratch_shapes=[pltpu.VMEM((2, tq, LANES), F32), pltpu.VMEM((2, tq, LANES), F32), pltpu.VMEM((2, tq, LANES), F32)],
        compiler_params=_cparams(("parallel", "parallel", "arbitrary")),
    )(q_all, kv_all, kv_all)


def _attn_delta(dcat, o, tm=512):
    S = o.shape[0]

    def body(do_ref, o_ref, d_ref):
        prod = do_ref[...] * o_ref[...]
        first = lax.broadcasted_iota(jnp.int32, (tm, LANES), 1) < MLA_V
        for p in range(N_PAIR):
            pp = prod[:, p * LANES:(p + 1) * LANES]
            d0 = jnp.sum(jnp.where(first, pp, 0.0), axis=-1, keepdims=True)
            d1 = jnp.sum(jnp.where(first, 0.0, pp), axis=-1, keepdims=True)
            d_ref[:, 2 * p * LANES:(2 * p + 1) * LANES] = jnp.broadcast_to(d0, (tm, LANES))
            d_ref[:, (2 * p + 1) * LANES:(2 * p + 2) * LANES] = jnp.broadcast_to(d1, (tm, LANES))

    W = MLA_HEADS * MLA_V
    return pl.pallas_call(
        body, name="mla_attn_delta", grid=(S // tm,),
        in_specs=[pl.BlockSpec((tm, W), lambda i: (i, 1)), pl.BlockSpec((tm, W), lambda i: (i, 0))],
        out_specs=pl.BlockSpec((tm, MLA_HEADS * LANES), lambda i: (i, 0)),
        out_shape=jax.ShapeDtypeStruct((S, MLA_HEADS * LANES), F32), compiler_params=_cparams(("parallel",)),
    )(dcat, o)


def _attn_bwd(q_all, kv_all, dcat, lse, delta, tq=512, tk=512):
    S = q_all.shape[0]
    tq, tk = min(tq, S), min(tk, S)
    nq, nk, nb = S // tq, S // tk, tk // LANES
    assert tq == tk

    def body(q_ref, k_ref, v_ref, do_ref, lse_ref, dl_ref, dq_ref, dk_ref, dv_ref, dk_s, dv_s):
        ki, qi = pl.program_id(1), pl.program_id(2)

        @pl.when((ki == 0) & (qi == 0))
        def _():
            dq_ref[...] = jnp.zeros(dq_ref.shape, F32)

        @pl.when(qi == 0)
        def _():
            dk_s[...] = jnp.zeros(dk_s.shape, F32)
            dv_s[...] = jnp.zeros(dv_s.shape, F32)

        @pl.when(qi >= ki)
        def _():
            v, do = v_ref[...], do_ref[...]
            first = lax.broadcasted_iota(jnp.int32, (tq, LANES), 1) < MLA_V
            firstk = lax.broadcasted_iota(jnp.int32, (tk, LANES), 1) < MLA_V
            mask = _causal_mask(qi, ki, tq, tk)
            do_b = do.astype(BF16)
            rows = pl.ds(pl.multiple_of(qi * tq, tq), tq)
            for hh in range(2):
                sl = slice(hh * LANES, (hh + 1) * LANES)
                q, k = q_ref[:, sl], k_ref[:, sl]
                s = lax.dot_general(q, k, (((1,), (1,)), ((), ())), preferred_element_type=F32)
                p = jnp.exp2(jnp.where(mask, s, NEG) - _lane_tile(lse_ref[:, sl], nb))
                do_h = jnp.where(first if hh == 0 else ~first, do, 0.0).astype(BF16)
                dp = lax.dot_general(do_h, v, (((1,), (1,)), ((), ())), preferred_element_type=F32)
                ds_b = (p * (dp - _lane_tile(dl_ref[:, sl], nb)) * LN2).astype(BF16)
                pv = lax.dot_general(p.astype(BF16), do_b, (((0,), (0,)), ((), ())), preferred_element_type=F32)
                dv_s[...] += jnp.where(firstk if hh == 0 else ~firstk, pv, 0.0)
                dk_s[:, sl] += lax.dot_general(ds_b, q, (((0,), (0,)), ((), ())), preferred_element_type=F32)
                dq_ref[rows, sl] += jnp.dot(ds_b, k, preferred_element_type=F32)

        @pl.when(qi == nq - 1)
        def _():
            dk_ref[...] = dk_s[...]
            dv_ref[...] = dv_s[...]

    wide = pl.BlockSpec((tq, 2 * LANES), lambda p, ki, qi: (jnp.maximum(qi, ki), p))
    return pl.pallas_call(
        body, name="mla_attn_bwd", grid=(N_PAIR, nk, nq),
        in_specs=[wide, pl.BlockSpec((tk, 2 * LANES), lambda p, ki, qi: (ki, p)),
                  pl.BlockSpec((tk, LANES), lambda p, ki, qi: (ki, MLA_HEADS + p)),
                  pl.BlockSpec((tq, LANES), lambda p, ki, qi: (jnp.maximum(qi, ki), N_PAIR + p)), wide, wide],
        out_specs=[pl.BlockSpec((S, 2 * LANES), lambda p, ki, qi: (0, p)),
                   pl.BlockSpec((tk, 2 * LANES), lambda p, ki, qi: (ki, p)), pl.BlockSpec((tk, LANES), lambda p, ki, qi: (ki, p))],
        out_shape=[jax.ShapeDtypeStruct((S, MLA_HEADS * LANES), F32), jax.ShapeDtypeStruct((S, MLA_HEADS * LANES), F32),
                   jax.ShapeDtypeStruct((S, MLA_HEADS * MLA_V), F32)],
        scratch_shapes=[pltpu.VMEM((tk, 2 * LANES), F32), pltpu.VMEM((tk, LANES), F32)],
        compiler_params=_cparams(("parallel", "arbitrary", "arbitrary")),
    )(q_all, kv_all, kv_all, dcat, lse, delta)


def _attn_p_ds(q, k, v, do, lse_h, delta_h, half_mask, mask):
    s = lax.dot_general(q, k, (((1,), (1,)), ((), ())), preferred_element_type=F32) * ATT_SCALE
    p = jnp.exp(jnp.where(mask, s, NEG) - lse_h)
    do_h = jnp.where(half_mask, do, 0.0).astype(BF16)
    dp = lax.dot_general(do_h, v, (((1,), (1,)), ((), ())), preferred_element_type=F32)
    ds = p * (dp - delta_h) * ATT_SCALE
    return p, ds


def _attn_deltas(do, o, first):
    prod = do * o
    return (jnp.sum(jnp.where(first, prod, 0.0), axis=-1, keepdims=True),
            jnp.sum(jnp.where(first, 0.0, prod), axis=-1, keepdims=True))


def _attn_bwd_dq(q_all, kv_all, dcat, o, lse, tq=512, tk=512):
    S = q_all.shape[0]
    nq, nk = S // tq, S // tk

    def body(q_ref, k_ref, v_ref, do_ref, o_ref, lse_ref, dq_ref, acc_s):
        qi, ki = pl.program_id(1), pl.program_id(2)

        @pl.when(ki == 0)
        def _():
            acc_s[...] = jnp.zeros(acc_s.shape, F32)

        @pl.when(ki <= qi)
        def _():
            v, do, lse_t = v_ref[...], do_ref[...], lse_ref[...]
            first = lax.broadcasted_iota(jnp.int32, (tq, LANES), 1) < MLA_V
            deltas = _attn_deltas(do, o_ref[...], first)
            mask = _causal_mask(qi, ki, tq, tk)
            for hh in range(2):
                k = k_ref[:, hh * LANES:(hh + 1) * LANES]
                _, ds = _attn_p_ds(q_ref[:, hh * LANES:(hh + 1) * LANES], k, v, do,
                                   lse_t[:, hh * MLA_V:hh * MLA_V + 1], deltas[hh], first if hh == 0 else ~first, mask)
                acc_s[:, hh * LANES:(hh + 1) * LANES] += jnp.dot(ds.astype(BF16), k, preferred_element_type=F32)

        @pl.when(ki == qi)
        def _():
            dq_ref[...] = acc_s[...]

    half = pl.BlockSpec((tq, LANES), lambda p, qi, ki: (qi, p))
    return pl.pallas_call(
        body, name="mla_attn_bwd_dq", grid=(N_PAIR, nq, nk),
        in_specs=[pl.BlockSpec((tq, 2 * LANES), lambda p, qi, ki: (qi, p)),
                  pl.BlockSpec((tk, 2 * LANES), lambda p, qi, ki: (jnp.minimum(ki, qi), p)),
                  pl.BlockSpec((tk, LANES), lambda p, qi, ki: (jnp.minimum(ki, qi), MLA_HEADS + p)),
                  pl.BlockSpec((tq, LANES), lambda p, qi, ki: (qi, N_PAIR + p)), half, half],
        out_specs=pl.BlockSpec((tq, 2 * LANES), lambda p, qi, ki: (qi, p)),
        out_shape=jax.ShapeDtypeStruct((S, MLA_HEADS * LANES), F32),
        scratch_shapes=[pltpu.VMEM((tq, 2 * LANES), F32)],
        compiler_params=_cparams(("parallel", "parallel", "arbitrary")),
    )(q_all, kv_all, kv_all, dcat, o, lse)


def _attn_bwd_dkv(q_all, kv_all, dcat, o, lse, tq=512, tk=512):
    S = q_all.shape[0]
    nq, nk = S // tq, S // tk

    def body(q_ref, k_ref, v_ref, do_ref, o_ref, lse_ref, dk_ref, dv_ref, dk_s, dv_s):
        ki, qi = pl.program_id(1), pl.program_id(2)

        @pl.when(qi == 0)
        def _():
            dk_s[...] = jnp.zeros(dk_s.shape, F32)
            dv_s[...] = jnp.zeros(dv_s.shape, F32)

        @pl.when(qi >= ki)
        def _():
            v, do, lse_t = v_ref[...], do_ref[...], lse_ref[...]
            first = lax.broadcasted_iota(jnp.int32, (tq, LANES), 1) < MLA_V
            firstk = lax.broadcasted_iota(jnp.int32, (tk, LANES), 1) < MLA_V
            deltas = _attn_deltas(do, o_ref[...], first)
            mask = _causal_mask(qi, ki, tq, tk)
            do_b = do.astype(BF16)
            for hh in range(2):
                q = q_ref[:, hh * LANES:(hh + 1) * LANES]
                p, ds = _attn_p_ds(q, k_ref[:, hh * LANES:(hh + 1) * LANES], v, do,
                                   lse_t[:, hh * MLA_V:hh * MLA_V + 1], deltas[hh], first if hh == 0 else ~first, mask)
                pv = lax.dot_general(p.astype(BF16), do_b, (((0,), (0,)), ((), ())), preferred_element_type=F32)
                dv_s[...] += jnp.where(firstk if hh == 0 else ~firstk, pv, 0.0)
                dk_s[:, hh * LANES:(hh + 1) * LANES] += lax.dot_general(ds.astype(BF16), q, (((0,), (0,)), ((), ())),
                                                                       preferred_element_type=F32)

        @pl.when(qi == nq - 1)
        def _():
            dk_ref[...] = dk_s[...]
            dv_ref[...] = dv_s[...]

    half = pl.BlockSpec((tq, LANES), lambda p, ki, qi: (jnp.maximum(qi, ki), p))
    return pl.pallas_call(
        body, name="mla_attn_bwd_dkv", grid=(N_PAIR, nk, nq),
        in_specs=[pl.BlockSpec((tq, 2 * LANES), lambda p, ki, qi: (jnp.maximum(qi, ki), p)),
                  pl.BlockSpec((tk, 2 * LANES), lambda p, ki, qi: (ki, p)),
                  pl.BlockSpec((tk, LANES), lambda p, ki, qi: (ki, MLA_HEADS + p)),
                  pl.BlockSpec((tq, LANES), lambda p, ki, qi: (jnp.maximum(qi, ki), N_PAIR + p)), half, half],
        out_specs=[pl.BlockSpec((tk, 2 * LANES), lambda p, ki, qi: (ki, p)), pl.BlockSpec((tk, LANES), lambda p, ki, qi: (ki, p))],
        out_shape=[jax.ShapeDtypeStruct((S, MLA_HEADS * LANES), F32), jax.ShapeDtypeStruct((S, MLA_HEADS * MLA_V), F32)],
        scratch_shapes=[pltpu.VMEM((tk, 2 * LANES), F32), pltpu.VMEM((tk, LANES), F32)],
        compiler_params=_cparams(("parallel", "parallel", "arbitrary")),
    )(q_all, kv_all, kv_all, dcat, o, lse)


MEM_SCALE = MEM_HEAD_DIM ** -0.5


def _mem_probs(q, k):
    s = lax.dot_general(q, k, (((1,), (1,)), ((), ())), preferred_element_type=F32) * MEM_SCALE
    e = jnp.exp(s - jnp.max(s, axis=-1, keepdims=True))
    return e / jnp.sum(e, axis=-1, keepdims=True)


def _mem_attn_fwd(qm, km, vm, tq=512):
    S, W = qm.shape
    M = km.shape[0]

    def body(q_ref, k_ref, v_ref, o_ref):
        for h in range(MEM_HEADS):
            sl = slice(h * MEM_HEAD_DIM, (h + 1) * MEM_HEAD_DIM)
            p = _mem_probs(q_ref[:, sl], k_ref[:, sl])
            o_ref[:, sl] = jnp.dot(p.astype(BF16), v_ref[:, sl], preferred_element_type=F32).astype(BF16)

    row = pl.BlockSpec((tq, W), lambda i: (i, 0))
    full = pl.BlockSpec((M, W), lambda i: (0, 0))
    return pl.pallas_call(body, name="mem_attn_fwd", grid=(S // tq,), in_specs=[row, full, full], out_specs=row,
                          out_shape=jax.ShapeDtypeStruct((S, W), BF16), compiler_params=_cparams(("parallel",)))(qm, km, vm)


def _mem_attn_bwd(qm, km, vm, dom, tq=512):
    S, W = qm.shape
    M = km.shape[0]

    def body(q_ref, k_ref, v_ref, do_ref, dq_ref, dk_ref, dv_ref):
        i = pl.program_id(0)

        @pl.when(i == 0)
        def _():
            dk_ref[...] = jnp.zeros(dk_ref.shape, F32)
            dv_ref[...] = jnp.zeros(dv_ref.shape, F32)

        for h in range(MEM_HEADS):
            sl = slice(h * MEM_HEAD_DIM, (h + 1) * MEM_HEAD_DIM)
            q, k, v, do = q_ref[:, sl], k_ref[:, sl], v_ref[:, sl], do_ref[:, sl]
            p = _mem_probs(q, k)
            dv_ref[:, sl] += lax.dot_general(p.astype(BF16), do, (((0,), (0,)), ((), ())), preferred_element_type=F32)
            dp = lax.dot_general(do, v, (((1,), (1,)), ((), ())), preferred_element_type=F32)
            ds = (p * (dp - jnp.sum(dp * p, axis=-1, keepdims=True)) * MEM_SCALE).astype(BF16)
            dq_ref[:, sl] = jnp.dot(ds, k, preferred_element_type=F32).astype(BF16)
            dk_ref[:, sl] += lax.dot_general(ds, q, (((0,), (0,)), ((), ())), preferred_element_type=F32)

    row = pl.BlockSpec((tq, W), lambda i: (i, 0))
    full = pl.BlockSpec((M, W), lambda i: (0, 0))
    return pl.pallas_call(
        body, name="mem_attn_bwd", grid=(S // tq,), in_specs=[row, full, full, row], out_specs=[row, full, full],
        out_shape=[jax.ShapeDtypeStruct((S, W), BF16), jax.ShapeDtypeStruct((M, W), F32), jax.ShapeDtypeStruct((M, W), F32)],
        compiler_params=_cparams(("arbitrary",)),
    )(qm, km, vm, dom)


L = SSD_CHUNK
N_SPAIR = SSD_HEADS // 2
GRP_W = SSD_INNER // 2
XB0, XC0 = SSD_INNER, SSD_INNER + 2 * SSD_STATE


def _cumsum_rows(a, reverse=False):
    row = lax.broadcasted_iota(jnp.int32, a.shape, 0)
    x, sft = a, 1
    while sft < L:
        if reverse:
            x = x + jnp.where(row < L - sft, pltpu.roll(x, L - sft, axis=0), 0.0)
        else:
            x = x + jnp.where(row >= sft, pltpu.roll(x, sft, axis=0), 0.0)
        sft *= 2
    return x


def _shift_down(cur, prev, s):
    if s == 0:
        return cur
    row = lax.broadcasted_iota(jnp.int32, cur.shape, 0)
    return jnp.where(row < s, pltpu.roll(prev, s, axis=0), pltpu.roll(cur, s, axis=0))


def _shift_up(cur, nxt, s):
    if s == 0:
        return cur
    row = lax.broadcasted_iota(jnp.int32, cur.shape, 0)
    return jnp.where(row >= L - s, pltpu.roll(nxt, L - s, axis=0), pltpu.roll(cur, L - s, axis=0))


def _ssd_conv(u, prev, cw, cb):
    conv = cb + cw[SSD_CONV - 1:SSD_CONV, :] * u
    for s in range(1, SSD_CONV):
        conv = conv + cw[SSD_CONV - 1 - s:SSD_CONV - s, :] * _shift_down(u, prev, s)
    return conv


def _pair_lanes(v, h0, first):
    return jnp.where(first, v[:, h0:h0 + 1], v[:, h0 + 1:h0 + 2])


def _ssd_common(u, prev, dt_raw, cw, cb, dtb, alog):
    conv = _ssd_conv(u, prev, cw, cb)
    sg = _sigmoid(conv)
    xa = conv * sg
    dpre = dt_raw + dtb
    dtv = jnp.maximum(dpre, 0.0) + jnp.log1p(jnp.exp(-jnp.abs(dpre)))
    a_row = -jnp.exp(alog)
    cs = _cumsum_rows(dtv * a_row)
    return conv, sg, xa, dpre, dtv, a_row, cs


def _ssd_pair_fwd(xa, dtv, cs, csT, G, Cg, Bg, Sp, dsk, pp, first, tri, rowfirst):
    h0 = 2 * pp
    x = xa[:, pp * LANES:(pp + 1) * LANES]
    xdt = x * _pair_lanes(dtv, h0, first)
    xdt_b = xdt.astype(BF16)
    Ms, yd = [], []
    for h in (h0, h0 + 1):
        lam = jnp.exp(jnp.where(tri, cs[:, h:h + 1] - csT[h:h + 1, :], NEG))
        M = G * lam
        Ms.append((M, lam))
        yd.append(jnp.dot(M.astype(BF16), xdt_b, preferred_element_type=F32))
    T = lax.dot_general(Cg, Sp.astype(BF16), (((1,), (1,)), ((), ())), preferred_element_type=F32)
    E = jnp.exp(_pair_lanes(cs, h0, first))
    yoff = E * T
    csl = cs[L - 1:L, :]
    Fd = jnp.exp(_pair_lanes(csl, h0, first) - _pair_lanes(cs, h0, first))
    el = jnp.exp(csl)
    el_rows = jnp.where(rowfirst, el[:, h0:h0 + 1], el[:, h0 + 1:h0 + 2])
    Sloc = lax.dot_general((xdt * Fd).astype(BF16), Bg, (((0,), (0,)), ((), ())), preferred_element_type=F32)
    S_new = el_rows * Sp + Sloc
    y = jnp.where(first, yd[0], yd[1]) + yoff + x * _pair_lanes(dsk, h0, first[:1])
    return y, S_new, (x, xdt, xdt_b, Ms, E, yoff, Fd, el, el_rows)


def _ssd_masks():
    lane = lax.broadcasted_iota(jnp.int32, (L, LANES), 1)
    row = lax.broadcasted_iota(jnp.int32, (L, LANES), 0)
    return lane, row, lane < SSD_HEAD_DIM, row >= lane, row[:, :1] < SSD_HEAD_DIM


def _ssd_specs(nc, rev):
    def cidx(i):
        return nc - 1 - i if rev else i
    z = pl.BlockSpec((L, SSD_INNER), lambda i: (cidx(i), PZ // SSD_INNER))
    u = pl.BlockSpec((L, SSD_XBC), lambda i: (cidx(i), PX // SSD_XBC))
    dt = pl.BlockSpec((L, LANES), lambda i: (cidx(i), PDT // LANES))
    return cidx, z, u, dt


def _vec(w):
    return pl.BlockSpec((1, w), lambda i: (0, 0))


def _ssd_fwd(proj, cw, cb, dtb, alog, dsk, ng):
    S = proj.shape[0]
    nc = S // L

    def body(z_ref, u_ref, dt_ref, cw_ref, cb_ref, dtb_ref, alog_ref, dsk_ref, ng_ref, y_ref, st_ref, prev_s, state_s):
        c = pl.program_id(0)

        @pl.when(c == 0)
        def _():
            prev_s[...] = jnp.zeros(prev_s.shape, F32)
            state_s[...] = jnp.zeros(state_s.shape, F32)

        u = u_ref[...]
        _, _, xa, _, dtv, _, cs = _ssd_common(u, prev_s[...], dt_ref[...], cw_ref[...], cb_ref[...], dtb_ref[...], alog_ref[...])
        prev_s[...] = u
        csT = cs.T
        _, _, first, tri, rowfirst = _ssd_masks()
        dsk_v = dsk_ref[...]
        ys = []
        for g in range(2):
            Bg = xa[:, XB0 + g * SSD_STATE:XB0 + (g + 1) * SSD_STATE].astype(BF16)
            Cg = xa[:, XC0 + g * SSD_STATE:XC0 + (g + 1) * SSD_STATE].astype(BF16)
            G = lax.dot_general(Cg, Bg, (((1,), (1,)), ((), ())), preferred_element_type=F32)
            for pp in (2 * g, 2 * g + 1):
                Sp = state_s[pp]
                st_ref[pp * LANES:(pp + 1) * LANES, :] = Sp
                y, S_new, _ = _ssd_pair_fwd(xa, dtv, cs, csT, G, Cg, Bg, Sp, dsk_v, pp, first, tri, rowfirst)
                state_s[pp] = S_new
                ys.append(y)
        z = z_ref[...]
        for g in range(2):
            yg = jnp.concatenate([ys[2 * g], ys[2 * g + 1]], axis=1)
            zg = z[:, g * GRP_W:(g + 1) * GRP_W]
            gated = yg * (zg * _sigmoid(zg))
            r = lax.rsqrt(jnp.mean(gated * gated, axis=-1, keepdims=True) + RMS_EPS)
            y_ref[:, g * GRP_W:(g + 1) * GRP_W] = gated * r * ng_ref[:, g * GRP_W:(g + 1) * GRP_W]

    _, zs, us, dts = _ssd_specs(nc, False)
    return pl.pallas_call(
        body, name="ssd_fwd", grid=(nc,),
        in_specs=[zs, us, dts, pl.BlockSpec((8, SSD_XBC), lambda i: (0, 0)), _vec(SSD_XBC), _vec(LANES), _vec(LANES), _vec(LANES),
                  _vec(SSD_INNER)],
        out_specs=[pl.BlockSpec((L, SSD_INNER), lambda i: (i, 0)), pl.BlockSpec((N_SPAIR * LANES, SSD_STATE), lambda i: (i, 0))],
        out_shape=[jax.ShapeDtypeStruct((S, SSD_INNER), F32), jax.ShapeDtypeStruct((nc * N_SPAIR * LANES, SSD_STATE), F32)],
        scratch_shapes=[pltpu.VMEM((L, SSD_XBC), F32), pltpu.VMEM((N_SPAIR, LANES, SSD_STATE), F32)],
        compiler_params=_cparams(("arbitrary",)),
    )(proj, proj, proj, cw, cb, dtb, alog, dsk, ng)


def _ssd_bwd(proj, states, dy, cw, cb, dtb, alog, dsk, ng):
    S = proj.shape[0]
    nc = S // L

    def body(z_ref, u_ref, up_ref, dt_ref, st_ref, dy_ref, cw_ref, cb_ref, dtb_ref, alog_ref, dsk_ref, ng_ref,
             dz_ref, du_ref, ddt_ref, dcw_ref, dcb_ref, ddtb_ref, dalog_ref, ddsk_ref, dng_ref,
             dS_s, dconv_s, dD_s):
        i = pl.program_id(0)
        c = nc - 1 - i

        @pl.when(i == 0)
        def _():
            dS_s[...] = jnp.zeros(dS_s.shape, F32)
            dconv_s[...] = jnp.zeros(dconv_s.shape, F32)
            dD_s[...] = jnp.zeros(dD_s.shape, F32)
            for r in (dcw_ref, dcb_ref, ddtb_ref, dalog_ref, ddsk_ref, dng_ref):
                r[...] = jnp.zeros(r.shape, F32)

        u = u_ref[...]
        prev = jnp.where(c > 0, up_ref[...], 0.0)
        cw_v = cw_ref[...]
        conv, sg, xa, dpre, dtv, a_row, cs = _ssd_common(u, prev, dt_ref[...], cw_v, cb_ref[...], dtb_ref[...], alog_ref[...])
        csT = cs.T
        lane, row, first, tri, rowfirst = _ssd_masks()
        dsk_v = dsk_ref[...]

        fw = []
        Gs, Bs, Cs = [], [], []
        for g in range(2):
            Bg = xa[:, XB0 + g * SSD_STATE:XB0 + (g + 1) * SSD_STATE].astype(BF16)
            Cg = xa[:, XC0 + g * SSD_STATE:XC0 + (g + 1) * SSD_STATE].astype(BF16)
            G = lax.dot_general(Cg, Bg, (((1,), (1,)), ((), ())), preferred_element_type=F32)
            Gs.append(G), Bs.append(Bg), Cs.append(Cg)
            for pp in (2 * g, 2 * g + 1):
                Sp = st_ref[pp * LANES:(pp + 1) * LANES, :]
                y, _, keep = _ssd_pair_fwd(xa, dtv, cs, csT, G, Cg, Bg, Sp, dsk_v, pp, first, tri, rowfirst)
                fw.append((y, Sp, keep))

        z = z_ref[...]
        dys = []
        for g in range(2):
            sl = slice(g * GRP_W, (g + 1) * GRP_W)
            yg = jnp.concatenate([fw[2 * g][0], fw[2 * g + 1][0]], axis=1)
            zg = z[:, sl]
            sz = _sigmoid(zg)
            silu_z = zg * sz
            gated = yg * silu_z
            r = lax.rsqrt(jnp.mean(gated * gated, axis=-1, keepdims=True) + RMS_EPS)
            nh = gated * r
            dout = dy_ref[:, sl]
            dng_ref[:, sl] += jnp.sum(dout * nh, axis=0, keepdims=True)
            dnh = dout * ng_ref[:, sl]
            dgated = r * (dnh - nh * jnp.mean(dnh * nh, axis=-1, keepdims=True))
            dz_ref[:, sl] = dgated * yg * (sz * (1.0 + zg * (1.0 - sz)))
            dyg = dgated * silu_z
            dys.append(dyg[:, :LANES]), dys.append(dyg[:, LANES:])

        dcs_col = [0.0] * SSD_HEADS
        dcs_row = [None] * SSD_HEADS
        ddt_col = [None] * SSD_HEADS
        dxs = []
        dB, dC = [None, None], [None, None]
        last = row == L - 1
        for g in range(2):
            Bg, Cg, G = Bs[g], Cs[g], Gs[g]
            dG = jnp.zeros((L, L), F32)
            dBg = jnp.zeros((L, SSD_STATE), F32)
            dCg = jnp.zeros((L, SSD_STATE), F32)
            for pp in (2 * g, 2 * g + 1):
                h0 = 2 * pp
                y, Sp, (x, xdt, xdt_b, Ms, E, yoff, Fd, el, el_rows) = fw[pp]
                dY = dys[pp]
                dS = dS_s[pp]
                dS_b, Sp_b = dS.astype(BF16), Sp.astype(BF16)
                dD_s[:, pp * LANES:(pp + 1) * LANES] += jnp.sum(dY * x, axis=0, keepdims=True)
                dx = dY * _pair_lanes(dsk_v, h0, first[:1])
                dxdt = jnp.zeros((L, LANES), F32)
                dY_b = dY.astype(BF16)
                for hh, h in enumerate((h0, h0 + 1)):
                    hm = first if hh == 0 else ~first
                    M, lam = Ms[hh]
                    dYh = jnp.where(hm, dY, 0.0).astype(BF16)
                    dM = lax.dot_general(dYh, xdt_b, (((1,), (1,)), ((), ())), preferred_element_type=F32)
                    W = dM * M
                    dcs_col[h] = dcs_col[h] + jnp.sum(W, axis=-1, keepdims=True)
                    dcs_row[h] = jnp.sum(W, axis=0, keepdims=True)
                    dG = dG + dM * lam
                    mt = lax.dot_general(M.astype(BF16), dY_b, (((0,), (0,)), ((), ())), preferred_element_type=F32)
                    dxdt = dxdt + jnp.where(hm, mt, 0.0)
                dT = (E * dY).astype(BF16)
                dCg = dCg + jnp.dot(dT, Sp_b, preferred_element_type=F32)
                dS_in = lax.dot_general(dT, Cg, (((0,), (0,)), ((), ())), preferred_element_type=F32) + el_rows * dS
                q1 = dY * yoff
                dZ = lax.dot_general(Bg, dS_b, (((1,), (1,)), ((), ())), preferred_element_type=F32)
                dBg = dBg + jnp.dot((xdt * Fd).astype(BF16), dS_b, preferred_element_type=F32)
                dxdt = dxdt + dZ * Fd
                q2 = dZ * xdt * Fd
                dSS = dS * Sp
                for hh, h in enumerate((h0, h0 + 1)):
                    hm = first if hh == 0 else ~first
                    rs1 = jnp.sum(jnp.where(hm, q1, 0.0), axis=-1, keepdims=True)
                    rs2 = jnp.sum(jnp.where(hm, q2, 0.0), axis=-1, keepdims=True)
                    rmask = rowfirst if hh == 0 else ~rowfirst
                    d_el = jnp.sum(jnp.sum(jnp.where(rmask, dSS, 0.0), axis=-1, keepdims=True), axis=0, keepdims=True)
                    tail = jnp.sum(rs2, axis=0, keepdims=True) + d_el * el[:, h:h + 1]
                    dcs_col[h] = dcs_col[h] + (rs1 - rs2 + jnp.where(last[:, :1], tail, 0.0))
                    ddt_col[h] = jnp.sum(jnp.where(hm, dxdt * x, 0.0), axis=-1, keepdims=True)
                dS_s[pp] = dS_in
                dxs.append(dx + dxdt * _pair_lanes(dtv, h0, first))
            dG_b = dG.astype(BF16)
            dC[g] = dCg + jnp.dot(dG_b, Bg, preferred_element_type=F32)
            dB[g] = dBg + lax.dot_general(dG_b, Cg, (((0,), (0,)), ((), ())), preferred_element_type=F32)

        dcs_c, dcs_r, ddt_c = (jnp.zeros((L, LANES), F32) for _ in range(3))
        for h in range(SSD_HEADS):
            dcs_c = dcs_c + jnp.where(lane == h, dcs_col[h], 0.0)
            dcs_r = dcs_r + jnp.where(row == h, dcs_row[h], 0.0)
            ddt_c = ddt_c + jnp.where(lane == h, ddt_col[h], 0.0)
        dcs = dcs_c - dcs_r.T
        da = _cumsum_rows(dcs, reverse=True)
        ddt_c = ddt_c + da * a_row
        dalog_ref[...] += jnp.sum(da * dtv, axis=0, keepdims=True) * a_row
        ddt_raw = ddt_c * _sigmoid(dpre)
        ddt_ref[...] = ddt_raw
        ddtb_ref[...] += jnp.sum(ddt_raw, axis=0, keepdims=True)

        dxa = jnp.concatenate(dxs + dB + dC, axis=1)
        dconv = dxa * (sg * (1.0 + conv * (1.0 - sg)))
        dcb_ref[...] += jnp.sum(dconv, axis=0, keepdims=True)
        nxt = dconv_s[...]
        du = cw_v[SSD_CONV - 1:SSD_CONV, :] * dconv
        dcw_ref[SSD_CONV - 1:SSD_CONV, :] += jnp.sum(dconv * u, axis=0, keepdims=True)
        for s in range(1, SSD_CONV):
            k = SSD_CONV - 1 - s
            du = du + cw_v[k:k + 1, :] * _shift_up(dconv, nxt, s)
            dcw_ref[k:k + 1, :] += jnp.sum(dconv * _shift_down(u, prev, s), axis=0, keepdims=True)
        du_ref[...] = du
        dconv_s[...] = dconv

        @pl.when(i == nc - 1)
        def _():
            acc = dD_s[...]
            lane1 = lax.broadcasted_iota(jnp.int32, (1, LANES), 1)
            lanew = lax.broadcasted_iota(jnp.int32, acc.shape, 1)
            out = jnp.zeros((1, LANES), F32)
            for h in range(SSD_HEADS):
                tot = jnp.sum(jnp.where((lanew >= h * SSD_HEAD_DIM) & (lanew < (h + 1) * SSD_HEAD_DIM), acc, 0.0),
                              axis=-1, keepdims=True)
                out = out + jnp.where(lane1 == h, tot, 0.0)
            ddsk_ref[...] = out

    cidx, zs, us, dts = _ssd_specs(nc, True)
    ups = pl.BlockSpec((L, SSD_XBC), lambda i: (jnp.maximum(cidx(i) - 1, 0), PX // SSD_XBC))
    rowc = lambda w: pl.BlockSpec((L, w), lambda i: (cidx(i), 0))
    return pl.pallas_call(
        body, name="ssd_bwd", grid=(nc,),
        in_specs=[zs, us, ups, dts, pl.BlockSpec((N_SPAIR * LANES, SSD_STATE), lambda i: (cidx(i), 0)), rowc(SSD_INNER),
                  pl.BlockSpec((8, SSD_XBC), lambda i: (0, 0)), _vec(SSD_XBC), _vec(LANES), _vec(LANES), _vec(LANES), _vec(SSD_INNER)],
        out_specs=[rowc(SSD_INNER), rowc(SSD_XBC), rowc(LANES), pl.BlockSpec((8, SSD_XBC), lambda i: (0, 0)), _vec(SSD_XBC),
                   _vec(LANES), _vec(LANES), _vec(LANES), _vec(SSD_INNER)],
        out_shape=[jax.ShapeDtypeStruct((S, SSD_INNER), F32), jax.ShapeDtypeStruct((S, SSD_XBC), F32),
                   jax.ShapeDtypeStruct((S, LANES), F32), jax.ShapeDtypeStruct((8, SSD_XBC), F32),
                   jax.ShapeDtypeStruct((1, SSD_XBC), F32), jax.ShapeDtypeStruct((1, LANES), F32),
                   jax.ShapeDtypeStruct((1, LANES), F32), jax.ShapeDtypeStruct((1, LANES), F32),
                   jax.ShapeDtypeStruct((1, SSD_INNER), F32)],
        scratch_shapes=[pltpu.VMEM((N_SPAIR, LANES, SSD_STATE), F32), pltpu.VMEM((L, SSD_XBC), F32),
                        pltpu.VMEM((1, SSD_INNER), F32)],
        compiler_params=_cparams(("arbitrary",)),
    )(proj, proj, proj, proj, states, dy, cw, cb, dtb, alog, dsk, ng)


_IN_SEGS = ((PZ, 0, 512), (PX, 512, 1024), (PDT, 1536, 8), (PQ, 1544, 384), (PKV, 1928, 256), (PKR + KR_LANE, 2184, 32))


def _pad_w_in(w):
    parts, at = [], 0
    for dst, src, n in sorted(_IN_SEGS):
        parts += [jnp.zeros((w.shape[0], dst - at), w.dtype), w[:, src:src + n]]
        at = dst + n
    return jnp.concatenate(parts + [jnp.zeros((w.shape[0], PW - at), w.dtype)], axis=1)


def _unpad_w_in(wp):
    segs = sorted(_IN_SEGS, key=lambda t: t[1])
    return jnp.concatenate([wp[:, dst:dst + n] for dst, src, n in segs], axis=1)


def _pad_w_q(w):
    return jnp.pad(w.reshape(MLA_Q_RANK, MLA_HEADS, MLA_QK), ((0, 0), (0, 0), (0, LANES - MLA_QK))).reshape(MLA_Q_RANK, MLA_HEADS * LANES)


def _unpad_w_q(wp):
    return wp.reshape(MLA_Q_RANK, MLA_HEADS, LANES)[:, :, :MLA_QK].reshape(MLA_Q_RANK, MLA_HEADS * MLA_QK)


def _pad_w_kv(w):
    w3 = w.reshape(MLA_KV_RANK, MLA_HEADS, MLA_NOPE + MLA_V)
    k = jnp.pad(w3[:, :, :MLA_NOPE], ((0, 0), (0, 0), (0, LANES - MLA_NOPE))).reshape(MLA_KV_RANK, MLA_HEADS * LANES)
    return jnp.concatenate([k, w3[:, :, MLA_NOPE:].reshape(MLA_KV_RANK, MLA_HEADS * MLA_V)], axis=1)


def _unpad_w_kv(wp):
    k = wp[:, :MLA_HEADS * LANES].reshape(MLA_KV_RANK, MLA_HEADS, LANES)[:, :, :MLA_NOPE]
    v = wp[:, MLA_HEADS * LANES:].reshape(MLA_KV_RANK, MLA_HEADS, MLA_V)
    return jnp.concatenate([k, v], axis=2).reshape(MLA_KV_RANK, MLA_HEADS * (MLA_NOPE + MLA_V))


def _head_lanes(v):
    return jnp.pad(v, ((0, 0), (0, LANES - v.shape[1])))


def _local_step(x, mem, positions, tgt, P, weights_at=None, emit=None):
    tabs = _rope_tables(positions)
    G = {}
    emit = emit or (lambda names, grads: 0.0)

    h0, h0b = _ln_fwd([(x, 1.0)], P["ln_in_g"], P["ln_in_b"], "ln_in")
    proj = _mm(h0b, P["w_in"], "nn", "proj_in", tm=1024, tn=640)
    if weights_at is not None:
        P = {**P, **weights_at("mid", proj)}
    y_ssd, states = _ssd_fwd(proj, P["conv_w"], P["conv_b"], P["dt_bias"], P["a_log"], P["d_skip"], P["ssd_norm_g"])
    qn, kvn, kpe = _mla_prep(proj, P["q_norm_g"], P["kv_norm_g"], tabs)

    def q_epi(acc, c, s1, s2):
        return (jnp.concatenate([_rope_block(acc[:, h * LANES:(h + 1) * LANES], c, s1, s2) for h in range(MLA_HEADS)], axis=1)
                * Q_PRESCALE,)

    q_all = _mm(qn, P["w_q_up"], "nn", "q_up", out_dtypes=(BF16,), epi=q_epi, extras=[(t, "m") for t in tabs])

    def kv_epi(acc, kp):
        kb = [acc[:, h * LANES:(h + 1) * LANES] + kp for h in range(MLA_HEADS)]
        return (jnp.concatenate(kb + [acc[:, MLA_HEADS * LANES:]], axis=1),)

    kv_all = _mm(kvn, P["w_kv_up"], "nn", "kv_up", out_dtypes=(BF16,), epi=kv_epi, extras=[(kpe, "m")])
    o_att, lse = _attn_fwd(q_all, kv_all)
    cat = jnp.concatenate([y_ssd, o_att], axis=1).astype(BF16)
    mix = _mm(cat, P["w_mix_out"], "nn", "mix_out", tm=1024)
    h1, h1b = _ln_fwd([(h0, ALPHA), (mix, 1.0)], P["ln1_g"], P["ln1_b"], "ln1")
    if weights_at is not None:
        P = {**P, **weights_at("late", h1b)}
    qm = _mm(h1b, P["w_mem_q"], "nn", "mem_q", tm=1024, out_dtypes=(BF16,))
    km = _mm(mem, P["w_mem_k"], "nn", "mem_k", out_dtypes=(BF16,))
    vm = _mm(mem, P["w_mem_v"], "nn", "mem_v", out_dtypes=(BF16,))
    om = _mem_attn_fwd(qm, km, vm)
    xa = _mm(om, P["w_mem_o"], "nn", "mem_o", tm=1024)
    h2, h2b = _ln_fwd([(h1, ALPHA), (xa, 1.0)], P["ln2_g"], P["ln2_b"], "ln2")

    def up_epi(acc):
        r = jnp.maximum(acc, 0.0)
        return (r * r,)

    act = _mm(h2b, P["w_up"], "nn", "mlp_up", tm=1024, tn=1024, out_dtypes=(BF16,), epi=up_epi)
    ff = _mm(act, P["w_down"], "nn", "mlp_down", tm=1024, tk=D_FF)

    loss, dr3, dr3b, G["ln3_g"], G["ln3_b"] = _ln_loss_bwd([(h2, ALPHA), (ff, 1.0)], P["ln3_g"], P["ln3_b"], tgt, "ln3_loss")

    def dact_epi(acc, a):
        return (acc * (2.0 * jnp.sqrt(a.astype(F32))),)

    du = _mm(dr3b, P["w_down"], "nt", "mlp_down_dx", tm=1024, tn=1024, out_dtypes=(BF16,), epi=dact_epi, extras=[(act, "mn")])
    G["w_down"] = _mm(act, dr3b, "tn", "mlp_down_dw", tm=1024, tk=TOK_K, out_dtypes=(BF16,))
    G["w_up"] = _mm(h2b, du, "tn", "mlp_up_dw", tm=1024, tn=D_FF // N_DEV, tk=TOK_K, out_dtypes=(BF16,), col_slots=True)
    tie = emit(("w_down", "w_up"), G)
    dh2_ff = _mm(du, P["w_up"], "nt", "mlp_up_dx", tm=1024, tk=D_FF)
    dr2, dr2b, G["ln2_g"], G["ln2_b"] = _ln_bwd([(h1, ALPHA), (xa, 1.0)], [(dr3, ALPHA), (dh2_ff, 1.0)], P["ln2_g"] + tie, "ln2_bwd")

    dom = _mm(dr2b, P["w_mem_o"], "nt", "mem_o_dx", tm=1024, out_dtypes=(BF16,))
    G["w_mem_o"] = _mm(om, dr2b, "tn", "mem_o_dw", tm=1024, tk=TOK_K, out_dtypes=(BF16,))
    dqm, dkm, dvm = _mem_attn_bwd(qm, km, vm, dom)
    G["w_mem_q"] = _mm(h1b, dqm, "tn", "mem_q_dw", tm=1024, tk=TOK_K, out_dtypes=(BF16,))
    G["w_mem_k"] = _mm(mem, dkm, "tn", "mem_k_dw", tm=1024, out_dtypes=(BF16,))
    G["w_mem_v"] = _mm(mem, dvm, "tn", "mem_v_dw", tm=1024, out_dtypes=(BF16,))
    tie = emit(("w_mem_o", "w_mem_q", "w_mem_k", "w_mem_v"), G)
    dh1_q = _mm(dqm, P["w_mem_q"], "nt", "mem_q_dx", tm=1024)
    dr1, dr1b, G["ln1_g"], G["ln1_b"] = _ln_bwd([(h0, ALPHA), (mix, 1.0)], [(dr2, ALPHA), (dh1_q, 1.0)], P["ln1_g"] + tie, "ln1_bwd")

    dcat = _mm(dr1b, P["w_mix_out"], "nt", "mix_out_dx", tm=1024)
    G["w_mix_out"] = _mm(cat, dr1b, "tn", "mix_out_dw", tm=1024, tk=TOK_K, out_dtypes=(BF16,))
    dq_all, dk_all, dv_all = _attn_bwd(q_all, kv_all, dcat, lse, _attn_delta(dcat, o_att))
    dq_pre = _rope_bwd_all(dq_all, tabs, DQ_POSTSCALE)
    G["w_q_up"] = _mm(qn, dq_pre, "tn", "q_up_dw", tk=TOK_K, out_dtypes=(BF16,))
    dqn = _mm(dq_pre, P["w_q_up"], "nt", "q_up_dx", tm=1024)
    dkv_all = jnp.concatenate([dk_all, dv_all], axis=1).astype(BF16)
    G["w_kv_up"] = _mm(kvn, dkv_all, "tn", "kv_up_dw", tk=TOK_K, out_dtypes=(BF16,))
    dkvn = _mm(dkv_all, P["w_kv_up"], "nt", "kv_up_dx", tm=1024)
    dql, dkvl, dkr, G["q_norm_g"], G["kv_norm_g"] = _mla_prep_bwd(proj, P["q_norm_g"], P["kv_norm_g"], tabs, dqn, dkvn, dk_all)
    (dz, dxbc, ddt, G["conv_w"], G["conv_b"], G["dt_bias"], G["a_log"], G["d_skip"], G["ssd_norm_g"]) = _ssd_bwd(
        proj, states, dcat, P["conv_w"], P["conv_b"], P["dt_bias"], P["a_log"], P["d_skip"], P["ssd_norm_g"])
    tie = emit(("w_mix_out", "w_q_up", "w_kv_up", "conv_w"), G)
    S = x.shape[0]
    dproj = jnp.concatenate([dql, jnp.zeros((S, PZ - MLA_Q_RANK), F32) + tie, dz, dxbc, dkvl, ddt, dkr], axis=1).astype(BF16)
    G["w_in"] = _mm(h0b, dproj, "tn", "proj_in_dw", tm=1024, tn=640, tk=TOK_K, out_dtypes=(BF16,))
    tie = emit(("w_in",), G)
    dh0_p = _mm(dproj, P["w_in"], "nt", "proj_in_dx", tm=1024, tk=PW, epi=lambda acc, t: (acc + t,),
                extras=[(jnp.zeros((1, D_MODEL), F32) + tie, "n")])
    gx, _, G["ln_in_g"], G["ln_in_b"] = _ln_bwd([(x, 1.0)], [(dr1, ALPHA), (dh0_p, 1.0)], P["ln_in_g"] + tie, "ln_in_bwd")
    return loss, gx, G


PACK_W = 1024
BIG = (("w_in", (1024, 277), 1), ("conv_w", (4, 128), 1), ("w_q_up", (384, 96), 1), ("w_kv_up", (256, 128), 1),
       ("w_mix_out", (128, 1024), 0), ("w_mem_q", (128, 1024), 0), ("w_mem_k", (128, 1024), 0), ("w_mem_v", (128, 1024), 0),
       ("w_mem_o", (128, 1024), 0), ("w_up", (1024, 512), 1), ("w_down", (512, 1024), 0))
SMALL = ("ln_in_g", "ln_in_b", "conv_b", "ln1_g", "ln1_b", "ln2_g", "ln2_b", "ln3_g", "ln3_b",
         "ssd_norm_g", "q_norm_g", "kv_norm_g", "dt_bias", "a_log", "d_skip")
ALL_W = ("ln_in_g", "ln_in_b", "w_in", "conv_w", "conv_b", "dt_bias", "a_log", "d_skip", "ssd_norm_g", "q_norm_g", "w_q_up",
         "kv_norm_g", "w_kv_up", "w_mix_out", "ln1_g", "ln1_b", "w_mem_q", "w_mem_k", "w_mem_v", "w_mem_o", "ln2_g", "ln2_b",
         "w_up", "w_down", "ln3_g", "ln3_b")


def _rows_of(shape):
    return -(-(shape[0] * shape[1]) // PACK_W)


BIG_ROWS = sum(_rows_of(s) for _, s, _ in BIG)
BIG_R = -(-BIG_ROWS // 32) * 32
SMALL_R = 16
LOSS_ROW = 15
ADAM_TILE = 96
assert BIG_R % ADAM_TILE == 0


def _flat_rows(a, rows):
    lead = a.shape[:-2]
    f = a.reshape(lead + (-1,))
    f = jnp.pad(f, [(0, 0)] * len(lead) + [(0, rows * PACK_W - f.shape[-1])])
    return f.reshape(lead + (rows, PACK_W))


def _pack_big(arrs, dtype):
    parts = []
    for n, s, _ in BIG:
        if n == "conv_w" and dtype == BF16:
            parts.append(lax.bitcast_convert_type(arrs[n].astype(F32), BF16).reshape(1, PACK_W))
        else:
            parts.append(_flat_rows(arrs[n].astype(dtype), _rows_of(s)))
    parts.append(jnp.zeros((BIG_R - BIG_ROWS, PACK_W), dtype))
    return jnp.concatenate(parts, axis=0)


def _unpack_big(buf):
    out, r = {}, 0
    for n, s, _ in BIG:
        k = _rows_of(s)
        out[n] = buf[r:r + k].reshape(-1)[:s[0] * s[1]].reshape(s)
        r += k
    return out


def _unpack_gathered(buf):
    out, r = {}, 0
    for n, s, ax in BIG:
        k = _rows_of(s)
        if n == "conv_w":
            sh = lax.bitcast_convert_type(buf[:, r:r + k].reshape((N_DEV,) + s + (2,)), F32)
        else:
            sh = buf[:, r:r + k].reshape(N_DEV, -1)[:, :s[0] * s[1]].reshape((N_DEV,) + s)
        out[n] = sh.reshape(N_DEV * s[0], s[1]) if ax == 0 else sh.transpose(1, 0, 2).reshape(s[0], N_DEV * s[1])
        r += k
    return out


def _pack_grads(G):
    parts = []
    for n, s, ax in BIG:
        g = G[n]
        sh = g.reshape((N_DEV,) + s) if ax == 0 else g.reshape(s[0], N_DEV, s[1]).transpose(1, 0, 2)
        parts.append(_flat_rows(sh, _rows_of(s)))
    parts.append(jnp.zeros((N_DEV, BIG_R - BIG_ROWS, PACK_W), F32))
    return jnp.concatenate(parts, axis=1)


_SMALL_ROWS = (("ln_in_g",), ("ln_in_b",), ("conv_b",), ("ln1_g",), ("ln1_b",), ("ln2_g",), ("ln2_b",), ("ln3_g",), ("ln3_b",),
               ("ssd_norm_g", "q_norm_g"), ("kv_norm_g", "dt_bias", "a_log", "d_skip"))
_SMALL_W = {"ssd_norm_g": 512, "q_norm_g": 384, "kv_norm_g": 256, "dt_bias": LANES, "a_log": LANES, "d_skip": LANES}


def _pack_small(V, extra_row=None):
    rows = []
    for names in _SMALL_ROWS:
        r = jnp.concatenate([V[n] for n in names], axis=1)
        rows.append(jnp.pad(r, ((0, 0), (0, PACK_W - r.shape[1]))))
    rows.append(jnp.zeros((SMALL_R - len(rows) - 1, PACK_W), F32))
    rows.append(jnp.zeros((1, PACK_W), F32) if extra_row is None else extra_row)
    return jnp.concatenate(rows, axis=0)


def _unpack_small(buf):
    out = {}
    for i, names in enumerate(_SMALL_ROWS):
        c = 0
        for n in names:
            w = _SMALL_W.get(n, PACK_W)
            out[n] = buf[i:i + 1, c:c + w]
            c += w
    return out


MESH = pl.DeviceIdType.MESH
ANY = pl.BlockSpec(memory_space=pl.ANY)
VM = pl.BlockSpec(memory_space=pltpu.VMEM)


def _coords():
    return lax.axis_index("x"), lax.axis_index("y"), lax.axis_index("c")


def _slot(px, py, pc):
    return 4 * px + 2 * py + pc


def _gather_big(shard):
    R, W = shard.shape

    def body(x_ref, out_ref, send_sems, recv_sems, local_sem):
        x, y, c = _coords()
        me, sibling = (x, y, c), (x, y, 1 - c)
        chips = [(1 - x, y), (x, 1 - y), (1 - x, 1 - y)]

        def rows(px, py, pc):
            return out_ref.at[_slot(px, py, pc)]

        def copy(k, block, to, src=None):
            return pltpu.make_async_remote_copy(
                src_ref=rows(*block) if src is None else src, dst_ref=rows(*block),
                send_sem=send_sems.at[k], recv_sem=recv_sems.at[k], device_id=to, device_id_type=MESH)

        mine = pltpu.make_async_copy(x_ref, rows(*me), local_sem)
        mine.start()
        first = [copy(0, me, sibling, src=x_ref)]
        first += [copy(1 + j, me, (*chip, c), src=x_ref) for j, chip in enumerate(chips)]
        for cp in first:
            cp.start()
        passed = [copy(4 + j, (*chip, c), sibling) for j, chip in enumerate(chips)]
        for j, chip in enumerate(chips):
            copy(1 + j, (*chip, c), me).wait_recv()
            passed[j].start()
        copy(0, sibling, me).wait_recv()
        for j, chip in enumerate(chips):
            copy(4 + j, (*chip, 1 - c), me).wait_recv()
        for cp in first + passed:
            cp.wait_send()
        mine.wait()

    return pl.pallas_call(
        body, name="gather_weights", out_shape=jax.ShapeDtypeStruct((N_DEV, R, W), shard.dtype),
        in_specs=[ANY], out_specs=ANY,
        scratch_shapes=[pltpu.SemaphoreType.DMA((7,)), pltpu.SemaphoreType.DMA((7,)), pltpu.SemaphoreType.DMA],
    )(shard)


def _peer(k, x, y, c):
    dx, dy, dc = (k >> 2) & 1, (k >> 1) & 1, k & 1
    return (1 - x if dx else x, 1 - y if dy else y, 1 - c if dc else c)


def _scatter_grads(gpack):
    _, R, W = gpack.shape

    def body(g_ref, out_ref, send_sems, recv_sems, local_sem):
        x, y, c = _coords()
        my = _slot(x, y, c)
        mine = pltpu.make_async_copy(g_ref.at[my], out_ref.at[my], local_sem)
        mine.start()
        cps = []
        for k in range(1, N_DEV):
            peer = _peer(k, x, y, c)
            cps.append(pltpu.make_async_remote_copy(
                src_ref=g_ref.at[_slot(*peer)], dst_ref=out_ref.at[my],
                send_sem=send_sems.at[k - 1], recv_sem=recv_sems.at[k - 1], device_id=peer, device_id_type=MESH))
        for cp in cps:
            cp.start()
        for cp in cps:
            cp.wait_recv()
        for cp in cps:
            cp.wait_send()
        mine.wait()

    return pl.pallas_call(
        body, name="scatter_grads", out_shape=jax.ShapeDtypeStruct(gpack.shape, gpack.dtype),
        in_specs=[ANY], out_specs=ANY,
        scratch_shapes=[pltpu.SemaphoreType.DMA((7,)), pltpu.SemaphoreType.DMA((7,)), pltpu.SemaphoreType.DMA],
    )(gpack)


def _adam(w, g, m, v):
    m = ADAM_B1 * m + (1.0 - ADAM_B1) * g
    v = ADAM_B2 * v + (1.0 - ADAM_B2) * (g * g)
    m_hat = m / (1.0 - ADAM_B1 ** ADAM_STEP)
    v_hat = v / (1.0 - ADAM_B2 ** ADAM_STEP)
    delta = -ADAM_LR * (m_hat / (jnp.sqrt(v_hat) + ADAM_EPS) + ADAM_WD * w)
    return delta, m, v


def _sum_slots(ref):
    tot = ref[0].astype(F32)
    for q in range(1, N_DEV):
        tot = tot + ref[q].astype(F32)
    return tot


def _reduce_adam_big(recv, w, m, v):
    _, R, W = recv.shape

    def body(r_ref, w_ref, m_ref, v_ref, g_ref, d_ref, nm_ref, nv_ref):
        g = _sum_slots(r_ref)
        g_ref[...] = g
        d_ref[...], nm_ref[...], nv_ref[...] = _adam(w_ref[...], g, m_ref[...], v_ref[...])

    row = pl.BlockSpec((ADAM_TILE, W), lambda i: (i, 0))
    return pl.pallas_call(
        body, name="reduce_adam", grid=(R // ADAM_TILE,),
        in_specs=[pl.BlockSpec((N_DEV, ADAM_TILE, W), lambda i: (0, i, 0)), row, row, row], out_specs=[row] * 4,
        out_shape=[jax.ShapeDtypeStruct((R, W), F32)] * 4, compiler_params=_cparams(("parallel",)),
    )(recv, w, m, v)


def _allreduce_adam_small(gs, w, m, v):
    R, W = gs.shape

    def body(g_ref, w_ref, m_ref, v_ref, go_ref, d_ref, nm_ref, nv_ref, land, send_sems, recv_sems):
        x, y, c = _coords()
        my = _slot(x, y, c)
        cps = []
        for k in range(1, N_DEV):
            peer = _peer(k, x, y, c)
            cps.append(pltpu.make_async_remote_copy(
                src_ref=g_ref, dst_ref=land.at[my], send_sem=send_sems.at[k - 1], recv_sem=recv_sems.at[k - 1],
                device_id=peer, device_id_type=MESH))
        for cp in cps:
            cp.start()
        land[my] = g_ref[...]
        for cp in cps:
            cp.wait_recv()
        for cp in cps:
            cp.wait_send()
        g = _sum_slots(land)
        go_ref[...] = g
        d_ref[...], nm_ref[...], nv_ref[...] = _adam(w_ref[...], g, m_ref[...], v_ref[...])

    return pl.pallas_call(
        body, name="allreduce_small", in_specs=[VM] * 4, out_specs=[VM] * 4,
        out_shape=[jax.ShapeDtypeStruct((R, W), F32)] * 4,
        scratch_shapes=[pltpu.VMEM((N_DEV, R, W), F32), pltpu.SemaphoreType.DMA((7,)), pltpu.SemaphoreType.DMA((7,))],
    )(gs, w, m, v)


def _row(v):
    return v.reshape(1, -1).astype(F32)


def _step(x, mem, positions, tgt, W, M, V):
    shard = {n: W[n][0] for n, _, _ in BIG}
    full = _unpack_gathered(_gather_big(_pack_big(shard, BF16)))
    P = {n: _row(W[n]) for n in SMALL}
    for n in ("dt_bias", "a_log", "d_skip"):
        P[n] = _head_lanes(P[n])
    P.update(w_in=_pad_w_in(full["w_in"]), w_q_up=_pad_w_q(full["w_q_up"]), w_kv_up=_pad_w_kv(full["w_kv_up"]),
             conv_w=jnp.pad(full["conv_w"].astype(F32), ((0, 8 - SSD_CONV), (0, 0))))
    for n in ("w_mix_out", "w_mem_q", "w_mem_k", "w_mem_v", "w_mem_o", "w_up", "w_down"):
        P[n] = full[n]

    loss, gx, G = _local_step(x[0], mem[0], positions[0], tgt[0], P)

    G["w_in"], G["w_q_up"], G["w_kv_up"] = _unpad_w_in(G["w_in"]), _unpad_w_q(G["w_q_up"]), _unpad_w_kv(G["w_kv_up"])
    G["conv_w"] = G["conv_w"][:SSD_CONV]
    recv = _scatter_grads(_pack_grads(G))
    big = [_unpack_big(b) for b in _reduce_adam_big(
        recv, _pack_big(shard, F32), _pack_big({n: M[n][0] for n, _, _ in BIG}, F32), _pack_big({n: V[n][0] for n, _, _ in BIG}, F32))]

    def small_rows(D):
        rows = {n: _row(D[n]) for n in SMALL}
        for n in ("dt_bias", "a_log", "d_skip"):
            rows[n] = _head_lanes(rows[n])
        return rows

    loss_row = jnp.broadcast_to(loss[:, :1], (1, PACK_W))
    small_bufs = _allreduce_adam_small(
        _pack_small({n: G[n] for n in SMALL}, loss_row), _pack_small(small_rows(W)), _pack_small(small_rows(M)),
        _pack_small(small_rows(V)))
    loss_tot = small_bufs[0][LOSS_ROW, 0]
    small = [_unpack_small(b) for b in small_bufs]

    outs = []
    for res_b, res_s in zip(big, small):
        for n in ALL_W:
            if n in res_b:
                outs.append(res_b[n].reshape(W[n].shape))
            else:
                outs.append(res_s[n][:, :W[n].size].reshape(W[n].shape))
    return (loss_tot, gx[None], *outs)


def kernel(x, mem, positions, ln_in_g, ln_in_b, w_in, conv_w, conv_b, dt_bias, a_log, d_skip, ssd_norm_g, q_norm_g, w_q_up, kv_norm_g, w_kv_up, w_mix_out, ln1_g, ln1_b, w_mem_q, w_mem_k, w_mem_v, w_mem_o, ln2_g, ln2_b, w_up, w_down, ln3_g, ln3_b, loss_target, m_ln_in_g, m_ln_in_b, m_w_in, m_conv_w, m_conv_b, m_dt_bias, m_a_log, m_d_skip, m_ssd_norm_g, m_q_norm_g, m_w_q_up, m_kv_norm_g, m_w_kv_up, m_w_mix_out, m_ln1_g, m_ln1_b, m_w_mem_q, m_w_mem_k, m_w_mem_v, m_w_mem_o, m_ln2_g, m_ln2_b, m_w_up, m_w_down, m_ln3_g, m_ln3_b, v_ln_in_g, v_ln_in_b, v_w_in, v_conv_w, v_conv_b, v_dt_bias, v_a_log, v_d_skip, v_ssd_norm_g, v_q_norm_g, v_w_q_up, v_kv_norm_g, v_w_kv_up, v_w_mix_out, v_ln1_g, v_ln1_b, v_w_mem_q, v_w_mem_k, v_w_mem_v, v_w_mem_o, v_ln2_g, v_ln2_b, v_w_up, v_w_down, v_ln3_g, v_ln3_b):
    a = dict(locals())
    W = {n: a[n] for n in ALL_W}
    M = {n: a["m_" + n] for n in ALL_W}
    V = {n: a["v_" + n] for n in ALL_W}
    return _step_overlapped(x, mem, positions, loss_target, W, M, V)


HBM = pl.BlockSpec(memory_space=pltpu.HBM)
SEM = pl.BlockSpec(memory_space=pltpu.SEMAPHORE)
EFFECT = pltpu.SideEffectType.DATAFLOW_SIDE_EFFECTING
SHARD_SHAPE = {n: s for n, s, _ in BIG}
SHARD_AXIS = {n: ax for n, _, ax in BIG}
GATHER_NOW = ("w_in",)
GATHER_MID = ("conv_w", "w_q_up", "w_kv_up", "w_mix_out")
GATHER_LATE = ("w_mem_q", "w_mem_k", "w_mem_v", "w_mem_o", "w_up", "w_down")


def _my_slot():
    return _slot(*_coords())


def _group_copies(src_refs, land_refs, send_sems, recv_sems, slotted, landing_of_peer):
    x, y, c = _coords()
    my = _slot(x, y, c)
    cps = []
    for a, (s_ref, l_ref) in enumerate(zip(src_refs, land_refs)):
        for k in range(1, N_DEV):
            peer = _peer(k, x, y, c)
            cps.append(pltpu.make_async_remote_copy(
                src_ref=s_ref.at[_slot(*peer)] if slotted else s_ref,
                dst_ref=l_ref.at[_slot(*peer)] if landing_of_peer else l_ref.at[my],
                send_sem=send_sems.at[7 * a + k - 1], recv_sem=recv_sems.at[7 * a + k - 1],
                device_id=peer, device_id_type=MESH))
    return cps


def _send_start(srcs, lands, slotted, name):
    n = len(srcs)

    def body(*refs):
        for cp in _group_copies(refs[:n], refs[n:2 * n], refs[2 * n], refs[2 * n + 1], slotted, False):
            cp.start()
        refs[-1][...] = jnp.zeros(refs[-1].shape, F32)

    res = pl.pallas_call(
        body, name=name,
        out_shape=(pltpu.SemaphoreType.DMA((7 * n,)), pltpu.SemaphoreType.DMA((7 * n,)),
                   *[pltpu.HBM(a.shape, a.dtype) for a in srcs], *[pltpu.HBM(a.shape, a.dtype) for a in lands],
                   jax.ShapeDtypeStruct((8, LANES), F32)),
        in_specs=[HBM] * (2 * n), out_specs=(SEM, SEM, *[HBM] * (2 * n), VM),
        input_output_aliases={i: 2 + i for i in range(2 * n)},
        compiler_params=pltpu.CompilerParams(has_side_effects=EFFECT),
    )(*[pltpu.with_memory_space_constraint(a, pltpu.HBM) for a in list(srcs) + list(lands)])
    return (res[0], res[1], res[2:2 + n], res[2 + n:2 + 2 * n]), res[-1][:1, :1]


def _send_wait(started, after, slotted, name):
    send_sems, recv_sems, srcs, lands = started
    n = len(srcs)

    def body(*refs):
        for cp in _group_copies(refs[:n], refs[n:2 * n], refs[2 * n], refs[2 * n + 1], slotted, True):
            cp.wait_send()
            cp.wait_recv()

    res = pl.pallas_call(
        body, name=name, out_shape=tuple(pltpu.HBM(a.shape, a.dtype) for a in list(srcs) + list(lands)),
        in_specs=[HBM] * (2 * n) + [SEM, SEM, ANY], out_specs=tuple([HBM] * (2 * n)),
        input_output_aliases={i: i for i in range(2 * n)},
        compiler_params=pltpu.CompilerParams(has_side_effects=EFFECT),
    )(*srcs, *lands, send_sems, recv_sems, after)
    return res[n:]


def _landing(own, my):
    return lax.dynamic_update_slice(lax.empty((N_DEV,) + own.shape, own.dtype), own[None], (my,) + (0,) * own.ndim)


def _gather_now(shards):
    n = len(shards)

    def body(*refs):
        x_refs, out_refs = refs[:n], refs[n:2 * n]
        send_sems, recv_sems, local_sems = refs[2 * n:]
        x, y, c = _coords()
        me, sibling = (x, y, c), (x, y, 1 - c)
        chips = [(1 - x, y), (x, 1 - y), (1 - x, 1 - y)]

        def copy(a, k, block, to, src=None):
            rows = out_refs[a].at[_slot(*block)]
            return pltpu.make_async_remote_copy(
                src_ref=rows if src is None else src, dst_ref=rows,
                send_sem=send_sems.at[7 * a + k], recv_sem=recv_sems.at[7 * a + k], device_id=to, device_id_type=MESH)

        mine = [pltpu.make_async_copy(x_refs[a], out_refs[a].at[_slot(*me)], local_sems.at[a]) for a in range(n)]
        for cp in mine:
            cp.start()
        first = []
        for a in range(n):
            first.append(copy(a, 0, me, sibling, src=x_refs[a]))
            first += [copy(a, 1 + j, me, (*chip, c), src=x_refs[a]) for j, chip in enumerate(chips)]
        for cp in first:
            cp.start()
        passed = []
        for j, chip in enumerate(chips):
            for a in range(n):
                copy(a, 1 + j, (*chip, c), me).wait_recv()
                fwd = copy(a, 4 + j, (*chip, c), sibling)
                fwd.start()
                passed.append(fwd)
        for a in range(n):
            copy(a, 0, sibling, me).wait_recv()
            for j, chip in enumerate(chips):
                copy(a, 4 + j, (*chip, 1 - c), me).wait_recv()
        for cp in first + passed:
            cp.wait_send()
        for cp in mine:
            cp.wait()

    return pl.pallas_call(
        body, name="gather_now", out_shape=[jax.ShapeDtypeStruct((N_DEV,) + s.shape, s.dtype) for s in shards],
        in_specs=[ANY] * n, out_specs=[ANY] * n,
        scratch_shapes=[pltpu.SemaphoreType.DMA((7 * n,)), pltpu.SemaphoreType.DMA((7 * n,)), pltpu.SemaphoreType.DMA((n,))],
    )(*shards)


def _full_from_slots(name, slots):
    a, b = SHARD_SHAPE[name]
    return slots.reshape(N_DEV * a, b) if SHARD_AXIS[name] == 0 else slots.transpose(1, 0, 2).reshape(a, N_DEV * b)


def _slots_from_full(name, g):
    a, b = SHARD_SHAPE[name]
    return g.reshape(N_DEV, a, b) if SHARD_AXIS[name] == 0 else g.reshape(a, N_DEV, b).transpose(1, 0, 2)


def _reduce_adam(recv, w, m, v, name):
    _, a, b = recv.shape
    ta = a
    while ta * b * 4 * N_DEV > 4 * 1024 * 1024 and ta % 16 == 0:
        ta //= 2

    def body(r_ref, w_ref, m_ref, v_ref, g_ref, d_ref, nm_ref, nv_ref):
        g = _sum_slots(r_ref)
        g_ref[...] = g
        d_ref[...], nm_ref[...], nv_ref[...] = _adam(w_ref[...], g, m_ref[...], v_ref[...])

    row = pl.BlockSpec((ta, b), lambda i: (i, 0))
    return pl.pallas_call(
        body, name=name, grid=(a // ta,),
        in_specs=[pl.BlockSpec((N_DEV, ta, b), lambda i: (0, i, 0)), row, row, row], out_specs=[row] * 4,
        out_shape=[jax.ShapeDtypeStruct((a, b), F32)] * 4, compiler_params=_cparams(("parallel",)),
    )(recv, w, m, v)


def _step_overlapped(x, mem, positions, tgt, W, M, V):
    my = _my_slot()
    shard = {n: W[n][0] for n, _, _ in BIG}
    send = {n: (shard[n] if n == "conv_w" else shard[n].astype(BF16)) for n in shard}

    now = dict(zip(GATHER_NOW, _gather_now([send[n] for n in GATHER_NOW])))
    mid_src, late_src = [send[n] for n in GATHER_MID], [send[n] for n in GATHER_LATE]
    mid_src, _ = lax.optimization_barrier((mid_src, now["w_in"]))
    mid, tie = _send_start(mid_src, [_landing(s, my) for s in mid_src], False, "gather_mid_start")
    my_after_mid = my + tie[0, 0].astype(jnp.int32)
    late, tie = _send_start(late_src, [_landing(s, my_after_mid) for s in late_src], False, "gather_late_start")

    P = {n: _row(W[n]) for n in SMALL}
    for n in ("dt_bias", "a_log", "d_skip"):
        P[n] = _head_lanes(P[n])
    P["ln_in_g"] = P["ln_in_g"] + tie
    P["w_in"] = _pad_w_in(_full_from_slots("w_in", now["w_in"]))

    def weights_at(stage, after):
        if stage == "mid":
            lands = _send_wait(mid, after, False, "gather_mid_wait")
            full = {n: _full_from_slots(n, l) for n, l in zip(GATHER_MID, lands)}
            return dict(w_q_up=_pad_w_q(full["w_q_up"]), w_kv_up=_pad_w_kv(full["w_kv_up"]), w_mix_out=full["w_mix_out"],
                        conv_w=jnp.pad(full["conv_w"], ((0, 8 - SSD_CONV), (0, 0))))
        lands = _send_wait(late, after, False, "gather_late_wait")
        return {n: _full_from_slots(n, l) for n, l in zip(GATHER_LATE, lands)}

    started, res = [], {}

    def finish(i, after):
        names, st = started[i]
        lands = _send_wait(st, after, True, "scatter_wait_%d" % i)
        for n, recv in zip(names, lands):
            res[n] = _reduce_adam(recv, shard[n], M[n][0], V[n][0], "reduce_adam_" + n)
        return sum(res[n][0][:1, :1] for n in names) * 0.0

    def emit(names, G):
        srcs = []
        for n in names:
            g = G[n]
            if n == "w_in":
                g = _unpad_w_in(g)
            elif n == "w_q_up":
                g = _unpad_w_q(g)
            elif n == "w_kv_up":
                g = _unpad_w_kv(g)
            elif n == "conv_w":
                g = g[:SSD_CONV]
            srcs.append(g if g.ndim == 3 else _slots_from_full(n, g))
        lands = [_landing(lax.dynamic_index_in_dim(s, my, 0, keepdims=False), my) for s in srcs]
        st, tie = _send_start(srcs, lands, True, "scatter_start_%d" % len(started))
        started.append((names, st))
        if len(started) == 3:
            tie = tie + finish(0, srcs[0]) + finish(1, srcs[0])
        return tie

    loss, gx, G = _local_step(x[0], mem[0], positions[0], tgt[0], P, weights_at, emit)
    finish(2, gx)
    finish(3, gx)

    small, loss_tot = _allreduce_adam_vectors(
        {n: G[n] for n in SMALL}, loss, {n: _row(W[n]) for n in SMALL}, {n: _row(M[n]) for n in SMALL}, {n: _row(V[n]) for n in SMALL})

    outs = []
    for j in range(4):
        for n in ALL_W:
            outs.append((res[n][j] if n in res else small[j][n]).reshape(W[n].shape))
    return (loss_tot[0, 0], gx[None], *outs)


def _vector_places():
    places = {}
    for r, names in enumerate(_SMALL_ROWS):
        c = 0
        for n in names:
            w = _SMALL_W.get(n, PACK_W)
            places[n] = (r, c, w, SSD_HEADS if n in ("dt_bias", "a_log", "d_skip") else w)
            c += w
    return places


def _allreduce_adam_vectors(grads, loss, Ws, Ms, Vs):
    places = _vector_places()
    ns = len(SMALL)

    def body(*refs):
        g_in, loss_in = refs[:ns], refs[ns]
        w_in, m_in, v_in = refs[ns + 1:2 * ns + 1], refs[2 * ns + 1:3 * ns + 1], refs[3 * ns + 1:4 * ns + 1]
        o = 4 * ns + 1
        outs = [refs[o + j * ns:o + (j + 1) * ns] for j in range(4)]
        loss_out, stage, land, send_sems, recv_sems = refs[o + 4 * ns:]
        stage[...] = jnp.zeros(stage.shape, F32)
        for i, n in enumerate(SMALL):
            r, c, w, _ = places[n]
            stage[r:r + 1, c:c + w] = g_in[i][...]
        stage[LOSS_ROW:LOSS_ROW + 1, 0:LANES] = loss_in[...]
        x, y, c_ = _coords()
        my = _slot(x, y, c_)
        cps = []
        for k in range(1, N_DEV):
            peer = _peer(k, x, y, c_)
            cps.append(pltpu.make_async_remote_copy(
                src_ref=stage, dst_ref=land.at[my], send_sem=send_sems.at[k - 1], recv_sem=recv_sems.at[k - 1],
                device_id=peer, device_id_type=MESH))
        for cp in cps:
            cp.start()
        land[my] = stage[...]
        for cp in cps:
            cp.wait_recv()
        for cp in cps:
            cp.wait_send()
        tot = _sum_slots(land)
        loss_out[...] = tot[LOSS_ROW:LOSS_ROW + 1, 0:LANES]
        for i, n in enumerate(SMALL):
            r, c, _, wt = places[n]
            g = tot[r:r + 1, c:c + wt]
            outs[0][i][...] = g
            outs[1][i][...], outs[2][i][...], outs[3][i][...] = _adam(w_in[i][...], g, m_in[i][...], v_in[i][...])

    shapes = [jax.ShapeDtypeStruct((1, places[n][3]), F32) for n in SMALL]
    res = pl.pallas_call(
        body, name="allreduce_vectors", in_specs=[VM] * (4 * ns + 1), out_specs=[VM] * (4 * ns + 1),
        out_shape=shapes * 4 + [jax.ShapeDtypeStruct((1, LANES), F32)],
        scratch_shapes=[pltpu.VMEM((SMALL_R, PACK_W), F32), pltpu.VMEM((N_DEV, SMALL_R, PACK_W), F32),
                        pltpu.SemaphoreType.DMA((7,)), pltpu.SemaphoreType.DMA((7,))],
    )(*[grads[n] for n in SMALL], loss, *[Ws[n] for n in SMALL], *[Ms[n] for n in SMALL], *[Vs[n] for n in SMALL])
    return [dict(zip(SMALL, res[j * ns:(j + 1) * ns])) for j in range(4)], res[4 * ns]
```

```python
import functools
import math

import jax
import jax.numpy as jnp
from jax import lax
from jax.experimental import pallas as pl
from jax.experimental.pallas import tpu as pltpu

F32, BF16 = jnp.float32, jnp.bfloat16

N_DEV = 8
D_MODEL = 1024
SSD_HEADS, SSD_HEAD_DIM, SSD_INNER, SSD_STATE, SSD_CONV, SSD_CHUNK = 8, 64, 512, 128, 4, 128
SSD_XBC = 1024
MLA_HEADS, MLA_NOPE, MLA_ROPE, MLA_QK, MLA_V = 8, 64, 32, 96, 64
MLA_Q_RANK, MLA_KV_RANK = 384, 256
ROPE_THETA = 10000.0
MEM_HEADS, MEM_HEAD_DIM = 4, 256
D_FF = 4096
IN_WIDTH = 2216
LN_EPS, RMS_EPS = 1e-5, 1e-6
ALPHA = 2.0 ** 0.25
ADAM_LR, ADAM_B1, ADAM_B2, ADAM_EPS, ADAM_WD, ADAM_STEP = 0.001, 0.9, 0.999, 1e-08, 0.01, 10

LANES = 128
NEG = -1e30
VMEM_LIMIT = 56 * 1024 * 1024
TOK_K = 4096

PQ, PZ, PX, PKV, PDT, PKR, PW = 0, 512, 1024, 2048, 2304, 2432, 2560
KR_LANE = 64


def _cparams(sem):
    return pltpu.CompilerParams(dimension_semantics=sem, vmem_limit_bytes=VMEM_LIMIT)


def _sigmoid(x):
    return 1.0 / (1.0 + jnp.exp(-x))


def _mm(a, b, mode, name, *, tm=512, tn=None, tk=None, out_dtypes=(F32,), epi=None, extras=(), col_slots=False):
    if mode == "nn":
        (M, K), (K2, N) = a.shape, b.shape
    elif mode == "nt":
        (M, K), (N, K2) = a.shape, b.shape
    else:
        (K, M), (K2, N) = a.shape, b.shape
    assert K == K2, (name, a.shape, b.shape)
    tm, tn, tk = min(tm, M), min(tn or N, N), min(tk or K, K)
    assert M % tm == 0 and N % tn == 0 and K % tk == 0, (name, M, N, K, tm, tn, tk)
    gk = K // tk
    a_spec = pl.BlockSpec((tk, tm), lambda i, j, k: (k, i)) if mode == "tn" else pl.BlockSpec((tm, tk), lambda i, j, k: (i, k))
    b_spec = pl.BlockSpec((tn, tk), lambda i, j, k: (j, k)) if mode == "nt" else pl.BlockSpec((tk, tn), lambda i, j, k: (k, j))
    dims = {"nn": ((1,), (0,)), "nt": ((1,), (1,)), "tn": ((0,), (0,))}[mode]
    ex_specs = []
    for arr, kind in extras:
        if kind == "mn":
            ex_specs.append(pl.BlockSpec((tm, tn), lambda i, j, k: (i, j)))
        elif kind == "n":
            ex_specs.append(pl.BlockSpec((1, tn), lambda i, j, k: (0, j)))
        else:
            ex_specs.append(pl.BlockSpec((tm, arr.shape[1]), lambda i, j, k: (i, 0)))
    ne, no = len(extras), len(out_dtypes)

    def body(*refs):
        a_ref, b_ref = refs[0], refs[1]
        ex, outs = refs[2:2 + ne], refs[2 + ne:2 + ne + no]
        part = lax.dot_general(a_ref[...].astype(BF16), b_ref[...].astype(BF16), (dims, ((), ())),
                               preferred_element_type=F32)

        def finish(acc):
            res = epi(acc, *[e[...] for e in ex]) if epi is not None else (acc,)
            for o, r in zip(outs, res):
                o[...] = r.astype(o.dtype)

        if gk == 1:
            finish(part)
        else:
            acc_ref = refs[-1]
            k = pl.program_id(2)

            @pl.when(k == 0)
            def _():
                acc_ref[...] = part

            @pl.when(k > 0)
            def _():
                acc_ref[...] += part

            @pl.when(k == gk - 1)
            def _():
                finish(acc_ref[...])

    res = pl.pallas_call(
        body, name=name, grid=(M // tm, N // tn, gk),
        in_specs=[a_spec, b_spec] + ex_specs,
        out_specs=[pl.BlockSpec((None, tm, tn), lambda i, j, k: (j, i, 0)) if col_slots else pl.BlockSpec((tm, tn), lambda i, j, k: (i, j))
                   for _ in out_dtypes],
        out_shape=[jax.ShapeDtypeStruct((N // tn, M, tn) if col_slots else (M, N), dt) for dt in out_dtypes],
        scratch_shapes=[pltpu.VMEM((tm, tn), F32)] if gk > 1 else [],
        compiler_params=_cparams(("parallel", "parallel", "arbitrary")),
    )(a, b, *[e[0] for e in extras])
    return res[0] if no == 1 else res


def _ln_stats(r):
    mu = jnp.mean(r, axis=-1, keepdims=True)
    xc = r - mu
    var = jnp.mean(xc * xc, axis=-1, keepdims=True)
    rstd = lax.rsqrt(var + LN_EPS)
    return xc * rstd, rstd


def _ln_fwd(terms, g, b, name, tm=512):
    S, D = terms[0][0].shape
    coefs = [c for _, c in terms]
    nt = len(terms)

    def body(*refs):
        r = sum(c * t[...] for t, c in zip(refs[:nt], coefs))
        xh, _ = _ln_stats(r)
        h = xh * refs[nt][...] + refs[nt + 1][...]
        refs[nt + 2][...] = h
        refs[nt + 3][...] = h.astype(BF16)

    row = pl.BlockSpec((tm, D), lambda i: (i, 0))
    vec = pl.BlockSpec((1, D), lambda i: (0, 0))
    return pl.pallas_call(
        body, name=name, grid=(S // tm,), in_specs=[row] * nt + [vec, vec], out_specs=[row, row],
        out_shape=[jax.ShapeDtypeStruct((S, D), F32), jax.ShapeDtypeStruct((S, D), BF16)], compiler_params=_cparams(("parallel",)),
    )(*[t for t, _ in terms], g, b)


def _ln_bwd(terms, dterms, g, name, tm=512):
    S, D = terms[0][0].shape
    coefs, dcoefs = [c for _, c in terms], [c for _, c in dterms]
    nt, nd = len(terms), len(dterms)

    def body(*refs):
        i = pl.program_id(0)
        r = sum(c * t[...] for t, c in zip(refs[:nt], coefs))
        dh = sum(c * t[...].astype(F32) for t, c in zip(refs[nt:nt + nd], dcoefs))
        g_ref = refs[nt + nd]
        dr_ref, drb_ref, dg_ref, db_ref = refs[nt + nd + 1:]
        xh, rstd = _ln_stats(r)
        dxh = dh * g_ref[...]
        m1 = jnp.mean(dxh, axis=-1, keepdims=True)
        m2 = jnp.mean(dxh * xh, axis=-1, keepdims=True)
        dr = rstd * (dxh - m1 - xh * m2)
        dr_ref[...] = dr
        drb_ref[...] = dr.astype(BF16)
        pg = jnp.sum(dh * xh, axis=0, keepdims=True)
        pb = jnp.sum(dh, axis=0, keepdims=True)

        @pl.when(i == 0)
        def _():
            dg_ref[...] = pg
            db_ref[...] = pb

        @pl.when(i > 0)
        def _():
            dg_ref[...] += pg
            db_ref[...] += pb

    row = pl.BlockSpec((tm, D), lambda i: (i, 0))
    vec = pl.BlockSpec((1, D), lambda i: (0, 0))
    return pl.pallas_call(
        body, name=name, grid=(S // tm,), in_specs=[row] * (nt + nd) + [vec], out_specs=[row, row, vec, vec],
        out_shape=[jax.ShapeDtypeStruct((S, D), F32), jax.ShapeDtypeStruct((S, D), BF16), jax.ShapeDtypeStruct((1, D), F32),
                   jax.ShapeDtypeStruct((1, D), F32)],
        compiler_params=_cparams(("arbitrary",)),
    )(*[t for t, _ in terms], *[t for t, _ in dterms], g)


def _ln_loss_bwd(terms, g, b, tgt, name, tm=512):
    S, D = terms[0][0].shape
    coefs = [c for _, c in terms]
    nt = len(terms)

    def body(*refs):
        i = pl.program_id(0)
        r = sum(c * t[...] for t, c in zip(refs[:nt], coefs))
        g_ref, b_ref, t_ref = refs[nt:nt + 3]
        loss_ref, dr_ref, drb_ref, dg_ref, db_ref = refs[nt + 3:]
        xh, rstd = _ln_stats(r)
        h = xh * g_ref[...] + b_ref[...]
        diff = h - t_ref[...]
        pl_ = 0.5 * jnp.sum(jnp.mean(diff * diff, axis=-1, keepdims=True), axis=0, keepdims=True)
        dh = diff * (1.0 / D)
        dxh = dh * g_ref[...]
        m1 = jnp.mean(dxh, axis=-1, keepdims=True)
        m2 = jnp.mean(dxh * xh, axis=-1, keepdims=True)
        dr = rstd * (dxh - m1 - xh * m2)
        dr_ref[...] = dr
        drb_ref[...] = dr.astype(BF16)
        pg = jnp.sum(dh * xh, axis=0, keepdims=True)
        pb = jnp.sum(dh, axis=0, keepdims=True)
        plb = jnp.broadcast_to(pl_, (1, LANES))

        @pl.when(i == 0)
        def _():
            dg_ref[...] = pg
            db_ref[...] = pb
            loss_ref[...] = plb

        @pl.when(i > 0)
        def _():
            dg_ref[...] += pg
            db_ref[...] += pb
            loss_ref[...] += plb

    row = pl.BlockSpec((tm, D), lambda i: (i, 0))
    vec = pl.BlockSpec((1, D), lambda i: (0, 0))
    lvec = pl.BlockSpec((1, LANES), lambda i: (0, 0))
    return pl.pallas_call(
        body, name=name, grid=(S // tm,), in_specs=[row] * nt + [vec, vec, row], out_specs=[lvec, row, row, vec, vec],
        out_shape=[jax.ShapeDtypeStruct((1, LANES), F32), jax.ShapeDtypeStruct((S, D), F32), jax.ShapeDtypeStruct((S, D), BF16),
                   jax.ShapeDtypeStruct((1, D), F32), jax.ShapeDtypeStruct((1, D), F32)],
        compiler_params=_cparams(("arbitrary",)),
    )(*[t for t, _ in terms], g, b, tgt)


def _rope_tables(positions):
    half = MLA_ROPE // 2
    inv_freq = jnp.power(ROPE_THETA, -jnp.arange(half, dtype=F32) / half)
    ang = positions.astype(F32)[:, None] * inv_freq
    cos, sin = jnp.cos(ang), jnp.sin(ang)
    S = positions.shape[0]
    one, zero = jnp.ones((S, MLA_NOPE), F32), jnp.zeros((S, half), F32)
    pad = jnp.zeros((S, LANES - MLA_QK), F32)
    c = jnp.concatenate([one, cos, cos, pad], axis=1)
    s1 = jnp.concatenate([0 * one, -sin, zero, pad], axis=1)
    s2 = jnp.concatenate([0 * one, zero, sin, pad], axis=1)
    return c, s1, s2


def _rope_block(x, c, s1, s2):
    half = MLA_ROPE // 2
    return x * c + pltpu.roll(x, LANES - half, axis=1) * s1 + pltpu.roll(x, half, axis=1) * s2


def _rms_fwd(x, g):
    r = lax.rsqrt(jnp.mean(x * x, axis=-1, keepdims=True) + RMS_EPS)
    return x * r * g


def _rms_bwd(x, g, dy):
    r = lax.rsqrt(jnp.mean(x * x, axis=-1, keepdims=True) + RMS_EPS)
    xh = x * r
    dyh = dy * g
    dx = r * (dyh - xh * jnp.mean(dyh * xh, axis=-1, keepdims=True))
    return dx, jnp.sum(dy * xh, axis=0, keepdims=True)


def _mla_prep(proj, qg, kvg, tabs, tm=512):
    S = proj.shape[0]

    def body(ql_ref, kvl_ref, kr_ref, qg_ref, kvg_ref, c_ref, s1_ref, s2_ref, qn_ref, kvn_ref, kpe_ref):
        qn_ref[...] = _rms_fwd(ql_ref[...], qg_ref[...]).astype(BF16)
        kvn_ref[...] = _rms_fwd(kvl_ref[...], kvg_ref[...]).astype(BF16)
        kpe_ref[...] = _rope_block(kr_ref[...], c_ref[...], s1_ref[...], s2_ref[...])

    tab = pl.BlockSpec((tm, LANES), lambda i: (i, 0))
    return pl.pallas_call(
        body, name="mla_prep", grid=(S // tm,),
        in_specs=[pl.BlockSpec((tm, MLA_Q_RANK), lambda i: (i, PQ // MLA_Q_RANK)),
                  pl.BlockSpec((tm, MLA_KV_RANK), lambda i: (i, PKV // MLA_KV_RANK)),
                  pl.BlockSpec((tm, LANES), lambda i: (i, PKR // LANES)),
                  pl.BlockSpec((1, MLA_Q_RANK), lambda i: (0, 0)), pl.BlockSpec((1, MLA_KV_RANK), lambda i: (0, 0)),
                  tab, tab, tab],
        out_specs=[pl.BlockSpec((tm, MLA_Q_RANK), lambda i: (i, 0)), pl.BlockSpec((tm, MLA_KV_RANK), lambda i: (i, 0)), tab],
        out_shape=[jax.ShapeDtypeStruct((S, MLA_Q_RANK), BF16), jax.ShapeDtypeStruct((S, MLA_KV_RANK), BF16),
                   jax.ShapeDtypeStruct((S, LANES), F32)],
        compiler_params=_cparams(("parallel",)),
    )(proj, proj, proj, qg, kvg, *tabs)


def _mla_prep_bwd(proj, qg, kvg, tabs, dqn, dkvn, dk_all, tm=512):
    S = proj.shape[0]

    def body(ql_ref, kvl_ref, qg_ref, kvg_ref, c_ref, s1_ref, s2_ref, dqn_ref, dkvn_ref, dk_ref,
             dql_ref, dkvl_ref, dkr_ref, dqg_ref, dkvg_ref):
        i = pl.program_id(0)
        dql, pq = _rms_bwd(ql_ref[...], qg_ref[...], dqn_ref[...])
        dkvl, pkv = _rms_bwd(kvl_ref[...], kvg_ref[...], dkvn_ref[...])
        dql_ref[...] = dql
        dkvl_ref[...] = dkvl
        dk = dk_ref[...]
        dkpe = dk[:, 0:LANES]
        for h in range(1, MLA_HEADS):
            dkpe = dkpe + dk[:, h * LANES:(h + 1) * LANES]
        lane = lax.broadcasted_iota(jnp.int32, dkpe.shape, 1)
        dkpe = jnp.where((lane >= KR_LANE) & (lane < KR_LANE + MLA_ROPE), dkpe, 0.0)
        dkr_ref[...] = _rope_block(dkpe, c_ref[...], -s1_ref[...], -s2_ref[...])

        @pl.when(i == 0)
        def _():
            dqg_ref[...] = pq
            dkvg_ref[...] = pkv

        @pl.when(i > 0)
        def _():
            dqg_ref[...] += pq
            dkvg_ref[...] += pkv

    tab = pl.BlockSpec((tm, LANES), lambda i: (i, 0))
    qspec = pl.BlockSpec((tm, MLA_Q_RANK), lambda i: (i, 0))
    kvspec = pl.BlockSpec((tm, MLA_KV_RANK), lambda i: (i, 0))
    qv, kvv = pl.BlockSpec((1, MLA_Q_RANK), lambda i: (0, 0)), pl.BlockSpec((1, MLA_KV_RANK), lambda i: (0, 0))
    return pl.pallas_call(
        body, name="mla_prep_bwd", grid=(S // tm,),
        in_specs=[pl.BlockSpec((tm, MLA_Q_RANK), lambda i: (i, PQ // MLA_Q_RANK)),
                  pl.BlockSpec((tm, MLA_KV_RANK), lambda i: (i, PKV // MLA_KV_RANK)),
                  qv, kvv, tab, tab, tab, qspec, kvspec, pl.BlockSpec((tm, MLA_HEADS * LANES), lambda i: (i, 0))],
        out_specs=[qspec, kvspec, tab, qv, kvv],
        out_shape=[jax.ShapeDtypeStruct((S, MLA_Q_RANK), F32), jax.ShapeDtypeStruct((S, MLA_KV_RANK), F32),
                   jax.ShapeDtypeStruct((S, LANES), F32), jax.ShapeDtypeStruct((1, MLA_Q_RANK), F32),
                   jax.ShapeDtypeStruct((1, MLA_KV_RANK), F32)],
        compiler_params=_cparams(("arbitrary",)),
    )(proj, proj, qg, kvg, *tabs, dqn, dkvn, dk_all)


def _rope_bwd_all(dq_all, tabs, scale, tm=512):
    S, W = dq_all.shape

    def body(dq_ref, c_ref, s1_ref, s2_ref, o_ref):
        c, s1, s2 = scale * c_ref[...], -scale * s1_ref[...], -scale * s2_ref[...]
        for h in range(W // LANES):
            o_ref[:, h * LANES:(h + 1) * LANES] = _rope_block(dq_ref[:, h * LANES:(h + 1) * LANES], c, s1, s2).astype(BF16)

    tab = pl.BlockSpec((tm, LANES), lambda i: (i, 0))
    row = pl.BlockSpec((tm, W), lambda i: (i, 0))
    return pl.pallas_call(body, name="rope_bwd", grid=(S // tm,), in_specs=[row, tab, tab, tab], out_specs=row,
                          out_shape=jax.ShapeDtypeStruct((S, W), BF16), compiler_params=_cparams(("parallel",)))(dq_all, *tabs)


ATT_SCALE = MLA_QK ** -0.5
LN2 = math.log(2.0)
Q_PRESCALE = ATT_SCALE / LN2
DQ_POSTSCALE = ATT_SCALE / LN2
N_PAIR = MLA_HEADS // 2


def _causal_mask(qi, ki, tq, tk):
    row = qi * tq + lax.broadcasted_iota(jnp.int32, (tq, tk), 0)
    col = ki * tk + lax.broadcasted_iota(jnp.int32, (tq, tk), 1)
    return col <= row


def _lane_tile(x, n):
    return jnp.concatenate([x] * n, axis=1) if n > 1 else x


def _attn_fwd(q_all, kv_all, tq=512, tk=1024):
    S = q_all.shape[0]
    tq, tk = min(tq, S), min(tk, S)
    nq, nk, nb, r = S // tq, S // tk, tk // LANES, tk // tq

    def body(q_ref, k_ref, v_ref, o_ref, lse_ref, m_s, l_s, acc_s):
        qi, ki = pl.program_id(1), pl.program_id(2)
        last = lax.div(qi, r)

        @pl.when(ki == 0)
        def _():
            m_s[...] = jnp.full(m_s.shape, NEG, F32)
            l_s[...] = jnp.zeros(l_s.shape, F32)
            acc_s[...] = jnp.zeros(acc_s.shape, F32)

        def block(kc, mask):
            v = v_ref[0:kc, :]
            for hh in range(2):
                q = q_ref[:, hh * LANES:(hh + 1) * LANES]
                k = k_ref[0:kc, hh * LANES:(hh + 1) * LANES]
                s = lax.dot_general(q, k, (((1,), (1,)), ((), ())), preferred_element_type=F32)
                if mask is not None:
                    s = jnp.where(mask, s, NEG)
                m_prev = m_s[hh]
                m_new = jnp.maximum(m_prev, jnp.max(s, axis=-1, keepdims=True))
                p = jnp.exp2(s - _lane_tile(m_new, kc // LANES))
                alpha = jnp.exp2(m_prev - m_new)
                ps = p[:, :LANES]
                for j in range(1, kc // LANES):
                    ps = ps + p[:, j * LANES:(j + 1) * LANES]
                l_s[hh] = alpha * l_s[hh] + ps
                acc_s[hh] = alpha * acc_s[hh] + jnp.dot(p.astype(BF16), v, preferred_element_type=F32)
                m_s[hh] = m_new

        if r == 2:
            @pl.when(ki < last)
            def _():
                block(tk, None)

            @pl.when((ki == last) & (lax.rem(qi, 2) == 0))
            def _():
                block(tq, _causal_mask(0, 0, tq, tq))

            @pl.when((ki == last) & (lax.rem(qi, 2) == 1))
            def _():
                block(tk, _causal_mask(qi, ki, tq, tk))
        else:
            @pl.when(ki <= last)
            def _():
                block(tk, _causal_mask(qi, ki, tq, tk))

        @pl.when(ki == last)
        def _():
            first = lax.broadcasted_iota(jnp.int32, (tq, LANES), 1) < MLA_V
            l0 = jnp.sum(l_s[0], axis=-1, keepdims=True)
            l1 = jnp.sum(l_s[1], axis=-1, keepdims=True)
            o_ref[...] = jnp.where(first, acc_s[0] / l0, acc_s[1] / l1)
            lse_ref[:, :LANES] = m_s[0] + jnp.log2(l0)
            lse_ref[:, LANES:] = m_s[1] + jnp.log2(l1)

    return pl.pallas_call(
        body, name="mla_attn_fwd", grid=(N_PAIR, nq, nk),
        in_specs=[pl.BlockSpec((tq, 2 * LANES), lambda p, qi, ki: (qi, p)),
                  pl.BlockSpec((tk, 2 * LANES), lambda p, qi, ki: (jnp.minimum(ki, lax.div(qi, r)), p)),
                  pl.BlockSpec((tk, LANES), lambda p, qi, ki: (jnp.minimum(ki, lax.div(qi, r)), MLA_HEADS + p))],
        out_specs=[pl.BlockSpec((tq, LANES), lambda p, qi, ki: (qi, p)), pl.BlockSpec((tq, 2 * LANES), lambda p, qi, ki: (qi, p))],
        out_shape=[jax.ShapeDtypeStruct((S, MLA_HEADS * MLA_V), F32), jax.ShapeDtypeStruct((S, MLA_HEADS * LANES), F32)],
        scratch_shapes=[pltpu.VMEM((2, tq, LANES), F32), pltpu.VMEM((2, tq, LANES), F32), pltpu.VMEM((2, tq, LANES), F32)],
        compiler_params=_cparams(("parallel", "parallel", "arbitrary")),
    )(q_all, kv_all, kv_all)


def _attn_delta(dcat, o, tm=512):
    S = o.shape[0]

    def body(do_ref, o_ref, d_ref):
        prod = do_ref[...] * o_ref[...]
        first = lax.broadcasted_iota(jnp.int32, (tm, LANES), 1) < MLA_V
        for p in range(N_PAIR):
            pp = prod[:, p * LANES:(p + 1) * LANES]
            d0 = jnp.sum(jnp.where(first, pp, 0.0), axis=-1, keepdims=True)
            d1 = jnp.sum(jnp.where(first, 0.0, pp), axis=-1, keepdims=True)
            d_ref[:, 2 * p * LANES:(2 * p + 1) * LANES] = jnp.broadcast_to(d0, (tm, LANES))
            d_ref[:, (2 * p + 1) * LANES:(2 * p + 2) * LANES] = jnp.broadcast_to(d1, (tm, LANES))

    W = MLA_HEADS * MLA_V
    return pl.pallas_call(
        body, name="mla_attn_delta", grid=(S // tm,),
        in_specs=[pl.BlockSpec((tm, W), lambda i: (i, 1)), pl.BlockSpec((tm, W), lambda i: (i, 0))],
        out_specs=pl.BlockSpec((tm, MLA_HEADS * LANES), lambda i: (i, 0)),
        out_shape=jax.ShapeDtypeStruct((S, MLA_HEADS * LANES), F32), compiler_params=_cparams(("parallel",)),
    )(dcat, o)


def _attn_bwd(q_all, kv_all, dcat, lse, delta, tq=512, tk=512):
    S = q_all.shape[0]
    tq, tk = min(tq, S), min(tk, S)
    nq, nk, nb = S // tq, S // tk, tk // LANES
    assert tq == tk

    def body(q_ref, k_ref, v_ref, do_ref, lse_ref, dl_ref, dq_ref, dk_ref, dv_ref, dk_s, dv_s):
        ki, qi = pl.program_id(1), pl.program_id(2)

        @pl.when((ki == 0) & (qi == 0))
        def _():
            dq_ref[...] = jnp.zeros(dq_ref.shape, F32)

        @pl.when(qi == 0)
        def _():
            dk_s[...] = jnp.zeros(dk_s.shape, F32)
            dv_s[...] = jnp.zeros(dv_s.shape, F32)

        @pl.when(qi >= ki)
        def _():
            v, do = v_ref[...], do_ref[...]
            first = lax.broadcasted_iota(jnp.int32, (tq, LANES), 1) < MLA_V
            firstk = lax.broadcasted_iota(jnp.int32, (tk, LANES), 1) < MLA_V
            mask = _causal_mask(qi, ki, tq, tk)
            do_b = do.astype(BF16)
            rows = pl.ds(pl.multiple_of(qi * tq, tq), tq)
            for hh in range(2):
                sl = slice(hh * LANES, (hh + 1) * LANES)
                q, k = q_ref[:, sl], k_ref[:, sl]
                s = lax.dot_general(q, k, (((1,), (1,)), ((), ())), preferred_element_type=F32)
                p = jnp.exp2(jnp.where(mask, s, NEG) - _lane_tile(lse_ref[:, sl], nb))
                do_h = jnp.where(first if hh == 0 else ~first, do, 0.0).astype(BF16)
                dp = lax.dot_general(do_h, v, (((1,), (1,)), ((), ())), preferred_element_type=F32)
                ds_b = (p * (dp - _lane_tile(dl_ref[:, sl], nb)) * LN2).astype(BF16)
                pv = lax.dot_general(p.astype(BF16), do_b, (((0,), (0,)), ((), ())), preferred_element_type=F32)
                dv_s[...] += jnp.where(firstk if hh == 0 else ~firstk, pv, 0.0)
                dk_s[:, sl] += lax.dot_general(ds_b, q, (((0,), (0,)), ((), ())), preferred_element_type=F32)
                dq_ref[rows, sl] += jnp.dot(ds_b, k, preferred_element_type=F32)

        @pl.when(qi == nq - 1)
        def _():
            dk_ref[...] = dk_s[...]
            dv_ref[...] = dv_s[...]

    wide = pl.BlockSpec((tq, 2 * LANES), lambda p, ki, qi: (jnp.maximum(qi, ki), p))
    return pl.pallas_call(
        body, name="mla_attn_bwd", grid=(N_PAIR, nk, nq),
        in_specs=[wide, pl.BlockSpec((tk, 2 * LANES), lambda p, ki, qi: (ki, p)),
                  pl.BlockSpec((tk, LANES), lambda p, ki, qi: (ki, MLA_HEADS + p)),
                  pl.BlockSpec((tq, LANES), lambda p, ki, qi: (jnp.maximum(qi, ki), N_PAIR + p)), wide, wide],
        out_specs=[pl.BlockSpec((S, 2 * LANES), lambda p, ki, qi: (0, p)),
                   pl.BlockSpec((tk, 2 * LANES), lambda p, ki, qi: (ki, p)), pl.BlockSpec((tk, LANES), lambda p, ki, qi: (ki, p))],
        out_shape=[jax.ShapeDtypeStruct((S, MLA_HEADS * LANES), F32), jax.ShapeDtypeStruct((S, MLA_HEADS * LANES), F32),
                   jax.ShapeDtypeStruct((S, MLA_HEADS * MLA_V), F32)],
        scratch_shapes=[pltpu.VMEM((tk, 2 * LANES), F32), pltpu.VMEM((tk, LANES), F32)],
        compiler_params=_cparams(("parallel", "arbitrary", "arbitrary")),
    )(q_all, kv_all, kv_all, dcat, lse, delta)


def _attn_p_ds(q, k, v, do, lse_h, delta_h, half_mask, mask):
    s = lax.dot_general(q, k, (((1,), (1,)), ((), ())), preferred_element_type=F32) * ATT_SCALE
    p = jnp.exp(jnp.where(mask, s, NEG) - lse_h)
    do_h = jnp.where(half_mask, do, 0.0).astype(BF16)
    dp = lax.dot_general(do_h, v, (((1,), (1,)), ((), ())), preferred_element_type=F32)
    ds = p * (dp - delta_h) * ATT_SCALE
    return p, ds


def _attn_deltas(do, o, first):
    prod = do * o
    return (jnp.sum(jnp.where(first, prod, 0.0), axis=-1, keepdims=True),
            jnp.sum(jnp.where(first, 0.0, prod), axis=-1, keepdims=True))


def _attn_bwd_dq(q_all, kv_all, dcat, o, lse, tq=512, tk=512):
    S = q_all.shape[0]
    nq, nk = S // tq, S // tk

    def body(q_ref, k_ref, v_ref, do_ref, o_ref, lse_ref, dq_ref, acc_s):
        qi, ki = pl.program_id(1), pl.program_id(2)

        @pl.when(ki == 0)
        def _():
            acc_s[...] = jnp.zeros(acc_s.shape, F32)

        @pl.when(ki <= qi)
        def _():
            v, do, lse_t = v_ref[...], do_ref[...], lse_ref[...]
            first = lax.broadcasted_iota(jnp.int32, (tq, LANES), 1) < MLA_V
            deltas = _attn_deltas(do, o_ref[...], first)
            mask = _causal_mask(qi, ki, tq, tk)
            for hh in range(2):
                k = k_ref[:, hh * LANES:(hh + 1) * LANES]
                _, ds = _attn_p_ds(q_ref[:, hh * LANES:(hh + 1) * LANES], k, v, do,
                                   lse_t[:, hh * MLA_V:hh * MLA_V + 1], deltas[hh], first if hh == 0 else ~first, mask)
                acc_s[:, hh * LANES:(hh + 1) * LANES] += jnp.dot(ds.astype(BF16), k, preferred_element_type=F32)

        @pl.when(ki == qi)
        def _():
            dq_ref[...] = acc_s[...]

    half = pl.BlockSpec((tq, LANES), lambda p, qi, ki: (qi, p))
    return pl.pallas_call(
        body, name="mla_attn_bwd_dq", grid=(N_PAIR, nq, nk),
        in_specs=[pl.BlockSpec((tq, 2 * LANES), lambda p, qi, ki: (qi, p)),
                  pl.BlockSpec((tk, 2 * LANES), lambda p, qi, ki: (jnp.minimum(ki, qi), p)),
                  pl.BlockSpec((tk, LANES), lambda p, qi, ki: (jnp.minimum(ki, qi), MLA_HEADS + p)),
                  pl.BlockSpec((tq, LANES), lambda p, qi, ki: (qi, N_PAIR + p)), half, half],
        out_specs=pl.BlockSpec((tq, 2 * LANES), lambda p, qi, ki: (qi, p)),
        out_shape=jax.ShapeDtypeStruct((S, MLA_HEADS * LANES), F32),
        scratch_shapes=[pltpu.VMEM((tq, 2 * LANES), F32)],
        compiler_params=_cparams(("parallel", "parallel", "arbitrary")),
    )(q_all, kv_all, kv_all, dcat, o, lse)


def _attn_bwd_dkv(q_all, kv_all, dcat, o, lse, tq=512, tk=512):
    S = q_all.shape[0]
    nq, nk = S // tq, S // tk

    def body(q_ref, k_ref, v_ref, do_ref, o_ref, lse_ref, dk_ref, dv_ref, dk_s, dv_s):
        ki, qi = pl.program_id(1), pl.program_id(2)

        @pl.when(qi == 0)
        def _():
            dk_s[...] = jnp.zeros(dk_s.shape, F32)
            dv_s[...] = jnp.zeros(dv_s.shape, F32)

        @pl.when(qi >= ki)
        def _():
            v, do, lse_t = v_ref[...], do_ref[...], lse_ref[...]
            first = lax.broadcasted_iota(jnp.int32, (tq, LANES), 1) < MLA_V
            firstk = lax.broadcasted_iota(jnp.int32, (tk, LANES), 1) < MLA_V
            deltas = _attn_deltas(do, o_ref[...], first)
            mask = _causal_mask(qi, ki, tq, tk)
            do_b = do.astype(BF16)
            for hh in range(2):
                q = q_ref[:, hh * LANES:(hh + 1) * LANES]
                p, ds = _attn_p_ds(q, k_ref[:, hh * LANES:(hh + 1) * LANES], v, do,
                                   lse_t[:, hh * MLA_V:hh * MLA_V + 1], deltas[hh], first if hh == 0 else ~first, mask)
                pv = lax.dot_general(p.astype(BF16), do_b, (((0,), (0,)), ((), ())), preferred_element_type=F32)
                dv_s[...] += jnp.where(firstk if hh == 0 else ~firstk, pv, 0.0)
                dk_s[:, hh * LANES:(hh + 1) * LANES] += lax.dot_general(ds.astype(BF16), q, (((0,), (0,)), ((), ())),
                                                                       preferred_element_type=F32)

        @pl.when(qi == nq - 1)
        def _():
            dk_ref[...] = dk_s[...]
            dv_ref[...] = dv_s[...]

    half = pl.BlockSpec((tq, LANES), lambda p, ki, qi: (jnp.maximum(qi, ki), p))
    return pl.pallas_call(
        body, name="mla_attn_bwd_dkv", grid=(N_PAIR, nk, nq),
        in_specs=[pl.BlockSpec((tq, 2 * LANES), lambda p, ki, qi: (jnp.maximum(qi, ki), p)),
                  pl.BlockSpec((tk, 2 * LANES), lambda p, ki, qi: (ki, p)),
                  pl.BlockSpec((tk, LANES), lambda p, ki, qi: (ki, MLA_HEADS + p)),
                  pl.BlockSpec((tq, LANES), lambda p, ki, qi: (jnp.maximum(qi, ki), N_PAIR + p)), half, half],
        out_specs=[pl.BlockSpec((tk, 2 * LANES), lambda p, ki, qi: (ki, p)), pl.BlockSpec((tk, LANES), lambda p, ki, qi: (ki, p))],
        out_shape=[jax.ShapeDtypeStruct((S, MLA_HEADS * LANES), F32), jax.ShapeDtypeStruct((S, MLA_HEADS * MLA_V), F32)],
        scratch_shapes=[pltpu.VMEM((tk, 2 * LANES), F32), pltpu.VMEM((tk, LANES), F32)],
        compiler_params=_cparams(("parallel", "parallel", "arbitrary")),
    )(q_all, kv_all, kv_all, dcat, o, lse)


MEM_SCALE = MEM_HEAD_DIM ** -0.5


def _mem_probs(q, k):
    s = lax.dot_general(q, k, (((1,), (1,)), ((), ())), preferred_element_type=F32) * MEM_SCALE
    e = jnp.exp(s - jnp.max(s, axis=-1, keepdims=True))
    return e / jnp.sum(e, axis=-1, keepdims=True)


def _mem_attn_fwd(qm, km, vm, tq=512):
    S, W = qm.shape
    M = km.shape[0]

    def body(q_ref, k_ref, v_ref, o_ref):
        for h in range(MEM_HEADS):
            sl = slice(h * MEM_HEAD_DIM, (h + 1) * MEM_HEAD_DIM)
            p = _mem_probs(q_ref[:, sl], k_ref[:, sl])
            o_ref[:, sl] = jnp.dot(p.astype(BF16), v_ref[:, sl], preferred_element_type=F32).astype(BF16)

    row = pl.BlockSpec((tq, W), lambda i: (i, 0))
    full = pl.BlockSpec((M, W), lambda i: (0, 0))
    return pl.pallas_call(body, name="mem_attn_fwd", grid=(S // tq,), in_specs=[row, full, full], out_specs=row,
                          out_shape=jax.ShapeDtypeStruct((S, W), BF16), compiler_params=_cparams(("parallel",)))(qm, km, vm)


def _mem_attn_bwd(qm, km, vm, dom, tq=512):
    S, W = qm.shape
    M = km.shape[0]

    def body(q_ref, k_ref, v_ref, do_ref, dq_ref, dk_ref, dv_ref):
        i = pl.program_id(0)

        @pl.when(i == 0)
        def _():
            dk_ref[...] = jnp.zeros(dk_ref.shape, F32)
            dv_ref[...] = jnp.zeros(dv_ref.shape, F32)

        for h in range(MEM_HEADS):
            sl = slice(h * MEM_HEAD_DIM, (h + 1) * MEM_HEAD_DIM)
            q, k, v, do = q_ref[:, sl], k_ref[:, sl], v_ref[:, sl], do_ref[:, sl]
            p = _mem_probs(q, k)
            dv_ref[:, sl] += lax.dot_general(p.astype(BF16), do, (((0,), (0,)), ((), ())), preferred_element_type=F32)
            dp = lax.dot_general(do, v, (((1,), (1,)), ((), ())), preferred_element_type=F32)
            ds = (p * (dp - jnp.sum(dp * p, axis=-1, keepdims=True)) * MEM_SCALE).astype(BF16)
            dq_ref[:, sl] = jnp.dot(ds, k, preferred_element_type=F32).astype(BF16)
            dk_ref[:, sl] += lax.dot_general(ds, q, (((0,), (0,)), ((), ())), preferred_element_type=F32)

    row = pl.BlockSpec((tq, W), lambda i: (i, 0))
    full = pl.BlockSpec((M, W), lambda i: (0, 0))
    return pl.pallas_call(
        body, name="mem_attn_bwd", grid=(S // tq,), in_specs=[row, full, full, row], out_specs=[row, full, full],
        out_shape=[jax.ShapeDtypeStruct((S, W), BF16), jax.ShapeDtypeStruct((M, W), F32), jax.ShapeDtypeStruct((M, W), F32)],
        compiler_params=_cparams(("arbitrary",)),
    )(qm, km, vm, dom)


L = SSD_CHUNK
N_SPAIR = SSD_HEADS // 2
GRP_W = SSD_INNER // 2
XB0, XC0 = SSD_INNER, SSD_INNER + 2 * SSD_STATE


def _cumsum_rows(a, reverse=False):
    row = lax.broadcasted_iota(jnp.int32, a.shape, 0)
    x, sft = a, 1
    while sft < L:
        if reverse:
            x = x + jnp.where(row < L - sft, pltpu.roll(x, L - sft, axis=0), 0.0)
        else:
            x = x + jnp.where(row >= sft, pltpu.roll(x, sft, axis=0), 0.0)
        sft *= 2
    return x


def _shift_down(cur, prev, s):
    if s == 0:
        return cur
    row = lax.broadcasted_iota(jnp.int32, cur.shape, 0)
    return jnp.where(row < s, pltpu.roll(prev, s, axis=0), pltpu.roll(cur, s, axis=0))


def _shift_up(cur, nxt, s):
    if s == 0:
        return cur
    row = lax.broadcasted_iota(jnp.int32, cur.shape, 0)
    return jnp.where(row >= L - s, pltpu.roll(nxt, L - s, axis=0), pltpu.roll(cur, L - s, axis=0))


def _ssd_conv(u, prev, cw, cb):
    delayed = [u] + [_shift_down(u, prev, s) for s in range(1, SSD_CONV)]
    conv = cb + cw[SSD_CONV - 1:SSD_CONV, :] * u
    for s in range(1, SSD_CONV):
        conv = conv + cw[SSD_CONV - 1 - s:SSD_CONV - s, :] * delayed[s]
    return conv, delayed


def _pair_lanes(v, h0, first):
    return jnp.where(first, v[:, h0:h0 + 1], v[:, h0 + 1:h0 + 2])


def _ssd_common(u, prev, dt_raw, cw, cb, dtb, alog):
    conv, delayed = _ssd_conv(u, prev, cw, cb)
    sg = _sigmoid(conv)
    xa = conv * sg
    dpre = dt_raw + dtb
    dtv = jnp.maximum(dpre, 0.0) + jnp.log1p(jnp.exp(-jnp.abs(dpre)))
    a_row = -jnp.exp(alog)
    cs = _cumsum_rows(dtv * a_row)
    return conv, sg, xa, dpre, dtv, a_row, cs, delayed


def _ssd_pair_fwd(xa, dtv, cs, csT, G, Cg, Bg, Sp, dsk, pp, first, tri, rowfirst):
    h0 = 2 * pp
    x = xa[:, pp * LANES:(pp + 1) * LANES]
    xdt = x * _pair_lanes(dtv, h0, first)
    xdt_b = xdt.astype(BF16)
    Ms, yd = [], []
    for h in (h0, h0 + 1):
        lam = jnp.exp(jnp.where(tri, cs[:, h:h + 1] - csT[h:h + 1, :], NEG))
        M = G * lam
        Ms.append((M, lam))
        yd.append(jnp.dot(M.astype(BF16), xdt_b, preferred_element_type=F32))
    T = lax.dot_general(Cg, Sp.astype(BF16), (((1,), (1,)), ((), ())), preferred_element_type=F32)
    E = jnp.exp(_pair_lanes(cs, h0, first))
    yoff = E * T
    csl = cs[L - 1:L, :]
    Fd = jnp.exp(_pair_lanes(csl, h0, first) - _pair_lanes(cs, h0, first))
    el = jnp.exp(csl)
    el_rows = jnp.where(rowfirst, el[:, h0:h0 + 1], el[:, h0 + 1:h0 + 2])
    Sloc = lax.dot_general((xdt * Fd).astype(BF16), Bg, (((0,), (0,)), ((), ())), preferred_element_type=F32)
    S_new = el_rows * Sp + Sloc
    y = jnp.where(first, yd[0], yd[1]) + yoff + x * _pair_lanes(dsk, h0, first[:1])
    return y, S_new, (x, xdt, xdt_b, Ms, E, yoff, Fd, el, el_rows)


def _ssd_masks():
    lane = lax.broadcasted_iota(jnp.int32, (L, LANES), 1)
    row = lax.broadcasted_iota(jnp.int32, (L, LANES), 0)
    return lane, row, lane < SSD_HEAD_DIM, row >= lane, row[:, :1] < SSD_HEAD_DIM


def _ssd_specs(nc, rev):
    def cidx(i):
        return nc - 1 - i if rev else i
    z = pl.BlockSpec((L, SSD_INNER), lambda i: (cidx(i), PZ // SSD_INNER))
    u = pl.BlockSpec((L, SSD_XBC), lambda i: (cidx(i), PX // SSD_XBC))
    dt = pl.BlockSpec((L, LANES), lambda i: (cidx(i), PDT // LANES))
    return cidx, z, u, dt


def _vec(w):
    return pl.BlockSpec((1, w), lambda i: (0, 0))


def _ssd_fwd(proj, cw, cb, dtb, alog, dsk, ng):
    S = proj.shape[0]
    nc = S // L

    def body(z_ref, u_ref, dt_ref, cw_ref, cb_ref, dtb_ref, alog_ref, dsk_ref, ng_ref, y_ref, st_ref, prev_s, state_s):
        c = pl.program_id(0)

        @pl.when(c == 0)
        def _():
            prev_s[...] = jnp.zeros(prev_s.shape, F32)
            state_s[...] = jnp.zeros(state_s.shape, F32)

        u = u_ref[...]
        _, _, xa, _, dtv, _, cs, _ = _ssd_common(u, prev_s[...], dt_ref[...], cw_ref[...], cb_ref[...], dtb_ref[...], alog_ref[...])
        prev_s[...] = u
        csT = cs.T
        _, _, first, tri, rowfirst = _ssd_masks()
        dsk_v = dsk_ref[...]
        ys = []
        for g in range(2):
            Bg = xa[:, XB0 + g * SSD_STATE:XB0 + (g + 1) * SSD_STATE].astype(BF16)
            Cg = xa[:, XC0 + g * SSD_STATE:XC0 + (g + 1) * SSD_STATE].astype(BF16)
            G = lax.dot_general(Cg, Bg, (((1,), (1,)), ((), ())), preferred_element_type=F32)
            for pp in (2 * g, 2 * g + 1):
                Sp = state_s[pp]
                st_ref[pp * LANES:(pp + 1) * LANES, :] = Sp
                y, S_new, _ = _ssd_pair_fwd(xa, dtv, cs, csT, G, Cg, Bg, Sp, dsk_v, pp, first, tri, rowfirst)
                state_s[pp] = S_new
                ys.append(y)
        z = z_ref[...]
        for g in range(2):
            yg = jnp.concatenate([ys[2 * g], ys[2 * g + 1]], axis=1)
            zg = z[:, g * GRP_W:(g + 1) * GRP_W]
            gated = yg * (zg * _sigmoid(zg))
            r = lax.rsqrt(jnp.mean(gated * gated, axis=-1, keepdims=True) + RMS_EPS)
            y_ref[:, g * GRP_W:(g + 1) * GRP_W] = gated * r * ng_ref[:, g * GRP_W:(g + 1) * GRP_W]

    _, zs, us, dts = _ssd_specs(nc, False)
    return pl.pallas_call(
        body, name="ssd_fwd", grid=(nc,),
        in_specs=[zs, us, dts, pl.BlockSpec((8, SSD_XBC), lambda i: (0, 0)), _vec(SSD_XBC), _vec(LANES), _vec(LANES), _vec(LANES),
                  _vec(SSD_INNER)],
        out_specs=[pl.BlockSpec((L, SSD_INNER), lambda i: (i, 0)), pl.BlockSpec((N_SPAIR * LANES, SSD_STATE), lambda i: (i, 0))],
        out_shape=[jax.ShapeDtypeStruct((S, SSD_INNER), F32), jax.ShapeDtypeStruct((nc * N_SPAIR * LANES, SSD_STATE), F32)],
        scratch_shapes=[pltpu.VMEM((L, SSD_XBC), F32), pltpu.VMEM((N_SPAIR, LANES, SSD_STATE), F32)],
        compiler_params=_cparams(("arbitrary",)),
    )(proj, proj, proj, cw, cb, dtb, alog, dsk, ng)


def _ssd_bwd(proj, states, dy, cw, cb, dtb, alog, dsk, ng):
    S = proj.shape[0]
    nc = S // L

    def body(z_ref, u_ref, up_ref, dt_ref, st_ref, dy_ref, cw_ref, cb_ref, dtb_ref, alog_ref, dsk_ref, ng_ref,
             dz_ref, du_ref, ddt_ref, dcw_ref, dcb_ref, ddtb_ref, dalog_ref, ddsk_ref, dng_ref,
             dS_s, dconv_s, dD_s):
        i = pl.program_id(0)
        c = nc - 1 - i

        @pl.when(i == 0)
        def _():
            dS_s[...] = jnp.zeros(dS_s.shape, F32)
            dconv_s[...] = jnp.zeros(dconv_s.shape, F32)
            dD_s[...] = jnp.zeros(dD_s.shape, F32)
            for r in (dcw_ref, dcb_ref, ddtb_ref, dalog_ref, ddsk_ref, dng_ref):
                r[...] = jnp.zeros(r.shape, F32)

        u = u_ref[...]
        prev = jnp.where(c > 0, up_ref[...], 0.0)
        cw_v = cw_ref[...]
        conv, sg, xa, dpre, dtv, a_row, cs, delayed = _ssd_common(u, prev, dt_ref[...], cw_v, cb_ref[...], dtb_ref[...], alog_ref[...])
        csT = cs.T
        lane, row, first, tri, rowfirst = _ssd_masks()
        dsk_v = dsk_ref[...]

        fw = []
        Gs, Bs, Cs = [], [], []
        for g in range(2):
            Bg = xa[:, XB0 + g * SSD_STATE:XB0 + (g + 1) * SSD_STATE].astype(BF16)
            Cg = xa[:, XC0 + g * SSD_STATE:XC0 + (g + 1) * SSD_STATE].astype(BF16)
            G = lax.dot_general(Cg, Bg, (((1,), (1,)), ((), ())), preferred_element_type=F32)
            Gs.append(G), Bs.append(Bg), Cs.append(Cg)
            for pp in (2 * g, 2 * g + 1):
                Sp = st_ref[pp * LANES:(pp + 1) * LANES, :]
                y, _, keep = _ssd_pair_fwd(xa, dtv, cs, csT, G, Cg, Bg, Sp, dsk_v, pp, first, tri, rowfirst)
                fw.append((y, Sp, keep))

        z = z_ref[...]
        dys = []
        for g in range(2):
            sl = slice(g * GRP_W, (g + 1) * GRP_W)
            yg = jnp.concatenate([fw[2 * g][0], fw[2 * g + 1][0]], axis=1)
            zg = z[:, sl]
            sz = _sigmoid(zg)
            silu_z = zg * sz
            gated = yg * silu_z
            r = lax.rsqrt(jnp.mean(gated * gated, axis=-1, keepdims=True) + RMS_EPS)
            nh = gated * r
            dout = dy_ref[:, sl]
            dng_ref[:, sl] += jnp.sum(dout * nh, axis=0, keepdims=True)
            dnh = dout * ng_ref[:, sl]
            dgated = r * (dnh - nh * jnp.mean(dnh * nh, axis=-1, keepdims=True))
            dz_ref[:, sl] = dgated * yg * (sz * (1.0 + zg * (1.0 - sz)))
            dyg = dgated * silu_z
            dys.append(dyg[:, :LANES]), dys.append(dyg[:, LANES:])

        dcs_col = [0.0] * SSD_HEADS
        dcs_row = [None] * SSD_HEADS
        ddt_col = [None] * SSD_HEADS
        dxs = []
        dB, dC = [None, None], [None, None]
        last = row == L - 1
        for g in range(2):
            Bg, Cg, G = Bs[g], Cs[g], Gs[g]
            dG = jnp.zeros((L, L), F32)
            dBg = jnp.zeros((L, SSD_STATE), F32)
            dCg = jnp.zeros((L, SSD_STATE), F32)
            for pp in (2 * g, 2 * g + 1):
                h0 = 2 * pp
                y, Sp, (x, xdt, xdt_b, Ms, E, yoff, Fd, el, el_rows) = fw[pp]
                dY = dys[pp]
                dS = dS_s[pp]
                dS_b, Sp_b = dS.astype(BF16), Sp.astype(BF16)
                dD_s[:, pp * LANES:(pp + 1) * LANES] += jnp.sum(dY * x, axis=0, keepdims=True)
                dx = dY * _pair_lanes(dsk_v, h0, first[:1])
                dxdt = jnp.zeros((L, LANES), F32)
                dY_b = dY.astype(BF16)
                for hh, h in enumerate((h0, h0 + 1)):
                    hm = first if hh == 0 else ~first
                    M, lam = Ms[hh]
                    dYh = jnp.where(hm, dY, 0.0).astype(BF16)
                    dM = lax.dot_general(dYh, xdt_b, (((1,), (1,)), ((), ())), preferred_element_type=F32)
                    W = dM * M
                    dcs_col[h] = dcs_col[h] + jnp.sum(W, axis=-1, keepdims=True)
                    dcs_row[h] = jnp.sum(W, axis=0, keepdims=True)
                    dG = dG + dM * lam
                    mt = lax.dot_general(M.astype(BF16), dY_b, (((0,), (0,)), ((), ())), preferred_element_type=F32)
                    dxdt = dxdt + jnp.where(hm, mt, 0.0)
                dT = (E * dY).astype(BF16)
                dCg = dCg + jnp.dot(dT, Sp_b, preferred_element_type=F32)
                dS_in = lax.dot_general(dT, Cg, (((0,), (0,)), ((), ())), preferred_element_type=F32) + el_rows * dS
                q1 = dY * yoff
                dZ = lax.dot_general(Bg, dS_b, (((1,), (1,)), ((), ())), preferred_element_type=F32)
                dBg = dBg + jnp.dot((xdt * Fd).astype(BF16), dS_b, preferred_element_type=F32)
                dxdt = dxdt + dZ * Fd
                q2 = dZ * xdt * Fd
                dSS = dS * Sp
                for hh, h in enumerate((h0, h0 + 1)):
                    hm = first if hh == 0 else ~first
                    rs1 = jnp.sum(jnp.where(hm, q1, 0.0), axis=-1, keepdims=True)
                    rs2 = jnp.sum(jnp.where(hm, q2, 0.0), axis=-1, keepdims=True)
                    rmask = rowfirst if hh == 0 else ~rowfirst
                    d_el = jnp.sum(jnp.sum(jnp.where(rmask, dSS, 0.0), axis=-1, keepdims=True), axis=0, keepdims=True)
                    tail = jnp.sum(rs2, axis=0, keepdims=True) + d_el * el[:, h:h + 1]
                    dcs_col[h] = dcs_col[h] + (rs1 - rs2 + jnp.where(last[:, :1], tail, 0.0))
                    ddt_col[h] = jnp.sum(jnp.where(hm, dxdt * x, 0.0), axis=-1, keepdims=True)
                dS_s[pp] = dS_in
                dxs.append(dx + dxdt * _pair_lanes(dtv, h0, first))
            dG_b = dG.astype(BF16)
            dC[g] = dCg + jnp.dot(dG_b, Bg, preferred_element_type=F32)
            dB[g] = dBg + lax.dot_general(dG_b, Cg, (((0,), (0,)), ((), ())), preferred_element_type=F32)

        dcs_c, dcs_r, ddt_c = (jnp.zeros((L, LANES), F32) for _ in range(3))
        for h in range(SSD_HEADS):
            dcs_c = dcs_c + jnp.where(lane == h, dcs_col[h], 0.0)
            dcs_r = dcs_r + jnp.where(row == h, dcs_row[h], 0.0)
            ddt_c = ddt_c + jnp.where(lane == h, ddt_col[h], 0.0)
        dcs = dcs_c - dcs_r.T
        da = _cumsum_rows(dcs, reverse=True)
        ddt_c = ddt_c + da * a_row
        dalog_ref[...] += jnp.sum(da * dtv, axis=0, keepdims=True) * a_row
        ddt_raw = ddt_c * _sigmoid(dpre)
        ddt_ref[...] = ddt_raw
        ddtb_ref[...] += jnp.sum(ddt_raw, axis=0, keepdims=True)

        dxa = jnp.concatenate(dxs + dB + dC, axis=1)
        dconv = dxa * (sg * (1.0 + conv * (1.0 - sg)))
        dcb_ref[...] += jnp.sum(dconv, axis=0, keepdims=True)
        nxt = dconv_s[...]
        du = cw_v[SSD_CONV - 1:SSD_CONV, :] * dconv
        dcw_ref[SSD_CONV - 1:SSD_CONV, :] += jnp.sum(dconv * u, axis=0, keepdims=True)
        for s in range(1, SSD_CONV):
            k = SSD_CONV - 1 - s
            du = du + cw_v[k:k + 1, :] * _shift_up(dconv, nxt, s)
            dcw_ref[k:k + 1, :] += jnp.sum(dconv * delayed[s], axis=0, keepdims=True)
        du_ref[...] = du
        dconv_s[...] = dconv

        @pl.when(i == nc - 1)
        def _():
            acc = dD_s[...]
            lane1 = lax.broadcasted_iota(jnp.int32, (1, LANES), 1)
            lanew = lax.broadcasted_iota(jnp.int32, acc.shape, 1)
            out = jnp.zeros((1, LANES), F32)
            for h in range(SSD_HEADS):
                tot = jnp.sum(jnp.where((lanew >= h * SSD_HEAD_DIM) & (lanew < (h + 1) * SSD_HEAD_DIM), acc, 0.0),
                              axis=-1, keepdims=True)
                out = out + jnp.where(lane1 == h, tot, 0.0)
            ddsk_ref[...] = out

    cidx, zs, us, dts = _ssd_specs(nc, True)
    ups = pl.BlockSpec((L, SSD_XBC), lambda i: (jnp.maximum(cidx(i) - 1, 0), PX // SSD_XBC))
    rowc = lambda w: pl.BlockSpec((L, w), lambda i: (cidx(i), 0))
    return pl.pallas_call(
        body, name="ssd_bwd", grid=(nc,),
        in_specs=[zs, us, ups, dts, pl.BlockSpec((N_SPAIR * LANES, SSD_STATE), lambda i: (cidx(i), 0)), rowc(SSD_INNER),
                  pl.BlockSpec((8, SSD_XBC), lambda i: (0, 0)), _vec(SSD_XBC), _vec(LANES), _vec(LANES), _vec(LANES), _vec(SSD_INNER)],
        out_specs=[rowc(SSD_INNER), rowc(SSD_XBC), rowc(LANES), pl.BlockSpec((8, SSD_XBC), lambda i: (0, 0)), _vec(SSD_XBC),
                   _vec(LANES), _vec(LANES), _vec(LANES), _vec(SSD_INNER)],
        out_shape=[jax.ShapeDtypeStruct((S, SSD_INNER), F32), jax.ShapeDtypeStruct((S, SSD_XBC), F32),
                   jax.ShapeDtypeStruct((S, LANES), F32), jax.ShapeDtypeStruct((8, SSD_XBC), F32),
                   jax.ShapeDtypeStruct((1, SSD_XBC), F32), jax.ShapeDtypeStruct((1, LANES), F32),
                   jax.ShapeDtypeStruct((1, LANES), F32), jax.ShapeDtypeStruct((1, LANES), F32),
                   jax.ShapeDtypeStruct((1, SSD_INNER), F32)],
        scratch_shapes=[pltpu.VMEM((N_SPAIR, LANES, SSD_STATE), F32), pltpu.VMEM((L, SSD_XBC), F32),
                        pltpu.VMEM((1, SSD_INNER), F32)],
        compiler_params=_cparams(("arbitrary",)),
    )(proj, proj, proj, proj, states, dy, cw, cb, dtb, alog, dsk, ng)


_IN_SEGS = ((PZ, 0, 512), (PX, 512, 1024), (PDT, 1536, 8), (PQ, 1544, 384), (PKV, 1928, 256), (PKR + KR_LANE, 2184, 32))


def _pad_w_in(w):
    parts, at = [], 0
    for dst, src, n in sorted(_IN_SEGS):
        parts += [jnp.zeros((w.shape[0], dst - at), w.dtype), w[:, src:src + n]]
        at = dst + n
    return jnp.concatenate(parts + [jnp.zeros((w.shape[0], PW - at), w.dtype)], axis=1)


def _unpad_w_in(wp):
    segs = sorted(_IN_SEGS, key=lambda t: t[1])
    return jnp.concatenate([wp[:, dst:dst + n] for dst, src, n in segs], axis=1)


def _pad_w_q(w):
    return jnp.pad(w.reshape(MLA_Q_RANK, MLA_HEADS, MLA_QK), ((0, 0), (0, 0), (0, LANES - MLA_QK))).reshape(MLA_Q_RANK, MLA_HEADS * LANES)


def _unpad_w_q(wp):
    return wp.reshape(MLA_Q_RANK, MLA_HEADS, LANES)[:, :, :MLA_QK].reshape(MLA_Q_RANK, MLA_HEADS * MLA_QK)


def _pad_w_kv(w):
    w3 = w.reshape(MLA_KV_RANK, MLA_HEADS, MLA_NOPE + MLA_V)
    k = jnp.pad(w3[:, :, :MLA_NOPE], ((0, 0), (0, 0), (0, LANES - MLA_NOPE))).reshape(MLA_KV_RANK, MLA_HEADS * LANES)
    return jnp.concatenate([k, w3[:, :, MLA_NOPE:].reshape(MLA_KV_RANK, MLA_HEADS * MLA_V)], axis=1)


def _unpad_w_kv(wp):
    k = wp[:, :MLA_HEADS * LANES].reshape(MLA_KV_RANK, MLA_HEADS, LANES)[:, :, :MLA_NOPE]
    v = wp[:, MLA_HEADS * LANES:].reshape(MLA_KV_RANK, MLA_HEADS, MLA_V)
    return jnp.concatenate([k, v], axis=2).reshape(MLA_KV_RANK, MLA_HEADS * (MLA_NOPE + MLA_V))


def _head_lanes(v):
    return jnp.pad(v, ((0, 0), (0, LANES - v.shape[1])))


def _local_step(x, mem, positions, tgt, P, weights_at=None, emit=None):
    tabs = _rope_tables(positions)
    G = {}
    emit = emit or (lambda names, grads: 0.0)

    h0, h0b = _ln_fwd([(x, 1.0)], P["ln_in_g"], P["ln_in_b"], "ln_in")
    if weights_at is not None:
        P = {**P, **weights_at("first", h0b)}
    proj = _mm(h0b, P["w_in"], "nn", "proj_in", tm=1024, tn=640)
    if weights_at is not None:
        P = {**P, **weights_at("mid", proj)}
    y_ssd, states = _ssd_fwd(proj, P["conv_w"], P["conv_b"], P["dt_bias"], P["a_log"], P["d_skip"], P["ssd_norm_g"])
    qn, kvn, kpe = _mla_prep(proj, P["q_norm_g"], P["kv_norm_g"], tabs)

    def q_epi(acc, c, s1, s2):
        return (jnp.concatenate([_rope_block(acc[:, h * LANES:(h + 1) * LANES], c, s1, s2) for h in range(MLA_HEADS)], axis=1)
                * Q_PRESCALE,)

    q_all = _mm(qn, P["w_q_up"], "nn", "q_up", out_dtypes=(BF16,), epi=q_epi, extras=[(t, "m") for t in tabs])

    def kv_epi(acc, kp):
        kb = [acc[:, h * LANES:(h + 1) * LANES] + kp for h in range(MLA_HEADS)]
        return (jnp.concatenate(kb + [acc[:, MLA_HEADS * LANES:]], axis=1),)

    kv_all = _mm(kvn, P["w_kv_up"], "nn", "kv_up", out_dtypes=(BF16,), epi=kv_epi, extras=[(kpe, "m")])
    o_att, lse = _attn_fwd(q_all, kv_all)
    cat = jnp.concatenate([y_ssd, o_att], axis=1).astype(BF16)
    def resid_epi(acc, h):
        return (ALPHA * h + acc,)

    r1 = _mm(cat, P["w_mix_out"], "nn", "mix_out", tm=1024, epi=resid_epi, extras=[(h0, "mn")])
    h1, h1b = _ln_fwd([(r1, 1.0)], P["ln1_g"], P["ln1_b"], "ln1")
    if weights_at is not None:
        P = {**P, **weights_at("late", h1b)}
    qm = _mm(h1b, P["w_mem_q"], "nn", "mem_q", tm=1024, out_dtypes=(BF16,))
    km = _mm(mem, P["w_mem_k"], "nn", "mem_k", out_dtypes=(BF16,))
    vm = _mm(mem, P["w_mem_v"], "nn", "mem_v", out_dtypes=(BF16,))
    om = _mem_attn_fwd(qm, km, vm)
    r2 = _mm(om, P["w_mem_o"], "nn", "mem_o", tm=1024, epi=resid_epi, extras=[(h1, "mn")])
    h2, h2b = _ln_fwd([(r2, 1.0)], P["ln2_g"], P["ln2_b"], "ln2")

    def up_epi(acc):
        r = jnp.maximum(acc, 0.0)
        return (r * r,)

    act = _mm(h2b, P["w_up"], "nn", "mlp_up", tm=1024, tn=1024, out_dtypes=(BF16,), epi=up_epi)
    r3 = _mm(act, P["w_down"], "nn", "mlp_down", tm=1024, tk=D_FF, epi=resid_epi, extras=[(h2, "mn")])

    loss, dr3, dr3b, G["ln3_g"], G["ln3_b"] = _ln_loss_bwd([(r3, 1.0)], P["ln3_g"], P["ln3_b"], tgt, "ln3_loss")

    def dact_epi(acc, a):
        return (acc * (2.0 * jnp.sqrt(a.astype(F32))),)

    du = _mm(dr3b, P["w_down"], "nt", "mlp_down_dx", tm=1024, tn=1024, out_dtypes=(BF16,), epi=dact_epi, extras=[(act, "mn")])
    G["w_down"] = _mm(act, dr3b, "tn", "mlp_down_dw", tm=1024, tk=TOK_K, out_dtypes=(BF16,))
    G["w_up"] = _mm(h2b, du, "tn", "mlp_up_dw", tm=1024, tn=D_FF // N_DEV, tk=TOK_K, out_dtypes=(BF16,), col_slots=True)
    tie = emit(("w_down", "w_up"), G)
    dh2 = _mm(du, P["w_up"], "nt", "mlp_up_dx", tm=1024, tk=D_FF, epi=resid_epi, extras=[(dr3, "mn")])
    dr2, dr2b, G["ln2_g"], G["ln2_b"] = _ln_bwd([(r2, 1.0)], [(dh2, 1.0)], P["ln2_g"] + tie, "ln2_bwd")

    dom = _mm(dr2b, P["w_mem_o"], "nt", "mem_o_dx", tm=1024, out_dtypes=(BF16,))
    G["w_mem_o"] = _mm(om, dr2b, "tn", "mem_o_dw", tm=1024, tk=TOK_K, out_dtypes=(BF16,))
    dqm, dkm, dvm = _mem_attn_bwd(qm, km, vm, dom)
    G["w_mem_q"] = _mm(h1b, dqm, "tn", "mem_q_dw", tm=1024, tk=TOK_K, out_dtypes=(BF16,))
    G["w_mem_k"] = _mm(mem, dkm, "tn", "mem_k_dw", tm=1024, out_dtypes=(BF16,))
    G["w_mem_v"] = _mm(mem, dvm, "tn", "mem_v_dw", tm=1024, out_dtypes=(BF16,))
    tie = emit(("w_mem_o", "w_mem_q", "w_mem_k", "w_mem_v"), G)
    dh1 = _mm(dqm, P["w_mem_q"], "nt", "mem_q_dx", tm=1024, epi=resid_epi, extras=[(dr2, "mn")])
    dr1, dr1b, G["ln1_g"], G["ln1_b"] = _ln_bwd([(r1, 1.0)], [(dh1, 1.0)], P["ln1_g"] + tie, "ln1_bwd")

    dcat = _mm(dr1b, P["w_mix_out"], "nt", "mix_out_dx", tm=1024)
    G["w_mix_out"] = _mm(cat, dr1b, "tn", "mix_out_dw", tm=1024, tk=TOK_K, out_dtypes=(BF16,))
    dq_all, dk_all, dv_all = _attn_bwd(q_all, kv_all, dcat, lse, _attn_delta(dcat, o_att))
    dq_pre = _rope_bwd_all(dq_all, tabs, DQ_POSTSCALE)
    G["w_q_up"] = _mm(qn, dq_pre, "tn", "q_up_dw", tk=TOK_K, out_dtypes=(BF16,))
    dqn = _mm(dq_pre, P["w_q_up"], "nt", "q_up_dx", tm=1024)
    dkv_all = jnp.concatenate([dk_all, dv_all], axis=1).astype(BF16)
    G["w_kv_up"] = _mm(kvn, dkv_all, "tn", "kv_up_dw", tk=TOK_K, out_dtypes=(BF16,))
    dkvn = _mm(dkv_all, P["w_kv_up"], "nt", "kv_up_dx", tm=1024)
    dql, dkvl, dkr, G["q_norm_g"], G["kv_norm_g"] = _mla_prep_bwd(proj, P["q_norm_g"], P["kv_norm_g"], tabs, dqn, dkvn, dk_all)
    (dz, dxbc, ddt, G["conv_w"], G["conv_b"], G["dt_bias"], G["a_log"], G["d_skip"], G["ssd_norm_g"]) = _ssd_bwd(
        proj, states, dcat, P["conv_w"], P["conv_b"], P["dt_bias"], P["a_log"], P["d_skip"], P["ssd_norm_g"])
    tie = emit(("w_mix_out", "w_q_up", "w_kv_up", "conv_w"), G)
    S = x.shape[0]
    dproj = jnp.concatenate([dql, jnp.zeros((S, PZ - MLA_Q_RANK), F32) + tie, dz, dxbc, dkvl, ddt, dkr], axis=1).astype(BF16)
    G["w_in"] = _mm(h0b, dproj, "tn", "proj_in_dw", tm=1024, tn=640, tk=TOK_K, out_dtypes=(BF16,))
    tie = emit(("w_in",), G)
    dh0 = _mm(dproj, P["w_in"], "nt", "proj_in_dx", tm=1024, tk=PW, epi=lambda acc, t, d: (acc + t + ALPHA * d,),
              extras=[(jnp.zeros((1, D_MODEL), F32) + tie, "n"), (dr1, "mn")])
    gx, _, G["ln_in_g"], G["ln_in_b"] = _ln_bwd([(x, 1.0)], [(dh0, 1.0)], P["ln_in_g"] + tie, "ln_in_bwd")
    return loss, gx, G


PACK_W = 1024
BIG = (("w_in", (1024, 277), 1), ("conv_w", (4, 128), 1), ("w_q_up", (384, 96), 1), ("w_kv_up", (256, 128), 1),
       ("w_mix_out", (128, 1024), 0), ("w_mem_q", (128, 1024), 0), ("w_mem_k", (128, 1024), 0), ("w_mem_v", (128, 1024), 0),
       ("w_mem_o", (128, 1024), 0), ("w_up", (1024, 512), 1), ("w_down", (512, 1024), 0))
SMALL = ("ln_in_g", "ln_in_b", "conv_b", "ln1_g", "ln1_b", "ln2_g", "ln2_b", "ln3_g", "ln3_b",
         "ssd_norm_g", "q_norm_g", "kv_norm_g", "dt_bias", "a_log", "d_skip")
ALL_W = ("ln_in_g", "ln_in_b", "w_in", "conv_w", "conv_b", "dt_bias", "a_log", "d_skip", "ssd_norm_g", "q_norm_g", "w_q_up",
         "kv_norm_g", "w_kv_up", "w_mix_out", "ln1_g", "ln1_b", "w_mem_q", "w_mem_k", "w_mem_v", "w_mem_o", "ln2_g", "ln2_b",
         "w_up", "w_down", "ln3_g", "ln3_b")


def _rows_of(shape):
    return -(-(shape[0] * shape[1]) // PACK_W)


BIG_ROWS = sum(_rows_of(s) for _, s, _ in BIG)
BIG_R = -(-BIG_ROWS // 32) * 32
SMALL_R = 16
LOSS_ROW = 15
ADAM_TILE = 96
assert BIG_R % ADAM_TILE == 0


def _flat_rows(a, rows):
    lead = a.shape[:-2]
    f = a.reshape(lead + (-1,))
    f = jnp.pad(f, [(0, 0)] * len(lead) + [(0, rows * PACK_W - f.shape[-1])])
    return f.reshape(lead + (rows, PACK_W))


def _pack_big(arrs, dtype):
    parts = []
    for n, s, _ in BIG:
        if n == "conv_w" and dtype == BF16:
            parts.append(lax.bitcast_convert_type(arrs[n].astype(F32), BF16).reshape(1, PACK_W))
        else:
            parts.append(_flat_rows(arrs[n].astype(dtype), _rows_of(s)))
    parts.append(jnp.zeros((BIG_R - BIG_ROWS, PACK_W), dtype))
    return jnp.concatenate(parts, axis=0)


def _unpack_big(buf):
    out, r = {}, 0
    for n, s, _ in BIG:
        k = _rows_of(s)
        out[n] = buf[r:r + k].reshape(-1)[:s[0] * s[1]].reshape(s)
        r += k
    return out


def _unpack_gathered(buf):
    out, r = {}, 0
    for n, s, ax in BIG:
        k = _rows_of(s)
        if n == "conv_w":
            sh = lax.bitcast_convert_type(buf[:, r:r + k].reshape((N_DEV,) + s + (2,)), F32)
        else:
            sh = buf[:, r:r + k].reshape(N_DEV, -1)[:, :s[0] * s[1]].reshape((N_DEV,) + s)
        out[n] = sh.reshape(N_DEV * s[0], s[1]) if ax == 0 else sh.transpose(1, 0, 2).reshape(s[0], N_DEV * s[1])
        r += k
    return out


def _pack_grads(G):
    parts = []
    for n, s, ax in BIG:
        g = G[n]
        sh = g.reshape((N_DEV,) + s) if ax == 0 else g.reshape(s[0], N_DEV, s[1]).transpose(1, 0, 2)
        parts.append(_flat_rows(sh, _rows_of(s)))
    parts.append(jnp.zeros((N_DEV, BIG_R - BIG_ROWS, PACK_W), F32))
    return jnp.concatenate(parts, axis=1)


_SMALL_ROWS = (("ln_in_g",), ("ln_in_b",), ("conv_b",), ("ln1_g",), ("ln1_b",), ("ln2_g",), ("ln2_b",), ("ln3_g",), ("ln3_b",),
               ("ssd_norm_g", "q_norm_g"), ("kv_norm_g", "dt_bias", "a_log", "d_skip"))
_SMALL_W = {"ssd_norm_g": 512, "q_norm_g": 384, "kv_norm_g": 256, "dt_bias": LANES, "a_log": LANES, "d_skip": LANES}


def _pack_small(V, extra_row=None):
    rows = []
    for names in _SMALL_ROWS:
        r = jnp.concatenate([V[n] for n in names], axis=1)
        rows.append(jnp.pad(r, ((0, 0), (0, PACK_W - r.shape[1]))))
    rows.append(jnp.zeros((SMALL_R - len(rows) - 1, PACK_W), F32))
    rows.append(jnp.zeros((1, PACK_W), F32) if extra_row is None else extra_row)
    return jnp.concatenate(rows, axis=0)


def _unpack_small(buf):
    out = {}
    for i, names in enumerate(_SMALL_ROWS):
        c = 0
        for n in names:
            w = _SMALL_W.get(n, PACK_W)
            out[n] = buf[i:i + 1, c:c + w]
            c += w
    return out


MESH = pl.DeviceIdType.MESH
ANY = pl.BlockSpec(memory_space=pl.ANY)
VM = pl.BlockSpec(memory_space=pltpu.VMEM)


def _coords():
    return lax.axis_index("x"), lax.axis_index("y"), lax.axis_index("c")


def _slot(px, py, pc):
    return 4 * px + 2 * py + pc


def _gather_big(shard):
    R, W = shard.shape

    def body(x_ref, out_ref, send_sems, recv_sems, local_sem):
        x, y, c = _coords()
        me, sibling = (x, y, c), (x, y, 1 - c)
        chips = [(1 - x, y), (x, 1 - y), (1 - x, 1 - y)]

        def rows(px, py, pc):
            return out_ref.at[_slot(px, py, pc)]

        def copy(k, block, to, src=None):
            return pltpu.make_async_remote_copy(
                src_ref=rows(*block) if src is None else src, dst_ref=rows(*block),
                send_sem=send_sems.at[k], recv_sem=recv_sems.at[k], device_id=to, device_id_type=MESH)

        mine = pltpu.make_async_copy(x_ref, rows(*me), local_sem)
        mine.start()
        first = [copy(0, me, sibling, src=x_ref)]
        first += [copy(1 + j, me, (*chip, c), src=x_ref) for j, chip in enumerate(chips)]
        for cp in first:
            cp.start()
        passed = [copy(4 + j, (*chip, c), sibling) for j, chip in enumerate(chips)]
        for j, chip in enumerate(chips):
            copy(1 + j, (*chip, c), me).wait_recv()
            passed[j].start()
        copy(0, sibling, me).wait_recv()
        for j, chip in enumerate(chips):
            copy(4 + j, (*chip, 1 - c), me).wait_recv()
        for cp in first + passed:
            cp.wait_send()
        mine.wait()

    return pl.pallas_call(
        body, name="gather_weights", out_shape=jax.ShapeDtypeStruct((N_DEV, R, W), shard.dtype),
        in_specs=[ANY], out_specs=ANY,
        scratch_shapes=[pltpu.SemaphoreType.DMA((7,)), pltpu.SemaphoreType.DMA((7,)), pltpu.SemaphoreType.DMA],
    )(shard)


def _peer(k, x, y, c):
    dx, dy, dc = (k >> 2) & 1, (k >> 1) & 1, k & 1
    return (1 - x if dx else x, 1 - y if dy else y, 1 - c if dc else c)


def _scatter_grads(gpack):
    _, R, W = gpack.shape

    def body(g_ref, out_ref, send_sems, recv_sems, local_sem):
        x, y, c = _coords()
        my = _slot(x, y, c)
        mine = pltpu.make_async_copy(g_ref.at[my], out_ref.at[my], local_sem)
        mine.start()
        cps = []
        for k in range(1, N_DEV):
            peer = _peer(k, x, y, c)
            cps.append(pltpu.make_async_remote_copy(
                src_ref=g_ref.at[_slot(*peer)], dst_ref=out_ref.at[my],
                send_sem=send_sems.at[k - 1], recv_sem=recv_sems.at[k - 1], device_id=peer, device_id_type=MESH))
        for cp in cps:
            cp.start()
        for cp in cps:
            cp.wait_recv()
        for cp in cps:
            cp.wait_send()
        mine.wait()

    return pl.pallas_call(
        body, name="scatter_grads", out_shape=jax.ShapeDtypeStruct(gpack.shape, gpack.dtype),
        in_specs=[ANY], out_specs=ANY,
        scratch_shapes=[pltpu.SemaphoreType.DMA((7,)), pltpu.SemaphoreType.DMA((7,)), pltpu.SemaphoreType.DMA],
    )(gpack)


def _adam(w, g, m, v):
    m = ADAM_B1 * m + (1.0 - ADAM_B1) * g
    v = ADAM_B2 * v + (1.0 - ADAM_B2) * (g * g)
    m_hat = m / (1.0 - ADAM_B1 ** ADAM_STEP)
    v_hat = v / (1.0 - ADAM_B2 ** ADAM_STEP)
    delta = -ADAM_LR * (m_hat / (jnp.sqrt(v_hat) + ADAM_EPS) + ADAM_WD * w)
    return delta, m, v


def _sum_slots(ref):
    tot = ref[0].astype(F32)
    for q in range(1, N_DEV):
        tot = tot + ref[q].astype(F32)
    return tot


def _reduce_adam_big(recv, w, m, v):
    _, R, W = recv.shape

    def body(r_ref, w_ref, m_ref, v_ref, g_ref, d_ref, nm_ref, nv_ref):
        g = _sum_slots(r_ref)
        g_ref[...] = g
        d_ref[...], nm_ref[...], nv_ref[...] = _adam(w_ref[...], g, m_ref[...], v_ref[...])

    row = pl.BlockSpec((ADAM_TILE, W), lambda i: (i, 0))
    return pl.pallas_call(
        body, name="reduce_adam", grid=(R // ADAM_TILE,),
        in_specs=[pl.BlockSpec((N_DEV, ADAM_TILE, W), lambda i: (0, i, 0)), row, row, row], out_specs=[row] * 4,
        out_shape=[jax.ShapeDtypeStruct((R, W), F32)] * 4, compiler_params=_cparams(("parallel",)),
    )(recv, w, m, v)


def _allreduce_adam_small(gs, w, m, v):
    R, W = gs.shape

    def body(g_ref, w_ref, m_ref, v_ref, go_ref, d_ref, nm_ref, nv_ref, land, send_sems, recv_sems):
        x, y, c = _coords()
        my = _slot(x, y, c)
        cps = []
        for k in range(1, N_DEV):
            peer = _peer(k, x, y, c)
            cps.append(pltpu.make_async_remote_copy(
                src_ref=g_ref, dst_ref=land.at[my], send_sem=send_sems.at[k - 1], recv_sem=recv_sems.at[k - 1],
                device_id=peer, device_id_type=MESH))
        for cp in cps:
            cp.start()
        land[my] = g_ref[...]
        for cp in cps:
            cp.wait_recv()
        for cp in cps:
            cp.wait_send()
        g = _sum_slots(land)
        go_ref[...] = g
        d_ref[...], nm_ref[...], nv_ref[...] = _adam(w_ref[...], g, m_ref[...], v_ref[...])

    return pl.pallas_call(
        body, name="allreduce_small", in_specs=[VM] * 4, out_specs=[VM] * 4,
        out_shape=[jax.ShapeDtypeStruct((R, W), F32)] * 4,
        scratch_shapes=[pltpu.VMEM((N_DEV, R, W), F32), pltpu.SemaphoreType.DMA((7,)), pltpu.SemaphoreType.DMA((7,))],
    )(gs, w, m, v)


def _row(v):
    return v.reshape(1, -1).astype(F32)


def _step(x, mem, positions, tgt, W, M, V):
    shard = {n: W[n][0] for n, _, _ in BIG}
    full = _unpack_gathered(_gather_big(_pack_big(shard, BF16)))
    P = {n: _row(W[n]) for n in SMALL}
    for n in ("dt_bias", "a_log", "d_skip"):
        P[n] = _head_lanes(P[n])
    P.update(w_in=_pad_w_in(full["w_in"]), w_q_up=_pad_w_q(full["w_q_up"]), w_kv_up=_pad_w_kv(full["w_kv_up"]),
             conv_w=jnp.pad(full["conv_w"].astype(F32), ((0, 8 - SSD_CONV), (0, 0))))
    for n in ("w_mix_out", "w_mem_q", "w_mem_k", "w_mem_v", "w_mem_o", "w_up", "w_down"):
        P[n] = full[n]

    loss, gx, G = _local_step(x[0], mem[0], positions[0], tgt[0], P)

    G["w_in"], G["w_q_up"], G["w_kv_up"] = _unpad_w_in(G["w_in"]), _unpad_w_q(G["w_q_up"]), _unpad_w_kv(G["w_kv_up"])
    G["conv_w"] = G["conv_w"][:SSD_CONV]
    recv = _scatter_grads(_pack_grads(G))
    big = [_unpack_big(b) for b in _reduce_adam_big(
        recv, _pack_big(shard, F32), _pack_big({n: M[n][0] for n, _, _ in BIG}, F32), _pack_big({n: V[n][0] for n, _, _ in BIG}, F32))]

    def small_rows(D):
        rows = {n: _row(D[n]) for n in SMALL}
        for n in ("dt_bias", "a_log", "d_skip"):
            rows[n] = _head_lanes(rows[n])
        return rows

    loss_row = jnp.broadcast_to(loss[:, :1], (1, PACK_W))
    small_bufs = _allreduce_adam_small(
        _pack_small({n: G[n] for n in SMALL}, loss_row), _pack_small(small_rows(W)), _pack_small(small_rows(M)),
        _pack_small(small_rows(V)))
    loss_tot = small_bufs[0][LOSS_ROW, 0]
    small = [_unpack_small(b) for b in small_bufs]

    outs = []
    for res_b, res_s in zip(big, small):
        for n in ALL_W:
            if n in res_b:
                outs.append(res_b[n].reshape(W[n].shape))
            else:
                outs.append(res_s[n][:, :W[n].size].reshape(W[n].shape))
    return (loss_tot, gx[None], *outs)


def kernel(x, mem, positions, ln_in_g, ln_in_b, w_in, conv_w, conv_b, dt_bias, a_log, d_skip, ssd_norm_g, q_norm_g, w_q_up, kv_norm_g, w_kv_up, w_mix_out, ln1_g, ln1_b, w_mem_q, w_mem_k, w_mem_v, w_mem_o, ln2_g, ln2_b, w_up, w_down, ln3_g, ln3_b, loss_target, m_ln_in_g, m_ln_in_b, m_w_in, m_conv_w, m_conv_b, m_dt_bias, m_a_log, m_d_skip, m_ssd_norm_g, m_q_norm_g, m_w_q_up, m_kv_norm_g, m_w_kv_up, m_w_mix_out, m_ln1_g, m_ln1_b, m_w_mem_q, m_w_mem_k, m_w_mem_v, m_w_mem_o, m_ln2_g, m_ln2_b, m_w_up, m_w_down, m_ln3_g, m_ln3_b, v_ln_in_g, v_ln_in_b, v_w_in, v_conv_w, v_conv_b, v_dt_bias, v_a_log, v_d_skip, v_ssd_norm_g, v_q_norm_g, v_w_q_up, v_kv_norm_g, v_w_kv_up, v_w_mix_out, v_ln1_g, v_ln1_b, v_w_mem_q, v_w_mem_k, v_w_mem_v, v_w_mem_o, v_ln2_g, v_ln2_b, v_w_up, v_w_down, v_ln3_g, v_ln3_b):
    a = dict(locals())
    W = {n: a[n] for n in ALL_W}
    M = {n: a["m_" + n] for n in ALL_W}
    V = {n: a["v_" + n] for n in ALL_W}
    return _step_overlapped(x, mem, positions, loss_target, W, M, V)


HBM = pl.BlockSpec(memory_space=pltpu.HBM)
SEM = pl.BlockSpec(memory_space=pltpu.SEMAPHORE)
EFFECT = pltpu.SideEffectType.DATAFLOW_SIDE_EFFECTING
SHARD_SHAPE = {n: s for n, s, _ in BIG}
SHARD_AXIS = {n: ax for n, _, ax in BIG}
GATHER_NOW = ("w_in",)
GATHER_MID = ("conv_w", "w_q_up", "w_kv_up", "w_mix_out")
GATHER_LATE = ("w_mem_q", "w_mem_k", "w_mem_v", "w_mem_o", "w_up", "w_down")


def _my_slot():
    return _slot(*_coords())


def _group_copies(src_refs, land_refs, send_sems, recv_sems, slotted, landing_of_peer):
    x, y, c = _coords()
    my = _slot(x, y, c)
    cps = []
    for a, (s_ref, l_ref) in enumerate(zip(src_refs, land_refs)):
        for k in range(1, N_DEV):
            peer = _peer(k, x, y, c)
            cps.append(pltpu.make_async_remote_copy(
                src_ref=s_ref.at[_slot(*peer)] if slotted else s_ref,
                dst_ref=l_ref.at[_slot(*peer)] if landing_of_peer else l_ref.at[my],
                send_sem=send_sems.at[7 * a + k - 1], recv_sem=recv_sems.at[7 * a + k - 1],
                device_id=peer, device_id_type=MESH))
    return cps


def _send_start(srcs, lands, slotted, name):
    n = len(srcs)

    def body(*refs):
        for cp in _group_copies(refs[:n], refs[n:2 * n], refs[2 * n], refs[2 * n + 1], slotted, False):
            cp.start()
        refs[-1][...] = jnp.zeros(refs[-1].shape, F32)

    res = pl.pallas_call(
        body, name=name,
        out_shape=(pltpu.SemaphoreType.DMA((7 * n,)), pltpu.SemaphoreType.DMA((7 * n,)),
                   *[pltpu.HBM(a.shape, a.dtype) for a in srcs], *[pltpu.HBM(a.shape, a.dtype) for a in lands],
                   jax.ShapeDtypeStruct((8, LANES), F32)),
        in_specs=[HBM] * (2 * n), out_specs=(SEM, SEM, *[HBM] * (2 * n), VM),
        input_output_aliases={i: 2 + i for i in range(2 * n)},
        compiler_params=pltpu.CompilerParams(has_side_effects=EFFECT),
    )(*[pltpu.with_memory_space_constraint(a, pltpu.HBM) for a in list(srcs) + list(lands)])
    return (res[0], res[1], res[2:2 + n], res[2 + n:2 + 2 * n]), res[-1][:1, :1]


def _send_wait(started, after, slotted, name):
    send_sems, recv_sems, srcs, lands = started
    n = len(srcs)

    def body(*refs):
        for cp in _group_copies(refs[:n], refs[n:2 * n], refs[2 * n], refs[2 * n + 1], slotted, True):
            cp.wait_send()
            cp.wait_recv()

    res = pl.pallas_call(
        body, name=name, out_shape=tuple(pltpu.HBM(a.shape, a.dtype) for a in list(srcs) + list(lands)),
        in_specs=[HBM] * (2 * n) + [SEM, SEM, ANY], out_specs=tuple([HBM] * (2 * n)),
        input_output_aliases={i: i for i in range(2 * n)},
        compiler_params=pltpu.CompilerParams(has_side_effects=EFFECT),
    )(*srcs, *lands, send_sems, recv_sems, after)
    return res[n:]


def _landing(own, my):
    return lax.dynamic_update_slice(lax.empty((N_DEV,) + own.shape, own.dtype), own[None], (my,) + (0,) * own.ndim)


def _gather_now(shards):
    n = len(shards)

    def body(*refs):
        x_refs, out_refs = refs[:n], refs[n:2 * n]
        send_sems, recv_sems, local_sems = refs[2 * n:]
        x, y, c = _coords()
        me, sibling = (x, y, c), (x, y, 1 - c)
        chips = [(1 - x, y), (x, 1 - y), (1 - x, 1 - y)]

        def copy(a, k, block, to, src=None):
            rows = out_refs[a].at[_slot(*block)]
            return pltpu.make_async_remote_copy(
                src_ref=rows if src is None else src, dst_ref=rows,
                send_sem=send_sems.at[7 * a + k], recv_sem=recv_sems.at[7 * a + k], device_id=to, device_id_type=MESH)

        mine = [pltpu.make_async_copy(x_refs[a], out_refs[a].at[_slot(*me)], local_sems.at[a]) for a in range(n)]
        for cp in mine:
            cp.start()
        first = []
        for a in range(n):
            first.append(copy(a, 0, me, sibling, src=x_refs[a]))
            first += [copy(a, 1 + j, me, (*chip, c), src=x_refs[a]) for j, chip in enumerate(chips)]
        for cp in first:
            cp.start()
        passed = []
        for j, chip in enumerate(chips):
            for a in range(n):
                copy(a, 1 + j, (*chip, c), me).wait_recv()
                fwd = copy(a, 4 + j, (*chip, c), sibling)
                fwd.start()
                passed.append(fwd)
        for a in range(n):
            copy(a, 0, sibling, me).wait_recv()
            for j, chip in enumerate(chips):
                copy(a, 4 + j, (*chip, 1 - c), me).wait_recv()
        for cp in first + passed:
            cp.wait_send()
        for cp in mine:
            cp.wait()

    return pl.pallas_call(
        body, name="gather_now", out_shape=[jax.ShapeDtypeStruct((N_DEV,) + s.shape, s.dtype) for s in shards],
        in_specs=[ANY] * n, out_specs=[ANY] * n,
        scratch_shapes=[pltpu.SemaphoreType.DMA((7 * n,)), pltpu.SemaphoreType.DMA((7 * n,)), pltpu.SemaphoreType.DMA((n,))],
    )(*shards)


def _full_from_slots(name, slots):
    a, b = SHARD_SHAPE[name]
    return slots.reshape(N_DEV * a, b) if SHARD_AXIS[name] == 0 else slots.transpose(1, 0, 2).reshape(a, N_DEV * b)


def _slots_from_full(name, g):
    a, b = SHARD_SHAPE[name]
    return g.reshape(N_DEV, a, b) if SHARD_AXIS[name] == 0 else g.reshape(a, N_DEV, b).transpose(1, 0, 2)


def _reduce_adam(recv, w, m, v, name):
    _, a, b = recv.shape
    ta = a
    while ta * b * 4 * N_DEV > 4 * 1024 * 1024 and ta % 16 == 0:
        ta //= 2

    def body(r_ref, w_ref, m_ref, v_ref, g_ref, d_ref, nm_ref, nv_ref):
        g = _sum_slots(r_ref)
        g_ref[...] = g
        d_ref[...], nm_ref[...], nv_ref[...] = _adam(w_ref[...], g, m_ref[...], v_ref[...])

    row = pl.BlockSpec((ta, b), lambda i: (i, 0))
    return pl.pallas_call(
        body, name=name, grid=(a // ta,),
        in_specs=[pl.BlockSpec((N_DEV, ta, b), lambda i: (0, i, 0)), row, row, row], out_specs=[row] * 4,
        out_shape=[jax.ShapeDtypeStruct((a, b), F32)] * 4, compiler_params=_cparams(("parallel",)),
    )(recv, w, m, v)


def _step_overlapped(x, mem, positions, tgt, W, M, V):
    my = _my_slot()
    shard = {n: W[n][0] for n, _, _ in BIG}
    send = {n: (shard[n] if n == "conv_w" else shard[n].astype(BF16)) for n in shard}

    first_src, mid_src, late_src = ([send[n] for n in grp] for grp in (GATHER_NOW, GATHER_MID, GATHER_LATE))
    first, tie = _send_start(first_src, [_landing(s, my) for s in first_src], False, "gather_first_start")
    my_then = my + tie[0, 0].astype(jnp.int32)
    mid, tie = _send_start(mid_src, [_landing(s, my_then) for s in mid_src], False, "gather_mid_start")
    my_then = my + tie[0, 0].astype(jnp.int32)
    late, tie = _send_start(late_src, [_landing(s, my_then) for s in late_src], False, "gather_late_start")

    P = {n: _row(W[n]) for n in SMALL}
    for n in ("dt_bias", "a_log", "d_skip"):
        P[n] = _head_lanes(P[n])
    P["ln_in_g"] = P["ln_in_g"] + tie

    def weights_at(stage, after):
        if stage == "first":
            lands = _send_wait(first, after, False, "gather_first_wait")
            return dict(w_in=_pad_w_in(_full_from_slots("w_in", lands[0])))
        if stage == "mid":
            lands = _send_wait(mid, after, False, "gather_mid_wait")
            full = {n: _full_from_slots(n, l) for n, l in zip(GATHER_MID, lands)}
            return dict(w_q_up=_pad_w_q(full["w_q_up"]), w_kv_up=_pad_w_kv(full["w_kv_up"]), w_mix_out=full["w_mix_out"],
                        conv_w=jnp.pad(full["conv_w"], ((0, 8 - SSD_CONV), (0, 0))))
        lands = _send_wait(late, after, False, "gather_late_wait")
        return {n: _full_from_slots(n, l) for n, l in zip(GATHER_LATE, lands)}

    started, res = [], {}

    def finish(i, after):
        names, st = started[i]
        lands = _send_wait(st, after, True, "scatter_wait_%d" % i)
        for n, recv in zip(names, lands):
            res[n] = _reduce_adam(recv, shard[n], M[n][0], V[n][0], "reduce_adam_" + n)
        return sum(res[n][0][:1, :1] for n in names) * 0.0

    def emit(names, G):
        srcs = []
        for n in names:
            g = G[n]
            if n == "w_in":
                g = _unpad_w_in(g)
            elif n == "w_q_up":
                g = _unpad_w_q(g)
            elif n == "w_kv_up":
                g = _unpad_w_kv(g)
            elif n == "conv_w":
                g = g[:SSD_CONV]
            srcs.append(g if g.ndim == 3 else _slots_from_full(n, g))
        lands = [_landing(lax.dynamic_index_in_dim(s, my, 0, keepdims=False), my) for s in srcs]
        st, tie = _send_start(srcs, lands, True, "scatter_start_%d" % len(started))
        started.append((names, st))
        if len(started) == 3:
            tie = tie + finish(0, srcs[0]) + finish(1, srcs[0])
        return tie

    loss, gx, G = _local_step(x[0], mem[0], positions[0], tgt[0], P, weights_at, emit)
    finish(2, gx)
    finish(3, gx)

    small, loss_tot = _allreduce_adam_vectors(
        {n: G[n] for n in SMALL}, loss, {n: _row(W[n]) for n in SMALL}, {n: _row(M[n]) for n in SMALL}, {n: _row(V[n]) for n in SMALL})

    outs = []
    for j in range(4):
        for n in ALL_W:
            outs.append((res[n][j] if n in res else small[j][n]).reshape(W[n].shape))
    return (loss_tot[0, 0], gx[None], *outs)


def _vector_places():
    places = {}
    for r, names in enumerate(_SMALL_ROWS):
        c = 0
        for n in names:
            w = _SMALL_W.get(n, PACK_W)
            places[n] = (r, c, w, SSD_HEADS if n in ("dt_bias", "a_log", "d_skip") else w)
            c += w
    return places


def _allreduce_adam_vectors(grads, loss, Ws, Ms, Vs):
    places = _vector_places()
    ns = len(SMALL)

    def body(*refs):
        g_in, loss_in = refs[:ns], refs[ns]
        w_in, m_in, v_in = refs[ns + 1:2 * ns + 1], refs[2 * ns + 1:3 * ns + 1], refs[3 * ns + 1:4 * ns + 1]
        o = 4 * ns + 1
        outs = [refs[o + j * ns:o + (j + 1) * ns] for j in range(4)]
        loss_out, stage, land, send_sems, recv_sems = refs[o + 4 * ns:]
        stage[...] = jnp.zeros(stage.shape, F32)
        for i, n in enumerate(SMALL):
            r, c, w, _ = places[n]
            stage[r:r + 1, c:c + w] = g_in[i][...]
        stage[LOSS_ROW:LOSS_ROW + 1, 0:LANES] = loss_in[...]
        x, y, c_ = _coords()
        my = _slot(x, y, c_)
        cps = []
        for k in range(1, N_DEV):
            peer = _peer(k, x, y, c_)
            cps.append(pltpu.make_async_remote_copy(
                src_ref=stage, dst_ref=land.at[my], send_sem=send_sems.at[k - 1], recv_sem=recv_sems.at[k - 1],
                device_id=peer, device_id_type=MESH))
        for cp in cps:
            cp.start()
        land[my] = stage[...]
        for cp in cps:
            cp.wait_recv()
        for cp in cps:
            cp.wait_send()
        tot = _sum_slots(land)
        loss_out[...] = tot[LOSS_ROW:LOSS_ROW + 1, 0:LANES]
        for i, n in enumerate(SMALL):
            r, c, _, wt = places[n]
            g = tot[r:r + 1, c:c + wt]
            outs[0][i][...] = g
            outs[1][i][...], outs[2][i][...], outs[3][i][...] = _adam(w_in[i][...], g, m_in[i][...], v_in[i][...])

    shapes = [jax.ShapeDtypeStruct((1, places[n][3]), F32) for n in SMALL]
    res = pl.pallas_call(
        body, name="allreduce_vectors", in_specs=[VM] * (4 * ns + 1), out_specs=[VM] * (4 * ns + 1),
        out_shape=shapes * 4 + [jax.ShapeDtypeStruct((1, LANES), F32)],
        scratch_shapes=[pltpu.VMEM((SMALL_R, PACK_W), F32), pltpu.VMEM((N_DEV, SMALL_R, PACK_W), F32),
                        pltpu.SemaphoreType.DMA((7,)), pltpu.SemaphoreType.DMA((7,))],
    )(*[grads[n] for n in SMALL], loss, *[Ws[n] for n in SMALL], *[Ms[n] for n in SMALL], *[Vs[n] for n in SMALL])
    return [dict(zip(SMALL, res[j * ns:(j + 1) * ns])) for j in range(4)], res[4 * ns]
```

```python
import math

import jax
import jax.numpy as jnp
from jax import lax
from jax.experimental import pallas as pl
from jax.experimental.pallas import tpu as pltpu

F32, BF16 = jnp.float32, jnp.bfloat16

N_DEV = 8
D_MODEL = 1024
SSD_HEADS, SSD_HEAD_DIM, SSD_INNER, SSD_STATE, SSD_CONV, SSD_CHUNK = 8, 64, 512, 128, 4, 128
SSD_XBC = 1024
MLA_HEADS, MLA_NOPE, MLA_ROPE, MLA_QK, MLA_V = 8, 64, 32, 96, 64
MLA_Q_RANK, MLA_KV_RANK = 384, 256
ROPE_THETA = 10000.0
MEM_HEADS, MEM_HEAD_DIM = 4, 256
D_FF = 4096
IN_WIDTH = 2216
LN_EPS, RMS_EPS = 1e-5, 1e-6
ALPHA = 2.0 ** 0.25
ADAM_LR, ADAM_B1, ADAM_B2, ADAM_EPS, ADAM_WD, ADAM_STEP = 0.001, 0.9, 0.999, 1e-08, 0.01, 10

LANES = 128
NEG = -1e30
VMEM_LIMIT = 56 * 1024 * 1024
TOK_K = 4096

PQ, PZ, PX, PKV, PDT, PKR, PW = 0, 512, 1024, 2048, 2304, 2432, 2560
KR_LANE = 64


def _cparams(sem):
    return pltpu.CompilerParams(dimension_semantics=sem, vmem_limit_bytes=VMEM_LIMIT)


def _sigmoid(x):
    return 1.0 / (1.0 + jnp.exp(-x))


def _mm(a, b, mode, name, *, tm=512, tn=None, tk=None, out_dtypes=(F32,), epi=None, extras=(), col_slots=False):
    if mode == "nn":
        (M, K), (K2, N) = a.shape, b.shape
    elif mode == "nt":
        (M, K), (N, K2) = a.shape, b.shape
    else:
        (K, M), (K2, N) = a.shape, b.shape
    assert K == K2, (name, a.shape, b.shape)
    tm, tn, tk = min(tm, M), min(tn or N, N), min(tk or K, K)
    assert M % tm == 0 and N % tn == 0 and K % tk == 0, (name, M, N, K, tm, tn, tk)
    gk = K // tk
    a_spec = pl.BlockSpec((tk, tm), lambda i, j, k: (k, i)) if mode == "tn" else pl.BlockSpec((tm, tk), lambda i, j, k: (i, k))
    b_spec = pl.BlockSpec((tn, tk), lambda i, j, k: (j, k)) if mode == "nt" else pl.BlockSpec((tk, tn), lambda i, j, k: (k, j))
    dims = {"nn": ((1,), (0,)), "nt": ((1,), (1,)), "tn": ((0,), (0,))}[mode]
    ex_specs = []
    for arr, kind in extras:
        if kind == "mn":
            ex_specs.append(pl.BlockSpec((tm, tn), lambda i, j, k: (i, j)))
        elif kind == "n":
            ex_specs.append(pl.BlockSpec((1, tn), lambda i, j, k: (0, j)))
        else:
            ex_specs.append(pl.BlockSpec((tm, arr.shape[1]), lambda i, j, k: (i, 0)))
    ne, no = len(extras), len(out_dtypes)

    def body(*refs):
        a_ref, b_ref = refs[0], refs[1]
        ex, outs = refs[2:2 + ne], refs[2 + ne:2 + ne + no]
        part = lax.dot_general(a_ref[...].astype(BF16), b_ref[...].astype(BF16), (dims, ((), ())),
                               preferred_element_type=F32)

        def finish(acc):
            res = epi(acc, *[e[...] for e in ex]) if epi is not None else (acc,)
            for o, r in zip(outs, res):
                o[...] = r.astype(o.dtype)

        if gk == 1:
            finish(part)
        else:
            acc_ref = refs[-1]
            k = pl.program_id(2)

            @pl.when(k == 0)
            def _():
                acc_ref[...] = part

            @pl.when(k > 0)
            def _():
                acc_ref[...] += part

            @pl.when(k == gk - 1)
            def _():
                finish(acc_ref[...])

    res = pl.pallas_call(
        body, name=name, grid=(M // tm, N // tn, gk),
        in_specs=[a_spec, b_spec] + ex_specs,
        out_specs=[pl.BlockSpec((None, tm, tn), lambda i, j, k: (j, i, 0)) if col_slots else pl.BlockSpec((tm, tn), lambda i, j, k: (i, j))
                   for _ in out_dtypes],
        out_shape=[jax.ShapeDtypeStruct((N // tn, M, tn) if col_slots else (M, N), dt) for dt in out_dtypes],
        scratch_shapes=[pltpu.VMEM((tm, tn), F32)] if gk > 1 else [],
        compiler_params=_cparams(("parallel", "parallel", "arbitrary")),
    )(a, b, *[e[0] for e in extras])
    return res[0] if no == 1 else res


def _ln_stats(r):
    mu = jnp.mean(r, axis=-1, keepdims=True)
    xc = r - mu
    var = jnp.mean(xc * xc, axis=-1, keepdims=True)
    rstd = lax.rsqrt(var + LN_EPS)
    return xc * rstd, rstd


def _ln_fwd(terms, g, b, name, tm=512):
    S, D = terms[0][0].shape
    coefs = [c for _, c in terms]
    nt = len(terms)

    def body(*refs):
        r = sum(c * t[...] for t, c in zip(refs[:nt], coefs))
        xh, _ = _ln_stats(r)
        h = xh * refs[nt][...] + refs[nt + 1][...]
        refs[nt + 2][...] = h
        refs[nt + 3][...] = h.astype(BF16)

    row = pl.BlockSpec((tm, D), lambda i: (i, 0))
    vec = pl.BlockSpec((1, D), lambda i: (0, 0))
    return pl.pallas_call(
        body, name=name, grid=(S // tm,), in_specs=[row] * nt + [vec, vec], out_specs=[row, row],
        out_shape=[jax.ShapeDtypeStruct((S, D), F32), jax.ShapeDtypeStruct((S, D), BF16)], compiler_params=_cparams(("parallel",)),
    )(*[t for t, _ in terms], g, b)


def _ln_bwd(terms, dterms, g, name, tm=512):
    S, D = terms[0][0].shape
    coefs, dcoefs = [c for _, c in terms], [c for _, c in dterms]
    nt, nd = len(terms), len(dterms)

    def body(*refs):
        i = pl.program_id(0)
        r = sum(c * t[...] for t, c in zip(refs[:nt], coefs))
        dh = sum(c * t[...].astype(F32) for t, c in zip(refs[nt:nt + nd], dcoefs))
        g_ref = refs[nt + nd]
        dr_ref, drb_ref, dg_ref, db_ref = refs[nt + nd + 1:]
        xh, rstd = _ln_stats(r)
        dxh = dh * g_ref[...]
        m1 = jnp.mean(dxh, axis=-1, keepdims=True)
        m2 = jnp.mean(dxh * xh, axis=-1, keepdims=True)
        dr = rstd * (dxh - m1 - xh * m2)
        dr_ref[...] = dr
        drb_ref[...] = dr.astype(BF16)
        pg = jnp.sum(dh * xh, axis=0, keepdims=True)
        pb = jnp.sum(dh, axis=0, keepdims=True)

        @pl.when(i == 0)
        def _():
            dg_ref[...] = pg
            db_ref[...] = pb

        @pl.when(i > 0)
        def _():
            dg_ref[...] += pg
            db_ref[...] += pb

    row = pl.BlockSpec((tm, D), lambda i: (i, 0))
    vec = pl.BlockSpec((1, D), lambda i: (0, 0))
    return pl.pallas_call(
        body, name=name, grid=(S // tm,), in_specs=[row] * (nt + nd) + [vec], out_specs=[row, row, vec, vec],
        out_shape=[jax.ShapeDtypeStruct((S, D), F32), jax.ShapeDtypeStruct((S, D), BF16), jax.ShapeDtypeStruct((1, D), F32),
                   jax.ShapeDtypeStruct((1, D), F32)],
        compiler_params=_cparams(("arbitrary",)),
    )(*[t for t, _ in terms], *[t for t, _ in dterms], g)


def _ln_loss_bwd(terms, g, b, tgt, name, tm=512):
    S, D = terms[0][0].shape
    coefs = [c for _, c in terms]
    nt = len(terms)

    def body(*refs):
        i = pl.program_id(0)
        r = sum(c * t[...] for t, c in zip(refs[:nt], coefs))
        g_ref, b_ref, t_ref = refs[nt:nt + 3]
        loss_ref, dr_ref, drb_ref, dg_ref, db_ref = refs[nt + 3:]
        xh, rstd = _ln_stats(r)
        h = xh * g_ref[...] + b_ref[...]
        diff = h - t_ref[...]
        pl_ = 0.5 * jnp.sum(jnp.mean(diff * diff, axis=-1, keepdims=True), axis=0, keepdims=True)
        dh = diff * (1.0 / D)
        dxh = dh * g_ref[...]
        m1 = jnp.mean(dxh, axis=-1, keepdims=True)
        m2 = jnp.mean(dxh * xh, axis=-1, keepdims=True)
        dr = rstd * (dxh - m1 - xh * m2)
        dr_ref[...] = dr
        drb_ref[...] = dr.astype(BF16)
        pg = jnp.sum(dh * xh, axis=0, keepdims=True)
        pb = jnp.sum(dh, axis=0, keepdims=True)
        plb = jnp.broadcast_to(pl_, (1, LANES))

        @pl.when(i == 0)
        def _():
            dg_ref[...] = pg
            db_ref[...] = pb
            loss_ref[...] = plb

        @pl.when(i > 0)
        def _():
            dg_ref[...] += pg
            db_ref[...] += pb
            loss_ref[...] += plb

    row = pl.BlockSpec((tm, D), lambda i: (i, 0))
    vec = pl.BlockSpec((1, D), lambda i: (0, 0))
    lvec = pl.BlockSpec((1, LANES), lambda i: (0, 0))
    return pl.pallas_call(
        body, name=name, grid=(S // tm,), in_specs=[row] * nt + [vec, vec, row], out_specs=[lvec, row, row, vec, vec],
        out_shape=[jax.ShapeDtypeStruct((1, LANES), F32), jax.ShapeDtypeStruct((S, D), F32), jax.ShapeDtypeStruct((S, D), BF16),
                   jax.ShapeDtypeStruct((1, D), F32), jax.ShapeDtypeStruct((1, D), F32)],
        compiler_params=_cparams(("arbitrary",)),
    )(*[t for t, _ in terms], g, b, tgt)


def _rope_tables(positions):
    half = MLA_ROPE // 2
    inv_freq = jnp.power(ROPE_THETA, -jnp.arange(half, dtype=F32) / half)
    ang = positions.astype(F32)[:, None] * inv_freq
    cos, sin = jnp.cos(ang), jnp.sin(ang)
    S = positions.shape[0]
    one, zero = jnp.ones((S, MLA_NOPE), F32), jnp.zeros((S, half), F32)
    pad = jnp.zeros((S, LANES - MLA_QK), F32)
    c = jnp.concatenate([one, cos, cos, pad], axis=1)
    s1 = jnp.concatenate([0 * one, -sin, zero, pad], axis=1)
    s2 = jnp.concatenate([0 * one, zero, sin, pad], axis=1)
    return c, s1, s2


def _rope_block(x, c, s1, s2):
    half = MLA_ROPE // 2
    return x * c + pltpu.roll(x, LANES - half, axis=1) * s1 + pltpu.roll(x, half, axis=1) * s2


def _rms_fwd(x, g):
    r = lax.rsqrt(jnp.mean(x * x, axis=-1, keepdims=True) + RMS_EPS)
    return x * r * g


def _rms_bwd(x, g, dy):
    r = lax.rsqrt(jnp.mean(x * x, axis=-1, keepdims=True) + RMS_EPS)
    xh = x * r
    dyh = dy * g
    dx = r * (dyh - xh * jnp.mean(dyh * xh, axis=-1, keepdims=True))
    return dx, jnp.sum(dy * xh, axis=0, keepdims=True)


def _mla_prep(proj, qg, kvg, tabs, tm=512):
    S = proj.shape[0]

    def body(ql_ref, kvl_ref, kr_ref, qg_ref, kvg_ref, c_ref, s1_ref, s2_ref, qn_ref, kvn_ref, kpe_ref):
        qn_ref[...] = _rms_fwd(ql_ref[...], qg_ref[...]).astype(BF16)
        kvn_ref[...] = _rms_fwd(kvl_ref[...], kvg_ref[...]).astype(BF16)
        kpe_ref[...] = _rope_block(kr_ref[...], c_ref[...], s1_ref[...], s2_ref[...])

    tab = pl.BlockSpec((tm, LANES), lambda i: (i, 0))
    return pl.pallas_call(
        body, name="mla_prep", grid=(S // tm,),
        in_specs=[pl.BlockSpec((tm, MLA_Q_RANK), lambda i: (i, PQ // MLA_Q_RANK)),
                  pl.BlockSpec((tm, MLA_KV_RANK), lambda i: (i, PKV // MLA_KV_RANK)),
                  pl.BlockSpec((tm, LANES), lambda i: (i, PKR // LANES)),
                  pl.BlockSpec((1, MLA_Q_RANK), lambda i: (0, 0)), pl.BlockSpec((1, MLA_KV_RANK), lambda i: (0, 0)),
                  tab, tab, tab],
        out_specs=[pl.BlockSpec((tm, MLA_Q_RANK), lambda i: (i, 0)), pl.BlockSpec((tm, MLA_KV_RANK), lambda i: (i, 0)), tab],
        out_shape=[jax.ShapeDtypeStruct((S, MLA_Q_RANK), BF16), jax.ShapeDtypeStruct((S, MLA_KV_RANK), BF16),
                   jax.ShapeDtypeStruct((S, LANES), F32)],
        compiler_params=_cparams(("parallel",)),
    )(proj, proj, proj, qg, kvg, *tabs)


def _mla_prep_bwd(proj, qg, kvg, tabs, dqn, dkvn, dk_all, tm=512):
    S = proj.shape[0]

    def body(ql_ref, kvl_ref, qg_ref, kvg_ref, c_ref, s1_ref, s2_ref, dqn_ref, dkvn_ref, dk_ref,
             dql_ref, dkvl_ref, dkr_ref, dqg_ref, dkvg_ref):
        i = pl.program_id(0)
        dql, pq = _rms_bwd(ql_ref[...], qg_ref[...], dqn_ref[...])
        dkvl, pkv = _rms_bwd(kvl_ref[...], kvg_ref[...], dkvn_ref[...])
        dql_ref[...] = dql
        dkvl_ref[...] = dkvl
        dk = dk_ref[...]
        dkpe = dk[:, 0:LANES]
        for h in range(1, MLA_HEADS):
            dkpe = dkpe + dk[:, h * LANES:(h + 1) * LANES]
        lane = lax.broadcasted_iota(jnp.int32, dkpe.shape, 1)
        dkpe = jnp.where((lane >= KR_LANE) & (lane < KR_LANE + MLA_ROPE), dkpe, 0.0)
        dkr_ref[...] = _rope_block(dkpe, c_ref[...], -s1_ref[...], -s2_ref[...])

        @pl.when(i == 0)
        def _():
            dqg_ref[...] = pq
            dkvg_ref[...] = pkv

        @pl.when(i > 0)
        def _():
            dqg_ref[...] += pq
            dkvg_ref[...] += pkv

    tab = pl.BlockSpec((tm, LANES), lambda i: (i, 0))
    qspec = pl.BlockSpec((tm, MLA_Q_RANK), lambda i: (i, 0))
    kvspec = pl.BlockSpec((tm, MLA_KV_RANK), lambda i: (i, 0))
    qv, kvv = pl.BlockSpec((1, MLA_Q_RANK), lambda i: (0, 0)), pl.BlockSpec((1, MLA_KV_RANK), lambda i: (0, 0))
    return pl.pallas_call(
        body, name="mla_prep_bwd", grid=(S // tm,),
        in_specs=[pl.BlockSpec((tm, MLA_Q_RANK), lambda i: (i, PQ // MLA_Q_RANK)),
                  pl.BlockSpec((tm, MLA_KV_RANK), lambda i: (i, PKV // MLA_KV_RANK)),
                  qv, kvv, tab, tab, tab, qspec, kvspec, pl.BlockSpec((tm, MLA_HEADS * LANES), lambda i: (i, 0))],
        out_specs=[qspec, kvspec, tab, qv, kvv],
        out_shape=[jax.ShapeDtypeStruct((S, MLA_Q_RANK), F32), jax.ShapeDtypeStruct((S, MLA_KV_RANK), F32),
                   jax.ShapeDtypeStruct((S, LANES), F32), jax.ShapeDtypeStruct((1, MLA_Q_RANK), F32),
                   jax.ShapeDtypeStruct((1, MLA_KV_RANK), F32)],
        compiler_params=_cparams(("arbitrary",)),
    )(proj, proj, qg, kvg, *tabs, dqn, dkvn, dk_all)


def _rope_bwd_all(dq_all, tabs, scale, tm=512):
    S, W = dq_all.shape

    def body(dq_ref, c_ref, s1_ref, s2_ref, o_ref):
        c, s1, s2 = scale * c_ref[...], -scale * s1_ref[...], -scale * s2_ref[...]
        for h in range(W // LANES):
            o_ref[:, h * LANES:(h + 1) * LANES] = _rope_block(dq_ref[:, h * LANES:(h + 1) * LANES], c, s1, s2).astype(BF16)

    tab = pl.BlockSpec((tm, LANES), lambda i: (i, 0))
    row = pl.BlockSpec((tm, W), lambda i: (i, 0))
    return pl.pallas_call(body, name="rope_bwd", grid=(S // tm,), in_specs=[row, tab, tab, tab], out_specs=row,
                          out_shape=jax.ShapeDtypeStruct((S, W), BF16), compiler_params=_cparams(("parallel",)))(dq_all, *tabs)


ATT_SCALE = MLA_QK ** -0.5
LN2 = math.log(2.0)
Q_PRESCALE = ATT_SCALE / LN2
DQ_POSTSCALE = ATT_SCALE / LN2
N_PAIR = MLA_HEADS // 2


def _causal_mask(qi, ki, tq, tk):
    row = qi * tq + lax.broadcasted_iota(jnp.int32, (tq, tk), 0)
    col = ki * tk + lax.broadcasted_iota(jnp.int32, (tq, tk), 1)
    return col <= row


def _lane_tile(x, n):
    return jnp.concatenate([x] * n, axis=1) if n > 1 else x


def _attn_fwd(q_all, kv_all, tq=512, tk=1024):
    S = q_all.shape[0]
    tq, tk = min(tq, S), min(tk, S)
    nq, nk, nb, r = S // tq, S // tk, tk // LANES, tk // tq

    def body(q_ref, k_ref, v_ref, o_ref, lse_ref, m_s, l_s, acc_s):
        qi, ki = pl.program_id(1), pl.program_id(2)
        last = lax.div(qi, r)

        @pl.when(ki == 0)
        def _():
            m_s[...] = jnp.full(m_s.shape, NEG, F32)
            l_s[...] = jnp.zeros(l_s.shape, F32)
            acc_s[...] = jnp.zeros(acc_s.shape, F32)

        def block(kc, mask):
            v = v_ref[0:kc, :]
            for hh in range(2):
                q = q_ref[:, hh * LANES:(hh + 1) * LANES]
                k = k_ref[0:kc, hh * LANES:(hh + 1) * LANES]
                s = lax.dot_general(q, k, (((1,), (1,)), ((), ())), preferred_element_type=F32)
                if mask is not None:
                    s = jnp.where(mask, s, NEG)
                m_prev = m_s[hh]
                m_new = jnp.maximum(m_prev, jnp.max(s, axis=-1, keepdims=True))
                p = jnp.exp2(s - _lane_tile(m_new, kc // LANES))
                alpha = jnp.exp2(m_prev - m_new)
                ps = p[:, :LANES]
                for j in range(1, kc // LANES):
                    ps = ps + p[:, j * LANES:(j + 1) * LANES]
                l_s[hh] = alpha * l_s[hh] + ps
                acc_s[hh] = alpha * acc_s[hh] + jnp.dot(p.astype(BF16), v, preferred_element_type=F32)
                m_s[hh] = m_new

        if r == 2:
            @pl.when(ki < last)
            def _():
                block(tk, None)

            @pl.when((ki == last) & (lax.rem(qi, 2) == 0))
            def _():
                block(tq, _causal_mask(0, 0, tq, tq))

            @pl.when((ki == last) & (lax.rem(qi, 2) == 1))
            def _():
                block(tk, _causal_mask(qi, ki, tq, tk))
        else:
            @pl.when(ki <= last)
            def _():
                block(tk, _causal_mask(qi, ki, tq, tk))

        @pl.when(ki == last)
        def _():
            first = lax.broadcasted_iota(jnp.int32, (tq, LANES), 1) < MLA_V
            l0 = jnp.sum(l_s[0], axis=-1, keepdims=True)
            l1 = jnp.sum(l_s[1], axis=-1, keepdims=True)
            o_ref[...] = jnp.where(first, acc_s[0] / l0, acc_s[1] / l1)
            lse_ref[:, :LANES] = m_s[0] + jnp.log2(l0)
            lse_ref[:, LANES:] = m_s[1] + jnp.log2(l1)

    return pl.pallas_call(
        body, name="mla_attn_fwd", grid=(N_PAIR, nq, nk),
        in_specs=[pl.BlockSpec((tq, 2 * LANES), lambda p, qi, ki: (qi, p)),
                  pl.BlockSpec((tk, 2 * LANES), lambda p, qi, ki: (jnp.minimum(ki, lax.div(qi, r)), p)),
                  pl.BlockSpec((tk, LANES), lambda p, qi, ki: (jnp.minimum(ki, lax.div(qi, r)), MLA_HEADS + p))],
        out_specs=[pl.BlockSpec((tq, LANES), lambda p, qi, ki: (qi, p)), pl.BlockSpec((tq, 2 * LANES), lambda p, qi, ki: (qi, p))],
        out_shape=[jax.ShapeDtypeStruct((S, MLA_HEADS * MLA_V), F32), jax.ShapeDtypeStruct((S, MLA_HEADS * LANES), F32)],
        scratch_shapes=[pltpu.VMEM((2, tq, LANES), F32), pltpu.VMEM((2, tq, LANES), F32), pltpu.VMEM((2, tq, LANES), F32)],
        compiler_params=_cparams(("parallel", "parallel", "arbitrary")),
    )(q_all, kv_all, kv_all)


def _attn_delta(do, o):
    prod = do * o
    tm = prod.shape[0]
    first = lax.broadcasted_iota(jnp.int32, (tm, LANES), 1) < MLA_V
    blocks = []
    for p in range(N_PAIR):
        pp = prod[:, p * LANES:(p + 1) * LANES]
        blocks.append(jnp.broadcast_to(jnp.sum(jnp.where(first, pp, 0.0), axis=-1, keepdims=True), (tm, LANES)))
        blocks.append(jnp.broadcast_to(jnp.sum(jnp.where(first, 0.0, pp), axis=-1, keepdims=True), (tm, LANES)))
    return jnp.concatenate(blocks, axis=1)


def _attn_bwd(q_all, kv_all, dcat, lse, delta, tq=512, tk=512):
    S = q_all.shape[0]
    tq, tk = min(tq, S), min(tk, S)
    nq, nk, nb = S // tq, S // tk, tk // LANES
    assert tq == tk

    def body(q_ref, k_ref, v_ref, do_ref, lse_ref, dl_ref, dq_ref, dk_ref, dv_ref, dk_s, dv_s):
        ki, qi = pl.program_id(1), pl.program_id(2)

        @pl.when((ki == 0) & (qi == 0))
        def _():
            dq_ref[...] = jnp.zeros(dq_ref.shape, F32)

        @pl.when(qi == 0)
        def _():
            dk_s[...] = jnp.zeros(dk_s.shape, F32)
            dv_s[...] = jnp.zeros(dv_s.shape, F32)

        def block(r0, nr, kc, mask):
            v, do = v_ref[0:kc, :], do_ref[r0:r0 + nr, :]
            first = lax.broadcasted_iota(jnp.int32, (nr, LANES), 1) < MLA_V
            firstk = lax.broadcasted_iota(jnp.int32, (kc, LANES), 1) < MLA_V
            do_b = do.astype(BF16)
            rows = pl.ds(pl.multiple_of(qi * tq + r0, LANES), nr)
            for hh in range(2):
                sl = slice(hh * LANES, (hh + 1) * LANES)
                q, k = q_ref[r0:r0 + nr, sl], k_ref[0:kc, sl]
                s = lax.dot_general(q, k, (((1,), (1,)), ((), ())), preferred_element_type=F32)
                if mask is not None:
                    s = jnp.where(mask, s, NEG)
                p = jnp.exp2(s - _lane_tile(lse_ref[r0:r0 + nr, sl], kc // LANES))
                do_h = jnp.where(first if hh == 0 else ~first, do, 0.0).astype(BF16)
                dp = lax.dot_general(do_h, v, (((1,), (1,)), ((), ())), preferred_element_type=F32)
                ds_b = (p * (dp - _lane_tile(dl_ref[r0:r0 + nr, sl], kc // LANES)) * LN2).astype(BF16)
                pv = lax.dot_general(p.astype(BF16), do_b, (((0,), (0,)), ((), ())), preferred_element_type=F32)
                dv_s[0:kc, :] += jnp.where(firstk if hh == 0 else ~firstk, pv, 0.0)
                dk_s[0:kc, sl] += lax.dot_general(ds_b, q, (((0,), (0,)), ((), ())), preferred_element_type=F32)
                dq_ref[rows, sl] += jnp.dot(ds_b, k, preferred_element_type=F32)

        @pl.when(qi > ki)
        def _():
            block(0, tq, tk, None)

        @pl.when(qi == ki)
        def _():
            h = tq // 2
            block(0, h, h, _causal_mask(0, 0, h, h))
            block(h, h, tk, lax.broadcasted_iota(jnp.int32, (h, tk), 1) <= h + lax.broadcasted_iota(jnp.int32, (h, tk), 0))

        @pl.when(qi == nq - 1)
        def _():
            dk_ref[...] = dk_s[...]
            dv_ref[...] = dv_s[...]

    wide = pl.BlockSpec((tq, 2 * LANES), lambda p, ki, qi: (jnp.maximum(qi, ki), p))
    return pl.pallas_call(
        body, name="mla_attn_bwd", grid=(N_PAIR, nk, nq),
        in_specs=[wide, pl.BlockSpec((tk, 2 * LANES), lambda p, ki, qi: (ki, p)),
                  pl.BlockSpec((tk, LANES), lambda p, ki, qi: (ki, MLA_HEADS + p)),
                  pl.BlockSpec((tq, LANES), lambda p, ki, qi: (jnp.maximum(qi, ki), N_PAIR + p)), wide, wide],
        out_specs=[pl.BlockSpec((S, 2 * LANES), lambda p, ki, qi: (0, p)),
                   pl.BlockSpec((tk, 2 * LANES), lambda p, ki, qi: (ki, p)), pl.BlockSpec((tk, LANES), lambda p, ki, qi: (ki, p))],
        out_shape=[jax.ShapeDtypeStruct((S, MLA_HEADS * LANES), F32), jax.ShapeDtypeStruct((S, MLA_HEADS * LANES), F32),
                   jax.ShapeDtypeStruct((S, MLA_HEADS * MLA_V), F32)],
        scratch_shapes=[pltpu.VMEM((tk, 2 * LANES), F32), pltpu.VMEM((tk, LANES), F32)],
        compiler_params=_cparams(("parallel", "arbitrary", "arbitrary")),
    )(q_all, kv_all, kv_all, dcat, lse, delta)


MEM_SCALE = MEM_HEAD_DIM ** -0.5


def _mem_probs(q, k):
    s = lax.dot_general(q, k, (((1,), (1,)), ((), ())), preferred_element_type=F32) * MEM_SCALE
    e = jnp.exp(s - jnp.max(s, axis=-1, keepdims=True))
    return e / jnp.sum(e, axis=-1, keepdims=True)


def _mem_attn_fwd(qm, km, vm, tq=512):
    S, W = qm.shape
    M = km.shape[0]

    def body(q_ref, k_ref, v_ref, o_ref):
        for h in range(MEM_HEADS):
            sl = slice(h * MEM_HEAD_DIM, (h + 1) * MEM_HEAD_DIM)
            p = _mem_probs(q_ref[:, sl], k_ref[:, sl])
            o_ref[:, sl] = jnp.dot(p.astype(BF16), v_ref[:, sl], preferred_element_type=F32).astype(BF16)

    row = pl.BlockSpec((tq, W), lambda i: (i, 0))
    full = pl.BlockSpec((M, W), lambda i: (0, 0))
    return pl.pallas_call(body, name="mem_attn_fwd", grid=(S // tq,), in_specs=[row, full, full], out_specs=row,
                          out_shape=jax.ShapeDtypeStruct((S, W), BF16), compiler_params=_cparams(("parallel",)))(qm, km, vm)


def _mem_attn_bwd(qm, km, vm, dom, tq=512):
    S, W = qm.shape
    M = km.shape[0]

    def body(q_ref, k_ref, v_ref, do_ref, dq_ref, dk_ref, dv_ref):
        i = pl.program_id(0)

        @pl.when(i == 0)
        def _():
            dk_ref[...] = jnp.zeros(dk_ref.shape, F32)
            dv_ref[...] = jnp.zeros(dv_ref.shape, F32)

        for h in range(MEM_HEADS):
            sl = slice(h * MEM_HEAD_DIM, (h + 1) * MEM_HEAD_DIM)
            q, k, v, do = q_ref[:, sl], k_ref[:, sl], v_ref[:, sl], do_ref[:, sl]
            p = _mem_probs(q, k)
            dv_ref[:, sl] += lax.dot_general(p.astype(BF16), do, (((0,), (0,)), ((), ())), preferred_element_type=F32)
            dp = lax.dot_general(do, v, (((1,), (1,)), ((), ())), preferred_element_type=F32)
            ds = (p * (dp - jnp.sum(dp * p, axis=-1, keepdims=True)) * MEM_SCALE).astype(BF16)
            dq_ref[:, sl] = jnp.dot(ds, k, preferred_element_type=F32).astype(BF16)
            dk_ref[:, sl] += lax.dot_general(ds, q, (((0,), (0,)), ((), ())), preferred_element_type=F32)

    row = pl.BlockSpec((tq, W), lambda i: (i, 0))
    full = pl.BlockSpec((M, W), lambda i: (0, 0))
    return pl.pallas_call(
        body, name="mem_attn_bwd", grid=(S // tq,), in_specs=[row, full, full, row], out_specs=[row, full, full],
        out_shape=[jax.ShapeDtypeStruct((S, W), BF16), jax.ShapeDtypeStruct((M, W), F32), jax.ShapeDtypeStruct((M, W), F32)],
        compiler_params=_cparams(("arbitrary",)),
    )(qm, km, vm, dom)


L = SSD_CHUNK
N_SPAIR = SSD_HEADS // 2
GRP_W = SSD_INNER // 2
XB0, XC0 = SSD_INNER, SSD_INNER + 2 * SSD_STATE


def _cumsum_rows(a, reverse=False):
    row = lax.broadcasted_iota(jnp.int32, a.shape, 0)
    x, sft = a, 1
    while sft < L:
        if reverse:
            x = x + jnp.where(row < L - sft, pltpu.roll(x, L - sft, axis=0), 0.0)
        else:
            x = x + jnp.where(row >= sft, pltpu.roll(x, sft, axis=0), 0.0)
        sft *= 2
    return x


def _shift_down(cur, prev, s):
    if s == 0:
        return cur
    row = lax.broadcasted_iota(jnp.int32, cur.shape, 0)
    return jnp.where(row < s, pltpu.roll(prev, s, axis=0), pltpu.roll(cur, s, axis=0))


def _shift_up(cur, nxt, s):
    if s == 0:
        return cur
    row = lax.broadcasted_iota(jnp.int32, cur.shape, 0)
    return jnp.where(row >= L - s, pltpu.roll(nxt, L - s, axis=0), pltpu.roll(cur, L - s, axis=0))


def _ssd_conv(u, prev, cw, cb):
    delayed = [u] + [_shift_down(u, prev, s) for s in range(1, SSD_CONV)]
    conv = cb + cw[SSD_CONV - 1:SSD_CONV, :] * u
    for s in range(1, SSD_CONV):
        conv = conv + cw[SSD_CONV - 1 - s:SSD_CONV - s, :] * delayed[s]
    return conv, delayed


def _pair_lanes(v, h0, first):
    return jnp.where(first, v[:, h0:h0 + 1], v[:, h0 + 1:h0 + 2])


def _ssd_common(u, prev, dt_raw, cw, cb, dtb, alog):
    conv, delayed = _ssd_conv(u, prev, cw, cb)
    sg = _sigmoid(conv)
    xa = conv * sg
    dpre = dt_raw + dtb
    dtv = jnp.maximum(dpre, 0.0) + jnp.log1p(jnp.exp(-jnp.abs(dpre)))
    a_row = -jnp.exp(alog)
    cs = _cumsum_rows(dtv * a_row)
    return conv, sg, xa, dpre, dtv, a_row, cs, delayed


def _ssd_pair_fwd(xa, dtv, cs, csT, G, Cg, Bg, Sp, dsk, pp, first, tri, rowfirst):
    h0 = 2 * pp
    x = xa[:, pp * LANES:(pp + 1) * LANES]
    xdt = x * _pair_lanes(dtv, h0, first)
    xdt_b = xdt.astype(BF16)
    Ms, yd = [], []
    for h in (h0, h0 + 1):
        lam = jnp.exp(jnp.where(tri, cs[:, h:h + 1] - csT[h:h + 1, :], NEG))
        M = G * lam
        Ms.append((M, lam))
        yd.append(jnp.dot(M.astype(BF16), xdt_b, preferred_element_type=F32))
    T = lax.dot_general(Cg, Sp.astype(BF16), (((1,), (1,)), ((), ())), preferred_element_type=F32)
    E = jnp.exp(_pair_lanes(cs, h0, first))
    yoff = E * T
    csl = cs[L - 1:L, :]
    Fd = jnp.exp(_pair_lanes(csl, h0, first) - _pair_lanes(cs, h0, first))
    el = jnp.exp(csl)
    el_rows = jnp.where(rowfirst, el[:, h0:h0 + 1], el[:, h0 + 1:h0 + 2])
    Sloc = lax.dot_general((xdt * Fd).astype(BF16), Bg, (((0,), (0,)), ((), ())), preferred_element_type=F32)
    S_new = el_rows * Sp + Sloc
    y = jnp.where(first, yd[0], yd[1]) + yoff + x * _pair_lanes(dsk, h0, first[:1])
    return y, S_new, (x, xdt, xdt_b, Ms, E, yoff, Fd, el, el_rows)


def _ssd_masks():
    lane = lax.broadcasted_iota(jnp.int32, (L, LANES), 1)
    row = lax.broadcasted_iota(jnp.int32, (L, LANES), 0)
    return lane, row, lane < SSD_HEAD_DIM, row >= lane, row[:, :1] < SSD_HEAD_DIM


def _ssd_specs(nc, rev):
    def cidx(i):
        return nc - 1 - i if rev else i
    z = pl.BlockSpec((L, SSD_INNER), lambda i: (cidx(i), PZ // SSD_INNER))
    u = pl.BlockSpec((L, SSD_XBC), lambda i: (cidx(i), PX // SSD_XBC))
    dt = pl.BlockSpec((L, LANES), lambda i: (cidx(i), PDT // LANES))
    return cidx, z, u, dt


def _vec(w):
    return pl.BlockSpec((1, w), lambda i: (0, 0))


def _ssd_fwd(proj, cw, cb, dtb, alog, dsk, ng):
    S = proj.shape[0]
    nc = S // L

    def body(z_ref, u_ref, dt_ref, cw_ref, cb_ref, dtb_ref, alog_ref, dsk_ref, ng_ref, y_ref, st_ref, prev_s, state_s):
        c = pl.program_id(0)

        @pl.when(c == 0)
        def _():
            prev_s[...] = jnp.zeros(prev_s.shape, F32)
            state_s[...] = jnp.zeros(state_s.shape, F32)

        u = u_ref[...]
        _, _, xa, _, dtv, _, cs, _ = _ssd_common(u, prev_s[...], dt_ref[...], cw_ref[...], cb_ref[...], dtb_ref[...], alog_ref[...])
        prev_s[...] = u
        csT = cs.T
        _, _, first, tri, rowfirst = _ssd_masks()
        dsk_v = dsk_ref[...]
        ys = []
        for g in range(2):
            Bg = xa[:, XB0 + g * SSD_STATE:XB0 + (g + 1) * SSD_STATE].astype(BF16)
            Cg = xa[:, XC0 + g * SSD_STATE:XC0 + (g + 1) * SSD_STATE].astype(BF16)
            G = lax.dot_general(Cg, Bg, (((1,), (1,)), ((), ())), preferred_element_type=F32)
            for pp in (2 * g, 2 * g + 1):
                Sp = state_s[pp]
                st_ref[pp * LANES:(pp + 1) * LANES, :] = Sp
                y, S_new, _ = _ssd_pair_fwd(xa, dtv, cs, csT, G, Cg, Bg, Sp, dsk_v, pp, first, tri, rowfirst)
                state_s[pp] = S_new
                ys.append(y)
        z = z_ref[...]
        for g in range(2):
            yg = jnp.concatenate([ys[2 * g], ys[2 * g + 1]], axis=1)
            zg = z[:, g * GRP_W:(g + 1) * GRP_W]
            gated = yg * (zg * _sigmoid(zg))
            r = lax.rsqrt(jnp.mean(gated * gated, axis=-1, keepdims=True) + RMS_EPS)
            y_ref[:, g * GRP_W:(g + 1) * GRP_W] = gated * r * ng_ref[:, g * GRP_W:(g + 1) * GRP_W]

    _, zs, us, dts = _ssd_specs(nc, False)
    return pl.pallas_call(
        body, name="ssd_fwd", grid=(nc,),
        in_specs=[zs, us, dts, pl.BlockSpec((8, SSD_XBC), lambda i: (0, 0)), _vec(SSD_XBC), _vec(LANES), _vec(LANES), _vec(LANES),
                  _vec(SSD_INNER)],
        out_specs=[pl.BlockSpec((L, SSD_INNER), lambda i: (i, 0)), pl.BlockSpec((N_SPAIR * LANES, SSD_STATE), lambda i: (i, 0))],
        out_shape=[jax.ShapeDtypeStruct((S, SSD_INNER), F32), jax.ShapeDtypeStruct((nc * N_SPAIR * LANES, SSD_STATE), F32)],
        scratch_shapes=[pltpu.VMEM((L, SSD_XBC), F32), pltpu.VMEM((N_SPAIR, LANES, SSD_STATE), F32)],
        compiler_params=_cparams(("arbitrary",)),
    )(proj, proj, proj, cw, cb, dtb, alog, dsk, ng)


def _ssd_bwd(proj, states, dy, cw, cb, dtb, alog, dsk, ng):
    S = proj.shape[0]
    nc = S // L

    def body(z_ref, u_ref, up_ref, dt_ref, st_ref, dy_ref, cw_ref, cb_ref, dtb_ref, alog_ref, dsk_ref, ng_ref,
             dz_ref, du_ref, ddt_ref, dcw_ref, dcb_ref, ddtb_ref, dalog_ref, ddsk_ref, dng_ref,
             dS_s, dconv_s, dD_s):
        i = pl.program_id(0)
        c = nc - 1 - i

        @pl.when(i == 0)
        def _():
            dS_s[...] = jnp.zeros(dS_s.shape, F32)
            dconv_s[...] = jnp.zeros(dconv_s.shape, F32)
            dD_s[...] = jnp.zeros(dD_s.shape, F32)
            for r in (dcw_ref, dcb_ref, ddtb_ref, dalog_ref, ddsk_ref, dng_ref):
                r[...] = jnp.zeros(r.shape, F32)

        u = u_ref[...]
        prev = jnp.where(c > 0, up_ref[...], 0.0)
        cw_v = cw_ref[...]
        conv, sg, xa, dpre, dtv, a_row, cs, delayed = _ssd_common(u, prev, dt_ref[...], cw_v, cb_ref[...], dtb_ref[...], alog_ref[...])
        csT = cs.T
        lane, row, first, tri, rowfirst = _ssd_masks()
        dsk_v = dsk_ref[...]

        fw = []
        Gs, Bs, Cs = [], [], []
        for g in range(2):
            Bg = xa[:, XB0 + g * SSD_STATE:XB0 + (g + 1) * SSD_STATE].astype(BF16)
            Cg = xa[:, XC0 + g * SSD_STATE:XC0 + (g + 1) * SSD_STATE].astype(BF16)
            G = lax.dot_general(Cg, Bg, (((1,), (1,)), ((), ())), preferred_element_type=F32)
            Gs.append(G), Bs.append(Bg), Cs.append(Cg)
            for pp in (2 * g, 2 * g + 1):
                Sp = st_ref[pp * LANES:(pp + 1) * LANES, :]
                y, _, keep = _ssd_pair_fwd(xa, dtv, cs, csT, G, Cg, Bg, Sp, dsk_v, pp, first, tri, rowfirst)
                fw.append((y, Sp, keep))

        z = z_ref[...]
        dys = []
        for g in range(2):
            sl = slice(g * GRP_W, (g + 1) * GRP_W)
            yg = jnp.concatenate([fw[2 * g][0], fw[2 * g + 1][0]], axis=1)
            zg = z[:, sl]
            sz = _sigmoid(zg)
            silu_z = zg * sz
            gated = yg * silu_z
            r = lax.rsqrt(jnp.mean(gated * gated, axis=-1, keepdims=True) + RMS_EPS)
            nh = gated * r
            dout = dy_ref[:, sl]
            dng_ref[:, sl] += jnp.sum(dout * nh, axis=0, keepdims=True)
            dnh = dout * ng_ref[:, sl]
            dgated = r * (dnh - nh * jnp.mean(dnh * nh, axis=-1, keepdims=True))
            dz_ref[:, sl] = dgated * yg * (sz * (1.0 + zg * (1.0 - sz)))
            dyg = dgated * silu_z
            dys.append(dyg[:, :LANES]), dys.append(dyg[:, LANES:])

        dcs_col = [0.0] * SSD_HEADS
        dcs_row = [None] * SSD_HEADS
        ddt_col = [None] * SSD_HEADS
        dxs = []
        dB, dC = [None, None], [None, None]
        last = row == L - 1
        for g in range(2):
            Bg, Cg, G = Bs[g], Cs[g], Gs[g]
            dG = jnp.zeros((L, L), F32)
            dBg = jnp.zeros((L, SSD_STATE), F32)
            dCg = jnp.zeros((L, SSD_STATE), F32)
            for pp in (2 * g, 2 * g + 1):
                h0 = 2 * pp
                y, Sp, (x, xdt, xdt_b, Ms, E, yoff, Fd, el, el_rows) = fw[pp]
                dY = dys[pp]
                dS = dS_s[pp]
                dS_b, Sp_b = dS.astype(BF16), Sp.astype(BF16)
                dD_s[:, pp * LANES:(pp + 1) * LANES] += jnp.sum(dY * x, axis=0, keepdims=True)
                dx = dY * _pair_lanes(dsk_v, h0, first[:1])
                dxdt = jnp.zeros((L, LANES), F32)
                dY_b = dY.astype(BF16)
                for hh, h in enumerate((h0, h0 + 1)):
                    hm = first if hh == 0 else ~first
                    M, lam = Ms[hh]
                    dYh = jnp.where(hm, dY, 0.0).astype(BF16)
                    dM = lax.dot_general(dYh, xdt_b, (((1,), (1,)), ((), ())), preferred_element_type=F32)
                    W = dM * M
                    dcs_col[h] = dcs_col[h] + jnp.sum(W, axis=-1, keepdims=True)
                    dcs_row[h] = jnp.sum(W, axis=0, keepdims=True)
                    dG = dG + dM * lam
                    mt = lax.dot_general(M.astype(BF16), dY_b, (((0,), (0,)), ((), ())), preferred_element_type=F32)
                    dxdt = dxdt + jnp.where(hm, mt, 0.0)
                dT = (E * dY).astype(BF16)
                dCg = dCg + jnp.dot(dT, Sp_b, preferred_element_type=F32)
                dS_in = lax.dot_general(dT, Cg, (((0,), (0,)), ((), ())), preferred_element_type=F32) + el_rows * dS
                q1 = dY * yoff
                dZ = lax.dot_general(Bg, dS_b, (((1,), (1,)), ((), ())), preferred_element_type=F32)
                dBg = dBg + jnp.dot((xdt * Fd).astype(BF16), dS_b, preferred_element_type=F32)
                dxdt = dxdt + dZ * Fd
                q2 = dZ * xdt * Fd
                dSS = dS * Sp
                for hh, h in enumerate((h0, h0 + 1)):
                    hm = first if hh == 0 else ~first
                    rs1 = jnp.sum(jnp.where(hm, q1, 0.0), axis=-1, keepdims=True)
                    rs2 = jnp.sum(jnp.where(hm, q2, 0.0), axis=-1, keepdims=True)
                    rmask = rowfirst if hh == 0 else ~rowfirst
                    d_el = jnp.sum(jnp.sum(jnp.where(rmask, dSS, 0.0), axis=-1, keepdims=True), axis=0, keepdims=True)
                    tail = jnp.sum(rs2, axis=0, keepdims=True) + d_el * el[:, h:h + 1]
                    dcs_col[h] = dcs_col[h] + (rs1 - rs2 + jnp.where(last[:, :1], tail, 0.0))
                    ddt_col[h] = jnp.sum(jnp.where(hm, dxdt * x, 0.0), axis=-1, keepdims=True)
                dS_s[pp] = dS_in
                dxs.append(dx + dxdt * _pair_lanes(dtv, h0, first))
            dG_b = dG.astype(BF16)
            dC[g] = dCg + jnp.dot(dG_b, Bg, preferred_element_type=F32)
            dB[g] = dBg + lax.dot_general(dG_b, Cg, (((0,), (0,)), ((), ())), preferred_element_type=F32)

        dcs_c, dcs_r, ddt_c = (jnp.zeros((L, LANES), F32) for _ in range(3))
        for h in range(SSD_HEADS):
            dcs_c = dcs_c + jnp.where(lane == h, dcs_col[h], 0.0)
            dcs_r = dcs_r + jnp.where(row == h, dcs_row[h], 0.0)
            ddt_c = ddt_c + jnp.where(lane == h, ddt_col[h], 0.0)
        dcs = dcs_c - dcs_r.T
        da = _cumsum_rows(dcs, reverse=True)
        ddt_c = ddt_c + da * a_row
        dalog_ref[...] += jnp.sum(da * dtv, axis=0, keepdims=True) * a_row
        ddt_raw = ddt_c * _sigmoid(dpre)
        ddt_ref[...] = ddt_raw
        ddtb_ref[...] += jnp.sum(ddt_raw, axis=0, keepdims=True)

        dxa = jnp.concatenate(dxs + dB + dC, axis=1)
        dconv = dxa * (sg * (1.0 + conv * (1.0 - sg)))
        dcb_ref[...] += jnp.sum(dconv, axis=0, keepdims=True)
        nxt = dconv_s[...]
        du = cw_v[SSD_CONV - 1:SSD_CONV, :] * dconv
        dcw_ref[SSD_CONV - 1:SSD_CONV, :] += jnp.sum(dconv * u, axis=0, keepdims=True)
        for s in range(1, SSD_CONV):
            k = SSD_CONV - 1 - s
            du = du + cw_v[k:k + 1, :] * _shift_up(dconv, nxt, s)
            dcw_ref[k:k + 1, :] += jnp.sum(dconv * delayed[s], axis=0, keepdims=True)
        du_ref[...] = du
        dconv_s[...] = dconv

        @pl.when(i == nc - 1)
        def _():
            acc = dD_s[...]
            lane1 = lax.broadcasted_iota(jnp.int32, (1, LANES), 1)
            lanew = lax.broadcasted_iota(jnp.int32, acc.shape, 1)
            out = jnp.zeros((1, LANES), F32)
            for h in range(SSD_HEADS):
                tot = jnp.sum(jnp.where((lanew >= h * SSD_HEAD_DIM) & (lanew < (h + 1) * SSD_HEAD_DIM), acc, 0.0),
                              axis=-1, keepdims=True)
                out = out + jnp.where(lane1 == h, tot, 0.0)
            ddsk_ref[...] = out

    cidx, zs, us, dts = _ssd_specs(nc, True)
    ups = pl.BlockSpec((L, SSD_XBC), lambda i: (jnp.maximum(cidx(i) - 1, 0), PX // SSD_XBC))
    rowc = lambda w: pl.BlockSpec((L, w), lambda i: (cidx(i), 0))
    return pl.pallas_call(
        body, name="ssd_bwd", grid=(nc,),
        in_specs=[zs, us, ups, dts, pl.BlockSpec((N_SPAIR * LANES, SSD_STATE), lambda i: (cidx(i), 0)), rowc(SSD_INNER),
                  pl.BlockSpec((8, SSD_XBC), lambda i: (0, 0)), _vec(SSD_XBC), _vec(LANES), _vec(LANES), _vec(LANES), _vec(SSD_INNER)],
        out_specs=[rowc(SSD_INNER), rowc(SSD_XBC), rowc(LANES), pl.BlockSpec((8, SSD_XBC), lambda i: (0, 0)), _vec(SSD_XBC),
                   _vec(LANES), _vec(LANES), _vec(LANES), _vec(SSD_INNER)],
        out_shape=[jax.ShapeDtypeStruct((S, SSD_INNER), F32), jax.ShapeDtypeStruct((S, SSD_XBC), F32),
                   jax.ShapeDtypeStruct((S, LANES), F32), jax.ShapeDtypeStruct((8, SSD_XBC), F32),
                   jax.ShapeDtypeStruct((1, SSD_XBC), F32), jax.ShapeDtypeStruct((1, LANES), F32),
                   jax.ShapeDtypeStruct((1, LANES), F32), jax.ShapeDtypeStruct((1, LANES), F32),
                   jax.ShapeDtypeStruct((1, SSD_INNER), F32)],
        scratch_shapes=[pltpu.VMEM((N_SPAIR, LANES, SSD_STATE), F32), pltpu.VMEM((L, SSD_XBC), F32),
                        pltpu.VMEM((1, SSD_INNER), F32)],
        compiler_params=_cparams(("arbitrary",)),
    )(proj, proj, proj, proj, states, dy, cw, cb, dtb, alog, dsk, ng)


_IN_SEGS = ((PZ, 0, 512), (PX, 512, 1024), (PDT, 1536, 8), (PQ, 1544, 384), (PKV, 1928, 256), (PKR + KR_LANE, 2184, 32))


def _pad_w_in(w):
    parts, at = [], 0
    for dst, src, n in sorted(_IN_SEGS):
        parts += [jnp.zeros((w.shape[0], dst - at), w.dtype), w[:, src:src + n]]
        at = dst + n
    return jnp.concatenate(parts + [jnp.zeros((w.shape[0], PW - at), w.dtype)], axis=1)


def _unpad_w_in(wp):
    segs = sorted(_IN_SEGS, key=lambda t: t[1])
    return jnp.concatenate([wp[:, dst:dst + n] for dst, src, n in segs], axis=1)


def _pad_w_q(w):
    return jnp.pad(w.reshape(MLA_Q_RANK, MLA_HEADS, MLA_QK), ((0, 0), (0, 0), (0, LANES - MLA_QK))).reshape(MLA_Q_RANK, MLA_HEADS * LANES)


def _unpad_w_q(wp):
    return wp.reshape(MLA_Q_RANK, MLA_HEADS, LANES)[:, :, :MLA_QK].reshape(MLA_Q_RANK, MLA_HEADS * MLA_QK)


def _pad_w_kv(w):
    w3 = w.reshape(MLA_KV_RANK, MLA_HEADS, MLA_NOPE + MLA_V)
    k = jnp.pad(w3[:, :, :MLA_NOPE], ((0, 0), (0, 0), (0, LANES - MLA_NOPE))).reshape(MLA_KV_RANK, MLA_HEADS * LANES)
    return jnp.concatenate([k, w3[:, :, MLA_NOPE:].reshape(MLA_KV_RANK, MLA_HEADS * MLA_V)], axis=1)


def _unpad_w_kv(wp):
    k = wp[:, :MLA_HEADS * LANES].reshape(MLA_KV_RANK, MLA_HEADS, LANES)[:, :, :MLA_NOPE]
    v = wp[:, MLA_HEADS * LANES:].reshape(MLA_KV_RANK, MLA_HEADS, MLA_V)
    return jnp.concatenate([k, v], axis=2).reshape(MLA_KV_RANK, MLA_HEADS * (MLA_NOPE + MLA_V))


def _head_lanes(v):
    return jnp.pad(v, ((0, 0), (0, LANES - v.shape[1])))


def _local_step(x, mem, positions, tgt, P, weights_at=None, emit=None):
    tabs = _rope_tables(positions)
    G = {}
    emit = emit or (lambda names, grads: 0.0)

    h0, h0b = _ln_fwd([(x, 1.0)], P["ln_in_g"], P["ln_in_b"], "ln_in")
    if weights_at is not None:
        P = {**P, **weights_at("first", (h0b,) + tuple(tabs))}
    proj = _mm(h0b, P["w_in"], "nn", "proj_in", tm=1024, tn=640)
    if weights_at is not None:
        P = {**P, **weights_at("mid", proj)}
    y_ssd, states = _ssd_fwd(proj, P["conv_w"], P["conv_b"], P["dt_bias"], P["a_log"], P["d_skip"], P["ssd_norm_g"])
    qn, kvn, kpe = _mla_prep(proj, P["q_norm_g"], P["kv_norm_g"], tabs)

    def q_epi(acc, c, s1, s2):
        return (jnp.concatenate([_rope_block(acc[:, h * LANES:(h + 1) * LANES], c, s1, s2) for h in range(MLA_HEADS)], axis=1)
                * Q_PRESCALE,)

    q_all = _mm(qn, P["w_q_up"], "nn", "q_up", out_dtypes=(BF16,), epi=q_epi, extras=[(t, "m") for t in tabs])

    def kv_epi(acc, kp):
        kb = [acc[:, h * LANES:(h + 1) * LANES] + kp for h in range(MLA_HEADS)]
        return (jnp.concatenate(kb + [acc[:, MLA_HEADS * LANES:]], axis=1),)

    kv_all = _mm(kvn, P["w_kv_up"], "nn", "kv_up", out_dtypes=(BF16,), epi=kv_epi, extras=[(kpe, "m")])
    o_att, lse = _attn_fwd(q_all, kv_all)
    cat = jnp.concatenate([y_ssd, o_att], axis=1).astype(BF16)
    def resid_epi(acc, h):
        return (ALPHA * h + acc,)

    r1 = _mm(cat, P["w_mix_out"], "nn", "mix_out", tm=1024, epi=resid_epi, extras=[(h0, "mn")])
    h1, h1b = _ln_fwd([(r1, 1.0)], P["ln1_g"], P["ln1_b"], "ln1")
    if weights_at is not None:
        P = {**P, **weights_at("late", h1b)}
    qm = _mm(h1b, P["w_mem_q"], "nn", "mem_q", tm=1024, out_dtypes=(BF16,))
    km = _mm(mem, P["w_mem_k"], "nn", "mem_k", out_dtypes=(BF16,))
    vm = _mm(mem, P["w_mem_v"], "nn", "mem_v", out_dtypes=(BF16,))
    om = _mem_attn_fwd(qm, km, vm)
    r2 = _mm(om, P["w_mem_o"], "nn", "mem_o", tm=1024, epi=resid_epi, extras=[(h1, "mn")])
    h2, h2b = _ln_fwd([(r2, 1.0)], P["ln2_g"], P["ln2_b"], "ln2")

    def up_epi(acc):
        r = jnp.maximum(acc, 0.0)
        return (r * r,)

    act = _mm(h2b, P["w_up"], "nn", "mlp_up", tm=1024, tn=1024, out_dtypes=(BF16,), epi=up_epi)
    r3 = _mm(act, P["w_down"], "nn", "mlp_down", tm=1024, tk=D_FF, epi=resid_epi, extras=[(h2, "mn")])

    loss, dr3, dr3b, G["ln3_g"], G["ln3_b"] = _ln_loss_bwd([(r3, 1.0)], P["ln3_g"], P["ln3_b"], tgt, "ln3_loss")

    def dact_epi(acc, a):
        return (acc * (2.0 * jnp.sqrt(a.astype(F32))),)

    du = _mm(dr3b, P["w_down"], "nt", "mlp_down_dx", tm=1024, tn=1024, out_dtypes=(BF16,), epi=dact_epi, extras=[(act, "mn")])
    G["w_down"] = _mm(act, dr3b, "tn", "mlp_down_dw", tm=1024, tk=TOK_K, out_dtypes=(BF16,))
    G["w_up"] = _mm(h2b, du, "tn", "mlp_up_dw", tm=1024, tn=D_FF // N_DEV, tk=TOK_K, out_dtypes=(BF16,), col_slots=True)
    tie = emit(("w_down", "w_up"), G)
    dh2 = _mm(du, P["w_up"], "nt", "mlp_up_dx", tm=1024, tk=D_FF, epi=resid_epi, extras=[(dr3, "mn")])
    dr2, dr2b, G["ln2_g"], G["ln2_b"] = _ln_bwd([(r2, 1.0)], [(dh2, 1.0)], P["ln2_g"] + tie, "ln2_bwd")

    dom = _mm(dr2b, P["w_mem_o"], "nt", "mem_o_dx", tm=1024, out_dtypes=(BF16,))
    G["w_mem_o"] = _mm(om, dr2b, "tn", "mem_o_dw", tm=1024, tk=TOK_K, out_dtypes=(BF16,))
    dqm, dkm, dvm = _mem_attn_bwd(qm, km, vm, dom)
    G["w_mem_q"] = _mm(h1b, dqm, "tn", "mem_q_dw", tm=1024, tk=TOK_K, out_dtypes=(BF16,))
    G["w_mem_k"] = _mm(mem, dkm, "tn", "mem_k_dw", tm=1024, out_dtypes=(BF16,))
    G["w_mem_v"] = _mm(mem, dvm, "tn", "mem_v_dw", tm=1024, out_dtypes=(BF16,))
    tie = emit(("w_mem_o", "w_mem_q", "w_mem_k", "w_mem_v"), G)
    dh1 = _mm(dqm, P["w_mem_q"], "nt", "mem_q_dx", tm=1024, epi=resid_epi, extras=[(dr2, "mn")])
    dr1, dr1b, G["ln1_g"], G["ln1_b"] = _ln_bwd([(r1, 1.0)], [(dh1, 1.0)], P["ln1_g"] + tie, "ln1_bwd")

    def dcat_epi(acc, o):
        return acc, _attn_delta(acc[:, SSD_INNER:], o)

    dcat, delta = _mm(dr1b, P["w_mix_out"], "nt", "mix_out_dx", tm=512, out_dtypes=(F32, F32), epi=dcat_epi, extras=[(o_att, "m")])
    G["w_mix_out"] = _mm(cat, dr1b, "tn", "mix_out_dw", tm=1024, tk=TOK_K, out_dtypes=(BF16,))
    dq_all, dk_all, dv_all = _attn_bwd(q_all, kv_all, dcat, lse, delta)
    dq_pre = _rope_bwd_all(dq_all, tabs, DQ_POSTSCALE)
    G["w_q_up"] = _mm(qn, dq_pre, "tn", "q_up_dw", tk=TOK_K, out_dtypes=(BF16,))
    dqn = _mm(dq_pre, P["w_q_up"], "nt", "q_up_dx", tm=1024)
    dkv_all = jnp.concatenate([dk_all, dv_all], axis=1).astype(BF16)
    G["w_kv_up"] = _mm(kvn, dkv_all, "tn", "kv_up_dw", tk=TOK_K, out_dtypes=(BF16,))
    dkvn = _mm(dkv_all, P["w_kv_up"], "nt", "kv_up_dx", tm=1024)
    dql, dkvl, dkr, G["q_norm_g"], G["kv_norm_g"] = _mla_prep_bwd(proj, P["q_norm_g"], P["kv_norm_g"], tabs, dqn, dkvn, dk_all)
    (dz, dxbc, ddt, G["conv_w"], G["conv_b"], G["dt_bias"], G["a_log"], G["d_skip"], G["ssd_norm_g"]) = _ssd_bwd(
        proj, states, dcat, P["conv_w"], P["conv_b"], P["dt_bias"], P["a_log"], P["d_skip"], P["ssd_norm_g"])
    tie = emit(("w_mix_out", "w_q_up", "w_kv_up", "conv_w"), G)
    S = x.shape[0]
    dproj = jnp.concatenate([dql, jnp.zeros((S, PZ - MLA_Q_RANK), F32) + tie, dz, dxbc, dkvl, ddt, dkr], axis=1).astype(BF16)
    G["w_in"] = _mm(h0b, dproj, "tn", "proj_in_dw", tm=1024, tn=640, tk=TOK_K, out_dtypes=(BF16,))
    tie = emit(("w_in",), G)
    dh0 = _mm(dproj, P["w_in"], "nt", "proj_in_dx", tm=1024, tk=PW, epi=lambda acc, t, d: (acc + t + ALPHA * d,),
              extras=[(jnp.zeros((1, D_MODEL), F32) + tie, "n"), (dr1, "mn")])
    gx, _, G["ln_in_g"], G["ln_in_b"] = _ln_bwd([(x, 1.0)], [(dh0, 1.0)], P["ln_in_g"] + tie, "ln_in_bwd")
    return loss, gx, G


PACK_W = 1024
BIG = (("w_in", (1024, 277), 1), ("conv_w", (4, 128), 1), ("w_q_up", (384, 96), 1), ("w_kv_up", (256, 128), 1),
       ("w_mix_out", (128, 1024), 0), ("w_mem_q", (128, 1024), 0), ("w_mem_k", (128, 1024), 0), ("w_mem_v", (128, 1024), 0),
       ("w_mem_o", (128, 1024), 0), ("w_up", (1024, 512), 1), ("w_down", (512, 1024), 0))
SMALL = ("ln_in_g", "ln_in_b", "conv_b", "ln1_g", "ln1_b", "ln2_g", "ln2_b", "ln3_g", "ln3_b",
         "ssd_norm_g", "q_norm_g", "kv_norm_g", "dt_bias", "a_log", "d_skip")
ALL_W = ("ln_in_g", "ln_in_b", "w_in", "conv_w", "conv_b", "dt_bias", "a_log", "d_skip", "ssd_norm_g", "q_norm_g", "w_q_up",
         "kv_norm_g", "w_kv_up", "w_mix_out", "ln1_g", "ln1_b", "w_mem_q", "w_mem_k", "w_mem_v", "w_mem_o", "ln2_g", "ln2_b",
         "w_up", "w_down", "ln3_g", "ln3_b")


SMALL_R = 16
LOSS_ROW = 15
_SMALL_ROWS = (("ln_in_g",), ("ln_in_b",), ("conv_b",), ("ln1_g",), ("ln1_b",), ("ln2_g",), ("ln2_b",), ("ln3_g",), ("ln3_b",),
               ("ssd_norm_g", "q_norm_g"), ("kv_norm_g", "dt_bias", "a_log", "d_skip"))
_SMALL_W = {"ssd_norm_g": 512, "q_norm_g": 384, "kv_norm_g": 256, "dt_bias": LANES, "a_log": LANES, "d_skip": LANES}


MESH = pl.DeviceIdType.MESH
ANY = pl.BlockSpec(memory_space=pl.ANY)
VM = pl.BlockSpec(memory_space=pltpu.VMEM)


def _coords():
    return lax.axis_index("x"), lax.axis_index("y"), lax.axis_index("c")


def _slot(px, py, pc):
    return 4 * px + 2 * py + pc


def _peer(k, x, y, c):
    dx, dy, dc = (k >> 2) & 1, (k >> 1) & 1, k & 1
    return (1 - x if dx else x, 1 - y if dy else y, 1 - c if dc else c)


def _adam(w, g, m, v):
    m = ADAM_B1 * m + (1.0 - ADAM_B1) * g
    v = ADAM_B2 * v + (1.0 - ADAM_B2) * (g * g)
    m_hat = m / (1.0 - ADAM_B1 ** ADAM_STEP)
    v_hat = v / (1.0 - ADAM_B2 ** ADAM_STEP)
    delta = -ADAM_LR * (m_hat / (jnp.sqrt(v_hat) + ADAM_EPS) + ADAM_WD * w)
    return delta, m, v


def _sum_slots(ref):
    tot = ref[0].astype(F32)
    for q in range(1, N_DEV):
        tot = tot + ref[q].astype(F32)
    return tot


def _row(v):
    return v.reshape(1, -1).astype(F32)


def kernel(x, mem, positions, ln_in_g, ln_in_b, w_in, conv_w, conv_b, dt_bias, a_log, d_skip, ssd_norm_g, q_norm_g, w_q_up, kv_norm_g, w_kv_up, w_mix_out, ln1_g, ln1_b, w_mem_q, w_mem_k, w_mem_v, w_mem_o, ln2_g, ln2_b, w_up, w_down, ln3_g, ln3_b, loss_target, m_ln_in_g, m_ln_in_b, m_w_in, m_conv_w, m_conv_b, m_dt_bias, m_a_log, m_d_skip, m_ssd_norm_g, m_q_norm_g, m_w_q_up, m_kv_norm_g, m_w_kv_up, m_w_mix_out, m_ln1_g, m_ln1_b, m_w_mem_q, m_w_mem_k, m_w_mem_v, m_w_mem_o, m_ln2_g, m_ln2_b, m_w_up, m_w_down, m_ln3_g, m_ln3_b, v_ln_in_g, v_ln_in_b, v_w_in, v_conv_w, v_conv_b, v_dt_bias, v_a_log, v_d_skip, v_ssd_norm_g, v_q_norm_g, v_w_q_up, v_kv_norm_g, v_w_kv_up, v_w_mix_out, v_ln1_g, v_ln1_b, v_w_mem_q, v_w_mem_k, v_w_mem_v, v_w_mem_o, v_ln2_g, v_ln2_b, v_w_up, v_w_down, v_ln3_g, v_ln3_b):
    a = dict(locals())
    W = {n: a[n] for n in ALL_W}
    M = {n: a["m_" + n] for n in ALL_W}
    V = {n: a["v_" + n] for n in ALL_W}
    return _step_overlapped(x, mem, positions, loss_target, W, M, V)


HBM = pl.BlockSpec(memory_space=pltpu.HBM)
SEM = pl.BlockSpec(memory_space=pltpu.SEMAPHORE)
EFFECT = pltpu.SideEffectType.DATAFLOW_SIDE_EFFECTING
SHARD_SHAPE = {n: s for n, s, _ in BIG}
SHARD_AXIS = {n: ax for n, _, ax in BIG}
GATHER_FIRST = ("w_in",)
GATHER_MID = ("conv_w", "w_q_up", "w_kv_up", "w_mix_out")
GATHER_LATE = ("w_mem_q", "w_mem_k", "w_mem_v", "w_mem_o", "w_up", "w_down")


def _my_slot():
    return _slot(*_coords())


def _group_copies(src_refs, land_refs, send_sems, recv_sems, slotted, landing_of_peer):
    x, y, c = _coords()
    my = _slot(x, y, c)
    cps = []
    for a, (s_ref, l_ref) in enumerate(zip(src_refs, land_refs)):
        for k in range(1, N_DEV):
            peer = _peer(k, x, y, c)
            cps.append(pltpu.make_async_remote_copy(
                src_ref=s_ref.at[_slot(*peer)] if slotted else s_ref,
                dst_ref=l_ref.at[_slot(*peer)] if landing_of_peer else l_ref.at[my],
                send_sem=send_sems.at[7 * a + k - 1], recv_sem=recv_sems.at[7 * a + k - 1],
                device_id=peer, device_id_type=MESH))
    return cps


def _send_start(srcs, lands, slotted, name):
    n = len(srcs)

    def body(*refs):
        for cp in _group_copies(refs[:n], refs[n:2 * n], refs[2 * n], refs[2 * n + 1], slotted, False):
            cp.start()
        refs[-1][...] = jnp.zeros(refs[-1].shape, F32)

    res = pl.pallas_call(
        body, name=name,
        out_shape=(pltpu.SemaphoreType.DMA((7 * n,)), pltpu.SemaphoreType.DMA((7 * n,)),
                   *[pltpu.HBM(a.shape, a.dtype) for a in srcs], *[pltpu.HBM(a.shape, a.dtype) for a in lands],
                   jax.ShapeDtypeStruct((8, LANES), F32)),
        in_specs=[HBM] * (2 * n), out_specs=(SEM, SEM, *[HBM] * (2 * n), VM),
        input_output_aliases={i: 2 + i for i in range(2 * n)},
        compiler_params=pltpu.CompilerParams(has_side_effects=EFFECT),
    )(*[pltpu.with_memory_space_constraint(a, pltpu.HBM) for a in list(srcs) + list(lands)])
    return (res[0], res[1], res[2:2 + n], res[2 + n:2 + 2 * n]), res[-1][:1, :1]


def _send_wait(started, after, slotted, name):
    send_sems, recv_sems, srcs, lands = started
    n = len(srcs)
    after = list(after) if isinstance(after, (list, tuple)) else [after]

    def body(*refs):
        for cp in _group_copies(refs[:n], refs[n:2 * n], refs[2 * n], refs[2 * n + 1], slotted, True):
            cp.wait_send()
            cp.wait_recv()

    res = pl.pallas_call(
        body, name=name, out_shape=tuple(pltpu.HBM(a.shape, a.dtype) for a in list(srcs) + list(lands)),
        in_specs=[HBM] * (2 * n) + [SEM, SEM] + [ANY] * len(after), out_specs=tuple([HBM] * (2 * n)),
        input_output_aliases={i: i for i in range(2 * n)},
        compiler_params=pltpu.CompilerParams(has_side_effects=EFFECT),
    )(*srcs, *lands, send_sems, recv_sems, *after)
    return res[n:]


def _landing(own, my):
    return lax.dynamic_update_slice(lax.empty((N_DEV,) + own.shape, own.dtype), own[None], (my,) + (0,) * own.ndim)


def _full_from_slots(name, slots):
    a, b = SHARD_SHAPE[name]
    return slots.reshape(N_DEV * a, b) if SHARD_AXIS[name] == 0 else slots.transpose(1, 0, 2).reshape(a, N_DEV * b)


def _slots_from_full(name, g):
    a, b = SHARD_SHAPE[name]
    return g.reshape(N_DEV, a, b) if SHARD_AXIS[name] == 0 else g.reshape(a, N_DEV, b).transpose(1, 0, 2)


def _reduce_adam(recv, w, m, v, name):
    _, a, b = recv.shape
    ta = a
    while ta * b * 4 * N_DEV > 4 * 1024 * 1024 and ta % 16 == 0:
        ta //= 2

    def body(r_ref, w_ref, m_ref, v_ref, g_ref, d_ref, nm_ref, nv_ref):
        g = _sum_slots(r_ref)
        g_ref[...] = g
        d_ref[...], nm_ref[...], nv_ref[...] = _adam(w_ref[...], g, m_ref[...], v_ref[...])

    row = pl.BlockSpec((ta, b), lambda i: (i, 0))
    return pl.pallas_call(
        body, name=name, grid=(a // ta,),
        in_specs=[pl.BlockSpec((N_DEV, ta, b), lambda i: (0, i, 0)), row, row, row], out_specs=[row] * 4,
        out_shape=[jax.ShapeDtypeStruct((a, b), F32)] * 4, compiler_params=_cparams(("parallel",)),
    )(recv, w, m, v)


def _step_overlapped(x, mem, positions, tgt, W, M, V):
    my = _my_slot()
    shard = {n: W[n][0] for n, _, _ in BIG}
    send = {n: (shard[n] if n == "conv_w" else shard[n].astype(BF16)) for n in shard}

    first_src, mid_src, late_src = ([send[n] for n in grp] for grp in (GATHER_FIRST, GATHER_MID, GATHER_LATE))
    first, tie = _send_start(first_src, [_landing(s, my) for s in first_src], False, "gather_first_start")
    my_then = my + tie[0, 0].astype(jnp.int32)
    mid, tie = _send_start(mid_src, [_landing(s, my_then) for s in mid_src], False, "gather_mid_start")
    my_then = my + tie[0, 0].astype(jnp.int32)
    late, tie = _send_start(late_src, [_landing(s, my_then) for s in late_src], False, "gather_late_start")

    P = {n: _row(W[n]) for n in SMALL}
    for n in ("dt_bias", "a_log", "d_skip"):
        P[n] = _head_lanes(P[n])
    P["ln_in_g"] = P["ln_in_g"] + tie

    def weights_at(stage, after):
        if stage == "first":
            lands = _send_wait(first, after, False, "gather_first_wait")
            return dict(w_in=_pad_w_in(_full_from_slots("w_in", lands[0])))
        if stage == "mid":
            lands = _send_wait(mid, after, False, "gather_mid_wait")
            full = {n: _full_from_slots(n, l) for n, l in zip(GATHER_MID, lands)}
            return dict(w_q_up=_pad_w_q(full["w_q_up"]), w_kv_up=_pad_w_kv(full["w_kv_up"]), w_mix_out=full["w_mix_out"],
                        conv_w=jnp.pad(full["conv_w"], ((0, 8 - SSD_CONV), (0, 0))))
        lands = _send_wait(late, after, False, "gather_late_wait")
        return {n: _full_from_slots(n, l) for n, l in zip(GATHER_LATE, lands)}

    started, res = [], {}

    def finish(i, after):
        names, st = started[i]
        lands = _send_wait(st, after, True, "scatter_wait_%d" % i)
        for n, recv in zip(names, lands):
            res[n] = _reduce_adam(recv, shard[n], M[n][0], V[n][0], "reduce_adam_" + n)
        return sum(res[n][0][:1, :1] for n in names) * 0.0

    def emit(names, G):
        srcs = []
        for n in names:
            g = G[n]
            if n == "w_in":
                g = _unpad_w_in(g)
            elif n == "w_q_up":
                g = _unpad_w_q(g)
            elif n == "w_kv_up":
                g = _unpad_w_kv(g)
            elif n == "conv_w":
                g = g[:SSD_CONV]
            srcs.append(g if g.ndim == 3 else _slots_from_full(n, g))
        lands = [_landing(lax.dynamic_index_in_dim(s, my, 0, keepdims=False), my) for s in srcs]
        st, tie = _send_start(srcs, lands, True, "scatter_start_%d" % len(started))
        started.append((names, st))
        if len(started) == 3:
            tie = tie + finish(0, srcs[0]) + finish(1, srcs[0])
        return tie

    loss, gx, G = _local_step(x[0], mem[0], positions[0], tgt[0], P, weights_at, emit)
    finish(2, gx)
    finish(3, gx)

    small, loss_tot = _allreduce_adam_vectors(
        {n: G[n] for n in SMALL}, loss, {n: _row(W[n]) for n in SMALL}, {n: _row(M[n]) for n in SMALL}, {n: _row(V[n]) for n in SMALL})

    outs = []
    for j in range(4):
        for n in ALL_W:
            outs.append((res[n][j] if n in res else small[j][n]).reshape(W[n].shape))
    return (loss_tot[0, 0], gx[None], *outs)


def _vector_places():
    places = {}
    for r, names in enumerate(_SMALL_ROWS):
        c = 0
        for n in names:
            w = _SMALL_W.get(n, PACK_W)
            places[n] = (r, c, w, SSD_HEADS if n in ("dt_bias", "a_log", "d_skip") else w)
            c += w
    return places


def _allreduce_adam_vectors(grads, loss, Ws, Ms, Vs):
    places = _vector_places()
    ns = len(SMALL)

    def body(*refs):
        g_in, loss_in = refs[:ns], refs[ns]
        w_in, m_in, v_in = refs[ns + 1:2 * ns + 1], refs[2 * ns + 1:3 * ns + 1], refs[3 * ns + 1:4 * ns + 1]
        o = 4 * ns + 1
        outs = [refs[o + j * ns:o + (j + 1) * ns] for j in range(4)]
        loss_out, stage, land, send_sems, recv_sems = refs[o + 4 * ns:]
        stage[...] = jnp.zeros(stage.shape, F32)
        for i, n in enumerate(SMALL):
            r, c, w, _ = places[n]
            stage[r:r + 1, c:c + w] = g_in[i][...]
        stage[LOSS_ROW:LOSS_ROW + 1, 0:LANES] = loss_in[...]
        x, y, c_ = _coords()
        my = _slot(x, y, c_)
        cps = []
        for k in range(1, N_DEV):
            peer = _peer(k, x, y, c_)
            cps.append(pltpu.make_async_remote_copy(
                src_ref=stage, dst_ref=land.at[my], send_sem=send_sems.at[k - 1], recv_sem=recv_sems.at[k - 1],
                device_id=peer, device_id_type=MESH))
        for cp in cps:
            cp.start()
        land[my] = stage[...]
        for cp in cps:
            cp.wait_recv()
        for cp in cps:
            cp.wait_send()
        tot = _sum_slots(land)
        loss_out[...] = tot[LOSS_ROW:LOSS_ROW + 1, 0:LANES]
        for i, n in enumerate(SMALL):
            r, c, _, wt = places[n]
            g = tot[r:r + 1, c:c + wt]
            outs[0][i][...] = g
            outs[1][i][...], outs[2][i][...], outs[3][i][...] = _adam(w_in[i][...], g, m_in[i][...], v_in[i][...])

    shapes = [jax.ShapeDtypeStruct((1, places[n][3]), F32) for n in SMALL]
    res = pl.pallas_call(
        body, name="allreduce_vectors", in_specs=[VM] * (4 * ns + 1), out_specs=[VM] * (4 * ns + 1),
        out_shape=shapes * 4 + [jax.ShapeDtypeStruct((1, LANES), F32)],
        scratch_shapes=[pltpu.VMEM((SMALL_R, PACK_W), F32), pltpu.VMEM((N_DEV, SMALL_R, PACK_W), F32),
                        pltpu.SemaphoreType.DMA((7,)), pltpu.SemaphoreType.DMA((7,))],
    )(*[grads[n] for n in SMALL], loss, *[Ws[n] for n in SMALL], *[Ms[n] for n in SMALL], *[Vs[n] for n in SMALL])
    return [dict(zip(SMALL, res[j * ns:(j + 1) * ns])) for j in range(4)], res[4 * ns]
```

```python
import math

import jax
import jax.numpy as jnp
from jax import lax
from jax.experimental import pallas as pl
from jax.experimental.pallas import tpu as pltpu

F32, BF16 = jnp.float32, jnp.bfloat16

N_DEV = 8
D_MODEL = 1024
SSD_HEADS, SSD_HEAD_DIM, SSD_INNER, SSD_STATE, SSD_CONV, SSD_CHUNK = 8, 64, 512, 128, 4, 128
SSD_XBC = 1024
MLA_HEADS, MLA_NOPE, MLA_ROPE, MLA_QK, MLA_V = 8, 64, 32, 96, 64
MLA_Q_RANK, MLA_KV_RANK = 384, 256
ROPE_THETA = 10000.0
MEM_HEADS, MEM_HEAD_DIM = 4, 256
D_FF = 4096
IN_WIDTH = 2216
LN_EPS, RMS_EPS = 1e-5, 1e-6
ALPHA = 2.0 ** 0.25
ADAM_LR, ADAM_B1, ADAM_B2, ADAM_EPS, ADAM_WD, ADAM_STEP = 0.001, 0.9, 0.999, 1e-08, 0.01, 10

LANES = 128
NEG = -1e30
VMEM_LIMIT = 56 * 1024 * 1024
TOK_K = 4096

PQ, PZ, PX, PKV, PDT, PKR, PW = 0, 512, 1024, 2048, 2304, 2432, 2560
KR_LANE = 64


def _cparams(sem):
    return pltpu.CompilerParams(dimension_semantics=sem, vmem_limit_bytes=VMEM_LIMIT)


def _sigmoid(x):
    return 1.0 / (1.0 + jnp.exp(-x))


def _mm(a, b, mode, name, *, tm=512, tn=None, tk=None, out_dtypes=(F32,), epi=None, extras=(), col_slots=False):
    if mode == "nn":
        (M, K), (K2, N) = a.shape, b.shape
    elif mode == "nt":
        (M, K), (N, K2) = a.shape, b.shape
    else:
        (K, M), (K2, N) = a.shape, b.shape
    assert K == K2, (name, a.shape, b.shape)
    tm, tn, tk = min(tm, M), min(tn or N, N), min(tk or K, K)
    assert M % tm == 0 and N % tn == 0 and K % tk == 0, (name, M, N, K, tm, tn, tk)
    gk = K // tk
    a_spec = pl.BlockSpec((tk, tm), lambda i, j, k: (k, i)) if mode == "tn" else pl.BlockSpec((tm, tk), lambda i, j, k: (i, k))
    b_spec = pl.BlockSpec((tn, tk), lambda i, j, k: (j, k)) if mode == "nt" else pl.BlockSpec((tk, tn), lambda i, j, k: (k, j))
    dims = {"nn": ((1,), (0,)), "nt": ((1,), (1,)), "tn": ((0,), (0,))}[mode]
    ex_specs = []
    for arr, kind in extras:
        if kind == "mn":
            ex_specs.append(pl.BlockSpec((tm, tn), lambda i, j, k: (i, j)))
        elif kind == "n":
            ex_specs.append(pl.BlockSpec((1, tn), lambda i, j, k: (0, j)))
        else:
            ex_specs.append(pl.BlockSpec((tm, arr.shape[1]), lambda i, j, k: (i, 0)))
    ne, no = len(extras), len(out_dtypes)

    def body(*refs):
        a_ref, b_ref = refs[0], refs[1]
        ex, outs = refs[2:2 + ne], refs[2 + ne:2 + ne + no]
        part = lax.dot_general(a_ref[...].astype(BF16), b_ref[...].astype(BF16), (dims, ((), ())),
                               preferred_element_type=F32)

        def finish(acc):
            res = epi(acc, *[e[...] for e in ex]) if epi is not None else (acc,)
            for o, r in zip(outs, res):
                o[...] = r.astype(o.dtype)

        if gk == 1:
            finish(part)
        else:
            acc_ref = refs[-1]
            k = pl.program_id(2)

            @pl.when(k == 0)
            def _():
                acc_ref[...] = part

            @pl.when(k > 0)
            def _():
                acc_ref[...] += part

            @pl.when(k == gk - 1)
            def _():
                finish(acc_ref[...])

    res = pl.pallas_call(
        body, name=name, grid=(M // tm, N // tn, gk),
        in_specs=[a_spec, b_spec] + ex_specs,
        out_specs=[pl.BlockSpec((None, tm, tn), lambda i, j, k: (j, i, 0)) if col_slots else pl.BlockSpec((tm, tn), lambda i, j, k: (i, j))
                   for _ in out_dtypes],
        out_shape=[jax.ShapeDtypeStruct((N // tn, M, tn) if col_slots else (M, N), dt) for dt in out_dtypes],
        scratch_shapes=[pltpu.VMEM((tm, tn), F32)] if gk > 1 else [],
        compiler_params=_cparams(("parallel", "parallel", "arbitrary")),
    )(a, b, *[e[0] for e in extras])
    return res[0] if no == 1 else res


def _ln_stats(r):
    mu = jnp.mean(r, axis=-1, keepdims=True)
    xc = r - mu
    var = jnp.mean(xc * xc, axis=-1, keepdims=True)
    rstd = lax.rsqrt(var + LN_EPS)
    return xc * rstd, rstd


def _ln_fwd(terms, g, b, name, tm=512):
    S, D = terms[0][0].shape
    coefs = [c for _, c in terms]
    nt = len(terms)

    def body(*refs):
        r = sum(c * t[...] for t, c in zip(refs[:nt], coefs))
        xh, _ = _ln_stats(r)
        h = xh * refs[nt][...] + refs[nt + 1][...]
        refs[nt + 2][...] = h
        refs[nt + 3][...] = h.astype(BF16)

    row = pl.BlockSpec((tm, D), lambda i: (i, 0))
    vec = pl.BlockSpec((1, D), lambda i: (0, 0))
    return pl.pallas_call(
        body, name=name, grid=(S // tm,), in_specs=[row] * nt + [vec, vec], out_specs=[row, row],
        out_shape=[jax.ShapeDtypeStruct((S, D), F32), jax.ShapeDtypeStruct((S, D), BF16)], compiler_params=_cparams(("parallel",)),
    )(*[t for t, _ in terms], g, b)


def _ln_bwd(terms, dterms, g, name, tm=512):
    S, D = terms[0][0].shape
    coefs, dcoefs = [c for _, c in terms], [c for _, c in dterms]
    nt, nd = len(terms), len(dterms)

    def body(*refs):
        i = pl.program_id(0)
        r = sum(c * t[...] for t, c in zip(refs[:nt], coefs))
        dh = sum(c * t[...].astype(F32) for t, c in zip(refs[nt:nt + nd], dcoefs))
        g_ref = refs[nt + nd]
        dr_ref, drb_ref, dg_ref, db_ref = refs[nt + nd + 1:]
        xh, rstd = _ln_stats(r)
        dxh = dh * g_ref[...]
        m1 = jnp.mean(dxh, axis=-1, keepdims=True)
        m2 = jnp.mean(dxh * xh, axis=-1, keepdims=True)
        dr = rstd * (dxh - m1 - xh * m2)
        dr_ref[...] = dr
        drb_ref[...] = dr.astype(BF16)
        pg = jnp.sum(dh * xh, axis=0, keepdims=True)
        pb = jnp.sum(dh, axis=0, keepdims=True)

        @pl.when(i == 0)
        def _():
            dg_ref[...] = pg
            db_ref[...] = pb

        @pl.when(i > 0)
        def _():
            dg_ref[...] += pg
            db_ref[...] += pb

    row = pl.BlockSpec((tm, D), lambda i: (i, 0))
    vec = pl.BlockSpec((1, D), lambda i: (0, 0))
    return pl.pallas_call(
        body, name=name, grid=(S // tm,), in_specs=[row] * (nt + nd) + [vec], out_specs=[row, row, vec, vec],
        out_shape=[jax.ShapeDtypeStruct((S, D), F32), jax.ShapeDtypeStruct((S, D), BF16), jax.ShapeDtypeStruct((1, D), F32),
                   jax.ShapeDtypeStruct((1, D), F32)],
        compiler_params=_cparams(("arbitrary",)),
    )(*[t for t, _ in terms], *[t for t, _ in dterms], g)


def _ln_loss_bwd(terms, g, b, tgt, name, tm=512):
    S, D = terms[0][0].shape
    coefs = [c for _, c in terms]
    nt = len(terms)

    def body(*refs):
        i = pl.program_id(0)
        r = sum(c * t[...] for t, c in zip(refs[:nt], coefs))
        g_ref, b_ref, t_ref = refs[nt:nt + 3]
        loss_ref, dr_ref, drb_ref, dg_ref, db_ref = refs[nt + 3:]
        xh, rstd = _ln_stats(r)
        h = xh * g_ref[...] + b_ref[...]
        diff = h - t_ref[...]
        pl_ = 0.5 * jnp.sum(jnp.mean(diff * diff, axis=-1, keepdims=True), axis=0, keepdims=True)
        dh = diff * (1.0 / D)
        dxh = dh * g_ref[...]
        m1 = jnp.mean(dxh, axis=-1, keepdims=True)
        m2 = jnp.mean(dxh * xh, axis=-1, keepdims=True)
        dr = rstd * (dxh - m1 - xh * m2)
        dr_ref[...] = dr
        drb_ref[...] = dr.astype(BF16)
        pg = jnp.sum(dh * xh, axis=0, keepdims=True)
        pb = jnp.sum(dh, axis=0, keepdims=True)
        plb = jnp.broadcast_to(pl_, (1, LANES))

        @pl.when(i == 0)
        def _():
            dg_ref[...] = pg
            db_ref[...] = pb
            loss_ref[...] = plb

        @pl.when(i > 0)
        def _():
            dg_ref[...] += pg
            db_ref[...] += pb
            loss_ref[...] += plb

    row = pl.BlockSpec((tm, D), lambda i: (i, 0))
    vec = pl.BlockSpec((1, D), lambda i: (0, 0))
    lvec = pl.BlockSpec((1, LANES), lambda i: (0, 0))
    return pl.pallas_call(
        body, name=name, grid=(S // tm,), in_specs=[row] * nt + [vec, vec, row], out_specs=[lvec, row, row, vec, vec],
        out_shape=[jax.ShapeDtypeStruct((1, LANES), F32), jax.ShapeDtypeStruct((S, D), F32), jax.ShapeDtypeStruct((S, D), BF16),
                   jax.ShapeDtypeStruct((1, D), F32), jax.ShapeDtypeStruct((1, D), F32)],
        compiler_params=_cparams(("arbitrary",)),
    )(*[t for t, _ in terms], g, b, tgt)


def _rope_tables(positions):
    half = MLA_ROPE // 2
    inv_freq = jnp.power(ROPE_THETA, -jnp.arange(half, dtype=F32) / half)
    ang = positions.astype(F32)[:, None] * inv_freq
    cos, sin = jnp.cos(ang), jnp.sin(ang)
    S = positions.shape[0]
    one, zero = jnp.ones((S, MLA_NOPE), F32), jnp.zeros((S, half), F32)
    pad = jnp.zeros((S, LANES - MLA_QK), F32)
    c = jnp.concatenate([one, cos, cos, pad], axis=1)
    s1 = jnp.concatenate([0 * one, -sin, zero, pad], axis=1)
    s2 = jnp.concatenate([0 * one, zero, sin, pad], axis=1)
    return c, s1, s2


def _rope_block(x, c, s1, s2):
    half = MLA_ROPE // 2
    return x * c + pltpu.roll(x, LANES - half, axis=1) * s1 + pltpu.roll(x, half, axis=1) * s2


def _rms_fwd(x, g):
    r = lax.rsqrt(jnp.mean(x * x, axis=-1, keepdims=True) + RMS_EPS)
    return x * r * g


def _rms_bwd(x, g, dy):
    r = lax.rsqrt(jnp.mean(x * x, axis=-1, keepdims=True) + RMS_EPS)
    xh = x * r
    dyh = dy * g
    dx = r * (dyh - xh * jnp.mean(dyh * xh, axis=-1, keepdims=True))
    return dx, jnp.sum(dy * xh, axis=0, keepdims=True)


def _mla_prep(proj, qg, kvg, tabs, tm=512):
    S = proj.shape[0]

    def body(ql_ref, kvl_ref, kr_ref, qg_ref, kvg_ref, c_ref, s1_ref, s2_ref, qn_ref, kvn_ref, kpe_ref):
        qn_ref[...] = _rms_fwd(ql_ref[...], qg_ref[...]).astype(BF16)
        kvn_ref[...] = _rms_fwd(kvl_ref[...], kvg_ref[...]).astype(BF16)
        kpe_ref[...] = _rope_block(kr_ref[...], c_ref[...], s1_ref[...], s2_ref[...])

    tab = pl.BlockSpec((tm, LANES), lambda i: (i, 0))
    return pl.pallas_call(
        body, name="mla_prep", grid=(S // tm,),
        in_specs=[pl.BlockSpec((tm, MLA_Q_RANK), lambda i: (i, PQ // MLA_Q_RANK)),
                  pl.BlockSpec((tm, MLA_KV_RANK), lambda i: (i, PKV // MLA_KV_RANK)),
                  pl.BlockSpec((tm, LANES), lambda i: (i, PKR // LANES)),
                  pl.BlockSpec((1, MLA_Q_RANK), lambda i: (0, 0)), pl.BlockSpec((1, MLA_KV_RANK), lambda i: (0, 0)),
                  tab, tab, tab],
        out_specs=[pl.BlockSpec((tm, MLA_Q_RANK), lambda i: (i, 0)), pl.BlockSpec((tm, MLA_KV_RANK), lambda i: (i, 0)), tab],
        out_shape=[jax.ShapeDtypeStruct((S, MLA_Q_RANK), BF16), jax.ShapeDtypeStruct((S, MLA_KV_RANK), BF16),
                   jax.ShapeDtypeStruct((S, LANES), F32)],
        compiler_params=_cparams(("parallel",)),
    )(proj, proj, proj, qg, kvg, *tabs)


def _mla_prep_bwd(proj, qg, kvg, tabs, dqn, dkvn, dk_all, tm=512):
    S = proj.shape[0]

    def body(ql_ref, kvl_ref, qg_ref, kvg_ref, c_ref, s1_ref, s2_ref, dqn_ref, dkvn_ref, dk_ref,
             dql_ref, dkvl_ref, dkr_ref, dqg_ref, dkvg_ref):
        i = pl.program_id(0)
        dql, pq = _rms_bwd(ql_ref[...], qg_ref[...], dqn_ref[...])
        dkvl, pkv = _rms_bwd(kvl_ref[...], kvg_ref[...], dkvn_ref[...])
        dql_ref[...] = dql
        dkvl_ref[...] = dkvl
        dk = dk_ref[...]
        dkpe = dk[:, 0:LANES]
        for h in range(1, MLA_HEADS):
            dkpe = dkpe + dk[:, h * LANES:(h + 1) * LANES]
        lane = lax.broadcasted_iota(jnp.int32, dkpe.shape, 1)
        dkpe = jnp.where((lane >= KR_LANE) & (lane < KR_LANE + MLA_ROPE), dkpe, 0.0)
        dkr_ref[...] = _rope_block(dkpe, c_ref[...], -s1_ref[...], -s2_ref[...])

        @pl.when(i == 0)
        def _():
            dqg_ref[...] = pq
            dkvg_ref[...] = pkv

        @pl.when(i > 0)
        def _():
            dqg_ref[...] += pq
            dkvg_ref[...] += pkv

    tab = pl.BlockSpec((tm, LANES), lambda i: (i, 0))
    qspec = pl.BlockSpec((tm, MLA_Q_RANK), lambda i: (i, 0))
    kvspec = pl.BlockSpec((tm, MLA_KV_RANK), lambda i: (i, 0))
    qv, kvv = pl.BlockSpec((1, MLA_Q_RANK), lambda i: (0, 0)), pl.BlockSpec((1, MLA_KV_RANK), lambda i: (0, 0))
    return pl.pallas_call(
        body, name="mla_prep_bwd", grid=(S // tm,),
        in_specs=[pl.BlockSpec((tm, MLA_Q_RANK), lambda i: (i, PQ // MLA_Q_RANK)),
                  pl.BlockSpec((tm, MLA_KV_RANK), lambda i: (i, PKV // MLA_KV_RANK)),
                  qv, kvv, tab, tab, tab, qspec, kvspec, pl.BlockSpec((tm, MLA_HEADS * LANES), lambda i: (i, 0))],
        out_specs=[qspec, kvspec, tab, qv, kvv],
        out_shape=[jax.ShapeDtypeStruct((S, MLA_Q_RANK), F32), jax.ShapeDtypeStruct((S, MLA_KV_RANK), F32),
                   jax.ShapeDtypeStruct((S, LANES), F32), jax.ShapeDtypeStruct((1, MLA_Q_RANK), F32),
                   jax.ShapeDtypeStruct((1, MLA_KV_RANK), F32)],
        compiler_params=_cparams(("arbitrary",)),
    )(proj, proj, qg, kvg, *tabs, dqn, dkvn, dk_all)


def _rope_bwd_all(dq_all, tabs, scale, tm=512):
    S, W = dq_all.shape

    def body(dq_ref, c_ref, s1_ref, s2_ref, o_ref):
        c, s1, s2 = scale * c_ref[...], -scale * s1_ref[...], -scale * s2_ref[...]
        for h in range(W // LANES):
            o_ref[:, h * LANES:(h + 1) * LANES] = _rope_block(dq_ref[:, h * LANES:(h + 1) * LANES], c, s1, s2).astype(BF16)

    tab = pl.BlockSpec((tm, LANES), lambda i: (i, 0))
    row = pl.BlockSpec((tm, W), lambda i: (i, 0))
    return pl.pallas_call(body, name="rope_bwd", grid=(S // tm,), in_specs=[row, tab, tab, tab], out_specs=row,
                          out_shape=jax.ShapeDtypeStruct((S, W), BF16), compiler_params=_cparams(("parallel",)))(dq_all, *tabs)


ATT_SCALE = MLA_QK ** -0.5
LN2 = math.log(2.0)
Q_PRESCALE = ATT_SCALE / LN2
DQ_POSTSCALE = ATT_SCALE / LN2
N_PAIR = MLA_HEADS // 2


def _causal_mask(qi, ki, tq, tk):
    row = qi * tq + lax.broadcasted_iota(jnp.int32, (tq, tk), 0)
    col = ki * tk + lax.broadcasted_iota(jnp.int32, (tq, tk), 1)
    return col <= row


def _lane_tile(x, n):
    return jnp.concatenate([x] * n, axis=1) if n > 1 else x


def _attn_fwd(q_all, kv_all, tq=512, tk=2048):
    S = q_all.shape[0]
    tq, tk = min(tq, S), min(tk, S)
    nq, nk, nb, r = S // tq, S // tk, tk // LANES, tk // tq

    def body(q_ref, k_ref, v_ref, o_ref, lse_ref, m_s, l_s, acc_s):
        qi, ki = pl.program_id(1), pl.program_id(2)
        last = lax.div(qi, r)

        @pl.when(ki == 0)
        def _():
            m_s[...] = jnp.full(m_s.shape, NEG, F32)
            l_s[...] = jnp.zeros(l_s.shape, F32)
            acc_s[...] = jnp.zeros(acc_s.shape, F32)

        def block(kc, mask):
            v = v_ref[0:kc, :]
            for hh in range(2):
                q = q_ref[:, hh * LANES:(hh + 1) * LANES]
                k = k_ref[0:kc, hh * LANES:(hh + 1) * LANES]
                s = lax.dot_general(q, k, (((1,), (1,)), ((), ())), preferred_element_type=F32)
                if mask is not None:
                    s = jnp.where(mask, s, NEG)
                m_prev = m_s[hh]
                m_new = jnp.maximum(m_prev, jnp.max(s, axis=-1, keepdims=True))
                p = jnp.exp2(s - _lane_tile(m_new, kc // LANES))
                alpha = jnp.exp2(m_prev - m_new)
                ps = p[:, :LANES]
                for j in range(1, kc // LANES):
                    ps = ps + p[:, j * LANES:(j + 1) * LANES]
                l_s[hh] = alpha * l_s[hh] + ps
                acc_s[hh] = alpha * acc_s[hh] + jnp.dot(p.astype(BF16), v, preferred_element_type=F32)
                m_s[hh] = m_new

        @pl.when(ki < last)
        def _():
            block(tk, None)

        for j in range(r):
            @pl.when((ki == last) & (lax.rem(qi, r) == j))
            def _(j=j):
                kc = (j + 1) * tq
                block(kc, lax.broadcasted_iota(jnp.int32, (tq, kc), 1) <= j * tq + lax.broadcasted_iota(jnp.int32, (tq, kc), 0))

        @pl.when(ki == last)
        def _():
            first = lax.broadcasted_iota(jnp.int32, (tq, LANES), 1) < MLA_V
            l0 = jnp.sum(l_s[0], axis=-1, keepdims=True)
            l1 = jnp.sum(l_s[1], axis=-1, keepdims=True)
            o_ref[...] = jnp.where(first, acc_s[0] / l0, acc_s[1] / l1)
            lse_ref[:, :LANES] = m_s[0] + jnp.log2(l0)
            lse_ref[:, LANES:] = m_s[1] + jnp.log2(l1)

    return pl.pallas_call(
        body, name="mla_attn_fwd", grid=(N_PAIR, nq, nk),
        in_specs=[pl.BlockSpec((tq, 2 * LANES), lambda p, qi, ki: (qi, p)),
                  pl.BlockSpec((tk, 2 * LANES), lambda p, qi, ki: (jnp.minimum(ki, lax.div(qi, r)), p)),
                  pl.BlockSpec((tk, LANES), lambda p, qi, ki: (jnp.minimum(ki, lax.div(qi, r)), MLA_HEADS + p))],
        out_specs=[pl.BlockSpec((tq, LANES), lambda p, qi, ki: (qi, p)), pl.BlockSpec((tq, 2 * LANES), lambda p, qi, ki: (qi, p))],
        out_shape=[jax.ShapeDtypeStruct((S, MLA_HEADS * MLA_V), F32), jax.ShapeDtypeStruct((S, MLA_HEADS * LANES), F32)],
        scratch_shapes=[pltpu.VMEM((2, tq, LANES), F32), pltpu.VMEM((2, tq, LANES), F32), pltpu.VMEM((2, tq, LANES), F32)],
        compiler_params=_cparams(("parallel", "parallel", "arbitrary")),
    )(q_all, kv_all, kv_all)


def _attn_delta(do, o):
    prod = do * o
    tm = prod.shape[0]
    first = lax.broadcasted_iota(jnp.int32, (tm, LANES), 1) < MLA_V
    blocks = []
    for p in range(N_PAIR):
        pp = prod[:, p * LANES:(p + 1) * LANES]
        blocks.append(jnp.broadcast_to(jnp.sum(jnp.where(first, pp, 0.0), axis=-1, keepdims=True), (tm, LANES)))
        blocks.append(jnp.broadcast_to(jnp.sum(jnp.where(first, 0.0, pp), axis=-1, keepdims=True), (tm, LANES)))
    return jnp.concatenate(blocks, axis=1)


def _attn_bwd(q_all, kv_all, dcat, lse, delta, tq=1024, tk=1024):
    S = q_all.shape[0]
    tq, tk = min(tq, S), min(tk, S)
    nq, nk, nb = S // tq, S // tk, tk // LANES
    assert tq == tk

    def body(q_ref, k_ref, v_ref, do_ref, lse_ref, dl_ref, dq_ref, dk_ref, dv_ref, dk_s, dv_s):
        ki, qi = pl.program_id(1), pl.program_id(2)

        @pl.when((ki == 0) & (qi == 0))
        def _():
            dq_ref[...] = jnp.zeros(dq_ref.shape, F32)

        @pl.when(qi == 0)
        def _():
            dk_s[...] = jnp.zeros(dk_s.shape, F32)
            dv_s[...] = jnp.zeros(dv_s.shape, F32)

        def block(r0, nr, kc, mask):
            v, do = v_ref[0:kc, :], do_ref[r0:r0 + nr, :]
            first = lax.broadcasted_iota(jnp.int32, (nr, LANES), 1) < MLA_V
            firstk = lax.broadcasted_iota(jnp.int32, (kc, LANES), 1) < MLA_V
            do_b = do.astype(BF16)
            rows = pl.ds(pl.multiple_of(qi * tq + r0, LANES), nr)
            for hh in range(2):
                sl = slice(hh * LANES, (hh + 1) * LANES)
                q, k = q_ref[r0:r0 + nr, sl], k_ref[0:kc, sl]
                s = lax.dot_general(q, k, (((1,), (1,)), ((), ())), preferred_element_type=F32)
                if mask is not None:
                    s = jnp.where(mask, s, NEG)
                p = jnp.exp2(s - _lane_tile(lse_ref[r0:r0 + nr, sl], kc // LANES))
                do_h = jnp.where(first if hh == 0 else ~first, do, 0.0).astype(BF16)
                dp = lax.dot_general(do_h, v, (((1,), (1,)), ((), ())), preferred_element_type=F32)
                ds_b = (p * (dp - _lane_tile(dl_ref[r0:r0 + nr, sl], kc // LANES)) * LN2).astype(BF16)
                pv = lax.dot_general(p.astype(BF16), do_b, (((0,), (0,)), ((), ())), preferred_element_type=F32)
                dv_s[0:kc, :] += jnp.where(firstk if hh == 0 else ~firstk, pv, 0.0)
                dk_s[0:kc, sl] += lax.dot_general(ds_b, q, (((0,), (0,)), ((), ())), preferred_element_type=F32)
                dq_ref[rows, sl] += jnp.dot(ds_b, k, preferred_element_type=F32)

        @pl.when(qi > ki)
        def _():
            block(0, tq, tk, None)

        @pl.when(qi == ki)
        def _():
            h = tq // 2
            block(0, h, h, _causal_mask(0, 0, h, h))
            block(h, h, tk, lax.broadcasted_iota(jnp.int32, (h, tk), 1) <= h + lax.broadcasted_iota(jnp.int32, (h, tk), 0))

        @pl.when(qi == nq - 1)
        def _():
            dk_ref[...] = dk_s[...]
            dv_ref[...] = dv_s[...]

    wide = pl.BlockSpec((tq, 2 * LANES), lambda p, ki, qi: (jnp.maximum(qi, ki), p))
    return pl.pallas_call(
        body, name="mla_attn_bwd", grid=(N_PAIR, nk, nq),
        in_specs=[wide, pl.BlockSpec((tk, 2 * LANES), lambda p, ki, qi: (ki, p)),
                  pl.BlockSpec((tk, LANES), lambda p, ki, qi: (ki, MLA_HEADS + p)),
                  pl.BlockSpec((tq, LANES), lambda p, ki, qi: (jnp.maximum(qi, ki), N_PAIR + p)), wide, wide],
        out_specs=[pl.BlockSpec((S, 2 * LANES), lambda p, ki, qi: (0, p)),
                   pl.BlockSpec((tk, 2 * LANES), lambda p, ki, qi: (ki, p)), pl.BlockSpec((tk, LANES), lambda p, ki, qi: (ki, p))],
        out_shape=[jax.ShapeDtypeStruct((S, MLA_HEADS * LANES), F32), jax.ShapeDtypeStruct((S, MLA_HEADS * LANES), F32),
                   jax.ShapeDtypeStruct((S, MLA_HEADS * MLA_V), F32)],
        scratch_shapes=[pltpu.VMEM((tk, 2 * LANES), F32), pltpu.VMEM((tk, LANES), F32)],
        compiler_params=_cparams(("parallel", "arbitrary", "arbitrary")),
    )(q_all, kv_all, kv_all, dcat, lse, delta)


MEM_SCALE = MEM_HEAD_DIM ** -0.5


def _mem_probs(q, k):
    s = lax.dot_general(q, k, (((1,), (1,)), ((), ())), preferred_element_type=F32) * MEM_SCALE
    e = jnp.exp(s - jnp.max(s, axis=-1, keepdims=True))
    return e / jnp.sum(e, axis=-1, keepdims=True)


def _mem_attn_fwd(qm, km, vm, tq=512):
    S, W = qm.shape
    M = km.shape[0]

    def body(q_ref, k_ref, v_ref, o_ref):
        for h in range(MEM_HEADS):
            sl = slice(h * MEM_HEAD_DIM, (h + 1) * MEM_HEAD_DIM)
            p = _mem_probs(q_ref[:, sl], k_ref[:, sl])
            o_ref[:, sl] = jnp.dot(p.astype(BF16), v_ref[:, sl], preferred_element_type=F32).astype(BF16)

    row = pl.BlockSpec((tq, W), lambda i: (i, 0))
    full = pl.BlockSpec((M, W), lambda i: (0, 0))
    return pl.pallas_call(body, name="mem_attn_fwd", grid=(S // tq,), in_specs=[row, full, full], out_specs=row,
                          out_shape=jax.ShapeDtypeStruct((S, W), BF16), compiler_params=_cparams(("parallel",)))(qm, km, vm)


def _mem_attn_bwd(qm, km, vm, dom, tq=512):
    S, W = qm.shape
    M = km.shape[0]

    def body(q_ref, k_ref, v_ref, do_ref, dq_ref, dk_ref, dv_ref):
        i = pl.program_id(0)

        @pl.when(i == 0)
        def _():
            dk_ref[...] = jnp.zeros(dk_ref.shape, F32)
            dv_ref[...] = jnp.zeros(dv_ref.shape, F32)

        for h in range(MEM_HEADS):
            sl = slice(h * MEM_HEAD_DIM, (h + 1) * MEM_HEAD_DIM)
            q, k, v, do = q_ref[:, sl], k_ref[:, sl], v_ref[:, sl], do_ref[:, sl]
            p = _mem_probs(q, k)
            dv_ref[:, sl] += lax.dot_general(p.astype(BF16), do, (((0,), (0,)), ((), ())), preferred_element_type=F32)
            dp = lax.dot_general(do, v, (((1,), (1,)), ((), ())), preferred_element_type=F32)
            ds = (p * (dp - jnp.sum(dp * p, axis=-1, keepdims=True)) * MEM_SCALE).astype(BF16)
            dq_ref[:, sl] = jnp.dot(ds, k, preferred_element_type=F32).astype(BF16)
            dk_ref[:, sl] += lax.dot_general(ds, q, (((0,), (0,)), ((), ())), preferred_element_type=F32)

    row = pl.BlockSpec((tq, W), lambda i: (i, 0))
    full = pl.BlockSpec((M, W), lambda i: (0, 0))
    return pl.pallas_call(
        body, name="mem_attn_bwd", grid=(S // tq,), in_specs=[row, full, full, row], out_specs=[row, full, full],
        out_shape=[jax.ShapeDtypeStruct((S, W), BF16), jax.ShapeDtypeStruct((M, W), F32), jax.ShapeDtypeStruct((M, W), F32)],
        compiler_params=_cparams(("arbitrary",)),
    )(qm, km, vm, dom)


L = SSD_CHUNK
N_SPAIR = SSD_HEADS // 2
GRP_W = SSD_INNER // 2
XB0, XC0 = SSD_INNER, SSD_INNER + 2 * SSD_STATE


def _cumsum_rows(a, reverse=False):
    row = lax.broadcasted_iota(jnp.int32, a.shape, 0)
    x, sft = a, 1
    while sft < L:
        if reverse:
            x = x + jnp.where(row < L - sft, pltpu.roll(x, L - sft, axis=0), 0.0)
        else:
            x = x + jnp.where(row >= sft, pltpu.roll(x, sft, axis=0), 0.0)
        sft *= 2
    return x


def _shift_down(cur, prev, s):
    if s == 0:
        return cur
    row = lax.broadcasted_iota(jnp.int32, cur.shape, 0)
    return jnp.where(row < s, pltpu.roll(prev, s, axis=0), pltpu.roll(cur, s, axis=0))


def _shift_up(cur, nxt, s):
    if s == 0:
        return cur
    row = lax.broadcasted_iota(jnp.int32, cur.shape, 0)
    return jnp.where(row >= L - s, pltpu.roll(nxt, L - s, axis=0), pltpu.roll(cur, L - s, axis=0))


def _ssd_conv(u, prev, cw, cb):
    delayed = [u] + [_shift_down(u, prev, s) for s in range(1, SSD_CONV)]
    conv = cb + cw[SSD_CONV - 1:SSD_CONV, :] * u
    for s in range(1, SSD_CONV):
        conv = conv + cw[SSD_CONV - 1 - s:SSD_CONV - s, :] * delayed[s]
    return conv, delayed


def _pair_lanes(v, h0, first):
    return jnp.where(first, v[:, h0:h0 + 1], v[:, h0 + 1:h0 + 2])


def _ssd_common(u, prev, dt_raw, cw, cb, dtb, alog):
    conv, delayed = _ssd_conv(u, prev, cw, cb)
    sg = _sigmoid(conv)
    xa = conv * sg
    dpre = dt_raw + dtb
    dtv = jnp.maximum(dpre, 0.0) + jnp.log1p(jnp.exp(-jnp.abs(dpre)))
    a_row = -jnp.exp(alog)
    cs = _cumsum_rows(dtv * a_row)
    return conv, sg, xa, dpre, dtv, a_row, cs, delayed


def _ssd_pair_fwd(xa, dtv, cs, csT, G, Cg, Bg, Sp, dsk, pp, first, tri, rowfirst):
    h0 = 2 * pp
    x = xa[:, pp * LANES:(pp + 1) * LANES]
    xdt = x * _pair_lanes(dtv, h0, first)
    xdt_b = xdt.astype(BF16)
    Ms, yd = [], []
    for h in (h0, h0 + 1):
        lam = jnp.exp(jnp.where(tri, cs[:, h:h + 1] - csT[h:h + 1, :], NEG))
        M = G * lam
        Ms.append((M, lam))
        yd.append(jnp.dot(M.astype(BF16), xdt_b, preferred_element_type=F32))
    T = lax.dot_general(Cg, Sp.astype(BF16), (((1,), (1,)), ((), ())), preferred_element_type=F32)
    E = jnp.exp(_pair_lanes(cs, h0, first))
    yoff = E * T
    csl = cs[L - 1:L, :]
    Fd = jnp.exp(_pair_lanes(csl, h0, first) - _pair_lanes(cs, h0, first))
    el = jnp.exp(csl)
    el_rows = jnp.where(rowfirst, el[:, h0:h0 + 1], el[:, h0 + 1:h0 + 2])
    Sloc = lax.dot_general((xdt * Fd).astype(BF16), Bg, (((0,), (0,)), ((), ())), preferred_element_type=F32)
    S_new = el_rows * Sp + Sloc
    y = jnp.where(first, yd[0], yd[1]) + yoff + x * _pair_lanes(dsk, h0, first[:1])
    return y, S_new, (x, xdt, xdt_b, Ms, E, yoff, Fd, el, el_rows)


def _ssd_masks():
    lane = lax.broadcasted_iota(jnp.int32, (L, LANES), 1)
    row = lax.broadcasted_iota(jnp.int32, (L, LANES), 0)
    return lane, row, lane < SSD_HEAD_DIM, row >= lane, row[:, :1] < SSD_HEAD_DIM


def _ssd_specs(nc, rev):
    def cidx(i):
        return nc - 1 - i if rev else i
    z = pl.BlockSpec((L, SSD_INNER), lambda i: (cidx(i), PZ // SSD_INNER))
    u = pl.BlockSpec((L, SSD_XBC), lambda i: (cidx(i), PX // SSD_XBC))
    dt = pl.BlockSpec((L, LANES), lambda i: (cidx(i), PDT // LANES))
    return cidx, z, u, dt


def _vec(w):
    return pl.BlockSpec((1, w), lambda i: (0, 0))


def _ssd_fwd(proj, cw, cb, dtb, alog, dsk, ng):
    S = proj.shape[0]
    nc = S // L

    def body(z_ref, u_ref, dt_ref, cw_ref, cb_ref, dtb_ref, alog_ref, dsk_ref, ng_ref, y_ref, st_ref, prev_s, state_s):
        c = pl.program_id(0)

        @pl.when(c == 0)
        def _():
            prev_s[...] = jnp.zeros(prev_s.shape, F32)
            state_s[...] = jnp.zeros(state_s.shape, F32)

        u = u_ref[...]
        _, _, xa, _, dtv, _, cs, _ = _ssd_common(u, prev_s[...], dt_ref[...], cw_ref[...], cb_ref[...], dtb_ref[...], alog_ref[...])
        prev_s[...] = u
        csT = cs.T
        _, _, first, tri, rowfirst = _ssd_masks()
        dsk_v = dsk_ref[...]
        ys = []
        for g in range(2):
            Bg = xa[:, XB0 + g * SSD_STATE:XB0 + (g + 1) * SSD_STATE].astype(BF16)
            Cg = xa[:, XC0 + g * SSD_STATE:XC0 + (g + 1) * SSD_STATE].astype(BF16)
            G = lax.dot_general(Cg, Bg, (((1,), (1,)), ((), ())), preferred_element_type=F32)
            for pp in (2 * g, 2 * g + 1):
                Sp = state_s[pp]
                st_ref[pp * LANES:(pp + 1) * LANES, :] = Sp
                y, S_new, _ = _ssd_pair_fwd(xa, dtv, cs, csT, G, Cg, Bg, Sp, dsk_v, pp, first, tri, rowfirst)
                state_s[pp] = S_new
                ys.append(y)
        z = z_ref[...]
        for g in range(2):
            yg = jnp.concatenate([ys[2 * g], ys[2 * g + 1]], axis=1)
            zg = z[:, g * GRP_W:(g + 1) * GRP_W]
            gated = yg * (zg * _sigmoid(zg))
            r = lax.rsqrt(jnp.mean(gated * gated, axis=-1, keepdims=True) + RMS_EPS)
            y_ref[:, g * GRP_W:(g + 1) * GRP_W] = gated * r * ng_ref[:, g * GRP_W:(g + 1) * GRP_W]

    _, zs, us, dts = _ssd_specs(nc, False)
    return pl.pallas_call(
        body, name="ssd_fwd", grid=(nc,),
        in_specs=[zs, us, dts, pl.BlockSpec((8, SSD_XBC), lambda i: (0, 0)), _vec(SSD_XBC), _vec(LANES), _vec(LANES), _vec(LANES),
                  _vec(SSD_INNER)],
        out_specs=[pl.BlockSpec((L, SSD_INNER), lambda i: (i, 0)), pl.BlockSpec((N_SPAIR * LANES, SSD_STATE), lambda i: (i, 0))],
        out_shape=[jax.ShapeDtypeStruct((S, SSD_INNER), F32), jax.ShapeDtypeStruct((nc * N_SPAIR * LANES, SSD_STATE), F32)],
        scratch_shapes=[pltpu.VMEM((L, SSD_XBC), F32), pltpu.VMEM((N_SPAIR, LANES, SSD_STATE), F32)],
        compiler_params=_cparams(("arbitrary",)),
    )(proj, proj, proj, cw, cb, dtb, alog, dsk, ng)


def _ssd_bwd(proj, states, dy, cw, cb, dtb, alog, dsk, ng):
    S = proj.shape[0]
    nc = S // L

    def body(z_ref, u_ref, up_ref, dt_ref, st_ref, dy_ref, cw_ref, cb_ref, dtb_ref, alog_ref, dsk_ref, ng_ref,
             dz_ref, du_ref, ddt_ref, dcw_ref, dcb_ref, ddtb_ref, dalog_ref, ddsk_ref, dng_ref,
             dS_s, dconv_s, dD_s):
        i = pl.program_id(0)
        c = nc - 1 - i

        @pl.when(i == 0)
        def _():
            dS_s[...] = jnp.zeros(dS_s.shape, F32)
            dconv_s[...] = jnp.zeros(dconv_s.shape, F32)
            dD_s[...] = jnp.zeros(dD_s.shape, F32)
            for r in (dcw_ref, dcb_ref, ddtb_ref, dalog_ref, ddsk_ref, dng_ref):
                r[...] = jnp.zeros(r.shape, F32)

        u = u_ref[...]
        prev = jnp.where(c > 0, up_ref[...], 0.0)
        cw_v = cw_ref[...]
        conv, sg, xa, dpre, dtv, a_row, cs, delayed = _ssd_common(u, prev, dt_ref[...], cw_v, cb_ref[...], dtb_ref[...], alog_ref[...])
        csT = cs.T
        lane, row, first, tri, rowfirst = _ssd_masks()
        dsk_v = dsk_ref[...]

        fw = []
        Gs, Bs, Cs = [], [], []
        for g in range(2):
            Bg = xa[:, XB0 + g * SSD_STATE:XB0 + (g + 1) * SSD_STATE].astype(BF16)
            Cg = xa[:, XC0 + g * SSD_STATE:XC0 + (g + 1) * SSD_STATE].astype(BF16)
            G = lax.dot_general(Cg, Bg, (((1,), (1,)), ((), ())), preferred_element_type=F32)
            Gs.append(G), Bs.append(Bg), Cs.append(Cg)
            for pp in (2 * g, 2 * g + 1):
                Sp = st_ref[pp * LANES:(pp + 1) * LANES, :]
                y, _, keep = _ssd_pair_fwd(xa, dtv, cs, csT, G, Cg, Bg, Sp, dsk_v, pp, first, tri, rowfirst)
                fw.append((y, Sp, keep))

        z = z_ref[...]
        dys = []
        for g in range(2):
            sl = slice(g * GRP_W, (g + 1) * GRP_W)
            yg = jnp.concatenate([fw[2 * g][0], fw[2 * g + 1][0]], axis=1)
            zg = z[:, sl]
            sz = _sigmoid(zg)
            silu_z = zg * sz
            gated = yg * silu_z
            r = lax.rsqrt(jnp.mean(gated * gated, axis=-1, keepdims=True) + RMS_EPS)
            nh = gated * r
            dout = dy_ref[:, sl]
            dng_ref[:, sl] += jnp.sum(dout * nh, axis=0, keepdims=True)
            dnh = dout * ng_ref[:, sl]
            dgated = r * (dnh - nh * jnp.mean(dnh * nh, axis=-1, keepdims=True))
            dz_ref[:, sl] = dgated * yg * (sz * (1.0 + zg * (1.0 - sz)))
            dyg = dgated * silu_z
            dys.append(dyg[:, :LANES]), dys.append(dyg[:, LANES:])

        dcs_col = [0.0] * SSD_HEADS
        dcs_row = [None] * SSD_HEADS
        ddt_col = [None] * SSD_HEADS
        dxs = []
        dB, dC = [None, None], [None, None]
        last = row == L - 1
        for g in range(2):
            Bg, Cg, G = Bs[g], Cs[g], Gs[g]
            dG = jnp.zeros((L, L), F32)
            dBg = jnp.zeros((L, SSD_STATE), F32)
            dCg = jnp.zeros((L, SSD_STATE), F32)
            for pp in (2 * g, 2 * g + 1):
                h0 = 2 * pp
                y, Sp, (x, xdt, xdt_b, Ms, E, yoff, Fd, el, el_rows) = fw[pp]
                dY = dys[pp]
                dS = dS_s[pp]
                dS_b, Sp_b = dS.astype(BF16), Sp.astype(BF16)
                dD_s[:, pp * LANES:(pp + 1) * LANES] += jnp.sum(dY * x, axis=0, keepdims=True)
                dx = dY * _pair_lanes(dsk_v, h0, first[:1])
                dxdt = jnp.zeros((L, LANES), F32)
                dY_b = dY.astype(BF16)
                for hh, h in enumerate((h0, h0 + 1)):
                    hm = first if hh == 0 else ~first
                    M, lam = Ms[hh]
                    dYh = jnp.where(hm, dY, 0.0).astype(BF16)
                    dM = lax.dot_general(dYh, xdt_b, (((1,), (1,)), ((), ())), preferred_element_type=F32)
                    W = dM * M
                    dcs_col[h] = dcs_col[h] + jnp.sum(W, axis=-1, keepdims=True)
                    dcs_row[h] = jnp.sum(W, axis=0, keepdims=True)
                    dG = dG + dM * lam
                    mt = lax.dot_general(M.astype(BF16), dY_b, (((0,), (0,)), ((), ())), preferred_element_type=F32)
                    dxdt = dxdt + jnp.where(hm, mt, 0.0)
                dT = (E * dY).astype(BF16)
                dCg = dCg + jnp.dot(dT, Sp_b, preferred_element_type=F32)
                dS_in = lax.dot_general(dT, Cg, (((0,), (0,)), ((), ())), preferred_element_type=F32) + el_rows * dS
                q1 = dY * yoff
                dZ = lax.dot_general(Bg, dS_b, (((1,), (1,)), ((), ())), preferred_element_type=F32)
                dBg = dBg + jnp.dot((xdt * Fd).astype(BF16), dS_b, preferred_element_type=F32)
                dxdt = dxdt + dZ * Fd
                q2 = dZ * xdt * Fd
                dSS = dS * Sp
                for hh, h in enumerate((h0, h0 + 1)):
                    hm = first if hh == 0 else ~first
                    rs1 = jnp.sum(jnp.where(hm, q1, 0.0), axis=-1, keepdims=True)
                    rs2 = jnp.sum(jnp.where(hm, q2, 0.0), axis=-1, keepdims=True)
                    rmask = rowfirst if hh == 0 else ~rowfirst
                    d_el = jnp.sum(jnp.sum(jnp.where(rmask, dSS, 0.0), axis=-1, keepdims=True), axis=0, keepdims=True)
                    tail = jnp.sum(rs2, axis=0, keepdims=True) + d_el * el[:, h:h + 1]
                    dcs_col[h] = dcs_col[h] + (rs1 - rs2 + jnp.where(last[:, :1], tail, 0.0))
                    ddt_col[h] = jnp.sum(jnp.where(hm, dxdt * x, 0.0), axis=-1, keepdims=True)
                dS_s[pp] = dS_in
                dxs.append(dx + dxdt * _pair_lanes(dtv, h0, first))
            dG_b = dG.astype(BF16)
            dC[g] = dCg + jnp.dot(dG_b, Bg, preferred_element_type=F32)
            dB[g] = dBg + lax.dot_general(dG_b, Cg, (((0,), (0,)), ((), ())), preferred_element_type=F32)

        dcs_c, dcs_r, ddt_c = (jnp.zeros((L, LANES), F32) for _ in range(3))
        for h in range(SSD_HEADS):
            dcs_c = dcs_c + jnp.where(lane == h, dcs_col[h], 0.0)
            dcs_r = dcs_r + jnp.where(row == h, dcs_row[h], 0.0)
            ddt_c = ddt_c + jnp.where(lane == h, ddt_col[h], 0.0)
        dcs = dcs_c - dcs_r.T
        da = _cumsum_rows(dcs, reverse=True)
        ddt_c = ddt_c + da * a_row
        dalog_ref[...] += jnp.sum(da * dtv, axis=0, keepdims=True) * a_row
        ddt_raw = ddt_c * _sigmoid(dpre)
        ddt_ref[...] = ddt_raw
        ddtb_ref[...] += jnp.sum(ddt_raw, axis=0, keepdims=True)

        dxa = jnp.concatenate(dxs + dB + dC, axis=1)
        dconv = dxa * (sg * (1.0 + conv * (1.0 - sg)))
        dcb_ref[...] += jnp.sum(dconv, axis=0, keepdims=True)
        nxt = dconv_s[...]
        du = cw_v[SSD_CONV - 1:SSD_CONV, :] * dconv
        dcw_ref[SSD_CONV - 1:SSD_CONV, :] += jnp.sum(dconv * u, axis=0, keepdims=True)
        for s in range(1, SSD_CONV):
            k = SSD_CONV - 1 - s
            du = du + cw_v[k:k + 1, :] * _shift_up(dconv, nxt, s)
            dcw_ref[k:k + 1, :] += jnp.sum(dconv * delayed[s], axis=0, keepdims=True)
        du_ref[...] = du
        dconv_s[...] = dconv

        @pl.when(i == nc - 1)
        def _():
            acc = dD_s[...]
            lane1 = lax.broadcasted_iota(jnp.int32, (1, LANES), 1)
            lanew = lax.broadcasted_iota(jnp.int32, acc.shape, 1)
            out = jnp.zeros((1, LANES), F32)
            for h in range(SSD_HEADS):
                tot = jnp.sum(jnp.where((lanew >= h * SSD_HEAD_DIM) & (lanew < (h + 1) * SSD_HEAD_DIM), acc, 0.0),
                              axis=-1, keepdims=True)
                out = out + jnp.where(lane1 == h, tot, 0.0)
            ddsk_ref[...] = out

    cidx, zs, us, dts = _ssd_specs(nc, True)
    ups = pl.BlockSpec((L, SSD_XBC), lambda i: (jnp.maximum(cidx(i) - 1, 0), PX // SSD_XBC))
    rowc = lambda w: pl.BlockSpec((L, w), lambda i: (cidx(i), 0))
    return pl.pallas_call(
        body, name="ssd_bwd", grid=(nc,),
        in_specs=[zs, us, ups, dts, pl.BlockSpec((N_SPAIR * LANES, SSD_STATE), lambda i: (cidx(i), 0)), rowc(SSD_INNER),
                  pl.BlockSpec((8, SSD_XBC), lambda i: (0, 0)), _vec(SSD_XBC), _vec(LANES), _vec(LANES), _vec(LANES), _vec(SSD_INNER)],
        out_specs=[rowc(SSD_INNER), rowc(SSD_XBC), rowc(LANES), pl.BlockSpec((8, SSD_XBC), lambda i: (0, 0)), _vec(SSD_XBC),
                   _vec(LANES), _vec(LANES), _vec(LANES), _vec(SSD_INNER)],
        out_shape=[jax.ShapeDtypeStruct((S, SSD_INNER), F32), jax.ShapeDtypeStruct((S, SSD_XBC), F32),
                   jax.ShapeDtypeStruct((S, LANES), F32), jax.ShapeDtypeStruct((8, SSD_XBC), F32),
                   jax.ShapeDtypeStruct((1, SSD_XBC), F32), jax.ShapeDtypeStruct((1, LANES), F32),
                   jax.ShapeDtypeStruct((1, LANES), F32), jax.ShapeDtypeStruct((1, LANES), F32),
                   jax.ShapeDtypeStruct((1, SSD_INNER), F32)],
        scratch_shapes=[pltpu.VMEM((N_SPAIR, LANES, SSD_STATE), F32), pltpu.VMEM((L, SSD_XBC), F32),
                        pltpu.VMEM((1, SSD_INNER), F32)],
        compiler_params=_cparams(("arbitrary",)),
    )(proj, proj, proj, proj, states, dy, cw, cb, dtb, alog, dsk, ng)


_IN_SEGS = ((PZ, 0, 512), (PX, 512, 1024), (PDT, 1536, 8), (PQ, 1544, 384), (PKV, 1928, 256), (PKR + KR_LANE, 2184, 32))


def _pad_w_in(w):
    parts, at = [], 0
    for dst, src, n in sorted(_IN_SEGS):
        parts += [jnp.zeros((w.shape[0], dst - at), w.dtype), w[:, src:src + n]]
        at = dst + n
    return jnp.concatenate(parts + [jnp.zeros((w.shape[0], PW - at), w.dtype)], axis=1)


def _unpad_w_in(wp):
    segs = sorted(_IN_SEGS, key=lambda t: t[1])
    return jnp.concatenate([wp[:, dst:dst + n] for dst, src, n in segs], axis=1)


def _pad_w_q(w):
    return jnp.pad(w.reshape(MLA_Q_RANK, MLA_HEADS, MLA_QK), ((0, 0), (0, 0), (0, LANES - MLA_QK))).reshape(MLA_Q_RANK, MLA_HEADS * LANES)


def _unpad_w_q(wp):
    return wp.reshape(MLA_Q_RANK, MLA_HEADS, LANES)[:, :, :MLA_QK].reshape(MLA_Q_RANK, MLA_HEADS * MLA_QK)


def _pad_w_kv(w):
    w3 = w.reshape(MLA_KV_RANK, MLA_HEADS, MLA_NOPE + MLA_V)
    k = jnp.pad(w3[:, :, :MLA_NOPE], ((0, 0), (0, 0), (0, LANES - MLA_NOPE))).reshape(MLA_KV_RANK, MLA_HEADS * LANES)
    return jnp.concatenate([k, w3[:, :, MLA_NOPE:].reshape(MLA_KV_RANK, MLA_HEADS * MLA_V)], axis=1)


def _unpad_w_kv(wp):
    k = wp[:, :MLA_HEADS * LANES].reshape(MLA_KV_RANK, MLA_HEADS, LANES)[:, :, :MLA_NOPE]
    v = wp[:, MLA_HEADS * LANES:].reshape(MLA_KV_RANK, MLA_HEADS, MLA_V)
    return jnp.concatenate([k, v], axis=2).reshape(MLA_KV_RANK, MLA_HEADS * (MLA_NOPE + MLA_V))


def _head_lanes(v):
    return jnp.pad(v, ((0, 0), (0, LANES - v.shape[1])))


def _local_step(x, mem, positions, tgt, P, weights_at=None, emit=None):
    tabs = _rope_tables(positions)
    G = {}
    emit = emit or (lambda names, grads: 0.0)

    h0, h0b = _ln_fwd([(x, 1.0)], P["ln_in_g"], P["ln_in_b"], "ln_in")
    if weights_at is not None:
        P = {**P, **weights_at("first", (h0b,) + tuple(tabs))}
    proj = _mm(h0b, P["w_in"], "nn", "proj_in", tm=1024, tn=640)
    if weights_at is not None:
        P = {**P, **weights_at("mid", proj)}
    y_ssd, states = _ssd_fwd(proj, P["conv_w"], P["conv_b"], P["dt_bias"], P["a_log"], P["d_skip"], P["ssd_norm_g"])
    qn, kvn, kpe = _mla_prep(proj, P["q_norm_g"], P["kv_norm_g"], tabs)

    def q_epi(acc, c, s1, s2):
        return (jnp.concatenate([_rope_block(acc[:, h * LANES:(h + 1) * LANES], c, s1, s2) for h in range(MLA_HEADS)], axis=1)
                * Q_PRESCALE,)

    q_all = _mm(qn, P["w_q_up"], "nn", "q_up", out_dtypes=(BF16,), epi=q_epi, extras=[(t, "m") for t in tabs])

    def kv_epi(acc, kp):
        kb = [acc[:, h * LANES:(h + 1) * LANES] + kp for h in range(MLA_HEADS)]
        return (jnp.concatenate(kb + [acc[:, MLA_HEADS * LANES:]], axis=1),)

    kv_all = _mm(kvn, P["w_kv_up"], "nn", "kv_up", out_dtypes=(BF16,), epi=kv_epi, extras=[(kpe, "m")])
    o_att, lse = _attn_fwd(q_all, kv_all)
    cat = jnp.concatenate([y_ssd, o_att], axis=1).astype(BF16)
    def resid_epi(acc, h):
        return (ALPHA * h + acc,)

    r1 = _mm(cat, P["w_mix_out"], "nn", "mix_out", tm=1024, epi=resid_epi, extras=[(h0, "mn")])
    h1, h1b = _ln_fwd([(r1, 1.0)], P["ln1_g"], P["ln1_b"], "ln1")
    if weights_at is not None:
        P = {**P, **weights_at("late", h1b)}
    qm = _mm(h1b, P["w_mem_q"], "nn", "mem_q", tm=1024, out_dtypes=(BF16,))
    km = _mm(mem, P["w_mem_k"], "nn", "mem_k", out_dtypes=(BF16,))
    vm = _mm(mem, P["w_mem_v"], "nn", "mem_v", out_dtypes=(BF16,))
    om = _mem_attn_fwd(qm, km, vm)
    r2 = _mm(om, P["w_mem_o"], "nn", "mem_o", tm=1024, epi=resid_epi, extras=[(h1, "mn")])
    h2, h2b = _ln_fwd([(r2, 1.0)], P["ln2_g"], P["ln2_b"], "ln2")

    def up_epi(acc):
        r = jnp.maximum(acc, 0.0)
        return (r * r,)

    act = _mm(h2b, P["w_up"], "nn", "mlp_up", tm=1024, tn=1024, out_dtypes=(BF16,), epi=up_epi)
    r3 = _mm(act, P["w_down"], "nn", "mlp_down", tm=1024, tk=D_FF, epi=resid_epi, extras=[(h2, "mn")])

    loss, dr3, dr3b, G["ln3_g"], G["ln3_b"] = _ln_loss_bwd([(r3, 1.0)], P["ln3_g"], P["ln3_b"], tgt, "ln3_loss")

    def dact_epi(acc, a):
        return (acc * (2.0 * jnp.sqrt(a.astype(F32))),)

    du = _mm(dr3b, P["w_down"], "nt", "mlp_down_dx", tm=1024, tn=1024, out_dtypes=(BF16,), epi=dact_epi, extras=[(act, "mn")])
    G["w_down"] = _mm(act, dr3b, "tn", "mlp_down_dw", tm=1024, tk=TOK_K, out_dtypes=(BF16,))
    G["w_up"] = _mm(h2b, du, "tn", "mlp_up_dw", tm=1024, tn=D_FF // N_DEV, tk=TOK_K, out_dtypes=(BF16,), col_slots=True)
    tie = emit(("w_down", "w_up"), G)
    dh2 = _mm(du, P["w_up"], "nt", "mlp_up_dx", tm=1024, tk=D_FF, epi=resid_epi, extras=[(dr3, "mn")])
    dr2, dr2b, G["ln2_g"], G["ln2_b"] = _ln_bwd([(r2, 1.0)], [(dh2, 1.0)], P["ln2_g"] + tie, "ln2_bwd")

    dom = _mm(dr2b, P["w_mem_o"], "nt", "mem_o_dx", tm=1024, out_dtypes=(BF16,))
    G["w_mem_o"] = _mm(om, dr2b, "tn", "mem_o_dw", tm=1024, tk=TOK_K, out_dtypes=(BF16,))
    dqm, dkm, dvm = _mem_attn_bwd(qm, km, vm, dom)
    G["w_mem_q"] = _mm(h1b, dqm, "tn", "mem_q_dw", tm=1024, tk=TOK_K, out_dtypes=(BF16,))
    G["w_mem_k"] = _mm(mem, dkm, "tn", "mem_k_dw", tm=1024, out_dtypes=(BF16,))
    G["w_mem_v"] = _mm(mem, dvm, "tn", "mem_v_dw", tm=1024, out_dtypes=(BF16,))
    tie = emit(("w_mem_o", "w_mem_q", "w_mem_k", "w_mem_v"), G)
    dh1 = _mm(dqm, P["w_mem_q"], "nt", "mem_q_dx", tm=1024, epi=resid_epi, extras=[(dr2, "mn")])
    dr1, dr1b, G["ln1_g"], G["ln1_b"] = _ln_bwd([(r1, 1.0)], [(dh1, 1.0)], P["ln1_g"] + tie, "ln1_bwd")

    def dcat_epi(acc, o):
        return acc, _attn_delta(acc[:, SSD_INNER:], o)

    dcat, delta = _mm(dr1b, P["w_mix_out"], "nt", "mix_out_dx", tm=512, out_dtypes=(F32, F32), epi=dcat_epi, extras=[(o_att, "m")])
    G["w_mix_out"] = _mm(cat, dr1b, "tn", "mix_out_dw", tm=1024, tk=TOK_K, out_dtypes=(BF16,))
    dq_all, dk_all, dv_all = _attn_bwd(q_all, kv_all, dcat, lse, delta)
    dq_pre = _rope_bwd_all(dq_all, tabs, DQ_POSTSCALE)
    G["w_q_up"] = _mm(qn, dq_pre, "tn", "q_up_dw", tk=TOK_K, out_dtypes=(BF16,))
    dqn = _mm(dq_pre, P["w_q_up"], "nt", "q_up_dx", tm=1024)
    dkv_all = jnp.concatenate([dk_all, dv_all], axis=1).astype(BF16)
    G["w_kv_up"] = _mm(kvn, dkv_all, "tn", "kv_up_dw", tk=TOK_K, out_dtypes=(BF16,))
    dkvn = _mm(dkv_all, P["w_kv_up"], "nt", "kv_up_dx", tm=1024)
    dql, dkvl, dkr, G["q_norm_g"], G["kv_norm_g"] = _mla_prep_bwd(proj, P["q_norm_g"], P["kv_norm_g"], tabs, dqn, dkvn, dk_all)
    (dz, dxbc, ddt, G["conv_w"], G["conv_b"], G["dt_bias"], G["a_log"], G["d_skip"], G["ssd_norm_g"]) = _ssd_bwd(
        proj, states, dcat, P["conv_w"], P["conv_b"], P["dt_bias"], P["a_log"], P["d_skip"], P["ssd_norm_g"])
    tie = emit(("w_mix_out", "w_q_up", "w_kv_up", "conv_w"), G)
    S = x.shape[0]
    dproj = jnp.concatenate([dql, jnp.zeros((S, PZ - MLA_Q_RANK), F32) + tie, dz, dxbc, dkvl, ddt, dkr], axis=1).astype(BF16)
    G["w_in"] = _mm(h0b, dproj, "tn", "proj_in_dw", tm=1024, tn=640, tk=TOK_K, out_dtypes=(BF16,))
    tie = emit(("w_in",), G)
    dh0 = _mm(dproj, P["w_in"], "nt", "proj_in_dx", tm=1024, tk=PW, epi=lambda acc, t, d: (acc + t + ALPHA * d,),
              extras=[(jnp.zeros((1, D_MODEL), F32) + tie, "n"), (dr1, "mn")])
    gx, _, G["ln_in_g"], G["ln_in_b"] = _ln_bwd([(x, 1.0)], [(dh0, 1.0)], P["ln_in_g"] + tie, "ln_in_bwd")
    return loss, gx, G


PACK_W = 1024
BIG = (("w_in", (1024, 277), 1), ("conv_w", (4, 128), 1), ("w_q_up", (384, 96), 1), ("w_kv_up", (256, 128), 1),
       ("w_mix_out", (128, 1024), 0), ("w_mem_q", (128, 1024), 0), ("w_mem_k", (128, 1024), 0), ("w_mem_v", (128, 1024), 0),
       ("w_mem_o", (128, 1024), 0), ("w_up", (1024, 512), 1), ("w_down", (512, 1024), 0))
SMALL = ("ln_in_g", "ln_in_b", "conv_b", "ln1_g", "ln1_b", "ln2_g", "ln2_b", "ln3_g", "ln3_b",
         "ssd_norm_g", "q_norm_g", "kv_norm_g", "dt_bias", "a_log", "d_skip")
ALL_W = ("ln_in_g", "ln_in_b", "w_in", "conv_w", "conv_b", "dt_bias", "a_log", "d_skip", "ssd_norm_g", "q_norm_g", "w_q_up",
         "kv_norm_g", "w_kv_up", "w_mix_out", "ln1_g", "ln1_b", "w_mem_q", "w_mem_k", "w_mem_v", "w_mem_o", "ln2_g", "ln2_b",
         "w_up", "w_down", "ln3_g", "ln3_b")


SMALL_R = 16
LOSS_ROW = 15
_SMALL_ROWS = (("ln_in_g",), ("ln_in_b",), ("conv_b",), ("ln1_g",), ("ln1_b",), ("ln2_g",), ("ln2_b",), ("ln3_g",), ("ln3_b",),
               ("ssd_norm_g", "q_norm_g"), ("kv_norm_g", "dt_bias", "a_log", "d_skip"))
_SMALL_W = {"ssd_norm_g": 512, "q_norm_g": 384, "kv_norm_g": 256, "dt_bias": LANES, "a_log": LANES, "d_skip": LANES}


MESH = pl.DeviceIdType.MESH
ANY = pl.BlockSpec(memory_space=pl.ANY)
VM = pl.BlockSpec(memory_space=pltpu.VMEM)


def _coords():
    return lax.axis_index("x"), lax.axis_index("y"), lax.axis_index("c")


def _slot(px, py, pc):
    return 4 * px + 2 * py + pc


def _peer(k, x, y, c):
    dx, dy, dc = (k >> 2) & 1, (k >> 1) & 1, k & 1
    return (1 - x if dx else x, 1 - y if dy else y, 1 - c if dc else c)


def _adam(w, g, m, v):
    m = ADAM_B1 * m + (1.0 - ADAM_B1) * g
    v = ADAM_B2 * v + (1.0 - ADAM_B2) * (g * g)
    m_hat = m / (1.0 - ADAM_B1 ** ADAM_STEP)
    v_hat = v / (1.0 - ADAM_B2 ** ADAM_STEP)
    delta = -ADAM_LR * (m_hat / (jnp.sqrt(v_hat) + ADAM_EPS) + ADAM_WD * w)
    return delta, m, v


def _sum_slots(ref):
    tot = ref[0].astype(F32)
    for q in range(1, N_DEV):
        tot = tot + ref[q].astype(F32)
    return tot


def _row(v):
    return v.reshape(1, -1).astype(F32)


def kernel(x, mem, positions, ln_in_g, ln_in_b, w_in, conv_w, conv_b, dt_bias, a_log, d_skip, ssd_norm_g, q_norm_g, w_q_up, kv_norm_g, w_kv_up, w_mix_out, ln1_g, ln1_b, w_mem_q, w_mem_k, w_mem_v, w_mem_o, ln2_g, ln2_b, w_up, w_down, ln3_g, ln3_b, loss_target, m_ln_in_g, m_ln_in_b, m_w_in, m_conv_w, m_conv_b, m_dt_bias, m_a_log, m_d_skip, m_ssd_norm_g, m_q_norm_g, m_w_q_up, m_kv_norm_g, m_w_kv_up, m_w_mix_out, m_ln1_g, m_ln1_b, m_w_mem_q, m_w_mem_k, m_w_mem_v, m_w_mem_o, m_ln2_g, m_ln2_b, m_w_up, m_w_down, m_ln3_g, m_ln3_b, v_ln_in_g, v_ln_in_b, v_w_in, v_conv_w, v_conv_b, v_dt_bias, v_a_log, v_d_skip, v_ssd_norm_g, v_q_norm_g, v_w_q_up, v_kv_norm_g, v_w_kv_up, v_w_mix_out, v_ln1_g, v_ln1_b, v_w_mem_q, v_w_mem_k, v_w_mem_v, v_w_mem_o, v_ln2_g, v_ln2_b, v_w_up, v_w_down, v_ln3_g, v_ln3_b):
    a = dict(locals())
    W = {n: a[n] for n in ALL_W}
    M = {n: a["m_" + n] for n in ALL_W}
    V = {n: a["v_" + n] for n in ALL_W}
    return _step_overlapped(x, mem, positions, loss_target, W, M, V)


HBM = pl.BlockSpec(memory_space=pltpu.HBM)
SEM = pl.BlockSpec(memory_space=pltpu.SEMAPHORE)
EFFECT = pltpu.SideEffectType.DATAFLOW_SIDE_EFFECTING
SHARD_SHAPE = {n: s for n, s, _ in BIG}
SHARD_AXIS = {n: ax for n, _, ax in BIG}
GATHER_FIRST = ("w_in",)
GATHER_MID = ("conv_w", "w_q_up", "w_kv_up", "w_mix_out")
GATHER_LATE = ("w_mem_q", "w_mem_k", "w_mem_v", "w_mem_o", "w_up", "w_down")


def _my_slot():
    return _slot(*_coords())


def _group_copies(src_refs, land_refs, send_sems, recv_sems, slotted, landing_of_peer):
    x, y, c = _coords()
    my = _slot(x, y, c)
    cps = []
    for a, (s_ref, l_ref) in enumerate(zip(src_refs, land_refs)):
        for k in range(1, N_DEV):
            peer = _peer(k, x, y, c)
            cps.append(pltpu.make_async_remote_copy(
                src_ref=s_ref.at[_slot(*peer)] if slotted else s_ref,
                dst_ref=l_ref.at[_slot(*peer)] if landing_of_peer else l_ref.at[my],
                send_sem=send_sems.at[7 * a + k - 1], recv_sem=recv_sems.at[7 * a + k - 1],
                device_id=peer, device_id_type=MESH))
    return cps


def _send_start(srcs, lands, slotted, name):
    n = len(srcs)

    def body(*refs):
        for cp in _group_copies(refs[:n], refs[n:2 * n], refs[2 * n], refs[2 * n + 1], slotted, False):
            cp.start()
        refs[-1][...] = jnp.zeros(refs[-1].shape, F32)

    res = pl.pallas_call(
        body, name=name,
        out_shape=(pltpu.SemaphoreType.DMA((7 * n,)), pltpu.SemaphoreType.DMA((7 * n,)),
                   *[pltpu.HBM(a.shape, a.dtype) for a in srcs], *[pltpu.HBM(a.shape, a.dtype) for a in lands],
                   jax.ShapeDtypeStruct((8, LANES), F32)),
        in_specs=[HBM] * (2 * n), out_specs=(SEM, SEM, *[HBM] * (2 * n), VM),
        input_output_aliases={i: 2 + i for i in range(2 * n)},
        compiler_params=pltpu.CompilerParams(has_side_effects=EFFECT),
    )(*[pltpu.with_memory_space_constraint(a, pltpu.HBM) for a in list(srcs) + list(lands)])
    return (res[0], res[1], res[2:2 + n], res[2 + n:2 + 2 * n]), res[-1][:1, :1]


def _send_wait(started, after, slotted, name):
    send_sems, recv_sems, srcs, lands = started
    n = len(srcs)
    after = list(after) if isinstance(after, (list, tuple)) else [after]

    def body(*refs):
        for cp in _group_copies(refs[:n], refs[n:2 * n], refs[2 * n], refs[2 * n + 1], slotted, True):
            cp.wait_send()
            cp.wait_recv()

    res = pl.pallas_call(
        body, name=name, out_shape=tuple(pltpu.HBM(a.shape, a.dtype) for a in list(srcs) + list(lands)),
        in_specs=[HBM] * (2 * n) + [SEM, SEM] + [ANY] * len(after), out_specs=tuple([HBM] * (2 * n)),
        input_output_aliases={i: i for i in range(2 * n)},
        compiler_params=pltpu.CompilerParams(has_side_effects=EFFECT),
    )(*srcs, *lands, send_sems, recv_sems, *after)
    return res[n:]


def _landing(own, my):
    return lax.dynamic_update_slice(lax.empty((N_DEV,) + own.shape, own.dtype), own[None], (my,) + (0,) * own.ndim)


def _full_from_slots(name, slots):
    a, b = SHARD_SHAPE[name]
    return slots.reshape(N_DEV * a, b) if SHARD_AXIS[name] == 0 else slots.transpose(1, 0, 2).reshape(a, N_DEV * b)


def _slots_from_full(name, g):
    a, b = SHARD_SHAPE[name]
    return g.reshape(N_DEV, a, b) if SHARD_AXIS[name] == 0 else g.reshape(a, N_DEV, b).transpose(1, 0, 2)


def _reduce_adam(recv, w, m, v, name):
    _, a, b = recv.shape
    ta = a
    while ta * b * 4 * N_DEV > 4 * 1024 * 1024 and ta % 16 == 0:
        ta //= 2

    def body(r_ref, w_ref, m_ref, v_ref, g_ref, d_ref, nm_ref, nv_ref):
        g = _sum_slots(r_ref)
        g_ref[...] = g
        d_ref[...], nm_ref[...], nv_ref[...] = _adam(w_ref[...], g, m_ref[...], v_ref[...])

    row = pl.BlockSpec((ta, b), lambda i: (i, 0))
    return pl.pallas_call(
        body, name=name, grid=(a // ta,),
        in_specs=[pl.BlockSpec((N_DEV, ta, b), lambda i: (0, i, 0)), row, row, row], out_specs=[row] * 4,
        out_shape=[jax.ShapeDtypeStruct((a, b), F32)] * 4, compiler_params=_cparams(("parallel",)),
    )(recv, w, m, v)


def _step_overlapped(x, mem, positions, tgt, W, M, V):
    my = _my_slot()
    shard = {n: W[n][0] for n, _, _ in BIG}
    send = {n: (shard[n] if n == "conv_w" else shard[n].astype(BF16)) for n in shard}

    first_src, mid_src, late_src = ([send[n] for n in grp] for grp in (GATHER_FIRST, GATHER_MID, GATHER_LATE))
    first, tie = _send_start(first_src, [_landing(s, my) for s in first_src], False, "gather_first_start")
    my_then = my + tie[0, 0].astype(jnp.int32)
    mid, tie = _send_start(mid_src, [_landing(s, my_then) for s in mid_src], False, "gather_mid_start")
    my_then = my + tie[0, 0].astype(jnp.int32)
    late, tie = _send_start(late_src, [_landing(s, my_then) for s in late_src], False, "gather_late_start")

    P = {n: _row(W[n]) for n in SMALL}
    for n in ("dt_bias", "a_log", "d_skip"):
        P[n] = _head_lanes(P[n])
    P["ln_in_g"] = P["ln_in_g"] + tie

    def weights_at(stage, after):
        if stage == "first":
            lands = _send_wait(first, after, False, "gather_first_wait")
            return dict(w_in=_pad_w_in(_full_from_slots("w_in", lands[0])))
        if stage == "mid":
            lands = _send_wait(mid, after, False, "gather_mid_wait")
            full = {n: _full_from_slots(n, l) for n, l in zip(GATHER_MID, lands)}
            return dict(w_q_up=_pad_w_q(full["w_q_up"]), w_kv_up=_pad_w_kv(full["w_kv_up"]), w_mix_out=full["w_mix_out"],
                        conv_w=jnp.pad(full["conv_w"], ((0, 8 - SSD_CONV), (0, 0))))
        lands = _send_wait(late, after, False, "gather_late_wait")
        return {n: _full_from_slots(n, l) for n, l in zip(GATHER_LATE, lands)}

    started, res = [], {}

    def finish(i, after):
        names, st = started[i]
        lands = _send_wait(st, after, True, "scatter_wait_%d" % i)
        for n, recv in zip(names, lands):
            res[n] = _reduce_adam(recv, shard[n], M[n][0], V[n][0], "reduce_adam_" + n)
        return sum(res[n][0][:1, :1] for n in names) * 0.0

    def emit(names, G):
        srcs = []
        for n in names:
            g = G[n]
            if n == "w_in":
                g = _unpad_w_in(g)
            elif n == "w_q_up":
                g = _unpad_w_q(g)
            elif n == "w_kv_up":
                g = _unpad_w_kv(g)
            elif n == "conv_w":
                g = g[:SSD_CONV]
            srcs.append(g if g.ndim == 3 else _slots_from_full(n, g))
        lands = [_landing(lax.dynamic_index_in_dim(s, my, 0, keepdims=False), my) for s in srcs]
        st, tie = _send_start(srcs, lands, True, "scatter_start_%d" % len(started))
        started.append((names, st))
        if len(started) == 3:
            tie = tie + finish(0, srcs[0]) + finish(1, srcs[0])
        return tie

    loss, gx, G = _local_step(x[0], mem[0], positions[0], tgt[0], P, weights_at, emit)
    finish(2, gx)
    finish(3, gx)

    small, loss_tot = _allreduce_adam_vectors(
        {n: G[n] for n in SMALL}, loss, {n: _row(W[n]) for n in SMALL}, {n: _row(M[n]) for n in SMALL}, {n: _row(V[n]) for n in SMALL})

    outs = []
    for j in range(4):
        for n in ALL_W:
            outs.append((res[n][j] if n in res else small[j][n]).reshape(W[n].shape))
    return (loss_tot[0, 0], gx[None], *outs)


def _vector_places():
    places = {}
    for r, names in enumerate(_SMALL_ROWS):
        c = 0
        for n in names:
            w = _SMALL_W.get(n, PACK_W)
            places[n] = (r, c, w, SSD_HEADS if n in ("dt_bias", "a_log", "d_skip") else w)
            c += w
    return places


def _allreduce_adam_vectors(grads, loss, Ws, Ms, Vs):
    places = _vector_places()
    ns = len(SMALL)

    def body(*refs):
        g_in, loss_in = refs[:ns], refs[ns]
        w_in, m_in, v_in = refs[ns + 1:2 * ns + 1], refs[2 * ns + 1:3 * ns + 1], refs[3 * ns + 1:4 * ns + 1]
        o = 4 * ns + 1
        outs = [refs[o + j * ns:o + (j + 1) * ns] for j in range(4)]
        loss_out, stage, land, send_sems, recv_sems = refs[o + 4 * ns:]
        stage[...] = jnp.zeros(stage.shape, F32)
        for i, n in enumerate(SMALL):
            r, c, w, _ = places[n]
            stage[r:r + 1, c:c + w] = g_in[i][...]
        stage[LOSS_ROW:LOSS_ROW + 1, 0:LANES] = loss_in[...]
        x, y, c_ = _coords()
        my = _slot(x, y, c_)
        cps = []
        for k in range(1, N_DEV):
            peer = _peer(k, x, y, c_)
            cps.append(pltpu.make_async_remote_copy(
                src_ref=stage, dst_ref=land.at[my], send_sem=send_sems.at[k - 1], recv_sem=recv_sems.at[k - 1],
                device_id=peer, device_id_type=MESH))
        for cp in cps:
            cp.start()
        land[my] = stage[...]
        for cp in cps:
            cp.wait_recv()
        for cp in cps:
            cp.wait_send()
        tot = _sum_slots(land)
        loss_out[...] = tot[LOSS_ROW:LOSS_ROW + 1, 0:LANES]
        for i, n in enumerate(SMALL):
            r, c, _, wt = places[n]
            g = tot[r:r + 1, c:c + wt]
            outs[0][i][...] = g
            outs[1][i][...], outs[2][i][...], outs[3][i][...] = _adam(w_in[i][...], g, m_in[i][...], v_in[i][...])

    shapes = [jax.ShapeDtypeStruct((1, places[n][3]), F32) for n in SMALL]
    res = pl.pallas_call(
        body, name="allreduce_vectors", in_specs=[VM] * (4 * ns + 1), out_specs=[VM] * (4 * ns + 1),
        out_shape=shapes * 4 + [jax.ShapeDtypeStruct((1, LANES), F32)],
        scratch_shapes=[pltpu.VMEM((SMALL_R, PACK_W), F32), pltpu.VMEM((N_DEV, SMALL_R, PACK_W), F32),
                        pltpu.SemaphoreType.DMA((7,)), pltpu.SemaphoreType.DMA((7,))],
    )(*[grads[n] for n in SMALL], loss, *[Ws[n] for n in SMALL], *[Ms[n] for n in SMALL], *[Vs[n] for n in SMALL])
    return [dict(zip(SMALL, res[j * ns:(j + 1) * ns])) for j in range(4)], res[4 * ns]
```

```python
import math

import jax
import jax.numpy as jnp
from jax import lax
from jax.experimental import pallas as pl
from jax.experimental.pallas import tpu as pltpu

F32, BF16 = jnp.float32, jnp.bfloat16

N_DEV = 8
D_MODEL = 1024
SSD_HEADS, SSD_HEAD_DIM, SSD_INNER, SSD_STATE, SSD_CONV, SSD_CHUNK = 8, 64, 512, 128, 4, 128
SSD_XBC = 1024
MLA_HEADS, MLA_NOPE, MLA_ROPE, MLA_QK, MLA_V = 8, 64, 32, 96, 64
MLA_Q_RANK, MLA_KV_RANK = 384, 256
ROPE_THETA = 10000.0
MEM_HEADS, MEM_HEAD_DIM = 4, 256
D_FF = 4096
IN_WIDTH = 2216
LN_EPS, RMS_EPS = 1e-5, 1e-6
ALPHA = 2.0 ** 0.25
ADAM_LR, ADAM_B1, ADAM_B2, ADAM_EPS, ADAM_WD, ADAM_STEP = 0.001, 0.9, 0.999, 1e-08, 0.01, 10

LANES = 128
NEG = -1e30
VMEM_LIMIT = 56 * 1024 * 1024
TOK_K = 4096

PQ, PZ, PX, PKV, PDT, PKR, PW = 0, 512, 1024, 2048, 2304, 2432, 2560
KR_LANE = 64


def _cparams(sem):
    return pltpu.CompilerParams(dimension_semantics=sem, vmem_limit_bytes=VMEM_LIMIT)


def _sigmoid(x):
    return 1.0 / (1.0 + jnp.exp(-x))


def _mm(a, b, mode, name, *, tm=512, tn=None, tk=None, out_dtypes=(F32,), epi=None, extras=(), col_slots=False):
    if mode == "nn":
        (M, K), (K2, N) = a.shape, b.shape
    elif mode == "nt":
        (M, K), (N, K2) = a.shape, b.shape
    else:
        (K, M), (K2, N) = a.shape, b.shape
    assert K == K2, (name, a.shape, b.shape)
    tm, tn, tk = min(tm, M), min(tn or N, N), min(tk or K, K)
    assert M % tm == 0 and N % tn == 0 and K % tk == 0, (name, M, N, K, tm, tn, tk)
    gk = K // tk
    a_spec = pl.BlockSpec((tk, tm), lambda i, j, k: (k, i)) if mode == "tn" else pl.BlockSpec((tm, tk), lambda i, j, k: (i, k))
    b_spec = pl.BlockSpec((tn, tk), lambda i, j, k: (j, k)) if mode == "nt" else pl.BlockSpec((tk, tn), lambda i, j, k: (k, j))
    dims = {"nn": ((1,), (0,)), "nt": ((1,), (1,)), "tn": ((0,), (0,))}[mode]
    ex_specs = []
    for arr, kind in extras:
        if kind == "mn":
            ex_specs.append(pl.BlockSpec((tm, tn), lambda i, j, k: (i, j)))
        elif kind == "n":
            ex_specs.append(pl.BlockSpec((1, tn), lambda i, j, k: (0, j)))
        else:
            ex_specs.append(pl.BlockSpec((tm, arr.shape[1]), lambda i, j, k: (i, 0)))
    ne, no = len(extras), len(out_dtypes)

    def body(*refs):
        a_ref, b_ref = refs[0], refs[1]
        ex, outs = refs[2:2 + ne], refs[2 + ne:2 + ne + no]
        part = lax.dot_general(a_ref[...].astype(BF16), b_ref[...].astype(BF16), (dims, ((), ())),
                               preferred_element_type=F32)

        def finish(acc):
            res = epi(acc, *[e[...] for e in ex]) if epi is not None else (acc,)
            for o, r in zip(outs, res):
                o[...] = r.astype(o.dtype)

        if gk == 1:
            finish(part)
        else:
            acc_ref = refs[-1]
            k = pl.program_id(2)

            @pl.when(k == 0)
            def _():
                acc_ref[...] = part

            @pl.when(k > 0)
            def _():
                acc_ref[...] += part

            @pl.when(k == gk - 1)
            def _():
                finish(acc_ref[...])

    res = pl.pallas_call(
        body, name=name, grid=(M // tm, N // tn, gk),
        in_specs=[a_spec, b_spec] + ex_specs,
        out_specs=[pl.BlockSpec((None, tm, tn), lambda i, j, k: (j, i, 0)) if col_slots else pl.BlockSpec((tm, tn), lambda i, j, k: (i, j))
                   for _ in out_dtypes],
        out_shape=[jax.ShapeDtypeStruct((N // tn, M, tn) if col_slots else (M, N), dt) for dt in out_dtypes],
        scratch_shapes=[pltpu.VMEM((tm, tn), F32)] if gk > 1 else [],
        compiler_params=_cparams(("parallel", "parallel", "arbitrary")),
    )(a, b, *[e[0] for e in extras])
    return res[0] if no == 1 else res


def _ln_stats(r):
    mu = jnp.mean(r, axis=-1, keepdims=True)
    xc = r - mu
    var = jnp.mean(xc * xc, axis=-1, keepdims=True)
    rstd = lax.rsqrt(var + LN_EPS)
    return xc * rstd, rstd


def _ln_fwd(terms, g, b, name, tm=512):
    S, D = terms[0][0].shape
    coefs = [c for _, c in terms]
    nt = len(terms)

    def body(*refs):
        r = sum(c * t[...] for t, c in zip(refs[:nt], coefs))
        xh, _ = _ln_stats(r)
        h = xh * refs[nt][...] + refs[nt + 1][...]
        refs[nt + 2][...] = h
        refs[nt + 3][...] = h.astype(BF16)

    row = pl.BlockSpec((tm, D), lambda i: (i, 0))
    vec = pl.BlockSpec((1, D), lambda i: (0, 0))
    return pl.pallas_call(
        body, name=name, grid=(S // tm,), in_specs=[row] * nt + [vec, vec], out_specs=[row, row],
        out_shape=[jax.ShapeDtypeStruct((S, D), F32), jax.ShapeDtypeStruct((S, D), BF16)], compiler_params=_cparams(("parallel",)),
    )(*[t for t, _ in terms], g, b)


def _ln_bwd(terms, dterms, g, name, tm=512):
    S, D = terms[0][0].shape
    coefs, dcoefs = [c for _, c in terms], [c for _, c in dterms]
    nt, nd = len(terms), len(dterms)

    def body(*refs):
        i = pl.program_id(0)
        r = sum(c * t[...] for t, c in zip(refs[:nt], coefs))
        dh = sum(c * t[...].astype(F32) for t, c in zip(refs[nt:nt + nd], dcoefs))
        g_ref = refs[nt + nd]
        dr_ref, drb_ref, dg_ref, db_ref = refs[nt + nd + 1:]
        xh, rstd = _ln_stats(r)
        dxh = dh * g_ref[...]
        m1 = jnp.mean(dxh, axis=-1, keepdims=True)
        m2 = jnp.mean(dxh * xh, axis=-1, keepdims=True)
        dr = rstd * (dxh - m1 - xh * m2)
        dr_ref[...] = dr
        drb_ref[...] = dr.astype(BF16)
        pg = jnp.sum(dh * xh, axis=0, keepdims=True)
        pb = jnp.sum(dh, axis=0, keepdims=True)

        @pl.when(i == 0)
        def _():
            dg_ref[...] = pg
            db_ref[...] = pb

        @pl.when(i > 0)
        def _():
            dg_ref[...] += pg
            db_ref[...] += pb

    row = pl.BlockSpec((tm, D), lambda i: (i, 0))
    vec = pl.BlockSpec((1, D), lambda i: (0, 0))
    return pl.pallas_call(
        body, name=name, grid=(S // tm,), in_specs=[row] * (nt + nd) + [vec], out_specs=[row, row, vec, vec],
        out_shape=[jax.ShapeDtypeStruct((S, D), F32), jax.ShapeDtypeStruct((S, D), BF16), jax.ShapeDtypeStruct((1, D), F32),
                   jax.ShapeDtypeStruct((1, D), F32)],
        compiler_params=_cparams(("arbitrary",)),
    )(*[t for t, _ in terms], *[t for t, _ in dterms], g)


def _ln_loss_bwd(terms, g, b, tgt, name, tm=512):
    S, D = terms[0][0].shape
    coefs = [c for _, c in terms]
    nt = len(terms)

    def body(*refs):
        i = pl.program_id(0)
        r = sum(c * t[...] for t, c in zip(refs[:nt], coefs))
        g_ref, b_ref, t_ref = refs[nt:nt + 3]
        loss_ref, dr_ref, drb_ref, dg_ref, db_ref = refs[nt + 3:]
        xh, rstd = _ln_stats(r)
        h = xh * g_ref[...] + b_ref[...]
        diff = h - t_ref[...]
        pl_ = 0.5 * jnp.sum(jnp.mean(diff * diff, axis=-1, keepdims=True), axis=0, keepdims=True)
        dh = diff * (1.0 / D)
        dxh = dh * g_ref[...]
        m1 = jnp.mean(dxh, axis=-1, keepdims=True)
        m2 = jnp.mean(dxh * xh, axis=-1, keepdims=True)
        dr = rstd * (dxh - m1 - xh * m2)
        dr_ref[...] = dr
        drb_ref[...] = dr.astype(BF16)
        pg = jnp.sum(dh * xh, axis=0, keepdims=True)
        pb = jnp.sum(dh, axis=0, keepdims=True)
        plb = jnp.broadcast_to(pl_, (1, LANES))

        @pl.when(i == 0)
        def _():
            dg_ref[...] = pg
            db_ref[...] = pb
            loss_ref[...] = plb

        @pl.when(i > 0)
        def _():
            dg_ref[...] += pg
            db_ref[...] += pb
            loss_ref[...] += plb

    row = pl.BlockSpec((tm, D), lambda i: (i, 0))
    vec = pl.BlockSpec((1, D), lambda i: (0, 0))
    lvec = pl.BlockSpec((1, LANES), lambda i: (0, 0))
    return pl.pallas_call(
        body, name=name, grid=(S // tm,), in_specs=[row] * nt + [vec, vec, row], out_specs=[lvec, row, row, vec, vec],
        out_shape=[jax.ShapeDtypeStruct((1, LANES), F32), jax.ShapeDtypeStruct((S, D), F32), jax.ShapeDtypeStruct((S, D), BF16),
                   jax.ShapeDtypeStruct((1, D), F32), jax.ShapeDtypeStruct((1, D), F32)],
        compiler_params=_cparams(("arbitrary",)),
    )(*[t for t, _ in terms], g, b, tgt)


def _rope_tables(positions):
    half = MLA_ROPE // 2
    inv_freq = jnp.power(ROPE_THETA, -jnp.arange(half, dtype=F32) / half)
    ang = positions.astype(F32)[:, None] * inv_freq
    cos, sin = jnp.cos(ang), jnp.sin(ang)
    S = positions.shape[0]
    one, zero = jnp.ones((S, MLA_NOPE), F32), jnp.zeros((S, half), F32)
    pad = jnp.zeros((S, LANES - MLA_QK), F32)
    c = jnp.concatenate([one, cos, cos, pad], axis=1)
    s1 = jnp.concatenate([0 * one, -sin, zero, pad], axis=1)
    s2 = jnp.concatenate([0 * one, zero, sin, pad], axis=1)
    return c, s1, s2


def _rope_block(x, c, s1, s2):
    half = MLA_ROPE // 2
    return x * c + pltpu.roll(x, LANES - half, axis=1) * s1 + pltpu.roll(x, half, axis=1) * s2


def _rms_fwd(x, g):
    r = lax.rsqrt(jnp.mean(x * x, axis=-1, keepdims=True) + RMS_EPS)
    return x * r * g


def _rms_bwd(x, g, dy):
    r = lax.rsqrt(jnp.mean(x * x, axis=-1, keepdims=True) + RMS_EPS)
    xh = x * r
    dyh = dy * g
    dx = r * (dyh - xh * jnp.mean(dyh * xh, axis=-1, keepdims=True))
    return dx, jnp.sum(dy * xh, axis=0, keepdims=True)


def _mla_prep(proj, qg, kvg, tabs, tm=512):
    S = proj.shape[0]

    def body(ql_ref, kvl_ref, kr_ref, qg_ref, kvg_ref, c_ref, s1_ref, s2_ref, qn_ref, kvn_ref, kpe_ref):
        qn_ref[...] = _rms_fwd(ql_ref[...], qg_ref[...]).astype(BF16)
        kvn_ref[...] = _rms_fwd(kvl_ref[...], kvg_ref[...]).astype(BF16)
        kpe_ref[...] = _rope_block(kr_ref[...], c_ref[...], s1_ref[...], s2_ref[...])

    tab = pl.BlockSpec((tm, LANES), lambda i: (i, 0))
    return pl.pallas_call(
        body, name="mla_prep", grid=(S // tm,),
        in_specs=[pl.BlockSpec((tm, MLA_Q_RANK), lambda i: (i, PQ // MLA_Q_RANK)),
                  pl.BlockSpec((tm, MLA_KV_RANK), lambda i: (i, PKV // MLA_KV_RANK)),
                  pl.BlockSpec((tm, LANES), lambda i: (i, PKR // LANES)),
                  pl.BlockSpec((1, MLA_Q_RANK), lambda i: (0, 0)), pl.BlockSpec((1, MLA_KV_RANK), lambda i: (0, 0)),
                  tab, tab, tab],
        out_specs=[pl.BlockSpec((tm, MLA_Q_RANK), lambda i: (i, 0)), pl.BlockSpec((tm, MLA_KV_RANK), lambda i: (i, 0)), tab],
        out_shape=[jax.ShapeDtypeStruct((S, MLA_Q_RANK), BF16), jax.ShapeDtypeStruct((S, MLA_KV_RANK), BF16),
                   jax.ShapeDtypeStruct((S, LANES), F32)],
        compiler_params=_cparams(("parallel",)),
    )(proj, proj, proj, qg, kvg, *tabs)


def _mla_prep_bwd(proj, qg, kvg, tabs, dqn, dkvn, dk_all, tm=512):
    S = proj.shape[0]

    def body(ql_ref, kvl_ref, qg_ref, kvg_ref, c_ref, s1_ref, s2_ref, dqn_ref, dkvn_ref, dk_ref,
             dql_ref, dkvl_ref, dkr_ref, dqg_ref, dkvg_ref):
        i = pl.program_id(0)
        dql, pq = _rms_bwd(ql_ref[...], qg_ref[...], dqn_ref[...])
        dkvl, pkv = _rms_bwd(kvl_ref[...], kvg_ref[...], dkvn_ref[...])
        dql_ref[...] = dql
        dkvl_ref[...] = dkvl
        dk = dk_ref[...]
        dkpe = dk[:, 0:LANES]
        for h in range(1, MLA_HEADS):
            dkpe = dkpe + dk[:, h * LANES:(h + 1) * LANES]
        lane = lax.broadcasted_iota(jnp.int32, dkpe.shape, 1)
        dkpe = jnp.where((lane >= KR_LANE) & (lane < KR_LANE + MLA_ROPE), dkpe, 0.0)
        dkr_ref[...] = _rope_block(dkpe, c_ref[...], -s1_ref[...], -s2_ref[...])

        @pl.when(i == 0)
        def _():
            dqg_ref[...] = pq
            dkvg_ref[...] = pkv

        @pl.when(i > 0)
        def _():
            dqg_ref[...] += pq
            dkvg_ref[...] += pkv

    tab = pl.BlockSpec((tm, LANES), lambda i: (i, 0))
    qspec = pl.BlockSpec((tm, MLA_Q_RANK), lambda i: (i, 0))
    kvspec = pl.BlockSpec((tm, MLA_KV_RANK), lambda i: (i, 0))
    qv, kvv = pl.BlockSpec((1, MLA_Q_RANK), lambda i: (0, 0)), pl.BlockSpec((1, MLA_KV_RANK), lambda i: (0, 0))
    return pl.pallas_call(
        body, name="mla_prep_bwd", grid=(S // tm,),
        in_specs=[pl.BlockSpec((tm, MLA_Q_RANK), lambda i: (i, PQ // MLA_Q_RANK)),
                  pl.BlockSpec((tm, MLA_KV_RANK), lambda i: (i, PKV // MLA_KV_RANK)),
                  qv, kvv, tab, tab, tab, qspec, kvspec, pl.BlockSpec((tm, MLA_HEADS * LANES), lambda i: (i, 0))],
        out_specs=[qspec, kvspec, tab, qv, kvv],
        out_shape=[jax.ShapeDtypeStruct((S, MLA_Q_RANK), F32), jax.ShapeDtypeStruct((S, MLA_KV_RANK), F32),
                   jax.ShapeDtypeStruct((S, LANES), F32), jax.ShapeDtypeStruct((1, MLA_Q_RANK), F32),
                   jax.ShapeDtypeStruct((1, MLA_KV_RANK), F32)],
        compiler_params=_cparams(("arbitrary",)),
    )(proj, proj, qg, kvg, *tabs, dqn, dkvn, dk_all)


def _rope_bwd_all(dq_all, tabs, scale, tm=512):
    S, W = dq_all.shape

    def body(dq_ref, c_ref, s1_ref, s2_ref, o_ref):
        c, s1, s2 = scale * c_ref[...], -scale * s1_ref[...], -scale * s2_ref[...]
        for h in range(W // LANES):
            o_ref[:, h * LANES:(h + 1) * LANES] = _rope_block(dq_ref[:, h * LANES:(h + 1) * LANES], c, s1, s2).astype(BF16)

    tab = pl.BlockSpec((tm, LANES), lambda i: (i, 0))
    row = pl.BlockSpec((tm, W), lambda i: (i, 0))
    return pl.pallas_call(body, name="rope_bwd", grid=(S // tm,), in_specs=[row, tab, tab, tab], out_specs=row,
                          out_shape=jax.ShapeDtypeStruct((S, W), BF16), compiler_params=_cparams(("parallel",)))(dq_all, *tabs)


ATT_SCALE = MLA_QK ** -0.5
LN2 = math.log(2.0)
Q_PRESCALE = ATT_SCALE / LN2
DQ_POSTSCALE = ATT_SCALE / LN2
N_PAIR = MLA_HEADS // 2


def _causal_mask(qi, ki, tq, tk):
    row = qi * tq + lax.broadcasted_iota(jnp.int32, (tq, tk), 0)
    col = ki * tk + lax.broadcasted_iota(jnp.int32, (tq, tk), 1)
    return col <= row


def _lane_tile(x, n):
    return jnp.concatenate([x] * n, axis=1) if n > 1 else x


def _attn_fwd(q_all, kv_all, cat, tq=512, tk=2048):
    S = q_all.shape[0]
    tq, tk = min(tq, S), min(tk, S)
    nq, nk, nb, r = S // tq, S // tk, tk // LANES, tk // tq

    def body(q_ref, k_ref, v_ref, cat_in, o_ref, lse_ref, cat_ref, m_s, l_s, acc_s):
        qi, ki = pl.program_id(1), pl.program_id(2)
        last = lax.div(qi, r)

        @pl.when(ki == 0)
        def _():
            m_s[...] = jnp.full(m_s.shape, NEG, F32)
            l_s[...] = jnp.zeros(l_s.shape, F32)
            acc_s[...] = jnp.zeros(acc_s.shape, F32)

        def block(kc, mask):
            v = v_ref[0:kc, :]
            for hh in range(2):
                q = q_ref[:, hh * LANES:(hh + 1) * LANES]
                k = k_ref[0:kc, hh * LANES:(hh + 1) * LANES]
                s = lax.dot_general(q, k, (((1,), (1,)), ((), ())), preferred_element_type=F32)
                if mask is not None:
                    s = jnp.where(mask, s, NEG)
                m_prev = m_s[hh]
                m_new = jnp.maximum(m_prev, jnp.max(s, axis=-1, keepdims=True))
                p = jnp.exp2(s - _lane_tile(m_new, kc // LANES))
                alpha = jnp.exp2(m_prev - m_new)
                ps = p[:, :LANES]
                for j in range(1, kc // LANES):
                    ps = ps + p[:, j * LANES:(j + 1) * LANES]
                l_s[hh] = alpha * l_s[hh] + ps
                acc_s[hh] = alpha * acc_s[hh] + jnp.dot(p.astype(BF16), v, preferred_element_type=F32)
                m_s[hh] = m_new

        @pl.when(ki < last)
        def _():
            block(tk, None)

        for j in range(r):
            @pl.when((ki == last) & (lax.rem(qi, r) == j))
            def _(j=j):
                kc = (j + 1) * tq
                block(kc, lax.broadcasted_iota(jnp.int32, (tq, kc), 1) <= j * tq + lax.broadcasted_iota(jnp.int32, (tq, kc), 0))

        @pl.when(ki == last)
        def _():
            first = lax.broadcasted_iota(jnp.int32, (tq, LANES), 1) < MLA_V
            l0 = jnp.sum(l_s[0], axis=-1, keepdims=True)
            l1 = jnp.sum(l_s[1], axis=-1, keepdims=True)
            o = jnp.where(first, acc_s[0] / l0, acc_s[1] / l1)
            o_ref[...] = o
            cat_ref[...] = o.astype(BF16)
            lse_ref[:, :LANES] = m_s[0] + jnp.log2(l0)
            lse_ref[:, LANES:] = m_s[1] + jnp.log2(l1)

    return pl.pallas_call(
        body, name="mla_attn_fwd", grid=(N_PAIR, nq, nk),
        in_specs=[pl.BlockSpec((tq, 2 * LANES), lambda p, qi, ki: (qi, p)),
                  pl.BlockSpec((tk, 2 * LANES), lambda p, qi, ki: (jnp.minimum(ki, lax.div(qi, r)), p)),
                  pl.BlockSpec((tk, LANES), lambda p, qi, ki: (jnp.minimum(ki, lax.div(qi, r)), MLA_HEADS + p)), ANY],
        out_specs=[pl.BlockSpec((tq, LANES), lambda p, qi, ki: (qi, p)), pl.BlockSpec((tq, 2 * LANES), lambda p, qi, ki: (qi, p)),
                   pl.BlockSpec((tq, LANES), lambda p, qi, ki: (qi, N_PAIR + p))],
        out_shape=[jax.ShapeDtypeStruct((S, MLA_HEADS * MLA_V), F32), jax.ShapeDtypeStruct((S, MLA_HEADS * LANES), F32),
                   jax.ShapeDtypeStruct(cat.shape, cat.dtype)],
        input_output_aliases={3: 2},
        scratch_shapes=[pltpu.VMEM((2, tq, LANES), F32), pltpu.VMEM((2, tq, LANES), F32), pltpu.VMEM((2, tq, LANES), F32)],
        compiler_params=_cparams(("parallel", "parallel", "arbitrary")),
    )(q_all, kv_all, kv_all, cat)


def _attn_delta(do, o):
    prod = do * o
    tm = prod.shape[0]
    first = lax.broadcasted_iota(jnp.int32, (tm, LANES), 1) < MLA_V
    blocks = []
    for p in range(N_PAIR):
        pp = prod[:, p * LANES:(p + 1) * LANES]
        blocks.append(jnp.broadcast_to(jnp.sum(jnp.where(first, pp, 0.0), axis=-1, keepdims=True), (tm, LANES)))
        blocks.append(jnp.broadcast_to(jnp.sum(jnp.where(first, 0.0, pp), axis=-1, keepdims=True), (tm, LANES)))
    return jnp.concatenate(blocks, axis=1)


def _attn_bwd(q_all, kv_all, dcat, lse, delta, tq=1024, tk=1024):
    S = q_all.shape[0]
    tq, tk = min(tq, S), min(tk, S)
    nq, nk, nb = S // tq, S // tk, tk // LANES
    assert tq == tk

    def body(q_ref, k_ref, v_ref, do_ref, lse_ref, dl_ref, dq_ref, dk_ref, dv_ref, dk_s, dv_s):
        ki, qi = pl.program_id(1), pl.program_id(2)

        @pl.when((ki == 0) & (qi == 0))
        def _():
            dq_ref[...] = jnp.zeros(dq_ref.shape, F32)

        @pl.when(qi == 0)
        def _():
            dk_s[...] = jnp.zeros(dk_s.shape, F32)
            dv_s[...] = jnp.zeros(dv_s.shape, F32)

        def block(r0, nr, kc, mask):
            v, do = v_ref[0:kc, :], do_ref[r0:r0 + nr, :]
            first = lax.broadcasted_iota(jnp.int32, (nr, LANES), 1) < MLA_V
            firstk = lax.broadcasted_iota(jnp.int32, (kc, LANES), 1) < MLA_V
            do_b = do.astype(BF16)
            rows = pl.ds(pl.multiple_of(qi * tq + r0, LANES), nr)
            for hh in range(2):
                sl = slice(hh * LANES, (hh + 1) * LANES)
                q, k = q_ref[r0:r0 + nr, sl], k_ref[0:kc, sl]
                s = lax.dot_general(q, k, (((1,), (1,)), ((), ())), preferred_element_type=F32)
                if mask is not None:
                    s = jnp.where(mask, s, NEG)
                p = jnp.exp2(s - _lane_tile(lse_ref[r0:r0 + nr, sl], kc // LANES))
                do_h = jnp.where(first if hh == 0 else ~first, do, 0.0).astype(BF16)
                dp = lax.dot_general(do_h, v, (((1,), (1,)), ((), ())), preferred_element_type=F32)
                ds_b = (p * (dp - _lane_tile(dl_ref[r0:r0 + nr, sl], kc // LANES)) * LN2).astype(BF16)
                pv = lax.dot_general(p.astype(BF16), do_b, (((0,), (0,)), ((), ())), preferred_element_type=F32)
                dv_s[0:kc, :] += jnp.where(firstk if hh == 0 else ~firstk, pv, 0.0)
                dk_s[0:kc, sl] += lax.dot_general(ds_b, q, (((0,), (0,)), ((), ())), preferred_element_type=F32)
                dq_ref[rows, sl] += jnp.dot(ds_b, k, preferred_element_type=F32)

        @pl.when(qi > ki)
        def _():
            block(0, tq, tk, None)

        @pl.when(qi == ki)
        def _():
            h = tq // 2
            block(0, h, h, _causal_mask(0, 0, h, h))
            block(h, h, tk, lax.broadcasted_iota(jnp.int32, (h, tk), 1) <= h + lax.broadcasted_iota(jnp.int32, (h, tk), 0))

        @pl.when(qi == nq - 1)
        def _():
            dk_ref[...] = dk_s[...]
            dv_ref[...] = dv_s[...]

    wide = pl.BlockSpec((tq, 2 * LANES), lambda p, ki, qi: (jnp.maximum(qi, ki), p))
    return pl.pallas_call(
        body, name="mla_attn_bwd", grid=(N_PAIR, nk, nq),
        in_specs=[wide, pl.BlockSpec((tk, 2 * LANES), lambda p, ki, qi: (ki, p)),
                  pl.BlockSpec((tk, LANES), lambda p, ki, qi: (ki, MLA_HEADS + p)),
                  pl.BlockSpec((tq, LANES), lambda p, ki, qi: (jnp.maximum(qi, ki), N_PAIR + p)), wide, wide],
        out_specs=[pl.BlockSpec((S, 2 * LANES), lambda p, ki, qi: (0, p)),
                   pl.BlockSpec((tk, 2 * LANES), lambda p, ki, qi: (ki, p)), pl.BlockSpec((tk, LANES), lambda p, ki, qi: (ki, p))],
        out_shape=[jax.ShapeDtypeStruct((S, MLA_HEADS * LANES), F32), jax.ShapeDtypeStruct((S, MLA_HEADS * LANES), F32),
                   jax.ShapeDtypeStruct((S, MLA_HEADS * MLA_V), F32)],
        scratch_shapes=[pltpu.VMEM((tk, 2 * LANES), F32), pltpu.VMEM((tk, LANES), F32)],
        compiler_params=_cparams(("parallel", "arbitrary", "arbitrary")),
    )(q_all, kv_all, kv_all, dcat, lse, delta)


MEM_SCALE = MEM_HEAD_DIM ** -0.5


def _mem_probs(q, k):
    s = lax.dot_general(q, k, (((1,), (1,)), ((), ())), preferred_element_type=F32) * MEM_SCALE
    e = jnp.exp(s - jnp.max(s, axis=-1, keepdims=True))
    return e / jnp.sum(e, axis=-1, keepdims=True)


def _mem_attn_fwd(qm, km, vm, tq=512):
    S, W = qm.shape
    M = km.shape[0]

    def body(q_ref, k_ref, v_ref, o_ref):
        for h in range(MEM_HEADS):
            sl = slice(h * MEM_HEAD_DIM, (h + 1) * MEM_HEAD_DIM)
            p = _mem_probs(q_ref[:, sl], k_ref[:, sl])
            o_ref[:, sl] = jnp.dot(p.astype(BF16), v_ref[:, sl], preferred_element_type=F32).astype(BF16)

    row = pl.BlockSpec((tq, W), lambda i: (i, 0))
    full = pl.BlockSpec((M, W), lambda i: (0, 0))
    return pl.pallas_call(body, name="mem_attn_fwd", grid=(S // tq,), in_specs=[row, full, full], out_specs=row,
                          out_shape=jax.ShapeDtypeStruct((S, W), BF16), compiler_params=_cparams(("parallel",)))(qm, km, vm)


def _mem_attn_bwd(qm, km, vm, dom, tq=512):
    S, W = qm.shape
    M = km.shape[0]

    def body(q_ref, k_ref, v_ref, do_ref, dq_ref, dk_ref, dv_ref):
        i = pl.program_id(0)

        @pl.when(i == 0)
        def _():
            dk_ref[...] = jnp.zeros(dk_ref.shape, F32)
            dv_ref[...] = jnp.zeros(dv_ref.shape, F32)

        for h in range(MEM_HEADS):
            sl = slice(h * MEM_HEAD_DIM, (h + 1) * MEM_HEAD_DIM)
            q, k, v, do = q_ref[:, sl], k_ref[:, sl], v_ref[:, sl], do_ref[:, sl]
            p = _mem_probs(q, k)
            dv_ref[:, sl] += lax.dot_general(p.astype(BF16), do, (((0,), (0,)), ((), ())), preferred_element_type=F32)
            dp = lax.dot_general(do, v, (((1,), (1,)), ((), ())), preferred_element_type=F32)
            ds = (p * (dp - jnp.sum(dp * p, axis=-1, keepdims=True)) * MEM_SCALE).astype(BF16)
            dq_ref[:, sl] = jnp.dot(ds, k, preferred_element_type=F32).astype(BF16)
            dk_ref[:, sl] += lax.dot_general(ds, q, (((0,), (0,)), ((), ())), preferred_element_type=F32)

    row = pl.BlockSpec((tq, W), lambda i: (i, 0))
    full = pl.BlockSpec((M, W), lambda i: (0, 0))
    return pl.pallas_call(
        body, name="mem_attn_bwd", grid=(S // tq,), in_specs=[row, full, full, row], out_specs=[row, full, full],
        out_shape=[jax.ShapeDtypeStruct((S, W), BF16), jax.ShapeDtypeStruct((M, W), F32), jax.ShapeDtypeStruct((M, W), F32)],
        compiler_params=_cparams(("arbitrary",)),
    )(qm, km, vm, dom)


L = SSD_CHUNK
N_SPAIR = SSD_HEADS // 2
GRP_W = SSD_INNER // 2
XB0, XC0 = SSD_INNER, SSD_INNER + 2 * SSD_STATE


def _cumsum_rows(a, reverse=False):
    row = lax.broadcasted_iota(jnp.int32, a.shape, 0)
    x, sft = a, 1
    while sft < L:
        if reverse:
            x = x + jnp.where(row < L - sft, pltpu.roll(x, L - sft, axis=0), 0.0)
        else:
            x = x + jnp.where(row >= sft, pltpu.roll(x, sft, axis=0), 0.0)
        sft *= 2
    return x


def _shift_down(cur, prev, s):
    if s == 0:
        return cur
    row = lax.broadcasted_iota(jnp.int32, cur.shape, 0)
    return jnp.where(row < s, pltpu.roll(prev, s, axis=0), pltpu.roll(cur, s, axis=0))


def _shift_up(cur, nxt, s):
    if s == 0:
        return cur
    row = lax.broadcasted_iota(jnp.int32, cur.shape, 0)
    return jnp.where(row >= L - s, pltpu.roll(nxt, L - s, axis=0), pltpu.roll(cur, L - s, axis=0))


def _ssd_conv(u, prev, cw, cb):
    delayed = [u] + [_shift_down(u, prev, s) for s in range(1, SSD_CONV)]
    conv = cb + cw[SSD_CONV - 1:SSD_CONV, :] * u
    for s in range(1, SSD_CONV):
        conv = conv + cw[SSD_CONV - 1 - s:SSD_CONV - s, :] * delayed[s]
    return conv, delayed


def _pair_lanes(v, h0, first):
    return jnp.where(first, v[:, h0:h0 + 1], v[:, h0 + 1:h0 + 2])


def _ssd_common(u, prev, dt_raw, cw, cb, dtb, alog):
    conv, delayed = _ssd_conv(u, prev, cw, cb)
    sg = _sigmoid(conv)
    xa = conv * sg
    dpre = dt_raw + dtb
    dtv = jnp.maximum(dpre, 0.0) + jnp.log1p(jnp.exp(-jnp.abs(dpre)))
    a_row = -jnp.exp(alog)
    cs = _cumsum_rows(dtv * a_row)
    return conv, sg, xa, dpre, dtv, a_row, cs, delayed


def _ssd_pair_fwd(xa, dtv, cs, csT, G, Cg, Bg, Sp, dsk, pp, first, tri, rowfirst):
    h0 = 2 * pp
    x = xa[:, pp * LANES:(pp + 1) * LANES]
    xdt = x * _pair_lanes(dtv, h0, first)
    xdt_b = xdt.astype(BF16)
    Ms, yd = [], []
    for h in (h0, h0 + 1):
        lam = jnp.exp(jnp.where(tri, cs[:, h:h + 1] - csT[h:h + 1, :], NEG))
        M = G * lam
        Ms.append((M, lam))
        yd.append(jnp.dot(M.astype(BF16), xdt_b, preferred_element_type=F32))
    T = lax.dot_general(Cg, Sp.astype(BF16), (((1,), (1,)), ((), ())), preferred_element_type=F32)
    E = jnp.exp(_pair_lanes(cs, h0, first))
    yoff = E * T
    csl = cs[L - 1:L, :]
    Fd = jnp.exp(_pair_lanes(csl, h0, first) - _pair_lanes(cs, h0, first))
    el = jnp.exp(csl)
    el_rows = jnp.where(rowfirst, el[:, h0:h0 + 1], el[:, h0 + 1:h0 + 2])
    Sloc = lax.dot_general((xdt * Fd).astype(BF16), Bg, (((0,), (0,)), ((), ())), preferred_element_type=F32)
    S_new = el_rows * Sp + Sloc
    y = jnp.where(first, yd[0], yd[1]) + yoff + x * _pair_lanes(dsk, h0, first[:1])
    return y, S_new, (x, xdt, xdt_b, Ms, E, yoff, Fd, el, el_rows)


def _ssd_masks():
    lane = lax.broadcasted_iota(jnp.int32, (L, LANES), 1)
    row = lax.broadcasted_iota(jnp.int32, (L, LANES), 0)
    return lane, row, lane < SSD_HEAD_DIM, row >= lane, row[:, :1] < SSD_HEAD_DIM


def _ssd_specs(nc, rev):
    def cidx(i):
        return nc - 1 - i if rev else i
    z = pl.BlockSpec((L, SSD_INNER), lambda i: (cidx(i), PZ // SSD_INNER))
    u = pl.BlockSpec((L, SSD_XBC), lambda i: (cidx(i), PX // SSD_XBC))
    dt = pl.BlockSpec((L, LANES), lambda i: (cidx(i), PDT // LANES))
    return cidx, z, u, dt


def _vec(w):
    return pl.BlockSpec((1, w), lambda i: (0, 0))


def _ssd_fwd(proj, cw, cb, dtb, alog, dsk, ng):
    S = proj.shape[0]
    nc = S // L

    def body(z_ref, u_ref, dt_ref, cw_ref, cb_ref, dtb_ref, alog_ref, dsk_ref, ng_ref, y_ref, st_ref, prev_s, state_s):
        c = pl.program_id(0)

        @pl.when(c == 0)
        def _():
            prev_s[...] = jnp.zeros(prev_s.shape, F32)
            state_s[...] = jnp.zeros(state_s.shape, F32)

        u = u_ref[...]
        _, _, xa, _, dtv, _, cs, _ = _ssd_common(u, prev_s[...], dt_ref[...], cw_ref[...], cb_ref[...], dtb_ref[...], alog_ref[...])
        prev_s[...] = u
        csT = cs.T
        _, _, first, tri, rowfirst = _ssd_masks()
        dsk_v = dsk_ref[...]
        ys = []
        for g in range(2):
            Bg = xa[:, XB0 + g * SSD_STATE:XB0 + (g + 1) * SSD_STATE].astype(BF16)
            Cg = xa[:, XC0 + g * SSD_STATE:XC0 + (g + 1) * SSD_STATE].astype(BF16)
            G = lax.dot_general(Cg, Bg, (((1,), (1,)), ((), ())), preferred_element_type=F32)
            for pp in (2 * g, 2 * g + 1):
                Sp = state_s[pp]
                st_ref[pp * LANES:(pp + 1) * LANES, :] = Sp
                y, S_new, _ = _ssd_pair_fwd(xa, dtv, cs, csT, G, Cg, Bg, Sp, dsk_v, pp, first, tri, rowfirst)
                state_s[pp] = S_new
                ys.append(y)
        z = z_ref[...]
        for g in range(2):
            yg = jnp.concatenate([ys[2 * g], ys[2 * g + 1]], axis=1)
            zg = z[:, g * GRP_W:(g + 1) * GRP_W]
            gated = yg * (zg * _sigmoid(zg))
            r = lax.rsqrt(jnp.mean(gated * gated, axis=-1, keepdims=True) + RMS_EPS)
            y_ref[:, g * GRP_W:(g + 1) * GRP_W] = (gated * r * ng_ref[:, g * GRP_W:(g + 1) * GRP_W]).astype(BF16)

    _, zs, us, dts = _ssd_specs(nc, False)
    return pl.pallas_call(
        body, name="ssd_fwd", grid=(nc,),
        in_specs=[zs, us, dts, pl.BlockSpec((8, SSD_XBC), lambda i: (0, 0)), _vec(SSD_XBC), _vec(LANES), _vec(LANES), _vec(LANES),
                  _vec(SSD_INNER)],
        out_specs=[pl.BlockSpec((L, SSD_INNER), lambda i: (i, 0)), pl.BlockSpec((N_SPAIR * LANES, SSD_STATE), lambda i: (i, 0))],
        out_shape=[jax.ShapeDtypeStruct((S, 2 * SSD_INNER), BF16), jax.ShapeDtypeStruct((nc * N_SPAIR * LANES, SSD_STATE), F32)],
        scratch_shapes=[pltpu.VMEM((L, SSD_XBC), F32), pltpu.VMEM((N_SPAIR, LANES, SSD_STATE), F32)],
        compiler_params=_cparams(("arbitrary",)),
    )(proj, proj, proj, cw, cb, dtb, alog, dsk, ng)


def _ssd_bwd(proj, states, dy, cw, cb, dtb, alog, dsk, ng):
    S = proj.shape[0]
    nc = S // L

    def body(z_ref, u_ref, up_ref, dt_ref, st_ref, dy_ref, cw_ref, cb_ref, dtb_ref, alog_ref, dsk_ref, ng_ref,
             dz_ref, du_ref, ddt_ref, dcw_ref, dcb_ref, ddtb_ref, dalog_ref, ddsk_ref, dng_ref,
             dS_s, dconv_s, dD_s):
        i = pl.program_id(0)
        c = nc - 1 - i

        @pl.when(i == 0)
        def _():
            dS_s[...] = jnp.zeros(dS_s.shape, F32)
            dconv_s[...] = jnp.zeros(dconv_s.shape, F32)
            dD_s[...] = jnp.zeros(dD_s.shape, F32)
            for r in (dcw_ref, dcb_ref, ddtb_ref, dalog_ref, ddsk_ref, dng_ref):
                r[...] = jnp.zeros(r.shape, F32)

        u = u_ref[...]
        prev = jnp.where(c > 0, up_ref[...], 0.0)
        cw_v = cw_ref[...]
        conv, sg, xa, dpre, dtv, a_row, cs, delayed = _ssd_common(u, prev, dt_ref[...], cw_v, cb_ref[...], dtb_ref[...], alog_ref[...])
        csT = cs.T
        lane, row, first, tri, rowfirst = _ssd_masks()
        dsk_v = dsk_ref[...]

        fw = []
        Gs, Bs, Cs = [], [], []
        for g in range(2):
            Bg = xa[:, XB0 + g * SSD_STATE:XB0 + (g + 1) * SSD_STATE].astype(BF16)
            Cg = xa[:, XC0 + g * SSD_STATE:XC0 + (g + 1) * SSD_STATE].astype(BF16)
            G = lax.dot_general(Cg, Bg, (((1,), (1,)), ((), ())), preferred_element_type=F32)
            Gs.append(G), Bs.append(Bg), Cs.append(Cg)
            for pp in (2 * g, 2 * g + 1):
                Sp = st_ref[pp * LANES:(pp + 1) * LANES, :]
                y, _, keep = _ssd_pair_fwd(xa, dtv, cs, csT, G, Cg, Bg, Sp, dsk_v, pp, first, tri, rowfirst)
                fw.append((y, Sp, keep))

        z = z_ref[...]
        dys = []
        for g in range(2):
            sl = slice(g * GRP_W, (g + 1) * GRP_W)
            yg = jnp.concatenate([fw[2 * g][0], fw[2 * g + 1][0]], axis=1)
            zg = z[:, sl]
            sz = _sigmoid(zg)
            silu_z = zg * sz
            gated = yg * silu_z
            r = lax.rsqrt(jnp.mean(gated * gated, axis=-1, keepdims=True) + RMS_EPS)
            nh = gated * r
            dout = dy_ref[:, sl]
            dng_ref[:, sl] += jnp.sum(dout * nh, axis=0, keepdims=True)
            dnh = dout * ng_ref[:, sl]
            dgated = r * (dnh - nh * jnp.mean(dnh * nh, axis=-1, keepdims=True))
            dz_ref[:, sl] = dgated * yg * (sz * (1.0 + zg * (1.0 - sz)))
            dyg = dgated * silu_z
            dys.append(dyg[:, :LANES]), dys.append(dyg[:, LANES:])

        dcs_col = [0.0] * SSD_HEADS
        dcs_row = [None] * SSD_HEADS
        ddt_col = [None] * SSD_HEADS
        dxs = []
        dB, dC = [None, None], [None, None]
        last = row == L - 1
        for g in range(2):
            Bg, Cg, G = Bs[g], Cs[g], Gs[g]
            dG = jnp.zeros((L, L), F32)
            dBg = jnp.zeros((L, SSD_STATE), F32)
            dCg = jnp.zeros((L, SSD_STATE), F32)
            for pp in (2 * g, 2 * g + 1):
                h0 = 2 * pp
                y, Sp, (x, xdt, xdt_b, Ms, E, yoff, Fd, el, el_rows) = fw[pp]
                dY = dys[pp]
                dS = dS_s[pp]
                dS_b, Sp_b = dS.astype(BF16), Sp.astype(BF16)
                dD_s[:, pp * LANES:(pp + 1) * LANES] += jnp.sum(dY * x, axis=0, keepdims=True)
                dx = dY * _pair_lanes(dsk_v, h0, first[:1])
                dxdt = jnp.zeros((L, LANES), F32)
                dY_b = dY.astype(BF16)
                for hh, h in enumerate((h0, h0 + 1)):
                    hm = first if hh == 0 else ~first
                    M, lam = Ms[hh]
                    dYh = jnp.where(hm, dY, 0.0).astype(BF16)
                    dM = lax.dot_general(dYh, xdt_b, (((1,), (1,)), ((), ())), preferred_element_type=F32)
                    W = dM * M
                    dcs_col[h] = dcs_col[h] + jnp.sum(W, axis=-1, keepdims=True)
                    dcs_row[h] = jnp.sum(W, axis=0, keepdims=True)
                    dG = dG + dM * lam
                    mt = lax.dot_general(M.astype(BF16), dY_b, (((0,), (0,)), ((), ())), preferred_element_type=F32)
                    dxdt = dxdt + jnp.where(hm, mt, 0.0)
                dT = (E * dY).astype(BF16)
                dCg = dCg + jnp.dot(dT, Sp_b, preferred_element_type=F32)
                dS_in = lax.dot_general(dT, Cg, (((0,), (0,)), ((), ())), preferred_element_type=F32) + el_rows * dS
                q1 = dY * yoff
                dZ = lax.dot_general(Bg, dS_b, (((1,), (1,)), ((), ())), preferred_element_type=F32)
                dBg = dBg + jnp.dot((xdt * Fd).astype(BF16), dS_b, preferred_element_type=F32)
                dxdt = dxdt + dZ * Fd
                q2 = dZ * xdt * Fd
                dSS = dS * Sp
                for hh, h in enumerate((h0, h0 + 1)):
                    hm = first if hh == 0 else ~first
                    rs1 = jnp.sum(jnp.where(hm, q1, 0.0), axis=-1, keepdims=True)
                    rs2 = jnp.sum(jnp.where(hm, q2, 0.0), axis=-1, keepdims=True)
                    rmask = rowfirst if hh == 0 else ~rowfirst
                    d_el = jnp.sum(jnp.sum(jnp.where(rmask, dSS, 0.0), axis=-1, keepdims=True), axis=0, keepdims=True)
                    tail = jnp.sum(rs2, axis=0, keepdims=True) + d_el * el[:, h:h + 1]
                    dcs_col[h] = dcs_col[h] + (rs1 - rs2 + jnp.where(last[:, :1], tail, 0.0))
                    ddt_col[h] = jnp.sum(jnp.where(hm, dxdt * x, 0.0), axis=-1, keepdims=True)
                dS_s[pp] = dS_in
                dxs.append(dx + dxdt * _pair_lanes(dtv, h0, first))
            dG_b = dG.astype(BF16)
            dC[g] = dCg + jnp.dot(dG_b, Bg, preferred_element_type=F32)
            dB[g] = dBg + lax.dot_general(dG_b, Cg, (((0,), (0,)), ((), ())), preferred_element_type=F32)

        dcs_c, dcs_r, ddt_c = (jnp.zeros((L, LANES), F32) for _ in range(3))
        for h in range(SSD_HEADS):
            dcs_c = dcs_c + jnp.where(lane == h, dcs_col[h], 0.0)
            dcs_r = dcs_r + jnp.where(row == h, dcs_row[h], 0.0)
            ddt_c = ddt_c + jnp.where(lane == h, ddt_col[h], 0.0)
        dcs = dcs_c - dcs_r.T
        da = _cumsum_rows(dcs, reverse=True)
        ddt_c = ddt_c + da * a_row
        dalog_ref[...] += jnp.sum(da * dtv, axis=0, keepdims=True) * a_row
        ddt_raw = ddt_c * _sigmoid(dpre)
        ddt_ref[...] = ddt_raw
        ddtb_ref[...] += jnp.sum(ddt_raw, axis=0, keepdims=True)

        dxa = jnp.concatenate(dxs + dB + dC, axis=1)
        dconv = dxa * (sg * (1.0 + conv * (1.0 - sg)))
        dcb_ref[...] += jnp.sum(dconv, axis=0, keepdims=True)
        nxt = dconv_s[...]
        du = cw_v[SSD_CONV - 1:SSD_CONV, :] * dconv
        dcw_ref[SSD_CONV - 1:SSD_CONV, :] += jnp.sum(dconv * u, axis=0, keepdims=True)
        for s in range(1, SSD_CONV):
            k = SSD_CONV - 1 - s
            du = du + cw_v[k:k + 1, :] * _shift_up(dconv, nxt, s)
            dcw_ref[k:k + 1, :] += jnp.sum(dconv * delayed[s], axis=0, keepdims=True)
        du_ref[...] = du
        dconv_s[...] = dconv

        @pl.when(i == nc - 1)
        def _():
            acc = dD_s[...]
            lane1 = lax.broadcasted_iota(jnp.int32, (1, LANES), 1)
            lanew = lax.broadcasted_iota(jnp.int32, acc.shape, 1)
            out = jnp.zeros((1, LANES), F32)
            for h in range(SSD_HEADS):
                tot = jnp.sum(jnp.where((lanew >= h * SSD_HEAD_DIM) & (lanew < (h + 1) * SSD_HEAD_DIM), acc, 0.0),
                              axis=-1, keepdims=True)
                out = out + jnp.where(lane1 == h, tot, 0.0)
            ddsk_ref[...] = out

    cidx, zs, us, dts = _ssd_specs(nc, True)
    ups = pl.BlockSpec((L, SSD_XBC), lambda i: (jnp.maximum(cidx(i) - 1, 0), PX // SSD_XBC))
    rowc = lambda w: pl.BlockSpec((L, w), lambda i: (cidx(i), 0))
    return pl.pallas_call(
        body, name="ssd_bwd", grid=(nc,),
        in_specs=[zs, us, ups, dts, pl.BlockSpec((N_SPAIR * LANES, SSD_STATE), lambda i: (cidx(i), 0)), rowc(SSD_INNER),
                  pl.BlockSpec((8, SSD_XBC), lambda i: (0, 0)), _vec(SSD_XBC), _vec(LANES), _vec(LANES), _vec(LANES), _vec(SSD_INNER)],
        out_specs=[rowc(SSD_INNER), rowc(SSD_XBC), rowc(LANES), pl.BlockSpec((8, SSD_XBC), lambda i: (0, 0)), _vec(SSD_XBC),
                   _vec(LANES), _vec(LANES), _vec(LANES), _vec(SSD_INNER)],
        out_shape=[jax.ShapeDtypeStruct((S, SSD_INNER), F32), jax.ShapeDtypeStruct((S, SSD_XBC), F32),
                   jax.ShapeDtypeStruct((S, LANES), F32), jax.ShapeDtypeStruct((8, SSD_XBC), F32),
                   jax.ShapeDtypeStruct((1, SSD_XBC), F32), jax.ShapeDtypeStruct((1, LANES), F32),
                   jax.ShapeDtypeStruct((1, LANES), F32), jax.ShapeDtypeStruct((1, LANES), F32),
                   jax.ShapeDtypeStruct((1, SSD_INNER), F32)],
        scratch_shapes=[pltpu.VMEM((N_SPAIR, LANES, SSD_STATE), F32), pltpu.VMEM((L, SSD_XBC), F32),
                        pltpu.VMEM((1, SSD_INNER), F32)],
        compiler_params=_cparams(("arbitrary",)),
    )(proj, proj, proj, proj, states, dy, cw, cb, dtb, alog, dsk, ng)


_IN_SEGS = ((PZ, 0, 512), (PX, 512, 1024), (PDT, 1536, 8), (PQ, 1544, 384), (PKV, 1928, 256), (PKR + KR_LANE, 2184, 32))


def _pad_w_in(w):
    parts, at = [], 0
    for dst, src, n in sorted(_IN_SEGS):
        parts += [jnp.zeros((w.shape[0], dst - at), w.dtype), w[:, src:src + n]]
        at = dst + n
    return jnp.concatenate(parts + [jnp.zeros((w.shape[0], PW - at), w.dtype)], axis=1)


def _unpad_w_in(wp):
    segs = sorted(_IN_SEGS, key=lambda t: t[1])
    return jnp.concatenate([wp[:, dst:dst + n] for dst, src, n in segs], axis=1)


def _pad_w_q(w):
    return jnp.pad(w.reshape(MLA_Q_RANK, MLA_HEADS, MLA_QK), ((0, 0), (0, 0), (0, LANES - MLA_QK))).reshape(MLA_Q_RANK, MLA_HEADS * LANES)


def _unpad_w_q(wp):
    return wp.reshape(MLA_Q_RANK, MLA_HEADS, LANES)[:, :, :MLA_QK].reshape(MLA_Q_RANK, MLA_HEADS * MLA_QK)


def _pad_w_kv(w):
    w3 = w.reshape(MLA_KV_RANK, MLA_HEADS, MLA_NOPE + MLA_V)
    k = jnp.pad(w3[:, :, :MLA_NOPE], ((0, 0), (0, 0), (0, LANES - MLA_NOPE))).reshape(MLA_KV_RANK, MLA_HEADS * LANES)
    return jnp.concatenate([k, w3[:, :, MLA_NOPE:].reshape(MLA_KV_RANK, MLA_HEADS * MLA_V)], axis=1)


def _unpad_w_kv(wp):
    k = wp[:, :MLA_HEADS * LANES].reshape(MLA_KV_RANK, MLA_HEADS, LANES)[:, :, :MLA_NOPE]
    v = wp[:, MLA_HEADS * LANES:].reshape(MLA_KV_RANK, MLA_HEADS, MLA_V)
    return jnp.concatenate([k, v], axis=2).reshape(MLA_KV_RANK, MLA_HEADS * (MLA_NOPE + MLA_V))


def _head_lanes(v):
    return jnp.pad(v, ((0, 0), (0, LANES - v.shape[1])))


def _local_step(x, mem, positions, tgt, P, weights_at=None, emit=None):
    tabs = _rope_tables(positions)
    G = {}
    emit = emit or (lambda names, grads: 0.0)

    h0, h0b = _ln_fwd([(x, 1.0)], P["ln_in_g"], P["ln_in_b"], "ln_in")
    if weights_at is not None:
        P = {**P, **weights_at("first", (h0b,) + tuple(tabs))}
    proj = _mm(h0b, P["w_in"], "nn", "proj_in", tm=1024, tn=640)
    if weights_at is not None:
        P = {**P, **weights_at("mid", proj)}
    cat, states = _ssd_fwd(proj, P["conv_w"], P["conv_b"], P["dt_bias"], P["a_log"], P["d_skip"], P["ssd_norm_g"])
    qn, kvn, kpe = _mla_prep(proj, P["q_norm_g"], P["kv_norm_g"], tabs)

    def q_epi(acc, c, s1, s2):
        return (jnp.concatenate([_rope_block(acc[:, h * LANES:(h + 1) * LANES], c, s1, s2) for h in range(MLA_HEADS)], axis=1)
                * Q_PRESCALE,)

    q_all = _mm(qn, P["w_q_up"], "nn", "q_up", out_dtypes=(BF16,), epi=q_epi, extras=[(t, "m") for t in tabs])

    def kv_epi(acc, kp):
        kb = [acc[:, h * LANES:(h + 1) * LANES] + kp for h in range(MLA_HEADS)]
        return (jnp.concatenate(kb + [acc[:, MLA_HEADS * LANES:]], axis=1),)

    kv_all = _mm(kvn, P["w_kv_up"], "nn", "kv_up", out_dtypes=(BF16,), epi=kv_epi, extras=[(kpe, "m")])
    o_att, lse, cat = _attn_fwd(q_all, kv_all, cat)
    def resid_epi(acc, h):
        return (ALPHA * h + acc,)

    r1 = _mm(cat, P["w_mix_out"], "nn", "mix_out", tm=1024, epi=resid_epi, extras=[(h0, "mn")])
    h1, h1b = _ln_fwd([(r1, 1.0)], P["ln1_g"], P["ln1_b"], "ln1")
    if weights_at is not None:
        P = {**P, **weights_at("late", h1b)}
    qm = _mm(h1b, P["w_mem_q"], "nn", "mem_q", tm=1024, out_dtypes=(BF16,))
    km = _mm(mem, P["w_mem_k"], "nn", "mem_k", out_dtypes=(BF16,))
    vm = _mm(mem, P["w_mem_v"], "nn", "mem_v", out_dtypes=(BF16,))
    om = _mem_attn_fwd(qm, km, vm)
    r2 = _mm(om, P["w_mem_o"], "nn", "mem_o", tm=1024, epi=resid_epi, extras=[(h1, "mn")])
    h2, h2b = _ln_fwd([(r2, 1.0)], P["ln2_g"], P["ln2_b"], "ln2")

    def up_epi(acc):
        r = jnp.maximum(acc, 0.0)
        return r * r, 2.0 * r

    act, dact = _mm(h2b, P["w_up"], "nn", "mlp_up", tm=1024, tn=1024, out_dtypes=(BF16, BF16), epi=up_epi)
    r3 = _mm(act, P["w_down"], "nn", "mlp_down", tm=1024, tk=D_FF, epi=resid_epi, extras=[(h2, "mn")])

    loss, dr3, dr3b, G["ln3_g"], G["ln3_b"] = _ln_loss_bwd([(r3, 1.0)], P["ln3_g"], P["ln3_b"], tgt, "ln3_loss")

    def dact_epi(acc, g):
        return (acc.astype(BF16) * g,)

    du = _mm(dr3b, P["w_down"], "nt", "mlp_down_dx", tm=1024, tn=1024, out_dtypes=(BF16,), epi=dact_epi, extras=[(dact, "mn")])
    G["w_down"] = _mm(act, dr3b, "tn", "mlp_down_dw", tm=1024, tk=TOK_K, out_dtypes=(BF16,))
    G["w_up"] = _mm(h2b, du, "tn", "mlp_up_dw", tm=1024, tn=D_FF // N_DEV, tk=TOK_K, out_dtypes=(BF16,), col_slots=True)
    tie = emit(("w_down", "w_up"), G)
    dh2 = _mm(du, P["w_up"], "nt", "mlp_up_dx", tm=1024, tk=D_FF, epi=resid_epi, extras=[(dr3, "mn")])
    dr2, dr2b, G["ln2_g"], G["ln2_b"] = _ln_bwd([(r2, 1.0)], [(dh2, 1.0)], P["ln2_g"] + tie, "ln2_bwd")

    dom = _mm(dr2b, P["w_mem_o"], "nt", "mem_o_dx", tm=1024, out_dtypes=(BF16,))
    G["w_mem_o"] = _mm(om, dr2b, "tn", "mem_o_dw", tm=1024, tk=TOK_K, out_dtypes=(BF16,))
    dqm, dkm, dvm = _mem_attn_bwd(qm, km, vm, dom)
    G["w_mem_q"] = _mm(h1b, dqm, "tn", "mem_q_dw", tm=1024, tk=TOK_K, out_dtypes=(BF16,))
    G["w_mem_k"] = _mm(mem, dkm, "tn", "mem_k_dw", tm=1024, out_dtypes=(BF16,))
    G["w_mem_v"] = _mm(mem, dvm, "tn", "mem_v_dw", tm=1024, out_dtypes=(BF16,))
    tie = emit(("w_mem_o", "w_mem_q", "w_mem_k", "w_mem_v"), G)
    dh1 = _mm(dqm, P["w_mem_q"], "nt", "mem_q_dx", tm=1024, epi=resid_epi, extras=[(dr2, "mn")])
    dr1, dr1b, G["ln1_g"], G["ln1_b"] = _ln_bwd([(r1, 1.0)], [(dh1, 1.0)], P["ln1_g"] + tie, "ln1_bwd")

    def dcat_epi(acc, o):
        return acc, _attn_delta(acc[:, SSD_INNER:], o)

    dcat, delta = _mm(dr1b, P["w_mix_out"], "nt", "mix_out_dx", tm=512, out_dtypes=(F32, F32), epi=dcat_epi, extras=[(o_att, "m")])
    G["w_mix_out"] = _mm(cat, dr1b, "tn", "mix_out_dw", tm=1024, tk=TOK_K, out_dtypes=(BF16,))
    dq_all, dk_all, dv_all = _attn_bwd(q_all, kv_all, dcat, lse, delta)
    dq_pre = _rope_bwd_all(dq_all, tabs, DQ_POSTSCALE)
    G["w_q_up"] = _mm(qn, dq_pre, "tn", "q_up_dw", tk=TOK_K, out_dtypes=(BF16,))
    dqn = _mm(dq_pre, P["w_q_up"], "nt", "q_up_dx", tm=1024)
    dkv_all = jnp.concatenate([dk_all, dv_all], axis=1).astype(BF16)
    G["w_kv_up"] = _mm(kvn, dkv_all, "tn", "kv_up_dw", tk=TOK_K, out_dtypes=(BF16,))
    dkvn = _mm(dkv_all, P["w_kv_up"], "nt", "kv_up_dx", tm=1024)
    dql, dkvl, dkr, G["q_norm_g"], G["kv_norm_g"] = _mla_prep_bwd(proj, P["q_norm_g"], P["kv_norm_g"], tabs, dqn, dkvn, dk_all)
    (dz, dxbc, ddt, G["conv_w"], G["conv_b"], G["dt_bias"], G["a_log"], G["d_skip"], G["ssd_norm_g"]) = _ssd_bwd(
        proj, states, dcat, P["conv_w"], P["conv_b"], P["dt_bias"], P["a_log"], P["d_skip"], P["ssd_norm_g"])
    tie = emit(("w_mix_out", "w_q_up", "w_kv_up", "conv_w"), G)
    S = x.shape[0]
    dproj = jnp.concatenate([dql, jnp.zeros((S, PZ - MLA_Q_RANK), F32) + tie, dz, dxbc, dkvl, ddt, dkr], axis=1).astype(BF16)
    G["w_in"] = _mm(h0b, dproj, "tn", "proj_in_dw", tm=1024, tn=640, tk=TOK_K, out_dtypes=(BF16,))
    tie = emit(("w_in",), G)
    dh0 = _mm(dproj, P["w_in"], "nt", "proj_in_dx", tm=1024, tk=PW, epi=lambda acc, t, d: (acc + t + ALPHA * d,),
              extras=[(jnp.zeros((1, D_MODEL), F32) + tie, "n"), (dr1, "mn")])
    gx, _, G["ln_in_g"], G["ln_in_b"] = _ln_bwd([(x, 1.0)], [(dh0, 1.0)], P["ln_in_g"] + tie, "ln_in_bwd")
    return loss, gx, G


PACK_W = 1024
BIG = (("w_in", (1024, 277), 1), ("conv_w", (4, 128), 1), ("w_q_up", (384, 96), 1), ("w_kv_up", (256, 128), 1),
       ("w_mix_out", (128, 1024), 0), ("w_mem_q", (128, 1024), 0), ("w_mem_k", (128, 1024), 0), ("w_mem_v", (128, 1024), 0),
       ("w_mem_o", (128, 1024), 0), ("w_up", (1024, 512), 1), ("w_down", (512, 1024), 0))
SMALL = ("ln_in_g", "ln_in_b", "conv_b", "ln1_g", "ln1_b", "ln2_g", "ln2_b", "ln3_g", "ln3_b",
         "ssd_norm_g", "q_norm_g", "kv_norm_g", "dt_bias", "a_log", "d_skip")
ALL_W = ("ln_in_g", "ln_in_b", "w_in", "conv_w", "conv_b", "dt_bias", "a_log", "d_skip", "ssd_norm_g", "q_norm_g", "w_q_up",
         "kv_norm_g", "w_kv_up", "w_mix_out", "ln1_g", "ln1_b", "w_mem_q", "w_mem_k", "w_mem_v", "w_mem_o", "ln2_g", "ln2_b",
         "w_up", "w_down", "ln3_g", "ln3_b")


SMALL_R = 16
LOSS_ROW = 15
_SMALL_ROWS = (("ln_in_g",), ("ln_in_b",), ("conv_b",), ("ln1_g",), ("ln1_b",), ("ln2_g",), ("ln2_b",), ("ln3_g",), ("ln3_b",),
               ("ssd_norm_g", "q_norm_g"), ("kv_norm_g", "dt_bias", "a_log", "d_skip"))
_SMALL_W = {"ssd_norm_g": 512, "q_norm_g": 384, "kv_norm_g": 256, "dt_bias": LANES, "a_log": LANES, "d_skip": LANES}


MESH = pl.DeviceIdType.MESH
ANY = pl.BlockSpec(memory_space=pl.ANY)
VM = pl.BlockSpec(memory_space=pltpu.VMEM)


def _coords():
    return lax.axis_index("x"), lax.axis_index("y"), lax.axis_index("c")


def _slot(px, py, pc):
    return 4 * px + 2 * py + pc


def _peer(k, x, y, c):
    dx, dy, dc = (k >> 2) & 1, (k >> 1) & 1, k & 1
    return (1 - x if dx else x, 1 - y if dy else y, 1 - c if dc else c)


def _adam(w, g, m, v):
    m = ADAM_B1 * m + (1.0 - ADAM_B1) * g
    v = ADAM_B2 * v + (1.0 - ADAM_B2) * (g * g)
    m_hat = m / (1.0 - ADAM_B1 ** ADAM_STEP)
    v_hat = v / (1.0 - ADAM_B2 ** ADAM_STEP)
    delta = -ADAM_LR * (m_hat / (jnp.sqrt(v_hat) + ADAM_EPS) + ADAM_WD * w)
    return delta, m, v


def _sum_slots(ref):
    tot = ref[0].astype(F32)
    for q in range(1, N_DEV):
        tot = tot + ref[q].astype(F32)
    return tot


def _row(v):
    return v.reshape(1, -1).astype(F32)


def kernel(x, mem, positions, ln_in_g, ln_in_b, w_in, conv_w, conv_b, dt_bias, a_log, d_skip, ssd_norm_g, q_norm_g, w_q_up, kv_norm_g, w_kv_up, w_mix_out, ln1_g, ln1_b, w_mem_q, w_mem_k, w_mem_v, w_mem_o, ln2_g, ln2_b, w_up, w_down, ln3_g, ln3_b, loss_target, m_ln_in_g, m_ln_in_b, m_w_in, m_conv_w, m_conv_b, m_dt_bias, m_a_log, m_d_skip, m_ssd_norm_g, m_q_norm_g, m_w_q_up, m_kv_norm_g, m_w_kv_up, m_w_mix_out, m_ln1_g, m_ln1_b, m_w_mem_q, m_w_mem_k, m_w_mem_v, m_w_mem_o, m_ln2_g, m_ln2_b, m_w_up, m_w_down, m_ln3_g, m_ln3_b, v_ln_in_g, v_ln_in_b, v_w_in, v_conv_w, v_conv_b, v_dt_bias, v_a_log, v_d_skip, v_ssd_norm_g, v_q_norm_g, v_w_q_up, v_kv_norm_g, v_w_kv_up, v_w_mix_out, v_ln1_g, v_ln1_b, v_w_mem_q, v_w_mem_k, v_w_mem_v, v_w_mem_o, v_ln2_g, v_ln2_b, v_w_up, v_w_down, v_ln3_g, v_ln3_b):
    a = dict(locals())
    W = {n: a[n] for n in ALL_W}
    M = {n: a["m_" + n] for n in ALL_W}
    V = {n: a["v_" + n] for n in ALL_W}
    return _step_overlapped(x, mem, positions, loss_target, W, M, V)


HBM = pl.BlockSpec(memory_space=pltpu.HBM)
SEM = pl.BlockSpec(memory_space=pltpu.SEMAPHORE)
EFFECT = pltpu.SideEffectType.DATAFLOW_SIDE_EFFECTING
SHARD_SHAPE = {n: s for n, s, _ in BIG}
SHARD_AXIS = {n: ax for n, _, ax in BIG}
GATHER_FIRST = ("w_in",)
GATHER_MID = ("conv_w", "w_q_up", "w_kv_up", "w_mix_out")
GATHER_LATE = ("w_mem_q", "w_mem_k", "w_mem_v", "w_mem_o", "w_up", "w_down")


def _my_slot():
    return _slot(*_coords())


def _group_copies(src_refs, land_refs, send_sems, recv_sems, slotted, landing_of_peer):
    x, y, c = _coords()
    my = _slot(x, y, c)
    cps = []
    for a, (s_ref, l_ref) in enumerate(zip(src_refs, land_refs)):
        for k in range(1, N_DEV):
            peer = _peer(k, x, y, c)
            cps.append(pltpu.make_async_remote_copy(
                src_ref=s_ref.at[_slot(*peer)] if slotted else s_ref,
                dst_ref=l_ref.at[_slot(*peer)] if landing_of_peer else l_ref.at[my],
                send_sem=send_sems.at[7 * a + k - 1], recv_sem=recv_sems.at[7 * a + k - 1],
                device_id=peer, device_id_type=MESH))
    return cps


def _send_start(srcs, lands, slotted, name):
    n = len(srcs)

    def body(*refs):
        for cp in _group_copies(refs[:n], refs[n:2 * n], refs[2 * n], refs[2 * n + 1], slotted, False):
            cp.start()
        refs[-1][...] = jnp.zeros(refs[-1].shape, F32)

    res = pl.pallas_call(
        body, name=name,
        out_shape=(pltpu.SemaphoreType.DMA((7 * n,)), pltpu.SemaphoreType.DMA((7 * n,)),
                   *[pltpu.HBM(a.shape, a.dtype) for a in srcs], *[pltpu.HBM(a.shape, a.dtype) for a in lands],
                   jax.ShapeDtypeStruct((8, LANES), F32)),
        in_specs=[HBM] * (2 * n), out_specs=(SEM, SEM, *[HBM] * (2 * n), VM),
        input_output_aliases={i: 2 + i for i in range(2 * n)},
        compiler_params=pltpu.CompilerParams(has_side_effects=EFFECT),
    )(*[pltpu.with_memory_space_constraint(a, pltpu.HBM) for a in list(srcs) + list(lands)])
    return (res[0], res[1], res[2:2 + n], res[2 + n:2 + 2 * n]), res[-1][:1, :1]


def _send_wait(started, after, slotted, name):
    send_sems, recv_sems, srcs, lands = started
    n = len(srcs)
    after = list(after) if isinstance(after, (list, tuple)) else [after]

    def body(*refs):
        for cp in _group_copies(refs[:n], refs[n:2 * n], refs[2 * n], refs[2 * n + 1], slotted, True):
            cp.wait_send()
            cp.wait_recv()

    res = pl.pallas_call(
        body, name=name, out_shape=tuple(pltpu.HBM(a.shape, a.dtype) for a in list(srcs) + list(lands)),
        in_specs=[HBM] * (2 * n) + [SEM, SEM] + [ANY] * len(after), out_specs=tuple([HBM] * (2 * n)),
        input_output_aliases={i: i for i in range(2 * n)},
        compiler_params=pltpu.CompilerParams(has_side_effects=EFFECT),
    )(*srcs, *lands, send_sems, recv_sems, *after)
    return res[n:]


def _landing(own, my):
    return lax.dynamic_update_slice(lax.empty((N_DEV,) + own.shape, own.dtype), own[None], (my,) + (0,) * own.ndim)


def _full_from_slots(name, slots):
    a, b = SHARD_SHAPE[name]
    return slots.reshape(N_DEV * a, b) if SHARD_AXIS[name] == 0 else slots.transpose(1, 0, 2).reshape(a, N_DEV * b)


def _slots_from_full(name, g):
    a, b = SHARD_SHAPE[name]
    return g.reshape(N_DEV, a, b) if SHARD_AXIS[name] == 0 else g.reshape(a, N_DEV, b).transpose(1, 0, 2)


def _reduce_adam(recv, w, m, v, name):
    _, a, b = recv.shape
    ta = a
    while ta * b * 4 * N_DEV > 4 * 1024 * 1024 and ta % 16 == 0:
        ta //= 2

    def body(r_ref, w_ref, m_ref, v_ref, g_ref, d_ref, nm_ref, nv_ref):
        g = _sum_slots(r_ref)
        g_ref[...] = g
        d_ref[...], nm_ref[...], nv_ref[...] = _adam(w_ref[...], g, m_ref[...], v_ref[...])

    row = pl.BlockSpec((ta, b), lambda i: (i, 0))
    return pl.pallas_call(
        body, name=name, grid=(a // ta,),
        in_specs=[pl.BlockSpec((N_DEV, ta, b), lambda i: (0, i, 0)), row, row, row], out_specs=[row] * 4,
        out_shape=[jax.ShapeDtypeStruct((a, b), F32)] * 4, compiler_params=_cparams(("parallel",)),
    )(recv, w, m, v)


def _step_overlapped(x, mem, positions, tgt, W, M, V):
    my = _my_slot()
    shard = {n: W[n][0] for n, _, _ in BIG}
    send = {n: (shard[n] if n == "conv_w" else shard[n].astype(BF16)) for n in shard}

    first_src, mid_src, late_src = ([send[n] for n in grp] for grp in (GATHER_FIRST, GATHER_MID, GATHER_LATE))
    first, tie = _send_start(first_src, [_landing(s, my) for s in first_src], False, "gather_first_start")
    my_then = my + tie[0, 0].astype(jnp.int32)
    mid, tie = _send_start(mid_src, [_landing(s, my_then) for s in mid_src], False, "gather_mid_start")
    my_then = my + tie[0, 0].astype(jnp.int32)
    late, tie = _send_start(late_src, [_landing(s, my_then) for s in late_src], False, "gather_late_start")

    P = {n: _row(W[n]) for n in SMALL}
    for n in ("dt_bias", "a_log", "d_skip"):
        P[n] = _head_lanes(P[n])
    P["ln_in_g"] = P["ln_in_g"] + tie

    def weights_at(stage, after):
        if stage == "first":
            lands = _send_wait(first, after, False, "gather_first_wait")
            return dict(w_in=_pad_w_in(_full_from_slots("w_in", lands[0])))
        if stage == "mid":
            lands = _send_wait(mid, after, False, "gather_mid_wait")
            full = {n: _full_from_slots(n, l) for n, l in zip(GATHER_MID, lands)}
            return dict(w_q_up=_pad_w_q(full["w_q_up"]), w_kv_up=_pad_w_kv(full["w_kv_up"]), w_mix_out=full["w_mix_out"],
                        conv_w=jnp.pad(full["conv_w"], ((0, 8 - SSD_CONV), (0, 0))))
        lands = _send_wait(late, after, False, "gather_late_wait")
        return {n: _full_from_slots(n, l) for n, l in zip(GATHER_LATE, lands)}

    started, res = [], {}

    def finish(i, after):
        names, st = started[i]
        lands = _send_wait(st, after, True, "scatter_wait_%d" % i)
        for n, recv in zip(names, lands):
            res[n] = _reduce_adam(recv, shard[n], M[n][0], V[n][0], "reduce_adam_" + n)
        return sum(res[n][0][:1, :1] for n in names) * 0.0

    def emit(names, G):
        srcs = []
        for n in names:
            g = G[n]
            if n == "w_in":
                g = _unpad_w_in(g)
            elif n == "w_q_up":
                g = _unpad_w_q(g)
            elif n == "w_kv_up":
                g = _unpad_w_kv(g)
            elif n == "conv_w":
                g = g[:SSD_CONV]
            srcs.append(g if g.ndim == 3 else _slots_from_full(n, g))
        lands = [_landing(lax.dynamic_index_in_dim(s, my, 0, keepdims=False), my) for s in srcs]
        st, tie = _send_start(srcs, lands, True, "scatter_start_%d" % len(started))
        started.append((names, st))
        if len(started) == 3:
            tie = tie + finish(0, srcs[0]) + finish(1, srcs[0])
        if len(started) == 4:
            tie = tie + finish(2, srcs[0])
        return tie

    loss, gx, G = _local_step(x[0], mem[0], positions[0], tgt[0], P, weights_at, emit)
    finish(3, gx)

    small, loss_tot = _allreduce_adam_vectors(
        {n: G[n] for n in SMALL}, loss, {n: _row(W[n]) for n in SMALL}, {n: _row(M[n]) for n in SMALL}, {n: _row(V[n]) for n in SMALL})

    outs = []
    for j in range(4):
        for n in ALL_W:
            outs.append((res[n][j] if n in res else small[j][n]).reshape(W[n].shape))
    return (loss_tot[0, 0], gx[None], *outs)


def _vector_places():
    places = {}
    for r, names in enumerate(_SMALL_ROWS):
        c = 0
        for n in names:
            w = _SMALL_W.get(n, PACK_W)
            places[n] = (r, c, w, SSD_HEADS if n in ("dt_bias", "a_log", "d_skip") else w)
            c += w
    return places


def _allreduce_adam_vectors(grads, loss, Ws, Ms, Vs):
    places = _vector_places()
    ns = len(SMALL)

    def body(*refs):
        g_in, loss_in = refs[:ns], refs[ns]
        w_in, m_in, v_in = refs[ns + 1:2 * ns + 1], refs[2 * ns + 1:3 * ns + 1], refs[3 * ns + 1:4 * ns + 1]
        o = 4 * ns + 1
        outs = [refs[o + j * ns:o + (j + 1) * ns] for j in range(4)]
        loss_out, stage, land, send_sems, recv_sems = refs[o + 4 * ns:]
        stage[...] = jnp.zeros(stage.shape, F32)
        for i, n in enumerate(SMALL):
            r, c, w, _ = places[n]
            stage[r:r + 1, c:c + w] = g_in[i][...]
        stage[LOSS_ROW:LOSS_ROW + 1, 0:LANES] = loss_in[...]
        x, y, c_ = _coords()
        my = _slot(x, y, c_)
        cps = []
        for k in range(1, N_DEV):
            peer = _peer(k, x, y, c_)
            cps.append(pltpu.make_async_remote_copy(
                src_ref=stage, dst_ref=land.at[my], send_sem=send_sems.at[k - 1], recv_sem=recv_sems.at[k - 1],
                device_id=peer, device_id_type=MESH))
        for cp in cps:
            cp.start()
        land[my] = stage[...]
        for cp in cps:
            cp.wait_recv()
        for cp in cps:
            cp.wait_send()
        tot = _sum_slots(land)
        loss_out[...] = tot[LOSS_ROW:LOSS_ROW + 1, 0:LANES]
        for i, n in enumerate(SMALL):
            r, c, _, wt = places[n]
            g = tot[r:r + 1, c:c + wt]
            outs[0][i][...] = g
            outs[1][i][...], outs[2][i][...], outs[3][i][...] = _adam(w_in[i][...], g, m_in[i][...], v_in[i][...])

    shapes = [jax.ShapeDtypeStruct((1, places[n][3]), F32) for n in SMALL]
    res = pl.pallas_call(
        body, name="allreduce_vectors", in_specs=[VM] * (4 * ns + 1), out_specs=[VM] * (4 * ns + 1),
        out_shape=shapes * 4 + [jax.ShapeDtypeStruct((1, LANES), F32)],
        scratch_shapes=[pltpu.VMEM((SMALL_R, PACK_W), F32), pltpu.VMEM((N_DEV, SMALL_R, PACK_W), F32),
                        pltpu.SemaphoreType.DMA((7,)), pltpu.SemaphoreType.DMA((7,))],
    )(*[grads[n] for n in SMALL], loss, *[Ws[n] for n in SMALL], *[Ms[n] for n in SMALL], *[Vs[n] for n in SMALL])
    return [dict(zip(SMALL, res[j * ns:(j + 1) * ns])) for j in range(4)], res[4 * ns]
```

```python
import math

import jax
import jax.numpy as jnp
from jax import lax
from jax.experimental import pallas as pl
from jax.experimental.pallas import tpu as pltpu

F32, BF16 = jnp.float32, jnp.bfloat16

N_DEV = 8
D_MODEL = 1024
SSD_HEADS, SSD_HEAD_DIM, SSD_INNER, SSD_STATE, SSD_CONV, SSD_CHUNK = 8, 64, 512, 128, 4, 128
SSD_XBC = 1024
MLA_HEADS, MLA_NOPE, MLA_ROPE, MLA_QK, MLA_V = 8, 64, 32, 96, 64
MLA_Q_RANK, MLA_KV_RANK = 384, 256
ROPE_THETA = 10000.0
MEM_HEADS, MEM_HEAD_DIM = 4, 256
D_FF = 4096
IN_WIDTH = 2216
LN_EPS, RMS_EPS = 1e-5, 1e-6
ALPHA = 2.0 ** 0.25
ADAM_LR, ADAM_B1, ADAM_B2, ADAM_EPS, ADAM_WD, ADAM_STEP = 0.001, 0.9, 0.999, 1e-08, 0.01, 10

LANES = 128
NEG = -1e30
VMEM_LIMIT = 56 * 1024 * 1024
TOK_K = 4096

PQ, PZ, PX, PKV, PDT, PKR, PW = 0, 512, 1024, 2048, 2304, 2432, 2560
KR_LANE = 64


def _cparams(sem):
    return pltpu.CompilerParams(dimension_semantics=sem, vmem_limit_bytes=VMEM_LIMIT)


def _sigmoid(x):
    return 1.0 / (1.0 + jnp.exp(-x))


def _mm(a, b, mode, name, *, tm=512, tn=None, tk=None, out_dtypes=(F32,), epi=None, extras=(), col_slots=False):
    if mode == "nn":
        (M, K), (K2, N) = a.shape, b.shape
    elif mode == "nt":
        (M, K), (N, K2) = a.shape, b.shape
    else:
        (K, M), (K2, N) = a.shape, b.shape
    assert K == K2, (name, a.shape, b.shape)
    tm, tn, tk = min(tm, M), min(tn or N, N), min(tk or K, K)
    assert M % tm == 0 and N % tn == 0 and K % tk == 0, (name, M, N, K, tm, tn, tk)
    gk = K // tk
    a_spec = pl.BlockSpec((tk, tm), lambda i, j, k: (k, i)) if mode == "tn" else pl.BlockSpec((tm, tk), lambda i, j, k: (i, k))
    b_spec = pl.BlockSpec((tn, tk), lambda i, j, k: (j, k)) if mode == "nt" else pl.BlockSpec((tk, tn), lambda i, j, k: (k, j))
    dims = {"nn": ((1,), (0,)), "nt": ((1,), (1,)), "tn": ((0,), (0,))}[mode]
    ex_specs = []
    for arr, kind in extras:
        if kind == "mn":
            ex_specs.append(pl.BlockSpec((tm, tn), lambda i, j, k: (i, j)))
        elif kind == "n":
            ex_specs.append(pl.BlockSpec((1, tn), lambda i, j, k: (0, j)))
        else:
            ex_specs.append(pl.BlockSpec((tm, arr.shape[1]), lambda i, j, k: (i, 0)))
    ne, no = len(extras), len(out_dtypes)

    def body(*refs):
        a_ref, b_ref = refs[0], refs[1]
        ex, outs = refs[2:2 + ne], refs[2 + ne:2 + ne + no]
        part = lax.dot_general(a_ref[...].astype(BF16), b_ref[...].astype(BF16), (dims, ((), ())),
                               preferred_element_type=F32)

        def finish(acc):
            res = epi(acc, *[e[...] for e in ex]) if epi is not None else (acc,)
            for o, r in zip(outs, res):
                o[...] = r.astype(o.dtype)

        if gk == 1:
            finish(part)
        else:
            acc_ref = refs[-1]
            k = pl.program_id(2)

            @pl.when(k == 0)
            def _():
                acc_ref[...] = part

            @pl.when(k > 0)
            def _():
                acc_ref[...] += part

            @pl.when(k == gk - 1)
            def _():
                finish(acc_ref[...])

    res = pl.pallas_call(
        body, name=name, grid=(M // tm, N // tn, gk),
        in_specs=[a_spec, b_spec] + ex_specs,
        out_specs=[pl.BlockSpec((None, tm, tn), lambda i, j, k: (j, i, 0)) if col_slots else pl.BlockSpec((tm, tn), lambda i, j, k: (i, j))
                   for _ in out_dtypes],
        out_shape=[jax.ShapeDtypeStruct((N // tn, M, tn) if col_slots else (M, N), dt) for dt in out_dtypes],
        scratch_shapes=[pltpu.VMEM((tm, tn), F32)] if gk > 1 else [],
        compiler_params=_cparams(("parallel", "parallel", "arbitrary")),
    )(a, b, *[e[0] for e in extras])
    return res[0] if no == 1 else res


def _ln_stats(r):
    mu = jnp.mean(r, axis=-1, keepdims=True)
    xc = r - mu
    var = jnp.mean(xc * xc, axis=-1, keepdims=True)
    rstd = lax.rsqrt(var + LN_EPS)
    return xc * rstd, rstd


def _ln_fwd(terms, g, b, name, tm=512):
    S, D = terms[0][0].shape
    coefs = [c for _, c in terms]
    nt = len(terms)

    def body(*refs):
        r = sum(c * t[...] for t, c in zip(refs[:nt], coefs))
        xh, _ = _ln_stats(r)
        h = xh * refs[nt][...] + refs[nt + 1][...]
        refs[nt + 2][...] = h
        refs[nt + 3][...] = h.astype(BF16)

    row = pl.BlockSpec((tm, D), lambda i: (i, 0))
    vec = pl.BlockSpec((1, D), lambda i: (0, 0))
    return pl.pallas_call(
        body, name=name, grid=(S // tm,), in_specs=[row] * nt + [vec, vec], out_specs=[row, row],
        out_shape=[jax.ShapeDtypeStruct((S, D), F32), jax.ShapeDtypeStruct((S, D), BF16)], compiler_params=_cparams(("parallel",)),
    )(*[t for t, _ in terms], g, b)


def _ln_bwd(terms, dterms, g, name, tm=512):
    S, D = terms[0][0].shape
    coefs, dcoefs = [c for _, c in terms], [c for _, c in dterms]
    nt, nd = len(terms), len(dterms)

    def body(*refs):
        i = pl.program_id(0)
        r = sum(c * t[...] for t, c in zip(refs[:nt], coefs))
        dh = sum(c * t[...].astype(F32) for t, c in zip(refs[nt:nt + nd], dcoefs))
        g_ref = refs[nt + nd]
        dr_ref, drb_ref, dg_ref, db_ref = refs[nt + nd + 1:]
        xh, rstd = _ln_stats(r)
        dxh = dh * g_ref[...]
        m1 = jnp.mean(dxh, axis=-1, keepdims=True)
        m2 = jnp.mean(dxh * xh, axis=-1, keepdims=True)
        dr = rstd * (dxh - m1 - xh * m2)
        dr_ref[...] = dr
        drb_ref[...] = dr.astype(BF16)
        pg = jnp.sum(dh * xh, axis=0, keepdims=True)
        pb = jnp.sum(dh, axis=0, keepdims=True)

        @pl.when(i == 0)
        def _():
            dg_ref[...] = pg
            db_ref[...] = pb

        @pl.when(i > 0)
        def _():
            dg_ref[...] += pg
            db_ref[...] += pb

    row = pl.BlockSpec((tm, D), lambda i: (i, 0))
    vec = pl.BlockSpec((1, D), lambda i: (0, 0))
    return pl.pallas_call(
        body, name=name, grid=(S // tm,), in_specs=[row] * (nt + nd) + [vec], out_specs=[row, row, vec, vec],
        out_shape=[jax.ShapeDtypeStruct((S, D), F32), jax.ShapeDtypeStruct((S, D), BF16), jax.ShapeDtypeStruct((1, D), F32),
                   jax.ShapeDtypeStruct((1, D), F32)],
        compiler_params=_cparams(("arbitrary",)),
    )(*[t for t, _ in terms], *[t for t, _ in dterms], g)


def _ln_loss_bwd(terms, g, b, tgt, name, tm=512):
    S, D = terms[0][0].shape
    coefs = [c for _, c in terms]
    nt = len(terms)

    def body(*refs):
        i = pl.program_id(0)
        r = sum(c * t[...] for t, c in zip(refs[:nt], coefs))
        g_ref, b_ref, t_ref = refs[nt:nt + 3]
        loss_ref, dr_ref, drb_ref, dg_ref, db_ref = refs[nt + 3:]
        xh, rstd = _ln_stats(r)
        h = xh * g_ref[...] + b_ref[...]
        diff = h - t_ref[...]
        pl_ = 0.5 * jnp.sum(jnp.mean(diff * diff, axis=-1, keepdims=True), axis=0, keepdims=True)
        dh = diff * (1.0 / D)
        dxh = dh * g_ref[...]
        m1 = jnp.mean(dxh, axis=-1, keepdims=True)
        m2 = jnp.mean(dxh * xh, axis=-1, keepdims=True)
        dr = rstd * (dxh - m1 - xh * m2)
        dr_ref[...] = dr
        drb_ref[...] = dr.astype(BF16)
        pg = jnp.sum(dh * xh, axis=0, keepdims=True)
        pb = jnp.sum(dh, axis=0, keepdims=True)
        plb = jnp.broadcast_to(pl_, (1, LANES))

        @pl.when(i == 0)
        def _():
            dg_ref[...] = pg
            db_ref[...] = pb
            loss_ref[...] = plb

        @pl.when(i > 0)
        def _():
            dg_ref[...] += pg
            db_ref[...] += pb
            loss_ref[...] += plb

    row = pl.BlockSpec((tm, D), lambda i: (i, 0))
    vec = pl.BlockSpec((1, D), lambda i: (0, 0))
    lvec = pl.BlockSpec((1, LANES), lambda i: (0, 0))
    return pl.pallas_call(
        body, name=name, grid=(S // tm,), in_specs=[row] * nt + [vec, vec, row], out_specs=[lvec, row, row, vec, vec],
        out_shape=[jax.ShapeDtypeStruct((1, LANES), F32), jax.ShapeDtypeStruct((S, D), F32), jax.ShapeDtypeStruct((S, D), BF16),
                   jax.ShapeDtypeStruct((1, D), F32), jax.ShapeDtypeStruct((1, D), F32)],
        compiler_params=_cparams(("arbitrary",)),
    )(*[t for t, _ in terms], g, b, tgt)


def _rope_tables(positions):
    half = MLA_ROPE // 2
    inv_freq = jnp.power(ROPE_THETA, -jnp.arange(half, dtype=F32) / half)
    ang = positions.astype(F32)[:, None] * inv_freq
    cos, sin = jnp.cos(ang), jnp.sin(ang)
    S = positions.shape[0]
    one, zero = jnp.ones((S, MLA_NOPE), F32), jnp.zeros((S, half), F32)
    pad = jnp.zeros((S, LANES - MLA_QK), F32)
    c = jnp.concatenate([one, cos, cos, pad], axis=1)
    s1 = jnp.concatenate([0 * one, -sin, zero, pad], axis=1)
    s2 = jnp.concatenate([0 * one, zero, sin, pad], axis=1)
    return c, s1, s2


def _rope_block(x, c, s1, s2):
    half = MLA_ROPE // 2
    return x * c + pltpu.roll(x, LANES - half, axis=1) * s1 + pltpu.roll(x, half, axis=1) * s2


def _rms_fwd(x, g):
    r = lax.rsqrt(jnp.mean(x * x, axis=-1, keepdims=True) + RMS_EPS)
    return x * r * g


def _rms_bwd(x, g, dy):
    r = lax.rsqrt(jnp.mean(x * x, axis=-1, keepdims=True) + RMS_EPS)
    xh = x * r
    dyh = dy * g
    dx = r * (dyh - xh * jnp.mean(dyh * xh, axis=-1, keepdims=True))
    return dx, jnp.sum(dy * xh, axis=0, keepdims=True)


def _mla_prep(proj, qg, kvg, tabs, tm=512):
    S = proj.shape[0]

    def body(ql_ref, kvl_ref, kr_ref, qg_ref, kvg_ref, c_ref, s1_ref, s2_ref, qn_ref, kvn_ref, kpe_ref):
        qn_ref[...] = _rms_fwd(ql_ref[...], qg_ref[...]).astype(BF16)
        kvn_ref[...] = _rms_fwd(kvl_ref[...], kvg_ref[...]).astype(BF16)
        kpe_ref[...] = _rope_block(kr_ref[...], c_ref[...], s1_ref[...], s2_ref[...])

    tab = pl.BlockSpec((tm, LANES), lambda i: (i, 0))
    return pl.pallas_call(
        body, name="mla_prep", grid=(S // tm,),
        in_specs=[pl.BlockSpec((tm, MLA_Q_RANK), lambda i: (i, PQ // MLA_Q_RANK)),
                  pl.BlockSpec((tm, MLA_KV_RANK), lambda i: (i, PKV // MLA_KV_RANK)),
                  pl.BlockSpec((tm, LANES), lambda i: (i, PKR // LANES)),
                  pl.BlockSpec((1, MLA_Q_RANK), lambda i: (0, 0)), pl.BlockSpec((1, MLA_KV_RANK), lambda i: (0, 0)),
                  tab, tab, tab],
        out_specs=[pl.BlockSpec((tm, MLA_Q_RANK), lambda i: (i, 0)), pl.BlockSpec((tm, MLA_KV_RANK), lambda i: (i, 0)), tab],
        out_shape=[jax.ShapeDtypeStruct((S, MLA_Q_RANK), BF16), jax.ShapeDtypeStruct((S, MLA_KV_RANK), BF16),
                   jax.ShapeDtypeStruct((S, LANES), F32)],
        compiler_params=_cparams(("parallel",)),
    )(proj, proj, proj, qg, kvg, *tabs)


def _mla_prep_bwd(proj, qg, kvg, tabs, dqn, dkvn, dk_all, tm=512):
    S = proj.shape[0]

    def body(ql_ref, kvl_ref, qg_ref, kvg_ref, c_ref, s1_ref, s2_ref, dqn_ref, dkvn_ref, dk_ref,
             dql_ref, dkvl_ref, dkr_ref, dqg_ref, dkvg_ref):
        i = pl.program_id(0)
        dql, pq = _rms_bwd(ql_ref[...], qg_ref[...], dqn_ref[...])
        dkvl, pkv = _rms_bwd(kvl_ref[...], kvg_ref[...], dkvn_ref[...])
        dql_ref[...] = dql
        dkvl_ref[...] = dkvl
        dk = dk_ref[...]
        dkpe = dk[:, 0:LANES]
        for h in range(1, MLA_HEADS):
            dkpe = dkpe + dk[:, h * LANES:(h + 1) * LANES]
        lane = lax.broadcasted_iota(jnp.int32, dkpe.shape, 1)
        dkpe = jnp.where((lane >= KR_LANE) & (lane < KR_LANE + MLA_ROPE), dkpe, 0.0)
        dkr_ref[...] = _rope_block(dkpe, c_ref[...], -s1_ref[...], -s2_ref[...])

        @pl.when(i == 0)
        def _():
            dqg_ref[...] = pq
            dkvg_ref[...] = pkv

        @pl.when(i > 0)
        def _():
            dqg_ref[...] += pq
            dkvg_ref[...] += pkv

    tab = pl.BlockSpec((tm, LANES), lambda i: (i, 0))
    qspec = pl.BlockSpec((tm, MLA_Q_RANK), lambda i: (i, 0))
    kvspec = pl.BlockSpec((tm, MLA_KV_RANK), lambda i: (i, 0))
    qv, kvv = pl.BlockSpec((1, MLA_Q_RANK), lambda i: (0, 0)), pl.BlockSpec((1, MLA_KV_RANK), lambda i: (0, 0))
    return pl.pallas_call(
        body, name="mla_prep_bwd", grid=(S // tm,),
        in_specs=[pl.BlockSpec((tm, MLA_Q_RANK), lambda i: (i, PQ // MLA_Q_RANK)),
                  pl.BlockSpec((tm, MLA_KV_RANK), lambda i: (i, PKV // MLA_KV_RANK)),
                  qv, kvv, tab, tab, tab, qspec, kvspec, pl.BlockSpec((tm, MLA_HEADS * LANES), lambda i: (i, 0))],
        out_specs=[qspec, kvspec, tab, qv, kvv],
        out_shape=[jax.ShapeDtypeStruct((S, MLA_Q_RANK), F32), jax.ShapeDtypeStruct((S, MLA_KV_RANK), F32),
                   jax.ShapeDtypeStruct((S, LANES), F32), jax.ShapeDtypeStruct((1, MLA_Q_RANK), F32),
                   jax.ShapeDtypeStruct((1, MLA_KV_RANK), F32)],
        compiler_params=_cparams(("arbitrary",)),
    )(proj, proj, qg, kvg, *tabs, dqn, dkvn, dk_all)


def _rope_bwd_all(dq_all, tabs, scale, tm=512):
    S, W = dq_all.shape

    def body(dq_ref, c_ref, s1_ref, s2_ref, o_ref):
        c, s1, s2 = scale * c_ref[...], -scale * s1_ref[...], -scale * s2_ref[...]
        for h in range(W // LANES):
            o_ref[:, h * LANES:(h + 1) * LANES] = _rope_block(dq_ref[:, h * LANES:(h + 1) * LANES], c, s1, s2).astype(BF16)

    tab = pl.BlockSpec((tm, LANES), lambda i: (i, 0))
    row = pl.BlockSpec((tm, W), lambda i: (i, 0))
    return pl.pallas_call(body, name="rope_bwd", grid=(S // tm,), in_specs=[row, tab, tab, tab], out_specs=row,
                          out_shape=jax.ShapeDtypeStruct((S, W), BF16), compiler_params=_cparams(("parallel",)))(dq_all, *tabs)


ATT_SCALE = MLA_QK ** -0.5
LN2 = math.log(2.0)
Q_PRESCALE = ATT_SCALE / LN2
DQ_POSTSCALE = ATT_SCALE / LN2
N_PAIR = MLA_HEADS // 2


def _causal_mask(qi, ki, tq, tk):
    row = qi * tq + lax.broadcasted_iota(jnp.int32, (tq, tk), 0)
    col = ki * tk + lax.broadcasted_iota(jnp.int32, (tq, tk), 1)
    return col <= row


def _lane_tile(x, n):
    return jnp.concatenate([x] * n, axis=1) if n > 1 else x


def _attn_fwd(q_all, kv_all, cat, tq=512, tk=2048):
    S = q_all.shape[0]
    tq, tk = min(tq, S), min(tk, S)
    nq, nk, nb, r = S // tq, S // tk, tk // LANES, tk // tq

    def body(q_ref, k_ref, v_ref, cat_in, o_ref, lse_ref, cat_ref, m_s, l_s, acc_s):
        qi, ki = pl.program_id(1), pl.program_id(2)
        last = lax.div(qi, r)

        @pl.when(ki == 0)
        def _():
            m_s[...] = jnp.full(m_s.shape, NEG, F32)
            l_s[...] = jnp.zeros(l_s.shape, F32)
            acc_s[...] = jnp.zeros(acc_s.shape, F32)

        def block(kc, mask):
            v = v_ref[0:kc, :]
            for hh in range(2):
                q = q_ref[:, hh * LANES:(hh + 1) * LANES]
                k = k_ref[0:kc, hh * LANES:(hh + 1) * LANES]
                s = lax.dot_general(q, k, (((1,), (1,)), ((), ())), preferred_element_type=F32)
                if mask is not None:
                    s = jnp.where(mask, s, NEG)
                m_prev = m_s[hh]
                m_new = jnp.maximum(m_prev, jnp.max(s, axis=-1, keepdims=True))
                p = jnp.exp2(s - _lane_tile(m_new, kc // LANES))
                alpha = jnp.exp2(m_prev - m_new)
                ps = p[:, :LANES]
                for j in range(1, kc // LANES):
                    ps = ps + p[:, j * LANES:(j + 1) * LANES]
                l_s[hh] = alpha * l_s[hh] + ps
                acc_s[hh] = alpha * acc_s[hh] + jnp.dot(p.astype(BF16), v, preferred_element_type=F32)
                m_s[hh] = m_new

        @pl.when(ki < last)
        def _():
            block(tk, None)

        for j in range(r):
            @pl.when((ki == last) & (lax.rem(qi, r) == j))
            def _(j=j):
                kc = (j + 1) * tq
                block(kc, lax.broadcasted_iota(jnp.int32, (tq, kc), 1) <= j * tq + lax.broadcasted_iota(jnp.int32, (tq, kc), 0))

        @pl.when(ki == last)
        def _():
            first = lax.broadcasted_iota(jnp.int32, (tq, LANES), 1) < MLA_V
            l0 = jnp.sum(l_s[0], axis=-1, keepdims=True)
            l1 = jnp.sum(l_s[1], axis=-1, keepdims=True)
            o = jnp.where(first, acc_s[0] / l0, acc_s[1] / l1)
            o_ref[...] = o
            cat_ref[...] = o.astype(BF16)
            lse_ref[:, :LANES] = m_s[0] + jnp.log2(l0)
            lse_ref[:, LANES:] = m_s[1] + jnp.log2(l1)

    return pl.pallas_call(
        body, name="mla_attn_fwd", grid=(N_PAIR, nq, nk),
        in_specs=[pl.BlockSpec((tq, 2 * LANES), lambda p, qi, ki: (qi, p)),
                  pl.BlockSpec((tk, 2 * LANES), lambda p, qi, ki: (jnp.minimum(ki, lax.div(qi, r)), p)),
                  pl.BlockSpec((tk, LANES), lambda p, qi, ki: (jnp.minimum(ki, lax.div(qi, r)), MLA_HEADS + p)), ANY],
        out_specs=[pl.BlockSpec((tq, LANES), lambda p, qi, ki: (qi, p)), pl.BlockSpec((tq, 2 * LANES), lambda p, qi, ki: (qi, p)),
                   pl.BlockSpec((tq, LANES), lambda p, qi, ki: (qi, N_PAIR + p))],
        out_shape=[jax.ShapeDtypeStruct((S, MLA_HEADS * MLA_V), F32), jax.ShapeDtypeStruct((S, MLA_HEADS * LANES), F32),
                   jax.ShapeDtypeStruct(cat.shape, cat.dtype)],
        input_output_aliases={3: 2},
        scratch_shapes=[pltpu.VMEM((2, tq, LANES), F32), pltpu.VMEM((2, tq, LANES), F32), pltpu.VMEM((2, tq, LANES), F32)],
        compiler_params=_cparams(("parallel", "parallel", "arbitrary")),
    )(q_all, kv_all, kv_all, cat)


def _attn_delta(do, o):
    prod = do * o
    tm = prod.shape[0]
    first = lax.broadcasted_iota(jnp.int32, (tm, LANES), 1) < MLA_V
    blocks = []
    for p in range(N_PAIR):
        pp = prod[:, p * LANES:(p + 1) * LANES]
        blocks.append(jnp.broadcast_to(jnp.sum(jnp.where(first, pp, 0.0), axis=-1, keepdims=True), (tm, LANES)))
        blocks.append(jnp.broadcast_to(jnp.sum(jnp.where(first, 0.0, pp), axis=-1, keepdims=True), (tm, LANES)))
    return jnp.concatenate(blocks, axis=1)


def _attn_bwd(q_all, kv_all, dcat, lse, delta, tq=1024, tk=1024):
    S = q_all.shape[0]
    tq, tk = min(tq, S), min(tk, S)
    nq, nk, nb = S // tq, S // tk, tk // LANES
    assert tq == tk

    def body(q_ref, k_ref, v_ref, do_ref, lse_ref, dl_ref, dq_ref, dk_ref, dv_ref, dk_s, dv_s):
        ki, qi = pl.program_id(1), pl.program_id(2)

        @pl.when((ki == 0) & (qi == 0))
        def _():
            dq_ref[...] = jnp.zeros(dq_ref.shape, F32)

        @pl.when(qi == 0)
        def _():
            dk_s[...] = jnp.zeros(dk_s.shape, F32)
            dv_s[...] = jnp.zeros(dv_s.shape, F32)

        def block(r0, nr, kc, mask):
            v, do = v_ref[0:kc, :], do_ref[r0:r0 + nr, :]
            first = lax.broadcasted_iota(jnp.int32, (nr, LANES), 1) < MLA_V
            firstk = lax.broadcasted_iota(jnp.int32, (kc, LANES), 1) < MLA_V
            do_b = do.astype(BF16)
            rows = pl.ds(pl.multiple_of(qi * tq + r0, LANES), nr)
            for hh in range(2):
                sl = slice(hh * LANES, (hh + 1) * LANES)
                q, k = q_ref[r0:r0 + nr, sl], k_ref[0:kc, sl]
                s = lax.dot_general(q, k, (((1,), (1,)), ((), ())), preferred_element_type=F32)
                if mask is not None:
                    s = jnp.where(mask, s, NEG)
                p = jnp.exp2(s - _lane_tile(lse_ref[r0:r0 + nr, sl], kc // LANES))
                do_h = jnp.where(first if hh == 0 else ~first, do, 0.0).astype(BF16)
                dp = lax.dot_general(do_h, v, (((1,), (1,)), ((), ())), preferred_element_type=F32)
                ds_b = (p * (dp - _lane_tile(dl_ref[r0:r0 + nr, sl], kc // LANES)) * LN2).astype(BF16)
                pv = lax.dot_general(p.astype(BF16), do_b, (((0,), (0,)), ((), ())), preferred_element_type=F32)
                dv_s[0:kc, :] += jnp.where(firstk if hh == 0 else ~firstk, pv, 0.0)
                dk_s[0:kc, sl] += lax.dot_general(ds_b, q, (((0,), (0,)), ((), ())), preferred_element_type=F32)
                dq_ref[rows, sl] += jnp.dot(ds_b, k, preferred_element_type=F32)

        @pl.when(qi > ki)
        def _():
            block(0, tq, tk, None)

        @pl.when(qi == ki)
        def _():
            h = tq // 2
            block(0, h, h, _causal_mask(0, 0, h, h))
            block(h, h, tk, lax.broadcasted_iota(jnp.int32, (h, tk), 1) <= h + lax.broadcasted_iota(jnp.int32, (h, tk), 0))

        @pl.when(qi == nq - 1)
        def _():
            dk_ref[...] = dk_s[...]
            dv_ref[...] = dv_s[...]

    wide = pl.BlockSpec((tq, 2 * LANES), lambda p, ki, qi: (jnp.maximum(qi, ki), p))
    return pl.pallas_call(
        body, name="mla_attn_bwd", grid=(N_PAIR, nk, nq),
        in_specs=[wide, pl.BlockSpec((tk, 2 * LANES), lambda p, ki, qi: (ki, p)),
                  pl.BlockSpec((tk, LANES), lambda p, ki, qi: (ki, MLA_HEADS + p)),
                  pl.BlockSpec((tq, LANES), lambda p, ki, qi: (jnp.maximum(qi, ki), N_PAIR + p)), wide, wide],
        out_specs=[pl.BlockSpec((S, 2 * LANES), lambda p, ki, qi: (0, p)),
                   pl.BlockSpec((tk, 2 * LANES), lambda p, ki, qi: (ki, p)), pl.BlockSpec((tk, LANES), lambda p, ki, qi: (ki, p))],
        out_shape=[jax.ShapeDtypeStruct((S, MLA_HEADS * LANES), F32), jax.ShapeDtypeStruct((S, MLA_HEADS * LANES), F32),
                   jax.ShapeDtypeStruct((S, MLA_HEADS * MLA_V), F32)],
        scratch_shapes=[pltpu.VMEM((tk, 2 * LANES), F32), pltpu.VMEM((tk, LANES), F32)],
        compiler_params=_cparams(("parallel", "arbitrary", "arbitrary")),
    )(q_all, kv_all, kv_all, dcat, lse, delta)


MEM_SCALE = MEM_HEAD_DIM ** -0.5


def _mem_probs(q, k):
    s = lax.dot_general(q, k, (((1,), (1,)), ((), ())), preferred_element_type=F32) * MEM_SCALE
    e = jnp.exp(s - jnp.max(s, axis=-1, keepdims=True))
    return e / jnp.sum(e, axis=-1, keepdims=True)


def _mem_attn_fwd(qm, km, vm, tq=512):
    S, W = qm.shape
    M = km.shape[0]

    def body(q_ref, k_ref, v_ref, o_ref):
        for h in range(MEM_HEADS):
            sl = slice(h * MEM_HEAD_DIM, (h + 1) * MEM_HEAD_DIM)
            p = _mem_probs(q_ref[:, sl], k_ref[:, sl])
            o_ref[:, sl] = jnp.dot(p.astype(BF16), v_ref[:, sl], preferred_element_type=F32).astype(BF16)

    row = pl.BlockSpec((tq, W), lambda i: (i, 0))
    full = pl.BlockSpec((M, W), lambda i: (0, 0))
    return pl.pallas_call(body, name="mem_attn_fwd", grid=(S // tq,), in_specs=[row, full, full], out_specs=row,
                          out_shape=jax.ShapeDtypeStruct((S, W), BF16), compiler_params=_cparams(("parallel",)))(qm, km, vm)


def _mem_attn_bwd(qm, km, vm, dom, tq=512):
    S, W = qm.shape
    M = km.shape[0]

    def body(q_ref, k_ref, v_ref, do_ref, dq_ref, dk_ref, dv_ref):
        i = pl.program_id(0)

        @pl.when(i == 0)
        def _():
            dk_ref[...] = jnp.zeros(dk_ref.shape, F32)
            dv_ref[...] = jnp.zeros(dv_ref.shape, F32)

        for h in range(MEM_HEADS):
            sl = slice(h * MEM_HEAD_DIM, (h + 1) * MEM_HEAD_DIM)
            q, k, v, do = q_ref[:, sl], k_ref[:, sl], v_ref[:, sl], do_ref[:, sl]
            p = _mem_probs(q, k)
            dv_ref[:, sl] += lax.dot_general(p.astype(BF16), do, (((0,), (0,)), ((), ())), preferred_element_type=F32)
            dp = lax.dot_general(do, v, (((1,), (1,)), ((), ())), preferred_element_type=F32)
            ds = (p * (dp - jnp.sum(dp * p, axis=-1, keepdims=True)) * MEM_SCALE).astype(BF16)
            dq_ref[:, sl] = jnp.dot(ds, k, preferred_element_type=F32).astype(BF16)
            dk_ref[:, sl] += lax.dot_general(ds, q, (((0,), (0,)), ((), ())), preferred_element_type=F32)

    row = pl.BlockSpec((tq, W), lambda i: (i, 0))
    full = pl.BlockSpec((M, W), lambda i: (0, 0))
    return pl.pallas_call(
        body, name="mem_attn_bwd", grid=(S // tq,), in_specs=[row, full, full, row], out_specs=[row, full, full],
        out_shape=[jax.ShapeDtypeStruct((S, W), BF16), jax.ShapeDtypeStruct((M, W), F32), jax.ShapeDtypeStruct((M, W), F32)],
        compiler_params=_cparams(("arbitrary",)),
    )(qm, km, vm, dom)


L = SSD_CHUNK
N_SPAIR = SSD_HEADS // 2
GRP_W = SSD_INNER // 2
XB0, XC0 = SSD_INNER, SSD_INNER + 2 * SSD_STATE


def _cumsum_rows(a, reverse=False):
    row = lax.broadcasted_iota(jnp.int32, a.shape, 0)
    x, sft = a, 1
    while sft < L:
        if reverse:
            x = x + jnp.where(row < L - sft, pltpu.roll(x, L - sft, axis=0), 0.0)
        else:
            x = x + jnp.where(row >= sft, pltpu.roll(x, sft, axis=0), 0.0)
        sft *= 2
    return x


def _shift_down(cur, prev, s):
    if s == 0:
        return cur
    row = lax.broadcasted_iota(jnp.int32, cur.shape, 0)
    return jnp.where(row < s, pltpu.roll(prev, s, axis=0), pltpu.roll(cur, s, axis=0))


def _shift_up(cur, nxt, s):
    if s == 0:
        return cur
    row = lax.broadcasted_iota(jnp.int32, cur.shape, 0)
    return jnp.where(row >= L - s, pltpu.roll(nxt, L - s, axis=0), pltpu.roll(cur, L - s, axis=0))


def _ssd_conv(u, prev, cw, cb):
    delayed = [u] + [_shift_down(u, prev, s) for s in range(1, SSD_CONV)]
    conv = cb + cw[SSD_CONV - 1:SSD_CONV, :] * u
    for s in range(1, SSD_CONV):
        conv = conv + cw[SSD_CONV - 1 - s:SSD_CONV - s, :] * delayed[s]
    return conv, delayed


def _pair_lanes(v, h0, first):
    return jnp.where(first, v[:, h0:h0 + 1], v[:, h0 + 1:h0 + 2])


def _ssd_common(u, prev, dt_raw, cw, cb, dtb, alog):
    conv, delayed = _ssd_conv(u, prev, cw, cb)
    sg = _sigmoid(conv)
    xa = conv * sg
    dpre = dt_raw + dtb
    dtv = jnp.maximum(dpre, 0.0) + jnp.log1p(jnp.exp(-jnp.abs(dpre)))
    a_row = -jnp.exp(alog)
    cs = _cumsum_rows(dtv * a_row)
    return conv, sg, xa, dpre, dtv, a_row, cs, delayed


def _ssd_pair_fwd(xa, dtv, cs, csT, G, Cg, Bg, Sp, dsk, pp, first, tri, rowfirst):
    h0 = 2 * pp
    x = xa[:, pp * LANES:(pp + 1) * LANES]
    xdt = x * _pair_lanes(dtv, h0, first)
    xdt_b = xdt.astype(BF16)
    Ms, yd = [], []
    for h in (h0, h0 + 1):
        lam = jnp.exp(jnp.where(tri, cs[:, h:h + 1] - csT[h:h + 1, :], NEG))
        M = G * lam
        Ms.append((M, lam))
        yd.append(jnp.dot(M.astype(BF16), xdt_b, preferred_element_type=F32))
    T = lax.dot_general(Cg, Sp.astype(BF16), (((1,), (1,)), ((), ())), preferred_element_type=F32)
    E = jnp.exp(_pair_lanes(cs, h0, first))
    yoff = E * T
    csl = cs[L - 1:L, :]
    Fd = jnp.exp(_pair_lanes(csl, h0, first) - _pair_lanes(cs, h0, first))
    el = jnp.exp(csl)
    el_rows = jnp.where(rowfirst, el[:, h0:h0 + 1], el[:, h0 + 1:h0 + 2])
    Sloc = lax.dot_general((xdt * Fd).astype(BF16), Bg, (((0,), (0,)), ((), ())), preferred_element_type=F32)
    S_new = el_rows * Sp + Sloc
    y = jnp.where(first, yd[0], yd[1]) + yoff + x * _pair_lanes(dsk, h0, first[:1])
    return y, S_new, (x, xdt, xdt_b, Ms, E, yoff, Fd, el, el_rows)


def _ssd_masks():
    lane = lax.broadcasted_iota(jnp.int32, (L, LANES), 1)
    row = lax.broadcasted_iota(jnp.int32, (L, LANES), 0)
    return lane, row, lane < SSD_HEAD_DIM, row >= lane, row[:, :1] < SSD_HEAD_DIM


def _ssd_specs(nc, rev):
    def cidx(i):
        return nc - 1 - i if rev else i
    z = pl.BlockSpec((L, SSD_INNER), lambda i: (cidx(i), PZ // SSD_INNER))
    u = pl.BlockSpec((L, SSD_XBC), lambda i: (cidx(i), PX // SSD_XBC))
    dt = pl.BlockSpec((L, LANES), lambda i: (cidx(i), PDT // LANES))
    return cidx, z, u, dt


def _vec(w):
    return pl.BlockSpec((1, w), lambda i: (0, 0))


def _ssd_fwd(proj, cw, cb, dtb, alog, dsk, ng):
    S = proj.shape[0]
    nc = S // L

    def body(z_ref, u_ref, dt_ref, cw_ref, cb_ref, dtb_ref, alog_ref, dsk_ref, ng_ref, y_ref, st_ref, prev_s, state_s):
        c = pl.program_id(0)

        @pl.when(c == 0)
        def _():
            prev_s[...] = jnp.zeros(prev_s.shape, F32)
            state_s[...] = jnp.zeros(state_s.shape, F32)

        u = u_ref[...]
        _, _, xa, _, dtv, _, cs, _ = _ssd_common(u, prev_s[...], dt_ref[...], cw_ref[...], cb_ref[...], dtb_ref[...], alog_ref[...])
        prev_s[...] = u
        csT = cs.T
        _, _, first, tri, rowfirst = _ssd_masks()
        dsk_v = dsk_ref[...]
        ys = []
        for g in range(2):
            Bg = xa[:, XB0 + g * SSD_STATE:XB0 + (g + 1) * SSD_STATE].astype(BF16)
            Cg = xa[:, XC0 + g * SSD_STATE:XC0 + (g + 1) * SSD_STATE].astype(BF16)
            G = lax.dot_general(Cg, Bg, (((1,), (1,)), ((), ())), preferred_element_type=F32)
            for pp in (2 * g, 2 * g + 1):
                Sp = state_s[pp]
                st_ref[pp * LANES:(pp + 1) * LANES, :] = Sp
                y, S_new, _ = _ssd_pair_fwd(xa, dtv, cs, csT, G, Cg, Bg, Sp, dsk_v, pp, first, tri, rowfirst)
                state_s[pp] = S_new
                ys.append(y)
        z = z_ref[...]
        for g in range(2):
            yg = jnp.concatenate([ys[2 * g], ys[2 * g + 1]], axis=1)
            zg = z[:, g * GRP_W:(g + 1) * GRP_W]
            gated = yg * (zg * _sigmoid(zg))
            r = lax.rsqrt(jnp.mean(gated * gated, axis=-1, keepdims=True) + RMS_EPS)
            y_ref[:, g * GRP_W:(g + 1) * GRP_W] = (gated * r * ng_ref[:, g * GRP_W:(g + 1) * GRP_W]).astype(BF16)

    _, zs, us, dts = _ssd_specs(nc, False)
    return pl.pallas_call(
        body, name="ssd_fwd", grid=(nc,),
        in_specs=[zs, us, dts, pl.BlockSpec((8, SSD_XBC), lambda i: (0, 0)), _vec(SSD_XBC), _vec(LANES), _vec(LANES), _vec(LANES),
                  _vec(SSD_INNER)],
        out_specs=[pl.BlockSpec((L, SSD_INNER), lambda i: (i, 0)), pl.BlockSpec((N_SPAIR * LANES, SSD_STATE), lambda i: (i, 0))],
        out_shape=[jax.ShapeDtypeStruct((S, 2 * SSD_INNER), BF16), jax.ShapeDtypeStruct((nc * N_SPAIR * LANES, SSD_STATE), F32)],
        scratch_shapes=[pltpu.VMEM((L, SSD_XBC), F32), pltpu.VMEM((N_SPAIR, LANES, SSD_STATE), F32)],
        compiler_params=_cparams(("arbitrary",)),
    )(proj, proj, proj, cw, cb, dtb, alog, dsk, ng)


def _ssd_bwd(proj, states, dy, cw, cb, dtb, alog, dsk, ng):
    S = proj.shape[0]
    nc = S // L

    def body(z_ref, u_ref, up_ref, dt_ref, st_ref, dy_ref, cw_ref, cb_ref, dtb_ref, alog_ref, dsk_ref, ng_ref,
             dz_ref, du_ref, ddt_ref, dcw_ref, dcb_ref, ddtb_ref, dalog_ref, ddsk_ref, dng_ref,
             dS_s, dconv_s, dD_s):
        i = pl.program_id(0)
        c = nc - 1 - i

        @pl.when(i == 0)
        def _():
            dS_s[...] = jnp.zeros(dS_s.shape, F32)
            dconv_s[...] = jnp.zeros(dconv_s.shape, F32)
            dD_s[...] = jnp.zeros(dD_s.shape, F32)
            for r in (dcw_ref, dcb_ref, ddtb_ref, dalog_ref, ddsk_ref, dng_ref):
                r[...] = jnp.zeros(r.shape, F32)

        u = u_ref[...]
        prev = jnp.where(c > 0, up_ref[...], 0.0)
        cw_v = cw_ref[...]
        conv, sg, xa, dpre, dtv, a_row, cs, delayed = _ssd_common(u, prev, dt_ref[...], cw_v, cb_ref[...], dtb_ref[...], alog_ref[...])
        csT = cs.T
        lane, row, first, tri, rowfirst = _ssd_masks()
        dsk_v = dsk_ref[...]

        fw = []
        Gs, Bs, Cs = [], [], []
        for g in range(2):
            Bg = xa[:, XB0 + g * SSD_STATE:XB0 + (g + 1) * SSD_STATE].astype(BF16)
            Cg = xa[:, XC0 + g * SSD_STATE:XC0 + (g + 1) * SSD_STATE].astype(BF16)
            G = lax.dot_general(Cg, Bg, (((1,), (1,)), ((), ())), preferred_element_type=F32)
            Gs.append(G), Bs.append(Bg), Cs.append(Cg)
            for pp in (2 * g, 2 * g + 1):
                Sp = st_ref[pp * LANES:(pp + 1) * LANES, :]
                y, _, keep = _ssd_pair_fwd(xa, dtv, cs, csT, G, Cg, Bg, Sp, dsk_v, pp, first, tri, rowfirst)
                fw.append((y, Sp, keep))

        z = z_ref[...]
        dys = []
        for g in range(2):
            sl = slice(g * GRP_W, (g + 1) * GRP_W)
            yg = jnp.concatenate([fw[2 * g][0], fw[2 * g + 1][0]], axis=1)
            zg = z[:, sl]
            sz = _sigmoid(zg)
            silu_z = zg * sz
            gated = yg * silu_z
            r = lax.rsqrt(jnp.mean(gated * gated, axis=-1, keepdims=True) + RMS_EPS)
            nh = gated * r
            dout = dy_ref[:, sl]
            dng_ref[:, sl] += jnp.sum(dout * nh, axis=0, keepdims=True)
            dnh = dout * ng_ref[:, sl]
            dgated = r * (dnh - nh * jnp.mean(dnh * nh, axis=-1, keepdims=True))
            dz_ref[:, sl] = dgated * yg * (sz * (1.0 + zg * (1.0 - sz)))
            dyg = dgated * silu_z
            dys.append(dyg[:, :LANES]), dys.append(dyg[:, LANES:])

        dcs_col = [0.0] * SSD_HEADS
        dcs_row = [None] * SSD_HEADS
        ddt_col = [None] * SSD_HEADS
        dxs = []
        dB, dC = [None, None], [None, None]
        last = row == L - 1
        for g in range(2):
            Bg, Cg, G = Bs[g], Cs[g], Gs[g]
            dG = jnp.zeros((L, L), F32)
            dBg = jnp.zeros((L, SSD_STATE), F32)
            dCg = jnp.zeros((L, SSD_STATE), F32)
            for pp in (2 * g, 2 * g + 1):
                h0 = 2 * pp
                y, Sp, (x, xdt, xdt_b, Ms, E, yoff, Fd, el, el_rows) = fw[pp]
                dY = dys[pp]
                dS = dS_s[pp]
                dS_b, Sp_b = dS.astype(BF16), Sp.astype(BF16)
                dD_s[:, pp * LANES:(pp + 1) * LANES] += jnp.sum(dY * x, axis=0, keepdims=True)
                dx = dY * _pair_lanes(dsk_v, h0, first[:1])
                dxdt = jnp.zeros((L, LANES), F32)
                dY_b = dY.astype(BF16)
                for hh, h in enumerate((h0, h0 + 1)):
                    hm = first if hh == 0 else ~first
                    M, lam = Ms[hh]
                    dYh = jnp.where(hm, dY, 0.0).astype(BF16)
                    dM = lax.dot_general(dYh, xdt_b, (((1,), (1,)), ((), ())), preferred_element_type=F32)
                    W = dM * M
                    dcs_col[h] = dcs_col[h] + jnp.sum(W, axis=-1, keepdims=True)
                    dcs_row[h] = jnp.sum(W, axis=0, keepdims=True)
                    dG = dG + dM * lam
                    mt = lax.dot_general(M.astype(BF16), dY_b, (((0,), (0,)), ((), ())), preferred_element_type=F32)
                    dxdt = dxdt + jnp.where(hm, mt, 0.0)
                dT = (E * dY).astype(BF16)
                dCg = dCg + jnp.dot(dT, Sp_b, preferred_element_type=F32)
                dS_in = lax.dot_general(dT, Cg, (((0,), (0,)), ((), ())), preferred_element_type=F32) + el_rows * dS
                q1 = dY * yoff
                dZ = lax.dot_general(Bg, dS_b, (((1,), (1,)), ((), ())), preferred_element_type=F32)
                dBg = dBg + jnp.dot((xdt * Fd).astype(BF16), dS_b, preferred_element_type=F32)
                dxdt = dxdt + dZ * Fd
                q2 = dZ * xdt * Fd
                dSS = dS * Sp
                for hh, h in enumerate((h0, h0 + 1)):
                    hm = first if hh == 0 else ~first
                    rs1 = jnp.sum(jnp.where(hm, q1, 0.0), axis=-1, keepdims=True)
                    rs2 = jnp.sum(jnp.where(hm, q2, 0.0), axis=-1, keepdims=True)
                    rmask = rowfirst if hh == 0 else ~rowfirst
                    d_el = jnp.sum(jnp.sum(jnp.where(rmask, dSS, 0.0), axis=-1, keepdims=True), axis=0, keepdims=True)
                    tail = jnp.sum(rs2, axis=0, keepdims=True) + d_el * el[:, h:h + 1]
                    dcs_col[h] = dcs_col[h] + (rs1 - rs2 + jnp.where(last[:, :1], tail, 0.0))
                    ddt_col[h] = jnp.sum(jnp.where(hm, dxdt * x, 0.0), axis=-1, keepdims=True)
                dS_s[pp] = dS_in
                dxs.append(dx + dxdt * _pair_lanes(dtv, h0, first))
            dG_b = dG.astype(BF16)
            dC[g] = dCg + jnp.dot(dG_b, Bg, preferred_element_type=F32)
            dB[g] = dBg + lax.dot_general(dG_b, Cg, (((0,), (0,)), ((), ())), preferred_element_type=F32)

        dcs_c, dcs_r, ddt_c = (jnp.zeros((L, LANES), F32) for _ in range(3))
        for h in range(SSD_HEADS):
            dcs_c = dcs_c + jnp.where(lane == h, dcs_col[h], 0.0)
            dcs_r = dcs_r + jnp.where(row == h, dcs_row[h], 0.0)
            ddt_c = ddt_c + jnp.where(lane == h, ddt_col[h], 0.0)
        dcs = dcs_c - dcs_r.T
        da = _cumsum_rows(dcs, reverse=True)
        ddt_c = ddt_c + da * a_row
        dalog_ref[...] += jnp.sum(da * dtv, axis=0, keepdims=True) * a_row
        ddt_raw = ddt_c * _sigmoid(dpre)
        ddt_ref[...] = ddt_raw
        ddtb_ref[...] += jnp.sum(ddt_raw, axis=0, keepdims=True)

        dxa = jnp.concatenate(dxs + dB + dC, axis=1)
        dconv = dxa * (sg * (1.0 + conv * (1.0 - sg)))
        dcb_ref[...] += jnp.sum(dconv, axis=0, keepdims=True)
        nxt = dconv_s[...]
        du = cw_v[SSD_CONV - 1:SSD_CONV, :] * dconv
        dcw_ref[SSD_CONV - 1:SSD_CONV, :] += jnp.sum(dconv * u, axis=0, keepdims=True)
        for s in range(1, SSD_CONV):
            k = SSD_CONV - 1 - s
            du = du + cw_v[k:k + 1, :] * _shift_up(dconv, nxt, s)
            dcw_ref[k:k + 1, :] += jnp.sum(dconv * delayed[s], axis=0, keepdims=True)
        du_ref[...] = du
        dconv_s[...] = dconv

        @pl.when(i == nc - 1)
        def _():
            acc = dD_s[...]
            lane1 = lax.broadcasted_iota(jnp.int32, (1, LANES), 1)
            lanew = lax.broadcasted_iota(jnp.int32, acc.shape, 1)
            out = jnp.zeros((1, LANES), F32)
            for h in range(SSD_HEADS):
                tot = jnp.sum(jnp.where((lanew >= h * SSD_HEAD_DIM) & (lanew < (h + 1) * SSD_HEAD_DIM), acc, 0.0),
                              axis=-1, keepdims=True)
                out = out + jnp.where(lane1 == h, tot, 0.0)
            ddsk_ref[...] = out

    cidx, zs, us, dts = _ssd_specs(nc, True)
    ups = pl.BlockSpec((L, SSD_XBC), lambda i: (jnp.maximum(cidx(i) - 1, 0), PX // SSD_XBC))
    rowc = lambda w: pl.BlockSpec((L, w), lambda i: (cidx(i), 0))
    return pl.pallas_call(
        body, name="ssd_bwd", grid=(nc,),
        in_specs=[zs, us, ups, dts, pl.BlockSpec((N_SPAIR * LANES, SSD_STATE), lambda i: (cidx(i), 0)), rowc(SSD_INNER),
                  pl.BlockSpec((8, SSD_XBC), lambda i: (0, 0)), _vec(SSD_XBC), _vec(LANES), _vec(LANES), _vec(LANES), _vec(SSD_INNER)],
        out_specs=[rowc(SSD_INNER), rowc(SSD_XBC), rowc(LANES), pl.BlockSpec((8, SSD_XBC), lambda i: (0, 0)), _vec(SSD_XBC),
                   _vec(LANES), _vec(LANES), _vec(LANES), _vec(SSD_INNER)],
        out_shape=[jax.ShapeDtypeStruct((S, SSD_INNER), F32), jax.ShapeDtypeStruct((S, SSD_XBC), F32),
                   jax.ShapeDtypeStruct((S, LANES), F32), jax.ShapeDtypeStruct((8, SSD_XBC), F32),
                   jax.ShapeDtypeStruct((1, SSD_XBC), F32), jax.ShapeDtypeStruct((1, LANES), F32),
                   jax.ShapeDtypeStruct((1, LANES), F32), jax.ShapeDtypeStruct((1, LANES), F32),
                   jax.ShapeDtypeStruct((1, SSD_INNER), F32)],
        scratch_shapes=[pltpu.VMEM((N_SPAIR, LANES, SSD_STATE), F32), pltpu.VMEM((L, SSD_XBC), F32),
                        pltpu.VMEM((1, SSD_INNER), F32)],
        compiler_params=_cparams(("arbitrary",)),
    )(proj, proj, proj, proj, states, dy, cw, cb, dtb, alog, dsk, ng)


_IN_SEGS = ((PZ, 0, 512), (PX, 512, 1024), (PDT, 1536, 8), (PQ, 1544, 384), (PKV, 1928, 256), (PKR + KR_LANE, 2184, 32))


def _pad_w_in(w):
    parts, at = [], 0
    for dst, src, n in sorted(_IN_SEGS):
        parts += [jnp.zeros((w.shape[0], dst - at), w.dtype), w[:, src:src + n]]
        at = dst + n
    return jnp.concatenate(parts + [jnp.zeros((w.shape[0], PW - at), w.dtype)], axis=1)


def _unpad_w_in(wp):
    segs = sorted(_IN_SEGS, key=lambda t: t[1])
    return jnp.concatenate([wp[:, dst:dst + n] for dst, src, n in segs], axis=1)


def _pad_w_q(w):
    return jnp.pad(w.reshape(MLA_Q_RANK, MLA_HEADS, MLA_QK), ((0, 0), (0, 0), (0, LANES - MLA_QK))).reshape(MLA_Q_RANK, MLA_HEADS * LANES)


def _unpad_w_q(wp):
    return wp.reshape(MLA_Q_RANK, MLA_HEADS, LANES)[:, :, :MLA_QK].reshape(MLA_Q_RANK, MLA_HEADS * MLA_QK)


def _pad_w_kv(w):
    w3 = w.reshape(MLA_KV_RANK, MLA_HEADS, MLA_NOPE + MLA_V)
    k = jnp.pad(w3[:, :, :MLA_NOPE], ((0, 0), (0, 0), (0, LANES - MLA_NOPE))).reshape(MLA_KV_RANK, MLA_HEADS * LANES)
    return jnp.concatenate([k, w3[:, :, MLA_NOPE:].reshape(MLA_KV_RANK, MLA_HEADS * MLA_V)], axis=1)


def _unpad_w_kv(wp):
    k = wp[:, :MLA_HEADS * LANES].reshape(MLA_KV_RANK, MLA_HEADS, LANES)[:, :, :MLA_NOPE]
    v = wp[:, MLA_HEADS * LANES:].reshape(MLA_KV_RANK, MLA_HEADS, MLA_V)
    return jnp.concatenate([k, v], axis=2).reshape(MLA_KV_RANK, MLA_HEADS * (MLA_NOPE + MLA_V))


def _head_lanes(v):
    return jnp.pad(v, ((0, 0), (0, LANES - v.shape[1])))


def _local_step(x, mem, positions, tgt, P, weights_at=None, emit=None):
    tabs = _rope_tables(positions)
    G = {}
    emit = emit or (lambda names, grads: 0.0)

    h0, h0b = _ln_fwd([(x, 1.0)], P["ln_in_g"], P["ln_in_b"], "ln_in")
    if weights_at is not None:
        P = {**P, **weights_at("first", (h0b,) + tuple(tabs))}
    proj = _mm(h0b, P["w_in"], "nn", "proj_in", tm=1024, tn=640)
    if weights_at is not None:
        P = {**P, **weights_at("mid", proj)}
    cat, states = _ssd_fwd(proj, P["conv_w"], P["conv_b"], P["dt_bias"], P["a_log"], P["d_skip"], P["ssd_norm_g"])
    qn, kvn, kpe = _mla_prep(proj, P["q_norm_g"], P["kv_norm_g"], tabs)

    def q_epi(acc, c, s1, s2):
        return (jnp.concatenate([_rope_block(acc[:, h * LANES:(h + 1) * LANES], c, s1, s2) for h in range(MLA_HEADS)], axis=1)
                * Q_PRESCALE,)

    q_all = _mm(qn, P["w_q_up"], "nn", "q_up", out_dtypes=(BF16,), epi=q_epi, extras=[(t, "m") for t in tabs])

    def kv_epi(acc, kp):
        kb = [acc[:, h * LANES:(h + 1) * LANES] + kp for h in range(MLA_HEADS)]
        return (jnp.concatenate(kb + [acc[:, MLA_HEADS * LANES:]], axis=1),)

    kv_all = _mm(kvn, P["w_kv_up"], "nn", "kv_up", out_dtypes=(BF16,), epi=kv_epi, extras=[(kpe, "m")])
    o_att, lse, cat = _attn_fwd(q_all, kv_all, cat)
    def resid_epi(acc, h):
        return (ALPHA * h + acc,)

    def resid_ln_epi(acc, h, g, b):
        r = ALPHA * h + acc
        y = _ln_stats(r)[0] * g + b
        return r, y, y

    ln_out = (F32, F32, BF16)
    r1, h1, h1b = _mm(cat, P["w_mix_out"], "nn", "mix_out", tm=512, out_dtypes=ln_out, epi=resid_ln_epi,
                      extras=[(h0, "mn"), (P["ln1_g"], "n"), (P["ln1_b"], "n")])
    if weights_at is not None:
        P = {**P, **weights_at("late", h1b)}
    qm = _mm(h1b, P["w_mem_q"], "nn", "mem_q", tm=1024, out_dtypes=(BF16,))
    km = _mm(mem, P["w_mem_k"], "nn", "mem_k", out_dtypes=(BF16,))
    vm = _mm(mem, P["w_mem_v"], "nn", "mem_v", out_dtypes=(BF16,))
    om = _mem_attn_fwd(qm, km, vm)
    r2, h2, h2b = _mm(om, P["w_mem_o"], "nn", "mem_o", tm=512, out_dtypes=ln_out, epi=resid_ln_epi,
                      extras=[(h1, "mn"), (P["ln2_g"], "n"), (P["ln2_b"], "n")])

    def up_epi(acc):
        r = jnp.maximum(acc, 0.0)
        return r * r, 2.0 * r

    act, dact = _mm(h2b, P["w_up"], "nn", "mlp_up", tm=1024, tn=1024, out_dtypes=(BF16, BF16), epi=up_epi)
    r3 = _mm(act, P["w_down"], "nn", "mlp_down", tm=1024, tk=D_FF, epi=resid_epi, extras=[(h2, "mn")])

    loss, dr3, dr3b, G["ln3_g"], G["ln3_b"] = _ln_loss_bwd([(r3, 1.0)], P["ln3_g"], P["ln3_b"], tgt, "ln3_loss")

    def dact_epi(acc, g):
        return (acc.astype(BF16) * g,)

    du = _mm(dr3b, P["w_down"], "nt", "mlp_down_dx", tm=1024, tn=1024, out_dtypes=(BF16,), epi=dact_epi, extras=[(dact, "mn")])
    G["w_down"] = _mm(act, dr3b, "tn", "mlp_down_dw", tm=1024, tk=TOK_K, out_dtypes=(BF16,))
    G["w_up"] = _mm(h2b, du, "tn", "mlp_up_dw", tm=1024, tn=D_FF // N_DEV, tk=TOK_K, out_dtypes=(BF16,), col_slots=True)
    tie = emit(("w_down", "w_up"), G)
    dh2 = _mm(du, P["w_up"], "nt", "mlp_up_dx", tm=1024, tk=D_FF, epi=resid_epi, extras=[(dr3, "mn")])
    dr2, dr2b, G["ln2_g"], G["ln2_b"] = _ln_bwd([(r2, 1.0)], [(dh2, 1.0)], P["ln2_g"] + tie, "ln2_bwd")

    dom = _mm(dr2b, P["w_mem_o"], "nt", "mem_o_dx", tm=1024, out_dtypes=(BF16,))
    G["w_mem_o"] = _mm(om, dr2b, "tn", "mem_o_dw", tm=1024, tk=TOK_K, out_dtypes=(BF16,))
    dqm, dkm, dvm = _mem_attn_bwd(qm, km, vm, dom)
    G["w_mem_q"] = _mm(h1b, dqm, "tn", "mem_q_dw", tm=1024, tk=TOK_K, out_dtypes=(BF16,))
    G["w_mem_k"] = _mm(mem, dkm, "tn", "mem_k_dw", tm=1024, out_dtypes=(BF16,))
    G["w_mem_v"] = _mm(mem, dvm, "tn", "mem_v_dw", tm=1024, out_dtypes=(BF16,))
    tie = emit(("w_mem_o", "w_mem_q", "w_mem_k", "w_mem_v"), G)
    dh1 = _mm(dqm, P["w_mem_q"], "nt", "mem_q_dx", tm=1024, epi=resid_epi, extras=[(dr2, "mn")])
    dr1, dr1b, G["ln1_g"], G["ln1_b"] = _ln_bwd([(r1, 1.0)], [(dh1, 1.0)], P["ln1_g"] + tie, "ln1_bwd")

    def dcat_epi(acc, o):
        return acc, _attn_delta(acc[:, SSD_INNER:], o)

    dcat, delta = _mm(dr1b, P["w_mix_out"], "nt", "mix_out_dx", tm=512, out_dtypes=(F32, F32), epi=dcat_epi, extras=[(o_att, "m")])
    G["w_mix_out"] = _mm(cat, dr1b, "tn", "mix_out_dw", tm=1024, tk=TOK_K, out_dtypes=(BF16,))
    dq_all, dk_all, dv_all = _attn_bwd(q_all, kv_all, dcat, lse, delta)
    dq_pre = _rope_bwd_all(dq_all, tabs, DQ_POSTSCALE)
    G["w_q_up"] = _mm(qn, dq_pre, "tn", "q_up_dw", tk=TOK_K, out_dtypes=(BF16,))
    dqn = _mm(dq_pre, P["w_q_up"], "nt", "q_up_dx", tm=1024)
    dkv_all = jnp.concatenate([dk_all, dv_all], axis=1).astype(BF16)
    G["w_kv_up"] = _mm(kvn, dkv_all, "tn", "kv_up_dw", tk=TOK_K, out_dtypes=(BF16,))
    dkvn = _mm(dkv_all, P["w_kv_up"], "nt", "kv_up_dx", tm=1024)
    dql, dkvl, dkr, G["q_norm_g"], G["kv_norm_g"] = _mla_prep_bwd(proj, P["q_norm_g"], P["kv_norm_g"], tabs, dqn, dkvn, dk_all)
    (dz, dxbc, ddt, G["conv_w"], G["conv_b"], G["dt_bias"], G["a_log"], G["d_skip"], G["ssd_norm_g"]) = _ssd_bwd(
        proj, states, dcat, P["conv_w"], P["conv_b"], P["dt_bias"], P["a_log"], P["d_skip"], P["ssd_norm_g"])
    tie = emit(("w_mix_out", "w_q_up", "w_kv_up", "conv_w"), G)
    S = x.shape[0]
    dproj = jnp.concatenate([dql, jnp.zeros((S, PZ - MLA_Q_RANK), F32) + tie, dz, dxbc, dkvl, ddt, dkr], axis=1).astype(BF16)
    G["w_in"] = _mm(h0b, dproj, "tn", "proj_in_dw", tm=1024, tn=640, tk=TOK_K, out_dtypes=(BF16,))
    tie = emit(("w_in",), G)
    dh0 = _mm(dproj, P["w_in"], "nt", "proj_in_dx", tm=1024, tk=PW, epi=lambda acc, t, d: (acc + t + ALPHA * d,),
              extras=[(jnp.zeros((1, D_MODEL), F32) + tie, "n"), (dr1, "mn")])
    gx, _, G["ln_in_g"], G["ln_in_b"] = _ln_bwd([(x, 1.0)], [(dh0, 1.0)], P["ln_in_g"] + tie, "ln_in_bwd")
    return loss, gx, G


PACK_W = 1024
BIG = (("w_in", (1024, 277), 1), ("conv_w", (4, 128), 1), ("w_q_up", (384, 96), 1), ("w_kv_up", (256, 128), 1),
       ("w_mix_out", (128, 1024), 0), ("w_mem_q", (128, 1024), 0), ("w_mem_k", (128, 1024), 0), ("w_mem_v", (128, 1024), 0),
       ("w_mem_o", (128, 1024), 0), ("w_up", (1024, 512), 1), ("w_down", (512, 1024), 0))
SMALL = ("ln_in_g", "ln_in_b", "conv_b", "ln1_g", "ln1_b", "ln2_g", "ln2_b", "ln3_g", "ln3_b",
         "ssd_norm_g", "q_norm_g", "kv_norm_g", "dt_bias", "a_log", "d_skip")
ALL_W = ("ln_in_g", "ln_in_b", "w_in", "conv_w", "conv_b", "dt_bias", "a_log", "d_skip", "ssd_norm_g", "q_norm_g", "w_q_up",
         "kv_norm_g", "w_kv_up", "w_mix_out", "ln1_g", "ln1_b", "w_mem_q", "w_mem_k", "w_mem_v", "w_mem_o", "ln2_g", "ln2_b",
         "w_up", "w_down", "ln3_g", "ln3_b")


SMALL_R = 16
LOSS_ROW = 15
_SMALL_ROWS = (("ln_in_g",), ("ln_in_b",), ("conv_b",), ("ln1_g",), ("ln1_b",), ("ln2_g",), ("ln2_b",), ("ln3_g",), ("ln3_b",),
               ("ssd_norm_g", "q_norm_g"), ("kv_norm_g", "dt_bias", "a_log", "d_skip"))
_SMALL_W = {"ssd_norm_g": 512, "q_norm_g": 384, "kv_norm_g": 256, "dt_bias": LANES, "a_log": LANES, "d_skip": LANES}


MESH = pl.DeviceIdType.MESH
ANY = pl.BlockSpec(memory_space=pl.ANY)
VM = pl.BlockSpec(memory_space=pltpu.VMEM)


def _coords():
    return lax.axis_index("x"), lax.axis_index("y"), lax.axis_index("c")


def _slot(px, py, pc):
    return 4 * px + 2 * py + pc


def _peer(k, x, y, c):
    dx, dy, dc = (k >> 2) & 1, (k >> 1) & 1, k & 1
    return (1 - x if dx else x, 1 - y if dy else y, 1 - c if dc else c)


def _adam(w, g, m, v):
    m = ADAM_B1 * m + (1.0 - ADAM_B1) * g
    v = ADAM_B2 * v + (1.0 - ADAM_B2) * (g * g)
    m_hat = m / (1.0 - ADAM_B1 ** ADAM_STEP)
    v_hat = v / (1.0 - ADAM_B2 ** ADAM_STEP)
    delta = -ADAM_LR * (m_hat / (jnp.sqrt(v_hat) + ADAM_EPS) + ADAM_WD * w)
    return delta, m, v


def _sum_slots(ref):
    tot = ref[0].astype(F32)
    for q in range(1, N_DEV):
        tot = tot + ref[q].astype(F32)
    return tot


def _row(v):
    return v.reshape(1, -1).astype(F32)


def kernel(x, mem, positions, ln_in_g, ln_in_b, w_in, conv_w, conv_b, dt_bias, a_log, d_skip, ssd_norm_g, q_norm_g, w_q_up, kv_norm_g, w_kv_up, w_mix_out, ln1_g, ln1_b, w_mem_q, w_mem_k, w_mem_v, w_mem_o, ln2_g, ln2_b, w_up, w_down, ln3_g, ln3_b, loss_target, m_ln_in_g, m_ln_in_b, m_w_in, m_conv_w, m_conv_b, m_dt_bias, m_a_log, m_d_skip, m_ssd_norm_g, m_q_norm_g, m_w_q_up, m_kv_norm_g, m_w_kv_up, m_w_mix_out, m_ln1_g, m_ln1_b, m_w_mem_q, m_w_mem_k, m_w_mem_v, m_w_mem_o, m_ln2_g, m_ln2_b, m_w_up, m_w_down, m_ln3_g, m_ln3_b, v_ln_in_g, v_ln_in_b, v_w_in, v_conv_w, v_conv_b, v_dt_bias, v_a_log, v_d_skip, v_ssd_norm_g, v_q_norm_g, v_w_q_up, v_kv_norm_g, v_w_kv_up, v_w_mix_out, v_ln1_g, v_ln1_b, v_w_mem_q, v_w_mem_k, v_w_mem_v, v_w_mem_o, v_ln2_g, v_ln2_b, v_w_up, v_w_down, v_ln3_g, v_ln3_b):
    a = dict(locals())
    W = {n: a[n] for n in ALL_W}
    M = {n: a["m_" + n] for n in ALL_W}
    V = {n: a["v_" + n] for n in ALL_W}
    return _step_overlapped(x, mem, positions, loss_target, W, M, V)


HBM = pl.BlockSpec(memory_space=pltpu.HBM)
SEM = pl.BlockSpec(memory_space=pltpu.SEMAPHORE)
EFFECT = pltpu.SideEffectType.DATAFLOW_SIDE_EFFECTING
SHARD_SHAPE = {n: s for n, s, _ in BIG}
SHARD_AXIS = {n: ax for n, _, ax in BIG}
GATHER_FIRST = ("w_in",)
GATHER_MID = ("conv_w", "w_q_up", "w_kv_up", "w_mix_out")
GATHER_LATE = ("w_mem_q", "w_mem_k", "w_mem_v", "w_mem_o", "w_up", "w_down")


def _my_slot():
    return _slot(*_coords())


def _group_copies(src_refs, land_refs, send_sems, recv_sems, slotted, landing_of_peer):
    x, y, c = _coords()
    my = _slot(x, y, c)
    cps = []
    for a, (s_ref, l_ref) in enumerate(zip(src_refs, land_refs)):
        for k in range(1, N_DEV):
            peer = _peer(k, x, y, c)
            cps.append(pltpu.make_async_remote_copy(
                src_ref=s_ref.at[_slot(*peer)] if slotted else s_ref,
                dst_ref=l_ref.at[_slot(*peer)] if landing_of_peer else l_ref.at[my],
                send_sem=send_sems.at[7 * a + k - 1], recv_sem=recv_sems.at[7 * a + k - 1],
                device_id=peer, device_id_type=MESH))
    return cps


def _send_start(srcs, lands, slotted, name):
    n = len(srcs)

    def body(*refs):
        for cp in _group_copies(refs[:n], refs[n:2 * n], refs[2 * n], refs[2 * n + 1], slotted, False):
            cp.start()
        refs[-1][...] = jnp.zeros(refs[-1].shape, F32)

    res = pl.pallas_call(
        body, name=name,
        out_shape=(pltpu.SemaphoreType.DMA((7 * n,)), pltpu.SemaphoreType.DMA((7 * n,)),
                   *[pltpu.HBM(a.shape, a.dtype) for a in srcs], *[pltpu.HBM(a.shape, a.dtype) for a in lands],
                   jax.ShapeDtypeStruct((8, LANES), F32)),
        in_specs=[HBM] * (2 * n), out_specs=(SEM, SEM, *[HBM] * (2 * n), VM),
        input_output_aliases={i: 2 + i for i in range(2 * n)},
        compiler_params=pltpu.CompilerParams(has_side_effects=EFFECT),
    )(*[pltpu.with_memory_space_constraint(a, pltpu.HBM) for a in list(srcs) + list(lands)])
    return (res[0], res[1], res[2:2 + n], res[2 + n:2 + 2 * n]), res[-1][:1, :1]


def _send_wait(started, after, slotted, name):
    send_sems, recv_sems, srcs, lands = started
    n = len(srcs)
    after = list(after) if isinstance(after, (list, tuple)) else [after]

    def body(*refs):
        for cp in _group_copies(refs[:n], refs[n:2 * n], refs[2 * n], refs[2 * n + 1], slotted, True):
            cp.wait_send()
            cp.wait_recv()

    res = pl.pallas_call(
        body, name=name, out_shape=tuple(pltpu.HBM(a.shape, a.dtype) for a in list(srcs) + list(lands)),
        in_specs=[HBM] * (2 * n) + [SEM, SEM] + [ANY] * len(after), out_specs=tuple([HBM] * (2 * n)),
        input_output_aliases={i: i for i in range(2 * n)},
        compiler_params=pltpu.CompilerParams(has_side_effects=EFFECT),
    )(*srcs, *lands, send_sems, recv_sems, *after)
    return res[n:]


def _landing(own, my):
    return lax.dynamic_update_slice(lax.empty((N_DEV,) + own.shape, own.dtype), own[None], (my,) + (0,) * own.ndim)


def _full_from_slots(name, slots):
    a, b = SHARD_SHAPE[name]
    return slots.reshape(N_DEV * a, b) if SHARD_AXIS[name] == 0 else slots.transpose(1, 0, 2).reshape(a, N_DEV * b)


def _slots_from_full(name, g):
    a, b = SHARD_SHAPE[name]
    return g.reshape(N_DEV, a, b) if SHARD_AXIS[name] == 0 else g.reshape(a, N_DEV, b).transpose(1, 0, 2)


def _reduce_adam(recv, w, m, v, name):
    _, a, b = recv.shape
    ta = a
    while ta * b * 4 * N_DEV > 4 * 1024 * 1024 and ta % 16 == 0:
        ta //= 2

    def body(r_ref, w_ref, m_ref, v_ref, g_ref, d_ref, nm_ref, nv_ref):
        g = _sum_slots(r_ref)
        g_ref[...] = g
        d_ref[...], nm_ref[...], nv_ref[...] = _adam(w_ref[...], g, m_ref[...], v_ref[...])

    row = pl.BlockSpec((ta, b), lambda i: (i, 0))
    return pl.pallas_call(
        body, name=name, grid=(a // ta,),
        in_specs=[pl.BlockSpec((N_DEV, ta, b), lambda i: (0, i, 0)), row, row, row], out_specs=[row] * 4,
        out_shape=[jax.ShapeDtypeStruct((a, b), F32)] * 4, compiler_params=_cparams(("parallel",)),
    )(recv, w, m, v)


def _step_overlapped(x, mem, positions, tgt, W, M, V):
    my = _my_slot()
    shard = {n: W[n][0] for n, _, _ in BIG}
    send = {n: (shard[n] if n == "conv_w" else shard[n].astype(BF16)) for n in shard}

    first_src, mid_src, late_src = ([send[n] for n in grp] for grp in (GATHER_FIRST, GATHER_MID, GATHER_LATE))
    first, tie = _send_start(first_src, [_landing(s, my) for s in first_src], False, "gather_first_start")
    my_then = my + tie[0, 0].astype(jnp.int32)
    mid, tie = _send_start(mid_src, [_landing(s, my_then) for s in mid_src], False, "gather_mid_start")
    my_then = my + tie[0, 0].astype(jnp.int32)
    late, tie = _send_start(late_src, [_landing(s, my_then) for s in late_src], False, "gather_late_start")

    P = {n: _row(W[n]) for n in SMALL}
    for n in ("dt_bias", "a_log", "d_skip"):
        P[n] = _head_lanes(P[n])
    P["ln_in_g"] = P["ln_in_g"] + tie

    def weights_at(stage, after):
        if stage == "first":
            lands = _send_wait(first, after, False, "gather_first_wait")
            return dict(w_in=_pad_w_in(_full_from_slots("w_in", lands[0])))
        if stage == "mid":
            lands = _send_wait(mid, after, False, "gather_mid_wait")
            full = {n: _full_from_slots(n, l) for n, l in zip(GATHER_MID, lands)}
            return dict(w_q_up=_pad_w_q(full["w_q_up"]), w_kv_up=_pad_w_kv(full["w_kv_up"]), w_mix_out=full["w_mix_out"],
                        conv_w=jnp.pad(full["conv_w"], ((0, 8 - SSD_CONV), (0, 0))))
        lands = _send_wait(late, after, False, "gather_late_wait")
        return {n: _full_from_slots(n, l) for n, l in zip(GATHER_LATE, lands)}

    started, res = [], {}

    def finish(i, after):
        names, st = started[i]
        lands = _send_wait(st, after, True, "scatter_wait_%d" % i)
        for n, recv in zip(names, lands):
            res[n] = _reduce_adam(recv, shard[n], M[n][0], V[n][0], "reduce_adam_" + n)
        return sum(res[n][0][:1, :1] for n in names) * 0.0

    def emit(names, G):
        srcs = []
        for n in names:
            g = G[n]
            if n == "w_in":
                g = _unpad_w_in(g)
            elif n == "w_q_up":
                g = _unpad_w_q(g)
            elif n == "w_kv_up":
                g = _unpad_w_kv(g)
            elif n == "conv_w":
                g = g[:SSD_CONV]
            srcs.append(g if g.ndim == 3 else _slots_from_full(n, g))
        lands = [_landing(lax.dynamic_index_in_dim(s, my, 0, keepdims=False), my) for s in srcs]
        st, tie = _send_start(srcs, lands, True, "scatter_start_%d" % len(started))
        started.append((names, st))
        if len(started) == 3:
            tie = tie + finish(0, srcs[0]) + finish(1, srcs[0])
        if len(started) == 4:
            tie = tie + finish(2, srcs[0])
        return tie

    loss, gx, G = _local_step(x[0], mem[0], positions[0], tgt[0], P, weights_at, emit)
    finish(3, gx)

    small, loss_tot = _allreduce_adam_vectors(
        {n: G[n] for n in SMALL}, loss, {n: _row(W[n]) for n in SMALL}, {n: _row(M[n]) for n in SMALL}, {n: _row(V[n]) for n in SMALL})

    outs = []
    for j in range(4):
        for n in ALL_W:
            outs.append((res[n][j] if n in res else small[j][n]).reshape(W[n].shape))
    return (loss_tot[0, 0], gx[None], *outs)


def _vector_places():
    places = {}
    for r, names in enumerate(_SMALL_ROWS):
        c = 0
        for n in names:
            w = _SMALL_W.get(n, PACK_W)
            places[n] = (r, c, w, SSD_HEADS if n in ("dt_bias", "a_log", "d_skip") else w)
            c += w
    return places


def _allreduce_adam_vectors(grads, loss, Ws, Ms, Vs):
    places = _vector_places()
    ns = len(SMALL)

    def body(*refs):
        g_in, loss_in = refs[:ns], refs[ns]
        w_in, m_in, v_in = refs[ns + 1:2 * ns + 1], refs[2 * ns + 1:3 * ns + 1], refs[3 * ns + 1:4 * ns + 1]
        o = 4 * ns + 1
        outs = [refs[o + j * ns:o + (j + 1) * ns] for j in range(4)]
        loss_out, stage, land, send_sems, recv_sems = refs[o + 4 * ns:]
        stage[...] = jnp.zeros(stage.shape, F32)
        for i, n in enumerate(SMALL):
            r, c, w, _ = places[n]
            stage[r:r + 1, c:c + w] = g_in[i][...]
        stage[LOSS_ROW:LOSS_ROW + 1, 0:LANES] = loss_in[...]
        x, y, c_ = _coords()
        my = _slot(x, y, c_)
        cps = []
        for k in range(1, N_DEV):
            peer = _peer(k, x, y, c_)
            cps.append(pltpu.make_async_remote_copy(
                src_ref=stage, dst_ref=land.at[my], send_sem=send_sems.at[k - 1], recv_sem=recv_sems.at[k - 1],
                device_id=peer, device_id_type=MESH))
        for cp in cps:
            cp.start()
        land[my] = stage[...]
        for cp in cps:
            cp.wait_recv()
        for cp in cps:
            cp.wait_send()
        tot = _sum_slots(land)
        loss_out[...] = tot[LOSS_ROW:LOSS_ROW + 1, 0:LANES]
        for i, n in enumerate(SMALL):
            r, c, _, wt = places[n]
            g = tot[r:r + 1, c:c + wt]
            outs[0][i][...] = g
            outs[1][i][...], outs[2][i][...], outs[3][i][...] = _adam(w_in[i][...], g, m_in[i][...], v_in[i][...])

    shapes = [jax.ShapeDtypeStruct((1, places[n][3]), F32) for n in SMALL]
    res = pl.pallas_call(
        body, name="allreduce_vectors", in_specs=[VM] * (4 * ns + 1), out_specs=[VM] * (4 * ns + 1),
        out_shape=shapes * 4 + [jax.ShapeDtypeStruct((1, LANES), F32)],
        scratch_shapes=[pltpu.VMEM((SMALL_R, PACK_W), F32), pltpu.VMEM((N_DEV, SMALL_R, PACK_W), F32),
                        pltpu.SemaphoreType.DMA((7,)), pltpu.SemaphoreType.DMA((7,))],
    )(*[grads[n] for n in SMALL], loss, *[Ws[n] for n in SMALL], *[Ms[n] for n in SMALL], *[Vs[n] for n in SMALL])
    return [dict(zip(SMALL, res[j * ns:(j + 1) * ns])) for j in range(4)], res[4 * ns]
```

```python
import math

import jax
import jax.numpy as jnp
from jax import lax
from jax.experimental import pallas as pl
from jax.experimental.pallas import tpu as pltpu

F32, BF16 = jnp.float32, jnp.bfloat16

N_DEV = 8
D_MODEL = 1024
SSD_HEADS, SSD_HEAD_DIM, SSD_INNER, SSD_STATE, SSD_CONV, SSD_CHUNK = 8, 64, 512, 128, 4, 128
SSD_XBC = 1024
MLA_HEADS, MLA_NOPE, MLA_ROPE, MLA_QK, MLA_V = 8, 64, 32, 96, 64
MLA_Q_RANK, MLA_KV_RANK = 384, 256
ROPE_THETA = 10000.0
MEM_HEADS, MEM_HEAD_DIM = 4, 256
D_FF = 4096
IN_WIDTH = 2216
LN_EPS, RMS_EPS = 1e-5, 1e-6
ALPHA = 2.0 ** 0.25
ADAM_LR, ADAM_B1, ADAM_B2, ADAM_EPS, ADAM_WD, ADAM_STEP = 0.001, 0.9, 0.999, 1e-08, 0.01, 10

LANES = 128
NEG = -1e30
VMEM_LIMIT = 56 * 1024 * 1024
TOK_K = 4096

PQ, PZ, PX, PKV, PDT, PKR, PW = 0, 512, 1024, 2048, 2304, 2432, 2560
KR_LANE = 64


def _cparams(sem):
    return pltpu.CompilerParams(dimension_semantics=sem, vmem_limit_bytes=VMEM_LIMIT)


def _sigmoid(x):
    return 1.0 / (1.0 + jnp.exp(-x))


def _mm(a, b, mode, name, *, tm=512, tn=None, tk=None, out_dtypes=(F32,), epi=None, extras=(), col_slots=False, n_sums=0):
    if mode == "nn":
        (M, K), (K2, N) = a.shape, b.shape
    elif mode == "nt":
        (M, K), (N, K2) = a.shape, b.shape
    else:
        (K, M), (K2, N) = a.shape, b.shape
    assert K == K2, (name, a.shape, b.shape)
    tm, tn, tk = min(tm, M), min(tn or N, N), min(tk or K, K)
    assert M % tm == 0 and N % tn == 0 and K % tk == 0, (name, M, N, K, tm, tn, tk)
    gk = K // tk
    a_spec = pl.BlockSpec((tk, tm), lambda i, j, k: (k, i)) if mode == "tn" else pl.BlockSpec((tm, tk), lambda i, j, k: (i, k))
    b_spec = pl.BlockSpec((tn, tk), lambda i, j, k: (j, k)) if mode == "nt" else pl.BlockSpec((tk, tn), lambda i, j, k: (k, j))
    dims = {"nn": ((1,), (0,)), "nt": ((1,), (1,)), "tn": ((0,), (0,))}[mode]
    ex_specs = []
    for arr, kind in extras:
        if kind == "mn":
            ex_specs.append(pl.BlockSpec((tm, tn), lambda i, j, k: (i, j)))
        elif kind == "n":
            ex_specs.append(pl.BlockSpec((1, tn), lambda i, j, k: (0, j)))
        else:
            ex_specs.append(pl.BlockSpec((tm, arr.shape[1]), lambda i, j, k: (i, 0)))
    ne, no = len(extras), len(out_dtypes)
    assert n_sums == 0 or (gk == 1 and tn == N and not col_slots), name

    def body(*refs):
        a_ref, b_ref = refs[0], refs[1]
        ex, outs, sums = refs[2:2 + ne], refs[2 + ne:2 + ne + no], refs[2 + ne + no:2 + ne + no + n_sums]
        part = lax.dot_general(a_ref[...].astype(BF16), b_ref[...].astype(BF16), (dims, ((), ())),
                               preferred_element_type=F32)

        def finish(acc):
            res = epi(acc, *[e[...] for e in ex]) if epi is not None else (acc,)
            for o, r in zip(outs, res[:no]):
                o[...] = r.astype(o.dtype)
            if n_sums:
                i = pl.program_id(0)

                @pl.when(i == 0)
                def _():
                    for o, r in zip(sums, res[no:]):
                        o[...] = r

                @pl.when(i > 0)
                def _():
                    for o, r in zip(sums, res[no:]):
                        o[...] += r

        if gk == 1:
            finish(part)
        else:
            acc_ref = refs[-1]
            k = pl.program_id(2)

            @pl.when(k == 0)
            def _():
                acc_ref[...] = part

            @pl.when(k > 0)
            def _():
                acc_ref[...] += part

            @pl.when(k == gk - 1)
            def _():
                finish(acc_ref[...])

    res = pl.pallas_call(
        body, name=name, grid=(M // tm, N // tn, gk),
        in_specs=[a_spec, b_spec] + ex_specs,
        out_specs=[pl.BlockSpec((None, tm, tn), lambda i, j, k: (j, i, 0)) if col_slots else pl.BlockSpec((tm, tn), lambda i, j, k: (i, j))
                   for _ in out_dtypes] + [pl.BlockSpec((1, N), lambda i, j, k: (0, 0))] * n_sums,
        out_shape=[jax.ShapeDtypeStruct((N // tn, M, tn) if col_slots else (M, N), dt) for dt in out_dtypes]
        + [jax.ShapeDtypeStruct((1, N), F32)] * n_sums,
        scratch_shapes=[pltpu.VMEM((tm, tn), F32)] if gk > 1 else [],
        compiler_params=_cparams(("arbitrary" if n_sums else "parallel", "parallel", "arbitrary")),
    )(a, b, *[e[0] for e in extras])
    return res[0] if no + n_sums == 1 else res


def _ln_stats(r):
    mu = jnp.mean(r, axis=-1, keepdims=True)
    xc = r - mu
    var = jnp.mean(xc * xc, axis=-1, keepdims=True)
    rstd = lax.rsqrt(var + LN_EPS)
    return xc * rstd, rstd


def _ln_fwd(terms, g, b, name, tm=512):
    S, D = terms[0][0].shape
    coefs = [c for _, c in terms]
    nt = len(terms)

    def body(*refs):
        r = sum(c * t[...] for t, c in zip(refs[:nt], coefs))
        xh, _ = _ln_stats(r)
        h = xh * refs[nt][...] + refs[nt + 1][...]
        refs[nt + 2][...] = h
        refs[nt + 3][...] = h.astype(BF16)

    row = pl.BlockSpec((tm, D), lambda i: (i, 0))
    vec = pl.BlockSpec((1, D), lambda i: (0, 0))
    return pl.pallas_call(
        body, name=name, grid=(S // tm,), in_specs=[row] * nt + [vec, vec], out_specs=[row, row],
        out_shape=[jax.ShapeDtypeStruct((S, D), F32), jax.ShapeDtypeStruct((S, D), BF16)], compiler_params=_cparams(("parallel",)),
    )(*[t for t, _ in terms], g, b)


def _ln_bwd_tile(xh, rstd, dh, g):
    dxh = dh * g
    m1 = jnp.mean(dxh, axis=-1, keepdims=True)
    m2 = jnp.mean(dxh * xh, axis=-1, keepdims=True)
    return rstd * (dxh - m1 - xh * m2), jnp.sum(dh * xh, axis=0, keepdims=True), jnp.sum(dh, axis=0, keepdims=True)


def _rope_tables(positions):
    half = MLA_ROPE // 2
    inv_freq = jnp.power(ROPE_THETA, -jnp.arange(half, dtype=F32) / half)
    ang = positions.astype(F32)[:, None] * inv_freq
    cos, sin = jnp.cos(ang), jnp.sin(ang)
    S = positions.shape[0]
    one, zero = jnp.ones((S, MLA_NOPE), F32), jnp.zeros((S, half), F32)
    pad = jnp.zeros((S, LANES - MLA_QK), F32)
    c = jnp.concatenate([one, cos, cos, pad], axis=1)
    s1 = jnp.concatenate([0 * one, -sin, zero, pad], axis=1)
    s2 = jnp.concatenate([0 * one, zero, sin, pad], axis=1)
    return c, s1, s2


def _rope_block(x, c, s1, s2):
    half = MLA_ROPE // 2
    return x * c + pltpu.roll(x, LANES - half, axis=1) * s1 + pltpu.roll(x, half, axis=1) * s2


def _rms_fwd(x, g):
    r = lax.rsqrt(jnp.mean(x * x, axis=-1, keepdims=True) + RMS_EPS)
    return x * r * g


def _rms_bwd(x, g, dy):
    r = lax.rsqrt(jnp.mean(x * x, axis=-1, keepdims=True) + RMS_EPS)
    xh = x * r
    dyh = dy * g
    dx = r * (dyh - xh * jnp.mean(dyh * xh, axis=-1, keepdims=True))
    return dx, jnp.sum(dy * xh, axis=0, keepdims=True)


def _mla_prep(proj, qg, kvg, tabs, tm=512):
    S = proj.shape[0]

    def body(ql_ref, kvl_ref, kr_ref, qg_ref, kvg_ref, c_ref, s1_ref, s2_ref, qn_ref, kvn_ref, kpe_ref):
        qn_ref[...] = _rms_fwd(ql_ref[...], qg_ref[...]).astype(BF16)
        kvn_ref[...] = _rms_fwd(kvl_ref[...], kvg_ref[...]).astype(BF16)
        kpe_ref[...] = _rope_block(kr_ref[...], c_ref[...], s1_ref[...], s2_ref[...])

    tab = pl.BlockSpec((tm, LANES), lambda i: (i, 0))
    return pl.pallas_call(
        body, name="mla_prep", grid=(S // tm,),
        in_specs=[pl.BlockSpec((tm, MLA_Q_RANK), lambda i: (i, PQ // MLA_Q_RANK)),
                  pl.BlockSpec((tm, MLA_KV_RANK), lambda i: (i, PKV // MLA_KV_RANK)),
                  pl.BlockSpec((tm, LANES), lambda i: (i, PKR // LANES)),
                  pl.BlockSpec((1, MLA_Q_RANK), lambda i: (0, 0)), pl.BlockSpec((1, MLA_KV_RANK), lambda i: (0, 0)),
                  tab, tab, tab],
        out_specs=[pl.BlockSpec((tm, MLA_Q_RANK), lambda i: (i, 0)), pl.BlockSpec((tm, MLA_KV_RANK), lambda i: (i, 0)), tab],
        out_shape=[jax.ShapeDtypeStruct((S, MLA_Q_RANK), BF16), jax.ShapeDtypeStruct((S, MLA_KV_RANK), BF16),
                   jax.ShapeDtypeStruct((S, LANES), F32)],
        compiler_params=_cparams(("parallel",)),
    )(proj, proj, proj, qg, kvg, *tabs)


def _mla_prep_bwd(proj, qg, kvg, tabs, dqn, dkvn, dk_all, tm=512):
    S = proj.shape[0]

    def body(ql_ref, kvl_ref, qg_ref, kvg_ref, c_ref, s1_ref, s2_ref, dqn_ref, dkvn_ref, dk_ref,
             dql_ref, dkvl_ref, dkr_ref, dqg_ref, dkvg_ref):
        i = pl.program_id(0)
        dql, pq = _rms_bwd(ql_ref[...], qg_ref[...], dqn_ref[...])
        dkvl, pkv = _rms_bwd(kvl_ref[...], kvg_ref[...], dkvn_ref[...])
        dql_ref[...] = dql
        dkvl_ref[...] = dkvl
        dk = dk_ref[...]
        dkpe = dk[:, 0:LANES]
        for h in range(1, MLA_HEADS):
            dkpe = dkpe + dk[:, h * LANES:(h + 1) * LANES]
        lane = lax.broadcasted_iota(jnp.int32, dkpe.shape, 1)
        dkpe = jnp.where((lane >= KR_LANE) & (lane < KR_LANE + MLA_ROPE), dkpe, 0.0)
        dkr_ref[...] = _rope_block(dkpe, c_ref[...], -s1_ref[...], -s2_ref[...])

        @pl.when(i == 0)
        def _():
            dqg_ref[...] = pq
            dkvg_ref[...] = pkv

        @pl.when(i > 0)
        def _():
            dqg_ref[...] += pq
            dkvg_ref[...] += pkv

    tab = pl.BlockSpec((tm, LANES), lambda i: (i, 0))
    qspec = pl.BlockSpec((tm, MLA_Q_RANK), lambda i: (i, 0))
    kvspec = pl.BlockSpec((tm, MLA_KV_RANK), lambda i: (i, 0))
    qv, kvv = pl.BlockSpec((1, MLA_Q_RANK), lambda i: (0, 0)), pl.BlockSpec((1, MLA_KV_RANK), lambda i: (0, 0))
    return pl.pallas_call(
        body, name="mla_prep_bwd", grid=(S // tm,),
        in_specs=[pl.BlockSpec((tm, MLA_Q_RANK), lambda i: (i, PQ // MLA_Q_RANK)),
                  pl.BlockSpec((tm, MLA_KV_RANK), lambda i: (i, PKV // MLA_KV_RANK)),
                  qv, kvv, tab, tab, tab, qspec, kvspec, pl.BlockSpec((tm, MLA_HEADS * LANES), lambda i: (i, 0))],
        out_specs=[qspec, kvspec, tab, qv, kvv],
        out_shape=[jax.ShapeDtypeStruct((S, MLA_Q_RANK), F32), jax.ShapeDtypeStruct((S, MLA_KV_RANK), F32),
                   jax.ShapeDtypeStruct((S, LANES), F32), jax.ShapeDtypeStruct((1, MLA_Q_RANK), F32),
                   jax.ShapeDtypeStruct((1, MLA_KV_RANK), F32)],
        compiler_params=_cparams(("arbitrary",)),
    )(proj, proj, qg, kvg, *tabs, dqn, dkvn, dk_all)


def _rope_bwd_all(dq_all, tabs, scale, tm=512):
    S, W = dq_all.shape

    def body(dq_ref, c_ref, s1_ref, s2_ref, o_ref):
        c, s1, s2 = scale * c_ref[...], -scale * s1_ref[...], -scale * s2_ref[...]
        for h in range(W // LANES):
            o_ref[:, h * LANES:(h + 1) * LANES] = _rope_block(dq_ref[:, h * LANES:(h + 1) * LANES], c, s1, s2).astype(BF16)

    tab = pl.BlockSpec((tm, LANES), lambda i: (i, 0))
    row = pl.BlockSpec((tm, W), lambda i: (i, 0))
    return pl.pallas_call(body, name="rope_bwd", grid=(S // tm,), in_specs=[row, tab, tab, tab], out_specs=row,
                          out_shape=jax.ShapeDtypeStruct((S, W), BF16), compiler_params=_cparams(("parallel",)))(dq_all, *tabs)


ATT_SCALE = MLA_QK ** -0.5
LN2 = math.log(2.0)
Q_PRESCALE = ATT_SCALE / LN2
DQ_POSTSCALE = ATT_SCALE / LN2
N_PAIR = MLA_HEADS // 2


def _causal_mask(qi, ki, tq, tk):
    row = qi * tq + lax.broadcasted_iota(jnp.int32, (tq, tk), 0)
    col = ki * tk + lax.broadcasted_iota(jnp.int32, (tq, tk), 1)
    return col <= row


def _lane_tile(x, n):
    return jnp.concatenate([x] * n, axis=1) if n > 1 else x


def _attn_fwd(q_all, kv_all, cat, tq=512, tk=2048):
    S = q_all.shape[0]
    tq, tk = min(tq, S), min(tk, S)
    nq, nk, nb, r = S // tq, S // tk, tk // LANES, tk // tq

    def body(q_ref, k_ref, v_ref, cat_in, o_ref, lse_ref, cat_ref, m_s, l_s, acc_s):
        qi, ki = pl.program_id(1), pl.program_id(2)
        last = lax.div(qi, r)

        @pl.when(ki == 0)
        def _():
            m_s[...] = jnp.full(m_s.shape, NEG, F32)
            l_s[...] = jnp.zeros(l_s.shape, F32)
            acc_s[...] = jnp.zeros(acc_s.shape, F32)

        def block(kc, mask):
            v = v_ref[0:kc, :]
            for hh in range(2):
                q = q_ref[:, hh * LANES:(hh + 1) * LANES]
                k = k_ref[0:kc, hh * LANES:(hh + 1) * LANES]
                s = lax.dot_general(q, k, (((1,), (1,)), ((), ())), preferred_element_type=F32)
                if mask is not None:
                    s = jnp.where(mask, s, NEG)
                m_prev = m_s[hh]
                m_new = jnp.maximum(m_prev, jnp.max(s, axis=-1, keepdims=True))
                p = jnp.exp2(s - _lane_tile(m_new, kc // LANES))
                alpha = jnp.exp2(m_prev - m_new)
                ps = p[:, :LANES]
                for j in range(1, kc // LANES):
                    ps = ps + p[:, j * LANES:(j + 1) * LANES]
                l_s[hh] = alpha * l_s[hh] + ps
                acc_s[hh] = alpha * acc_s[hh] + jnp.dot(p.astype(BF16), v, preferred_element_type=F32)
                m_s[hh] = m_new

        @pl.when(ki < last)
        def _():
            block(tk, None)

        for j in range(r):
            @pl.when((ki == last) & (lax.rem(qi, r) == j))
            def _(j=j):
                kc = (j + 1) * tq
                block(kc, lax.broadcasted_iota(jnp.int32, (tq, kc), 1) <= j * tq + lax.broadcasted_iota(jnp.int32, (tq, kc), 0))

        @pl.when(ki == last)
        def _():
            first = lax.broadcasted_iota(jnp.int32, (tq, LANES), 1) < MLA_V
            l0 = jnp.sum(l_s[0], axis=-1, keepdims=True)
            l1 = jnp.sum(l_s[1], axis=-1, keepdims=True)
            o = jnp.where(first, acc_s[0] / l0, acc_s[1] / l1)
            o_ref[...] = o
            cat_ref[...] = o.astype(BF16)
            lse_ref[:, :LANES] = m_s[0] + jnp.log2(l0)
            lse_ref[:, LANES:] = m_s[1] + jnp.log2(l1)

    return pl.pallas_call(
        body, name="mla_attn_fwd", grid=(N_PAIR, nq, nk),
        in_specs=[pl.BlockSpec((tq, 2 * LANES), lambda p, qi, ki: (qi, p)),
                  pl.BlockSpec((tk, 2 * LANES), lambda p, qi, ki: (jnp.minimum(ki, lax.div(qi, r)), p)),
                  pl.BlockSpec((tk, LANES), lambda p, qi, ki: (jnp.minimum(ki, lax.div(qi, r)), MLA_HEADS + p)), ANY],
        out_specs=[pl.BlockSpec((tq, LANES), lambda p, qi, ki: (qi, p)), pl.BlockSpec((tq, 2 * LANES), lambda p, qi, ki: (qi, p)),
                   pl.BlockSpec((tq, LANES), lambda p, qi, ki: (qi, N_PAIR + p))],
        out_shape=[jax.ShapeDtypeStruct((S, MLA_HEADS * MLA_V), F32), jax.ShapeDtypeStruct((S, MLA_HEADS * LANES), F32),
                   jax.ShapeDtypeStruct(cat.shape, cat.dtype)],
        input_output_aliases={3: 2},
        scratch_shapes=[pltpu.VMEM((2, tq, LANES), F32), pltpu.VMEM((2, tq, LANES), F32), pltpu.VMEM((2, tq, LANES), F32)],
        compiler_params=_cparams(("parallel", "parallel", "arbitrary")),
    )(q_all, kv_all, kv_all, cat)


def _attn_delta(do, o):
    prod = do * o
    tm = prod.shape[0]
    first = lax.broadcasted_iota(jnp.int32, (tm, LANES), 1) < MLA_V
    blocks = []
    for p in range(N_PAIR):
        pp = prod[:, p * LANES:(p + 1) * LANES]
        blocks.append(jnp.broadcast_to(jnp.sum(jnp.where(first, pp, 0.0), axis=-1, keepdims=True), (tm, LANES)))
        blocks.append(jnp.broadcast_to(jnp.sum(jnp.where(first, 0.0, pp), axis=-1, keepdims=True), (tm, LANES)))
    return jnp.concatenate(blocks, axis=1)


def _attn_bwd(q_all, kv_all, dcat, lse, delta, tq=1024, tk=1024):
    S = q_all.shape[0]
    tq, tk = min(tq, S), min(tk, S)
    nq, nk, nb = S // tq, S // tk, tk // LANES
    assert tq == tk

    def body(q_ref, k_ref, v_ref, do_ref, lse_ref, dl_ref, dq_ref, dk_ref, dv_ref, dk_s, dv_s):
        ki, qi = pl.program_id(1), pl.program_id(2)

        @pl.when((ki == 0) & (qi == 0))
        def _():
            dq_ref[...] = jnp.zeros(dq_ref.shape, F32)

        @pl.when(qi == 0)
        def _():
            dk_s[...] = jnp.zeros(dk_s.shape, F32)
            dv_s[...] = jnp.zeros(dv_s.shape, F32)

        def block(r0, nr, kc, mask):
            v, do = v_ref[0:kc, :], do_ref[r0:r0 + nr, :]
            first = lax.broadcasted_iota(jnp.int32, (nr, LANES), 1) < MLA_V
            firstk = lax.broadcasted_iota(jnp.int32, (kc, LANES), 1) < MLA_V
            do_b = do.astype(BF16)
            rows = pl.ds(pl.multiple_of(qi * tq + r0, LANES), nr)
            for hh in range(2):
                sl = slice(hh * LANES, (hh + 1) * LANES)
                q, k = q_ref[r0:r0 + nr, sl], k_ref[0:kc, sl]
                s = lax.dot_general(q, k, (((1,), (1,)), ((), ())), preferred_element_type=F32)
                if mask is not None:
                    s = jnp.where(mask, s, NEG)
                p = jnp.exp2(s - _lane_tile(lse_ref[r0:r0 + nr, sl], kc // LANES))
                do_h = jnp.where(first if hh == 0 else ~first, do, 0.0).astype(BF16)
                dp = lax.dot_general(do_h, v, (((1,), (1,)), ((), ())), preferred_element_type=F32)
                ds_b = (p * (dp - _lane_tile(dl_ref[r0:r0 + nr, sl], kc // LANES)) * LN2).astype(BF16)
                pv = lax.dot_general(p.astype(BF16), do_b, (((0,), (0,)), ((), ())), preferred_element_type=F32)
                dv_s[0:kc, :] += jnp.where(firstk if hh == 0 else ~firstk, pv, 0.0)
                dk_s[0:kc, sl] += lax.dot_general(ds_b, q, (((0,), (0,)), ((), ())), preferred_element_type=F32)
                dq_ref[rows, sl] += jnp.dot(ds_b, k, preferred_element_type=F32)

        @pl.when(qi > ki)
        def _():
            block(0, tq, tk, None)

        @pl.when(qi == ki)
        def _():
            h = tq // 2
            block(0, h, h, _causal_mask(0, 0, h, h))
            block(h, h, tk, lax.broadcasted_iota(jnp.int32, (h, tk), 1) <= h + lax.broadcasted_iota(jnp.int32, (h, tk), 0))

        @pl.when(qi == nq - 1)
        def _():
            dk_ref[...] = dk_s[...]
            dv_ref[...] = dv_s[...]

    wide = pl.BlockSpec((tq, 2 * LANES), lambda p, ki, qi: (jnp.maximum(qi, ki), p))
    return pl.pallas_call(
        body, name="mla_attn_bwd", grid=(N_PAIR, nk, nq),
        in_specs=[wide, pl.BlockSpec((tk, 2 * LANES), lambda p, ki, qi: (ki, p)),
                  pl.BlockSpec((tk, LANES), lambda p, ki, qi: (ki, MLA_HEADS + p)),
                  pl.BlockSpec((tq, LANES), lambda p, ki, qi: (jnp.maximum(qi, ki), N_PAIR + p)), wide, wide],
        out_specs=[pl.BlockSpec((S, 2 * LANES), lambda p, ki, qi: (0, p)),
                   pl.BlockSpec((tk, 2 * LANES), lambda p, ki, qi: (ki, p)), pl.BlockSpec((tk, LANES), lambda p, ki, qi: (ki, p))],
        out_shape=[jax.ShapeDtypeStruct((S, MLA_HEADS * LANES), F32), jax.ShapeDtypeStruct((S, MLA_HEADS * LANES), F32),
                   jax.ShapeDtypeStruct((S, MLA_HEADS * MLA_V), F32)],
        scratch_shapes=[pltpu.VMEM((tk, 2 * LANES), F32), pltpu.VMEM((tk, LANES), F32)],
        compiler_params=_cparams(("parallel", "arbitrary", "arbitrary")),
    )(q_all, kv_all, kv_all, dcat, lse, delta)


MEM_SCALE = MEM_HEAD_DIM ** -0.5


def _mem_probs(q, k):
    s = lax.dot_general(q, k, (((1,), (1,)), ((), ())), preferred_element_type=F32) * MEM_SCALE
    e = jnp.exp(s - jnp.max(s, axis=-1, keepdims=True))
    return e / jnp.sum(e, axis=-1, keepdims=True)


def _mem_attn_fwd(qm, km, vm, tq=512):
    S, W = qm.shape
    M = km.shape[0]

    def body(q_ref, k_ref, v_ref, o_ref):
        for h in range(MEM_HEADS):
            sl = slice(h * MEM_HEAD_DIM, (h + 1) * MEM_HEAD_DIM)
            p = _mem_probs(q_ref[:, sl], k_ref[:, sl])
            o_ref[:, sl] = jnp.dot(p.astype(BF16), v_ref[:, sl], preferred_element_type=F32).astype(BF16)

    row = pl.BlockSpec((tq, W), lambda i: (i, 0))
    full = pl.BlockSpec((M, W), lambda i: (0, 0))
    return pl.pallas_call(body, name="mem_attn_fwd", grid=(S // tq,), in_specs=[row, full, full], out_specs=row,
                          out_shape=jax.ShapeDtypeStruct((S, W), BF16), compiler_params=_cparams(("parallel",)))(qm, km, vm)


def _mem_attn_bwd(qm, km, vm, dom, tq=512):
    S, W = qm.shape
    M = km.shape[0]

    def body(q_ref, k_ref, v_ref, do_ref, dq_ref, dk_ref, dv_ref):
        i = pl.program_id(0)

        @pl.when(i == 0)
        def _():
            dk_ref[...] = jnp.zeros(dk_ref.shape, F32)
            dv_ref[...] = jnp.zeros(dv_ref.shape, F32)

        for h in range(MEM_HEADS):
            sl = slice(h * MEM_HEAD_DIM, (h + 1) * MEM_HEAD_DIM)
            q, k, v, do = q_ref[:, sl], k_ref[:, sl], v_ref[:, sl], do_ref[:, sl]
            p = _mem_probs(q, k)
            dv_ref[:, sl] += lax.dot_general(p.astype(BF16), do, (((0,), (0,)), ((), ())), preferred_element_type=F32)
            dp = lax.dot_general(do, v, (((1,), (1,)), ((), ())), preferred_element_type=F32)
            ds = (p * (dp - jnp.sum(dp * p, axis=-1, keepdims=True)) * MEM_SCALE).astype(BF16)
            dq_ref[:, sl] = jnp.dot(ds, k, preferred_element_type=F32).astype(BF16)
            dk_ref[:, sl] += lax.dot_general(ds, q, (((0,), (0,)), ((), ())), preferred_element_type=F32)

    row = pl.BlockSpec((tq, W), lambda i: (i, 0))
    full = pl.BlockSpec((M, W), lambda i: (0, 0))
    return pl.pallas_call(
        body, name="mem_attn_bwd", grid=(S // tq,), in_specs=[row, full, full, row], out_specs=[row, full, full],
        out_shape=[jax.ShapeDtypeStruct((S, W), BF16), jax.ShapeDtypeStruct((M, W), F32), jax.ShapeDtypeStruct((M, W), F32)],
        compiler_params=_cparams(("arbitrary",)),
    )(qm, km, vm, dom)


L = SSD_CHUNK
N_SPAIR = SSD_HEADS // 2
GRP_W = SSD_INNER // 2
XB0, XC0 = SSD_INNER, SSD_INNER + 2 * SSD_STATE


def _cumsum_rows(a, reverse=False):
    row = lax.broadcasted_iota(jnp.int32, a.shape, 0)
    x, sft = a, 1
    while sft < L:
        if reverse:
            x = x + jnp.where(row < L - sft, pltpu.roll(x, L - sft, axis=0), 0.0)
        else:
            x = x + jnp.where(row >= sft, pltpu.roll(x, sft, axis=0), 0.0)
        sft *= 2
    return x


def _shift_down(cur, prev, s):
    if s == 0:
        return cur
    row = lax.broadcasted_iota(jnp.int32, cur.shape, 0)
    return jnp.where(row < s, pltpu.roll(prev, s, axis=0), pltpu.roll(cur, s, axis=0))


def _shift_up(cur, nxt, s):
    if s == 0:
        return cur
    row = lax.broadcasted_iota(jnp.int32, cur.shape, 0)
    return jnp.where(row >= L - s, pltpu.roll(nxt, L - s, axis=0), pltpu.roll(cur, L - s, axis=0))


def _ssd_conv(u, prev, cw, cb):
    delayed = [u] + [_shift_down(u, prev, s) for s in range(1, SSD_CONV)]
    conv = cb + cw[SSD_CONV - 1:SSD_CONV, :] * u
    for s in range(1, SSD_CONV):
        conv = conv + cw[SSD_CONV - 1 - s:SSD_CONV - s, :] * delayed[s]
    return conv, delayed


def _pair_lanes(v, h0, first):
    return jnp.where(first, v[:, h0:h0 + 1], v[:, h0 + 1:h0 + 2])


def _ssd_common(u, prev, dt_raw, cw, cb, dtb, alog):
    conv, delayed = _ssd_conv(u, prev, cw, cb)
    sg = _sigmoid(conv)
    xa = conv * sg
    dpre = dt_raw + dtb
    dtv = jnp.maximum(dpre, 0.0) + jnp.log1p(jnp.exp(-jnp.abs(dpre)))
    a_row = -jnp.exp(alog)
    cs = _cumsum_rows(dtv * a_row)
    return conv, sg, xa, dpre, dtv, a_row, cs, delayed


def _ssd_pair_fwd(xa, dtv, cs, csT, G, Cg, Bg, Sp, dsk, pp, first, tri, rowfirst):
    h0 = 2 * pp
    x = xa[:, pp * LANES:(pp + 1) * LANES]
    xdt = x * _pair_lanes(dtv, h0, first)
    xdt_b = xdt.astype(BF16)
    Ms, yd = [], []
    for h in (h0, h0 + 1):
        lam = jnp.exp(jnp.where(tri, cs[:, h:h + 1] - csT[h:h + 1, :], NEG))
        M = G * lam
        Ms.append((M, lam))
        yd.append(jnp.dot(M.astype(BF16), xdt_b, preferred_element_type=F32))
    T = lax.dot_general(Cg, Sp.astype(BF16), (((1,), (1,)), ((), ())), preferred_element_type=F32)
    E = jnp.exp(_pair_lanes(cs, h0, first))
    yoff = E * T
    csl = cs[L - 1:L, :]
    Fd = jnp.exp(_pair_lanes(csl, h0, first) - _pair_lanes(cs, h0, first))
    el = jnp.exp(csl)
    el_rows = jnp.where(rowfirst, el[:, h0:h0 + 1], el[:, h0 + 1:h0 + 2])
    Sloc = lax.dot_general((xdt * Fd).astype(BF16), Bg, (((0,), (0,)), ((), ())), preferred_element_type=F32)
    S_new = el_rows * Sp + Sloc
    y = jnp.where(first, yd[0], yd[1]) + yoff + x * _pair_lanes(dsk, h0, first[:1])
    return y, S_new, (x, xdt, xdt_b, Ms, E, yoff, Fd, el, el_rows)


def _ssd_masks():
    lane = lax.broadcasted_iota(jnp.int32, (L, LANES), 1)
    row = lax.broadcasted_iota(jnp.int32, (L, LANES), 0)
    return lane, row, lane < SSD_HEAD_DIM, row >= lane, row[:, :1] < SSD_HEAD_DIM


def _ssd_specs(nc, rev):
    def cidx(i):
        return nc - 1 - i if rev else i
    z = pl.BlockSpec((L, SSD_INNER), lambda i: (cidx(i), PZ // SSD_INNER))
    u = pl.BlockSpec((L, SSD_XBC), lambda i: (cidx(i), PX // SSD_XBC))
    dt = pl.BlockSpec((L, LANES), lambda i: (cidx(i), PDT // LANES))
    return cidx, z, u, dt


def _vec(w):
    return pl.BlockSpec((1, w), lambda i: (0, 0))


def _ssd_fwd(proj, cw, cb, dtb, alog, dsk, ng):
    S = proj.shape[0]
    nc = S // L

    def body(z_ref, u_ref, dt_ref, cw_ref, cb_ref, dtb_ref, alog_ref, dsk_ref, ng_ref, y_ref, st_ref, prev_s, state_s):
        c = pl.program_id(0)

        @pl.when(c == 0)
        def _():
            prev_s[...] = jnp.zeros(prev_s.shape, F32)
            state_s[...] = jnp.zeros(state_s.shape, F32)

        u = u_ref[...]
        _, _, xa, _, dtv, _, cs, _ = _ssd_common(u, prev_s[...], dt_ref[...], cw_ref[...], cb_ref[...], dtb_ref[...], alog_ref[...])
        prev_s[...] = u
        csT = cs.T
        _, _, first, tri, rowfirst = _ssd_masks()
        dsk_v = dsk_ref[...]
        ys = []
        for g in range(2):
            Bg = xa[:, XB0 + g * SSD_STATE:XB0 + (g + 1) * SSD_STATE].astype(BF16)
            Cg = xa[:, XC0 + g * SSD_STATE:XC0 + (g + 1) * SSD_STATE].astype(BF16)
            G = lax.dot_general(Cg, Bg, (((1,), (1,)), ((), ())), preferred_element_type=F32)
            for pp in (2 * g, 2 * g + 1):
                Sp = state_s[pp]
                st_ref[pp * LANES:(pp + 1) * LANES, :] = Sp
                y, S_new, _ = _ssd_pair_fwd(xa, dtv, cs, csT, G, Cg, Bg, Sp, dsk_v, pp, first, tri, rowfirst)
                state_s[pp] = S_new
                ys.append(y)
        z = z_ref[...]
        for g in range(2):
            yg = jnp.concatenate([ys[2 * g], ys[2 * g + 1]], axis=1)
            zg = z[:, g * GRP_W:(g + 1) * GRP_W]
            gated = yg * (zg * _sigmoid(zg))
            r = lax.rsqrt(jnp.mean(gated * gated, axis=-1, keepdims=True) + RMS_EPS)
            y_ref[:, g * GRP_W:(g + 1) * GRP_W] = (gated * r * ng_ref[:, g * GRP_W:(g + 1) * GRP_W]).astype(BF16)

    _, zs, us, dts = _ssd_specs(nc, False)
    return pl.pallas_call(
        body, name="ssd_fwd", grid=(nc,),
        in_specs=[zs, us, dts, pl.BlockSpec((8, SSD_XBC), lambda i: (0, 0)), _vec(SSD_XBC), _vec(LANES), _vec(LANES), _vec(LANES),
                  _vec(SSD_INNER)],
        out_specs=[pl.BlockSpec((L, SSD_INNER), lambda i: (i, 0)), pl.BlockSpec((N_SPAIR * LANES, SSD_STATE), lambda i: (i, 0))],
        out_shape=[jax.ShapeDtypeStruct((S, 2 * SSD_INNER), BF16), jax.ShapeDtypeStruct((nc * N_SPAIR * LANES, SSD_STATE), F32)],
        scratch_shapes=[pltpu.VMEM((L, SSD_XBC), F32), pltpu.VMEM((N_SPAIR, LANES, SSD_STATE), F32)],
        compiler_params=_cparams(("arbitrary",)),
    )(proj, proj, proj, cw, cb, dtb, alog, dsk, ng)


def _ssd_bwd(proj, states, dy, cw, cb, dtb, alog, dsk, ng):
    S = proj.shape[0]
    nc = S // L

    def body(z_ref, u_ref, up_ref, dt_ref, st_ref, dy_ref, cw_ref, cb_ref, dtb_ref, alog_ref, dsk_ref, ng_ref,
             dz_ref, du_ref, ddt_ref, dcw_ref, dcb_ref, ddtb_ref, dalog_ref, ddsk_ref, dng_ref,
             dS_s, dconv_s, dD_s):
        i = pl.program_id(0)
        c = nc - 1 - i

        @pl.when(i == 0)
        def _():
            dS_s[...] = jnp.zeros(dS_s.shape, F32)
            dconv_s[...] = jnp.zeros(dconv_s.shape, F32)
            dD_s[...] = jnp.zeros(dD_s.shape, F32)
            for r in (dcw_ref, dcb_ref, ddtb_ref, dalog_ref, ddsk_ref, dng_ref):
                r[...] = jnp.zeros(r.shape, F32)

        u = u_ref[...]
        prev = jnp.where(c > 0, up_ref[...], 0.0)
        cw_v = cw_ref[...]
        conv, sg, xa, dpre, dtv, a_row, cs, delayed = _ssd_common(u, prev, dt_ref[...], cw_v, cb_ref[...], dtb_ref[...], alog_ref[...])
        csT = cs.T
        lane, row, first, tri, rowfirst = _ssd_masks()
        dsk_v = dsk_ref[...]

        fw = []
        Gs, Bs, Cs = [], [], []
        for g in range(2):
            Bg = xa[:, XB0 + g * SSD_STATE:XB0 + (g + 1) * SSD_STATE].astype(BF16)
            Cg = xa[:, XC0 + g * SSD_STATE:XC0 + (g + 1) * SSD_STATE].astype(BF16)
            G = lax.dot_general(Cg, Bg, (((1,), (1,)), ((), ())), preferred_element_type=F32)
            Gs.append(G), Bs.append(Bg), Cs.append(Cg)
            for pp in (2 * g, 2 * g + 1):
                Sp = st_ref[pp * LANES:(pp + 1) * LANES, :]
                y, _, keep = _ssd_pair_fwd(xa, dtv, cs, csT, G, Cg, Bg, Sp, dsk_v, pp, first, tri, rowfirst)
                fw.append((y, Sp, keep))

        z = z_ref[...]
        dys = []
        for g in range(2):
            sl = slice(g * GRP_W, (g + 1) * GRP_W)
            yg = jnp.concatenate([fw[2 * g][0], fw[2 * g + 1][0]], axis=1)
            zg = z[:, sl]
            sz = _sigmoid(zg)
            silu_z = zg * sz
            gated = yg * silu_z
            r = lax.rsqrt(jnp.mean(gated * gated, axis=-1, keepdims=True) + RMS_EPS)
            nh = gated * r
            dout = dy_ref[:, sl]
            dng_ref[:, sl] += jnp.sum(dout * nh, axis=0, keepdims=True)
            dnh = dout * ng_ref[:, sl]
            dgated = r * (dnh - nh * jnp.mean(dnh * nh, axis=-1, keepdims=True))
            dz_ref[:, sl] = dgated * yg * (sz * (1.0 + zg * (1.0 - sz)))
            dyg = dgated * silu_z
            dys.append(dyg[:, :LANES]), dys.append(dyg[:, LANES:])

        dcs_col = [0.0] * SSD_HEADS
        dcs_row = [None] * SSD_HEADS
        ddt_col = [None] * SSD_HEADS
        dxs = []
        dB, dC = [None, None], [None, None]
        last = row == L - 1
        for g in range(2):
            Bg, Cg, G = Bs[g], Cs[g], Gs[g]
            dG = jnp.zeros((L, L), F32)
            dBg = jnp.zeros((L, SSD_STATE), F32)
            dCg = jnp.zeros((L, SSD_STATE), F32)
            for pp in (2 * g, 2 * g + 1):
                h0 = 2 * pp
                y, Sp, (x, xdt, xdt_b, Ms, E, yoff, Fd, el, el_rows) = fw[pp]
                dY = dys[pp]
                dS = dS_s[pp]
                dS_b, Sp_b = dS.astype(BF16), Sp.astype(BF16)
                dD_s[:, pp * LANES:(pp + 1) * LANES] += jnp.sum(dY * x, axis=0, keepdims=True)
                dx = dY * _pair_lanes(dsk_v, h0, first[:1])
                dxdt = jnp.zeros((L, LANES), F32)
                dY_b = dY.astype(BF16)
                for hh, h in enumerate((h0, h0 + 1)):
                    hm = first if hh == 0 else ~first
                    M, lam = Ms[hh]
                    dYh = jnp.where(hm, dY, 0.0).astype(BF16)
                    dM = lax.dot_general(dYh, xdt_b, (((1,), (1,)), ((), ())), preferred_element_type=F32)
                    W = dM * M
                    dcs_col[h] = dcs_col[h] + jnp.sum(W, axis=-1, keepdims=True)
                    dcs_row[h] = jnp.sum(W, axis=0, keepdims=True)
                    dG = dG + dM * lam
                    mt = lax.dot_general(M.astype(BF16), dY_b, (((0,), (0,)), ((), ())), preferred_element_type=F32)
                    dxdt = dxdt + jnp.where(hm, mt, 0.0)
                dT = (E * dY).astype(BF16)
                dCg = dCg + jnp.dot(dT, Sp_b, preferred_element_type=F32)
                dS_in = lax.dot_general(dT, Cg, (((0,), (0,)), ((), ())), preferred_element_type=F32) + el_rows * dS
                q1 = dY * yoff
                dZ = lax.dot_general(Bg, dS_b, (((1,), (1,)), ((), ())), preferred_element_type=F32)
                dBg = dBg + jnp.dot((xdt * Fd).astype(BF16), dS_b, preferred_element_type=F32)
                dxdt = dxdt + dZ * Fd
                q2 = dZ * xdt * Fd
                dSS = dS * Sp
                for hh, h in enumerate((h0, h0 + 1)):
                    hm = first if hh == 0 else ~first
                    rs1 = jnp.sum(jnp.where(hm, q1, 0.0), axis=-1, keepdims=True)
                    rs2 = jnp.sum(jnp.where(hm, q2, 0.0), axis=-1, keepdims=True)
                    rmask = rowfirst if hh == 0 else ~rowfirst
                    d_el = jnp.sum(jnp.sum(jnp.where(rmask, dSS, 0.0), axis=-1, keepdims=True), axis=0, keepdims=True)
                    tail = jnp.sum(rs2, axis=0, keepdims=True) + d_el * el[:, h:h + 1]
                    dcs_col[h] = dcs_col[h] + (rs1 - rs2 + jnp.where(last[:, :1], tail, 0.0))
                    ddt_col[h] = jnp.sum(jnp.where(hm, dxdt * x, 0.0), axis=-1, keepdims=True)
                dS_s[pp] = dS_in
                dxs.append(dx + dxdt * _pair_lanes(dtv, h0, first))
            dG_b = dG.astype(BF16)
            dC[g] = dCg + jnp.dot(dG_b, Bg, preferred_element_type=F32)
            dB[g] = dBg + lax.dot_general(dG_b, Cg, (((0,), (0,)), ((), ())), preferred_element_type=F32)

        dcs_c, dcs_r, ddt_c = (jnp.zeros((L, LANES), F32) for _ in range(3))
        for h in range(SSD_HEADS):
            dcs_c = dcs_c + jnp.where(lane == h, dcs_col[h], 0.0)
            dcs_r = dcs_r + jnp.where(row == h, dcs_row[h], 0.0)
            ddt_c = ddt_c + jnp.where(lane == h, ddt_col[h], 0.0)
        dcs = dcs_c - dcs_r.T
        da = _cumsum_rows(dcs, reverse=True)
        ddt_c = ddt_c + da * a_row
        dalog_ref[...] += jnp.sum(da * dtv, axis=0, keepdims=True) * a_row
        ddt_raw = ddt_c * _sigmoid(dpre)
        ddt_ref[...] = ddt_raw
        ddtb_ref[...] += jnp.sum(ddt_raw, axis=0, keepdims=True)

        dxa = jnp.concatenate(dxs + dB + dC, axis=1)
        dconv = dxa * (sg * (1.0 + conv * (1.0 - sg)))
        dcb_ref[...] += jnp.sum(dconv, axis=0, keepdims=True)
        nxt = dconv_s[...]
        du = cw_v[SSD_CONV - 1:SSD_CONV, :] * dconv
        dcw_ref[SSD_CONV - 1:SSD_CONV, :] += jnp.sum(dconv * u, axis=0, keepdims=True)
        for s in range(1, SSD_CONV):
            k = SSD_CONV - 1 - s
            du = du + cw_v[k:k + 1, :] * _shift_up(dconv, nxt, s)
            dcw_ref[k:k + 1, :] += jnp.sum(dconv * delayed[s], axis=0, keepdims=True)
        du_ref[...] = du
        dconv_s[...] = dconv

        @pl.when(i == nc - 1)
        def _():
            acc = dD_s[...]
            lane1 = lax.broadcasted_iota(jnp.int32, (1, LANES), 1)
            lanew = lax.broadcasted_iota(jnp.int32, acc.shape, 1)
            out = jnp.zeros((1, LANES), F32)
            for h in range(SSD_HEADS):
                tot = jnp.sum(jnp.where((lanew >= h * SSD_HEAD_DIM) & (lanew < (h + 1) * SSD_HEAD_DIM), acc, 0.0),
                              axis=-1, keepdims=True)
                out = out + jnp.where(lane1 == h, tot, 0.0)
            ddsk_ref[...] = out

    cidx, zs, us, dts = _ssd_specs(nc, True)
    ups = pl.BlockSpec((L, SSD_XBC), lambda i: (jnp.maximum(cidx(i) - 1, 0), PX // SSD_XBC))
    rowc = lambda w: pl.BlockSpec((L, w), lambda i: (cidx(i), 0))
    return pl.pallas_call(
        body, name="ssd_bwd", grid=(nc,),
        in_specs=[zs, us, ups, dts, pl.BlockSpec((N_SPAIR * LANES, SSD_STATE), lambda i: (cidx(i), 0)), rowc(SSD_INNER),
                  pl.BlockSpec((8, SSD_XBC), lambda i: (0, 0)), _vec(SSD_XBC), _vec(LANES), _vec(LANES), _vec(LANES), _vec(SSD_INNER)],
        out_specs=[rowc(SSD_INNER), rowc(SSD_XBC), rowc(LANES), pl.BlockSpec((8, SSD_XBC), lambda i: (0, 0)), _vec(SSD_XBC),
                   _vec(LANES), _vec(LANES), _vec(LANES), _vec(SSD_INNER)],
        out_shape=[jax.ShapeDtypeStruct((S, SSD_INNER), F32), jax.ShapeDtypeStruct((S, SSD_XBC), F32),
                   jax.ShapeDtypeStruct((S, LANES), F32), jax.ShapeDtypeStruct((8, SSD_XBC), F32),
                   jax.ShapeDtypeStruct((1, SSD_XBC), F32), jax.ShapeDtypeStruct((1, LANES), F32),
                   jax.ShapeDtypeStruct((1, LANES), F32), jax.ShapeDtypeStruct((1, LANES), F32),
                   jax.ShapeDtypeStruct((1, SSD_INNER), F32)],
        scratch_shapes=[pltpu.VMEM((N_SPAIR, LANES, SSD_STATE), F32), pltpu.VMEM((L, SSD_XBC), F32),
                        pltpu.VMEM((1, SSD_INNER), F32)],
        compiler_params=_cparams(("arbitrary",)),
    )(proj, proj, proj, proj, states, dy, cw, cb, dtb, alog, dsk, ng)


_IN_SEGS = ((PZ, 0, 512), (PX, 512, 1024), (PDT, 1536, 8), (PQ, 1544, 384), (PKV, 1928, 256), (PKR + KR_LANE, 2184, 32))


def _pad_w_in(w):
    parts, at = [], 0
    for dst, src, n in sorted(_IN_SEGS):
        parts += [jnp.zeros((w.shape[0], dst - at), w.dtype), w[:, src:src + n]]
        at = dst + n
    return jnp.concatenate(parts + [jnp.zeros((w.shape[0], PW - at), w.dtype)], axis=1)


def _unpad_w_in(wp):
    segs = sorted(_IN_SEGS, key=lambda t: t[1])
    return jnp.concatenate([wp[:, dst:dst + n] for dst, src, n in segs], axis=1)


def _pad_w_q(w):
    return jnp.pad(w.reshape(MLA_Q_RANK, MLA_HEADS, MLA_QK), ((0, 0), (0, 0), (0, LANES - MLA_QK))).reshape(MLA_Q_RANK, MLA_HEADS * LANES)


def _unpad_w_q(wp):
    return wp.reshape(MLA_Q_RANK, MLA_HEADS, LANES)[:, :, :MLA_QK].reshape(MLA_Q_RANK, MLA_HEADS * MLA_QK)


def _pad_w_kv(w):
    w3 = w.reshape(MLA_KV_RANK, MLA_HEADS, MLA_NOPE + MLA_V)
    k = jnp.pad(w3[:, :, :MLA_NOPE], ((0, 0), (0, 0), (0, LANES - MLA_NOPE))).reshape(MLA_KV_RANK, MLA_HEADS * LANES)
    return jnp.concatenate([k, w3[:, :, MLA_NOPE:].reshape(MLA_KV_RANK, MLA_HEADS * MLA_V)], axis=1)


def _unpad_w_kv(wp):
    k = wp[:, :MLA_HEADS * LANES].reshape(MLA_KV_RANK, MLA_HEADS, LANES)[:, :, :MLA_NOPE]
    v = wp[:, MLA_HEADS * LANES:].reshape(MLA_KV_RANK, MLA_HEADS, MLA_V)
    return jnp.concatenate([k, v], axis=2).reshape(MLA_KV_RANK, MLA_HEADS * (MLA_NOPE + MLA_V))


def _head_lanes(v):
    return jnp.pad(v, ((0, 0), (0, LANES - v.shape[1])))


def _local_step(x, mem, positions, tgt, P, weights_at=None, emit=None):
    tabs = _rope_tables(positions)
    G = {}
    emit = emit or (lambda names, grads: 0.0)

    h0, h0b = _ln_fwd([(x, 1.0)], P["ln_in_g"], P["ln_in_b"], "ln_in")
    if weights_at is not None:
        P = {**P, **weights_at("first", (h0b,) + tuple(tabs))}
    proj = _mm(h0b, P["w_in"], "nn", "proj_in", tm=1024, tn=640)
    if weights_at is not None:
        P = {**P, **weights_at("mid", proj)}
    cat, states = _ssd_fwd(proj, P["conv_w"], P["conv_b"], P["dt_bias"], P["a_log"], P["d_skip"], P["ssd_norm_g"])
    qn, kvn, kpe = _mla_prep(proj, P["q_norm_g"], P["kv_norm_g"], tabs)

    def q_epi(acc, c, s1, s2):
        return (jnp.concatenate([_rope_block(acc[:, h * LANES:(h + 1) * LANES], c, s1, s2) for h in range(MLA_HEADS)], axis=1)
                * Q_PRESCALE,)

    q_all = _mm(qn, P["w_q_up"], "nn", "q_up", out_dtypes=(BF16,), epi=q_epi, extras=[(t, "m") for t in tabs])

    def kv_epi(acc, kp):
        kb = [acc[:, h * LANES:(h + 1) * LANES] + kp for h in range(MLA_HEADS)]
        return (jnp.concatenate(kb + [acc[:, MLA_HEADS * LANES:]], axis=1),)

    kv_all = _mm(kvn, P["w_kv_up"], "nn", "kv_up", out_dtypes=(BF16,), epi=kv_epi, extras=[(kpe, "m")])
    o_att, lse, cat = _attn_fwd(q_all, kv_all, cat)
    def resid_ln_epi(acc, h, g, b):
        r = ALPHA * h + acc
        y = _ln_stats(r)[0] * g + b
        return r, y, y

    ln_out = (F32, F32, BF16)
    r1, h1, h1b = _mm(cat, P["w_mix_out"], "nn", "mix_out", tm=512, out_dtypes=ln_out, epi=resid_ln_epi,
                      extras=[(h0, "mn"), (P["ln1_g"], "n"), (P["ln1_b"], "n")])
    if weights_at is not None:
        P = {**P, **weights_at("late", h1b)}
    qm = _mm(h1b, P["w_mem_q"], "nn", "mem_q", tm=1024, out_dtypes=(BF16,))
    km = _mm(mem, P["w_mem_k"], "nn", "mem_k", out_dtypes=(BF16,))
    vm = _mm(mem, P["w_mem_v"], "nn", "mem_v", out_dtypes=(BF16,))
    om = _mem_attn_fwd(qm, km, vm)
    r2, h2, h2b = _mm(om, P["w_mem_o"], "nn", "mem_o", tm=512, out_dtypes=ln_out, epi=resid_ln_epi,
                      extras=[(h1, "mn"), (P["ln2_g"], "n"), (P["ln2_b"], "n")])

    def up_epi(acc):
        r = jnp.maximum(acc, 0.0)
        return r * r, 2.0 * r

    act, dact = _mm(h2b, P["w_up"], "nn", "mlp_up", tm=1024, tn=1024, out_dtypes=(BF16, BF16), epi=up_epi)

    def loss_epi(acc, h, g, b, t):
        xh, rstd = _ln_stats(ALPHA * h + acc)
        diff = xh * g + b - t
        part = 0.5 * jnp.sum(jnp.mean(diff * diff, axis=-1, keepdims=True), axis=0, keepdims=True)
        dr, pg, pb = _ln_bwd_tile(xh, rstd, diff * (1.0 / D_MODEL), g)
        return dr, dr, jnp.broadcast_to(part, pg.shape), pg, pb

    dr3, dr3b, loss, G["ln3_g"], G["ln3_b"] = _mm(
        act, P["w_down"], "nn", "mlp_down", tm=512, tk=D_FF, out_dtypes=(F32, BF16), epi=loss_epi, n_sums=3,
        extras=[(h2, "mn"), (P["ln3_g"], "n"), (P["ln3_b"], "n"), (tgt, "mn")])
    loss = loss[:, :LANES]

    def ln_bwd_epi(acc, d, r, g):
        dr, pg, pb = _ln_bwd_tile(*_ln_stats(r), ALPHA * d + acc, g)
        return dr, dr, pg, pb

    def dact_epi(acc, g):
        return (acc.astype(BF16) * g,)

    du = _mm(dr3b, P["w_down"], "nt", "mlp_down_dx", tm=1024, tn=1024, out_dtypes=(BF16,), epi=dact_epi, extras=[(dact, "mn")])
    G["w_down"] = _mm(act, dr3b, "tn", "mlp_down_dw", tm=1024, tk=TOK_K, out_dtypes=(BF16,))
    G["w_up"] = _mm(h2b, du, "tn", "mlp_up_dw", tm=1024, tn=D_FF // N_DEV, tk=TOK_K, out_dtypes=(BF16,), col_slots=True)
    tie = emit(("w_down", "w_up"), G)
    dr2, dr2b, G["ln2_g"], G["ln2_b"] = _mm(
        du, P["w_up"], "nt", "mlp_up_dx", tm=512, tk=D_FF, out_dtypes=(F32, BF16), epi=ln_bwd_epi, n_sums=2,
        extras=[(dr3, "mn"), (r2, "mn"), (P["ln2_g"] + tie, "n")])

    dom = _mm(dr2b, P["w_mem_o"], "nt", "mem_o_dx", tm=1024, out_dtypes=(BF16,))
    G["w_mem_o"] = _mm(om, dr2b, "tn", "mem_o_dw", tm=1024, tk=TOK_K, out_dtypes=(BF16,))
    dqm, dkm, dvm = _mem_attn_bwd(qm, km, vm, dom)
    G["w_mem_q"] = _mm(h1b, dqm, "tn", "mem_q_dw", tm=1024, tk=TOK_K, out_dtypes=(BF16,))
    G["w_mem_k"] = _mm(mem, dkm, "tn", "mem_k_dw", tm=1024, out_dtypes=(BF16,))
    G["w_mem_v"] = _mm(mem, dvm, "tn", "mem_v_dw", tm=1024, out_dtypes=(BF16,))
    tie = emit(("w_mem_o", "w_mem_q", "w_mem_k", "w_mem_v"), G)
    dr1, dr1b, G["ln1_g"], G["ln1_b"] = _mm(
        dqm, P["w_mem_q"], "nt", "mem_q_dx", tm=512, out_dtypes=(F32, BF16), epi=ln_bwd_epi, n_sums=2,
        extras=[(dr2, "mn"), (r1, "mn"), (P["ln1_g"] + tie, "n")])

    def dcat_epi(acc, o):
        return acc, _attn_delta(acc[:, SSD_INNER:], o)

    dcat, delta = _mm(dr1b, P["w_mix_out"], "nt", "mix_out_dx", tm=512, out_dtypes=(F32, F32), epi=dcat_epi, extras=[(o_att, "m")])
    G["w_mix_out"] = _mm(cat, dr1b, "tn", "mix_out_dw", tm=1024, tk=TOK_K, out_dtypes=(BF16,))
    dq_all, dk_all, dv_all = _attn_bwd(q_all, kv_all, dcat, lse, delta)
    dq_pre = _rope_bwd_all(dq_all, tabs, DQ_POSTSCALE)
    G["w_q_up"] = _mm(qn, dq_pre, "tn", "q_up_dw", tk=TOK_K, out_dtypes=(BF16,))
    dqn = _mm(dq_pre, P["w_q_up"], "nt", "q_up_dx", tm=1024)
    dkv_all = jnp.concatenate([dk_all, dv_all], axis=1).astype(BF16)
    G["w_kv_up"] = _mm(kvn, dkv_all, "tn", "kv_up_dw", tk=TOK_K, out_dtypes=(BF16,))
    dkvn = _mm(dkv_all, P["w_kv_up"], "nt", "kv_up_dx", tm=1024)
    dql, dkvl, dkr, G["q_norm_g"], G["kv_norm_g"] = _mla_prep_bwd(proj, P["q_norm_g"], P["kv_norm_g"], tabs, dqn, dkvn, dk_all)
    (dz, dxbc, ddt, G["conv_w"], G["conv_b"], G["dt_bias"], G["a_log"], G["d_skip"], G["ssd_norm_g"]) = _ssd_bwd(
        proj, states, dcat, P["conv_w"], P["conv_b"], P["dt_bias"], P["a_log"], P["d_skip"], P["ssd_norm_g"])
    tie = emit(("w_mix_out", "w_q_up", "w_kv_up", "conv_w"), G)
    S = x.shape[0]
    dproj = jnp.concatenate([dql, jnp.zeros((S, PZ - MLA_Q_RANK), F32) + tie, dz, dxbc, dkvl, ddt, dkr], axis=1).astype(BF16)
    G["w_in"] = _mm(h0b, dproj, "tn", "proj_in_dw", tm=1024, tn=640, tk=TOK_K, out_dtypes=(BF16,))
    tie = emit(("w_in",), G)
    gx, G["ln_in_g"], G["ln_in_b"] = _mm(
        dproj, P["w_in"], "nt", "proj_in_dx", tm=512, tk=PW, epi=lambda acc, d, r, g: ln_bwd_epi(acc, d, r, g)[1:], n_sums=2,
        extras=[(dr1, "mn"), (x, "mn"), (P["ln_in_g"] + tie, "n")])
    return loss, gx, G


PACK_W = 1024
BIG = (("w_in", (1024, 277), 1), ("conv_w", (4, 128), 1), ("w_q_up", (384, 96), 1), ("w_kv_up", (256, 128), 1),
       ("w_mix_out", (128, 1024), 0), ("w_mem_q", (128, 1024), 0), ("w_mem_k", (128, 1024), 0), ("w_mem_v", (128, 1024), 0),
       ("w_mem_o", (128, 1024), 0), ("w_up", (1024, 512), 1), ("w_down", (512, 1024), 0))
SMALL = ("ln_in_g", "ln_in_b", "conv_b", "ln1_g", "ln1_b", "ln2_g", "ln2_b", "ln3_g", "ln3_b",
         "ssd_norm_g", "q_norm_g", "kv_norm_g", "dt_bias", "a_log", "d_skip")
ALL_W = ("ln_in_g", "ln_in_b", "w_in", "conv_w", "conv_b", "dt_bias", "a_log", "d_skip", "ssd_norm_g", "q_norm_g", "w_q_up",
         "kv_norm_g", "w_kv_up", "w_mix_out", "ln1_g", "ln1_b", "w_mem_q", "w_mem_k", "w_mem_v", "w_mem_o", "ln2_g", "ln2_b",
         "w_up", "w_down", "ln3_g", "ln3_b")


SMALL_R = 16
LOSS_ROW = 15
_SMALL_ROWS = (("ln_in_g",), ("ln_in_b",), ("conv_b",), ("ln1_g",), ("ln1_b",), ("ln2_g",), ("ln2_b",), ("ln3_g",), ("ln3_b",),
               ("ssd_norm_g", "q_norm_g"), ("kv_norm_g", "dt_bias", "a_log", "d_skip"))
_SMALL_W = {"ssd_norm_g": 512, "q_norm_g": 384, "kv_norm_g": 256, "dt_bias": LANES, "a_log": LANES, "d_skip": LANES}


MESH = pl.DeviceIdType.MESH
ANY = pl.BlockSpec(memory_space=pl.ANY)
VM = pl.BlockSpec(memory_space=pltpu.VMEM)


def _coords():
    return lax.axis_index("x"), lax.axis_index("y"), lax.axis_index("c")


def _slot(px, py, pc):
    return 4 * px + 2 * py + pc


def _peer(k, x, y, c):
    dx, dy, dc = (k >> 2) & 1, (k >> 1) & 1, k & 1
    return (1 - x if dx else x, 1 - y if dy else y, 1 - c if dc else c)


def _adam(w, g, m, v):
    m = ADAM_B1 * m + (1.0 - ADAM_B1) * g
    v = ADAM_B2 * v + (1.0 - ADAM_B2) * (g * g)
    m_hat = m / (1.0 - ADAM_B1 ** ADAM_STEP)
    v_hat = v / (1.0 - ADAM_B2 ** ADAM_STEP)
    delta = -ADAM_LR * (m_hat / (jnp.sqrt(v_hat) + ADAM_EPS) + ADAM_WD * w)
    return delta, m, v


def _sum_slots(ref):
    tot = ref[0].astype(F32)
    for q in range(1, N_DEV):
        tot = tot + ref[q].astype(F32)
    return tot


def _row(v):
    return v.reshape(1, -1).astype(F32)


def kernel(x, mem, positions, ln_in_g, ln_in_b, w_in, conv_w, conv_b, dt_bias, a_log, d_skip, ssd_norm_g, q_norm_g, w_q_up, kv_norm_g, w_kv_up, w_mix_out, ln1_g, ln1_b, w_mem_q, w_mem_k, w_mem_v, w_mem_o, ln2_g, ln2_b, w_up, w_down, ln3_g, ln3_b, loss_target, m_ln_in_g, m_ln_in_b, m_w_in, m_conv_w, m_conv_b, m_dt_bias, m_a_log, m_d_skip, m_ssd_norm_g, m_q_norm_g, m_w_q_up, m_kv_norm_g, m_w_kv_up, m_w_mix_out, m_ln1_g, m_ln1_b, m_w_mem_q, m_w_mem_k, m_w_mem_v, m_w_mem_o, m_ln2_g, m_ln2_b, m_w_up, m_w_down, m_ln3_g, m_ln3_b, v_ln_in_g, v_ln_in_b, v_w_in, v_conv_w, v_conv_b, v_dt_bias, v_a_log, v_d_skip, v_ssd_norm_g, v_q_norm_g, v_w_q_up, v_kv_norm_g, v_w_kv_up, v_w_mix_out, v_ln1_g, v_ln1_b, v_w_mem_q, v_w_mem_k, v_w_mem_v, v_w_mem_o, v_ln2_g, v_ln2_b, v_w_up, v_w_down, v_ln3_g, v_ln3_b):
    a = dict(locals())
    W = {n: a[n] for n in ALL_W}
    M = {n: a["m_" + n] for n in ALL_W}
    V = {n: a["v_" + n] for n in ALL_W}
    return _step_overlapped(x, mem, positions, loss_target, W, M, V)


HBM = pl.BlockSpec(memory_space=pltpu.HBM)
SEM = pl.BlockSpec(memory_space=pltpu.SEMAPHORE)
EFFECT = pltpu.SideEffectType.DATAFLOW_SIDE_EFFECTING
SHARD_SHAPE = {n: s for n, s, _ in BIG}
SHARD_AXIS = {n: ax for n, _, ax in BIG}
GATHER_FIRST = ("w_in",)
GATHER_MID = ("conv_w", "w_q_up", "w_kv_up", "w_mix_out")
GATHER_LATE = ("w_mem_q", "w_mem_k", "w_mem_v", "w_mem_o", "w_up", "w_down")


def _my_slot():
    return _slot(*_coords())


def _group_copies(src_refs, land_refs, send_sems, recv_sems, slotted, landing_of_peer):
    x, y, c = _coords()
    my = _slot(x, y, c)
    cps = []
    for a, (s_ref, l_ref) in enumerate(zip(src_refs, land_refs)):
        for k in range(1, N_DEV):
            peer = _peer(k, x, y, c)
            cps.append(pltpu.make_async_remote_copy(
                src_ref=s_ref.at[_slot(*peer)] if slotted else s_ref,
                dst_ref=l_ref.at[_slot(*peer)] if landing_of_peer else l_ref.at[my],
                send_sem=send_sems.at[7 * a + k - 1], recv_sem=recv_sems.at[7 * a + k - 1],
                device_id=peer, device_id_type=MESH))
    return cps


def _send_start(srcs, lands, slotted, name):
    n = len(srcs)

    def body(*refs):
        for cp in _group_copies(refs[:n], refs[n:2 * n], refs[2 * n], refs[2 * n + 1], slotted, False):
            cp.start()
        refs[-1][...] = jnp.zeros(refs[-1].shape, F32)

    res = pl.pallas_call(
        body, name=name,
        out_shape=(pltpu.SemaphoreType.DMA((7 * n,)), pltpu.SemaphoreType.DMA((7 * n,)),
                   *[pltpu.HBM(a.shape, a.dtype) for a in srcs], *[pltpu.HBM(a.shape, a.dtype) for a in lands],
                   jax.ShapeDtypeStruct((8, LANES), F32)),
        in_specs=[HBM] * (2 * n), out_specs=(SEM, SEM, *[HBM] * (2 * n), VM),
        input_output_aliases={i: 2 + i for i in range(2 * n)},
        compiler_params=pltpu.CompilerParams(has_side_effects=EFFECT),
    )(*[pltpu.with_memory_space_constraint(a, pltpu.HBM) for a in list(srcs) + list(lands)])
    return (res[0], res[1], res[2:2 + n], res[2 + n:2 + 2 * n]), res[-1][:1, :1]


def _send_wait(started, after, slotted, name):
    send_sems, recv_sems, srcs, lands = started
    n = len(srcs)
    after = list(after) if isinstance(after, (list, tuple)) else [after]

    def body(*refs):
        for cp in _group_copies(refs[:n], refs[n:2 * n], refs[2 * n], refs[2 * n + 1], slotted, True):
            cp.wait_send()
            cp.wait_recv()

    res = pl.pallas_call(
        body, name=name, out_shape=tuple(pltpu.HBM(a.shape, a.dtype) for a in list(srcs) + list(lands)),
        in_specs=[HBM] * (2 * n) + [SEM, SEM] + [ANY] * len(after), out_specs=tuple([HBM] * (2 * n)),
        input_output_aliases={i: i for i in range(2 * n)},
        compiler_params=pltpu.CompilerParams(has_side_effects=EFFECT),
    )(*srcs, *lands, send_sems, recv_sems, *after)
    return res[n:]


def _landing(own, my):
    return lax.dynamic_update_slice(lax.empty((N_DEV,) + own.shape, own.dtype), own[None], (my,) + (0,) * own.ndim)


def _full_from_slots(name, slots):
    a, b = SHARD_SHAPE[name]
    return slots.reshape(N_DEV * a, b) if SHARD_AXIS[name] == 0 else slots.transpose(1, 0, 2).reshape(a, N_DEV * b)


def _slots_from_full(name, g):
    a, b = SHARD_SHAPE[name]
    return g.reshape(N_DEV, a, b) if SHARD_AXIS[name] == 0 else g.reshape(a, N_DEV, b).transpose(1, 0, 2)


def _reduce_adam(recv, w, m, v, name):
    _, a, b = recv.shape
    ta = a
    while ta * b * 4 * N_DEV > 4 * 1024 * 1024 and ta % 16 == 0:
        ta //= 2

    def body(r_ref, w_ref, m_ref, v_ref, g_ref, d_ref, nm_ref, nv_ref):
        g = _sum_slots(r_ref)
        g_ref[...] = g
        d_ref[...], nm_ref[...], nv_ref[...] = _adam(w_ref[...], g, m_ref[...], v_ref[...])

    row = pl.BlockSpec((ta, b), lambda i: (i, 0))
    return pl.pallas_call(
        body, name=name, grid=(a // ta,),
        in_specs=[pl.BlockSpec((N_DEV, ta, b), lambda i: (0, i, 0)), row, row, row], out_specs=[row] * 4,
        out_shape=[jax.ShapeDtypeStruct((a, b), F32)] * 4, compiler_params=_cparams(("parallel",)),
    )(recv, w, m, v)


def _step_overlapped(x, mem, positions, tgt, W, M, V):
    my = _my_slot()
    shard = {n: W[n][0] for n, _, _ in BIG}
    send = {n: (shard[n] if n == "conv_w" else shard[n].astype(BF16)) for n in shard}

    first_src, mid_src, late_src = ([send[n] for n in grp] for grp in (GATHER_FIRST, GATHER_MID, GATHER_LATE))
    first, tie = _send_start(first_src, [_landing(s, my) for s in first_src], False, "gather_first_start")
    my_then = my + tie[0, 0].astype(jnp.int32)
    mid, tie = _send_start(mid_src, [_landing(s, my_then) for s in mid_src], False, "gather_mid_start")
    my_then = my + tie[0, 0].astype(jnp.int32)
    late, tie = _send_start(late_src, [_landing(s, my_then) for s in late_src], False, "gather_late_start")

    P = {n: _row(W[n]) for n in SMALL}
    for n in ("dt_bias", "a_log", "d_skip"):
        P[n] = _head_lanes(P[n])
    P["ln_in_g"] = P["ln_in_g"] + tie

    def weights_at(stage, after):
        if stage == "first":
            lands = _send_wait(first, after, False, "gather_first_wait")
            return dict(w_in=_pad_w_in(_full_from_slots("w_in", lands[0])))
        if stage == "mid":
            lands = _send_wait(mid, after, False, "gather_mid_wait")
            full = {n: _full_from_slots(n, l) for n, l in zip(GATHER_MID, lands)}
            return dict(w_q_up=_pad_w_q(full["w_q_up"]), w_kv_up=_pad_w_kv(full["w_kv_up"]), w_mix_out=full["w_mix_out"],
                        conv_w=jnp.pad(full["conv_w"], ((0, 8 - SSD_CONV), (0, 0))))
        lands = _send_wait(late, after, False, "gather_late_wait")
        return {n: _full_from_slots(n, l) for n, l in zip(GATHER_LATE, lands)}

    started, res = [], {}

    def finish(i, after):
        names, st = started[i]
        lands = _send_wait(st, after, True, "scatter_wait_%d" % i)
        for n, recv in zip(names, lands):
            res[n] = _reduce_adam(recv, shard[n], M[n][0], V[n][0], "reduce_adam_" + n)
        return sum(res[n][0][:1, :1] for n in names) * 0.0

    def emit(names, G):
        srcs = []
        for n in names:
            g = G[n]
            if n == "w_in":
                g = _unpad_w_in(g)
            elif n == "w_q_up":
                g = _unpad_w_q(g)
            elif n == "w_kv_up":
                g = _unpad_w_kv(g)
            elif n == "conv_w":
                g = g[:SSD_CONV]
            srcs.append(g if g.ndim == 3 else _slots_from_full(n, g))
        lands = [_landing(lax.dynamic_index_in_dim(s, my, 0, keepdims=False), my) for s in srcs]
        st, tie = _send_start(srcs, lands, True, "scatter_start_%d" % len(started))
        started.append((names, st))
        if len(started) == 3:
            tie = tie + finish(0, srcs[0]) + finish(1, srcs[0])
        if len(started) == 4:
            tie = tie + finish(2, srcs[0])
        return tie

    loss, gx, G = _local_step(x[0], mem[0], positions[0], tgt[0], P, weights_at, emit)
    finish(3, gx)

    small, loss_tot = _allreduce_adam_vectors(
        {n: G[n] for n in SMALL}, loss, {n: _row(W[n]) for n in SMALL}, {n: _row(M[n]) for n in SMALL}, {n: _row(V[n]) for n in SMALL})

    outs = []
    for j in range(4):
        for n in ALL_W:
            outs.append((res[n][j] if n in res else small[j][n]).reshape(W[n].shape))
    return (loss_tot[0, 0], gx[None], *outs)


def _vector_places():
    places = {}
    for r, names in enumerate(_SMALL_ROWS):
        c = 0
        for n in names:
            w = _SMALL_W.get(n, PACK_W)
            places[n] = (r, c, w, SSD_HEADS if n in ("dt_bias", "a_log", "d_skip") else w)
            c += w
    return places


def _allreduce_adam_vectors(grads, loss, Ws, Ms, Vs):
    places = _vector_places()
    ns = len(SMALL)

    def body(*refs):
        g_in, loss_in = refs[:ns], refs[ns]
        w_in, m_in, v_in = refs[ns + 1:2 * ns + 1], refs[2 * ns + 1:3 * ns + 1], refs[3 * ns + 1:4 * ns + 1]
        o = 4 * ns + 1
        outs = [refs[o + j * ns:o + (j + 1) * ns] for j in range(4)]
        loss_out, stage, land, send_sems, recv_sems = refs[o + 4 * ns:]
        stage[...] = jnp.zeros(stage.shape, F32)
        for i, n in enumerate(SMALL):
            r, c, w, _ = places[n]
            stage[r:r + 1, c:c + w] = g_in[i][...]
        stage[LOSS_ROW:LOSS_ROW + 1, 0:LANES] = loss_in[...]
        x, y, c_ = _coords()
        my = _slot(x, y, c_)
        cps = []
        for k in range(1, N_DEV):
            peer = _peer(k, x, y, c_)
            cps.append(pltpu.make_async_remote_copy(
                src_ref=stage, dst_ref=land.at[my], send_sem=send_sems.at[k - 1], recv_sem=recv_sems.at[k - 1],
                device_id=peer, device_id_type=MESH))
        for cp in cps:
            cp.start()
        land[my] = stage[...]
        for cp in cps:
            cp.wait_recv()
        for cp in cps:
            cp.wait_send()
        tot = _sum_slots(land)
        loss_out[...] = tot[LOSS_ROW:LOSS_ROW + 1, 0:LANES]
        for i, n in enumerate(SMALL):
            r, c, _, wt = places[n]
            g = tot[r:r + 1, c:c + wt]
            outs[0][i][...] = g
            outs[1][i][...], outs[2][i][...], outs[3][i][...] = _adam(w_in[i][...], g, m_in[i][...], v_in[i][...])

    shapes = [jax.ShapeDtypeStruct((1, places[n][3]), F32) for n in SMALL]
    res = pl.pallas_call(
        body, name="allreduce_vectors", in_specs=[VM] * (4 * ns + 1), out_specs=[VM] * (4 * ns + 1),
        out_shape=shapes * 4 + [jax.ShapeDtypeStruct((1, LANES), F32)],
        scratch_shapes=[pltpu.VMEM((SMALL_R, PACK_W), F32), pltpu.VMEM((N_DEV, SMALL_R, PACK_W), F32),
                        pltpu.SemaphoreType.DMA((7,)), pltpu.SemaphoreType.DMA((7,))],
    )(*[grads[n] for n in SMALL], loss, *[Ws[n] for n in SMALL], *[Ms[n] for n in SMALL], *[Vs[n] for n in SMALL])
    return [dict(zip(SMALL, res[j * ns:(j + 1) * ns])) for j in range(4)], res[4 * ns]
```

```python
import math

import jax
import jax.numpy as jnp
from jax import lax
from jax.experimental import pallas as pl
from jax.experimental.pallas import tpu as pltpu

F32, BF16 = jnp.float32, jnp.bfloat16

N_DEV = 8
D_MODEL = 1024
SSD_HEADS, SSD_HEAD_DIM, SSD_INNER, SSD_STATE, SSD_CONV, SSD_CHUNK = 8, 64, 512, 128, 4, 128
SSD_XBC = 1024
MLA_HEADS, MLA_NOPE, MLA_ROPE, MLA_QK, MLA_V = 8, 64, 32, 96, 64
MLA_Q_RANK, MLA_KV_RANK = 384, 256
ROPE_THETA = 10000.0
MEM_HEADS, MEM_HEAD_DIM = 4, 256
D_FF = 4096
IN_WIDTH = 2216
LN_EPS, RMS_EPS = 1e-5, 1e-6
ALPHA = 2.0 ** 0.25
ADAM_LR, ADAM_B1, ADAM_B2, ADAM_EPS, ADAM_WD, ADAM_STEP = 0.001, 0.9, 0.999, 1e-08, 0.01, 10

LANES = 128
NEG = -1e30
VMEM_LIMIT = 56 * 1024 * 1024
TOK_K = 4096

PQ, PZ, PX, PKV, PDT, PKR, PW = 0, 512, 1024, 2048, 2304, 2432, 2560
KR_LANE = 64


def _cparams(sem):
    return pltpu.CompilerParams(dimension_semantics=sem, vmem_limit_bytes=VMEM_LIMIT)


def _sigmoid(x):
    return 1.0 / (1.0 + jnp.exp(-x))


def _mm(a, b, mode, name, *, tm=512, tn=None, tk=None, out_dtypes=(F32,), epi=None, extras=(), col_slots=False, n_sums=0):
    if mode == "nn":
        (M, K), (K2, N) = a.shape, b.shape
    elif mode == "nt":
        (M, K), (N, K2) = a.shape, b.shape
    else:
        (K, M), (K2, N) = a.shape, b.shape
    assert K == K2, (name, a.shape, b.shape)
    tm, tn, tk = min(tm, M), min(tn or N, N), min(tk or K, K)
    assert M % tm == 0 and N % tn == 0 and K % tk == 0, (name, M, N, K, tm, tn, tk)
    gk = K // tk
    a_spec = pl.BlockSpec((tk, tm), lambda i, j, k: (k, i)) if mode == "tn" else pl.BlockSpec((tm, tk), lambda i, j, k: (i, k))
    b_spec = pl.BlockSpec((tn, tk), lambda i, j, k: (j, k)) if mode == "nt" else pl.BlockSpec((tk, tn), lambda i, j, k: (k, j))
    dims = {"nn": ((1,), (0,)), "nt": ((1,), (1,)), "tn": ((0,), (0,))}[mode]
    ex_specs = []
    for arr, kind in extras:
        if kind == "mn":
            ex_specs.append(pl.BlockSpec((tm, tn), lambda i, j, k: (i, j)))
        elif kind == "n":
            ex_specs.append(pl.BlockSpec((1, tn), lambda i, j, k: (0, j)))
        else:
            ex_specs.append(pl.BlockSpec((tm, arr.shape[1]), lambda i, j, k: (i, 0)))
    ne, no = len(extras), len(out_dtypes)
    assert n_sums == 0 or (gk == 1 and tn == N and not col_slots), name

    def body(*refs):
        a_ref, b_ref = refs[0], refs[1]
        ex, outs, sums = refs[2:2 + ne], refs[2 + ne:2 + ne + no], refs[2 + ne + no:2 + ne + no + n_sums]
        part = lax.dot_general(a_ref[...].astype(BF16), b_ref[...].astype(BF16), (dims, ((), ())),
                               preferred_element_type=F32)

        def finish(acc):
            res = epi(acc, *[e[...] for e in ex]) if epi is not None else (acc,)
            for o, r in zip(outs, res[:no]):
                o[...] = r.astype(o.dtype)
            if n_sums:
                i = pl.program_id(0)

                @pl.when(i == 0)
                def _():
                    for o, r in zip(sums, res[no:]):
                        o[...] = r

                @pl.when(i > 0)
                def _():
                    for o, r in zip(sums, res[no:]):
                        o[...] += r

        if gk == 1:
            finish(part)
        else:
            acc_ref = refs[-1]
            k = pl.program_id(2)

            @pl.when(k == 0)
            def _():
                acc_ref[...] = part

            @pl.when(k > 0)
            def _():
                acc_ref[...] += part

            @pl.when(k == gk - 1)
            def _():
                finish(acc_ref[...])

    res = pl.pallas_call(
        body, name=name, grid=(M // tm, N // tn, gk),
        in_specs=[a_spec, b_spec] + ex_specs,
        out_specs=[pl.BlockSpec((None, tm, tn), lambda i, j, k: (j, i, 0)) if col_slots else pl.BlockSpec((tm, tn), lambda i, j, k: (i, j))
                   for _ in out_dtypes] + [pl.BlockSpec((1, N), lambda i, j, k: (0, 0))] * n_sums,
        out_shape=[jax.ShapeDtypeStruct((N // tn, M, tn) if col_slots else (M, N), dt) for dt in out_dtypes]
        + [jax.ShapeDtypeStruct((1, N), F32)] * n_sums,
        scratch_shapes=[pltpu.VMEM((tm, tn), F32)] if gk > 1 else [],
        compiler_params=_cparams(("arbitrary" if n_sums else "parallel", "parallel", "arbitrary")),
    )(a, b, *[e[0] for e in extras])
    return res[0] if no + n_sums == 1 else res


def _ln_stats(r):
    mu = jnp.mean(r, axis=-1, keepdims=True)
    xc = r - mu
    var = jnp.mean(xc * xc, axis=-1, keepdims=True)
    rstd = lax.rsqrt(var + LN_EPS)
    return xc * rstd, rstd


def _ln_fwd(terms, g, b, name, tm=512):
    S, D = terms[0][0].shape
    coefs = [c for _, c in terms]
    nt = len(terms)

    def body(*refs):
        r = sum(c * t[...] for t, c in zip(refs[:nt], coefs))
        xh, _ = _ln_stats(r)
        h = xh * refs[nt][...] + refs[nt + 1][...]
        refs[nt + 2][...] = h
        refs[nt + 3][...] = h.astype(BF16)

    row = pl.BlockSpec((tm, D), lambda i: (i, 0))
    vec = pl.BlockSpec((1, D), lambda i: (0, 0))
    return pl.pallas_call(
        body, name=name, grid=(S // tm,), in_specs=[row] * nt + [vec, vec], out_specs=[row, row],
        out_shape=[jax.ShapeDtypeStruct((S, D), F32), jax.ShapeDtypeStruct((S, D), BF16)], compiler_params=_cparams(("parallel",)),
    )(*[t for t, _ in terms], g, b)


def _ln_bwd_tile(xh, rstd, dh, g):
    dxh = dh * g
    m1 = jnp.mean(dxh, axis=-1, keepdims=True)
    m2 = jnp.mean(dxh * xh, axis=-1, keepdims=True)
    return rstd * (dxh - m1 - xh * m2), jnp.sum(dh * xh, axis=0, keepdims=True), jnp.sum(dh, axis=0, keepdims=True)


def _rope_tables(positions):
    half = MLA_ROPE // 2
    inv_freq = jnp.power(ROPE_THETA, -jnp.arange(half, dtype=F32) / half)
    ang = positions.astype(F32)[:, None] * inv_freq
    cos, sin = jnp.cos(ang), jnp.sin(ang)
    S = positions.shape[0]
    one, zero = jnp.ones((S, MLA_NOPE), F32), jnp.zeros((S, half), F32)
    pad = jnp.zeros((S, LANES - MLA_QK), F32)
    c = jnp.concatenate([one, cos, cos, pad], axis=1)
    s1 = jnp.concatenate([0 * one, -sin, zero, pad], axis=1)
    s2 = jnp.concatenate([0 * one, zero, sin, pad], axis=1)
    return c, s1, s2


def _rope_block(x, c, s1, s2):
    half = MLA_ROPE // 2
    return x * c + pltpu.roll(x, LANES - half, axis=1) * s1 + pltpu.roll(x, half, axis=1) * s2


def _rms_fwd(x, g):
    r = lax.rsqrt(jnp.mean(x * x, axis=-1, keepdims=True) + RMS_EPS)
    return x * r * g


def _rms_bwd(x, g, dy):
    r = lax.rsqrt(jnp.mean(x * x, axis=-1, keepdims=True) + RMS_EPS)
    xh = x * r
    dyh = dy * g
    dx = r * (dyh - xh * jnp.mean(dyh * xh, axis=-1, keepdims=True))
    return dx, jnp.sum(dy * xh, axis=0, keepdims=True)


def _mla_prep(proj, qg, kvg, tabs, tm=512):
    S = proj.shape[0]

    def body(ql_ref, kvl_ref, kr_ref, qg_ref, kvg_ref, c_ref, s1_ref, s2_ref, qn_ref, kvn_ref, kpe_ref):
        qn_ref[...] = _rms_fwd(ql_ref[...], qg_ref[...]).astype(BF16)
        kvn_ref[...] = _rms_fwd(kvl_ref[...], kvg_ref[...]).astype(BF16)
        kpe_ref[...] = _rope_block(kr_ref[...], c_ref[...], s1_ref[...], s2_ref[...])

    tab = pl.BlockSpec((tm, LANES), lambda i: (i, 0))
    return pl.pallas_call(
        body, name="mla_prep", grid=(S // tm,),
        in_specs=[pl.BlockSpec((tm, MLA_Q_RANK), lambda i: (i, PQ // MLA_Q_RANK)),
                  pl.BlockSpec((tm, MLA_KV_RANK), lambda i: (i, PKV // MLA_KV_RANK)),
                  pl.BlockSpec((tm, LANES), lambda i: (i, PKR // LANES)),
                  pl.BlockSpec((1, MLA_Q_RANK), lambda i: (0, 0)), pl.BlockSpec((1, MLA_KV_RANK), lambda i: (0, 0)),
                  tab, tab, tab],
        out_specs=[pl.BlockSpec((tm, MLA_Q_RANK), lambda i: (i, 0)), pl.BlockSpec((tm, MLA_KV_RANK), lambda i: (i, 0)), tab],
        out_shape=[jax.ShapeDtypeStruct((S, MLA_Q_RANK), BF16), jax.ShapeDtypeStruct((S, MLA_KV_RANK), BF16),
                   jax.ShapeDtypeStruct((S, LANES), F32)],
        compiler_params=_cparams(("parallel",)),
    )(proj, proj, proj, qg, kvg, *tabs)


def _mla_prep_bwd(proj, qg, kvg, tabs, dqn, dkvn, dk_all, tm=512):
    S = proj.shape[0]

    def body(ql_ref, kvl_ref, qg_ref, kvg_ref, c_ref, s1_ref, s2_ref, dqn_ref, dkvn_ref, dk_ref,
             dql_ref, dkvl_ref, dkr_ref, dqg_ref, dkvg_ref):
        i = pl.program_id(0)
        dql, pq = _rms_bwd(ql_ref[...], qg_ref[...], dqn_ref[...])
        dkvl, pkv = _rms_bwd(kvl_ref[...], kvg_ref[...], dkvn_ref[...])
        dql_ref[...] = dql
        dkvl_ref[...] = dkvl
        dk = dk_ref[...]
        dkpe = dk[:, 0:LANES]
        for h in range(1, MLA_HEADS):
            dkpe = dkpe + dk[:, h * LANES:(h + 1) * LANES]
        lane = lax.broadcasted_iota(jnp.int32, dkpe.shape, 1)
        dkpe = jnp.where((lane >= KR_LANE) & (lane < KR_LANE + MLA_ROPE), dkpe, 0.0)
        dkr_ref[...] = _rope_block(dkpe, c_ref[...], -s1_ref[...], -s2_ref[...])

        @pl.when(i == 0)
        def _():
            dqg_ref[...] = pq
            dkvg_ref[...] = pkv

        @pl.when(i > 0)
        def _():
            dqg_ref[...] += pq
            dkvg_ref[...] += pkv

    tab = pl.BlockSpec((tm, LANES), lambda i: (i, 0))
    qspec = pl.BlockSpec((tm, MLA_Q_RANK), lambda i: (i, 0))
    kvspec = pl.BlockSpec((tm, MLA_KV_RANK), lambda i: (i, 0))
    qv, kvv = pl.BlockSpec((1, MLA_Q_RANK), lambda i: (0, 0)), pl.BlockSpec((1, MLA_KV_RANK), lambda i: (0, 0))
    return pl.pallas_call(
        body, name="mla_prep_bwd", grid=(S // tm,),
        in_specs=[pl.BlockSpec((tm, MLA_Q_RANK), lambda i: (i, PQ // MLA_Q_RANK)),
                  pl.BlockSpec((tm, MLA_KV_RANK), lambda i: (i, PKV // MLA_KV_RANK)),
                  qv, kvv, tab, tab, tab, qspec, kvspec, pl.BlockSpec((tm, MLA_HEADS * LANES), lambda i: (i, 0))],
        out_specs=[qspec, kvspec, tab, qv, kvv],
        out_shape=[jax.ShapeDtypeStruct((S, MLA_Q_RANK), F32), jax.ShapeDtypeStruct((S, MLA_KV_RANK), F32),
                   jax.ShapeDtypeStruct((S, LANES), F32), jax.ShapeDtypeStruct((1, MLA_Q_RANK), F32),
                   jax.ShapeDtypeStruct((1, MLA_KV_RANK), F32)],
        compiler_params=_cparams(("arbitrary",)),
    )(proj, proj, qg, kvg, *tabs, dqn, dkvn, dk_all)


def _rope_bwd_all(dq_all, tabs, scale, tm=512):
    S, W = dq_all.shape

    def body(dq_ref, c_ref, s1_ref, s2_ref, o_ref):
        c, s1, s2 = scale * c_ref[...], -scale * s1_ref[...], -scale * s2_ref[...]
        for h in range(W // LANES):
            o_ref[:, h * LANES:(h + 1) * LANES] = _rope_block(dq_ref[:, h * LANES:(h + 1) * LANES], c, s1, s2).astype(BF16)

    tab = pl.BlockSpec((tm, LANES), lambda i: (i, 0))
    row = pl.BlockSpec((tm, W), lambda i: (i, 0))
    return pl.pallas_call(body, name="rope_bwd", grid=(S // tm,), in_specs=[row, tab, tab, tab], out_specs=row,
                          out_shape=jax.ShapeDtypeStruct((S, W), BF16), compiler_params=_cparams(("parallel",)))(dq_all, *tabs)


ATT_SCALE = MLA_QK ** -0.5
LN2 = math.log(2.0)
Q_PRESCALE = ATT_SCALE / LN2
DQ_POSTSCALE = ATT_SCALE / LN2
N_PAIR = MLA_HEADS // 2


def _causal_mask(qi, ki, tq, tk):
    row = qi * tq + lax.broadcasted_iota(jnp.int32, (tq, tk), 0)
    col = ki * tk + lax.broadcasted_iota(jnp.int32, (tq, tk), 1)
    return col <= row


def _lane_tile(x, n):
    return jnp.concatenate([x] * n, axis=1) if n > 1 else x


def _attn_fwd(q_all, kv_all, cat, tq=512, tk=2048):
    S = q_all.shape[0]
    tq, tk = min(tq, S), min(tk, S)
    nq, nk, nb, r = S // tq, S // tk, tk // LANES, tk // tq

    def body(q_ref, k_ref, v_ref, cat_in, o_ref, lse_ref, cat_ref, m_s, l_s, acc_s):
        qi, ki = pl.program_id(1), pl.program_id(2)
        last = lax.div(qi, r)

        @pl.when(ki == 0)
        def _():
            m_s[...] = jnp.full(m_s.shape, NEG, F32)
            l_s[...] = jnp.zeros(l_s.shape, F32)
            acc_s[...] = jnp.zeros(acc_s.shape, F32)

        def block(kc, mask):
            v = v_ref[0:kc, :]
            for hh in range(2):
                q = q_ref[:, hh * LANES:(hh + 1) * LANES]
                k = k_ref[0:kc, hh * LANES:(hh + 1) * LANES]
                s = lax.dot_general(q, k, (((1,), (1,)), ((), ())), preferred_element_type=F32)
                if mask is not None:
                    s = jnp.where(mask, s, NEG)
                m_prev = m_s[hh]
                m_new = jnp.maximum(m_prev, jnp.max(s, axis=-1, keepdims=True))
                p = jnp.exp2(s - _lane_tile(m_new, kc // LANES))
                alpha = jnp.exp2(m_prev - m_new)
                ps = p[:, :LANES]
                for j in range(1, kc // LANES):
                    ps = ps + p[:, j * LANES:(j + 1) * LANES]
                l_s[hh] = alpha * l_s[hh] + ps
                acc_s[hh] = alpha * acc_s[hh] + jnp.dot(p.astype(BF16), v, preferred_element_type=F32)
                m_s[hh] = m_new

        @pl.when(ki < last)
        def _():
            block(tk, None)

        for j in range(r):
            @pl.when((ki == last) & (lax.rem(qi, r) == j))
            def _(j=j):
                kc = (j + 1) * tq
                block(kc, lax.broadcasted_iota(jnp.int32, (tq, kc), 1) <= j * tq + lax.broadcasted_iota(jnp.int32, (tq, kc), 0))

        @pl.when(ki == last)
        def _():
            first = lax.broadcasted_iota(jnp.int32, (tq, LANES), 1) < MLA_V
            l0 = jnp.sum(l_s[0], axis=-1, keepdims=True)
            l1 = jnp.sum(l_s[1], axis=-1, keepdims=True)
            o = jnp.where(first, acc_s[0] / l0, acc_s[1] / l1)
            o_ref[...] = o
            cat_ref[...] = o.astype(BF16)
            lse_ref[:, :LANES] = m_s[0] + jnp.log2(l0)
            lse_ref[:, LANES:] = m_s[1] + jnp.log2(l1)

    return pl.pallas_call(
        body, name="mla_attn_fwd", grid=(N_PAIR, nq, nk),
        in_specs=[pl.BlockSpec((tq, 2 * LANES), lambda p, qi, ki: (qi, p)),
                  pl.BlockSpec((tk, 2 * LANES), lambda p, qi, ki: (jnp.minimum(ki, lax.div(qi, r)), p)),
                  pl.BlockSpec((tk, LANES), lambda p, qi, ki: (jnp.minimum(ki, lax.div(qi, r)), MLA_HEADS + p)), ANY],
        out_specs=[pl.BlockSpec((tq, LANES), lambda p, qi, ki: (qi, p)), pl.BlockSpec((tq, 2 * LANES), lambda p, qi, ki: (qi, p)),
                   pl.BlockSpec((tq, LANES), lambda p, qi, ki: (qi, N_PAIR + p))],
        out_shape=[jax.ShapeDtypeStruct((S, MLA_HEADS * MLA_V), F32), jax.ShapeDtypeStruct((S, MLA_HEADS * LANES), F32),
                   jax.ShapeDtypeStruct(cat.shape, cat.dtype)],
        input_output_aliases={3: 2},
        scratch_shapes=[pltpu.VMEM((2, tq, LANES), F32), pltpu.VMEM((2, tq, LANES), F32), pltpu.VMEM((2, tq, LANES), F32)],
        compiler_params=_cparams(("parallel", "parallel", "arbitrary")),
    )(q_all, kv_all, kv_all, cat)


def _attn_delta(do, o):
    prod = do * o
    tm = prod.shape[0]
    first = lax.broadcasted_iota(jnp.int32, (tm, LANES), 1) < MLA_V
    blocks = []
    for p in range(N_PAIR):
        pp = prod[:, p * LANES:(p + 1) * LANES]
        blocks.append(jnp.broadcast_to(jnp.sum(jnp.where(first, pp, 0.0), axis=-1, keepdims=True), (tm, LANES)))
        blocks.append(jnp.broadcast_to(jnp.sum(jnp.where(first, 0.0, pp), axis=-1, keepdims=True), (tm, LANES)))
    return jnp.concatenate(blocks, axis=1)


def _attn_bwd(q_all, kv_all, dcat, lse, delta, tq=1024, tk=1024):
    S = q_all.shape[0]
    tq, tk = min(tq, S), min(tk, S)
    nq, nk, nb = S // tq, S // tk, tk // LANES
    assert tq == tk

    def body(q_ref, k_ref, v_ref, do_ref, lse_ref, dl_ref, dq_ref, dk_ref, dv_ref, dk_s, dv_s):
        ki, qi = pl.program_id(1), pl.program_id(2)

        @pl.when((ki == 0) & (qi == 0))
        def _():
            dq_ref[...] = jnp.zeros(dq_ref.shape, F32)

        @pl.when(qi == 0)
        def _():
            dk_s[...] = jnp.zeros(dk_s.shape, F32)
            dv_s[...] = jnp.zeros(dv_s.shape, F32)

        def block(r0, nr, kc, mask):
            v, do = v_ref[0:kc, :], do_ref[r0:r0 + nr, :]
            first = lax.broadcasted_iota(jnp.int32, (nr, LANES), 1) < MLA_V
            firstk = lax.broadcasted_iota(jnp.int32, (kc, LANES), 1) < MLA_V
            do_b = do.astype(BF16)
            rows = pl.ds(pl.multiple_of(qi * tq + r0, LANES), nr)
            for hh in range(2):
                sl = slice(hh * LANES, (hh + 1) * LANES)
                q, k = q_ref[r0:r0 + nr, sl], k_ref[0:kc, sl]
                s = lax.dot_general(q, k, (((1,), (1,)), ((), ())), preferred_element_type=F32)
                if mask is not None:
                    s = jnp.where(mask, s, NEG)
                p = jnp.exp2(s - _lane_tile(lse_ref[r0:r0 + nr, sl], kc // LANES))
                do_h = jnp.where(first if hh == 0 else ~first, do, 0.0).astype(BF16)
                dp = lax.dot_general(do_h, v, (((1,), (1,)), ((), ())), preferred_element_type=F32)
                ds_b = (p * (dp - _lane_tile(dl_ref[r0:r0 + nr, sl], kc // LANES)) * LN2).astype(BF16)
                pv = lax.dot_general(p.astype(BF16), do_b, (((0,), (0,)), ((), ())), preferred_element_type=F32)
                dv_s[0:kc, :] += jnp.where(firstk if hh == 0 else ~firstk, pv, 0.0)
                dk_s[0:kc, sl] += lax.dot_general(ds_b, q, (((0,), (0,)), ((), ())), preferred_element_type=F32)
                dq_ref[rows, sl] += jnp.dot(ds_b, k, preferred_element_type=F32)

        @pl.when(qi > ki)
        def _():
            block(0, tq, tk, None)

        @pl.when(qi == ki)
        def _():
            h = tq // 2
            block(0, h, h, _causal_mask(0, 0, h, h))
            block(h, h, tk, lax.broadcasted_iota(jnp.int32, (h, tk), 1) <= h + lax.broadcasted_iota(jnp.int32, (h, tk), 0))

        @pl.when(qi == nq - 1)
        def _():
            dk_ref[...] = dk_s[...]
            dv_ref[...] = dv_s[...]

    wide = pl.BlockSpec((tq, 2 * LANES), lambda p, ki, qi: (jnp.maximum(qi, ki), p))
    return pl.pallas_call(
        body, name="mla_attn_bwd", grid=(N_PAIR, nk, nq),
        in_specs=[wide, pl.BlockSpec((tk, 2 * LANES), lambda p, ki, qi: (ki, p)),
                  pl.BlockSpec((tk, LANES), lambda p, ki, qi: (ki, MLA_HEADS + p)),
                  pl.BlockSpec((tq, LANES), lambda p, ki, qi: (jnp.maximum(qi, ki), N_PAIR + p)), wide, wide],
        out_specs=[pl.BlockSpec((S, 2 * LANES), lambda p, ki, qi: (0, p)),
                   pl.BlockSpec((tk, 2 * LANES), lambda p, ki, qi: (ki, p)), pl.BlockSpec((tk, LANES), lambda p, ki, qi: (ki, p))],
        out_shape=[jax.ShapeDtypeStruct((S, MLA_HEADS * LANES), F32), jax.ShapeDtypeStruct((S, MLA_HEADS * LANES), F32),
                   jax.ShapeDtypeStruct((S, MLA_HEADS * MLA_V), F32)],
        scratch_shapes=[pltpu.VMEM((tk, 2 * LANES), F32), pltpu.VMEM((tk, LANES), F32)],
        compiler_params=_cparams(("parallel", "arbitrary", "arbitrary")),
    )(q_all, kv_all, kv_all, dcat, lse, delta)


MEM_SCALE = MEM_HEAD_DIM ** -0.5


def _mem_probs(q, k):
    s = lax.dot_general(q, k, (((1,), (1,)), ((), ())), preferred_element_type=F32) * MEM_SCALE
    e = jnp.exp(s - jnp.max(s, axis=-1, keepdims=True))
    return e / jnp.sum(e, axis=-1, keepdims=True)


def _mem_attn_fwd(qm, km, vm, tq=512):
    S, W = qm.shape
    M = km.shape[0]

    def body(q_ref, k_ref, v_ref, o_ref):
        for h in range(MEM_HEADS):
            sl = slice(h * MEM_HEAD_DIM, (h + 1) * MEM_HEAD_DIM)
            p = _mem_probs(q_ref[:, sl], k_ref[:, sl])
            o_ref[:, sl] = jnp.dot(p.astype(BF16), v_ref[:, sl], preferred_element_type=F32).astype(BF16)

    row = pl.BlockSpec((tq, W), lambda i: (i, 0))
    full = pl.BlockSpec((M, W), lambda i: (0, 0))
    return pl.pallas_call(body, name="mem_attn_fwd", grid=(S // tq,), in_specs=[row, full, full], out_specs=row,
                          out_shape=jax.ShapeDtypeStruct((S, W), BF16), compiler_params=_cparams(("parallel",)))(qm, km, vm)


def _mem_attn_bwd(qm, km, vm, dom, tq=512):
    S, W = qm.shape
    M = km.shape[0]

    def body(q_ref, k_ref, v_ref, do_ref, dq_ref, dk_ref, dv_ref):
        i = pl.program_id(0)

        @pl.when(i == 0)
        def _():
            dk_ref[...] = jnp.zeros(dk_ref.shape, F32)
            dv_ref[...] = jnp.zeros(dv_ref.shape, F32)

        for h in range(MEM_HEADS):
            sl = slice(h * MEM_HEAD_DIM, (h + 1) * MEM_HEAD_DIM)
            q, k, v, do = q_ref[:, sl], k_ref[:, sl], v_ref[:, sl], do_ref[:, sl]
            p = _mem_probs(q, k)
            dv_ref[:, sl] += lax.dot_general(p.astype(BF16), do, (((0,), (0,)), ((), ())), preferred_element_type=F32)
            dp = lax.dot_general(do, v, (((1,), (1,)), ((), ())), preferred_element_type=F32)
            ds = (p * (dp - jnp.sum(dp * p, axis=-1, keepdims=True)) * MEM_SCALE).astype(BF16)
            dq_ref[:, sl] = jnp.dot(ds, k, preferred_element_type=F32).astype(BF16)
            dk_ref[:, sl] += lax.dot_general(ds, q, (((0,), (0,)), ((), ())), preferred_element_type=F32)

    row = pl.BlockSpec((tq, W), lambda i: (i, 0))
    full = pl.BlockSpec((M, W), lambda i: (0, 0))
    return pl.pallas_call(
        body, name="mem_attn_bwd", grid=(S // tq,), in_specs=[row, full, full, row], out_specs=[row, full, full],
        out_shape=[jax.ShapeDtypeStruct((S, W), BF16), jax.ShapeDtypeStruct((M, W), F32), jax.ShapeDtypeStruct((M, W), F32)],
        compiler_params=_cparams(("arbitrary",)),
    )(qm, km, vm, dom)


L = SSD_CHUNK
N_SPAIR = SSD_HEADS // 2
GRP_W = SSD_INNER // 2
XB0, XC0 = SSD_INNER, SSD_INNER + 2 * SSD_STATE


def _cumsum_rows(a, reverse=False):
    row = lax.broadcasted_iota(jnp.int32, a.shape, 0)
    x, sft = a, 1
    while sft < L:
        if reverse:
            x = x + jnp.where(row < L - sft, pltpu.roll(x, L - sft, axis=0), 0.0)
        else:
            x = x + jnp.where(row >= sft, pltpu.roll(x, sft, axis=0), 0.0)
        sft *= 2
    return x


def _shift_down(cur, prev, s):
    if s == 0:
        return cur
    row = lax.broadcasted_iota(jnp.int32, cur.shape, 0)
    return jnp.where(row < s, pltpu.roll(prev, s, axis=0), pltpu.roll(cur, s, axis=0))


def _shift_up(cur, nxt, s):
    if s == 0:
        return cur
    row = lax.broadcasted_iota(jnp.int32, cur.shape, 0)
    return jnp.where(row >= L - s, pltpu.roll(nxt, L - s, axis=0), pltpu.roll(cur, L - s, axis=0))


def _ssd_conv(u, prev, cw, cb):
    delayed = [u] + [_shift_down(u, prev, s) for s in range(1, SSD_CONV)]
    conv = cb + cw[SSD_CONV - 1:SSD_CONV, :] * u
    for s in range(1, SSD_CONV):
        conv = conv + cw[SSD_CONV - 1 - s:SSD_CONV - s, :] * delayed[s]
    return conv, delayed


def _pair_lanes(v, h0, first):
    return jnp.where(first, v[:, h0:h0 + 1], v[:, h0 + 1:h0 + 2])


def _ssd_common(u, prev, dt_raw, cw, cb, dtb, alog):
    conv, delayed = _ssd_conv(u, prev, cw, cb)
    sg = _sigmoid(conv)
    xa = conv * sg
    dpre = dt_raw + dtb
    dtv = jnp.maximum(dpre, 0.0) + jnp.log1p(jnp.exp(-jnp.abs(dpre)))
    a_row = -jnp.exp(alog)
    cs = _cumsum_rows(dtv * a_row)
    return conv, sg, xa, dpre, dtv, a_row, cs, delayed


def _ssd_pair_fwd(xa, dtv, cs, csT, G, Cg, Bg, Sp, dsk, pp, first, tri, rowfirst):
    h0 = 2 * pp
    x = xa[:, pp * LANES:(pp + 1) * LANES]
    xdt = x * _pair_lanes(dtv, h0, first)
    xdt_b = xdt.astype(BF16)
    Ms, yd = [], []
    for h in (h0, h0 + 1):
        lam = jnp.exp(jnp.where(tri, cs[:, h:h + 1] - csT[h:h + 1, :], NEG))
        M = G * lam
        Ms.append((M, lam))
        yd.append(jnp.dot(M.astype(BF16), xdt_b, preferred_element_type=F32))
    T = lax.dot_general(Cg, Sp.astype(BF16), (((1,), (1,)), ((), ())), preferred_element_type=F32)
    E = jnp.exp(_pair_lanes(cs, h0, first))
    yoff = E * T
    csl = cs[L - 1:L, :]
    Fd = jnp.exp(_pair_lanes(csl, h0, first) - _pair_lanes(cs, h0, first))
    el = jnp.exp(csl)
    el_rows = jnp.where(rowfirst, el[:, h0:h0 + 1], el[:, h0 + 1:h0 + 2])
    Sloc = lax.dot_general((xdt * Fd).astype(BF16), Bg, (((0,), (0,)), ((), ())), preferred_element_type=F32)
    S_new = el_rows * Sp + Sloc
    y = jnp.where(first, yd[0], yd[1]) + yoff + x * _pair_lanes(dsk, h0, first[:1])
    return y, S_new, (x, xdt, xdt_b, Ms, E, yoff, Fd, el, el_rows)


def _ssd_masks():
    lane = lax.broadcasted_iota(jnp.int32, (L, LANES), 1)
    row = lax.broadcasted_iota(jnp.int32, (L, LANES), 0)
    return lane, row, lane < SSD_HEAD_DIM, row >= lane, row[:, :1] < SSD_HEAD_DIM


def _ssd_specs(nc, rev):
    def cidx(i):
        return nc - 1 - i if rev else i
    z = pl.BlockSpec((L, SSD_INNER), lambda i: (cidx(i), PZ // SSD_INNER))
    u = pl.BlockSpec((L, SSD_XBC), lambda i: (cidx(i), PX // SSD_XBC))
    dt = pl.BlockSpec((L, LANES), lambda i: (cidx(i), PDT // LANES))
    return cidx, z, u, dt


def _vec(w):
    return pl.BlockSpec((1, w), lambda i: (0, 0))


def _ssd_fwd(proj, cw, cb, dtb, alog, dsk, ng):
    S = proj.shape[0]
    nc = S // L

    def body(z_ref, u_ref, dt_ref, cw_ref, cb_ref, dtb_ref, alog_ref, dsk_ref, ng_ref, y_ref, st_ref, prev_s, state_s):
        c = pl.program_id(0)

        @pl.when(c == 0)
        def _():
            prev_s[...] = jnp.zeros(prev_s.shape, F32)
            state_s[...] = jnp.zeros(state_s.shape, F32)

        u = u_ref[...]
        _, _, xa, _, dtv, _, cs, _ = _ssd_common(u, prev_s[...], dt_ref[...], cw_ref[...], cb_ref[...], dtb_ref[...], alog_ref[...])
        prev_s[...] = u
        csT = cs.T
        _, _, first, tri, rowfirst = _ssd_masks()
        dsk_v = dsk_ref[...]
        ys = []
        for g in range(2):
            Bg = xa[:, XB0 + g * SSD_STATE:XB0 + (g + 1) * SSD_STATE].astype(BF16)
            Cg = xa[:, XC0 + g * SSD_STATE:XC0 + (g + 1) * SSD_STATE].astype(BF16)
            G = lax.dot_general(Cg, Bg, (((1,), (1,)), ((), ())), preferred_element_type=F32)
            for pp in (2 * g, 2 * g + 1):
                Sp = state_s[pp]
                st_ref[pp * LANES:(pp + 1) * LANES, :] = Sp
                y, S_new, _ = _ssd_pair_fwd(xa, dtv, cs, csT, G, Cg, Bg, Sp, dsk_v, pp, first, tri, rowfirst)
                state_s[pp] = S_new
                ys.append(y)
        z = z_ref[...]
        for g in range(2):
            yg = jnp.concatenate([ys[2 * g], ys[2 * g + 1]], axis=1)
            zg = z[:, g * GRP_W:(g + 1) * GRP_W]
            gated = yg * (zg * _sigmoid(zg))
            r = lax.rsqrt(jnp.mean(gated * gated, axis=-1, keepdims=True) + RMS_EPS)
            y_ref[:, g * GRP_W:(g + 1) * GRP_W] = (gated * r * ng_ref[:, g * GRP_W:(g + 1) * GRP_W]).astype(BF16)

    _, zs, us, dts = _ssd_specs(nc, False)
    return pl.pallas_call(
        body, name="ssd_fwd", grid=(nc,),
        in_specs=[zs, us, dts, pl.BlockSpec((8, SSD_XBC), lambda i: (0, 0)), _vec(SSD_XBC), _vec(LANES), _vec(LANES), _vec(LANES),
                  _vec(SSD_INNER)],
        out_specs=[pl.BlockSpec((L, SSD_INNER), lambda i: (i, 0)), pl.BlockSpec((N_SPAIR * LANES, SSD_STATE), lambda i: (i, 0))],
        out_shape=[jax.ShapeDtypeStruct((S, 2 * SSD_INNER), BF16), jax.ShapeDtypeStruct((nc * N_SPAIR * LANES, SSD_STATE), F32)],
        scratch_shapes=[pltpu.VMEM((L, SSD_XBC), F32), pltpu.VMEM((N_SPAIR, LANES, SSD_STATE), F32)],
        compiler_params=_cparams(("arbitrary",)),
    )(proj, proj, proj, cw, cb, dtb, alog, dsk, ng)


def _ssd_bwd(proj, states, dy, cw, cb, dtb, alog, dsk, ng):
    S = proj.shape[0]
    nc = S // L

    def body(z_ref, u_ref, up_ref, dt_ref, st_ref, dy_ref, cw_ref, cb_ref, dtb_ref, alog_ref, dsk_ref, ng_ref,
             dz_ref, du_ref, ddt_ref, dcw_ref, dcb_ref, ddtb_ref, dalog_ref, ddsk_ref, dng_ref,
             dS_s, dconv_s, dD_s):
        i = pl.program_id(0)
        c = nc - 1 - i

        @pl.when(i == 0)
        def _():
            dS_s[...] = jnp.zeros(dS_s.shape, F32)
            dconv_s[...] = jnp.zeros(dconv_s.shape, F32)
            dD_s[...] = jnp.zeros(dD_s.shape, F32)
            for r in (dcw_ref, dcb_ref, ddtb_ref, dalog_ref, ddsk_ref, dng_ref):
                r[...] = jnp.zeros(r.shape, F32)

        u = u_ref[...]
        prev = jnp.where(c > 0, up_ref[...], 0.0)
        cw_v = cw_ref[...]
        conv, sg, xa, dpre, dtv, a_row, cs, delayed = _ssd_common(u, prev, dt_ref[...], cw_v, cb_ref[...], dtb_ref[...], alog_ref[...])
        csT = cs.T
        lane, row, first, tri, rowfirst = _ssd_masks()
        dsk_v = dsk_ref[...]

        fw = []
        Gs, Bs, Cs = [], [], []
        for g in range(2):
            Bg = xa[:, XB0 + g * SSD_STATE:XB0 + (g + 1) * SSD_STATE].astype(BF16)
            Cg = xa[:, XC0 + g * SSD_STATE:XC0 + (g + 1) * SSD_STATE].astype(BF16)
            G = lax.dot_general(Cg, Bg, (((1,), (1,)), ((), ())), preferred_element_type=F32)
            Gs.append(G), Bs.append(Bg), Cs.append(Cg)
            for pp in (2 * g, 2 * g + 1):
                Sp = st_ref[pp * LANES:(pp + 1) * LANES, :]
                y, _, keep = _ssd_pair_fwd(xa, dtv, cs, csT, G, Cg, Bg, Sp, dsk_v, pp, first, tri, rowfirst)
                fw.append((y, Sp, keep))

        z = z_ref[...]
        dys = []
        for g in range(2):
            sl = slice(g * GRP_W, (g + 1) * GRP_W)
            yg = jnp.concatenate([fw[2 * g][0], fw[2 * g + 1][0]], axis=1)
            zg = z[:, sl]
            sz = _sigmoid(zg)
            silu_z = zg * sz
            gated = yg * silu_z
            r = lax.rsqrt(jnp.mean(gated * gated, axis=-1, keepdims=True) + RMS_EPS)
            nh = gated * r
            dout = dy_ref[:, sl]
            dng_ref[:, sl] += jnp.sum(dout * nh, axis=0, keepdims=True)
            dnh = dout * ng_ref[:, sl]
            dgated = r * (dnh - nh * jnp.mean(dnh * nh, axis=-1, keepdims=True))
            dz_ref[:, sl] = dgated * yg * (sz * (1.0 + zg * (1.0 - sz)))
            dyg = dgated * silu_z
            dys.append(dyg[:, :LANES]), dys.append(dyg[:, LANES:])

        dcs_col = [0.0] * SSD_HEADS
        dcs_row = [None] * SSD_HEADS
        ddt_col = [None] * SSD_HEADS
        dxs = []
        dB, dC = [None, None], [None, None]
        last = row == L - 1
        for g in range(2):
            Bg, Cg, G = Bs[g], Cs[g], Gs[g]
            dG = jnp.zeros((L, L), F32)
            dBg = jnp.zeros((L, SSD_STATE), F32)
            dCg = jnp.zeros((L, SSD_STATE), F32)
            for pp in (2 * g, 2 * g + 1):
                h0 = 2 * pp
                y, Sp, (x, xdt, xdt_b, Ms, E, yoff, Fd, el, el_rows) = fw[pp]
                dY = dys[pp]
                dS = dS_s[pp]
                dS_b, Sp_b = dS.astype(BF16), Sp.astype(BF16)
                dD_s[:, pp * LANES:(pp + 1) * LANES] += jnp.sum(dY * x, axis=0, keepdims=True)
                dx = dY * _pair_lanes(dsk_v, h0, first[:1])
                dxdt = jnp.zeros((L, LANES), F32)
                dY_b = dY.astype(BF16)
                for hh, h in enumerate((h0, h0 + 1)):
                    hm = first if hh == 0 else ~first
                    M, lam = Ms[hh]
                    dYh = jnp.where(hm, dY, 0.0).astype(BF16)
                    dM = lax.dot_general(dYh, xdt_b, (((1,), (1,)), ((), ())), preferred_element_type=F32)
                    W = dM * M
                    dcs_col[h] = dcs_col[h] + jnp.sum(W, axis=-1, keepdims=True)
                    dcs_row[h] = jnp.sum(W, axis=0, keepdims=True)
                    dG = dG + dM * lam
                    mt = lax.dot_general(M.astype(BF16), dY_b, (((0,), (0,)), ((), ())), preferred_element_type=F32)
                    dxdt = dxdt + jnp.where(hm, mt, 0.0)
                dT = (E * dY).astype(BF16)
                dCg = dCg + jnp.dot(dT, Sp_b, preferred_element_type=F32)
                dS_in = lax.dot_general(dT, Cg, (((0,), (0,)), ((), ())), preferred_element_type=F32) + el_rows * dS
                q1 = dY * yoff
                dZ = lax.dot_general(Bg, dS_b, (((1,), (1,)), ((), ())), preferred_element_type=F32)
                dBg = dBg + jnp.dot((xdt * Fd).astype(BF16), dS_b, preferred_element_type=F32)
                dxdt = dxdt + dZ * Fd
                q2 = dZ * xdt * Fd
                dSS = dS * Sp
                for hh, h in enumerate((h0, h0 + 1)):
                    hm = first if hh == 0 else ~first
                    rs1 = jnp.sum(jnp.where(hm, q1, 0.0), axis=-1, keepdims=True)
                    rs2 = jnp.sum(jnp.where(hm, q2, 0.0), axis=-1, keepdims=True)
                    rmask = rowfirst if hh == 0 else ~rowfirst
                    d_el = jnp.sum(jnp.sum(jnp.where(rmask, dSS, 0.0), axis=-1, keepdims=True), axis=0, keepdims=True)
                    tail = jnp.sum(rs2, axis=0, keepdims=True) + d_el * el[:, h:h + 1]
                    dcs_col[h] = dcs_col[h] + (rs1 - rs2 + jnp.where(last[:, :1], tail, 0.0))
                    ddt_col[h] = jnp.sum(jnp.where(hm, dxdt * x, 0.0), axis=-1, keepdims=True)
                dS_s[pp] = dS_in
                dxs.append(dx + dxdt * _pair_lanes(dtv, h0, first))
            dG_b = dG.astype(BF16)
            dC[g] = dCg + jnp.dot(dG_b, Bg, preferred_element_type=F32)
            dB[g] = dBg + lax.dot_general(dG_b, Cg, (((0,), (0,)), ((), ())), preferred_element_type=F32)

        dcs_c, dcs_r, ddt_c = (jnp.zeros((L, LANES), F32) for _ in range(3))
        for h in range(SSD_HEADS):
            dcs_c = dcs_c + jnp.where(lane == h, dcs_col[h], 0.0)
            dcs_r = dcs_r + jnp.where(row == h, dcs_row[h], 0.0)
            ddt_c = ddt_c + jnp.where(lane == h, ddt_col[h], 0.0)
        dcs = dcs_c - dcs_r.T
        da = _cumsum_rows(dcs, reverse=True)
        ddt_c = ddt_c + da * a_row
        dalog_ref[...] += jnp.sum(da * dtv, axis=0, keepdims=True) * a_row
        ddt_raw = ddt_c * _sigmoid(dpre)
        ddt_ref[...] = ddt_raw
        ddtb_ref[...] += jnp.sum(ddt_raw, axis=0, keepdims=True)

        dxa = jnp.concatenate(dxs + dB + dC, axis=1)
        dconv = dxa * (sg * (1.0 + conv * (1.0 - sg)))
        dcb_ref[...] += jnp.sum(dconv, axis=0, keepdims=True)
        nxt = dconv_s[...]
        du = cw_v[SSD_CONV - 1:SSD_CONV, :] * dconv
        dcw_ref[SSD_CONV - 1:SSD_CONV, :] += jnp.sum(dconv * u, axis=0, keepdims=True)
        for s in range(1, SSD_CONV):
            k = SSD_CONV - 1 - s
            du = du + cw_v[k:k + 1, :] * _shift_up(dconv, nxt, s)
            dcw_ref[k:k + 1, :] += jnp.sum(dconv * delayed[s], axis=0, keepdims=True)
        du_ref[...] = du
        dconv_s[...] = dconv

        @pl.when(i == nc - 1)
        def _():
            acc = dD_s[...]
            lane1 = lax.broadcasted_iota(jnp.int32, (1, LANES), 1)
            lanew = lax.broadcasted_iota(jnp.int32, acc.shape, 1)
            out = jnp.zeros((1, LANES), F32)
            for h in range(SSD_HEADS):
                tot = jnp.sum(jnp.where((lanew >= h * SSD_HEAD_DIM) & (lanew < (h + 1) * SSD_HEAD_DIM), acc, 0.0),
                              axis=-1, keepdims=True)
                out = out + jnp.where(lane1 == h, tot, 0.0)
            ddsk_ref[...] = out

    cidx, zs, us, dts = _ssd_specs(nc, True)
    ups = pl.BlockSpec((L, SSD_XBC), lambda i: (jnp.maximum(cidx(i) - 1, 0), PX // SSD_XBC))
    rowc = lambda w: pl.BlockSpec((L, w), lambda i: (cidx(i), 0))
    return pl.pallas_call(
        body, name="ssd_bwd", grid=(nc,),
        in_specs=[zs, us, ups, dts, pl.BlockSpec((N_SPAIR * LANES, SSD_STATE), lambda i: (cidx(i), 0)), rowc(SSD_INNER),
                  pl.BlockSpec((8, SSD_XBC), lambda i: (0, 0)), _vec(SSD_XBC), _vec(LANES), _vec(LANES), _vec(LANES), _vec(SSD_INNER)],
        out_specs=[rowc(SSD_INNER), rowc(SSD_XBC), rowc(LANES), pl.BlockSpec((8, SSD_XBC), lambda i: (0, 0)), _vec(SSD_XBC),
                   _vec(LANES), _vec(LANES), _vec(LANES), _vec(SSD_INNER)],
        out_shape=[jax.ShapeDtypeStruct((S, SSD_INNER), F32), jax.ShapeDtypeStruct((S, SSD_XBC), F32),
                   jax.ShapeDtypeStruct((S, LANES), F32), jax.ShapeDtypeStruct((8, SSD_XBC), F32),
                   jax.ShapeDtypeStruct((1, SSD_XBC), F32), jax.ShapeDtypeStruct((1, LANES), F32),
                   jax.ShapeDtypeStruct((1, LANES), F32), jax.ShapeDtypeStruct((1, LANES), F32),
                   jax.ShapeDtypeStruct((1, SSD_INNER), F32)],
        scratch_shapes=[pltpu.VMEM((N_SPAIR, LANES, SSD_STATE), F32), pltpu.VMEM((L, SSD_XBC), F32),
                        pltpu.VMEM((1, SSD_INNER), F32)],
        compiler_params=_cparams(("arbitrary",)),
    )(proj, proj, proj, proj, states, dy, cw, cb, dtb, alog, dsk, ng)


_IN_SEGS = ((PZ, 0, 512), (PX, 512, 1024), (PDT, 1536, 8), (PQ, 1544, 384), (PKV, 1928, 256), (PKR + KR_LANE, 2184, 32))


def _pad_w_in(w):
    parts, at = [], 0
    for dst, src, n in sorted(_IN_SEGS):
        parts += [jnp.zeros((w.shape[0], dst - at), w.dtype), w[:, src:src + n]]
        at = dst + n
    return jnp.concatenate(parts + [jnp.zeros((w.shape[0], PW - at), w.dtype)], axis=1)


def _unpad_w_in(wp):
    segs = sorted(_IN_SEGS, key=lambda t: t[1])
    return jnp.concatenate([wp[:, dst:dst + n] for dst, src, n in segs], axis=1)


def _pad_w_q(w):
    return jnp.pad(w.reshape(MLA_Q_RANK, MLA_HEADS, MLA_QK), ((0, 0), (0, 0), (0, LANES - MLA_QK))).reshape(MLA_Q_RANK, MLA_HEADS * LANES)


def _unpad_w_q(wp):
    return wp.reshape(MLA_Q_RANK, MLA_HEADS, LANES)[:, :, :MLA_QK].reshape(MLA_Q_RANK, MLA_HEADS * MLA_QK)


def _pad_w_kv(w):
    w3 = w.reshape(MLA_KV_RANK, MLA_HEADS, MLA_NOPE + MLA_V)
    k = jnp.pad(w3[:, :, :MLA_NOPE], ((0, 0), (0, 0), (0, LANES - MLA_NOPE))).reshape(MLA_KV_RANK, MLA_HEADS * LANES)
    return jnp.concatenate([k, w3[:, :, MLA_NOPE:].reshape(MLA_KV_RANK, MLA_HEADS * MLA_V)], axis=1)


def _unpad_w_kv(wp):
    k = wp[:, :MLA_HEADS * LANES].reshape(MLA_KV_RANK, MLA_HEADS, LANES)[:, :, :MLA_NOPE]
    v = wp[:, MLA_HEADS * LANES:].reshape(MLA_KV_RANK, MLA_HEADS, MLA_V)
    return jnp.concatenate([k, v], axis=2).reshape(MLA_KV_RANK, MLA_HEADS * (MLA_NOPE + MLA_V))


def _head_lanes(v):
    return jnp.pad(v, ((0, 0), (0, LANES - v.shape[1])))


def _local_step(x, mem, positions, tgt, P, weights_at=None, emit=None):
    tabs = _rope_tables(positions)
    G = {}
    emit = emit or (lambda names, grads: 0.0)

    h0, h0b = _ln_fwd([(x, 1.0)], P["ln_in_g"], P["ln_in_b"], "ln_in")
    if weights_at is not None:
        P = {**P, **weights_at("first", (h0b,) + tuple(tabs))}
    proj = _mm(h0b, P["w_in"], "nn", "proj_in", tm=1024, tn=PW // 2)
    if weights_at is not None:
        P = {**P, **weights_at("mid", proj)}
    cat, states = _ssd_fwd(proj, P["conv_w"], P["conv_b"], P["dt_bias"], P["a_log"], P["d_skip"], P["ssd_norm_g"])
    qn, kvn, kpe = _mla_prep(proj, P["q_norm_g"], P["kv_norm_g"], tabs)

    def q_epi(acc, c, s1, s2):
        return (jnp.concatenate([_rope_block(acc[:, h * LANES:(h + 1) * LANES], c, s1, s2) for h in range(MLA_HEADS)], axis=1)
                * Q_PRESCALE,)

    q_all = _mm(qn, P["w_q_up"], "nn", "q_up", out_dtypes=(BF16,), epi=q_epi, extras=[(t, "m") for t in tabs])

    def kv_epi(acc, kp):
        kb = [acc[:, h * LANES:(h + 1) * LANES] + kp for h in range(MLA_HEADS)]
        return (jnp.concatenate(kb + [acc[:, MLA_HEADS * LANES:]], axis=1),)

    kv_all = _mm(kvn, P["w_kv_up"], "nn", "kv_up", out_dtypes=(BF16,), epi=kv_epi, extras=[(kpe, "m")])
    o_att, lse, cat = _attn_fwd(q_all, kv_all, cat)
    def resid_ln_epi(acc, h, g, b):
        r = ALPHA * h + acc
        y = _ln_stats(r)[0] * g + b
        return r, y, y

    ln_out = (F32, F32, BF16)
    r1, h1, h1b = _mm(cat, P["w_mix_out"], "nn", "mix_out", tm=512, out_dtypes=ln_out, epi=resid_ln_epi,
                      extras=[(h0, "mn"), (P["ln1_g"], "n"), (P["ln1_b"], "n")])
    if weights_at is not None:
        P = {**P, **weights_at("late", h1b)}
    qm = _mm(h1b, P["w_mem_q"], "nn", "mem_q", tm=1024, out_dtypes=(BF16,))
    km = _mm(mem, P["w_mem_k"], "nn", "mem_k", out_dtypes=(BF16,))
    vm = _mm(mem, P["w_mem_v"], "nn", "mem_v", out_dtypes=(BF16,))
    om = _mem_attn_fwd(qm, km, vm)
    r2, h2, h2b = _mm(om, P["w_mem_o"], "nn", "mem_o", tm=512, out_dtypes=ln_out, epi=resid_ln_epi,
                      extras=[(h1, "mn"), (P["ln2_g"], "n"), (P["ln2_b"], "n")])

    def up_epi(acc):
        r = jnp.maximum(acc, 0.0)
        return r * r, 2.0 * r

    act, dact = _mm(h2b, P["w_up"], "nn", "mlp_up", tm=1024, tn=1024, out_dtypes=(BF16, BF16), epi=up_epi)

    def loss_epi(acc, h, g, b, t):
        xh, rstd = _ln_stats(ALPHA * h + acc)
        diff = xh * g + b - t
        part = 0.5 * jnp.sum(jnp.mean(diff * diff, axis=-1, keepdims=True), axis=0, keepdims=True)
        dr, pg, pb = _ln_bwd_tile(xh, rstd, diff * (1.0 / D_MODEL), g)
        return dr, dr, jnp.broadcast_to(part, pg.shape), pg, pb

    dr3, dr3b, loss, G["ln3_g"], G["ln3_b"] = _mm(
        act, P["w_down"], "nn", "mlp_down", tm=512, tk=D_FF, out_dtypes=(F32, BF16), epi=loss_epi, n_sums=3,
        extras=[(h2, "mn"), (P["ln3_g"], "n"), (P["ln3_b"], "n"), (tgt, "mn")])
    loss = loss[:, :LANES]

    def ln_bwd_epi(acc, d, r, g):
        dr, pg, pb = _ln_bwd_tile(*_ln_stats(r), ALPHA * d + acc, g)
        return dr, dr, pg, pb

    def dact_epi(acc, g):
        return (acc.astype(BF16) * g,)

    du = _mm(dr3b, P["w_down"], "nt", "mlp_down_dx", tm=1024, tn=1024, out_dtypes=(BF16,), epi=dact_epi, extras=[(dact, "mn")])
    G["w_down"] = _mm(act, dr3b, "tn", "mlp_down_dw", tm=1024, tk=TOK_K, out_dtypes=(BF16,))
    G["w_up"] = _mm(h2b, du, "tn", "mlp_up_dw", tm=1024, tn=D_FF // N_DEV, tk=TOK_K, out_dtypes=(BF16,), col_slots=True)
    tie = emit(("w_down", "w_up"), G)
    dr2, dr2b, G["ln2_g"], G["ln2_b"] = _mm(
        du, P["w_up"], "nt", "mlp_up_dx", tm=512, tk=D_FF, out_dtypes=(F32, BF16), epi=ln_bwd_epi, n_sums=2,
        extras=[(dr3, "mn"), (r2, "mn"), (P["ln2_g"] + tie, "n")])

    dom = _mm(dr2b, P["w_mem_o"], "nt", "mem_o_dx", tm=1024, out_dtypes=(BF16,))
    G["w_mem_o"] = _mm(om, dr2b, "tn", "mem_o_dw", tm=1024, tk=TOK_K, out_dtypes=(BF16,))
    dqm, dkm, dvm = _mem_attn_bwd(qm, km, vm, dom)
    G["w_mem_q"] = _mm(h1b, dqm, "tn", "mem_q_dw", tm=1024, tk=TOK_K, out_dtypes=(BF16,))
    G["w_mem_k"] = _mm(mem, dkm, "tn", "mem_k_dw", tm=1024, out_dtypes=(BF16,))
    G["w_mem_v"] = _mm(mem, dvm, "tn", "mem_v_dw", tm=1024, out_dtypes=(BF16,))
    tie = emit(("w_mem_o", "w_mem_q", "w_mem_k", "w_mem_v"), G)
    dr1, dr1b, G["ln1_g"], G["ln1_b"] = _mm(
        dqm, P["w_mem_q"], "nt", "mem_q_dx", tm=512, out_dtypes=(F32, BF16), epi=ln_bwd_epi, n_sums=2,
        extras=[(dr2, "mn"), (r1, "mn"), (P["ln1_g"] + tie, "n")])

    def dcat_epi(acc, o):
        return acc, _attn_delta(acc[:, SSD_INNER:], o)

    dcat, delta = _mm(dr1b, P["w_mix_out"], "nt", "mix_out_dx", tm=512, out_dtypes=(F32, F32), epi=dcat_epi, extras=[(o_att, "m")])
    G["w_mix_out"] = _mm(cat, dr1b, "tn", "mix_out_dw", tm=1024, tk=TOK_K, out_dtypes=(BF16,))
    dq_all, dk_all, dv_all = _attn_bwd(q_all, kv_all, dcat, lse, delta)
    dq_pre = _rope_bwd_all(dq_all, tabs, DQ_POSTSCALE)
    G["w_q_up"] = _mm(qn, dq_pre, "tn", "q_up_dw", tk=TOK_K, out_dtypes=(BF16,))
    dqn = _mm(dq_pre, P["w_q_up"], "nt", "q_up_dx", tm=1024)
    dkv_all = jnp.concatenate([dk_all, dv_all], axis=1).astype(BF16)
    G["w_kv_up"] = _mm(kvn, dkv_all, "tn", "kv_up_dw", tk=TOK_K, out_dtypes=(BF16,))
    dkvn = _mm(dkv_all, P["w_kv_up"], "nt", "kv_up_dx", tm=1024)
    dql, dkvl, dkr, G["q_norm_g"], G["kv_norm_g"] = _mla_prep_bwd(proj, P["q_norm_g"], P["kv_norm_g"], tabs, dqn, dkvn, dk_all)
    (dz, dxbc, ddt, G["conv_w"], G["conv_b"], G["dt_bias"], G["a_log"], G["d_skip"], G["ssd_norm_g"]) = _ssd_bwd(
        proj, states, dcat, P["conv_w"], P["conv_b"], P["dt_bias"], P["a_log"], P["d_skip"], P["ssd_norm_g"])
    tie = emit(("w_mix_out", "w_q_up", "w_kv_up", "conv_w"), G)
    S = x.shape[0]
    dproj = jnp.concatenate([dql, jnp.zeros((S, PZ - MLA_Q_RANK), F32) + tie, dz, dxbc, dkvl, ddt, dkr], axis=1).astype(BF16)
    G["w_in"] = _mm(h0b, dproj, "tn", "proj_in_dw", tm=1024, tn=PW // 2, tk=TOK_K, out_dtypes=(BF16,))
    tie = emit(("w_in",), G)
    gx, G["ln_in_g"], G["ln_in_b"] = _mm(
        dproj, P["w_in"], "nt", "proj_in_dx", tm=512, tk=PW, epi=lambda acc, d, r, g: ln_bwd_epi(acc, d, r, g)[1:], n_sums=2,
        extras=[(dr1, "mn"), (x, "mn"), (P["ln_in_g"] + tie, "n")])
    return loss, gx, G


PACK_W = 1024
BIG = (("w_in", (1024, 277), 1), ("conv_w", (4, 128), 1), ("w_q_up", (384, 96), 1), ("w_kv_up", (256, 128), 1),
       ("w_mix_out", (128, 1024), 0), ("w_mem_q", (128, 1024), 0), ("w_mem_k", (128, 1024), 0), ("w_mem_v", (128, 1024), 0),
       ("w_mem_o", (128, 1024), 0), ("w_up", (1024, 512), 1), ("w_down", (512, 1024), 0))
SMALL = ("ln_in_g", "ln_in_b", "conv_b", "ln1_g", "ln1_b", "ln2_g", "ln2_b", "ln3_g", "ln3_b",
         "ssd_norm_g", "q_norm_g", "kv_norm_g", "dt_bias", "a_log", "d_skip")
ALL_W = ("ln_in_g", "ln_in_b", "w_in", "conv_w", "conv_b", "dt_bias", "a_log", "d_skip", "ssd_norm_g", "q_norm_g", "w_q_up",
         "kv_norm_g", "w_kv_up", "w_mix_out", "ln1_g", "ln1_b", "w_mem_q", "w_mem_k", "w_mem_v", "w_mem_o", "ln2_g", "ln2_b",
         "w_up", "w_down", "ln3_g", "ln3_b")


SMALL_R = 16
LOSS_ROW = 15
_SMALL_ROWS = (("ln_in_g",), ("ln_in_b",), ("conv_b",), ("ln1_g",), ("ln1_b",), ("ln2_g",), ("ln2_b",), ("ln3_g",), ("ln3_b",),
               ("ssd_norm_g", "q_norm_g"), ("kv_norm_g", "dt_bias", "a_log", "d_skip"))
_SMALL_W = {"ssd_norm_g": 512, "q_norm_g": 384, "kv_norm_g": 256, "dt_bias": LANES, "a_log": LANES, "d_skip": LANES}


MESH = pl.DeviceIdType.MESH
ANY = pl.BlockSpec(memory_space=pl.ANY)
VM = pl.BlockSpec(memory_space=pltpu.VMEM)


def _coords():
    return lax.axis_index("x"), lax.axis_index("y"), lax.axis_index("c")


def _slot(px, py, pc):
    return 4 * px + 2 * py + pc


def _peer(k, x, y, c):
    dx, dy, dc = (k >> 2) & 1, (k >> 1) & 1, k & 1
    return (1 - x if dx else x, 1 - y if dy else y, 1 - c if dc else c)


def _adam(w, g, m, v):
    m = ADAM_B1 * m + (1.0 - ADAM_B1) * g
    v = ADAM_B2 * v + (1.0 - ADAM_B2) * (g * g)
    m_hat = m / (1.0 - ADAM_B1 ** ADAM_STEP)
    v_hat = v / (1.0 - ADAM_B2 ** ADAM_STEP)
    delta = -ADAM_LR * (m_hat / (jnp.sqrt(v_hat) + ADAM_EPS) + ADAM_WD * w)
    return delta, m, v


def _sum_slots(ref):
    tot = ref[0].astype(F32)
    for q in range(1, N_DEV):
        tot = tot + ref[q].astype(F32)
    return tot


def _row(v):
    return v.reshape(1, -1).astype(F32)


def kernel(x, mem, positions, ln_in_g, ln_in_b, w_in, conv_w, conv_b, dt_bias, a_log, d_skip, ssd_norm_g, q_norm_g, w_q_up, kv_norm_g, w_kv_up, w_mix_out, ln1_g, ln1_b, w_mem_q, w_mem_k, w_mem_v, w_mem_o, ln2_g, ln2_b, w_up, w_down, ln3_g, ln3_b, loss_target, m_ln_in_g, m_ln_in_b, m_w_in, m_conv_w, m_conv_b, m_dt_bias, m_a_log, m_d_skip, m_ssd_norm_g, m_q_norm_g, m_w_q_up, m_kv_norm_g, m_w_kv_up, m_w_mix_out, m_ln1_g, m_ln1_b, m_w_mem_q, m_w_mem_k, m_w_mem_v, m_w_mem_o, m_ln2_g, m_ln2_b, m_w_up, m_w_down, m_ln3_g, m_ln3_b, v_ln_in_g, v_ln_in_b, v_w_in, v_conv_w, v_conv_b, v_dt_bias, v_a_log, v_d_skip, v_ssd_norm_g, v_q_norm_g, v_w_q_up, v_kv_norm_g, v_w_kv_up, v_w_mix_out, v_ln1_g, v_ln1_b, v_w_mem_q, v_w_mem_k, v_w_mem_v, v_w_mem_o, v_ln2_g, v_ln2_b, v_w_up, v_w_down, v_ln3_g, v_ln3_b):
    a = dict(locals())
    W = {n: a[n] for n in ALL_W}
    M = {n: a["m_" + n] for n in ALL_W}
    V = {n: a["v_" + n] for n in ALL_W}
    return _step_overlapped(x, mem, positions, loss_target, W, M, V)


HBM = pl.BlockSpec(memory_space=pltpu.HBM)
SEM = pl.BlockSpec(memory_space=pltpu.SEMAPHORE)
EFFECT = pltpu.SideEffectType.DATAFLOW_SIDE_EFFECTING
SHARD_SHAPE = {n: s for n, s, _ in BIG}
SHARD_AXIS = {n: ax for n, _, ax in BIG}
GATHER_FIRST = ("w_in",)
GATHER_MID = ("conv_w", "w_q_up", "w_kv_up", "w_mix_out")
GATHER_LATE = ("w_mem_q", "w_mem_k", "w_mem_v", "w_mem_o", "w_up", "w_down")


def _my_slot():
    return _slot(*_coords())


def _group_copies(src_refs, land_refs, send_sems, recv_sems, slotted, landing_of_peer):
    x, y, c = _coords()
    my = _slot(x, y, c)
    cps = []
    for a, (s_ref, l_ref) in enumerate(zip(src_refs, land_refs)):
        for k in range(1, N_DEV):
            peer = _peer(k, x, y, c)
            cps.append(pltpu.make_async_remote_copy(
                src_ref=s_ref.at[_slot(*peer)] if slotted else s_ref,
                dst_ref=l_ref.at[_slot(*peer)] if landing_of_peer else l_ref.at[my],
                send_sem=send_sems.at[7 * a + k - 1], recv_sem=recv_sems.at[7 * a + k - 1],
                device_id=peer, device_id_type=MESH))
    return cps


def _send_start(srcs, lands, slotted, name):
    n = len(srcs)

    def body(*refs):
        for cp in _group_copies(refs[:n], refs[n:2 * n], refs[2 * n], refs[2 * n + 1], slotted, False):
            cp.start()
        refs[-1][...] = jnp.zeros(refs[-1].shape, F32)

    res = pl.pallas_call(
        body, name=name,
        out_shape=(pltpu.SemaphoreType.DMA((7 * n,)), pltpu.SemaphoreType.DMA((7 * n,)),
                   *[pltpu.HBM(a.shape, a.dtype) for a in srcs], *[pltpu.HBM(a.shape, a.dtype) for a in lands],
                   jax.ShapeDtypeStruct((8, LANES), F32)),
        in_specs=[HBM] * (2 * n), out_specs=(SEM, SEM, *[HBM] * (2 * n), VM),
        input_output_aliases={i: 2 + i for i in range(2 * n)},
        compiler_params=pltpu.CompilerParams(has_side_effects=EFFECT),
    )(*[pltpu.with_memory_space_constraint(a, pltpu.HBM) for a in list(srcs) + list(lands)])
    return (res[0], res[1], res[2:2 + n], res[2 + n:2 + 2 * n]), res[-1][:1, :1]


def _send_wait(started, after, slotted, name):
    send_sems, recv_sems, srcs, lands = started
    n = len(srcs)
    after = list(after) if isinstance(after, (list, tuple)) else [after]

    def body(*refs):
        for cp in _group_copies(refs[:n], refs[n:2 * n], refs[2 * n], refs[2 * n + 1], slotted, True):
            cp.wait_send()
            cp.wait_recv()

    res = pl.pallas_call(
        body, name=name, out_shape=tuple(pltpu.HBM(a.shape, a.dtype) for a in list(srcs) + list(lands)),
        in_specs=[HBM] * (2 * n) + [SEM, SEM] + [ANY] * len(after), out_specs=tuple([HBM] * (2 * n)),
        input_output_aliases={i: i for i in range(2 * n)},
        compiler_params=pltpu.CompilerParams(has_side_effects=EFFECT),
    )(*srcs, *lands, send_sems, recv_sems, *after)
    return res[n:]


def _landing(own, my):
    return lax.dynamic_update_slice(lax.empty((N_DEV,) + own.shape, own.dtype), own[None], (my,) + (0,) * own.ndim)


def _full_from_slots(name, slots):
    a, b = SHARD_SHAPE[name]
    return slots.reshape(N_DEV * a, b) if SHARD_AXIS[name] == 0 else slots.transpose(1, 0, 2).reshape(a, N_DEV * b)


def _slots_from_full(name, g):
    a, b = SHARD_SHAPE[name]
    return g.reshape(N_DEV, a, b) if SHARD_AXIS[name] == 0 else g.reshape(a, N_DEV, b).transpose(1, 0, 2)


def _reduce_adam(recv, w, m, v, name):
    _, a, b = recv.shape
    ta = a
    while ta * b * 4 * N_DEV > 4 * 1024 * 1024 and ta % 16 == 0:
        ta //= 2

    def body(r_ref, w_ref, m_ref, v_ref, g_ref, d_ref, nm_ref, nv_ref):
        g = _sum_slots(r_ref)
        g_ref[...] = g
        d_ref[...], nm_ref[...], nv_ref[...] = _adam(w_ref[...], g, m_ref[...], v_ref[...])

    row = pl.BlockSpec((ta, b), lambda i: (i, 0))
    return pl.pallas_call(
        body, name=name, grid=(a // ta,),
        in_specs=[pl.BlockSpec((N_DEV, ta, b), lambda i: (0, i, 0)), row, row, row], out_specs=[row] * 4,
        out_shape=[jax.ShapeDtypeStruct((a, b), F32)] * 4, compiler_params=_cparams(("parallel",)),
    )(recv, w, m, v)


def _step_overlapped(x, mem, positions, tgt, W, M, V):
    my = _my_slot()
    shard = {n: W[n][0] for n, _, _ in BIG}
    send = {n: (shard[n] if n == "conv_w" else shard[n].astype(BF16)) for n in shard}

    first_src, mid_src, late_src = ([send[n] for n in grp] for grp in (GATHER_FIRST, GATHER_MID, GATHER_LATE))
    first, tie = _send_start(first_src, [_landing(s, my) for s in first_src], False, "gather_first_start")
    my_then = my + tie[0, 0].astype(jnp.int32)
    mid, tie = _send_start(mid_src, [_landing(s, my_then) for s in mid_src], False, "gather_mid_start")
    my_then = my + tie[0, 0].astype(jnp.int32)
    late, tie = _send_start(late_src, [_landing(s, my_then) for s in late_src], False, "gather_late_start")

    P = {n: _row(W[n]) for n in SMALL}
    for n in ("dt_bias", "a_log", "d_skip"):
        P[n] = _head_lanes(P[n])
    P["ln_in_g"] = P["ln_in_g"] + tie

    def weights_at(stage, after):
        if stage == "first":
            lands = _send_wait(first, after, False, "gather_first_wait")
            return dict(w_in=_pad_w_in(_full_from_slots("w_in", lands[0])))
        if stage == "mid":
            lands = _send_wait(mid, after, False, "gather_mid_wait")
            full = {n: _full_from_slots(n, l) for n, l in zip(GATHER_MID, lands)}
            return dict(w_q_up=_pad_w_q(full["w_q_up"]), w_kv_up=_pad_w_kv(full["w_kv_up"]), w_mix_out=full["w_mix_out"],
                        conv_w=jnp.pad(full["conv_w"], ((0, 8 - SSD_CONV), (0, 0))))
        lands = _send_wait(late, after, False, "gather_late_wait")
        return {n: _full_from_slots(n, l) for n, l in zip(GATHER_LATE, lands)}

    started, res = [], {}

    def finish(i, after):
        names, st = started[i]
        lands = _send_wait(st, after, True, "scatter_wait_%d" % i)
        for n, recv in zip(names, lands):
            res[n] = _reduce_adam(recv, shard[n], M[n][0], V[n][0], "reduce_adam_" + n)
        return sum(res[n][0][:1, :1] for n in names) * 0.0

    def emit(names, G):
        srcs = []
        for n in names:
            g = G[n]
            if n == "w_in":
                g = _unpad_w_in(g)
            elif n == "w_q_up":
                g = _unpad_w_q(g)
            elif n == "w_kv_up":
                g = _unpad_w_kv(g)
            elif n == "conv_w":
                g = g[:SSD_CONV]
            srcs.append(g if g.ndim == 3 else _slots_from_full(n, g))
        lands = [_landing(lax.dynamic_index_in_dim(s, my, 0, keepdims=False), my) for s in srcs]
        st, tie = _send_start(srcs, lands, True, "scatter_start_%d" % len(started))
        started.append((names, st))
        if len(started) == 3:
            tie = tie + finish(0, srcs[0]) + finish(1, srcs[0])
        if len(started) == 4:
            tie = tie + finish(2, srcs[0])
        return tie

    loss, gx, G = _local_step(x[0], mem[0], positions[0], tgt[0], P, weights_at, emit)
    finish(3, gx)

    small, loss_tot = _allreduce_adam_vectors(
        {n: G[n] for n in SMALL}, loss, {n: _row(W[n]) for n in SMALL}, {n: _row(M[n]) for n in SMALL}, {n: _row(V[n]) for n in SMALL})

    outs = []
    for j in range(4):
        for n in ALL_W:
            outs.append((res[n][j] if n in res else small[j][n]).reshape(W[n].shape))
    return (loss_tot[0, 0], gx[None], *outs)


def _vector_places():
    places = {}
    for r, names in enumerate(_SMALL_ROWS):
        c = 0
        for n in names:
            w = _SMALL_W.get(n, PACK_W)
            places[n] = (r, c, w, SSD_HEADS if n in ("dt_bias", "a_log", "d_skip") else w)
            c += w
    return places


def _allreduce_adam_vectors(grads, loss, Ws, Ms, Vs):
    places = _vector_places()
    ns = len(SMALL)

    def body(*refs):
        g_in, loss_in = refs[:ns], refs[ns]
        w_in, m_in, v_in = refs[ns + 1:2 * ns + 1], refs[2 * ns + 1:3 * ns + 1], refs[3 * ns + 1:4 * ns + 1]
        o = 4 * ns + 1
        outs = [refs[o + j * ns:o + (j + 1) * ns] for j in range(4)]
        loss_out, stage, land, send_sems, recv_sems = refs[o + 4 * ns:]
        stage[...] = jnp.zeros(stage.shape, F32)
        for i, n in enumerate(SMALL):
            r, c, w, _ = places[n]
            stage[r:r + 1, c:c + w] = g_in[i][...]
        stage[LOSS_ROW:LOSS_ROW + 1, 0:LANES] = loss_in[...]
        x, y, c_ = _coords()
        my = _slot(x, y, c_)
        cps = []
        for k in range(1, N_DEV):
            peer = _peer(k, x, y, c_)
            cps.append(pltpu.make_async_remote_copy(
                src_ref=stage, dst_ref=land.at[my], send_sem=send_sems.at[k - 1], recv_sem=recv_sems.at[k - 1],
                device_id=peer, device_id_type=MESH))
        for cp in cps:
            cp.start()
        land[my] = stage[...]
        for cp in cps:
            cp.wait_recv()
        for cp in cps:
            cp.wait_send()
        tot = _sum_slots(land)
        loss_out[...] = tot[LOSS_ROW:LOSS_ROW + 1, 0:LANES]
        for i, n in enumerate(SMALL):
            r, c, _, wt = places[n]
            g = tot[r:r + 1, c:c + wt]
            outs[0][i][...] = g
            outs[1][i][...], outs[2][i][...], outs[3][i][...] = _adam(w_in[i][...], g, m_in[i][...], v_in[i][...])

    shapes = [jax.ShapeDtypeStruct((1, places[n][3]), F32) for n in SMALL]
    res = pl.pallas_call(
        body, name="allreduce_vectors", in_specs=[VM] * (4 * ns + 1), out_specs=[VM] * (4 * ns + 1),
        out_shape=shapes * 4 + [jax.ShapeDtypeStruct((1, LANES), F32)],
        scratch_shapes=[pltpu.VMEM((SMALL_R, PACK_W), F32), pltpu.VMEM((N_DEV, SMALL_R, PACK_W), F32),
                        pltpu.SemaphoreType.DMA((7,)), pltpu.SemaphoreType.DMA((7,))],
    )(*[grads[n] for n in SMALL], loss, *[Ws[n] for n in SMALL], *[Ms[n] for n in SMALL], *[Vs[n] for n in SMALL])
    return [dict(zip(SMALL, res[j * ns:(j + 1) * ns])) for j in range(4)], res[4 * ns]
```

```python
import math

import jax
import jax.numpy as jnp
from jax import lax
from jax.experimental import pallas as pl
from jax.experimental.pallas import tpu as pltpu

F32, BF16 = jnp.float32, jnp.bfloat16

N_DEV = 8
D_MODEL = 1024
SSD_HEADS, SSD_HEAD_DIM, SSD_INNER, SSD_STATE, SSD_CONV, SSD_CHUNK = 8, 64, 512, 128, 4, 128
SSD_XBC = 1024
MLA_HEADS, MLA_NOPE, MLA_ROPE, MLA_QK, MLA_V = 8, 64, 32, 96, 64
MLA_Q_RANK, MLA_KV_RANK = 384, 256
ROPE_THETA = 10000.0
MEM_HEADS, MEM_HEAD_DIM = 4, 256
D_FF = 4096
IN_WIDTH = 2216
LN_EPS, RMS_EPS = 1e-5, 1e-6
ALPHA = 2.0 ** 0.25
ADAM_LR, ADAM_B1, ADAM_B2, ADAM_EPS, ADAM_WD, ADAM_STEP = 0.001, 0.9, 0.999, 1e-08, 0.01, 10

LANES = 128
NEG = -1e30
VMEM_LIMIT = 56 * 1024 * 1024
TOK_K = 4096

PQ, PZ, PX, PKV, PDT, PKR, PW = 0, 512, 1024, 2048, 2304, 2432, 2560
KR_LANE = 64


def _cparams(sem):
    return pltpu.CompilerParams(dimension_semantics=sem, vmem_limit_bytes=VMEM_LIMIT)


def _sigmoid(x):
    return 1.0 / (1.0 + jnp.exp(-x))


def _mm(a, b, mode, name, *, tm=512, tn=None, tk=None, out_dtypes=(F32,), epi=None, extras=(), col_slots=False, n_sums=0):
    if mode == "nn":
        (M, K), (K2, N) = a.shape, b.shape
    elif mode == "nt":
        (M, K), (N, K2) = a.shape, b.shape
    else:
        (K, M), (K2, N) = a.shape, b.shape
    assert K == K2, (name, a.shape, b.shape)
    tm, tn, tk = min(tm, M), min(tn or N, N), min(tk or K, K)
    assert M % tm == 0 and N % tn == 0 and K % tk == 0, (name, M, N, K, tm, tn, tk)
    gk = K // tk
    a_spec = pl.BlockSpec((tk, tm), lambda i, j, k: (k, i)) if mode == "tn" else pl.BlockSpec((tm, tk), lambda i, j, k: (i, k))
    b_spec = pl.BlockSpec((tn, tk), lambda i, j, k: (j, k)) if mode == "nt" else pl.BlockSpec((tk, tn), lambda i, j, k: (k, j))
    dims = {"nn": ((1,), (0,)), "nt": ((1,), (1,)), "tn": ((0,), (0,))}[mode]
    ex_specs = []
    for arr, kind in extras:
        if kind == "mn":
            ex_specs.append(pl.BlockSpec((tm, tn), lambda i, j, k: (i, j)))
        elif kind == "n":
            ex_specs.append(pl.BlockSpec((1, tn), lambda i, j, k: (0, j)))
        else:
            ex_specs.append(pl.BlockSpec((tm, arr.shape[1]), lambda i, j, k: (i, 0)))
    ne, no = len(extras), len(out_dtypes)
    assert n_sums == 0 or (gk == 1 and tn == N and not col_slots), name

    def body(*refs):
        a_ref, b_ref = refs[0], refs[1]
        ex, outs, sums = refs[2:2 + ne], refs[2 + ne:2 + ne + no], refs[2 + ne + no:2 + ne + no + n_sums]
        part = lax.dot_general(a_ref[...].astype(BF16), b_ref[...].astype(BF16), (dims, ((), ())),
                               preferred_element_type=F32)

        def finish(acc):
            res = epi(acc, *[e[...] for e in ex]) if epi is not None else (acc,)
            for o, r in zip(outs, res[:no]):
                o[...] = r.astype(o.dtype)
            if n_sums:
                i = pl.program_id(0)

                @pl.when(i == 0)
                def _():
                    for o, r in zip(sums, res[no:]):
                        o[...] = r

                @pl.when(i > 0)
                def _():
                    for o, r in zip(sums, res[no:]):
                        o[...] += r

        if gk == 1:
            finish(part)
        else:
            acc_ref = refs[-1]
            k = pl.program_id(2)

            @pl.when(k == 0)
            def _():
                acc_ref[...] = part

            @pl.when(k > 0)
            def _():
                acc_ref[...] += part

            @pl.when(k == gk - 1)
            def _():
                finish(acc_ref[...])

    res = pl.pallas_call(
        body, name=name, grid=(M // tm, N // tn, gk),
        in_specs=[a_spec, b_spec] + ex_specs,
        out_specs=[pl.BlockSpec((None, tm, tn), lambda i, j, k: (j, i, 0)) if col_slots else pl.BlockSpec((tm, tn), lambda i, j, k: (i, j))
                   for _ in out_dtypes] + [pl.BlockSpec((1, N), lambda i, j, k: (0, 0))] * n_sums,
        out_shape=[jax.ShapeDtypeStruct((N // tn, M, tn) if col_slots else (M, N), dt) for dt in out_dtypes]
        + [jax.ShapeDtypeStruct((1, N), F32)] * n_sums,
        scratch_shapes=[pltpu.VMEM((tm, tn), F32)] if gk > 1 else [],
        compiler_params=_cparams(("arbitrary" if n_sums else "parallel", "parallel", "arbitrary")),
    )(a, b, *[e[0] for e in extras])
    return res[0] if no + n_sums == 1 else res


def _ln_stats(r):
    mu = jnp.mean(r, axis=-1, keepdims=True)
    xc = r - mu
    var = jnp.mean(xc * xc, axis=-1, keepdims=True)
    rstd = lax.rsqrt(var + LN_EPS)
    return xc * rstd, rstd


def _ln_fwd(terms, g, b, name, tm=512):
    S, D = terms[0][0].shape
    coefs = [c for _, c in terms]
    nt = len(terms)

    def body(*refs):
        r = sum(c * t[...] for t, c in zip(refs[:nt], coefs))
        xh, _ = _ln_stats(r)
        h = xh * refs[nt][...] + refs[nt + 1][...]
        refs[nt + 2][...] = h
        refs[nt + 3][...] = h.astype(BF16)

    row = pl.BlockSpec((tm, D), lambda i: (i, 0))
    vec = pl.BlockSpec((1, D), lambda i: (0, 0))
    return pl.pallas_call(
        body, name=name, grid=(S // tm,), in_specs=[row] * nt + [vec, vec], out_specs=[row, row],
        out_shape=[jax.ShapeDtypeStruct((S, D), F32), jax.ShapeDtypeStruct((S, D), BF16)], compiler_params=_cparams(("parallel",)),
    )(*[t for t, _ in terms], g, b)


def _ln_bwd_tile(xh, rstd, dh, g):
    dxh = dh * g
    m1 = jnp.mean(dxh, axis=-1, keepdims=True)
    m2 = jnp.mean(dxh * xh, axis=-1, keepdims=True)
    return rstd * (dxh - m1 - xh * m2), jnp.sum(dh * xh, axis=0, keepdims=True), jnp.sum(dh, axis=0, keepdims=True)


def _rope_tables(positions):
    half = MLA_ROPE // 2
    inv_freq = jnp.power(ROPE_THETA, -jnp.arange(half, dtype=F32) / half)
    ang = positions.astype(F32)[:, None] * inv_freq
    cos, sin = jnp.cos(ang), jnp.sin(ang)
    S = positions.shape[0]
    one, zero = jnp.ones((S, MLA_NOPE), F32), jnp.zeros((S, half), F32)
    pad = jnp.zeros((S, LANES - MLA_QK), F32)
    c = jnp.concatenate([one, cos, cos, pad], axis=1)
    s1 = jnp.concatenate([0 * one, -sin, zero, pad], axis=1)
    s2 = jnp.concatenate([0 * one, zero, sin, pad], axis=1)
    return c, s1, s2


def _rope_block(x, c, s1, s2):
    half = MLA_ROPE // 2
    return x * c + pltpu.roll(x, LANES - half, axis=1) * s1 + pltpu.roll(x, half, axis=1) * s2


def _rms_fwd(x, g):
    r = lax.rsqrt(jnp.mean(x * x, axis=-1, keepdims=True) + RMS_EPS)
    return x * r * g


def _rms_bwd(x, g, dy):
    r = lax.rsqrt(jnp.mean(x * x, axis=-1, keepdims=True) + RMS_EPS)
    xh = x * r
    dyh = dy * g
    dx = r * (dyh - xh * jnp.mean(dyh * xh, axis=-1, keepdims=True))
    return dx, jnp.sum(dy * xh, axis=0, keepdims=True)


def _mla_prep(proj, qg, kvg, tabs, tm=512):
    S = proj.shape[0]

    def body(ql_ref, kvl_ref, kr_ref, qg_ref, kvg_ref, c_ref, s1_ref, s2_ref, qn_ref, kvn_ref, kpe_ref):
        qn_ref[...] = _rms_fwd(ql_ref[...], qg_ref[...]).astype(BF16)
        kvn_ref[...] = _rms_fwd(kvl_ref[...], kvg_ref[...]).astype(BF16)
        kpe_ref[...] = _rope_block(kr_ref[...], c_ref[...], s1_ref[...], s2_ref[...])

    tab = pl.BlockSpec((tm, LANES), lambda i: (i, 0))
    return pl.pallas_call(
        body, name="mla_prep", grid=(S // tm,),
        in_specs=[pl.BlockSpec((tm, MLA_Q_RANK), lambda i: (i, PQ // MLA_Q_RANK)),
                  pl.BlockSpec((tm, MLA_KV_RANK), lambda i: (i, PKV // MLA_KV_RANK)),
                  pl.BlockSpec((tm, LANES), lambda i: (i, PKR // LANES)),
                  pl.BlockSpec((1, MLA_Q_RANK), lambda i: (0, 0)), pl.BlockSpec((1, MLA_KV_RANK), lambda i: (0, 0)),
                  tab, tab, tab],
        out_specs=[pl.BlockSpec((tm, MLA_Q_RANK), lambda i: (i, 0)), pl.BlockSpec((tm, MLA_KV_RANK), lambda i: (i, 0)), tab],
        out_shape=[jax.ShapeDtypeStruct((S, MLA_Q_RANK), BF16), jax.ShapeDtypeStruct((S, MLA_KV_RANK), BF16),
                   jax.ShapeDtypeStruct((S, LANES), F32)],
        compiler_params=_cparams(("parallel",)),
    )(proj, proj, proj, qg, kvg, *tabs)


def _mla_prep_bwd(proj, qg, kvg, tabs, dqn, dkvn, dk_all, tm=512):
    S = proj.shape[0]

    def body(ql_ref, kvl_ref, qg_ref, kvg_ref, c_ref, s1_ref, s2_ref, dqn_ref, dkvn_ref, dk_ref,
             dql_ref, dkvl_ref, dkr_ref, dqg_ref, dkvg_ref):
        i = pl.program_id(0)
        dql, pq = _rms_bwd(ql_ref[...], qg_ref[...], dqn_ref[...])
        dkvl, pkv = _rms_bwd(kvl_ref[...], kvg_ref[...], dkvn_ref[...])
        dql_ref[...] = dql
        dkvl_ref[...] = dkvl
        dk = dk_ref[...]
        dkpe = dk[:, 0:LANES]
        for h in range(1, MLA_HEADS):
            dkpe = dkpe + dk[:, h * LANES:(h + 1) * LANES]
        lane = lax.broadcasted_iota(jnp.int32, dkpe.shape, 1)
        dkpe = jnp.where((lane >= KR_LANE) & (lane < KR_LANE + MLA_ROPE), dkpe, 0.0)
        dkr_ref[...] = _rope_block(dkpe, c_ref[...], -s1_ref[...], -s2_ref[...])

        @pl.when(i == 0)
        def _():
            dqg_ref[...] = pq
            dkvg_ref[...] = pkv

        @pl.when(i > 0)
        def _():
            dqg_ref[...] += pq
            dkvg_ref[...] += pkv

    tab = pl.BlockSpec((tm, LANES), lambda i: (i, 0))
    qspec = pl.BlockSpec((tm, MLA_Q_RANK), lambda i: (i, 0))
    kvspec = pl.BlockSpec((tm, MLA_KV_RANK), lambda i: (i, 0))
    qv, kvv = pl.BlockSpec((1, MLA_Q_RANK), lambda i: (0, 0)), pl.BlockSpec((1, MLA_KV_RANK), lambda i: (0, 0))
    return pl.pallas_call(
        body, name="mla_prep_bwd", grid=(S // tm,),
        in_specs=[pl.BlockSpec((tm, MLA_Q_RANK), lambda i: (i, PQ // MLA_Q_RANK)),
                  pl.BlockSpec((tm, MLA_KV_RANK), lambda i: (i, PKV // MLA_KV_RANK)),
                  qv, kvv, tab, tab, tab, qspec, kvspec, pl.BlockSpec((tm, MLA_HEADS * LANES), lambda i: (i, 0))],
        out_specs=[qspec, kvspec, tab, qv, kvv],
        out_shape=[jax.ShapeDtypeStruct((S, MLA_Q_RANK), F32), jax.ShapeDtypeStruct((S, MLA_KV_RANK), F32),
                   jax.ShapeDtypeStruct((S, LANES), F32), jax.ShapeDtypeStruct((1, MLA_Q_RANK), F32),
                   jax.ShapeDtypeStruct((1, MLA_KV_RANK), F32)],
        compiler_params=_cparams(("arbitrary",)),
    )(proj, proj, qg, kvg, *tabs, dqn, dkvn, dk_all)


def _rope_bwd_all(dq_all, tabs, scale, tm=512):
    S, W = dq_all.shape

    def body(dq_ref, c_ref, s1_ref, s2_ref, o_ref):
        c, s1, s2 = scale * c_ref[...], -scale * s1_ref[...], -scale * s2_ref[...]
        for h in range(W // LANES):
            o_ref[:, h * LANES:(h + 1) * LANES] = _rope_block(dq_ref[:, h * LANES:(h + 1) * LANES], c, s1, s2).astype(BF16)

    tab = pl.BlockSpec((tm, LANES), lambda i: (i, 0))
    row = pl.BlockSpec((tm, W), lambda i: (i, 0))
    return pl.pallas_call(body, name="rope_bwd", grid=(S // tm,), in_specs=[row, tab, tab, tab], out_specs=row,
                          out_shape=jax.ShapeDtypeStruct((S, W), BF16), compiler_params=_cparams(("parallel",)))(dq_all, *tabs)


ATT_SCALE = MLA_QK ** -0.5
LN2 = math.log(2.0)
Q_PRESCALE = ATT_SCALE / LN2
DQ_POSTSCALE = ATT_SCALE / LN2
N_PAIR = MLA_HEADS // 2


def _causal_mask(qi, ki, tq, tk):
    row = qi * tq + lax.broadcasted_iota(jnp.int32, (tq, tk), 0)
    col = ki * tk + lax.broadcasted_iota(jnp.int32, (tq, tk), 1)
    return col <= row


def _lane_tile(x, n):
    return jnp.concatenate([x] * n, axis=1) if n > 1 else x


def _attn_fwd(q_all, kv_all, cat, tq=512, tk=2048):
    S = q_all.shape[0]
    tq, tk = min(tq, S), min(tk, S)
    nq, nk, nb, r = S // tq, S // tk, tk // LANES, tk // tq

    def body(q_ref, k_ref, v_ref, cat_in, o_ref, lse_ref, cat_ref, m_s, l_s, acc_s):
        qi, ki = pl.program_id(1), pl.program_id(2)
        last = lax.div(qi, r)

        @pl.when(ki == 0)
        def _():
            m_s[...] = jnp.full(m_s.shape, NEG, F32)
            l_s[...] = jnp.zeros(l_s.shape, F32)
            acc_s[...] = jnp.zeros(acc_s.shape, F32)

        def block(kc, mask):
            v = v_ref[0:kc, :]
            for hh in range(2):
                q = q_ref[:, hh * LANES:(hh + 1) * LANES]
                k = k_ref[0:kc, hh * LANES:(hh + 1) * LANES]
                s = lax.dot_general(q, k, (((1,), (1,)), ((), ())), preferred_element_type=F32)
                if mask is not None:
                    s = jnp.where(mask, s, NEG)
                m_prev = m_s[hh]
                m_new = jnp.maximum(m_prev, jnp.max(s, axis=-1, keepdims=True))
                p = jnp.exp2(s - _lane_tile(m_new, kc // LANES))
                alpha = jnp.exp2(m_prev - m_new)
                ps = p[:, :LANES]
                for j in range(1, kc // LANES):
                    ps = ps + p[:, j * LANES:(j + 1) * LANES]
                l_s[hh] = alpha * l_s[hh] + ps
                acc_s[hh] = alpha * acc_s[hh] + jnp.dot(p.astype(BF16), v, preferred_element_type=F32)
                m_s[hh] = m_new

        @pl.when(ki < last)
        def _():
            block(tk, None)

        for j in range(r):
            @pl.when((ki == last) & (lax.rem(qi, r) == j))
            def _(j=j):
                kc = (j + 1) * tq
                block(kc, lax.broadcasted_iota(jnp.int32, (tq, kc), 1) <= j * tq + lax.broadcasted_iota(jnp.int32, (tq, kc), 0))

        @pl.when(ki == last)
        def _():
            first = lax.broadcasted_iota(jnp.int32, (tq, LANES), 1) < MLA_V
            l0 = jnp.sum(l_s[0], axis=-1, keepdims=True)
            l1 = jnp.sum(l_s[1], axis=-1, keepdims=True)
            o = jnp.where(first, acc_s[0] / l0, acc_s[1] / l1)
            o_ref[...] = o
            cat_ref[...] = o.astype(BF16)
            lse_ref[:, :LANES] = m_s[0] + jnp.log2(l0)
            lse_ref[:, LANES:] = m_s[1] + jnp.log2(l1)

    return pl.pallas_call(
        body, name="mla_attn_fwd", grid=(N_PAIR, nq, nk),
        in_specs=[pl.BlockSpec((tq, 2 * LANES), lambda p, qi, ki: (qi, p)),
                  pl.BlockSpec((tk, 2 * LANES), lambda p, qi, ki: (jnp.minimum(ki, lax.div(qi, r)), p)),
                  pl.BlockSpec((tk, LANES), lambda p, qi, ki: (jnp.minimum(ki, lax.div(qi, r)), MLA_HEADS + p)), ANY],
        out_specs=[pl.BlockSpec((tq, LANES), lambda p, qi, ki: (qi, p)), pl.BlockSpec((tq, 2 * LANES), lambda p, qi, ki: (qi, p)),
                   pl.BlockSpec((tq, LANES), lambda p, qi, ki: (qi, N_PAIR + p))],
        out_shape=[jax.ShapeDtypeStruct((S, MLA_HEADS * MLA_V), F32), jax.ShapeDtypeStruct((S, MLA_HEADS * LANES), F32),
                   jax.ShapeDtypeStruct(cat.shape, cat.dtype)],
        input_output_aliases={3: 2},
        scratch_shapes=[pltpu.VMEM((2, tq, LANES), F32), pltpu.VMEM((2, tq, LANES), F32), pltpu.VMEM((2, tq, LANES), F32)],
        compiler_params=_cparams(("parallel", "parallel", "arbitrary")),
    )(q_all, kv_all, kv_all, cat)


def _attn_delta(do, o):
    prod = do * o
    tm = prod.shape[0]
    first = lax.broadcasted_iota(jnp.int32, (tm, LANES), 1) < MLA_V
    blocks = []
    for p in range(N_PAIR):
        pp = prod[:, p * LANES:(p + 1) * LANES]
        blocks.append(jnp.broadcast_to(jnp.sum(jnp.where(first, pp, 0.0), axis=-1, keepdims=True), (tm, LANES)))
        blocks.append(jnp.broadcast_to(jnp.sum(jnp.where(first, 0.0, pp), axis=-1, keepdims=True), (tm, LANES)))
    return jnp.concatenate(blocks, axis=1)


def _attn_bwd(q_all, kv_all, dcat, lse, delta, tq=1024, tk=1024):
    S = q_all.shape[0]
    tq, tk = min(tq, S), min(tk, S)
    nq, nk, nb = S // tq, S // tk, tk // LANES
    assert tq == tk

    def body(q_ref, k_ref, v_ref, do_ref, lse_ref, dl_ref, dq_ref, dk_ref, dv_ref, dk_s, dv_s):
        ki, qi = pl.program_id(1), pl.program_id(2)

        @pl.when((ki == 0) & (qi == 0))
        def _():
            dq_ref[...] = jnp.zeros(dq_ref.shape, F32)

        @pl.when(qi == 0)
        def _():
            dk_s[...] = jnp.zeros(dk_s.shape, F32)
            dv_s[...] = jnp.zeros(dv_s.shape, F32)

        def block(r0, nr, kc, mask):
            v, do = v_ref[0:kc, :], do_ref[r0:r0 + nr, :]
            first = lax.broadcasted_iota(jnp.int32, (nr, LANES), 1) < MLA_V
            firstk = lax.broadcasted_iota(jnp.int32, (kc, LANES), 1) < MLA_V
            do_b = do.astype(BF16)
            rows = pl.ds(pl.multiple_of(qi * tq + r0, LANES), nr)
            for hh in range(2):
                sl = slice(hh * LANES, (hh + 1) * LANES)
                q, k = q_ref[r0:r0 + nr, sl], k_ref[0:kc, sl]
                s = lax.dot_general(q, k, (((1,), (1,)), ((), ())), preferred_element_type=F32)
                if mask is not None:
                    s = jnp.where(mask, s, NEG)
                p = jnp.exp2(s - _lane_tile(lse_ref[r0:r0 + nr, sl], kc // LANES))
                do_h = jnp.where(first if hh == 0 else ~first, do, 0.0).astype(BF16)
                dp = lax.dot_general(do_h, v, (((1,), (1,)), ((), ())), preferred_element_type=F32)
                ds_b = (p * (dp - _lane_tile(dl_ref[r0:r0 + nr, sl], kc // LANES)) * LN2).astype(BF16)
                pv = lax.dot_general(p.astype(BF16), do_b, (((0,), (0,)), ((), ())), preferred_element_type=F32)
                dv_s[0:kc, :] += jnp.where(firstk if hh == 0 else ~firstk, pv, 0.0)
                dk_s[0:kc, sl] += lax.dot_general(ds_b, q, (((0,), (0,)), ((), ())), preferred_element_type=F32)
                dq_ref[rows, sl] += jnp.dot(ds_b, k, preferred_element_type=F32)

        @pl.when(qi > ki)
        def _():
            block(0, tq, tk, None)

        @pl.when(qi == ki)
        def _():
            h = tq // 2
            block(0, h, h, _causal_mask(0, 0, h, h))
            block(h, h, tk, lax.broadcasted_iota(jnp.int32, (h, tk), 1) <= h + lax.broadcasted_iota(jnp.int32, (h, tk), 0))

        @pl.when(qi == nq - 1)
        def _():
            dk_ref[...] = dk_s[...]
            dv_ref[...] = dv_s[...]

    wide = pl.BlockSpec((tq, 2 * LANES), lambda p, ki, qi: (jnp.maximum(qi, ki), p))
    return pl.pallas_call(
        body, name="mla_attn_bwd", grid=(N_PAIR, nk, nq),
        in_specs=[wide, pl.BlockSpec((tk, 2 * LANES), lambda p, ki, qi: (ki, p)),
                  pl.BlockSpec((tk, LANES), lambda p, ki, qi: (ki, MLA_HEADS + p)),
                  pl.BlockSpec((tq, LANES), lambda p, ki, qi: (jnp.maximum(qi, ki), N_PAIR + p)), wide, wide],
        out_specs=[pl.BlockSpec((S, 2 * LANES), lambda p, ki, qi: (0, p)),
                   pl.BlockSpec((tk, 2 * LANES), lambda p, ki, qi: (ki, p)), pl.BlockSpec((tk, LANES), lambda p, ki, qi: (ki, p))],
        out_shape=[jax.ShapeDtypeStruct((S, MLA_HEADS * LANES), F32), jax.ShapeDtypeStruct((S, MLA_HEADS * LANES), F32),
                   jax.ShapeDtypeStruct((S, MLA_HEADS * MLA_V), F32)],
        scratch_shapes=[pltpu.VMEM((tk, 2 * LANES), F32), pltpu.VMEM((tk, LANES), F32)],
        compiler_params=_cparams(("parallel", "arbitrary", "arbitrary")),
    )(q_all, kv_all, kv_all, dcat, lse, delta)


MEM_SCALE = MEM_HEAD_DIM ** -0.5


def _mem_probs(q, k):
    s = lax.dot_general(q, k, (((1,), (1,)), ((), ())), preferred_element_type=F32) * MEM_SCALE
    e = jnp.exp(s - jnp.max(s, axis=-1, keepdims=True))
    return e / jnp.sum(e, axis=-1, keepdims=True)


def _mem_attn_fwd(qm, km, vm, tq=512):
    S, W = qm.shape
    M = km.shape[0]

    def body(q_ref, k_ref, v_ref, o_ref):
        for h in range(MEM_HEADS):
            sl = slice(h * MEM_HEAD_DIM, (h + 1) * MEM_HEAD_DIM)
            p = _mem_probs(q_ref[:, sl], k_ref[:, sl])
            o_ref[:, sl] = jnp.dot(p.astype(BF16), v_ref[:, sl], preferred_element_type=F32).astype(BF16)

    row = pl.BlockSpec((tq, W), lambda i: (i, 0))
    full = pl.BlockSpec((M, W), lambda i: (0, 0))
    return pl.pallas_call(body, name="mem_attn_fwd", grid=(S // tq,), in_specs=[row, full, full], out_specs=row,
                          out_shape=jax.ShapeDtypeStruct((S, W), BF16), compiler_params=_cparams(("parallel",)))(qm, km, vm)


def _mem_attn_bwd(qm, km, vm, dom, tq=512):
    S, W = qm.shape
    M = km.shape[0]

    def body(q_ref, k_ref, v_ref, do_ref, dq_ref, dk_ref, dv_ref):
        i = pl.program_id(0)

        @pl.when(i == 0)
        def _():
            dk_ref[...] = jnp.zeros(dk_ref.shape, F32)
            dv_ref[...] = jnp.zeros(dv_ref.shape, F32)

        for h in range(MEM_HEADS):
            sl = slice(h * MEM_HEAD_DIM, (h + 1) * MEM_HEAD_DIM)
            q, k, v, do = q_ref[:, sl], k_ref[:, sl], v_ref[:, sl], do_ref[:, sl]
            p = _mem_probs(q, k)
            dv_ref[:, sl] += lax.dot_general(p.astype(BF16), do, (((0,), (0,)), ((), ())), preferred_element_type=F32)
            dp = lax.dot_general(do, v, (((1,), (1,)), ((), ())), preferred_element_type=F32)
            ds = (p * (dp - jnp.sum(dp * p, axis=-1, keepdims=True)) * MEM_SCALE).astype(BF16)
            dq_ref[:, sl] = jnp.dot(ds, k, preferred_element_type=F32).astype(BF16)
            dk_ref[:, sl] += lax.dot_general(ds, q, (((0,), (0,)), ((), ())), preferred_element_type=F32)

    row = pl.BlockSpec((tq, W), lambda i: (i, 0))
    full = pl.BlockSpec((M, W), lambda i: (0, 0))
    return pl.pallas_call(
        body, name="mem_attn_bwd", grid=(S // tq,), in_specs=[row, full, full, row], out_specs=[row, full, full],
        out_shape=[jax.ShapeDtypeStruct((S, W), BF16), jax.ShapeDtypeStruct((M, W), F32), jax.ShapeDtypeStruct((M, W), F32)],
        compiler_params=_cparams(("arbitrary",)),
    )(qm, km, vm, dom)


L = SSD_CHUNK
N_SPAIR = SSD_HEADS // 2
GRP_W = SSD_INNER // 2
XB0, XC0 = SSD_INNER, SSD_INNER + 2 * SSD_STATE


def _cumsum_rows(a, reverse=False):
    row = lax.broadcasted_iota(jnp.int32, a.shape, 0)
    x, sft = a, 1
    while sft < L:
        if reverse:
            x = x + jnp.where(row < L - sft, pltpu.roll(x, L - sft, axis=0), 0.0)
        else:
            x = x + jnp.where(row >= sft, pltpu.roll(x, sft, axis=0), 0.0)
        sft *= 2
    return x


def _shift_down(cur, prev, s):
    if s == 0:
        return cur
    row = lax.broadcasted_iota(jnp.int32, cur.shape, 0)
    return jnp.where(row < s, pltpu.roll(prev, s, axis=0), pltpu.roll(cur, s, axis=0))


def _shift_up(cur, nxt, s):
    if s == 0:
        return cur
    row = lax.broadcasted_iota(jnp.int32, cur.shape, 0)
    return jnp.where(row >= L - s, pltpu.roll(nxt, L - s, axis=0), pltpu.roll(cur, L - s, axis=0))


def _ssd_conv(u, prev, cw, cb):
    delayed = [u] + [_shift_down(u, prev, s) for s in range(1, SSD_CONV)]
    conv = cb + cw[SSD_CONV - 1:SSD_CONV, :] * u
    for s in range(1, SSD_CONV):
        conv = conv + cw[SSD_CONV - 1 - s:SSD_CONV - s, :] * delayed[s]
    return conv, delayed


def _pair_lanes(v, h0, first):
    return jnp.where(first, v[:, h0:h0 + 1], v[:, h0 + 1:h0 + 2])


def _ssd_common(u, prev, dt_raw, cw, cb, dtb, alog):
    conv, delayed = _ssd_conv(u, prev, cw, cb)
    sg = _sigmoid(conv)
    xa = conv * sg
    dpre = dt_raw + dtb
    dtv = jnp.maximum(dpre, 0.0) + jnp.log1p(jnp.exp(-jnp.abs(dpre)))
    a_row = -jnp.exp(alog)
    cs = _cumsum_rows(dtv * a_row)
    return conv, sg, xa, dpre, dtv, a_row, cs, delayed


def _ssd_pair_fwd(xa, dtv, cs, csT, G, Cg, Bg, Sp, dsk, pp, first, tri, rowfirst):
    h0 = 2 * pp
    x = xa[:, pp * LANES:(pp + 1) * LANES]
    xdt = x * _pair_lanes(dtv, h0, first)
    xdt_b = xdt.astype(BF16)
    Ms, yd = [], []
    for h in (h0, h0 + 1):
        lam = jnp.exp(jnp.where(tri, cs[:, h:h + 1] - csT[h:h + 1, :], NEG))
        M = G * lam
        Ms.append((M, lam))
        yd.append(jnp.dot(M.astype(BF16), xdt_b, preferred_element_type=F32))
    T = lax.dot_general(Cg, Sp.astype(BF16), (((1,), (1,)), ((), ())), preferred_element_type=F32)
    E = jnp.exp(_pair_lanes(cs, h0, first))
    yoff = E * T
    csl = cs[L - 1:L, :]
    Fd = jnp.exp(_pair_lanes(csl, h0, first) - _pair_lanes(cs, h0, first))
    el = jnp.exp(csl)
    el_rows = jnp.where(rowfirst, el[:, h0:h0 + 1], el[:, h0 + 1:h0 + 2])
    Sloc = lax.dot_general((xdt * Fd).astype(BF16), Bg, (((0,), (0,)), ((), ())), preferred_element_type=F32)
    S_new = el_rows * Sp + Sloc
    y = jnp.where(first, yd[0], yd[1]) + yoff + x * _pair_lanes(dsk, h0, first[:1])
    return y, S_new, (x, xdt, xdt_b, Ms, E, yoff, Fd, el, el_rows)


def _ssd_masks():
    lane = lax.broadcasted_iota(jnp.int32, (L, LANES), 1)
    row = lax.broadcasted_iota(jnp.int32, (L, LANES), 0)
    return lane, row, lane < SSD_HEAD_DIM, row >= lane, row[:, :1] < SSD_HEAD_DIM


def _ssd_specs(nc, rev):
    def cidx(i):
        return nc - 1 - i if rev else i
    z = pl.BlockSpec((L, SSD_INNER), lambda i: (cidx(i), PZ // SSD_INNER))
    u = pl.BlockSpec((L, SSD_XBC), lambda i: (cidx(i), PX // SSD_XBC))
    dt = pl.BlockSpec((L, LANES), lambda i: (cidx(i), PDT // LANES))
    return cidx, z, u, dt


def _vec(w):
    return pl.BlockSpec((1, w), lambda i: (0, 0))


def _ssd_fwd(proj, cw, cb, dtb, alog, dsk, ng):
    S = proj.shape[0]
    nc = S // L

    def body(z_ref, u_ref, dt_ref, cw_ref, cb_ref, dtb_ref, alog_ref, dsk_ref, ng_ref, y_ref, st_ref, prev_s, state_s):
        c = pl.program_id(0)

        @pl.when(c == 0)
        def _():
            prev_s[...] = jnp.zeros(prev_s.shape, F32)
            state_s[...] = jnp.zeros(state_s.shape, F32)

        u = u_ref[...]
        _, _, xa, _, dtv, _, cs, _ = _ssd_common(u, prev_s[...], dt_ref[...], cw_ref[...], cb_ref[...], dtb_ref[...], alog_ref[...])
        prev_s[...] = u
        csT = cs.T
        _, _, first, tri, rowfirst = _ssd_masks()
        dsk_v = dsk_ref[...]
        ys = []
        for g in range(2):
            Bg = xa[:, XB0 + g * SSD_STATE:XB0 + (g + 1) * SSD_STATE].astype(BF16)
            Cg = xa[:, XC0 + g * SSD_STATE:XC0 + (g + 1) * SSD_STATE].astype(BF16)
            G = lax.dot_general(Cg, Bg, (((1,), (1,)), ((), ())), preferred_element_type=F32)
            for pp in (2 * g, 2 * g + 1):
                Sp = state_s[pp]
                st_ref[pp * LANES:(pp + 1) * LANES, :] = Sp
                y, S_new, _ = _ssd_pair_fwd(xa, dtv, cs, csT, G, Cg, Bg, Sp, dsk_v, pp, first, tri, rowfirst)
                state_s[pp] = S_new
                ys.append(y)
        z = z_ref[...]
        for g in range(2):
            yg = jnp.concatenate([ys[2 * g], ys[2 * g + 1]], axis=1)
            zg = z[:, g * GRP_W:(g + 1) * GRP_W]
            gated = yg * (zg * _sigmoid(zg))
            r = lax.rsqrt(jnp.mean(gated * gated, axis=-1, keepdims=True) + RMS_EPS)
            y_ref[:, g * GRP_W:(g + 1) * GRP_W] = (gated * r * ng_ref[:, g * GRP_W:(g + 1) * GRP_W]).astype(BF16)

    _, zs, us, dts = _ssd_specs(nc, False)
    return pl.pallas_call(
        body, name="ssd_fwd", grid=(nc,),
        in_specs=[zs, us, dts, pl.BlockSpec((8, SSD_XBC), lambda i: (0, 0)), _vec(SSD_XBC), _vec(LANES), _vec(LANES), _vec(LANES),
                  _vec(SSD_INNER)],
        out_specs=[pl.BlockSpec((L, SSD_INNER), lambda i: (i, 0)), pl.BlockSpec((N_SPAIR * LANES, SSD_STATE), lambda i: (i, 0))],
        out_shape=[jax.ShapeDtypeStruct((S, 2 * SSD_INNER), BF16), jax.ShapeDtypeStruct((nc * N_SPAIR * LANES, SSD_STATE), F32)],
        scratch_shapes=[pltpu.VMEM((L, SSD_XBC), F32), pltpu.VMEM((N_SPAIR, LANES, SSD_STATE), F32)],
        compiler_params=_cparams(("arbitrary",)),
    )(proj, proj, proj, cw, cb, dtb, alog, dsk, ng)


def _ssd_bwd(proj, states, dy, cw, cb, dtb, alog, dsk, ng):
    S = proj.shape[0]
    nc = S // L

    def body(z_ref, u_ref, up_ref, dt_ref, st_ref, dy_ref, cw_ref, cb_ref, dtb_ref, alog_ref, dsk_ref, ng_ref,
             dz_ref, du_ref, ddt_ref, dcw_ref, dcb_ref, ddtb_ref, dalog_ref, ddsk_ref, dng_ref,
             dS_s, dconv_s, dD_s):
        i = pl.program_id(0)
        c = nc - 1 - i

        @pl.when(i == 0)
        def _():
            dS_s[...] = jnp.zeros(dS_s.shape, F32)
            dconv_s[...] = jnp.zeros(dconv_s.shape, F32)
            dD_s[...] = jnp.zeros(dD_s.shape, F32)
            for r in (dcw_ref, dcb_ref, ddtb_ref, dalog_ref, ddsk_ref, dng_ref):
                r[...] = jnp.zeros(r.shape, F32)

        u = u_ref[...]
        prev = jnp.where(c > 0, up_ref[...], 0.0)
        cw_v = cw_ref[...]
        conv, sg, xa, dpre, dtv, a_row, cs, delayed = _ssd_common(u, prev, dt_ref[...], cw_v, cb_ref[...], dtb_ref[...], alog_ref[...])
        csT = cs.T
        lane, row, first, tri, rowfirst = _ssd_masks()
        dsk_v = dsk_ref[...]

        fw = []
        Gs, Bs, Cs = [], [], []
        for g in range(2):
            Bg = xa[:, XB0 + g * SSD_STATE:XB0 + (g + 1) * SSD_STATE].astype(BF16)
            Cg = xa[:, XC0 + g * SSD_STATE:XC0 + (g + 1) * SSD_STATE].astype(BF16)
            G = lax.dot_general(Cg, Bg, (((1,), (1,)), ((), ())), preferred_element_type=F32)
            Gs.append(G), Bs.append(Bg), Cs.append(Cg)
            for pp in (2 * g, 2 * g + 1):
                Sp = st_ref[pp * LANES:(pp + 1) * LANES, :]
                y, _, keep = _ssd_pair_fwd(xa, dtv, cs, csT, G, Cg, Bg, Sp, dsk_v, pp, first, tri, rowfirst)
                fw.append((y, Sp, keep))

        z = z_ref[...]
        dys = []
        for g in range(2):
            sl = slice(g * GRP_W, (g + 1) * GRP_W)
            yg = jnp.concatenate([fw[2 * g][0], fw[2 * g + 1][0]], axis=1)
            zg = z[:, sl]
            sz = _sigmoid(zg)
            silu_z = zg * sz
            gated = yg * silu_z
            r = lax.rsqrt(jnp.mean(gated * gated, axis=-1, keepdims=True) + RMS_EPS)
            nh = gated * r
            dout = dy_ref[:, sl]
            dng_ref[:, sl] += jnp.sum(dout * nh, axis=0, keepdims=True)
            dnh = dout * ng_ref[:, sl]
            dgated = r * (dnh - nh * jnp.mean(dnh * nh, axis=-1, keepdims=True))
            dz_ref[:, sl] = dgated * yg * (sz * (1.0 + zg * (1.0 - sz)))
            dyg = dgated * silu_z
            dys.append(dyg[:, :LANES]), dys.append(dyg[:, LANES:])

        dcs_col = [0.0] * SSD_HEADS
        dcs_row = [None] * SSD_HEADS
        ddt_col = [None] * SSD_HEADS
        dxs = []
        dB, dC = [None, None], [None, None]
        last = row == L - 1
        for g in range(2):
            Bg, Cg, G = Bs[g], Cs[g], Gs[g]
            dG = jnp.zeros((L, L), F32)
            dBg = jnp.zeros((L, SSD_STATE), F32)
            dCg = jnp.zeros((L, SSD_STATE), F32)
            for pp in (2 * g, 2 * g + 1):
                h0 = 2 * pp
                y, Sp, (x, xdt, xdt_b, Ms, E, yoff, Fd, el, el_rows) = fw[pp]
                dY = dys[pp]
                dS = dS_s[pp]
                dS_b, Sp_b = dS.astype(BF16), Sp.astype(BF16)
                dD_s[:, pp * LANES:(pp + 1) * LANES] += jnp.sum(dY * x, axis=0, keepdims=True)
                dx = dY * _pair_lanes(dsk_v, h0, first[:1])
                dxdt = jnp.zeros((L, LANES), F32)
                dY_b = dY.astype(BF16)
                for hh, h in enumerate((h0, h0 + 1)):
                    hm = first if hh == 0 else ~first
                    M, lam = Ms[hh]
                    dYh = jnp.where(hm, dY, 0.0).astype(BF16)
                    dM = lax.dot_general(dYh, xdt_b, (((1,), (1,)), ((), ())), preferred_element_type=F32)
                    W = dM * M
                    dcs_col[h] = dcs_col[h] + jnp.sum(W, axis=-1, keepdims=True)
                    dcs_row[h] = jnp.sum(W, axis=0, keepdims=True)
                    dG = dG + dM * lam
                    mt = lax.dot_general(M.astype(BF16), dY_b, (((0,), (0,)), ((), ())), preferred_element_type=F32)
                    dxdt = dxdt + jnp.where(hm, mt, 0.0)
                dT = (E * dY).astype(BF16)
                dCg = dCg + jnp.dot(dT, Sp_b, preferred_element_type=F32)
                dS_in = lax.dot_general(dT, Cg, (((0,), (0,)), ((), ())), preferred_element_type=F32) + el_rows * dS
                q1 = dY * yoff
                dZ = lax.dot_general(Bg, dS_b, (((1,), (1,)), ((), ())), preferred_element_type=F32)
                dBg = dBg + jnp.dot((xdt * Fd).astype(BF16), dS_b, preferred_element_type=F32)
                dxdt = dxdt + dZ * Fd
                q2 = dZ * xdt * Fd
                dSS = dS * Sp
                for hh, h in enumerate((h0, h0 + 1)):
                    hm = first if hh == 0 else ~first
                    rs1 = jnp.sum(jnp.where(hm, q1, 0.0), axis=-1, keepdims=True)
                    rs2 = jnp.sum(jnp.where(hm, q2, 0.0), axis=-1, keepdims=True)
                    rmask = rowfirst if hh == 0 else ~rowfirst
                    d_el = jnp.sum(jnp.sum(jnp.where(rmask, dSS, 0.0), axis=-1, keepdims=True), axis=0, keepdims=True)
                    tail = jnp.sum(rs2, axis=0, keepdims=True) + d_el * el[:, h:h + 1]
                    dcs_col[h] = dcs_col[h] + (rs1 - rs2 + jnp.where(last[:, :1], tail, 0.0))
                    ddt_col[h] = jnp.sum(jnp.where(hm, dxdt * x, 0.0), axis=-1, keepdims=True)
                dS_s[pp] = dS_in
                dxs.append(dx + dxdt * _pair_lanes(dtv, h0, first))
            dG_b = dG.astype(BF16)
            dC[g] = dCg + jnp.dot(dG_b, Bg, preferred_element_type=F32)
            dB[g] = dBg + lax.dot_general(dG_b, Cg, (((0,), (0,)), ((), ())), preferred_element_type=F32)

        dcs_c, dcs_r, ddt_c = (jnp.zeros((L, LANES), F32) for _ in range(3))
        for h in range(SSD_HEADS):
            dcs_c = dcs_c + jnp.where(lane == h, dcs_col[h], 0.0)
            dcs_r = dcs_r + jnp.where(row == h, dcs_row[h], 0.0)
            ddt_c = ddt_c + jnp.where(lane == h, ddt_col[h], 0.0)
        dcs = dcs_c - dcs_r.T
        da = _cumsum_rows(dcs, reverse=True)
        ddt_c = ddt_c + da * a_row
        dalog_ref[...] += jnp.sum(da * dtv, axis=0, keepdims=True) * a_row
        ddt_raw = ddt_c * _sigmoid(dpre)
        ddt_ref[...] = ddt_raw
        ddtb_ref[...] += jnp.sum(ddt_raw, axis=0, keepdims=True)

        dxa = jnp.concatenate(dxs + dB + dC, axis=1)
        dconv = dxa * (sg * (1.0 + conv * (1.0 - sg)))
        dcb_ref[...] += jnp.sum(dconv, axis=0, keepdims=True)
        nxt = dconv_s[...]
        du = cw_v[SSD_CONV - 1:SSD_CONV, :] * dconv
        dcw_ref[SSD_CONV - 1:SSD_CONV, :] += jnp.sum(dconv * u, axis=0, keepdims=True)
        for s in range(1, SSD_CONV):
            k = SSD_CONV - 1 - s
            du = du + cw_v[k:k + 1, :] * _shift_up(dconv, nxt, s)
            dcw_ref[k:k + 1, :] += jnp.sum(dconv * delayed[s], axis=0, keepdims=True)
        du_ref[...] = du
        dconv_s[...] = dconv

        @pl.when(i == nc - 1)
        def _():
            acc = dD_s[...]
            lane1 = lax.broadcasted_iota(jnp.int32, (1, LANES), 1)
            lanew = lax.broadcasted_iota(jnp.int32, acc.shape, 1)
            out = jnp.zeros((1, LANES), F32)
            for h in range(SSD_HEADS):
                tot = jnp.sum(jnp.where((lanew >= h * SSD_HEAD_DIM) & (lanew < (h + 1) * SSD_HEAD_DIM), acc, 0.0),
                              axis=-1, keepdims=True)
                out = out + jnp.where(lane1 == h, tot, 0.0)
            ddsk_ref[...] = out

    cidx, zs, us, dts = _ssd_specs(nc, True)
    ups = pl.BlockSpec((L, SSD_XBC), lambda i: (jnp.maximum(cidx(i) - 1, 0), PX // SSD_XBC))
    rowc = lambda w: pl.BlockSpec((L, w), lambda i: (cidx(i), 0))
    return pl.pallas_call(
        body, name="ssd_bwd", grid=(nc,),
        in_specs=[zs, us, ups, dts, pl.BlockSpec((N_SPAIR * LANES, SSD_STATE), lambda i: (cidx(i), 0)), rowc(SSD_INNER),
                  pl.BlockSpec((8, SSD_XBC), lambda i: (0, 0)), _vec(SSD_XBC), _vec(LANES), _vec(LANES), _vec(LANES), _vec(SSD_INNER)],
        out_specs=[rowc(SSD_INNER), rowc(SSD_XBC), rowc(LANES), pl.BlockSpec((8, SSD_XBC), lambda i: (0, 0)), _vec(SSD_XBC),
                   _vec(LANES), _vec(LANES), _vec(LANES), _vec(SSD_INNER)],
        out_shape=[jax.ShapeDtypeStruct((S, SSD_INNER), F32), jax.ShapeDtypeStruct((S, SSD_XBC), F32),
                   jax.ShapeDtypeStruct((S, LANES), F32), jax.ShapeDtypeStruct((8, SSD_XBC), F32),
                   jax.ShapeDtypeStruct((1, SSD_XBC), F32), jax.ShapeDtypeStruct((1, LANES), F32),
                   jax.ShapeDtypeStruct((1, LANES), F32), jax.ShapeDtypeStruct((1, LANES), F32),
                   jax.ShapeDtypeStruct((1, SSD_INNER), F32)],
        scratch_shapes=[pltpu.VMEM((N_SPAIR, LANES, SSD_STATE), F32), pltpu.VMEM((L, SSD_XBC), F32),
                        pltpu.VMEM((1, SSD_INNER), F32)],
        compiler_params=_cparams(("arbitrary",)),
    )(proj, proj, proj, proj, states, dy, cw, cb, dtb, alog, dsk, ng)


_IN_SEGS = ((PZ, 0, 512), (PX, 512, 1024), (PDT, 1536, 8), (PQ, 1544, 384), (PKV, 1928, 256), (PKR + KR_LANE, 2184, 32))


def _pad_w_in(w):
    parts, at = [], 0
    for dst, src, n in sorted(_IN_SEGS):
        parts += [jnp.zeros((w.shape[0], dst - at), w.dtype), w[:, src:src + n]]
        at = dst + n
    return jnp.concatenate(parts + [jnp.zeros((w.shape[0], PW - at), w.dtype)], axis=1)


def _unpad_w_in(wp):
    segs = sorted(_IN_SEGS, key=lambda t: t[1])
    return jnp.concatenate([wp[:, dst:dst + n] for dst, src, n in segs], axis=1)


def _pad_w_q(w):
    return jnp.pad(w.reshape(MLA_Q_RANK, MLA_HEADS, MLA_QK), ((0, 0), (0, 0), (0, LANES - MLA_QK))).reshape(MLA_Q_RANK, MLA_HEADS * LANES)


def _unpad_w_q(wp):
    return wp.reshape(MLA_Q_RANK, MLA_HEADS, LANES)[:, :, :MLA_QK].reshape(MLA_Q_RANK, MLA_HEADS * MLA_QK)


def _pad_w_kv(w):
    w3 = w.reshape(MLA_KV_RANK, MLA_HEADS, MLA_NOPE + MLA_V)
    k = jnp.pad(w3[:, :, :MLA_NOPE], ((0, 0), (0, 0), (0, LANES - MLA_NOPE))).reshape(MLA_KV_RANK, MLA_HEADS * LANES)
    return jnp.concatenate([k, w3[:, :, MLA_NOPE:].reshape(MLA_KV_RANK, MLA_HEADS * MLA_V)], axis=1)


def _unpad_w_kv(wp):
    k = wp[:, :MLA_HEADS * LANES].reshape(MLA_KV_RANK, MLA_HEADS, LANES)[:, :, :MLA_NOPE]
    v = wp[:, MLA_HEADS * LANES:].reshape(MLA_KV_RANK, MLA_HEADS, MLA_V)
    return jnp.concatenate([k, v], axis=2).reshape(MLA_KV_RANK, MLA_HEADS * (MLA_NOPE + MLA_V))


def _head_lanes(v):
    return jnp.pad(v, ((0, 0), (0, LANES - v.shape[1])))


def _local_step(x, mem, positions, tgt, P, weights_at=None, emit=None):
    tabs = _rope_tables(positions)
    G = {}
    emit = emit or (lambda names, grads: 0.0)

    rows = min(1024, x.shape[0])
    h0, h0b = _ln_fwd([(x, 1.0)], P["ln_in_g"], P["ln_in_b"], "ln_in", tm=rows)
    if weights_at is not None:
        P = {**P, **weights_at("first", (h0b,) + tuple(tabs))}
    proj = _mm(h0b, P["w_in"], "nn", "proj_in", tm=1024, tn=PW // 2)
    if weights_at is not None:
        P = {**P, **weights_at("mid", proj)}
    cat, states = _ssd_fwd(proj, P["conv_w"], P["conv_b"], P["dt_bias"], P["a_log"], P["d_skip"], P["ssd_norm_g"])
    qn, kvn, kpe = _mla_prep(proj, P["q_norm_g"], P["kv_norm_g"], tabs, tm=rows)

    def q_epi(acc, c, s1, s2):
        return (jnp.concatenate([_rope_block(acc[:, h * LANES:(h + 1) * LANES], c, s1, s2) for h in range(MLA_HEADS)], axis=1)
                * Q_PRESCALE,)

    q_all = _mm(qn, P["w_q_up"], "nn", "q_up", out_dtypes=(BF16,), epi=q_epi, extras=[(t, "m") for t in tabs])

    def kv_epi(acc, kp):
        kb = [acc[:, h * LANES:(h + 1) * LANES] + kp for h in range(MLA_HEADS)]
        return (jnp.concatenate(kb + [acc[:, MLA_HEADS * LANES:]], axis=1),)

    kv_all = _mm(kvn, P["w_kv_up"], "nn", "kv_up", out_dtypes=(BF16,), epi=kv_epi, extras=[(kpe, "m")])
    o_att, lse, cat = _attn_fwd(q_all, kv_all, cat)
    def resid_ln_epi(acc, h, g, b):
        r = ALPHA * h + acc
        y = _ln_stats(r)[0] * g + b
        return r, y, y

    ln_out = (F32, F32, BF16)
    r1, h1, h1b = _mm(cat, P["w_mix_out"], "nn", "mix_out", tm=512, out_dtypes=ln_out, epi=resid_ln_epi,
                      extras=[(h0, "mn"), (P["ln1_g"], "n"), (P["ln1_b"], "n")])
    if weights_at is not None:
        P = {**P, **weights_at("late", h1b)}
    qm = _mm(h1b, P["w_mem_q"], "nn", "mem_q", tm=1024, out_dtypes=(BF16,))
    km = _mm(mem, P["w_mem_k"], "nn", "mem_k", out_dtypes=(BF16,))
    vm = _mm(mem, P["w_mem_v"], "nn", "mem_v", out_dtypes=(BF16,))
    om = _mem_attn_fwd(qm, km, vm, tq=rows)
    r2, h2, h2b = _mm(om, P["w_mem_o"], "nn", "mem_o", tm=512, out_dtypes=ln_out, epi=resid_ln_epi,
                      extras=[(h1, "mn"), (P["ln2_g"], "n"), (P["ln2_b"], "n")])

    def up_epi(acc):
        r = jnp.maximum(acc, 0.0)
        return r * r, 2.0 * r

    act, dact = _mm(h2b, P["w_up"], "nn", "mlp_up", tm=1024, tn=1024, out_dtypes=(BF16, BF16), epi=up_epi)

    def loss_epi(acc, h, g, b, t):
        xh, rstd = _ln_stats(ALPHA * h + acc)
        diff = xh * g + b - t
        part = 0.5 * jnp.sum(jnp.mean(diff * diff, axis=-1, keepdims=True), axis=0, keepdims=True)
        dr, pg, pb = _ln_bwd_tile(xh, rstd, diff * (1.0 / D_MODEL), g)
        return dr, dr, jnp.broadcast_to(part, pg.shape), pg, pb

    dr3, dr3b, loss, G["ln3_g"], G["ln3_b"] = _mm(
        act, P["w_down"], "nn", "mlp_down", tm=512, tk=D_FF, out_dtypes=(F32, BF16), epi=loss_epi, n_sums=3,
        extras=[(h2, "mn"), (P["ln3_g"], "n"), (P["ln3_b"], "n"), (tgt, "mn")])
    loss = loss[:, :LANES]

    def ln_bwd_epi(acc, d, r, g):
        dr, pg, pb = _ln_bwd_tile(*_ln_stats(r), ALPHA * d + acc, g)
        return dr, dr, pg, pb

    def dact_epi(acc, g):
        return (acc.astype(BF16) * g,)

    du = _mm(dr3b, P["w_down"], "nt", "mlp_down_dx", tm=1024, tn=1024, out_dtypes=(BF16,), epi=dact_epi, extras=[(dact, "mn")])
    G["w_down"] = _mm(act, dr3b, "tn", "mlp_down_dw", tm=1024, tk=TOK_K, out_dtypes=(BF16,))
    G["w_up"] = _mm(h2b, du, "tn", "mlp_up_dw", tm=1024, tn=D_FF // N_DEV, tk=TOK_K, out_dtypes=(BF16,), col_slots=True)
    tie = emit(("w_down", "w_up"), G)
    dr2, dr2b, G["ln2_g"], G["ln2_b"] = _mm(
        du, P["w_up"], "nt", "mlp_up_dx", tm=512, tk=D_FF, out_dtypes=(F32, BF16), epi=ln_bwd_epi, n_sums=2,
        extras=[(dr3, "mn"), (r2, "mn"), (P["ln2_g"] + tie, "n")])

    dom = _mm(dr2b, P["w_mem_o"], "nt", "mem_o_dx", tm=1024, out_dtypes=(BF16,))
    G["w_mem_o"] = _mm(om, dr2b, "tn", "mem_o_dw", tm=1024, tk=TOK_K, out_dtypes=(BF16,))
    dqm, dkm, dvm = _mem_attn_bwd(qm, km, vm, dom, tq=rows)
    G["w_mem_q"] = _mm(h1b, dqm, "tn", "mem_q_dw", tm=1024, tk=TOK_K, out_dtypes=(BF16,))
    G["w_mem_k"] = _mm(mem, dkm, "tn", "mem_k_dw", tm=1024, out_dtypes=(BF16,))
    G["w_mem_v"] = _mm(mem, dvm, "tn", "mem_v_dw", tm=1024, out_dtypes=(BF16,))
    tie = emit(("w_mem_o", "w_mem_q", "w_mem_k", "w_mem_v"), G)
    dr1, dr1b, G["ln1_g"], G["ln1_b"] = _mm(
        dqm, P["w_mem_q"], "nt", "mem_q_dx", tm=512, out_dtypes=(F32, BF16), epi=ln_bwd_epi, n_sums=2,
        extras=[(dr2, "mn"), (r1, "mn"), (P["ln1_g"] + tie, "n")])

    def dcat_epi(acc, o):
        return acc, _attn_delta(acc[:, SSD_INNER:], o)

    dcat, delta = _mm(dr1b, P["w_mix_out"], "nt", "mix_out_dx", tm=512, out_dtypes=(F32, F32), epi=dcat_epi, extras=[(o_att, "m")])
    G["w_mix_out"] = _mm(cat, dr1b, "tn", "mix_out_dw", tm=1024, tk=TOK_K, out_dtypes=(BF16,))
    dq_all, dk_all, dv_all = _attn_bwd(q_all, kv_all, dcat, lse, delta)
    dq_pre = _rope_bwd_all(dq_all, tabs, DQ_POSTSCALE, tm=rows)
    G["w_q_up"] = _mm(qn, dq_pre, "tn", "q_up_dw", tk=TOK_K, out_dtypes=(BF16,))
    dqn = _mm(dq_pre, P["w_q_up"], "nt", "q_up_dx", tm=1024)
    dkv_all = jnp.concatenate([dk_all, dv_all], axis=1).astype(BF16)
    G["w_kv_up"] = _mm(kvn, dkv_all, "tn", "kv_up_dw", tk=TOK_K, out_dtypes=(BF16,))
    dkvn = _mm(dkv_all, P["w_kv_up"], "nt", "kv_up_dx", tm=1024)
    dql, dkvl, dkr, G["q_norm_g"], G["kv_norm_g"] = _mla_prep_bwd(proj, P["q_norm_g"], P["kv_norm_g"], tabs, dqn, dkvn, dk_all, tm=rows)
    (dz, dxbc, ddt, G["conv_w"], G["conv_b"], G["dt_bias"], G["a_log"], G["d_skip"], G["ssd_norm_g"]) = _ssd_bwd(
        proj, states, dcat, P["conv_w"], P["conv_b"], P["dt_bias"], P["a_log"], P["d_skip"], P["ssd_norm_g"])
    tie = emit(("w_mix_out", "w_q_up", "w_kv_up", "conv_w"), G)
    S = x.shape[0]
    dproj = jnp.concatenate([dql, jnp.zeros((S, PZ - MLA_Q_RANK), F32) + tie, dz, dxbc, dkvl, ddt, dkr], axis=1).astype(BF16)
    G["w_in"] = _mm(h0b, dproj, "tn", "proj_in_dw", tm=1024, tn=PW // 2, tk=TOK_K, out_dtypes=(BF16,))
    tie = emit(("w_in",), G)
    gx, G["ln_in_g"], G["ln_in_b"] = _mm(
        dproj, P["w_in"], "nt", "proj_in_dx", tm=512, tk=PW, epi=lambda acc, d, r, g: ln_bwd_epi(acc, d, r, g)[1:], n_sums=2,
        extras=[(dr1, "mn"), (x, "mn"), (P["ln_in_g"] + tie, "n")])
    return loss, gx, G


PACK_W = 1024
BIG = (("w_in", (1024, 277), 1), ("conv_w", (4, 128), 1), ("w_q_up", (384, 96), 1), ("w_kv_up", (256, 128), 1),
       ("w_mix_out", (128, 1024), 0), ("w_mem_q", (128, 1024), 0), ("w_mem_k", (128, 1024), 0), ("w_mem_v", (128, 1024), 0),
       ("w_mem_o", (128, 1024), 0), ("w_up", (1024, 512), 1), ("w_down", (512, 1024), 0))
SMALL = ("ln_in_g", "ln_in_b", "conv_b", "ln1_g", "ln1_b", "ln2_g", "ln2_b", "ln3_g", "ln3_b",
         "ssd_norm_g", "q_norm_g", "kv_norm_g", "dt_bias", "a_log", "d_skip")
ALL_W = ("ln_in_g", "ln_in_b", "w_in", "conv_w", "conv_b", "dt_bias", "a_log", "d_skip", "ssd_norm_g", "q_norm_g", "w_q_up",
         "kv_norm_g", "w_kv_up", "w_mix_out", "ln1_g", "ln1_b", "w_mem_q", "w_mem_k", "w_mem_v", "w_mem_o", "ln2_g", "ln2_b",
         "w_up", "w_down", "ln3_g", "ln3_b")


SMALL_R = 16
LOSS_ROW = 15
_SMALL_ROWS = (("ln_in_g",), ("ln_in_b",), ("conv_b",), ("ln1_g",), ("ln1_b",), ("ln2_g",), ("ln2_b",), ("ln3_g",), ("ln3_b",),
               ("ssd_norm_g", "q_norm_g"), ("kv_norm_g", "dt_bias", "a_log", "d_skip"))
_SMALL_W = {"ssd_norm_g": 512, "q_norm_g": 384, "kv_norm_g": 256, "dt_bias": LANES, "a_log": LANES, "d_skip": LANES}


MESH = pl.DeviceIdType.MESH
ANY = pl.BlockSpec(memory_space=pl.ANY)
VM = pl.BlockSpec(memory_space=pltpu.VMEM)


def _coords():
    return lax.axis_index("x"), lax.axis_index("y"), lax.axis_index("c")


def _slot(px, py, pc):
    return 4 * px + 2 * py + pc


def _peer(k, x, y, c):
    dx, dy, dc = (k >> 2) & 1, (k >> 1) & 1, k & 1
    return (1 - x if dx else x, 1 - y if dy else y, 1 - c if dc else c)


def _adam(w, g, m, v):
    m = ADAM_B1 * m + (1.0 - ADAM_B1) * g
    v = ADAM_B2 * v + (1.0 - ADAM_B2) * (g * g)
    m_hat = m / (1.0 - ADAM_B1 ** ADAM_STEP)
    v_hat = v / (1.0 - ADAM_B2 ** ADAM_STEP)
    delta = -ADAM_LR * (m_hat / (jnp.sqrt(v_hat) + ADAM_EPS) + ADAM_WD * w)
    return delta, m, v


def _sum_slots(ref):
    tot = ref[0].astype(F32)
    for q in range(1, N_DEV):
        tot = tot + ref[q].astype(F32)
    return tot


def _row(v):
    return v.reshape(1, -1).astype(F32)


def kernel(x, mem, positions, ln_in_g, ln_in_b, w_in, conv_w, conv_b, dt_bias, a_log, d_skip, ssd_norm_g, q_norm_g, w_q_up, kv_norm_g, w_kv_up, w_mix_out, ln1_g, ln1_b, w_mem_q, w_mem_k, w_mem_v, w_mem_o, ln2_g, ln2_b, w_up, w_down, ln3_g, ln3_b, loss_target, m_ln_in_g, m_ln_in_b, m_w_in, m_conv_w, m_conv_b, m_dt_bias, m_a_log, m_d_skip, m_ssd_norm_g, m_q_norm_g, m_w_q_up, m_kv_norm_g, m_w_kv_up, m_w_mix_out, m_ln1_g, m_ln1_b, m_w_mem_q, m_w_mem_k, m_w_mem_v, m_w_mem_o, m_ln2_g, m_ln2_b, m_w_up, m_w_down, m_ln3_g, m_ln3_b, v_ln_in_g, v_ln_in_b, v_w_in, v_conv_w, v_conv_b, v_dt_bias, v_a_log, v_d_skip, v_ssd_norm_g, v_q_norm_g, v_w_q_up, v_kv_norm_g, v_w_kv_up, v_w_mix_out, v_ln1_g, v_ln1_b, v_w_mem_q, v_w_mem_k, v_w_mem_v, v_w_mem_o, v_ln2_g, v_ln2_b, v_w_up, v_w_down, v_ln3_g, v_ln3_b):
    a = dict(locals())
    W = {n: a[n] for n in ALL_W}
    M = {n: a["m_" + n] for n in ALL_W}
    V = {n: a["v_" + n] for n in ALL_W}
    return _step_overlapped(x, mem, positions, loss_target, W, M, V)


HBM = pl.BlockSpec(memory_space=pltpu.HBM)
SEM = pl.BlockSpec(memory_space=pltpu.SEMAPHORE)
EFFECT = pltpu.SideEffectType.DATAFLOW_SIDE_EFFECTING
SHARD_SHAPE = {n: s for n, s, _ in BIG}
SHARD_AXIS = {n: ax for n, _, ax in BIG}
GATHER_FIRST = ("w_in",)
GATHER_MID = ("conv_w", "w_q_up", "w_kv_up", "w_mix_out")
GATHER_LATE = ("w_mem_q", "w_mem_k", "w_mem_v", "w_mem_o", "w_up", "w_down")


def _my_slot():
    return _slot(*_coords())


def _group_copies(src_refs, land_refs, send_sems, recv_sems, slotted, landing_of_peer):
    x, y, c = _coords()
    my = _slot(x, y, c)
    cps = []
    for a, (s_ref, l_ref) in enumerate(zip(src_refs, land_refs)):
        for k in range(1, N_DEV):
            peer = _peer(k, x, y, c)
            cps.append(pltpu.make_async_remote_copy(
                src_ref=s_ref.at[_slot(*peer)] if slotted else s_ref,
                dst_ref=l_ref.at[_slot(*peer)] if landing_of_peer else l_ref.at[my],
                send_sem=send_sems.at[7 * a + k - 1], recv_sem=recv_sems.at[7 * a + k - 1],
                device_id=peer, device_id_type=MESH))
    return cps


def _send_start(srcs, lands, slotted, name):
    n = len(srcs)

    def body(*refs):
        for cp in _group_copies(refs[:n], refs[n:2 * n], refs[2 * n], refs[2 * n + 1], slotted, False):
            cp.start()
        refs[-1][...] = jnp.zeros(refs[-1].shape, F32)

    res = pl.pallas_call(
        body, name=name,
        out_shape=(pltpu.SemaphoreType.DMA((7 * n,)), pltpu.SemaphoreType.DMA((7 * n,)),
                   *[pltpu.HBM(a.shape, a.dtype) for a in srcs], *[pltpu.HBM(a.shape, a.dtype) for a in lands],
                   jax.ShapeDtypeStruct((8, LANES), F32)),
        in_specs=[HBM] * (2 * n), out_specs=(SEM, SEM, *[HBM] * (2 * n), VM),
        input_output_aliases={i: 2 + i for i in range(2 * n)},
        compiler_params=pltpu.CompilerParams(has_side_effects=EFFECT),
    )(*[pltpu.with_memory_space_constraint(a, pltpu.HBM) for a in list(srcs) + list(lands)])
    return (res[0], res[1], res[2:2 + n], res[2 + n:2 + 2 * n]), res[-1][:1, :1]


def _send_wait(started, after, slotted, name):
    send_sems, recv_sems, srcs, lands = started
    n = len(srcs)
    after = list(after) if isinstance(after, (list, tuple)) else [after]

    def body(*refs):
        for cp in _group_copies(refs[:n], refs[n:2 * n], refs[2 * n], refs[2 * n + 1], slotted, True):
            cp.wait_send()
            cp.wait_recv()

    res = pl.pallas_call(
        body, name=name, out_shape=tuple(pltpu.HBM(a.shape, a.dtype) for a in list(srcs) + list(lands)),
        in_specs=[HBM] * (2 * n) + [SEM, SEM] + [ANY] * len(after), out_specs=tuple([HBM] * (2 * n)),
        input_output_aliases={i: i for i in range(2 * n)},
        compiler_params=pltpu.CompilerParams(has_side_effects=EFFECT),
    )(*srcs, *lands, send_sems, recv_sems, *after)
    return res[n:]


def _landing(own, my):
    return lax.dynamic_update_slice(lax.empty((N_DEV,) + own.shape, own.dtype), own[None], (my,) + (0,) * own.ndim)


def _full_from_slots(name, slots):
    a, b = SHARD_SHAPE[name]
    return slots.reshape(N_DEV * a, b) if SHARD_AXIS[name] == 0 else slots.transpose(1, 0, 2).reshape(a, N_DEV * b)


def _slots_from_full(name, g):
    a, b = SHARD_SHAPE[name]
    return g.reshape(N_DEV, a, b) if SHARD_AXIS[name] == 0 else g.reshape(a, N_DEV, b).transpose(1, 0, 2)


def _reduce_adam(recv, w, m, v, name):
    _, a, b = recv.shape
    ta = a
    while ta * b * 4 * N_DEV > 4 * 1024 * 1024 and ta % 16 == 0:
        ta //= 2

    def body(r_ref, w_ref, m_ref, v_ref, g_ref, d_ref, nm_ref, nv_ref):
        g = _sum_slots(r_ref)
        g_ref[...] = g
        d_ref[...], nm_ref[...], nv_ref[...] = _adam(w_ref[...], g, m_ref[...], v_ref[...])

    row = pl.BlockSpec((ta, b), lambda i: (i, 0))
    return pl.pallas_call(
        body, name=name, grid=(a // ta,),
        in_specs=[pl.BlockSpec((N_DEV, ta, b), lambda i: (0, i, 0)), row, row, row], out_specs=[row] * 4,
        out_shape=[jax.ShapeDtypeStruct((a, b), F32)] * 4, compiler_params=_cparams(("parallel",)),
    )(recv, w, m, v)


def _step_overlapped(x, mem, positions, tgt, W, M, V):
    my = _my_slot()
    shard = {n: W[n][0] for n, _, _ in BIG}
    send = {n: (shard[n] if n == "conv_w" else shard[n].astype(BF16)) for n in shard}

    first_src, mid_src, late_src = ([send[n] for n in grp] for grp in (GATHER_FIRST, GATHER_MID, GATHER_LATE))
    first, tie = _send_start(first_src, [_landing(s, my) for s in first_src], False, "gather_first_start")
    my_then = my + tie[0, 0].astype(jnp.int32)
    mid, tie = _send_start(mid_src, [_landing(s, my_then) for s in mid_src], False, "gather_mid_start")
    my_then = my + tie[0, 0].astype(jnp.int32)
    late, tie = _send_start(late_src, [_landing(s, my_then) for s in late_src], False, "gather_late_start")

    P = {n: _row(W[n]) for n in SMALL}
    for n in ("dt_bias", "a_log", "d_skip"):
        P[n] = _head_lanes(P[n])
    P["ln_in_g"] = P["ln_in_g"] + tie

    def weights_at(stage, after):
        if stage == "first":
            lands = _send_wait(first, after, False, "gather_first_wait")
            return dict(w_in=_pad_w_in(_full_from_slots("w_in", lands[0])))
        if stage == "mid":
            lands = _send_wait(mid, after, False, "gather_mid_wait")
            full = {n: _full_from_slots(n, l) for n, l in zip(GATHER_MID, lands)}
            return dict(w_q_up=_pad_w_q(full["w_q_up"]), w_kv_up=_pad_w_kv(full["w_kv_up"]), w_mix_out=full["w_mix_out"],
                        conv_w=jnp.pad(full["conv_w"], ((0, 8 - SSD_CONV), (0, 0))))
        lands = _send_wait(late, after, False, "gather_late_wait")
        return {n: _full_from_slots(n, l) for n, l in zip(GATHER_LATE, lands)}

    started, res = [], {}

    def finish(i, after):
        names, st = started[i]
        lands = _send_wait(st, after, True, "scatter_wait_%d" % i)
        for n, recv in zip(names, lands):
            res[n] = _reduce_adam(recv, shard[n], M[n][0], V[n][0], "reduce_adam_" + n)
        return sum(res[n][0][:1, :1] for n in names) * 0.0

    def emit(names, G):
        srcs = []
        for n in names:
            g = G[n]
            if n == "w_in":
                g = _unpad_w_in(g)
            elif n == "w_q_up":
                g = _unpad_w_q(g)
            elif n == "w_kv_up":
                g = _unpad_w_kv(g)
            elif n == "conv_w":
                g = g[:SSD_CONV]
            srcs.append(g if g.ndim == 3 else _slots_from_full(n, g))
        lands = [_landing(lax.dynamic_index_in_dim(s, my, 0, keepdims=False), my) for s in srcs]
        st, tie = _send_start(srcs, lands, True, "scatter_start_%d" % len(started))
        started.append((names, st))
        if len(started) == 3:
            tie = tie + finish(0, srcs[0]) + finish(1, srcs[0])
        if len(started) == 4:
            tie = tie + finish(2, srcs[0])
        return tie

    loss, gx, G = _local_step(x[0], mem[0], positions[0], tgt[0], P, weights_at, emit)
    finish(3, gx)

    small, loss_tot = _allreduce_adam_vectors(
        {n: G[n] for n in SMALL}, loss, {n: _row(W[n]) for n in SMALL}, {n: _row(M[n]) for n in SMALL}, {n: _row(V[n]) for n in SMALL})

    outs = []
    for j in range(4):
        for n in ALL_W:
            outs.append((res[n][j] if n in res else small[j][n]).reshape(W[n].shape))
    return (loss_tot[0, 0], gx[None], *outs)


def _vector_places():
    places = {}
    for r, names in enumerate(_SMALL_ROWS):
        c = 0
        for n in names:
            w = _SMALL_W.get(n, PACK_W)
            places[n] = (r, c, w, SSD_HEADS if n in ("dt_bias", "a_log", "d_skip") else w)
            c += w
    return places


def _allreduce_adam_vectors(grads, loss, Ws, Ms, Vs):
    places = _vector_places()
    ns = len(SMALL)

    def body(*refs):
        g_in, loss_in = refs[:ns], refs[ns]
        w_in, m_in, v_in = refs[ns + 1:2 * ns + 1], refs[2 * ns + 1:3 * ns + 1], refs[3 * ns + 1:4 * ns + 1]
        o = 4 * ns + 1
        outs = [refs[o + j * ns:o + (j + 1) * ns] for j in range(4)]
        loss_out, stage, land, send_sems, recv_sems = refs[o + 4 * ns:]
        stage[...] = jnp.zeros(stage.shape, F32)
        for i, n in enumerate(SMALL):
            r, c, w, _ = places[n]
            stage[r:r + 1, c:c + w] = g_in[i][...]
        stage[LOSS_ROW:LOSS_ROW + 1, 0:LANES] = loss_in[...]
        x, y, c_ = _coords()
        my = _slot(x, y, c_)
        cps = []
        for k in range(1, N_DEV):
            peer = _peer(k, x, y, c_)
            cps.append(pltpu.make_async_remote_copy(
                src_ref=stage, dst_ref=land.at[my], send_sem=send_sems.at[k - 1], recv_sem=recv_sems.at[k - 1],
                device_id=peer, device_id_type=MESH))
        for cp in cps:
            cp.start()
        land[my] = stage[...]
        for cp in cps:
            cp.wait_recv()
        for cp in cps:
            cp.wait_send()
        tot = _sum_slots(land)
        loss_out[...] = tot[LOSS_ROW:LOSS_ROW + 1, 0:LANES]
        for i, n in enumerate(SMALL):
            r, c, _, wt = places[n]
            g = tot[r:r + 1, c:c + wt]
            outs[0][i][...] = g
            outs[1][i][...], outs[2][i][...], outs[3][i][...] = _adam(w_in[i][...], g, m_in[i][...], v_in[i][...])

    shapes = [jax.ShapeDtypeStruct((1, places[n][3]), F32) for n in SMALL]
    res = pl.pallas_call(
        body, name="allreduce_vectors", in_specs=[VM] * (4 * ns + 1), out_specs=[VM] * (4 * ns + 1),
        out_shape=shapes * 4 + [jax.ShapeDtypeStruct((1, LANES), F32)],
        scratch_shapes=[pltpu.VMEM((SMALL_R, PACK_W), F32), pltpu.VMEM((N_DEV, SMALL_R, PACK_W), F32),
                        pltpu.SemaphoreType.DMA((7,)), pltpu.SemaphoreType.DMA((7,))],
    )(*[grads[n] for n in SMALL], loss, *[Ws[n] for n in SMALL], *[Ms[n] for n in SMALL], *[Vs[n] for n in SMALL])
    return [dict(zip(SMALL, res[j * ns:(j + 1) * ns])) for j in range(4)], res[4 * ns]
```

```python
import math

import jax
import jax.numpy as jnp
from jax import lax
from jax.experimental import pallas as pl
from jax.experimental.pallas import tpu as pltpu

F32, BF16 = jnp.float32, jnp.bfloat16

N_DEV = 8
D_MODEL = 1024
SSD_HEADS, SSD_HEAD_DIM, SSD_INNER, SSD_STATE, SSD_CONV, SSD_CHUNK = 8, 64, 512, 128, 4, 128
SSD_XBC = 1024
MLA_HEADS, MLA_NOPE, MLA_ROPE, MLA_QK, MLA_V = 8, 64, 32, 96, 64
MLA_Q_RANK, MLA_KV_RANK = 384, 256
ROPE_THETA = 10000.0
MEM_HEADS, MEM_HEAD_DIM = 4, 256
D_FF = 4096
IN_WIDTH = 2216
LN_EPS, RMS_EPS = 1e-5, 1e-6
ALPHA = 2.0 ** 0.25
ADAM_LR, ADAM_B1, ADAM_B2, ADAM_EPS, ADAM_WD, ADAM_STEP = 0.001, 0.9, 0.999, 1e-08, 0.01, 10

LANES = 128
NEG = -1e30
VMEM_LIMIT = 56 * 1024 * 1024
TOK_K = 4096

PQ, PZ, PX, PKV, PDT, PKR, PW = 0, 512, 1024, 2048, 2304, 2432, 2560
KR_LANE = 64


def _cparams(sem):
    return pltpu.CompilerParams(dimension_semantics=sem, vmem_limit_bytes=VMEM_LIMIT)


def _sigmoid(x):
    return 1.0 / (1.0 + jnp.exp(-x))


def _mm(a, b, mode, name, *, tm=512, tn=None, tk=None, out_dtypes=(F32,), epi=None, extras=(), col_slots=False, n_sums=0):
    if mode == "nn":
        (M, K), (K2, N) = a.shape, b.shape
    elif mode == "nt":
        (M, K), (N, K2) = a.shape, b.shape
    else:
        (K, M), (K2, N) = a.shape, b.shape
    assert K == K2, (name, a.shape, b.shape)
    tm, tn, tk = min(tm, M), min(tn or N, N), min(tk or K, K)
    assert M % tm == 0 and N % tn == 0 and K % tk == 0, (name, M, N, K, tm, tn, tk)
    gk = K // tk
    a_spec = pl.BlockSpec((tk, tm), lambda i, j, k: (k, i)) if mode == "tn" else pl.BlockSpec((tm, tk), lambda i, j, k: (i, k))
    b_spec = pl.BlockSpec((tn, tk), lambda i, j, k: (j, k)) if mode == "nt" else pl.BlockSpec((tk, tn), lambda i, j, k: (k, j))
    dims = {"nn": ((1,), (0,)), "nt": ((1,), (1,)), "tn": ((0,), (0,))}[mode]
    ex_specs = []
    for arr, kind in extras:
        if kind == "mn":
            ex_specs.append(pl.BlockSpec((tm, tn), lambda i, j, k: (i, j)))
        elif kind == "n":
            ex_specs.append(pl.BlockSpec((1, tn), lambda i, j, k: (0, j)))
        else:
            ex_specs.append(pl.BlockSpec((tm, arr.shape[1]), lambda i, j, k: (i, 0)))
    ne, no = len(extras), len(out_dtypes)
    assert n_sums == 0 or (gk == 1 and tn == N and not col_slots), name

    def body(*refs):
        a_ref, b_ref = refs[0], refs[1]
        ex, outs, sums = refs[2:2 + ne], refs[2 + ne:2 + ne + no], refs[2 + ne + no:2 + ne + no + n_sums]
        part = lax.dot_general(a_ref[...].astype(BF16), b_ref[...].astype(BF16), (dims, ((), ())),
                               preferred_element_type=F32)

        def finish(acc):
            res = epi(acc, *[e[...] for e in ex]) if epi is not None else (acc,)
            for o, r in zip(outs, res[:no]):
                o[...] = r.astype(o.dtype)
            if n_sums:
                i = pl.program_id(0)

                @pl.when(i == 0)
                def _():
                    for o, r in zip(sums, res[no:]):
                        o[...] = r

                @pl.when(i > 0)
                def _():
                    for o, r in zip(sums, res[no:]):
                        o[...] += r

        if gk == 1:
            finish(part)
        else:
            acc_ref = refs[-1]
            k = pl.program_id(2)

            @pl.when(k == 0)
            def _():
                acc_ref[...] = part

            @pl.when(k > 0)
            def _():
                acc_ref[...] += part

            @pl.when(k == gk - 1)
            def _():
                finish(acc_ref[...])

    res = pl.pallas_call(
        body, name=name, grid=(M // tm, N // tn, gk),
        in_specs=[a_spec, b_spec] + ex_specs,
        out_specs=[pl.BlockSpec((None, tm, tn), lambda i, j, k: (j, i, 0)) if col_slots else pl.BlockSpec((tm, tn), lambda i, j, k: (i, j))
                   for _ in out_dtypes] + [pl.BlockSpec((1, N), lambda i, j, k: (0, 0))] * n_sums,
        out_shape=[jax.ShapeDtypeStruct((N // tn, M, tn) if col_slots else (M, N), dt) for dt in out_dtypes]
        + [jax.ShapeDtypeStruct((1, N), F32)] * n_sums,
        scratch_shapes=[pltpu.VMEM((tm, tn), F32)] if gk > 1 else [],
        compiler_params=_cparams(("arbitrary" if n_sums else "parallel", "parallel", "arbitrary")),
    )(a, b, *[e[0] for e in extras])
    return res[0] if no + n_sums == 1 else res


def _ln_stats(r):
    mu = jnp.mean(r, axis=-1, keepdims=True)
    xc = r - mu
    var = jnp.mean(xc * xc, axis=-1, keepdims=True)
    rstd = lax.rsqrt(var + LN_EPS)
    return xc * rstd, rstd


def _ln_fwd(terms, g, b, name, tm=512):
    S, D = terms[0][0].shape
    coefs = [c for _, c in terms]
    nt = len(terms)

    def body(*refs):
        r = sum(c * t[...] for t, c in zip(refs[:nt], coefs))
        xh, _ = _ln_stats(r)
        h = xh * refs[nt][...] + refs[nt + 1][...]
        refs[nt + 2][...] = h
        refs[nt + 3][...] = h.astype(BF16)

    row = pl.BlockSpec((tm, D), lambda i: (i, 0))
    vec = pl.BlockSpec((1, D), lambda i: (0, 0))
    return pl.pallas_call(
        body, name=name, grid=(S // tm,), in_specs=[row] * nt + [vec, vec], out_specs=[row, row],
        out_shape=[jax.ShapeDtypeStruct((S, D), F32), jax.ShapeDtypeStruct((S, D), BF16)], compiler_params=_cparams(("parallel",)),
    )(*[t for t, _ in terms], g, b)


def _ln_bwd_tile(xh, rstd, dh, g):
    dxh = dh * g
    m1 = jnp.mean(dxh, axis=-1, keepdims=True)
    m2 = jnp.mean(dxh * xh, axis=-1, keepdims=True)
    return rstd * (dxh - m1 - xh * m2), jnp.sum(dh * xh, axis=0, keepdims=True), jnp.sum(dh, axis=0, keepdims=True)


def _rope_tables(positions):
    half = MLA_ROPE // 2
    inv_freq = jnp.power(ROPE_THETA, -jnp.arange(half, dtype=F32) / half)
    ang = positions.astype(F32)[:, None] * inv_freq
    cos, sin = jnp.cos(ang), jnp.sin(ang)
    S = positions.shape[0]
    one, zero = jnp.ones((S, MLA_NOPE), F32), jnp.zeros((S, half), F32)
    pad = jnp.zeros((S, LANES - MLA_QK), F32)
    c = jnp.concatenate([one, cos, cos, pad], axis=1)
    s1 = jnp.concatenate([0 * one, -sin, zero, pad], axis=1)
    s2 = jnp.concatenate([0 * one, zero, sin, pad], axis=1)
    return c, s1, s2


def _rope_block(x, c, s1, s2):
    half = MLA_ROPE // 2
    return x * c + pltpu.roll(x, LANES - half, axis=1) * s1 + pltpu.roll(x, half, axis=1) * s2


def _rms_fwd(x, g):
    r = lax.rsqrt(jnp.mean(x * x, axis=-1, keepdims=True) + RMS_EPS)
    return x * r * g


def _rms_bwd(x, g, dy):
    r = lax.rsqrt(jnp.mean(x * x, axis=-1, keepdims=True) + RMS_EPS)
    xh = x * r
    dyh = dy * g
    dx = r * (dyh - xh * jnp.mean(dyh * xh, axis=-1, keepdims=True))
    return dx, jnp.sum(dy * xh, axis=0, keepdims=True)


def _mla_prep(proj, qg, kvg, tabs, tm=512):
    S = proj.shape[0]

    def body(ql_ref, kvl_ref, kr_ref, qg_ref, kvg_ref, c_ref, s1_ref, s2_ref, qn_ref, kvn_ref, kpe_ref):
        qn_ref[...] = _rms_fwd(ql_ref[...], qg_ref[...]).astype(BF16)
        kvn_ref[...] = _rms_fwd(kvl_ref[...], kvg_ref[...]).astype(BF16)
        kpe_ref[...] = _rope_block(kr_ref[...], c_ref[...], s1_ref[...], s2_ref[...])

    tab = pl.BlockSpec((tm, LANES), lambda i: (i, 0))
    return pl.pallas_call(
        body, name="mla_prep", grid=(S // tm,),
        in_specs=[pl.BlockSpec((tm, MLA_Q_RANK), lambda i: (i, PQ // MLA_Q_RANK)),
                  pl.BlockSpec((tm, MLA_KV_RANK), lambda i: (i, PKV // MLA_KV_RANK)),
                  pl.BlockSpec((tm, LANES), lambda i: (i, PKR // LANES)),
                  pl.BlockSpec((1, MLA_Q_RANK), lambda i: (0, 0)), pl.BlockSpec((1, MLA_KV_RANK), lambda i: (0, 0)),
                  tab, tab, tab],
        out_specs=[pl.BlockSpec((tm, MLA_Q_RANK), lambda i: (i, 0)), pl.BlockSpec((tm, MLA_KV_RANK), lambda i: (i, 0)), tab],
        out_shape=[jax.ShapeDtypeStruct((S, MLA_Q_RANK), BF16), jax.ShapeDtypeStruct((S, MLA_KV_RANK), BF16),
                   jax.ShapeDtypeStruct((S, LANES), F32)],
        compiler_params=_cparams(("parallel",)),
    )(proj, proj, proj, qg, kvg, *tabs)


def _mla_prep_bwd(proj, qg, kvg, tabs, dqn, dkvn, dk_all, tm=512):
    S = proj.shape[0]

    def body(ql_ref, kvl_ref, qg_ref, kvg_ref, c_ref, s1_ref, s2_ref, dqn_ref, dkvn_ref, dk_ref,
             dql_ref, dkvl_ref, dkr_ref, dqg_ref, dkvg_ref):
        i = pl.program_id(0)
        dql, pq = _rms_bwd(ql_ref[...], qg_ref[...], dqn_ref[...])
        dkvl, pkv = _rms_bwd(kvl_ref[...], kvg_ref[...], dkvn_ref[...])
        dql_ref[...] = dql
        dkvl_ref[...] = dkvl
        dk = dk_ref[...]
        dkpe = dk[:, 0:LANES]
        for h in range(1, MLA_HEADS):
            dkpe = dkpe + dk[:, h * LANES:(h + 1) * LANES]
        lane = lax.broadcasted_iota(jnp.int32, dkpe.shape, 1)
        dkpe = jnp.where((lane >= KR_LANE) & (lane < KR_LANE + MLA_ROPE), dkpe, 0.0)
        dkr_ref[...] = _rope_block(dkpe, c_ref[...], -s1_ref[...], -s2_ref[...])

        @pl.when(i == 0)
        def _():
            dqg_ref[...] = pq
            dkvg_ref[...] = pkv

        @pl.when(i > 0)
        def _():
            dqg_ref[...] += pq
            dkvg_ref[...] += pkv

    tab = pl.BlockSpec((tm, LANES), lambda i: (i, 0))
    qspec = pl.BlockSpec((tm, MLA_Q_RANK), lambda i: (i, 0))
    kvspec = pl.BlockSpec((tm, MLA_KV_RANK), lambda i: (i, 0))
    qv, kvv = pl.BlockSpec((1, MLA_Q_RANK), lambda i: (0, 0)), pl.BlockSpec((1, MLA_KV_RANK), lambda i: (0, 0))
    return pl.pallas_call(
        body, name="mla_prep_bwd", grid=(S // tm,),
        in_specs=[pl.BlockSpec((tm, MLA_Q_RANK), lambda i: (i, PQ // MLA_Q_RANK)),
                  pl.BlockSpec((tm, MLA_KV_RANK), lambda i: (i, PKV // MLA_KV_RANK)),
                  qv, kvv, tab, tab, tab, qspec, kvspec, pl.BlockSpec((tm, MLA_HEADS * LANES), lambda i: (i, 0))],
        out_specs=[qspec, kvspec, tab, qv, kvv],
        out_shape=[jax.ShapeDtypeStruct((S, MLA_Q_RANK), F32), jax.ShapeDtypeStruct((S, MLA_KV_RANK), F32),
                   jax.ShapeDtypeStruct((S, LANES), F32), jax.ShapeDtypeStruct((1, MLA_Q_RANK), F32),
                   jax.ShapeDtypeStruct((1, MLA_KV_RANK), F32)],
        compiler_params=_cparams(("arbitrary",)),
    )(proj, proj, qg, kvg, *tabs, dqn, dkvn, dk_all)


def _rope_bwd_all(dq_all, tabs, scale, tm=512):
    S, W = dq_all.shape

    def body(dq_ref, c_ref, s1_ref, s2_ref, o_ref):
        c, s1, s2 = scale * c_ref[...], -scale * s1_ref[...], -scale * s2_ref[...]
        for h in range(W // LANES):
            o_ref[:, h * LANES:(h + 1) * LANES] = _rope_block(dq_ref[:, h * LANES:(h + 1) * LANES], c, s1, s2).astype(BF16)

    tab = pl.BlockSpec((tm, LANES), lambda i: (i, 0))
    row = pl.BlockSpec((tm, W), lambda i: (i, 0))
    return pl.pallas_call(body, name="rope_bwd", grid=(S // tm,), in_specs=[row, tab, tab, tab], out_specs=row,
                          out_shape=jax.ShapeDtypeStruct((S, W), BF16), compiler_params=_cparams(("parallel",)))(dq_all, *tabs)


ATT_SCALE = MLA_QK ** -0.5
LN2 = math.log(2.0)
Q_PRESCALE = ATT_SCALE / LN2
DQ_POSTSCALE = ATT_SCALE / LN2
N_PAIR = MLA_HEADS // 2


def _causal_mask(qi, ki, tq, tk):
    row = qi * tq + lax.broadcasted_iota(jnp.int32, (tq, tk), 0)
    col = ki * tk + lax.broadcasted_iota(jnp.int32, (tq, tk), 1)
    return col <= row


def _lane_tile(x, n):
    return jnp.concatenate([x] * n, axis=1) if n > 1 else x


def _attn_fwd(q_all, kv_all, cat, tq=512, tk=2048):
    S = q_all.shape[0]
    tq, tk = min(tq, S), min(tk, S)
    nq, nk, nb, r = S // tq, S // tk, tk // LANES, tk // tq

    def body(q_ref, k_ref, v_ref, cat_in, o_ref, lse_ref, cat_ref, m_s, l_s, acc_s):
        qi, ki = pl.program_id(1), pl.program_id(2)
        last = lax.div(qi, r)

        @pl.when(ki == 0)
        def _():
            m_s[...] = jnp.full(m_s.shape, NEG, F32)
            l_s[...] = jnp.zeros(l_s.shape, F32)
            acc_s[...] = jnp.zeros(acc_s.shape, F32)

        def block(kc, mask):
            v = v_ref[0:kc, :]
            for hh in range(2):
                q = q_ref[:, hh * LANES:(hh + 1) * LANES]
                k = k_ref[0:kc, hh * LANES:(hh + 1) * LANES]
                s = lax.dot_general(q, k, (((1,), (1,)), ((), ())), preferred_element_type=F32)
                if mask is not None:
                    s = jnp.where(mask, s, NEG)
                m_prev = m_s[hh]
                m_new = jnp.maximum(m_prev, jnp.max(s, axis=-1, keepdims=True))
                p = jnp.exp2(s - _lane_tile(m_new, kc // LANES))
                alpha = jnp.exp2(m_prev - m_new)
                ps = p[:, :LANES]
                for j in range(1, kc // LANES):
                    ps = ps + p[:, j * LANES:(j + 1) * LANES]
                l_s[hh] = alpha * l_s[hh] + ps
                acc_s[hh] = alpha * acc_s[hh] + jnp.dot(p.astype(BF16), v, preferred_element_type=F32)
                m_s[hh] = m_new

        @pl.when(ki < last)
        def _():
            block(tk, None)

        for j in range(r):
            @pl.when((ki == last) & (lax.rem(qi, r) == j))
            def _(j=j):
                kc = (j + 1) * tq
                block(kc, lax.broadcasted_iota(jnp.int32, (tq, kc), 1) <= j * tq + lax.broadcasted_iota(jnp.int32, (tq, kc), 0))

        @pl.when(ki == last)
        def _():
            first = lax.broadcasted_iota(jnp.int32, (tq, LANES), 1) < MLA_V
            l0 = jnp.sum(l_s[0], axis=-1, keepdims=True)
            l1 = jnp.sum(l_s[1], axis=-1, keepdims=True)
            o = jnp.where(first, acc_s[0] / l0, acc_s[1] / l1)
            o_ref[...] = o
            cat_ref[...] = o.astype(BF16)
            lse_ref[:, :LANES] = m_s[0] + jnp.log2(l0)
            lse_ref[:, LANES:] = m_s[1] + jnp.log2(l1)

    return pl.pallas_call(
        body, name="mla_attn_fwd", grid=(N_PAIR, nq, nk),
        in_specs=[pl.BlockSpec((tq, 2 * LANES), lambda p, qi, ki: (qi, p)),
                  pl.BlockSpec((tk, 2 * LANES), lambda p, qi, ki: (jnp.minimum(ki, lax.div(qi, r)), p)),
                  pl.BlockSpec((tk, LANES), lambda p, qi, ki: (jnp.minimum(ki, lax.div(qi, r)), MLA_HEADS + p)), ANY],
        out_specs=[pl.BlockSpec((tq, LANES), lambda p, qi, ki: (qi, p)), pl.BlockSpec((tq, 2 * LANES), lambda p, qi, ki: (qi, p)),
                   pl.BlockSpec((tq, LANES), lambda p, qi, ki: (qi, N_PAIR + p))],
        out_shape=[jax.ShapeDtypeStruct((S, MLA_HEADS * MLA_V), F32), jax.ShapeDtypeStruct((S, MLA_HEADS * LANES), F32),
                   jax.ShapeDtypeStruct(cat.shape, cat.dtype)],
        input_output_aliases={3: 2},
        scratch_shapes=[pltpu.VMEM((2, tq, LANES), F32), pltpu.VMEM((2, tq, LANES), F32), pltpu.VMEM((2, tq, LANES), F32)],
        compiler_params=_cparams(("parallel", "parallel", "arbitrary")),
    )(q_all, kv_all, kv_all, cat)


def _attn_delta(do, o):
    prod = do * o
    tm = prod.shape[0]
    first = lax.broadcasted_iota(jnp.int32, (tm, LANES), 1) < MLA_V
    blocks = []
    for p in range(N_PAIR):
        pp = prod[:, p * LANES:(p + 1) * LANES]
        blocks.append(jnp.broadcast_to(jnp.sum(jnp.where(first, pp, 0.0), axis=-1, keepdims=True), (tm, LANES)))
        blocks.append(jnp.broadcast_to(jnp.sum(jnp.where(first, 0.0, pp), axis=-1, keepdims=True), (tm, LANES)))
    return jnp.concatenate(blocks, axis=1)


def _attn_bwd(q_all, kv_all, dcat, lse, delta, tq=1024, tk=1024):
    S = q_all.shape[0]
    tq, tk = min(tq, S), min(tk, S)
    nq, nk, nb = S // tq, S // tk, tk // LANES
    assert tq == tk

    def body(q_ref, k_ref, v_ref, do_ref, lse_ref, dl_ref, dq_ref, dk_ref, dv_ref, dk_s, dv_s):
        ki, qi = pl.program_id(1), pl.program_id(2)

        @pl.when((ki == 0) & (qi == 0))
        def _():
            dq_ref[...] = jnp.zeros(dq_ref.shape, F32)

        @pl.when(qi == 0)
        def _():
            dk_s[...] = jnp.zeros(dk_s.shape, F32)
            dv_s[...] = jnp.zeros(dv_s.shape, F32)

        def block(r0, nr, kc, mask):
            v, do = v_ref[0:kc, :], do_ref[r0:r0 + nr, :]
            first = lax.broadcasted_iota(jnp.int32, (nr, LANES), 1) < MLA_V
            firstk = lax.broadcasted_iota(jnp.int32, (kc, LANES), 1) < MLA_V
            do_b = do.astype(BF16)
            rows = pl.ds(pl.multiple_of(qi * tq + r0, LANES), nr)
            for hh in range(2):
                sl = slice(hh * LANES, (hh + 1) * LANES)
                q, k = q_ref[r0:r0 + nr, sl], k_ref[0:kc, sl]
                s = lax.dot_general(q, k, (((1,), (1,)), ((), ())), preferred_element_type=F32)
                if mask is not None:
                    s = jnp.where(mask, s, NEG)
                p = jnp.exp2(s - _lane_tile(lse_ref[r0:r0 + nr, sl], kc // LANES))
                do_h = jnp.where(first if hh == 0 else ~first, do, 0.0).astype(BF16)
                dp = lax.dot_general(do_h, v, (((1,), (1,)), ((), ())), preferred_element_type=F32)
                ds_b = (p * (dp - _lane_tile(dl_ref[r0:r0 + nr, sl], kc // LANES)) * LN2).astype(BF16)
                pv = lax.dot_general(p.astype(BF16), do_b, (((0,), (0,)), ((), ())), preferred_element_type=F32)
                dv_s[0:kc, :] += jnp.where(firstk if hh == 0 else ~firstk, pv, 0.0)
                dk_s[0:kc, sl] += lax.dot_general(ds_b, q, (((0,), (0,)), ((), ())), preferred_element_type=F32)
                dq_ref[rows, sl] += jnp.dot(ds_b, k, preferred_element_type=F32)

        @pl.when(qi > ki)
        def _():
            block(0, tq, tk, None)

        @pl.when(qi == ki)
        def _():
            h = tq // 2
            block(0, h, h, _causal_mask(0, 0, h, h))
            block(h, h, tk, lax.broadcasted_iota(jnp.int32, (h, tk), 1) <= h + lax.broadcasted_iota(jnp.int32, (h, tk), 0))

        @pl.when(qi == nq - 1)
        def _():
            dk_ref[...] = dk_s[...]
            dv_ref[...] = dv_s[...]

    wide = pl.BlockSpec((tq, 2 * LANES), lambda p, ki, qi: (jnp.maximum(qi, ki), p))
    return pl.pallas_call(
        body, name="mla_attn_bwd", grid=(N_PAIR, nk, nq),
        in_specs=[wide, pl.BlockSpec((tk, 2 * LANES), lambda p, ki, qi: (ki, p)),
                  pl.BlockSpec((tk, LANES), lambda p, ki, qi: (ki, MLA_HEADS + p)),
                  pl.BlockSpec((tq, LANES), lambda p, ki, qi: (jnp.maximum(qi, ki), N_PAIR + p)), wide, wide],
        out_specs=[pl.BlockSpec((S, 2 * LANES), lambda p, ki, qi: (0, p)),
                   pl.BlockSpec((tk, 2 * LANES), lambda p, ki, qi: (ki, p)), pl.BlockSpec((tk, LANES), lambda p, ki, qi: (ki, p))],
        out_shape=[jax.ShapeDtypeStruct((S, MLA_HEADS * LANES), F32), jax.ShapeDtypeStruct((S, MLA_HEADS * LANES), F32),
                   jax.ShapeDtypeStruct((S, MLA_HEADS * MLA_V), F32)],
        scratch_shapes=[pltpu.VMEM((tk, 2 * LANES), F32), pltpu.VMEM((tk, LANES), F32)],
        compiler_params=_cparams(("parallel", "arbitrary", "arbitrary")),
    )(q_all, kv_all, kv_all, dcat, lse, delta)


MEM_SCALE = MEM_HEAD_DIM ** -0.5


def _mem_probs(q, k):
    s = lax.dot_general(q, k, (((1,), (1,)), ((), ())), preferred_element_type=F32) * MEM_SCALE
    e = jnp.exp(s - jnp.max(s, axis=-1, keepdims=True))
    return e / jnp.sum(e, axis=-1, keepdims=True)


def _mem_attn_fwd(qm, km, vm, tq=512):
    S, W = qm.shape
    M = km.shape[0]

    def body(q_ref, k_ref, v_ref, o_ref):
        for h in range(MEM_HEADS):
            sl = slice(h * MEM_HEAD_DIM, (h + 1) * MEM_HEAD_DIM)
            p = _mem_probs(q_ref[:, sl], k_ref[:, sl])
            o_ref[:, sl] = jnp.dot(p.astype(BF16), v_ref[:, sl], preferred_element_type=F32).astype(BF16)

    row = pl.BlockSpec((tq, W), lambda i: (i, 0))
    full = pl.BlockSpec((M, W), lambda i: (0, 0))
    return pl.pallas_call(body, name="mem_attn_fwd", grid=(S // tq,), in_specs=[row, full, full], out_specs=row,
                          out_shape=jax.ShapeDtypeStruct((S, W), BF16), compiler_params=_cparams(("parallel",)))(qm, km, vm)


def _mem_attn_bwd(qm, km, vm, dom, tq=512):
    S, W = qm.shape
    M = km.shape[0]

    def body(q_ref, k_ref, v_ref, do_ref, dq_ref, dk_ref, dv_ref):
        i = pl.program_id(0)

        @pl.when(i == 0)
        def _():
            dk_ref[...] = jnp.zeros(dk_ref.shape, F32)
            dv_ref[...] = jnp.zeros(dv_ref.shape, F32)

        for h in range(MEM_HEADS):
            sl = slice(h * MEM_HEAD_DIM, (h + 1) * MEM_HEAD_DIM)
            q, k, v, do = q_ref[:, sl], k_ref[:, sl], v_ref[:, sl], do_ref[:, sl]
            p = _mem_probs(q, k)
            dv_ref[:, sl] += lax.dot_general(p.astype(BF16), do, (((0,), (0,)), ((), ())), preferred_element_type=F32)
            dp = lax.dot_general(do, v, (((1,), (1,)), ((), ())), preferred_element_type=F32)
            ds = (p * (dp - jnp.sum(dp * p, axis=-1, keepdims=True)) * MEM_SCALE).astype(BF16)
            dq_ref[:, sl] = jnp.dot(ds, k, preferred_element_type=F32).astype(BF16)
            dk_ref[:, sl] += lax.dot_general(ds, q, (((0,), (0,)), ((), ())), preferred_element_type=F32)

    row = pl.BlockSpec((tq, W), lambda i: (i, 0))
    full = pl.BlockSpec((M, W), lambda i: (0, 0))
    return pl.pallas_call(
        body, name="mem_attn_bwd", grid=(S // tq,), in_specs=[row, full, full, row], out_specs=[row, full, full],
        out_shape=[jax.ShapeDtypeStruct((S, W), BF16), jax.ShapeDtypeStruct((M, W), F32), jax.ShapeDtypeStruct((M, W), F32)],
        compiler_params=_cparams(("arbitrary",)),
    )(qm, km, vm, dom)


L = SSD_CHUNK
N_SPAIR = SSD_HEADS // 2
GRP_W = SSD_INNER // 2
XB0, XC0 = SSD_INNER, SSD_INNER + 2 * SSD_STATE


def _cumsum_rows(a, reverse=False):
    row = lax.broadcasted_iota(jnp.int32, a.shape, 0)
    x, sft = a, 1
    while sft < L:
        if reverse:
            x = x + jnp.where(row < L - sft, pltpu.roll(x, L - sft, axis=0), 0.0)
        else:
            x = x + jnp.where(row >= sft, pltpu.roll(x, sft, axis=0), 0.0)
        sft *= 2
    return x


def _shift_down(cur, prev, s):
    if s == 0:
        return cur
    row = lax.broadcasted_iota(jnp.int32, cur.shape, 0)
    return jnp.where(row < s, pltpu.roll(prev, s, axis=0), pltpu.roll(cur, s, axis=0))


def _shift_up(cur, nxt, s):
    if s == 0:
        return cur
    row = lax.broadcasted_iota(jnp.int32, cur.shape, 0)
    return jnp.where(row >= L - s, pltpu.roll(nxt, L - s, axis=0), pltpu.roll(cur, L - s, axis=0))


def _ssd_conv(u, prev, cw, cb):
    delayed = [u] + [_shift_down(u, prev, s) for s in range(1, SSD_CONV)]
    conv = cb + cw[SSD_CONV - 1:SSD_CONV, :] * u
    for s in range(1, SSD_CONV):
        conv = conv + cw[SSD_CONV - 1 - s:SSD_CONV - s, :] * delayed[s]
    return conv, delayed


def _pair_lanes(v, h0, first):
    return jnp.where(first, v[:, h0:h0 + 1], v[:, h0 + 1:h0 + 2])


def _ssd_common(u, prev, dt_raw, cw, cb, dtb, alog):
    conv, delayed = _ssd_conv(u, prev, cw, cb)
    sg = _sigmoid(conv)
    xa = conv * sg
    dpre = dt_raw + dtb
    dtv = jnp.maximum(dpre, 0.0) + jnp.log1p(jnp.exp(-jnp.abs(dpre)))
    a_row = -jnp.exp(alog)
    cs = _cumsum_rows(dtv * a_row)
    return conv, sg, xa, dpre, dtv, a_row, cs, delayed


def _ssd_pair_fwd(xa, dtv, cs, csT, G, Cg, Bg, Sp, dsk, pp, first, tri, rowfirst):
    h0 = 2 * pp
    x = xa[:, pp * LANES:(pp + 1) * LANES]
    xdt = x * _pair_lanes(dtv, h0, first)
    xdt_b = xdt.astype(BF16)
    Ms, yd = [], []
    for h in (h0, h0 + 1):
        lam = jnp.exp(jnp.where(tri, cs[:, h:h + 1] - csT[h:h + 1, :], NEG))
        M = G * lam
        Ms.append((M, lam))
        yd.append(jnp.dot(M.astype(BF16), xdt_b, preferred_element_type=F32))
    T = lax.dot_general(Cg, Sp.astype(BF16), (((1,), (1,)), ((), ())), preferred_element_type=F32)
    E = jnp.exp(_pair_lanes(cs, h0, first))
    yoff = E * T
    csl = cs[L - 1:L, :]
    Fd = jnp.exp(_pair_lanes(csl, h0, first) - _pair_lanes(cs, h0, first))
    el = jnp.exp(csl)
    el_rows = jnp.where(rowfirst, el[:, h0:h0 + 1], el[:, h0 + 1:h0 + 2])
    Sloc = lax.dot_general((xdt * Fd).astype(BF16), Bg, (((0,), (0,)), ((), ())), preferred_element_type=F32)
    S_new = el_rows * Sp + Sloc
    y = jnp.where(first, yd[0], yd[1]) + yoff + x * _pair_lanes(dsk, h0, first[:1])
    return y, S_new, (x, xdt, xdt_b, Ms, E, yoff, Fd, el, el_rows)


def _ssd_masks():
    lane = lax.broadcasted_iota(jnp.int32, (L, LANES), 1)
    row = lax.broadcasted_iota(jnp.int32, (L, LANES), 0)
    return lane, row, lane < SSD_HEAD_DIM, row >= lane, row[:, :1] < SSD_HEAD_DIM


def _ssd_specs(nc, rev):
    def cidx(i):
        return nc - 1 - i if rev else i
    z = pl.BlockSpec((L, SSD_INNER), lambda i: (cidx(i), PZ // SSD_INNER))
    u = pl.BlockSpec((L, SSD_XBC), lambda i: (cidx(i), PX // SSD_XBC))
    dt = pl.BlockSpec((L, LANES), lambda i: (cidx(i), PDT // LANES))
    return cidx, z, u, dt


def _vec(w):
    return pl.BlockSpec((1, w), lambda i: (0, 0))


def _ssd_fwd(proj, cw, cb, dtb, alog, dsk, ng):
    S = proj.shape[0]
    nc = S // L

    def body(z_ref, u_ref, dt_ref, cw_ref, cb_ref, dtb_ref, alog_ref, dsk_ref, ng_ref, y_ref, st_ref, prev_s, state_s):
        c = pl.program_id(0)

        @pl.when(c == 0)
        def _():
            prev_s[...] = jnp.zeros(prev_s.shape, F32)
            state_s[...] = jnp.zeros(state_s.shape, F32)

        u = u_ref[...]
        _, _, xa, _, dtv, _, cs, _ = _ssd_common(u, prev_s[...], dt_ref[...], cw_ref[...], cb_ref[...], dtb_ref[...], alog_ref[...])
        prev_s[...] = u
        csT = cs.T
        _, _, first, tri, rowfirst = _ssd_masks()
        dsk_v = dsk_ref[...]
        ys = []
        for g in range(2):
            Bg = xa[:, XB0 + g * SSD_STATE:XB0 + (g + 1) * SSD_STATE].astype(BF16)
            Cg = xa[:, XC0 + g * SSD_STATE:XC0 + (g + 1) * SSD_STATE].astype(BF16)
            G = lax.dot_general(Cg, Bg, (((1,), (1,)), ((), ())), preferred_element_type=F32)
            for pp in (2 * g, 2 * g + 1):
                Sp = state_s[pp]
                st_ref[pp * LANES:(pp + 1) * LANES, :] = Sp
                y, S_new, _ = _ssd_pair_fwd(xa, dtv, cs, csT, G, Cg, Bg, Sp, dsk_v, pp, first, tri, rowfirst)
                state_s[pp] = S_new
                ys.append(y)
        z = z_ref[...]
        for g in range(2):
            yg = jnp.concatenate([ys[2 * g], ys[2 * g + 1]], axis=1)
            zg = z[:, g * GRP_W:(g + 1) * GRP_W]
            gated = yg * (zg * _sigmoid(zg))
            r = lax.rsqrt(jnp.mean(gated * gated, axis=-1, keepdims=True) + RMS_EPS)
            y_ref[:, g * GRP_W:(g + 1) * GRP_W] = (gated * r * ng_ref[:, g * GRP_W:(g + 1) * GRP_W]).astype(BF16)

    _, zs, us, dts = _ssd_specs(nc, False)
    return pl.pallas_call(
        body, name="ssd_fwd", grid=(nc,),
        in_specs=[zs, us, dts, pl.BlockSpec((8, SSD_XBC), lambda i: (0, 0)), _vec(SSD_XBC), _vec(LANES), _vec(LANES), _vec(LANES),
                  _vec(SSD_INNER)],
        out_specs=[pl.BlockSpec((L, SSD_INNER), lambda i: (i, 0)), pl.BlockSpec((N_SPAIR * LANES, SSD_STATE), lambda i: (i, 0))],
        out_shape=[jax.ShapeDtypeStruct((S, 2 * SSD_INNER), BF16), jax.ShapeDtypeStruct((nc * N_SPAIR * LANES, SSD_STATE), F32)],
        scratch_shapes=[pltpu.VMEM((L, SSD_XBC), F32), pltpu.VMEM((N_SPAIR, LANES, SSD_STATE), F32)],
        compiler_params=_cparams(("arbitrary",)),
    )(proj, proj, proj, cw, cb, dtb, alog, dsk, ng)


def _ssd_bwd(proj, states, dy, cw, cb, dtb, alog, dsk, ng):
    S = proj.shape[0]
    nc = S // L

    def body(z_ref, u_ref, up_ref, dt_ref, st_ref, dy_ref, cw_ref, cb_ref, dtb_ref, alog_ref, dsk_ref, ng_ref,
             dz_ref, du_ref, ddt_ref, dcw_ref, dcb_ref, ddtb_ref, dalog_ref, ddsk_ref, dng_ref,
             dS_s, dconv_s, dD_s):
        i = pl.program_id(0)
        c = nc - 1 - i

        @pl.when(i == 0)
        def _():
            dS_s[...] = jnp.zeros(dS_s.shape, F32)
            dconv_s[...] = jnp.zeros(dconv_s.shape, F32)
            dD_s[...] = jnp.zeros(dD_s.shape, F32)
            for r in (dcw_ref, dcb_ref, ddtb_ref, dalog_ref, ddsk_ref, dng_ref):
                r[...] = jnp.zeros(r.shape, F32)

        u = u_ref[...]
        prev = jnp.where(c > 0, up_ref[...], 0.0)
        cw_v = cw_ref[...]
        conv, sg, xa, dpre, dtv, a_row, cs, delayed = _ssd_common(u, prev, dt_ref[...], cw_v, cb_ref[...], dtb_ref[...], alog_ref[...])
        csT = cs.T
        lane, row, first, tri, rowfirst = _ssd_masks()
        dsk_v = dsk_ref[...]

        fw = []
        Gs, Bs, Cs = [], [], []
        for g in range(2):
            Bg = xa[:, XB0 + g * SSD_STATE:XB0 + (g + 1) * SSD_STATE].astype(BF16)
            Cg = xa[:, XC0 + g * SSD_STATE:XC0 + (g + 1) * SSD_STATE].astype(BF16)
            G = lax.dot_general(Cg, Bg, (((1,), (1,)), ((), ())), preferred_element_type=F32)
            Gs.append(G), Bs.append(Bg), Cs.append(Cg)
            for pp in (2 * g, 2 * g + 1):
                Sp = st_ref[pp * LANES:(pp + 1) * LANES, :]
                y, _, keep = _ssd_pair_fwd(xa, dtv, cs, csT, G, Cg, Bg, Sp, dsk_v, pp, first, tri, rowfirst)
                fw.append((y, Sp, keep))

        z = z_ref[...]
        dys = []
        for g in range(2):
            sl = slice(g * GRP_W, (g + 1) * GRP_W)
            yg = jnp.concatenate([fw[2 * g][0], fw[2 * g + 1][0]], axis=1)
            zg = z[:, sl]
            sz = _sigmoid(zg)
            silu_z = zg * sz
            gated = yg * silu_z
            r = lax.rsqrt(jnp.mean(gated * gated, axis=-1, keepdims=True) + RMS_EPS)
            nh = gated * r
            dout = dy_ref[:, sl]
            dng_ref[:, sl] += jnp.sum(dout * nh, axis=0, keepdims=True)
            dnh = dout * ng_ref[:, sl]
            dgated = r * (dnh - nh * jnp.mean(dnh * nh, axis=-1, keepdims=True))
            dz_ref[:, sl] = dgated * yg * (sz * (1.0 + zg * (1.0 - sz)))
            dyg = dgated * silu_z
            dys.append(dyg[:, :LANES]), dys.append(dyg[:, LANES:])

        dcs_col = [0.0] * SSD_HEADS
        dcs_row = [None] * SSD_HEADS
        ddt_col = [None] * SSD_HEADS
        dxs = []
        dB, dC = [None, None], [None, None]
        last = row == L - 1
        for g in range(2):
            Bg, Cg, G = Bs[g], Cs[g], Gs[g]
            dG = jnp.zeros((L, L), F32)
            dBg = jnp.zeros((L, SSD_STATE), F32)
            dCg = jnp.zeros((L, SSD_STATE), F32)
            for pp in (2 * g, 2 * g + 1):
                h0 = 2 * pp
                y, Sp, (x, xdt, xdt_b, Ms, E, yoff, Fd, el, el_rows) = fw[pp]
                dY = dys[pp]
                dS = dS_s[pp]
                dS_b, Sp_b = dS.astype(BF16), Sp.astype(BF16)
                dD_s[:, pp * LANES:(pp + 1) * LANES] += jnp.sum(dY * x, axis=0, keepdims=True)
                dx = dY * _pair_lanes(dsk_v, h0, first[:1])
                dxdt = jnp.zeros((L, LANES), F32)
                dY_b = dY.astype(BF16)
                for hh, h in enumerate((h0, h0 + 1)):
                    hm = first if hh == 0 else ~first
                    M, lam = Ms[hh]
                    dYh = jnp.where(hm, dY, 0.0).astype(BF16)
                    dM = lax.dot_general(dYh, xdt_b, (((1,), (1,)), ((), ())), preferred_element_type=F32)
                    W = dM * M
                    dcs_col[h] = dcs_col[h] + jnp.sum(W, axis=-1, keepdims=True)
                    dcs_row[h] = jnp.sum(W, axis=0, keepdims=True)
                    dG = dG + dM * lam
                    mt = lax.dot_general(M.astype(BF16), dY_b, (((0,), (0,)), ((), ())), preferred_element_type=F32)
                    dxdt = dxdt + jnp.where(hm, mt, 0.0)
                dT = (E * dY).astype(BF16)
                dCg = dCg + jnp.dot(dT, Sp_b, preferred_element_type=F32)
                dS_in = lax.dot_general(dT, Cg, (((0,), (0,)), ((), ())), preferred_element_type=F32) + el_rows * dS
                q1 = dY * yoff
                dZ = lax.dot_general(Bg, dS_b, (((1,), (1,)), ((), ())), preferred_element_type=F32)
                dBg = dBg + jnp.dot((xdt * Fd).astype(BF16), dS_b, preferred_element_type=F32)
                dxdt = dxdt + dZ * Fd
                q2 = dZ * xdt * Fd
                dSS = dS * Sp
                for hh, h in enumerate((h0, h0 + 1)):
                    hm = first if hh == 0 else ~first
                    rs1 = jnp.sum(jnp.where(hm, q1, 0.0), axis=-1, keepdims=True)
                    rs2 = jnp.sum(jnp.where(hm, q2, 0.0), axis=-1, keepdims=True)
                    rmask = rowfirst if hh == 0 else ~rowfirst
                    d_el = jnp.sum(jnp.sum(jnp.where(rmask, dSS, 0.0), axis=-1, keepdims=True), axis=0, keepdims=True)
                    tail = jnp.sum(rs2, axis=0, keepdims=True) + d_el * el[:, h:h + 1]
                    dcs_col[h] = dcs_col[h] + (rs1 - rs2 + jnp.where(last[:, :1], tail, 0.0))
                    ddt_col[h] = jnp.sum(jnp.where(hm, dxdt * x, 0.0), axis=-1, keepdims=True)
                dS_s[pp] = dS_in
                dxs.append(dx + dxdt * _pair_lanes(dtv, h0, first))
            dG_b = dG.astype(BF16)
            dC[g] = dCg + jnp.dot(dG_b, Bg, preferred_element_type=F32)
            dB[g] = dBg + lax.dot_general(dG_b, Cg, (((0,), (0,)), ((), ())), preferred_element_type=F32)

        dcs_c, dcs_r, ddt_c = (jnp.zeros((L, LANES), F32) for _ in range(3))
        for h in range(SSD_HEADS):
            dcs_c = dcs_c + jnp.where(lane == h, dcs_col[h], 0.0)
            dcs_r = dcs_r + jnp.where(row == h, dcs_row[h], 0.0)
            ddt_c = ddt_c + jnp.where(lane == h, ddt_col[h], 0.0)
        dcs = dcs_c - dcs_r.T
        da = _cumsum_rows(dcs, reverse=True)
        ddt_c = ddt_c + da * a_row
        dalog_ref[...] += jnp.sum(da * dtv, axis=0, keepdims=True) * a_row
        ddt_raw = ddt_c * _sigmoid(dpre)
        ddt_ref[...] = ddt_raw
        ddtb_ref[...] += jnp.sum(ddt_raw, axis=0, keepdims=True)

        dxa = jnp.concatenate(dxs + dB + dC, axis=1)
        dconv = dxa * (sg * (1.0 + conv * (1.0 - sg)))
        dcb_ref[...] += jnp.sum(dconv, axis=0, keepdims=True)
        nxt = dconv_s[...]
        du = cw_v[SSD_CONV - 1:SSD_CONV, :] * dconv
        dcw_ref[SSD_CONV - 1:SSD_CONV, :] += jnp.sum(dconv * u, axis=0, keepdims=True)
        for s in range(1, SSD_CONV):
            k = SSD_CONV - 1 - s
            du = du + cw_v[k:k + 1, :] * _shift_up(dconv, nxt, s)
            dcw_ref[k:k + 1, :] += jnp.sum(dconv * delayed[s], axis=0, keepdims=True)
        du_ref[...] = du
        dconv_s[...] = dconv

        @pl.when(i == nc - 1)
        def _():
            acc = dD_s[...]
            lane1 = lax.broadcasted_iota(jnp.int32, (1, LANES), 1)
            lanew = lax.broadcasted_iota(jnp.int32, acc.shape, 1)
            out = jnp.zeros((1, LANES), F32)
            for h in range(SSD_HEADS):
                tot = jnp.sum(jnp.where((lanew >= h * SSD_HEAD_DIM) & (lanew < (h + 1) * SSD_HEAD_DIM), acc, 0.0),
                              axis=-1, keepdims=True)
                out = out + jnp.where(lane1 == h, tot, 0.0)
            ddsk_ref[...] = out

    cidx, zs, us, dts = _ssd_specs(nc, True)
    ups = pl.BlockSpec((L, SSD_XBC), lambda i: (jnp.maximum(cidx(i) - 1, 0), PX // SSD_XBC))
    rowc = lambda w: pl.BlockSpec((L, w), lambda i: (cidx(i), 0))
    return pl.pallas_call(
        body, name="ssd_bwd", grid=(nc,),
        in_specs=[zs, us, ups, dts, pl.BlockSpec((N_SPAIR * LANES, SSD_STATE), lambda i: (cidx(i), 0)), rowc(SSD_INNER),
                  pl.BlockSpec((8, SSD_XBC), lambda i: (0, 0)), _vec(SSD_XBC), _vec(LANES), _vec(LANES), _vec(LANES), _vec(SSD_INNER)],
        out_specs=[rowc(SSD_INNER), rowc(SSD_XBC), rowc(LANES), pl.BlockSpec((8, SSD_XBC), lambda i: (0, 0)), _vec(SSD_XBC),
                   _vec(LANES), _vec(LANES), _vec(LANES), _vec(SSD_INNER)],
        out_shape=[jax.ShapeDtypeStruct((S, SSD_INNER), F32), jax.ShapeDtypeStruct((S, SSD_XBC), F32),
                   jax.ShapeDtypeStruct((S, LANES), F32), jax.ShapeDtypeStruct((8, SSD_XBC), F32),
                   jax.ShapeDtypeStruct((1, SSD_XBC), F32), jax.ShapeDtypeStruct((1, LANES), F32),
                   jax.ShapeDtypeStruct((1, LANES), F32), jax.ShapeDtypeStruct((1, LANES), F32),
                   jax.ShapeDtypeStruct((1, SSD_INNER), F32)],
        scratch_shapes=[pltpu.VMEM((N_SPAIR, LANES, SSD_STATE), F32), pltpu.VMEM((L, SSD_XBC), F32),
                        pltpu.VMEM((1, SSD_INNER), F32)],
        compiler_params=_cparams(("arbitrary",)),
    )(proj, proj, proj, proj, states, dy, cw, cb, dtb, alog, dsk, ng)


_IN_SEGS = ((PZ, 0, 512), (PX, 512, 1024), (PDT, 1536, 8), (PQ, 1544, 384), (PKV, 1928, 256), (PKR + KR_LANE, 2184, 32))


def _pad_w_in(w):
    parts, at = [], 0
    for dst, src, n in sorted(_IN_SEGS):
        parts += [jnp.zeros((w.shape[0], dst - at), w.dtype), w[:, src:src + n]]
        at = dst + n
    return jnp.concatenate(parts + [jnp.zeros((w.shape[0], PW - at), w.dtype)], axis=1)


def _unpad_w_in(wp):
    segs = sorted(_IN_SEGS, key=lambda t: t[1])
    return jnp.concatenate([wp[:, dst:dst + n] for dst, src, n in segs], axis=1)


def _pad_w_q(w):
    return jnp.pad(w.reshape(MLA_Q_RANK, MLA_HEADS, MLA_QK), ((0, 0), (0, 0), (0, LANES - MLA_QK))).reshape(MLA_Q_RANK, MLA_HEADS * LANES)


def _unpad_w_q(wp):
    return wp.reshape(MLA_Q_RANK, MLA_HEADS, LANES)[:, :, :MLA_QK].reshape(MLA_Q_RANK, MLA_HEADS * MLA_QK)


def _pad_w_kv(w):
    w3 = w.reshape(MLA_KV_RANK, MLA_HEADS, MLA_NOPE + MLA_V)
    k = jnp.pad(w3[:, :, :MLA_NOPE], ((0, 0), (0, 0), (0, LANES - MLA_NOPE))).reshape(MLA_KV_RANK, MLA_HEADS * LANES)
    return jnp.concatenate([k, w3[:, :, MLA_NOPE:].reshape(MLA_KV_RANK, MLA_HEADS * MLA_V)], axis=1)


def _unpad_w_kv(wp):
    k = wp[:, :MLA_HEADS * LANES].reshape(MLA_KV_RANK, MLA_HEADS, LANES)[:, :, :MLA_NOPE]
    v = wp[:, MLA_HEADS * LANES:].reshape(MLA_KV_RANK, MLA_HEADS, MLA_V)
    return jnp.concatenate([k, v], axis=2).reshape(MLA_KV_RANK, MLA_HEADS * (MLA_NOPE + MLA_V))


def _head_lanes(v):
    return jnp.pad(v, ((0, 0), (0, LANES - v.shape[1])))


def _local_step(x, mem, positions, tgt, P, weights_at=None, emit=None):
    tabs = _rope_tables(positions)
    G = {}
    emit = emit or (lambda names, grads: 0.0)

    rows = min(1024, x.shape[0])
    h0, h0b = _ln_fwd([(x, 1.0)], P["ln_in_g"], P["ln_in_b"], "ln_in", tm=rows)
    if weights_at is not None:
        P = {**P, **weights_at("first", (h0b,) + tuple(tabs))}
    proj = _mm(h0b, P["w_in"], "nn", "proj_in", tm=1024, tn=PW // 2)
    if weights_at is not None:
        P = {**P, **weights_at("mid", proj)}
    cat, states = _ssd_fwd(proj, P["conv_w"], P["conv_b"], P["dt_bias"], P["a_log"], P["d_skip"], P["ssd_norm_g"])
    qn, kvn, kpe = _mla_prep(proj, P["q_norm_g"], P["kv_norm_g"], tabs, tm=rows)

    def q_epi(acc, c, s1, s2):
        return (jnp.concatenate([_rope_block(acc[:, h * LANES:(h + 1) * LANES], c, s1, s2) for h in range(MLA_HEADS)], axis=1)
                * Q_PRESCALE,)

    q_all = _mm(qn, P["w_q_up"], "nn", "q_up", out_dtypes=(BF16,), epi=q_epi, extras=[(t, "m") for t in tabs])

    def kv_epi(acc, kp):
        kb = [acc[:, h * LANES:(h + 1) * LANES] + kp for h in range(MLA_HEADS)]
        return (jnp.concatenate(kb + [acc[:, MLA_HEADS * LANES:]], axis=1),)

    kv_all = _mm(kvn, P["w_kv_up"], "nn", "kv_up", out_dtypes=(BF16,), epi=kv_epi, extras=[(kpe, "m")])
    o_att, lse, cat = _attn_fwd(q_all, kv_all, cat)
    def resid_ln_epi(acc, h, g, b):
        r = ALPHA * h + acc
        y = _ln_stats(r)[0] * g + b
        return r, y, y

    ln_out = (F32, F32, BF16)
    r1, h1, h1b = _mm(cat, P["w_mix_out"], "nn", "mix_out", tm=512, out_dtypes=ln_out, epi=resid_ln_epi,
                      extras=[(h0, "mn"), (P["ln1_g"], "n"), (P["ln1_b"], "n")])
    if weights_at is not None:
        P = {**P, **weights_at("late", h1b)}
    qm = _mm(h1b, P["w_mem_q"], "nn", "mem_q", tm=1024, out_dtypes=(BF16,))
    km = _mm(mem, P["w_mem_k"], "nn", "mem_k", out_dtypes=(BF16,))
    vm = _mm(mem, P["w_mem_v"], "nn", "mem_v", out_dtypes=(BF16,))
    om = _mem_attn_fwd(qm, km, vm, tq=rows)
    r2, h2, h2b = _mm(om, P["w_mem_o"], "nn", "mem_o", tm=512, out_dtypes=ln_out, epi=resid_ln_epi,
                      extras=[(h1, "mn"), (P["ln2_g"], "n"), (P["ln2_b"], "n")])

    def up_epi(acc):
        r = jnp.maximum(acc, 0.0)
        return r * r, 2.0 * r

    act, dact = _mm(h2b, P["w_up"], "nn", "mlp_up", tm=1024, tn=1024, out_dtypes=(BF16, BF16), epi=up_epi)

    def loss_epi(acc, h, g, b, t):
        xh, rstd = _ln_stats(ALPHA * h + acc)
        diff = xh * g + b - t
        part = 0.5 * jnp.sum(jnp.mean(diff * diff, axis=-1, keepdims=True), axis=0, keepdims=True)
        dr, pg, pb = _ln_bwd_tile(xh, rstd, diff * (1.0 / D_MODEL), g)
        return dr, dr, jnp.broadcast_to(part, pg.shape), pg, pb

    dr3, dr3b, loss, G["ln3_g"], G["ln3_b"] = _mm(
        act, P["w_down"], "nn", "mlp_down", tm=512, tk=D_FF, out_dtypes=(F32, BF16), epi=loss_epi, n_sums=3,
        extras=[(h2, "mn"), (P["ln3_g"], "n"), (P["ln3_b"], "n"), (tgt, "mn")])
    loss = loss[:, :LANES]

    def ln_bwd_epi(acc, d, r, g):
        dr, pg, pb = _ln_bwd_tile(*_ln_stats(r), ALPHA * d + acc, g)
        return dr, dr, pg, pb

    def dact_epi(acc, g):
        return (acc.astype(BF16) * g,)

    du = _mm(dr3b, P["w_down"], "nt", "mlp_down_dx", tm=1024, tn=1024, out_dtypes=(BF16,), epi=dact_epi, extras=[(dact, "mn")])
    G["w_down"] = _mm(act, dr3b, "tn", "mlp_down_dw", tm=1024, tk=TOK_K, out_dtypes=(BF16,))
    G["w_up"] = _mm(h2b, du, "tn", "mlp_up_dw", tm=1024, tn=D_FF // N_DEV, tk=TOK_K, out_dtypes=(BF16,), col_slots=True)
    tie = emit(("w_down", "w_up"), G)
    dr2, dr2b, G["ln2_g"], G["ln2_b"] = _mm(
        du, P["w_up"], "nt", "mlp_up_dx", tm=512, tk=D_FF, out_dtypes=(F32, BF16), epi=ln_bwd_epi, n_sums=2,
        extras=[(dr3, "mn"), (r2, "mn"), (P["ln2_g"] + tie, "n")])

    dom = _mm(dr2b, P["w_mem_o"], "nt", "mem_o_dx", tm=1024, out_dtypes=(BF16,))
    G["w_mem_o"] = _mm(om, dr2b, "tn", "mem_o_dw", tm=1024, tk=TOK_K, out_dtypes=(BF16,))
    dqm, dkm, dvm = _mem_attn_bwd(qm, km, vm, dom, tq=rows)
    G["w_mem_q"] = _mm(h1b, dqm, "tn", "mem_q_dw", tm=1024, tk=TOK_K, out_dtypes=(BF16,))
    G["w_mem_k"] = _mm(mem, dkm, "tn", "mem_k_dw", tm=1024, out_dtypes=(BF16,))
    G["w_mem_v"] = _mm(mem, dvm, "tn", "mem_v_dw", tm=1024, out_dtypes=(BF16,))
    tie = emit(("w_mem_o", "w_mem_q", "w_mem_k", "w_mem_v"), G)
    dr1, dr1b, G["ln1_g"], G["ln1_b"] = _mm(
        dqm, P["w_mem_q"], "nt", "mem_q_dx", tm=512, out_dtypes=(F32, BF16), epi=ln_bwd_epi, n_sums=2,
        extras=[(dr2, "mn"), (r1, "mn"), (P["ln1_g"] + tie, "n")])

    def dcat_epi(acc, o):
        return acc, _attn_delta(acc[:, SSD_INNER:], o)

    dcat, delta = _mm(dr1b, P["w_mix_out"], "nt", "mix_out_dx", tm=512, out_dtypes=(F32, F32), epi=dcat_epi, extras=[(o_att, "m")])
    G["w_mix_out"] = _mm(cat, dr1b, "tn", "mix_out_dw", tm=1024, tk=TOK_K, out_dtypes=(BF16,))
    dq_all, dk_all, dv_all = _attn_bwd(q_all, kv_all, dcat, lse, delta)
    dq_pre = _rope_bwd_all(dq_all, tabs, DQ_POSTSCALE, tm=rows)
    G["w_q_up"] = _mm(qn, dq_pre, "tn", "q_up_dw", tk=TOK_K, out_dtypes=(BF16,))
    dqn = _mm(dq_pre, P["w_q_up"], "nt", "q_up_dx", tm=1024)
    wk, wv = P["w_kv_up"][:, :MLA_HEADS * LANES], P["w_kv_up"][:, MLA_HEADS * LANES:]
    G["w_kv_up"] = jnp.concatenate([_mm(kvn, dk_all, "tn", "k_up_dw", tk=2048, out_dtypes=(BF16,)),
                                    _mm(kvn, dv_all, "tn", "v_up_dw", tk=2048, out_dtypes=(BF16,))], axis=1)
    dkvn = _mm(dk_all, wk, "nt", "k_up_dx", tm=1024, epi=lambda acc, e: (acc + e,),
               extras=[(_mm(dv_all, wv, "nt", "v_up_dx", tm=1024), "mn")])
    dql, dkvl, dkr, G["q_norm_g"], G["kv_norm_g"] = _mla_prep_bwd(proj, P["q_norm_g"], P["kv_norm_g"], tabs, dqn, dkvn, dk_all, tm=rows)
    (dz, dxbc, ddt, G["conv_w"], G["conv_b"], G["dt_bias"], G["a_log"], G["d_skip"], G["ssd_norm_g"]) = _ssd_bwd(
        proj, states, dcat, P["conv_w"], P["conv_b"], P["dt_bias"], P["a_log"], P["d_skip"], P["ssd_norm_g"])
    tie = emit(("w_mix_out", "w_q_up", "w_kv_up", "conv_w"), G)
    S = x.shape[0]
    dproj = jnp.concatenate([dql, jnp.zeros((S, PZ - MLA_Q_RANK), F32) + tie, dz, dxbc, dkvl, ddt, dkr], axis=1).astype(BF16)
    G["w_in"] = _mm(h0b, dproj, "tn", "proj_in_dw", tm=1024, tn=PW // 2, tk=TOK_K, out_dtypes=(BF16,))
    tie = emit(("w_in",), G)
    gx, G["ln_in_g"], G["ln_in_b"] = _mm(
        dproj, P["w_in"], "nt", "proj_in_dx", tm=512, tk=PW, epi=lambda acc, d, r, g: ln_bwd_epi(acc, d, r, g)[1:], n_sums=2,
        extras=[(dr1, "mn"), (x, "mn"), (P["ln_in_g"] + tie, "n")])
    return loss, gx, G


PACK_W = 1024
BIG = (("w_in", (1024, 277), 1), ("conv_w", (4, 128), 1), ("w_q_up", (384, 96), 1), ("w_kv_up", (256, 128), 1),
       ("w_mix_out", (128, 1024), 0), ("w_mem_q", (128, 1024), 0), ("w_mem_k", (128, 1024), 0), ("w_mem_v", (128, 1024), 0),
       ("w_mem_o", (128, 1024), 0), ("w_up", (1024, 512), 1), ("w_down", (512, 1024), 0))
SMALL = ("ln_in_g", "ln_in_b", "conv_b", "ln1_g", "ln1_b", "ln2_g", "ln2_b", "ln3_g", "ln3_b",
         "ssd_norm_g", "q_norm_g", "kv_norm_g", "dt_bias", "a_log", "d_skip")
ALL_W = ("ln_in_g", "ln_in_b", "w_in", "conv_w", "conv_b", "dt_bias", "a_log", "d_skip", "ssd_norm_g", "q_norm_g", "w_q_up",
         "kv_norm_g", "w_kv_up", "w_mix_out", "ln1_g", "ln1_b", "w_mem_q", "w_mem_k", "w_mem_v", "w_mem_o", "ln2_g", "ln2_b",
         "w_up", "w_down", "ln3_g", "ln3_b")


SMALL_R = 16
LOSS_ROW = 15
_SMALL_ROWS = (("ln_in_g",), ("ln_in_b",), ("conv_b",), ("ln1_g",), ("ln1_b",), ("ln2_g",), ("ln2_b",), ("ln3_g",), ("ln3_b",),
               ("ssd_norm_g", "q_norm_g"), ("kv_norm_g", "dt_bias", "a_log", "d_skip"))
_SMALL_W = {"ssd_norm_g": 512, "q_norm_g": 384, "kv_norm_g": 256, "dt_bias": LANES, "a_log": LANES, "d_skip": LANES}


MESH = pl.DeviceIdType.MESH
ANY = pl.BlockSpec(memory_space=pl.ANY)
VM = pl.BlockSpec(memory_space=pltpu.VMEM)


def _coords():
    return lax.axis_index("x"), lax.axis_index("y"), lax.axis_index("c")


def _slot(px, py, pc):
    return 4 * px + 2 * py + pc


def _peer(k, x, y, c):
    dx, dy, dc = (k >> 2) & 1, (k >> 1) & 1, k & 1
    return (1 - x if dx else x, 1 - y if dy else y, 1 - c if dc else c)


def _adam(w, g, m, v):
    m = ADAM_B1 * m + (1.0 - ADAM_B1) * g
    v = ADAM_B2 * v + (1.0 - ADAM_B2) * (g * g)
    m_hat = m / (1.0 - ADAM_B1 ** ADAM_STEP)
    v_hat = v / (1.0 - ADAM_B2 ** ADAM_STEP)
    delta = -ADAM_LR * (m_hat / (jnp.sqrt(v_hat) + ADAM_EPS) + ADAM_WD * w)
    return delta, m, v


def _sum_slots(ref):
    tot = ref[0].astype(F32)
    for q in range(1, N_DEV):
        tot = tot + ref[q].astype(F32)
    return tot


def _row(v):
    return v.reshape(1, -1).astype(F32)


def kernel(x, mem, positions, ln_in_g, ln_in_b, w_in, conv_w, conv_b, dt_bias, a_log, d_skip, ssd_norm_g, q_norm_g, w_q_up, kv_norm_g, w_kv_up, w_mix_out, ln1_g, ln1_b, w_mem_q, w_mem_k, w_mem_v, w_mem_o, ln2_g, ln2_b, w_up, w_down, ln3_g, ln3_b, loss_target, m_ln_in_g, m_ln_in_b, m_w_in, m_conv_w, m_conv_b, m_dt_bias, m_a_log, m_d_skip, m_ssd_norm_g, m_q_norm_g, m_w_q_up, m_kv_norm_g, m_w_kv_up, m_w_mix_out, m_ln1_g, m_ln1_b, m_w_mem_q, m_w_mem_k, m_w_mem_v, m_w_mem_o, m_ln2_g, m_ln2_b, m_w_up, m_w_down, m_ln3_g, m_ln3_b, v_ln_in_g, v_ln_in_b, v_w_in, v_conv_w, v_conv_b, v_dt_bias, v_a_log, v_d_skip, v_ssd_norm_g, v_q_norm_g, v_w_q_up, v_kv_norm_g, v_w_kv_up, v_w_mix_out, v_ln1_g, v_ln1_b, v_w_mem_q, v_w_mem_k, v_w_mem_v, v_w_mem_o, v_ln2_g, v_ln2_b, v_w_up, v_w_down, v_ln3_g, v_ln3_b):
    a = dict(locals())
    W = {n: a[n] for n in ALL_W}
    M = {n: a["m_" + n] for n in ALL_W}
    V = {n: a["v_" + n] for n in ALL_W}
    return _step_overlapped(x, mem, positions, loss_target, W, M, V)


HBM = pl.BlockSpec(memory_space=pltpu.HBM)
SEM = pl.BlockSpec(memory_space=pltpu.SEMAPHORE)
EFFECT = pltpu.SideEffectType.DATAFLOW_SIDE_EFFECTING
SHARD_SHAPE = {n: s for n, s, _ in BIG}
SHARD_AXIS = {n: ax for n, _, ax in BIG}
GATHER_FIRST = ("w_in",)
GATHER_MID = ("conv_w", "w_q_up", "w_kv_up", "w_mix_out")
GATHER_LATE = ("w_mem_q", "w_mem_k", "w_mem_v", "w_mem_o", "w_up", "w_down")


def _my_slot():
    return _slot(*_coords())


def _group_copies(src_refs, land_refs, send_sems, recv_sems, slotted, landing_of_peer):
    x, y, c = _coords()
    my = _slot(x, y, c)
    cps = []
    for a, (s_ref, l_ref) in enumerate(zip(src_refs, land_refs)):
        for k in range(1, N_DEV):
            peer = _peer(k, x, y, c)
            cps.append(pltpu.make_async_remote_copy(
                src_ref=s_ref.at[_slot(*peer)] if slotted else s_ref,
                dst_ref=l_ref.at[_slot(*peer)] if landing_of_peer else l_ref.at[my],
                send_sem=send_sems.at[7 * a + k - 1], recv_sem=recv_sems.at[7 * a + k - 1],
                device_id=peer, device_id_type=MESH))
    return cps


def _send_start(srcs, lands, slotted, name):
    n = len(srcs)

    def body(*refs):
        for cp in _group_copies(refs[:n], refs[n:2 * n], refs[2 * n], refs[2 * n + 1], slotted, False):
            cp.start()
        refs[-1][...] = jnp.zeros(refs[-1].shape, F32)

    res = pl.pallas_call(
        body, name=name,
        out_shape=(pltpu.SemaphoreType.DMA((7 * n,)), pltpu.SemaphoreType.DMA((7 * n,)),
                   *[pltpu.HBM(a.shape, a.dtype) for a in srcs], *[pltpu.HBM(a.shape, a.dtype) for a in lands],
                   jax.ShapeDtypeStruct((8, LANES), F32)),
        in_specs=[HBM] * (2 * n), out_specs=(SEM, SEM, *[HBM] * (2 * n), VM),
        input_output_aliases={i: 2 + i for i in range(2 * n)},
        compiler_params=pltpu.CompilerParams(has_side_effects=EFFECT),
    )(*[pltpu.with_memory_space_constraint(a, pltpu.HBM) for a in list(srcs) + list(lands)])
    return (res[0], res[1], res[2:2 + n], res[2 + n:2 + 2 * n]), res[-1][:1, :1]


def _send_wait(started, after, slotted, name):
    send_sems, recv_sems, srcs, lands = started
    n = len(srcs)
    after = list(after) if isinstance(after, (list, tuple)) else [after]

    def body(*refs):
        for cp in _group_copies(refs[:n], refs[n:2 * n], refs[2 * n], refs[2 * n + 1], slotted, True):
            cp.wait_send()
            cp.wait_recv()

    res = pl.pallas_call(
        body, name=name, out_shape=tuple(pltpu.HBM(a.shape, a.dtype) for a in list(srcs) + list(lands)),
        in_specs=[HBM] * (2 * n) + [SEM, SEM] + [ANY] * len(after), out_specs=tuple([HBM] * (2 * n)),
        input_output_aliases={i: i for i in range(2 * n)},
        compiler_params=pltpu.CompilerParams(has_side_effects=EFFECT),
    )(*srcs, *lands, send_sems, recv_sems, *after)
    return res[n:]


def _landing(own, my):
    return lax.dynamic_update_slice(lax.empty((N_DEV,) + own.shape, own.dtype), own[None], (my,) + (0,) * own.ndim)


def _full_from_slots(name, slots):
    a, b = SHARD_SHAPE[name]
    return slots.reshape(N_DEV * a, b) if SHARD_AXIS[name] == 0 else slots.transpose(1, 0, 2).reshape(a, N_DEV * b)


def _slots_from_full(name, g):
    a, b = SHARD_SHAPE[name]
    return g.reshape(N_DEV, a, b) if SHARD_AXIS[name] == 0 else g.reshape(a, N_DEV, b).transpose(1, 0, 2)


def _reduce_adam(recv, w, m, v, name):
    _, a, b = recv.shape
    ta = a
    while ta * b * 4 * N_DEV > 4 * 1024 * 1024 and ta % 16 == 0:
        ta //= 2

    def body(r_ref, w_ref, m_ref, v_ref, g_ref, d_ref, nm_ref, nv_ref):
        g = _sum_slots(r_ref)
        g_ref[...] = g
        d_ref[...], nm_ref[...], nv_ref[...] = _adam(w_ref[...], g, m_ref[...], v_ref[...])

    row = pl.BlockSpec((ta, b), lambda i: (i, 0))
    return pl.pallas_call(
        body, name=name, grid=(a // ta,),
        in_specs=[pl.BlockSpec((N_DEV, ta, b), lambda i: (0, i, 0)), row, row, row], out_specs=[row] * 4,
        out_shape=[jax.ShapeDtypeStruct((a, b), F32)] * 4, compiler_params=_cparams(("parallel",)),
    )(recv, w, m, v)


def _step_overlapped(x, mem, positions, tgt, W, M, V):
    my = _my_slot()
    shard = {n: W[n][0] for n, _, _ in BIG}
    send = {n: (shard[n] if n == "conv_w" else shard[n].astype(BF16)) for n in shard}

    first_src, mid_src, late_src = ([send[n] for n in grp] for grp in (GATHER_FIRST, GATHER_MID, GATHER_LATE))
    first, tie = _send_start(first_src, [_landing(s, my) for s in first_src], False, "gather_first_start")
    my_then = my + tie[0, 0].astype(jnp.int32)
    mid, tie = _send_start(mid_src, [_landing(s, my_then) for s in mid_src], False, "gather_mid_start")
    my_then = my + tie[0, 0].astype(jnp.int32)
    late, tie = _send_start(late_src, [_landing(s, my_then) for s in late_src], False, "gather_late_start")

    P = {n: _row(W[n]) for n in SMALL}
    for n in ("dt_bias", "a_log", "d_skip"):
        P[n] = _head_lanes(P[n])
    P["ln_in_g"] = P["ln_in_g"] + tie

    def weights_at(stage, after):
        if stage == "first":
            lands = _send_wait(first, after, False, "gather_first_wait")
            return dict(w_in=_pad_w_in(_full_from_slots("w_in", lands[0])))
        if stage == "mid":
            lands = _send_wait(mid, after, False, "gather_mid_wait")
            full = {n: _full_from_slots(n, l) for n, l in zip(GATHER_MID, lands)}
            return dict(w_q_up=_pad_w_q(full["w_q_up"]), w_kv_up=_pad_w_kv(full["w_kv_up"]), w_mix_out=full["w_mix_out"],
                        conv_w=jnp.pad(full["conv_w"], ((0, 8 - SSD_CONV), (0, 0))))
        lands = _send_wait(late, after, False, "gather_late_wait")
        return {n: _full_from_slots(n, l) for n, l in zip(GATHER_LATE, lands)}

    started, res = [], {}

    def finish(i, after):
        names, st = started[i]
        lands = _send_wait(st, after, True, "scatter_wait_%d" % i)
        for n, recv in zip(names, lands):
            res[n] = _reduce_adam(recv, shard[n], M[n][0], V[n][0], "reduce_adam_" + n)
        return sum(res[n][0][:1, :1] for n in names) * 0.0

    def emit(names, G):
        srcs = []
        for n in names:
            g = G[n]
            if n == "w_in":
                g = _unpad_w_in(g)
            elif n == "w_q_up":
                g = _unpad_w_q(g)
            elif n == "w_kv_up":
                g = _unpad_w_kv(g)
            elif n == "conv_w":
                g = g[:SSD_CONV]
            srcs.append(g if g.ndim == 3 else _slots_from_full(n, g))
        lands = [_landing(lax.dynamic_index_in_dim(s, my, 0, keepdims=False), my) for s in srcs]
        st, tie = _send_start(srcs, lands, True, "scatter_start_%d" % len(started))
        started.append((names, st))
        if len(started) == 3:
            tie = tie + finish(0, srcs[0]) + finish(1, srcs[0])
        if len(started) == 4:
            tie = tie + finish(2, srcs[0])
        return tie

    loss, gx, G = _local_step(x[0], mem[0], positions[0], tgt[0], P, weights_at, emit)
    finish(3, gx)

    small, loss_tot = _allreduce_adam_vectors(
        {n: G[n] for n in SMALL}, loss, {n: _row(W[n]) for n in SMALL}, {n: _row(M[n]) for n in SMALL}, {n: _row(V[n]) for n in SMALL})

    outs = []
    for j in range(4):
        for n in ALL_W:
            outs.append((res[n][j] if n in res else small[j][n]).reshape(W[n].shape))
    return (loss_tot[0, 0], gx[None], *outs)


def _vector_places():
    places = {}
    for r, names in enumerate(_SMALL_ROWS):
        c = 0
        for n in names:
            w = _SMALL_W.get(n, PACK_W)
            places[n] = (r, c, w, SSD_HEADS if n in ("dt_bias", "a_log", "d_skip") else w)
            c += w
    return places


def _allreduce_adam_vectors(grads, loss, Ws, Ms, Vs):
    places = _vector_places()
    ns = len(SMALL)

    def body(*refs):
        g_in, loss_in = refs[:ns], refs[ns]
        w_in, m_in, v_in = refs[ns + 1:2 * ns + 1], refs[2 * ns + 1:3 * ns + 1], refs[3 * ns + 1:4 * ns + 1]
        o = 4 * ns + 1
        outs = [refs[o + j * ns:o + (j + 1) * ns] for j in range(4)]
        loss_out, stage, land, send_sems, recv_sems = refs[o + 4 * ns:]
        stage[...] = jnp.zeros(stage.shape, F32)
        for i, n in enumerate(SMALL):
            r, c, w, _ = places[n]
            stage[r:r + 1, c:c + w] = g_in[i][...]
        stage[LOSS_ROW:LOSS_ROW + 1, 0:LANES] = loss_in[...]
        x, y, c_ = _coords()
        my = _slot(x, y, c_)
        cps = []
        for k in range(1, N_DEV):
            peer = _peer(k, x, y, c_)
            cps.append(pltpu.make_async_remote_copy(
                src_ref=stage, dst_ref=land.at[my], send_sem=send_sems.at[k - 1], recv_sem=recv_sems.at[k - 1],
                device_id=peer, device_id_type=MESH))
        for cp in cps:
            cp.start()
        land[my] = stage[...]
        for cp in cps:
            cp.wait_recv()
        for cp in cps:
            cp.wait_send()
        tot = _sum_slots(land)
        loss_out[...] = tot[LOSS_ROW:LOSS_ROW + 1, 0:LANES]
        for i, n in enumerate(SMALL):
            r, c, _, wt = places[n]
            g = tot[r:r + 1, c:c + wt]
            outs[0][i][...] = g
            outs[1][i][...], outs[2][i][...], outs[3][i][...] = _adam(w_in[i][...], g, m_in[i][...], v_in[i][...])

    shapes = [jax.ShapeDtypeStruct((1, places[n][3]), F32) for n in SMALL]
    res = pl.pallas_call(
        body, name="allreduce_vectors", in_specs=[VM] * (4 * ns + 1), out_specs=[VM] * (4 * ns + 1),
        out_shape=shapes * 4 + [jax.ShapeDtypeStruct((1, LANES), F32)],
        scratch_shapes=[pltpu.VMEM((SMALL_R, PACK_W), F32), pltpu.VMEM((N_DEV, SMALL_R, PACK_W), F32),
                        pltpu.SemaphoreType.DMA((7,)), pltpu.SemaphoreType.DMA((7,))],
    )(*[grads[n] for n in SMALL], loss, *[Ws[n] for n in SMALL], *[Ms[n] for n in SMALL], *[Vs[n] for n in SMALL])
    return [dict(zip(SMALL, res[j * ns:(j + 1) * ns])) for j in range(4)], res[4 * ns]
```
